```python
import math
import jax
import jax.numpy as jnp
from jax import lax
import numpy as np

D_MODEL = 1024
BATCH = 8
SEQ = 2048
DEPTH = 2
DEC_BATCH = 128
DEC_SEQ = 4
PAST_LEN = 16384
PAGE_SIZE = 128

DN_HEADS = 6
DN_DK = 64
DN_DV = 64
DN_QK = DN_HEADS * DN_DK
DN_WIDTH = DN_HEADS * DN_DV
CONV_W = 4
ML_HEADS = 6
ML_DK = 64
ML_DV = 64
ML_QK = ML_HEADS * ML_DK
ML_WIDTH = ML_HEADS * ML_DV
S5_WIDTH = D_MODEL - DN_WIDTH - ML_WIDTH
S5_GROUP = 16
S5_GROUPS = S5_WIDTH // S5_GROUP
S5_P = 64
N_MEM = 256
X_HEADS = 4
X_HD = 128
X_WIDTH = X_HEADS * X_HD
D_FF = 4 * D_MODEL
CHUNK = 64
EPS = 1e-6
IN_SIZES = (DN_QK, DN_QK, DN_WIDTH, DN_HEADS, DN_HEADS, DN_WIDTH,
            ML_QK, ML_QK, ML_WIDTH, ML_HEADS, ML_HEADS, ML_WIDTH, S5_WIDTH)
P_IN = 2 * DN_QK + 2 * DN_WIDTH + 2 * DN_HEADS + 2 * ML_QK + 2 * ML_WIDTH + 2 * ML_HEADS + S5_WIDTH
CONV_CH = 2 * DN_QK + DN_WIDTH

kernel_name = 'hybrid_deltanet_mlstm_s5_decoder_step'


def _split_points():
    return [int(s) for s in np.cumsum(IN_SIZES)[:-1]]


def rmsnorm(x, g):
    xf = x.astype(jnp.float32)
    y = xf * lax.rsqrt(jnp.mean(xf * xf, axis=-1, keepdims=True) + EPS)
    return (y * g.astype(jnp.float32)).astype(x.dtype)


def _l2norm(t):
    t = t.astype(jnp.float32)
    return t * lax.rsqrt(jnp.sum(t * t, axis=-1, keepdims=True) + EPS)


def _chunk_len(L):
    return L if L <= CHUNK else math.gcd(L, CHUNK)


def _to_blocks(t, c):
    b, l, h = t.shape[:3]
    t = t.astype(jnp.float32).reshape((b, l // c, c, h) + t.shape[3:])
    return jnp.moveaxis(t, (1, 3), (0, 2))


def _from_blocks(t):
    t = jnp.moveaxis(t, (0, 2), (1, 3))
    return t.reshape((t.shape[0], t.shape[1] * t.shape[2]) + t.shape[3:])


def causal_conv_silu(u, buf, w):
    L = u.shape[1]
    ext = jnp.concatenate([buf.astype(u.dtype), u], axis=1)
    out = ext[:, 0:L] * w[0]
    for j in range(1, CONV_W):
        out = out + ext[:, j:j + L] * w[j]
    return jax.nn.silu(out), ext[:, L:]


def gated_delta_rule(q, k, v, g, beta, s0):
    dk = q.shape[-1]
    dv = v.shape[-1]
    c = _chunk_len(q.shape[1])
    q = _to_blocks(q, c) * (dk ** -0.5)
    k = _to_blocks(k, c)
    v = _to_blocks(v, c)
    g = _to_blocks(g, c)
    beta = _to_blocks(beta, c)
    G = jnp.cumsum(g, axis=-1)
    incl = jnp.tril(jnp.ones((c, c), dtype=bool))
    strict = jnp.tril(jnp.ones((c, c), dtype=bool), -1)
    decay = jnp.exp(jnp.where(incl, G[..., :, None] - G[..., None, :], -jnp.inf))
    kb = k * beta[..., None]
    m_low = jnp.where(strict, jnp.einsum('nbhik,nbhjk->nbhij', kb, k) * decay, 0.0)
    rhs = jnp.concatenate([v * beta[..., None], kb * jnp.exp(G)[..., None]], axis=-1)
    sol = lax.linalg.triangular_solve(m_low + jnp.eye(c, dtype=jnp.float32), rhs,
                                      left_side=True, lower=True, unit_diagonal=True)
    u_blk, w_blk = sol[..., :dv], sol[..., dv:]
    a_in = jnp.where(incl, jnp.einsum('nbhik,nbhjk->nbhij', q, k) * decay, 0.0)

    def step(S, xs):
        qc, kc, uc, wc, Gc, ac = xs
        v_new = uc - jnp.einsum('bhck,bhkv->bhcv', wc, S)
        o = (jnp.einsum('bhck,bhkv->bhcv', qc * jnp.exp(Gc)[..., None], S)
             + jnp.einsum('bhij,bhjv->bhiv', ac, v_new))
        g_last = Gc[..., -1]
        S = (S * jnp.exp(g_last)[..., None, None]
             + jnp.einsum('bhck,bhcv->bhkv', kc * jnp.exp(g_last[..., None] - Gc)[..., None], v_new))
        return S, o

    S, o = lax.scan(step, s0.astype(jnp.float32), (q, k, u_blk, w_blk, G, a_in))
    return _from_blocks(o), S


def mlstm_chunkwise(q, k, v, i_pre, f_pre, c0, n0, m0):
    dk = q.shape[-1]
    c = _chunk_len(q.shape[1])
    q = _to_blocks(q, c)
    k = _to_blocks(k, c) * (dk ** -0.5)
    v = _to_blocks(v, c)
    ig = _to_blocks(i_pre, c)
    b = jnp.cumsum(jax.nn.log_sigmoid(_to_blocks(f_pre, c)), axis=-1)
    incl = jnp.tril(jnp.ones((c, c), dtype=bool))
    dlog = jnp.where(incl, b[..., :, None] - b[..., None, :] + ig[..., None, :], -jnp.inf)

    def step(carry, xs):
        cs, ns, ms = carry
        qc, kc, vc, bc, ic, dc = xs
        inter = bc + ms[..., None]
        mt = jnp.maximum(inter, jnp.max(dc, axis=-1))
        w_inter = jnp.exp(inter - mt)
        s = jnp.einsum('bhik,bhjk->bhij', qc, kc) * jnp.exp(dc - mt[..., None])
        num = (w_inter[..., None] * jnp.einsum('bhck,bhkv->bhcv', qc, cs)
               + jnp.einsum('bhij,bhjv->bhiv', s, vc))
        den = w_inter * jnp.einsum('bhck,bhk->bhc', qc, ns) + jnp.sum(s, axis=-1)
        h = num / jnp.maximum(jnp.abs(den), jnp.exp(-mt))[..., None]
        m_end = mt[..., -1]
        a_end = jnp.exp(bc[..., -1] + ms - m_end)
        wk = jnp.exp(bc[..., -1:] - bc + ic - m_end[..., None])
        cs = a_end[..., None, None] * cs + jnp.einsum('bhck,bhcv->bhkv', kc * wk[..., None], vc)
        ns = a_end[..., None] * ns + jnp.einsum('bhck,bhc->bhk', kc, wk)
        return (cs, ns, m_end), h

    f32 = jnp.float32
    (cs, ns, ms), h = lax.scan(step, (c0.astype(f32), n0.astype(f32), m0.astype(f32)),
                               (q, k, v, b, ig, dlog))
    return _from_blocks(h), cs, ns, ms


def s5_ssm(u, h0_re, h0_im, lam_re, lam_im, log_dt, b_re, b_im, c_re, c_im, d):
    f32 = jnp.float32
    bsz, L, _ = u.shape
    ug = u.astype(f32).reshape(bsz, L, S5_GROUPS, S5_GROUP)
    lr, li = lam_re.astype(f32), lam_im.astype(f32)
    dt = jnp.exp(log_dt.astype(f32))[:, None]
    mag = jnp.exp(lr * dt)
    ar, ai = mag * jnp.cos(li * dt), mag * jnp.sin(li * dt)
    inv = 1.0 / (lr * lr + li * li)
    cr = ((ar - 1.0) * lr + ai * li) * inv
    ci = (ai * lr - (ar - 1.0) * li) * inv
    bur = jnp.einsum('blgc,gpc->blgp', ug, b_re.astype(f32))
    bui = jnp.einsum('blgc,gpc->blgp', ug, b_im.astype(f32))
    xr = cr * bur - ci * bui
    xi = cr * bui + ci * bur
    h0r, h0i = h0_re.astype(f32), h0_im.astype(f32)
    xr = xr.at[:, 0].add(ar * h0r - ai * h0i)
    xi = xi.at[:, 0].add(ar * h0i + ai * h0r)

    def combine(e1, e2):
        a1r, a1i, b1r, b1i = e1
        a2r, a2i, b2r, b2i = e2
        return (a2r * a1r - a2i * a1i, a2r * a1i + a2i * a1r,
                a2r * b1r - a2i * b1i + b2r, a2r * b1i + a2i * b1r + b2i)

    _, _, hr, hi = lax.associative_scan(
        combine, (jnp.broadcast_to(ar, xr.shape), jnp.broadcast_to(ai, xr.shape), xr, xi), axis=1)
    y = (jnp.einsum('blgp,gcp->blgc', hr, c_re.astype(f32))
         - jnp.einsum('blgp,gcp->blgc', hi, c_im.astype(f32))
         + d.astype(f32) * ug)
    return y.reshape(bsz, L, S5_WIDTH), hr[:, -1], hi[:, -1]


def cross_attn(h, mem_k, mem_v, wq, wo):
    b, L, _ = h.shape
    q = (h @ wq).reshape(b, L, X_HEADS, X_HD)
    s = jnp.einsum('blhd,bmhd->bhlm', q, mem_k).astype(jnp.float32) * (X_HD ** -0.5)
    p = jax.nn.softmax(s, axis=-1).astype(h.dtype)
    o = jnp.einsum('bhlm,bmhd->blhd', p, mem_v).reshape(b, L, X_WIDTH)
    return o @ wo


def decoder_layer(x, mem_k, mem_v, conv_buf, dn_s, ml_c, ml_n, ml_m, ssm_re, ssm_im, p):
    f32 = jnp.float32
    b, L, _ = x.shape
    h = rmsnorm(x, p['norm_mix'])
    z = h @ p['w_in']
    (d_q, d_k, d_v, d_beta, d_alpha, d_gate,
     l_q, l_k, l_v, l_i, l_f, l_o, s_u) = jnp.split(z, _split_points(), axis=-1)

    qkv, new_conv = causal_conv_silu(jnp.concatenate([d_q, d_k, d_v], axis=-1), conv_buf, p['dn_conv_w'])
    cq, ck, cv = jnp.split(qkv, [DN_QK, 2 * DN_QK], axis=-1)
    q = _l2norm(cq.reshape(b, L, DN_HEADS, DN_DK))
    k = _l2norm(ck.reshape(b, L, DN_HEADS, DN_DK))
    v = cv.reshape(b, L, DN_HEADS, DN_DV)
    beta = jax.nn.sigmoid(d_beta.astype(f32))
    g = -jnp.exp(p['dn_a_log'].astype(f32)) * jax.nn.softplus(d_alpha.astype(f32) + p['dn_dt_bias'].astype(f32))
    o, new_s = gated_delta_rule(q, k, v, g, beta, dn_s)
    o = rmsnorm(o, p['dn_norm']) * jax.nn.silu(d_gate.astype(f32).reshape(b, L, DN_HEADS, DN_DV))
    dn_out = o.reshape(b, L, DN_WIDTH)

    hm, new_c, new_n, new_m = mlstm_chunkwise(
        l_q.reshape(b, L, ML_HEADS, ML_DK), l_k.reshape(b, L, ML_HEADS, ML_DK),
        l_v.reshape(b, L, ML_HEADS, ML_DV),
        l_i.astype(f32) + p['ml_i_bias'].astype(f32), l_f.astype(f32) + p['ml_f_bias'].astype(f32),
        ml_c, ml_n, ml_m)
    hm = rmsnorm(hm, p['ml_norm']) * jax.nn.sigmoid(l_o.astype(f32).reshape(b, L, ML_HEADS, ML_DV))
    ml_out = hm.reshape(b, L, ML_WIDTH)

    ys, new_re, new_im = s5_ssm(s_u, ssm_re, ssm_im, p['s5_lam_re'], p['s5_lam_im'], p['s5_log_dt'],
                                p['s5_b_re'], p['s5_b_im'], p['s5_c_re'], p['s5_c_im'], p['s5_d'])
    zg = jax.nn.gelu(ys)
    s5_out = zg * jax.nn.sigmoid(zg @ p['s5_w_glu'].astype(f32) + p['s5_b_glu'].astype(f32))

    mix = jnp.concatenate([dn_out, ml_out, s5_out], axis=-1).astype(x.dtype)
    x = x + mix @ p['w_out']
    x = x + cross_attn(rmsnorm(x, p['norm_x']), mem_k, mem_v, p['w_xq'], p['w_xo'])
    hf = rmsnorm(x, p['norm_ff']) @ p['w_ff1']
    x = x + jnp.square(jax.nn.relu(hf)) @ p['w_ff2']
    sd = x.dtype
    return x, (new_conv.astype(sd), new_s.astype(sd), new_c.astype(sd), new_n.astype(sd),
               new_m.astype(sd), new_re.astype(sd), new_im.astype(sd))


def setup_inputs(seed: int = 0) -> dict:
    key = jax.random.key(seed)
    ks = iter(jax.random.split(key, 64))
    f32 = jnp.float32

    def nrm(shape, scale=1.0):
        return jax.random.normal(next(ks), shape, f32) * scale

    def unif(shape, lo, hi):
        return jax.random.uniform(next(ks), shape, f32, lo, hi)

    def gain(shape):
        return 1.0 + nrm(shape, 0.02)

    Dp = DEPTH
    dt0 = jnp.exp(unif((Dp, DN_HEADS), math.log(1e-3), math.log(1e-1)))
    inp = {}
    inp['x_prompt'] = nrm((BATCH, SEQ, D_MODEL))
    inp['x_sample'] = nrm((DEC_BATCH, DEC_SEQ, D_MODEL))
    inp['mem_prompt'] = nrm((BATCH, N_MEM, D_MODEL))
    inp['state_dn_conv'] = nrm((Dp, DEC_BATCH, CONV_W - 1, CONV_CH))
    inp['state_dn_s'] = nrm((Dp, DEC_BATCH, DN_HEADS, DN_DK, DN_DV), 0.3)
    inp['state_ml_c'] = nrm((Dp, DEC_BATCH, ML_HEADS, ML_DK, ML_DV), 0.3)
    inp['state_ml_n'] = nrm((Dp, DEC_BATCH, ML_HEADS, ML_DK), 0.3)
    inp['state_ml_m'] = nrm((Dp, DEC_BATCH, ML_HEADS))
    inp['state_ssm_re'] = nrm((Dp, DEC_BATCH, S5_GROUPS, S5_P), 0.5)
    inp['state_ssm_im'] = nrm((Dp, DEC_BATCH, S5_GROUPS, S5_P), 0.5)
    inp['cache_mem_k'] = nrm((Dp, DEC_BATCH, N_MEM, X_HEADS, X_HD))
    inp['cache_mem_v'] = nrm((Dp, DEC_BATCH, N_MEM, X_HEADS, X_HD))
    inp['norm_mix'] = gain((Dp, D_MODEL))
    inp['w_in'] = nrm((Dp, D_MODEL, P_IN), D_MODEL ** -0.5)
    inp['dn_conv_w'] = nrm((Dp, CONV_W, CONV_CH), 0.5)
    inp['dn_a_log'] = jnp.log(unif((Dp, DN_HEADS), 1.0, 16.0))
    inp['dn_dt_bias'] = dt0 + jnp.log(-jnp.expm1(-dt0))
    inp['dn_norm'] = gain((Dp, DN_DV))
    inp['ml_i_bias'] = nrm((Dp, ML_HEADS), 0.1)
    inp['ml_f_bias'] = unif((Dp, ML_HEADS), 3.0, 6.0)
    inp['ml_norm'] = gain((Dp, ML_DV))
    inp['s5_lam_re'] = -0.5 + nrm((Dp, S5_GROUPS, S5_P), 0.01)
    inp['s5_lam_im'] = (math.pi * jnp.arange(S5_P, dtype=f32))[None, None, :] + nrm((Dp, S5_GROUPS, S5_P), 0.01)
    inp['s5_log_dt'] = unif((Dp, S5_GROUPS), math.log(1e-3), math.log(1e-1))
    inp['s5_b_re'] = nrm((Dp, S5_GROUPS, S5_P, S5_GROUP), (2 * S5_GROUP) ** -0.5)
    inp['s5_b_im'] = nrm((Dp, S5_GROUPS, S5_P, S5_GROUP), (2 * S5_GROUP) ** -0.5)
    inp['s5_c_re'] = nrm((Dp, S5_GROUPS, S5_GROUP, S5_P), S5_P ** -0.5)
    inp['s5_c_im'] = nrm((Dp, S5_GROUPS, S5_GROUP, S5_P), S5_P ** -0.5)
    inp['s5_d'] = nrm((Dp, S5_GROUPS, S5_GROUP), 0.5)
    inp['s5_w_glu'] = nrm((Dp, S5_WIDTH, S5_WIDTH), S5_WIDTH ** -0.5)
    inp['s5_b_glu'] = nrm((Dp, S5_WIDTH), 0.02)
    inp['w_out'] = nrm((Dp, D_MODEL, D_MODEL), D_MODEL ** -0.5)
    inp['norm_x'] = gain((Dp, D_MODEL))
    inp['norm_mem'] = gain((Dp, D_MODEL))
    inp['w_xq'] = nrm((Dp, D_MODEL, X_WIDTH), D_MODEL ** -0.5)
    inp['w_xk'] = nrm((Dp, D_MODEL, X_WIDTH), D_MODEL ** -0.5)
    inp['w_xv'] = nrm((Dp, D_MODEL, X_WIDTH), D_MODEL ** -0.5)
    inp['w_xo'] = nrm((Dp, X_WIDTH, D_MODEL), X_WIDTH ** -0.5)
    inp['norm_ff'] = gain((Dp, D_MODEL))
    inp['w_ff1'] = nrm((Dp, D_MODEL, D_FF), D_MODEL ** -0.5)
    inp['w_ff2'] = nrm((Dp, D_FF, D_MODEL), D_FF ** -0.5)
    inp['norm_final'] = gain((D_MODEL,))
    return inp


def reference(x_prompt, x_sample, mem_prompt, state_dn_conv, state_dn_s, state_ml_c, state_ml_n,
              state_ml_m, state_ssm_re, state_ssm_im, cache_mem_k, cache_mem_v,
              norm_mix, w_in, dn_conv_w, dn_a_log, dn_dt_bias, dn_norm, ml_i_bias, ml_f_bias, ml_norm,
              s5_lam_re, s5_lam_im, s5_log_dt, s5_b_re, s5_b_im, s5_c_re, s5_c_im, s5_d, s5_w_glu, s5_b_glu,
              w_out, norm_x, norm_mem, w_xq, w_xk, w_xv, w_xo, norm_ff, w_ff1, w_ff2, norm_final):
    f32 = jnp.float32
    layers = [dict(norm_mix=norm_mix[l], w_in=w_in[l], dn_conv_w=dn_conv_w[l], dn_a_log=dn_a_log[l],
                   dn_dt_bias=dn_dt_bias[l], dn_norm=dn_norm[l], ml_i_bias=ml_i_bias[l],
                   ml_f_bias=ml_f_bias[l], ml_norm=ml_norm[l], s5_lam_re=s5_lam_re[l],
                   s5_lam_im=s5_lam_im[l], s5_log_dt=s5_log_dt[l], s5_b_re=s5_b_re[l], s5_b_im=s5_b_im[l],
                   s5_c_re=s5_c_re[l], s5_c_im=s5_c_im[l], s5_d=s5_d[l], s5_w_glu=s5_w_glu[l],
                   s5_b_glu=s5_b_glu[l], w_out=w_out[l], norm_x=norm_x[l], w_xq=w_xq[l], w_xo=w_xo[l],
                   norm_ff=norm_ff[l], w_ff1=w_ff1[l], w_ff2=w_ff2[l])
              for l in range(DEPTH)]

    bp = x_prompt.shape[0]
    sd = x_prompt.dtype
    z_conv = jnp.zeros((bp, CONV_W - 1, CONV_CH), sd)
    z_dn = jnp.zeros((bp, DN_HEADS, DN_DK, DN_DV), f32)
    z_mc = jnp.zeros((bp, ML_HEADS, ML_DK, ML_DV), f32)
    z_mn = jnp.zeros((bp, ML_HEADS, ML_DK), f32)
    z_mm = jnp.zeros((bp, ML_HEADS), f32)
    z_s = jnp.zeros((bp, S5_GROUPS, S5_P), f32)
    xp = x_prompt
    p_st = []
    for l in range(DEPTH):
        mn = rmsnorm(mem_prompt, norm_mem[l])
        mk = (mn @ w_xk[l]).reshape(bp, N_MEM, X_HEADS, X_HD)
        mv = (mn @ w_xv[l]).reshape(bp, N_MEM, X_HEADS, X_HD)
        xp, st = decoder_layer(xp, mk, mv, z_conv, z_dn, z_mc, z_mn, z_mm, z_s, z_s, layers[l])
        p_st.append(st + (mk, mv))
    y_prompt = rmsnorm(xp, norm_final)

    xs = x_sample
    s_st = []
    for l in range(DEPTH):
        xs, st = decoder_layer(xs, cache_mem_k[l], cache_mem_v[l], state_dn_conv[l], state_dn_s[l],
                               state_ml_c[l], state_ml_n[l], state_ml_m[l], state_ssm_re[l],
                               state_ssm_im[l], layers[l])
        s_st.append(st)
    y_sample = rmsnorm(xs, norm_final)

    p_dn_conv = jnp.stack([s[0] for s in p_st])
    p_dn_s = jnp.stack([s[1] for s in p_st])
    p_ml_c = jnp.stack([s[2] for s in p_st])
    p_ml_n = jnp.stack([s[3] for s in p_st])
    p_ml_m = jnp.stack([s[4] for s in p_st])
    p_ssm_re = jnp.stack([s[5] for s in p_st])
    p_ssm_im = jnp.stack([s[6] for s in p_st])
    p_mem_k = jnp.stack([s[7] for s in p_st])
    p_mem_v = jnp.stack([s[8] for s in p_st])
    s_dn_conv = jnp.stack([s[0] for s in s_st])
    s_dn_s = jnp.stack([s[1] for s in s_st])
    s_ml_c = jnp.stack([s[2] for s in s_st])
    s_ml_n = jnp.stack([s[3] for s in s_st])
    s_ml_m = jnp.stack([s[4] for s in s_st])
    s_ssm_re = jnp.stack([s[5] for s in s_st])
    s_ssm_im = jnp.stack([s[6] for s in s_st])
    return (y_prompt, y_sample, p_dn_conv, p_dn_s, p_ml_c, p_ml_n, p_ml_m, p_ssm_re, p_ssm_im,
            p_mem_k, p_mem_v, s_dn_conv, s_dn_s, s_ml_c, s_ml_n, s_ml_m, s_ssm_re, s_ssm_im)
```

```python
import functools
import math

import numpy as np
import jax
import jax.numpy as jnp
from jax import lax
from jax.experimental import pallas as pl
from jax.experimental.pallas import tpu as pltpu

F32 = jnp.float32
BF16 = jnp.bfloat16

D_MODEL = 1024
DEPTH = 2
N_HEADS = 6
HEAD_DIM = 64
N_PAIRS = N_HEADS // 2
HEADS_W = N_HEADS * HEAD_DIM
CONV_W = 4
CONV_CH = 3 * HEADS_W
S5_WIDTH = 256
S5_GROUP = 16
S5_GROUPS = 16
S5_P = 64
S5_STATE = S5_GROUPS * S5_P
N_MEM = 256
X_HEADS = 4
X_HD = 128
X_WIDTH = X_HEADS * X_HD
D_FF = 4 * D_MODEL
EPS = 1e-6
IN_SIZES = (384, 384, 384, 6, 6, 384, 384, 384, 384, 6, 6, 384, 256)

LANES = 128
CHUNK = 64
SUB = 8
NEG = -1e30
Z_COLS = 3456
VMEM_LIMIT = 52 * 1024 * 1024

_NN = (((1,), (0,)), ((), ()))
_NT = (((1,), (1,)), ((), ()))


def _dot(a, b, dims=_NN):
    return lax.dot_general(a, b, dims, preferred_element_type=F32)


def _split(a, n):
    out = []
    r = a
    for i in range(n):
        h = r.astype(BF16)
        out.append(h)
        if i + 1 < n:
            r = r - h.astype(F32)
    return out


def _mm(a, b, dims=_NN, prec=1):
    if prec == 1:
        return _dot(a.astype(BF16), b.astype(BF16), dims)
    a1, a2 = _split(a, 2)
    b1, b2 = _split(b, 2)
    return (_dot(a1, b2, dims) + _dot(a2, b1, dims)) + _dot(a1, b1, dims)


def _mm_sel_r(a, sel, n=3):
    parts = _split(a, n)
    acc = _dot(parts[-1], sel)
    for p_ in parts[-2::-1]:
        acc = acc + _dot(p_, sel)
    return acc


def _mm_sel_l(sel, a, n=3):
    parts = _split(a, n)
    acc = _dot(sel, parts[-1])
    for p_ in parts[-2::-1]:
        acc = acc + _dot(sel, p_)
    return acc


def _sigmoid(x):
    return 1.0 / (1.0 + jnp.exp(-x))


def _softplus(x):
    return jnp.maximum(x, 0.0) + jnp.log(1.0 + jnp.exp(-jnp.abs(x)))


def _rmsnorm_rows(x, g):
    return x * lax.rsqrt(jnp.mean(x * x, axis=-1, keepdims=True) + EPS) * g


def _iota(shape, dim):
    return lax.broadcasted_iota(jnp.int32, shape, dim)


def _idiv(x, n):
    return x >> (n.bit_length() - 1)


def _imod(x, n):
    return x & (n - 1)


def _lane_lo(shape):
    return _iota(shape, 1) < HEAD_DIM


def _x2(x):
    lo = _lane_lo(x.shape)
    return jnp.concatenate([jnp.where(lo, x, 0.0), jnp.where(lo, 0.0, x)], axis=0)


def _fold(y):
    r = y.shape[0] // 2
    return y[:r] + y[r:]


def _expand(a, c0, c1):
    r = a.shape[0]
    return jnp.where(_lane_lo((r, LANES)), a[:, c0:c0 + 1], a[:, c1:c1 + 1])


def _block_ones():
    i = _iota((LANES, LANES), 0)
    j = _iota((LANES, LANES), 1)
    return jnp.where((i < HEAD_DIM) == (j < HEAD_DIM), 1.0, 0.0).astype(BF16)


def _seg_mean(x, bones):
    return _mm_sel_r(x, bones, n=2) * (1.0 / HEAD_DIM)


def _ltri(sub):
    i = _iota((CHUNK, CHUNK), 0)
    j = _iota((CHUNK, CHUNK), 1)
    ok = (j <= i) & (_idiv(i, sub) == _idiv(j, sub)) if sub != CHUNK else (j <= i)
    return jnp.where(ok, 1.0, 0.0).astype(BF16)


def _upair(sub):
    a = _iota((LANES, LANES), 0)
    b = _iota((LANES, LANES), 1)
    ok = ((a < HEAD_DIM) == (b < HEAD_DIM)) & (a <= b)
    if sub != CHUNK:
        ok = ok & (_idiv(_imod(a, HEAD_DIM), sub) == _idiv(_imod(b, HEAD_DIM), sub))
    return jnp.where(ok, 1.0, 0.0).astype(BF16)


def _pair_masks(sub):
    i = _iota((CHUNK, LANES), 0)
    j = _imod(_iota((CHUNK, LANES), 1), HEAD_DIM)
    same = (_idiv(i, sub) == _idiv(j, sub)) if sub != CHUNK else (i >= 0)
    return same & (j <= i), same & (j < i)


def _valid_col(sub, n_valid):
    i = _iota((CHUNK, 1), 0)
    return _imod(i, sub) < n_valid


def _valid_row(sub, n_valid):
    j = _iota((1, LANES), 1)
    return _imod(j, sub) < n_valid


def _bd_mask():
    i = _iota((LANES, LANES), 0)
    j = _iota((LANES, LANES), 1)
    return (i < HEAD_DIM) == (j < HEAD_DIM)


def _neumann_inverse(m, n_sq, prec):
    n = m.shape[0]
    eye = jnp.where(_iota((n, n), 0) == _iota((n, n), 1), 1.0, 0.0)
    t = eye - m
    p_ = m
    for _ in range(n_sq):
        p_ = _mm(p_, p_, prec=prec)
        t = t + _mm(t, p_, prec=prec)
    return t


def _last_row_bcast(a, sub):
    if sub == CHUNK:
        return jnp.broadcast_to(a[CHUNK - 1:CHUNK, :], a.shape)
    parts = [jnp.broadcast_to(a[s * sub + sub - 1:s * sub + sub, :], (sub, a.shape[1]))
             for s in range(CHUNK // sub)]
    return jnp.concatenate(parts, axis=0)


def _in_proj_kernel(x_ref, g_ref, w_ref, o_ref):
    xn = _rmsnorm_rows(x_ref[...], g_ref[...]).astype(BF16)
    o_ref[...] = _dot(xn, w_ref[...])


def _norm_matmul(x, g, w, tm):
    t, k = x.shape
    n = w.shape[1]
    return pl.pallas_call(
        _in_proj_kernel,
        out_shape=jax.ShapeDtypeStruct((t, n), F32),
        grid=(t // tm,),
        in_specs=[pl.BlockSpec((tm, k), lambda i: (i, 0)),
                  pl.BlockSpec((1, k), lambda i: (0, 0)),
                  pl.BlockSpec((k, n), lambda i: (0, 0))],
        out_specs=pl.BlockSpec((tm, n), lambda i: (i, 0)),
        compiler_params=pltpu.CompilerParams(dimension_semantics=("parallel",),
                                             vmem_limit_bytes=VMEM_LIMIT),
        name="norm_matmul",
    )(x, g, w)


def _out_proj_kernel(dn_ref, ml_ref, s5_ref, x_ref, w1_ref, w2_ref, w3_ref, g_ref, wq_ref,
                     x1_ref, q_ref):
    acc = x_ref[...] + _dot(dn_ref[...], w1_ref[...])
    acc = acc + _dot(ml_ref[...], w2_ref[...])
    acc = acc + _dot(s5_ref[...], w3_ref[...])
    x1_ref[...] = acc
    xn = _rmsnorm_rows(acc, g_ref[...]).astype(BF16)
    q_ref[...] = _dot(xn, wq_ref[...]).astype(BF16)


def _out_proj(dn, ml, s5, x, w1, w2, w3, g, wq, tm):
    t = x.shape[0]
    row = lambda i: (i, 0)
    fix = lambda i: (0, 0)
    return pl.pallas_call(
        _out_proj_kernel,
        out_shape=(jax.ShapeDtypeStruct((t, D_MODEL), F32), jax.ShapeDtypeStruct((t, X_WIDTH), BF16)),
        grid=(t // tm,),
        in_specs=[pl.BlockSpec((tm, HEADS_W), row), pl.BlockSpec((tm, HEADS_W), row),
                  pl.BlockSpec((tm, S5_WIDTH), row), pl.BlockSpec((tm, D_MODEL), row),
                  pl.BlockSpec((HEADS_W, D_MODEL), fix), pl.BlockSpec((HEADS_W, D_MODEL), fix),
                  pl.BlockSpec((S5_WIDTH, D_MODEL), fix), pl.BlockSpec((1, D_MODEL), fix),
                  pl.BlockSpec((D_MODEL, X_WIDTH), fix)],
        out_specs=(pl.BlockSpec((tm, D_MODEL), row), pl.BlockSpec((tm, X_WIDTH), row)),
        compiler_params=pltpu.CompilerParams(dimension_semantics=("parallel",),
                                             vmem_limit_bytes=VMEM_LIMIT),
        name="out_proj",
    )(dn, ml, s5, x, w1, w2, w3, g, wq)


def _ffn_kernel(x_ref, o_ref, wo_ref, g_ref, w1_ref, w2_ref, gf_ref, y_ref, *, final_norm, ff_chunk):
    x2 = x_ref[...] + _dot(o_ref[...], wo_ref[...])
    xn = _rmsnorm_rows(x2, g_ref[...]).astype(BF16)
    acc = x2
    for c in range(D_FF // ff_chunk):
        hf = _dot(xn, w1_ref[:, c * ff_chunk:(c + 1) * ff_chunk])
        a = jnp.square(jnp.maximum(hf, 0.0)).astype(BF16)
        acc = acc + _dot(a, w2_ref[c * ff_chunk:(c + 1) * ff_chunk, :])
    if final_norm:
        acc = _rmsnorm_rows(acc, gf_ref[...])
    y_ref[...] = acc


def _ffn(x, o, wo, g, w1, w2, gf, tm, final_norm):
    t = x.shape[0]
    row = lambda i: (i, 0)
    fix = lambda i: (0, 0)
    return pl.pallas_call(
        functools.partial(_ffn_kernel, final_norm=final_norm, ff_chunk=1024),
        out_shape=jax.ShapeDtypeStruct((t, D_MODEL), F32),
        grid=(t // tm,),
        in_specs=[pl.BlockSpec((tm, D_MODEL), row), pl.BlockSpec((tm, X_WIDTH), row),
                  pl.BlockSpec((X_WIDTH, D_MODEL), fix), pl.BlockSpec((1, D_MODEL), fix),
                  pl.BlockSpec((D_MODEL, D_FF), fix), pl.BlockSpec((D_FF, D_MODEL), fix),
                  pl.BlockSpec((1, D_MODEL), fix)],
        out_specs=pl.BlockSpec((tm, D_MODEL), row),
        compiler_params=pltpu.CompilerParams(dimension_semantics=("parallel",),
                                             vmem_limit_bytes=VMEM_LIMIT),
        name="ffn",
    )(x, o, wo, g, w1, w2, gf)


def _attn_heads(q, k_ref, v_ref, bi):
    outs = []
    for h in range(X_HEADS):
        sl = slice(h * X_HD, (h + 1) * X_HD)
        kh = k_ref[bi, :, sl].astype(BF16)
        vh = v_ref[bi, :, sl].astype(BF16)
        s = _dot(q[:, sl], kh, _NT) * (X_HD ** -0.5)
        e = jnp.exp(s - jnp.max(s, axis=-1, keepdims=True))
        p = e / jnp.sum(e, axis=-1, keepdims=True)
        outs.append(_dot(p.astype(BF16), vh))
    return jnp.concatenate(outs, axis=-1)


def _attn_seq_kernel(q_ref, k_ref, v_ref, o_ref):
    o_ref[...] = _attn_heads(q_ref[...], k_ref, v_ref, 0).astype(BF16)


def _attn_seq(q, k, v, batch, tq):
    t = q.shape[0]
    nt = t // batch // tq
    return pl.pallas_call(
        _attn_seq_kernel,
        out_shape=jax.ShapeDtypeStruct((t, X_WIDTH), BF16),
        grid=(batch, nt),
        in_specs=[pl.BlockSpec((tq, X_WIDTH), lambda b, i: (b * nt + i, 0)),
                  pl.BlockSpec((1, N_MEM, X_WIDTH), lambda b, i: (b, 0, 0)),
                  pl.BlockSpec((1, N_MEM, X_WIDTH), lambda b, i: (b, 0, 0))],
        out_specs=pl.BlockSpec((tq, X_WIDTH), lambda b, i: (b * nt + i, 0)),
        compiler_params=pltpu.CompilerParams(dimension_semantics=("parallel", "parallel"),
                                             vmem_limit_bytes=VMEM_LIMIT),
        name="attn_seq",
    )(q, k, v)


def _attn_batch_kernel(q_ref, k_ref, v_ref, o_ref, *, bb):
    for bi in range(bb):
        o_ref[bi] = _attn_heads(q_ref[bi], k_ref, v_ref, bi).astype(BF16)


def _attn_batch(q, k, v, bb):
    b, l, _ = q.shape
    blk = lambda i: (i, 0, 0)
    return pl.pallas_call(
        functools.partial(_attn_batch_kernel, bb=bb),
        out_shape=jax.ShapeDtypeStruct((b, l, X_WIDTH), BF16),
        grid=(b // bb,),
        in_specs=[pl.BlockSpec((bb, l, X_WIDTH), blk), pl.BlockSpec((bb, N_MEM, X_WIDTH), blk),
                  pl.BlockSpec((bb, N_MEM, X_WIDTH), blk)],
        out_specs=pl.BlockSpec((bb, l, X_WIDTH), blk),
        compiler_params=pltpu.CompilerParams(dimension_semantics=("parallel",),
                                             vmem_limit_bytes=VMEM_LIMIT),
        name="attn_batch",
    )(q, k, v)


def _s5_kernel(u_ref, h0r_ref, h0i_ref, lre_ref, lim_ref, ldt_ref, bre_ref, bim_ref, cre_ref, cim_ref,
               d_ref, wg_ref, bg_ref, y_ref, hr_out, hi_out,
               xr_s, xi_s, st_r, st_i, coef_s, *, tb, nb, lane_chunk):
    step = pl.program_id(0)

    @pl.when(step == 0)
    def _():
        lr = lre_ref[...]
        li = lim_ref[...]
        dt = jnp.exp(ldt_ref[...])
        mag = jnp.exp(lr * dt)
        ar = mag * jnp.cos(li * dt)
        ai = mag * jnp.sin(li * dt)
        inv = 1.0 / (lr * lr + li * li)
        coef_s[0:1, :] = ar
        coef_s[1:2, :] = ai
        coef_s[2:3, :] = ((ar - 1.0) * lr + ai * li) * inv
        coef_s[3:4, :] = (ai * lr - (ar - 1.0) * li) * inv
        st_r[...] = h0r_ref[...]
        st_i[...] = h0i_ref[...]

    u = u_ref[...]
    ub = u.astype(BF16)
    bur = _dot(ub, bre_ref[...])
    bui = _dot(ub, bim_ref[...])
    cr = coef_s[2:3, :]
    ci = coef_s[3:4, :]
    xr_s[...] = cr * bur - ci * bui
    xi_s[...] = cr * bui + ci * bur

    for lc in range(S5_STATE // lane_chunk):
        sl = slice(lc * lane_chunk, (lc + 1) * lane_chunk)
        ar = jnp.broadcast_to(coef_s[0:1, sl], (nb, lane_chunk))
        ai = jnp.broadcast_to(coef_s[1:2, sl], (nb, lane_chunk))

        def body(t, carry, sl=sl, ar=ar, ai=ai):
            hr, hi = carry
            r0 = pl.multiple_of(t * nb, 8)
            nhr = ar * hr - ai * hi + xr_s[pl.ds(r0, nb), sl]
            nhi = ar * hi + ai * hr + xi_s[pl.ds(r0, nb), sl]
            xr_s[pl.ds(r0, nb), sl] = nhr
            xi_s[pl.ds(r0, nb), sl] = nhi
            return nhr, nhi

        hr, hi = lax.fori_loop(0, tb, body, (st_r[:, sl], st_i[:, sl]), unroll=min(tb, 8))
        st_r[:, sl] = hr
        st_i[:, sl] = hi

    y = (_dot(xr_s[...].astype(BF16), cre_ref[...]) - _dot(xi_s[...].astype(BF16), cim_ref[...])
         + d_ref[...] * u)
    zg = 0.5 * y * (1.0 + jnp.tanh(math.sqrt(2.0 / math.pi) * (y + 0.044715 * (y * y * y))))
    gate = _sigmoid(_dot(zg.astype(BF16), wg_ref[...]) + bg_ref[...])
    y_ref[...] = (zg * gate).astype(BF16)

    @pl.when(step == pl.num_programs(0) - 1)
    def _():
        hr_out[...] = st_r[...]
        hi_out[...] = st_i[...]


def _s5(u_tm, h0r, h0i, prm, tb, lane_chunk):
    nb = h0r.shape[0]
    l = u_tm.shape[0] // nb
    fix2 = lambda i: (0, 0)
    vec = pl.BlockSpec((1, S5_STATE), fix2)
    st = pl.BlockSpec((nb, S5_STATE), fix2)
    rows = tb * nb
    return pl.pallas_call(
        functools.partial(_s5_kernel, tb=tb, nb=nb, lane_chunk=lane_chunk),
        out_shape=(jax.ShapeDtypeStruct((l * nb, S5_WIDTH), BF16),
                   jax.ShapeDtypeStruct((nb, S5_STATE), F32), jax.ShapeDtypeStruct((nb, S5_STATE), F32)),
        grid=(l // tb,),
        in_specs=[pl.BlockSpec((rows, S5_WIDTH), lambda i: (i, 0)), st, st, vec, vec, vec,
                  pl.BlockSpec((S5_WIDTH, S5_STATE), fix2), pl.BlockSpec((S5_WIDTH, S5_STATE), fix2),
                  pl.BlockSpec((S5_STATE, S5_WIDTH), fix2), pl.BlockSpec((S5_STATE, S5_WIDTH), fix2),
                  pl.BlockSpec((1, S5_WIDTH), fix2), pl.BlockSpec((S5_WIDTH, S5_WIDTH), fix2),
                  pl.BlockSpec((1, S5_WIDTH), fix2)],
        out_specs=(pl.BlockSpec((rows, S5_WIDTH), lambda i: (i, 0)), st, st),
        scratch_shapes=[pltpu.VMEM((rows, S5_STATE), F32), pltpu.VMEM((rows, S5_STATE), F32),
                        pltpu.VMEM((nb, S5_STATE), F32), pltpu.VMEM((nb, S5_STATE), F32),
                        pltpu.VMEM((8, S5_STATE), F32)],
        compiler_params=pltpu.CompilerParams(dimension_semantics=("arbitrary",),
                                             vmem_limit_bytes=VMEM_LIMIT),
        name="s5",
    )(u_tm, h0r, h0i, prm["lre"], prm["lim"], prm["ldt"], prm["bre"], prm["bim"], prm["cre"],
      prm["cim"], prm["d"], prm["wg"], prm["bg"])


def _conv_silu(u, w8, fix):
    out = fix(pltpu.roll(u, 3, axis=0), 3) * w8[0:1, :]
    for s in (2, 1):
        out = out + fix(pltpu.roll(u, s, axis=0), s) * w8[3 - s:4 - s, :]
    out = out + u * w8[3:4, :]
    return out * _sigmoid(out)


def _dn_chunk_local(q, k, v, zs, rows, prm, sub, n_valid, n_sq, prec):
    bones = prm["bones"]
    incl, strict = _pair_masks(sub)
    sg = _sigmoid(zs)
    gcomp = -jnp.exp(prm["alog_row"]) * _softplus(zs)
    g_rows = -jnp.exp(prm["rowalog"]) * _softplus(rows + prm["rowadd"])
    if n_valid != sub:
        vc = _valid_col(sub, n_valid)
        sg = jnp.where(vc, sg, 0.0)
        gcomp = jnp.where(vc, gcomp, 0.0)
        g_rows = jnp.where(_valid_row(sub, n_valid), g_rows, 0.0)
    g_cum = _mm_sel_l(_ltri(sub), gcomp)
    g_cum_rows = _mm_sel_r(g_rows, _upair(sub))
    res = []
    for p_ in range(N_PAIRS):
        sl = slice(p_ * LANES, (p_ + 1) * LANES)
        qp, kp, vp = q[:, sl], k[:, sl], v[:, sl]
        qp = qp * lax.rsqrt(_seg_mean(qp * qp, bones) * HEAD_DIM + EPS) * (HEAD_DIM ** -0.5)
        kp = kp * lax.rsqrt(_seg_mean(kp * kp, bones) * HEAD_DIM + EPS)
        beta = _expand(sg, 2 * p_, 2 * p_ + 1)
        g_col = _expand(g_cum, 6 + 2 * p_, 7 + 2 * p_)
        g_row = g_cum_rows[p_:p_ + 1, :]
        dec = jnp.exp(jnp.where(incl, g_col - g_row, NEG))
        kb = kp * beta
        sc = _mm(jnp.concatenate([kb, qp], axis=0), _x2(kp), _NT)
        m_pair = jnp.where(strict, sc[:CHUNK] * dec, 0.0)
        a_in = sc[CHUNK:] * dec
        t_bd = _neumann_inverse(_x2(m_pair), n_sq, prec)
        eg = jnp.exp(g_col)
        rhs = jnp.concatenate([_x2(vp * beta), _x2(kb * eg)], axis=1)
        sol = _mm(t_bd, rhs, prec=prec)
        u_ = _fold(sol[:, :LANES])
        w_ = _fold(sol[:, LANES:])
        g_last = _last_row_bcast(g_col, sub)
        res.append(dict(u=u_, w=w_, a_in=a_in, qg=qp * eg, kd=kp * jnp.exp(g_last - g_col),
                        eg_last=jnp.exp(g_last)))
    return res


def _dn_out(o, gate, normw, bones):
    o = o * lax.rsqrt(_seg_mean(o * o, bones) + EPS) * normw
    return o * (gate * _sigmoid(gate))


def _dn_seq_kernel(zq_ref, zk_ref, zv_ref, zg_ref, zs_ref, rows_ref, cp_ref, cw_ref, s0_ref,
                   add_ref, alog_ref, rowadd_ref, rowalog_ref, nw_ref,
                   o_ref, s_out_ref, s_scr, prev_scr, *, nc, prec):
    tstep = pl.program_id(1)

    @pl.when(tstep == 0)
    def _():
        s_scr[...] = s0_ref[0]
        prev_scr[...] = cp_ref[0]

    rb = nc * CHUNK
    prev = prev_scr[...]
    r8 = _iota((8, CONV_CH), 0)

    def fix(rolled, s):
        first = jnp.where(r8 < s, pltpu.roll(prev, s, axis=0), rolled[0:8])
        return jnp.concatenate([first, rolled[8:]], axis=0)

    u_all = jnp.concatenate([zq_ref[...], zk_ref[...], zv_ref[...]], axis=1)
    qkv = _conv_silu(u_all, cw_ref[...], fix)
    prev_scr[...] = u_all[rb - 8:rb]

    prm = dict(bones=_block_ones(), alog_row=alog_ref[...], rowadd=rowadd_ref[...],
               rowalog=rowalog_ref[...])
    bdm = _bd_mask()
    zs_all = zs_ref[...] + add_ref[...]
    for c in range(nc):
        rs = slice(c * CHUNK, (c + 1) * CHUNK)
        loc = _dn_chunk_local(qkv[rs, 0:HEADS_W], qkv[rs, HEADS_W:2 * HEADS_W], qkv[rs, 2 * HEADS_W:],
                              zs_all[rs], rows_ref[c], prm, CHUNK, CHUNK, 5, prec)
        for p_ in range(N_PAIRS):
            d = loc[p_]
            s_prev = s_scr[p_]
            ws = _mm(jnp.concatenate([d["w"], d["qg"]], axis=0), s_prev)
            v_new = d["u"] - ws[:CHUNK]
            o = ws[CHUNK:] + _mm(d["a_in"], _x2(v_new))
            upd = _mm(d["kd"].T, v_new)
            s_scr[p_] = d["eg_last"][0:1, :] * s_prev + jnp.where(bdm, upd, 0.0)
            sl = slice(p_ * LANES, (p_ + 1) * LANES)
            o_ref[rs, sl] = _dn_out(o, zg_ref[rs, sl], nw_ref[...], prm["bones"]).astype(BF16)

    @pl.when(tstep == pl.num_programs(1) - 1)
    def _():
        s_out_ref[0] = s_scr[...]


def _dn_batch_kernel(zq_ref, zk_ref, zv_ref, zg_ref, zs_ref, rows_ref, cp_ref, cw_ref, s0_ref,
                     add_ref, alog_ref, rowadd_ref, rowalog_ref, nw_ref,
                     o_ref, s_out_ref, *, nc, n_valid, prec):
    rb = nc * CHUNK
    pall = cp_ref[...]
    rr = _imod(_iota((rb, CONV_CH), 0), SUB)

    def fix(rolled, s):
        return jnp.where(rr < s, pltpu.roll(pall, (rb - SUB + s) % rb, axis=0), rolled)

    u_all = jnp.concatenate([zq_ref[...], zk_ref[...], zv_ref[...]], axis=1)
    qkv = _conv_silu(u_all, cw_ref[...], fix)

    prm = dict(bones=_block_ones(), alog_row=alog_ref[...], rowadd=rowadd_ref[...],
               rowalog=rowalog_ref[...])
    bdm = _bd_mask()
    zs_all = zs_ref[...] + add_ref[...]
    nsub = CHUNK // SUB
    lane_sub = _idiv(_iota((LANES, CHUNK), 1), SUB)
    for c in range(nc):
        rs = slice(c * CHUNK, (c + 1) * CHUNK)
        loc = _dn_chunk_local(qkv[rs, 0:HEADS_W], qkv[rs, HEADS_W:2 * HEADS_W], qkv[rs, 2 * HEADS_W:],
                              zs_all[rs], rows_ref[c], prm, SUB, n_valid,
                              max(0, math.ceil(math.log2(n_valid)) - 1), prec)
        for p_ in range(N_PAIRS):
            d = loc[p_]
            kd_t = d["kd"].T
            vnews, qss, s_prevs = [], [], []
            for sb in range(nsub):
                r_ = slice(sb * SUB, (sb + 1) * SUB)
                s_prev = s0_ref[c * nsub + sb, p_]
                ws = _mm(jnp.concatenate([d["w"][r_], d["qg"][r_]], axis=0), s_prev)
                vnews.append(d["u"][r_] - ws[:SUB])
                qss.append(ws[SUB:])
                s_prevs.append(s_prev)
            v_new = jnp.concatenate(vnews, axis=0)
            o = jnp.concatenate(qss, axis=0) + _mm(d["a_in"], _x2(v_new))
            for sb in range(nsub):
                upd = _mm(jnp.where(lane_sub == sb, kd_t, 0.0), v_new)
                eg = d["eg_last"][sb * SUB:sb * SUB + 1, :]
                s_out_ref[c * nsub + sb, p_] = eg * s_prevs[sb] + jnp.where(bdm, upd, 0.0)
            sl = slice(p_ * LANES, (p_ + 1) * LANES)
            o_ref[rs, sl] = _dn_out(o, zg_ref[rs, sl], nw_ref[...], prm["bones"]).astype(BF16)


def _zspec(rb, width, col, rowmap):
    return pl.BlockSpec((rb, width), lambda *g: (rowmap(*g), col))


def _deltanet(z, rows, convp, s0, prm, *, seq, batch, nc, n_valid, prec):
    t = z.shape[0]
    rb = nc * CHUNK
    fix2 = lambda *g: (0, 0)
    small = [pl.BlockSpec((1, LANES), fix2), pl.BlockSpec((1, LANES), fix2),
             pl.BlockSpec((16, LANES), fix2), pl.BlockSpec((16, LANES), fix2),
             pl.BlockSpec((1, LANES), fix2)]
    small_args = (prm["add_row"], prm["alog_row"], prm["rowadd"], prm["rowalog"], prm["dn_normw"])
    if seq:
        nt = t // batch // rb
        rowmap = lambda b, i: b * nt + i
        grid = (batch, nt)
        cp_spec = pl.BlockSpec((1, 8, CONV_CH), lambda b, i: (b, 0, 0))
        s_spec = pl.BlockSpec((1, N_PAIRS, LANES, LANES), lambda b, i: (b, 0, 0, 0))
        kern = functools.partial(_dn_seq_kernel, nc=nc, prec=prec)
        scratch = [pltpu.VMEM((N_PAIRS, LANES, LANES), F32), pltpu.VMEM((8, CONV_CH), F32)]
        sem = ("parallel", "arbitrary")
    else:
        rowmap = lambda i: i
        grid = (t // rb,)
        cp_spec = pl.BlockSpec((rb, CONV_CH), lambda i: (i, 0))
        nsb = rb // SUB
        s_spec = pl.BlockSpec((nsb, N_PAIRS, LANES, LANES), lambda i: (i, 0, 0, 0))
        kern = functools.partial(_dn_batch_kernel, nc=nc, n_valid=n_valid, prec=prec)
        scratch = []
        sem = ("parallel",)
    in_specs = [_zspec(rb, HEADS_W, 0, rowmap), _zspec(rb, HEADS_W, 1, rowmap), _zspec(rb, HEADS_W, 2, rowmap),
                _zspec(rb, HEADS_W, 3, rowmap), _zspec(rb, LANES, 26, rowmap),
                pl.BlockSpec((nc, 16, LANES), lambda *g: (rowmap(*g), 0, 0)), cp_spec,
                pl.BlockSpec((8, CONV_CH), fix2), s_spec] + small
    return pl.pallas_call(
        kern,
        out_shape=(jax.ShapeDtypeStruct((t, HEADS_W), BF16), jax.ShapeDtypeStruct(s0.shape, F32)),
        grid=grid,
        in_specs=in_specs,
        out_specs=(pl.BlockSpec((rb, HEADS_W), lambda *g: (rowmap(*g), 0)), s_spec),
        scratch_shapes=scratch,
        compiler_params=pltpu.CompilerParams(dimension_semantics=sem, vmem_limit_bytes=VMEM_LIMIT),
        name="deltanet_seq" if seq else "deltanet_batch",
    )(z, z, z, z, z, rows, convp, prm["conv_w8"], s0, *small_args)


def _ml_chunk(q, k, v, zs, rows, ms_rows, cn_get, prm, sub, n_valid):
    incl, _ = _pair_masks(sub)
    lo = _lane_lo((CHUNK, LANES))
    rows = rows + prm["rowadd"]
    ig_c = zs
    ls_c = -_softplus(-zs)
    ig_r = rows
    ls_r = -_softplus(-rows)
    if n_valid != sub:
        vc = _valid_col(sub, n_valid)
        vr = _valid_row(sub, n_valid)
        ig_c = jnp.where(vc, ig_c, NEG)
        ls_c = jnp.where(vc, ls_c, 0.0)
        ig_r = jnp.where(vr, ig_r, NEG)
        ls_r = jnp.where(vr, ls_r, 0.0)
    b_c = _mm_sel_l(_ltri(sub), ls_c)
    b_r = _mm_sel_r(ls_r, _upair(sub))
    ones2 = prm["bones"]
    bdm2 = jnp.concatenate([_bd_mask(), _bd_mask()], axis=1)
    nsub = CHUNK // sub
    out = []
    for p_ in range(N_PAIRS):
        sl = slice(p_ * LANES, (p_ + 1) * LANES)
        qp, kp, vp = q[:, sl], k[:, sl] * (HEAD_DIM ** -0.5), v[:, sl]
        b_col = _expand(b_c, 18 + 2 * p_, 19 + 2 * p_)
        ig_col = _expand(ig_c, 12 + 2 * p_, 13 + 2 * p_)
        b_row = b_r[6 + p_:7 + p_, :]
        ig_row = ig_r[3 + p_:4 + p_, :]
        dl = jnp.where(incl, b_col - b_row + ig_row, NEG)
        m0 = jnp.max(jnp.where(lo, dl, NEG), axis=1, keepdims=True)
        m1 = jnp.max(jnp.where(lo, NEG, dl), axis=1, keepdims=True)
        inter = b_col + ms_rows[p_]
        mt = jnp.maximum(inter, jnp.where(lo, m0, m1))
        s = _mm(qp, _x2(kp), _NT) * jnp.exp(dl - mt)
        sv = _mm(s, jnp.concatenate([_x2(vp), ones2], axis=1))
        qcn = jnp.concatenate([_mm(qp[sb * sub:(sb + 1) * sub], cn_get(p_, sb)) for sb in range(nsub)],
                              axis=0)
        w_inter = jnp.exp(inter - mt)
        num = w_inter * qcn[:, :LANES] + sv[:, :LANES]
        den = w_inter * qcn[:, LANES:] + sv[:, LANES:]
        h = num / jnp.maximum(jnp.abs(den), jnp.exp(-mt))
        m_end = _last_row_bcast(mt, sub)
        b_last = _last_row_bcast(b_col, sub)
        a_end = jnp.exp(b_last + ms_rows[p_] - m_end)
        kw = kp * jnp.exp(b_last - b_col + ig_col - m_end)
        kw_t = kw.T
        v1 = jnp.concatenate([vp, jnp.ones((CHUNK, LANES), F32)], axis=1)
        new_states, new_m = [], []
        for sb in range(nsub):
            if nsub == 1:
                lhs = kw_t
            else:
                lhs = jnp.where(_idiv(_iota((LANES, CHUNK), 1), sub) == sb, kw_t, 0.0)
            upd = _mm(lhs, v1)
            a_row = a_end[sb * sub:sb * sub + 1, :]
            a2 = jnp.concatenate([a_row, a_row], axis=1)
            new_states.append(a2 * cn_get(p_, sb) + jnp.where(bdm2, upd, 0.0))
            new_m.append(m_end[sb * sub:sb * sub + 1, :])
        out.append((h, new_states, new_m))
    return out


def _ml_out(h, gate, normw, bones):
    h = h * lax.rsqrt(_seg_mean(h * h, bones) + EPS) * normw
    return h * _sigmoid(gate)


def _ml_seq_kernel(zq_ref, zk_ref, zv_ref, zo_ref, zs_ref, rows_ref, cn0_ref, m0_ref, add_ref, rowadd_ref,
                   nw_ref, o_ref, cn_out_ref, m_out_ref, cn_scr, m_scr, *, nc):
    tstep = pl.program_id(1)

    @pl.when(tstep == 0)
    def _():
        cn_scr[...] = cn0_ref[0]
        m_scr[...] = m0_ref[0]

    prm = dict(bones=_block_ones(), rowadd=rowadd_ref[...])
    zs_all = zs_ref[...] + add_ref[...]
    for c in range(nc):
        rs = slice(c * CHUNK, (c + 1) * CHUNK)
        ms_rows = [jnp.broadcast_to(m_scr[p_:p_ + 1, :], (CHUNK, LANES)) for p_ in range(N_PAIRS)]
        res = _ml_chunk(zq_ref[rs, :], zk_ref[rs, :], zv_ref[rs, :], zs_all[rs], rows_ref[c], ms_rows,
                        lambda p_, sb: cn_scr[p_], prm, CHUNK, CHUNK)
        for p_ in range(N_PAIRS):
            h, new_states, new_m = res[p_]
            cn_scr[p_] = new_states[0]
            m_scr[p_:p_ + 1, :] = new_m[0]
            sl = slice(p_ * LANES, (p_ + 1) * LANES)
            o_ref[rs, sl] = _ml_out(h, zo_ref[rs, sl], nw_ref[...], prm["bones"]).astype(BF16)

    @pl.when(tstep == pl.num_programs(1) - 1)
    def _():
        cn_out_ref[0] = cn_scr[...]
        m_out_ref[0] = m_scr[...]


def _ml_batch_kernel(zq_ref, zk_ref, zv_ref, zo_ref, zs_ref, rows_ref, cn0_ref, m0_ref, add_ref, rowadd_ref,
                     nw_ref, o_ref, cn_out_ref, m_out_ref, *, nc, n_valid):
    prm = dict(bones=_block_ones(), rowadd=rowadd_ref[...])
    zs_all = zs_ref[...] + add_ref[...]
    nsub = CHUNK // SUB
    for c in range(nc):
        rs = slice(c * CHUNK, (c + 1) * CHUNK)
        ms_rows = []
        for p_ in range(N_PAIRS):
            ms_rows.append(jnp.concatenate(
                [jnp.broadcast_to(m0_ref[c * nsub + sb, p_:p_ + 1, :], (SUB, LANES)) for sb in range(nsub)],
                axis=0))
        res = _ml_chunk(zq_ref[rs, :], zk_ref[rs, :], zv_ref[rs, :], zs_all[rs], rows_ref[c], ms_rows,
                        lambda p_, sb, c=c: cn0_ref[c * nsub + sb, p_], prm, SUB, n_valid)
        for p_ in range(N_PAIRS):
            h, new_states, new_m = res[p_]
            for sb in range(nsub):
                cn_out_ref[c * nsub + sb, p_] = new_states[sb]
                m_out_ref[c * nsub + sb, p_:p_ + 1, :] = new_m[sb]
            sl = slice(p_ * LANES, (p_ + 1) * LANES)
            o_ref[rs, sl] = _ml_out(h, zo_ref[rs, sl], nw_ref[...], prm["bones"]).astype(BF16)


def _mlstm(z, rows, cn0, m0, prm, *, seq, batch, nc, n_valid):
    t = z.shape[0]
    rb = nc * CHUNK
    fix2 = lambda *g: (0, 0)
    if seq:
        nt = t // batch // rb
        rowmap = lambda b, i: b * nt + i
        grid = (batch, nt)
        cn_spec = pl.BlockSpec((1, N_PAIRS, LANES, 2 * LANES), lambda b, i: (b, 0, 0, 0))
        m_spec = pl.BlockSpec((1, 8, LANES), lambda b, i: (b, 0, 0))
        kern = functools.partial(_ml_seq_kernel, nc=nc)
        scratch = [pltpu.VMEM((N_PAIRS, LANES, 2 * LANES), F32), pltpu.VMEM((8, LANES), F32)]
        sem = ("parallel", "arbitrary")
    else:
        rowmap = lambda i: i
        grid = (t // rb,)
        nsb = rb // SUB
        cn_spec = pl.BlockSpec((nsb, N_PAIRS, LANES, 2 * LANES), lambda i: (i, 0, 0, 0))
        m_spec = pl.BlockSpec((nsb, 8, LANES), lambda i: (i, 0, 0))
        kern = functools.partial(_ml_batch_kernel, nc=nc, n_valid=n_valid)
        scratch = []
        sem = ("parallel",)
    in_specs = [_zspec(rb, HEADS_W, 4, rowmap), _zspec(rb, HEADS_W, 5, rowmap), _zspec(rb, HEADS_W, 6, rowmap),
                _zspec(rb, HEADS_W, 7, rowmap), _zspec(rb, LANES, 26, rowmap),
                pl.BlockSpec((nc, 16, LANES), lambda *g: (rowmap(*g), 0, 0)), cn_spec, m_spec,
                pl.BlockSpec((1, LANES), fix2), pl.BlockSpec((16, LANES), fix2),
                pl.BlockSpec((1, LANES), fix2)]
    return pl.pallas_call(
        kern,
        out_shape=(jax.ShapeDtypeStruct((t, HEADS_W), BF16), jax.ShapeDtypeStruct(cn0.shape, F32),
                   jax.ShapeDtypeStruct(m0.shape, F32)),
        grid=grid,
        in_specs=in_specs,
        out_specs=(pl.BlockSpec((rb, HEADS_W), lambda *g: (rowmap(*g), 0)), cn_spec, m_spec),
        scratch_shapes=scratch,
        compiler_params=pltpu.CompilerParams(dimension_semantics=sem, vmem_limit_bytes=VMEM_LIMIT),
        name="mlstm_seq" if seq else "mlstm_batch",
    )(z, z, z, z, z, rows, cn0, m0, prm["add_row"], prm["rowadd"], prm["ml_normw"])


def _pack_w_in(w_in):
    offs = np.concatenate([[0], np.cumsum(IN_SIZES)])
    seg = lambda i: w_in[:, int(offs[i]):int(offs[i + 1])]
    small = jnp.concatenate([seg(3), seg(4), seg(9), seg(10)], axis=1)
    small = jnp.pad(small, ((0, 0), (0, LANES - 4 * N_HEADS)))
    cols = [seg(0), seg(1), seg(2), seg(5), seg(6), seg(7), seg(8), seg(11), seg(12), small]
    return jnp.concatenate(cols, axis=1).astype(BF16)


def _pair_rows(v6):
    return jnp.repeat(v6.reshape(N_PAIRS, 2), HEAD_DIM, axis=1).reshape(N_PAIRS, LANES)


def _layer_params(l, a):
    f = lambda name: a[name][l].astype(F32)
    zeros6 = jnp.zeros((N_HEADS,), F32)
    add_row = jnp.concatenate([zeros6, f("dn_dt_bias"), f("ml_i_bias"), f("ml_f_bias")])
    add_row = jnp.pad(add_row, (0, LANES - 4 * N_HEADS))[None, :]
    alog_row = jnp.pad(jnp.concatenate([zeros6, f("dn_a_log")]), (0, LANES - 2 * N_HEADS))[None, :]
    rowadd = jnp.concatenate([_pair_rows(f("dn_dt_bias")), _pair_rows(f("ml_i_bias")),
                              _pair_rows(f("ml_f_bias")), jnp.zeros((7, LANES), F32)], axis=0)
    rowalog = jnp.concatenate([_pair_rows(f("dn_a_log")), jnp.zeros((13, LANES), F32)], axis=0)
    eye_g = jnp.eye(S5_GROUPS, dtype=F32)
    bd_in = lambda b: jnp.einsum("gpc,gh->gchp", b, eye_g).reshape(S5_WIDTH, S5_STATE).astype(BF16)
    bd_out = lambda c: jnp.einsum("gcp,gh->gphc", c, eye_g).reshape(S5_STATE, S5_WIDTH).astype(BF16)
    w_out = f("w_out").astype(BF16)
    return dict(
        norm_mix=f("norm_mix")[None, :], w_in=_pack_w_in(f("w_in")),
        add_row=add_row, alog_row=alog_row, rowadd=rowadd, rowalog=rowalog,
        conv_w8=jnp.pad(f("dn_conv_w"), ((0, 8 - CONV_W), (0, 0))),
        dn_normw=jnp.tile(f("dn_norm"), 2)[None, :], ml_normw=jnp.tile(f("ml_norm"), 2)[None, :],
        s5=dict(lre=f("s5_lam_re").reshape(1, S5_STATE), lim=f("s5_lam_im").reshape(1, S5_STATE),
                ldt=jnp.repeat(f("s5_log_dt"), S5_P)[None, :],
                bre=bd_in(f("s5_b_re")), bim=bd_in(f("s5_b_im")),
                cre=bd_out(f("s5_c_re")), cim=bd_out(f("s5_c_im")),
                d=f("s5_d").reshape(1, S5_WIDTH), wg=f("s5_w_glu").astype(BF16),
                bg=f("s5_b_glu")[None, :]),
        w_out1=w_out[:HEADS_W], w_out2=w_out[HEADS_W:2 * HEADS_W], w_out3=w_out[2 * HEADS_W:],
        norm_x=f("norm_x")[None, :], w_xq=f("w_xq").astype(BF16), w_xo=f("w_xo").astype(BF16),
        norm_ff=f("norm_ff")[None, :], w_ff1=f("w_ff1").astype(BF16), w_ff2=f("w_ff2").astype(BF16),
    )


def _row_form(z, n_chunks):
    zs = z[:, Z_COLS - LANES:Z_COLS - LANES + 4 * N_HEADS].reshape(n_chunks, CHUNK, 4, N_PAIRS, 2)
    r = jnp.transpose(zs[:, :, 1:], (0, 2, 3, 4, 1)).reshape(n_chunks, 9, LANES)
    return jnp.pad(r, ((0, 0), (0, 7), (0, 0)))


def _bd_pairs(s):
    n, _, r, c = s.shape
    s = s.reshape(n, N_PAIRS, 2, r, c)
    z = jnp.zeros_like(s[:, :, 0])
    top = jnp.concatenate([s[:, :, 0], z], axis=-1)
    bot = jnp.concatenate([z, s[:, :, 1]], axis=-1)
    return jnp.concatenate([top, bot], axis=-2)


def _un_bd_pairs(s):
    n = s.shape[0]
    a = s[:, :, :HEAD_DIM, :HEAD_DIM]
    b = s[:, :, HEAD_DIM:, HEAD_DIM:]
    return jnp.stack([a, b], axis=2).reshape(n, N_HEADS, HEAD_DIM, HEAD_DIM)


def _pack_cn(c, n):
    nb = jnp.broadcast_to(n[..., None], c.shape)
    return jnp.concatenate([_bd_pairs(c), _bd_pairs(nb)], axis=-1)


def _unpack_cn(cn):
    c = _un_bd_pairs(cn[..., :LANES])
    n = _un_bd_pairs(cn[..., LANES:])[..., 0]
    return c, n


def _pack_m(m):
    s = m.shape[0]
    r = jnp.repeat(m.reshape(s, N_PAIRS, 2), HEAD_DIM, axis=2).reshape(s, N_PAIRS, LANES)
    return jnp.pad(r, ((0, 0), (0, 8 - N_PAIRS), (0, 0)))


def _unpack_m(mp):
    return mp[:, :N_PAIRS, ::HEAD_DIM].reshape(mp.shape[0], N_HEADS)


ROW_TILE = 512
SEQ_CHUNKS = 2
S5_STEPS = 64


def _tile(n, pref):
    tile = min(n, pref)
    assert n % tile == 0, (n, pref)
    return tile


def _layer_prompt(x, mem_k, mem_v, prm, batch, seq_len, final_g):
    t = batch * seq_len
    tm = _tile(t, ROW_TILE)
    nc = _tile(seq_len // CHUNK, SEQ_CHUNKS)
    z = _norm_matmul(x, prm["norm_mix"], prm["w_in"], tm)
    rows = _row_form(z, t // CHUNK)
    new_conv = z.reshape(batch, seq_len, Z_COLS)[:, seq_len - 3:, :CONV_CH]

    dn, s_new = _deltanet(z, rows, jnp.zeros((batch, 8, CONV_CH), F32),
                          jnp.zeros((batch, N_PAIRS, LANES, LANES), F32), prm,
                          seq=True, batch=batch, nc=nc, n_valid=CHUNK, prec=3)
    ml, cn_new, m_new = _mlstm(z, rows, jnp.zeros((batch, N_PAIRS, LANES, 2 * LANES), F32),
                               jnp.zeros((batch, 8, LANES), F32), prm,
                               seq=True, batch=batch, nc=nc, n_valid=CHUNK)
    u_tm = jnp.transpose(z[:, 3072:3072 + S5_WIDTH].reshape(batch, seq_len, S5_WIDTH), (1, 0, 2))
    zero_h = jnp.zeros((batch, S5_STATE), F32)
    y_tm, hr, hi = _s5(u_tm.reshape(t, S5_WIDTH), zero_h, zero_h, prm["s5"], _tile(seq_len, S5_STEPS),
                       S5_STATE)
    s5o = jnp.transpose(y_tm.reshape(seq_len, batch, S5_WIDTH), (1, 0, 2)).reshape(t, S5_WIDTH)

    x1, q = _out_proj(dn, ml, s5o, x, prm["w_out1"], prm["w_out2"], prm["w_out3"], prm["norm_x"],
                      prm["w_xq"], tm)
    att = _attn_seq(q, mem_k, mem_v, batch, _tile(seq_len, ROW_TILE))
    g_fin = final_g if final_g is not None else prm["norm_ff"]
    x3 = _ffn(x1, att, prm["w_xo"], prm["norm_ff"], prm["w_ff1"], prm["w_ff2"], g_fin, tm,
              final_g is not None)
    c_new, n_new = _unpack_cn(cn_new)
    st = (new_conv, _un_bd_pairs(s_new), c_new, n_new, _unpack_m(m_new),
          hr.reshape(batch, S5_GROUPS, S5_P), hi.reshape(batch, S5_GROUPS, S5_P))
    return x3, st


def _layer_sample(x, mem_k, mem_v, conv_buf, dn_s, ml_c, ml_n, ml_m, ssm_re, ssm_im, prm, batch, seq_len,
                  final_g):
    t = batch * seq_len
    z = _norm_matmul(x, prm["norm_mix"], prm["w_in"], t)
    z3 = z.reshape(batch, seq_len, Z_COLS)
    new_conv = jnp.concatenate([conv_buf, z3[:, :, :CONV_CH]], axis=1)[:, seq_len:]
    zp = jnp.pad(z3, ((0, 0), (0, SUB - seq_len), (0, 0))).reshape(batch * SUB, Z_COLS)
    rows = _row_form(zp, batch * SUB // CHUNK)
    convp = jnp.pad(conv_buf, ((0, 0), (SUB - (CONV_W - 1), 0), (0, 0))).reshape(batch * SUB, CONV_CH)

    dn, s_new = _deltanet(zp, rows, convp, _bd_pairs(dn_s), prm,
                          seq=False, batch=batch, nc=1, n_valid=seq_len, prec=3)
    ml, cn_new, m_new = _mlstm(zp, rows, _pack_cn(ml_c, ml_n), _pack_m(ml_m), prm,
                               seq=False, batch=batch, nc=1, n_valid=seq_len)
    unpad = lambda a: a.reshape(batch, SUB, -1)[:, :seq_len].reshape(t, -1)
    dn, ml = unpad(dn), unpad(ml)

    u_tm = jnp.transpose(z3[:, :, 3072:3072 + S5_WIDTH], (1, 0, 2))
    y_tm, hr, hi = _s5(u_tm.reshape(t, S5_WIDTH), ssm_re.reshape(batch, S5_STATE),
                       ssm_im.reshape(batch, S5_STATE), prm["s5"], seq_len, LANES)
    s5o = jnp.transpose(y_tm.reshape(seq_len, batch, S5_WIDTH), (1, 0, 2)).reshape(t, S5_WIDTH)

    x1, q = _out_proj(dn, ml, s5o, x, prm["w_out1"], prm["w_out2"], prm["w_out3"], prm["norm_x"],
                      prm["w_xq"], t)
    q3 = jnp.pad(q.reshape(batch, seq_len, X_WIDTH), ((0, 0), (0, SUB - seq_len), (0, 0)))
    att = _attn_batch(q3, mem_k, mem_v, 8)[:, :seq_len].reshape(t, X_WIDTH)
    g_fin = final_g if final_g is not None else prm["norm_ff"]
    x3 = _ffn(x1, att, prm["w_xo"], prm["norm_ff"], prm["w_ff1"], prm["w_ff2"], g_fin, t,
              final_g is not None)
    c_new, n_new = _unpack_cn(cn_new)
    st = (new_conv, _un_bd_pairs(s_new), c_new, n_new, _unpack_m(m_new),
          hr.reshape(batch, S5_GROUPS, S5_P), hi.reshape(batch, S5_GROUPS, S5_P))
    return x3, st


def kernel(x_prompt, x_sample, mem_prompt, state_dn_conv, state_dn_s, state_ml_c, state_ml_n, state_ml_m, state_ssm_re, state_ssm_im, cache_mem_k, cache_mem_v, norm_mix, w_in, dn_conv_w, dn_a_log, dn_dt_bias, dn_norm, ml_i_bias, ml_f_bias, ml_norm, s5_lam_re, s5_lam_im, s5_log_dt, s5_b_re, s5_b_im, s5_c_re, s5_c_im, s5_d, s5_w_glu, s5_b_glu, w_out, norm_x, norm_mem, w_xq, w_xk, w_xv, w_xo, norm_ff, w_ff1, w_ff2, norm_final):
    a = dict(norm_mix=norm_mix, w_in=w_in, dn_conv_w=dn_conv_w, dn_a_log=dn_a_log, dn_dt_bias=dn_dt_bias,
             dn_norm=dn_norm, ml_i_bias=ml_i_bias, ml_f_bias=ml_f_bias, ml_norm=ml_norm,
             s5_lam_re=s5_lam_re, s5_lam_im=s5_lam_im, s5_log_dt=s5_log_dt, s5_b_re=s5_b_re,
             s5_b_im=s5_b_im, s5_c_re=s5_c_re, s5_c_im=s5_c_im, s5_d=s5_d, s5_w_glu=s5_w_glu,
             s5_b_glu=s5_b_glu, w_out=w_out, norm_x=norm_x, w_xq=w_xq, w_xo=w_xo, norm_ff=norm_ff,
             w_ff1=w_ff1, w_ff2=w_ff2)
    prms = [_layer_params(l, a) for l in range(DEPTH)]
    g_final = norm_final.astype(F32)[None, :]

    bp, lp, _ = x_prompt.shape
    bs, ls, _ = x_sample.shape

    xp = x_prompt.reshape(bp * lp, D_MODEL)
    mem = mem_prompt.reshape(bp * N_MEM, D_MODEL)
    p_st = []
    for l in range(DEPTH):
        tmem = _tile(bp * N_MEM, ROW_TILE)
        mk = _norm_matmul(mem, norm_mem[l][None, :], w_xk[l].astype(BF16), tmem).reshape(bp, N_MEM, X_WIDTH)
        mv = _norm_matmul(mem, norm_mem[l][None, :], w_xv[l].astype(BF16), tmem).reshape(bp, N_MEM, X_WIDTH)
        xp, st = _layer_prompt(xp, mk, mv, prms[l], bp, lp, g_final if l == DEPTH - 1 else None)
        p_st.append(st + (mk.reshape(bp, N_MEM, X_HEADS, X_HD), mv.reshape(bp, N_MEM, X_HEADS, X_HD)))
    y_prompt = xp.reshape(bp, lp, D_MODEL)

    xs = x_sample.reshape(bs * ls, D_MODEL)
    s_st = []
    for l in range(DEPTH):
        xs, st = _layer_sample(xs, cache_mem_k[l].reshape(bs, N_MEM, X_WIDTH),
                               cache_mem_v[l].reshape(bs, N_MEM, X_WIDTH), state_dn_conv[l], state_dn_s[l],
                               state_ml_c[l], state_ml_n[l], state_ml_m[l], state_ssm_re[l], state_ssm_im[l],
                               prms[l], bs, ls, g_final if l == DEPTH - 1 else None)
        s_st.append(st)
    y_sample = xs.reshape(bs, ls, D_MODEL)

    stack = lambda sts, i: jnp.stack([s[i] for s in sts])
    return (y_prompt, y_sample) + tuple(stack(p_st, i) for i in range(9)) + tuple(stack(s_st, i) for i in range(7))
```

```python
import functools
import math

import numpy as np
import jax
import jax.numpy as jnp
from jax import lax
from jax.experimental import pallas as pl
from jax.experimental.pallas import tpu as pltpu

F32 = jnp.float32
BF16 = jnp.bfloat16

D_MODEL = 1024
DEPTH = 2
N_HEADS = 6
HEAD_DIM = 64
N_PAIRS = N_HEADS // 2
HEADS_W = N_HEADS * HEAD_DIM
CONV_W = 4
CONV_CH = 3 * HEADS_W
S5_WIDTH = 256
S5_GROUP = 16
S5_GROUPS = 16
S5_P = 64
S5_STATE = S5_GROUPS * S5_P
N_MEM = 256
X_HEADS = 4
X_HD = 128
X_WIDTH = X_HEADS * X_HD
D_FF = 4 * D_MODEL
EPS = 1e-6
IN_SIZES = (384, 384, 384, 6, 6, 384, 384, 384, 384, 6, 6, 384, 256)

LANES = 128
CHUNK = 64
SUB = 8
NEG = -1e30
Z_COLS = 3456
VMEM_LIMIT = 52 * 1024 * 1024

_NN = (((1,), (0,)), ((), ()))
_NT = (((1,), (1,)), ((), ()))


def _dot(a, b, dims=_NN):
    return lax.dot_general(a, b, dims, preferred_element_type=F32)


def _split(a, n):
    out = []
    r = a
    for i in range(n):
        h = r.astype(BF16)
        out.append(h)
        if i + 1 < n:
            r = r - h.astype(F32)
    return out


def _mm(a, b, dims=_NN, prec=1):
    if prec == 1:
        return _dot(a.astype(BF16), b.astype(BF16), dims)
    a1, a2 = _split(a, 2)
    b1, b2 = _split(b, 2)
    return (_dot(a1, b2, dims) + _dot(a2, b1, dims)) + _dot(a1, b1, dims)


def _mm_sel_r(a, sel, n=3):
    parts = _split(a, n)
    acc = _dot(parts[-1], sel)
    for p_ in parts[-2::-1]:
        acc = acc + _dot(p_, sel)
    return acc


def _mm_sel_l(sel, a, n=3):
    parts = _split(a, n)
    acc = _dot(sel, parts[-1])
    for p_ in parts[-2::-1]:
        acc = acc + _dot(sel, p_)
    return acc


def _sigmoid(x):
    return 1.0 / (1.0 + jnp.exp(-x))


def _softplus(x):
    return jnp.maximum(x, 0.0) + jnp.log(1.0 + jnp.exp(-jnp.abs(x)))


def _rmsnorm_rows(x, g):
    return x * lax.rsqrt(jnp.mean(x * x, axis=-1, keepdims=True) + EPS) * g


def _iota(shape, dim):
    return lax.broadcasted_iota(jnp.int32, shape, dim)


def _idiv(x, n):
    return x >> (n.bit_length() - 1)


def _imod(x, n):
    return x & (n - 1)


def _lane_lo(shape):
    return _iota(shape, 1) < HEAD_DIM


def _x2(x):
    lo = _lane_lo(x.shape)
    return jnp.concatenate([jnp.where(lo, x, 0.0), jnp.where(lo, 0.0, x)], axis=0)


def _fold(y):
    r = y.shape[0] // 2
    return y[:r] + y[r:]


def _expand(a, c0, c1):
    r = a.shape[0]
    return jnp.where(_lane_lo((r, LANES)), a[:, c0:c0 + 1], a[:, c1:c1 + 1])


def _block_ones():
    i = _iota((LANES, LANES), 0)
    j = _iota((LANES, LANES), 1)
    return jnp.where((i < HEAD_DIM) == (j < HEAD_DIM), 1.0, 0.0).astype(BF16)


def _seg_mean(x, bones):
    return _mm_sel_r(x, bones, n=2) * (1.0 / HEAD_DIM)


def _ltri(sub):
    i = _iota((CHUNK, CHUNK), 0)
    j = _iota((CHUNK, CHUNK), 1)
    ok = (j <= i) & (_idiv(i, sub) == _idiv(j, sub)) if sub != CHUNK else (j <= i)
    return jnp.where(ok, 1.0, 0.0).astype(BF16)


def _upair(sub):
    a = _iota((LANES, LANES), 0)
    b = _iota((LANES, LANES), 1)
    ok = ((a < HEAD_DIM) == (b < HEAD_DIM)) & (a <= b)
    if sub != CHUNK:
        ok = ok & (_idiv(_imod(a, HEAD_DIM), sub) == _idiv(_imod(b, HEAD_DIM), sub))
    return jnp.where(ok, 1.0, 0.0).astype(BF16)


def _pair_masks(sub):
    i = _iota((CHUNK, LANES), 0)
    j = _imod(_iota((CHUNK, LANES), 1), HEAD_DIM)
    same = (_idiv(i, sub) == _idiv(j, sub)) if sub != CHUNK else (i >= 0)
    return same & (j <= i), same & (j < i)


def _valid_col(sub, n_valid):
    i = _iota((CHUNK, 1), 0)
    return _imod(i, sub) < n_valid


def _valid_row(sub, n_valid):
    j = _iota((1, LANES), 1)
    return _imod(j, sub) < n_valid


def _bd_mask():
    i = _iota((LANES, LANES), 0)
    j = _iota((LANES, LANES), 1)
    return (i < HEAD_DIM) == (j < HEAD_DIM)


def _parts(a, prec):
    return tuple(_split(a, 1 if prec == 1 else 2))


def _mmp(ap, bp, dims=_NN):
    acc = None
    if len(bp) > 1:
        acc = _dot(ap[0], bp[1], dims)
    if len(ap) > 1:
        cross = _dot(ap[1], bp[0], dims)
        acc = cross if acc is None else acc + cross
    lead = _dot(ap[0], bp[0], dims)
    return lead if acc is None else acc + lead


def _blk_eq(n, b):
    return _idiv(_iota((n, n), 0), b) == _idiv(_iota((n, n), 1), b)


TRI_BASE = 8


def _tri_inverse_units(mbds, sub, n_valid, prec):
    n = LANES
    eye = jnp.where(_iota((n, n), 0) == _iota((n, n), 1), 1.0, 0.0)
    if sub == CHUNK:
        base, n_sq = TRI_BASE, TRI_BASE.bit_length() - 2
        blk = _blk_eq(n, base)
        m0 = [jnp.where(blk, m, 0.0) for m in mbds]
    else:
        base, n_sq = sub, max(0, math.ceil(math.log2(n_valid)) - 1)
        m0 = mbds
    ts = [eye - m for m in m0]
    ps = m0
    for _ in range(n_sq):
        pp = [_parts(p_, prec) for p_ in ps]
        ps = [_mmp(q_, q_) for q_ in pp]
        pp = [_parts(p_, prec) for p_ in ps]
        ts = [t + _mmp(_parts(t, prec), q_) for t, q_ in zip(ts, pp)]
    b = base
    while b < sub:
        outer = _blk_eq(n, 2 * b) & jnp.logical_not(_blk_eq(n, b))
        tp = [_parts(t, prec) for t in ts]
        xs = [_mmp(_parts(jnp.where(outer, m, 0.0), prec), t_) for m, t_ in zip(mbds, tp)]
        ts = [t - _mmp(t_, _parts(x, prec)) for t, t_, x in zip(ts, tp, xs)]
        b *= 2
    return ts


def _last_row_bcast(a, sub):
    if sub == CHUNK:
        return jnp.broadcast_to(a[CHUNK - 1:CHUNK, :], a.shape)
    parts = [jnp.broadcast_to(a[s * sub + sub - 1:s * sub + sub, :], (sub, a.shape[1]))
             for s in range(CHUNK // sub)]
    return jnp.concatenate(parts, axis=0)


def _bd_from_stacked(s, bdm):
    return jnp.where(bdm, jnp.concatenate([s, s], axis=1), 0.0)


def _stacked_from_bd(s):
    return s[:, :HEAD_DIM] + s[:, HEAD_DIM:]


def _in_proj_kernel(x_ref, g_ref, w_ref, o_ref):
    xn = _rmsnorm_rows(x_ref[...], g_ref[...]).astype(BF16)
    o_ref[...] = _dot(xn, w_ref[...])


def _norm_matmul(x, g, w, tm):
    t, k = x.shape
    n = w.shape[1]
    return pl.pallas_call(
        _in_proj_kernel,
        out_shape=jax.ShapeDtypeStruct((t, n), F32),
        grid=(t // tm,),
        in_specs=[pl.BlockSpec((tm, k), lambda i: (i, 0)),
                  pl.BlockSpec((1, k), lambda i: (0, 0)),
                  pl.BlockSpec((k, n), lambda i: (0, 0))],
        out_specs=pl.BlockSpec((tm, n), lambda i: (i, 0)),
        compiler_params=pltpu.CompilerParams(dimension_semantics=("parallel",),
                                             vmem_limit_bytes=VMEM_LIMIT),
        name="norm_matmul",
    )(x, g, w)


def _out_proj_kernel(dn_ref, ml_ref, s5_ref, x_ref, w1_ref, w2_ref, w3_ref, g_ref, wq_ref,
                     x1_ref, q_ref):
    acc = x_ref[...] + _dot(dn_ref[...], w1_ref[...])
    acc = acc + _dot(ml_ref[...], w2_ref[...])
    acc = acc + _dot(s5_ref[...], w3_ref[...])
    x1_ref[...] = acc
    xn = _rmsnorm_rows(acc, g_ref[...]).astype(BF16)
    q_ref[...] = _dot(xn, wq_ref[...]).astype(BF16)


def _out_proj(dn, ml, s5, x, w1, w2, w3, g, wq, tm):
    t = x.shape[0]
    row = lambda i: (i, 0)
    fix = lambda i: (0, 0)
    return pl.pallas_call(
        _out_proj_kernel,
        out_shape=(jax.ShapeDtypeStruct((t, D_MODEL), F32), jax.ShapeDtypeStruct((t, X_WIDTH), BF16)),
        grid=(t // tm,),
        in_specs=[pl.BlockSpec((tm, HEADS_W), row), pl.BlockSpec((tm, HEADS_W), row),
                  pl.BlockSpec((tm, S5_WIDTH), row), pl.BlockSpec((tm, D_MODEL), row),
                  pl.BlockSpec((HEADS_W, D_MODEL), fix), pl.BlockSpec((HEADS_W, D_MODEL), fix),
                  pl.BlockSpec((S5_WIDTH, D_MODEL), fix), pl.BlockSpec((1, D_MODEL), fix),
                  pl.BlockSpec((D_MODEL, X_WIDTH), fix)],
        out_specs=(pl.BlockSpec((tm, D_MODEL), row), pl.BlockSpec((tm, X_WIDTH), row)),
        compiler_params=pltpu.CompilerParams(dimension_semantics=("parallel",),
                                             vmem_limit_bytes=VMEM_LIMIT),
        name="out_proj",
    )(dn, ml, s5, x, w1, w2, w3, g, wq)


def _ffn_kernel(x_ref, o_ref, wo_ref, g_ref, w1_ref, w2_ref, gf_ref, y_ref, *, final_norm, ff_chunk):
    x2 = x_ref[...] + _dot(o_ref[...], wo_ref[...])
    xn = _rmsnorm_rows(x2, g_ref[...]).astype(BF16)
    acc = x2
    for c in range(D_FF // ff_chunk):
        hf = _dot(xn, w1_ref[:, c * ff_chunk:(c + 1) * ff_chunk])
        a = jnp.square(jnp.maximum(hf, 0.0)).astype(BF16)
        acc = acc + _dot(a, w2_ref[c * ff_chunk:(c + 1) * ff_chunk, :])
    if final_norm:
        acc = _rmsnorm_rows(acc, gf_ref[...])
    y_ref[...] = acc


def _ffn(x, o, wo, g, w1, w2, gf, tm, final_norm):
    t = x.shape[0]
    row = lambda i: (i, 0)
    fix = lambda i: (0, 0)
    return pl.pallas_call(
        functools.partial(_ffn_kernel, final_norm=final_norm, ff_chunk=1024),
        out_shape=jax.ShapeDtypeStruct((t, D_MODEL), F32),
        grid=(t // tm,),
        in_specs=[pl.BlockSpec((tm, D_MODEL), row), pl.BlockSpec((tm, X_WIDTH), row),
                  pl.BlockSpec((X_WIDTH, D_MODEL), fix), pl.BlockSpec((1, D_MODEL), fix),
                  pl.BlockSpec((D_MODEL, D_FF), fix), pl.BlockSpec((D_FF, D_MODEL), fix),
                  pl.BlockSpec((1, D_MODEL), fix)],
        out_specs=pl.BlockSpec((tm, D_MODEL), row),
        compiler_params=pltpu.CompilerParams(dimension_semantics=("parallel",),
                                             vmem_limit_bytes=VMEM_LIMIT),
        name="ffn",
    )(x, o, wo, g, w1, w2, gf)


def _attn_heads(q, k_ref, v_ref, bi):
    outs = []
    for h in range(X_HEADS):
        sl = slice(h * X_HD, (h + 1) * X_HD)
        kh = k_ref[bi, :, sl].astype(BF16)
        vh = v_ref[bi, :, sl].astype(BF16)
        s = _dot(q[:, sl], kh, _NT) * (X_HD ** -0.5)
        e = jnp.exp(s - jnp.max(s, axis=-1, keepdims=True))
        p = e / jnp.sum(e, axis=-1, keepdims=True)
        outs.append(_dot(p.astype(BF16), vh))
    return jnp.concatenate(outs, axis=-1)


def _attn_seq_kernel(q_ref, k_ref, v_ref, o_ref):
    o_ref[...] = _attn_heads(q_ref[...], k_ref, v_ref, 0).astype(BF16)


def _attn_seq(q, k, v, batch, tq):
    t = q.shape[0]
    nt = t // batch // tq
    return pl.pallas_call(
        _attn_seq_kernel,
        out_shape=jax.ShapeDtypeStruct((t, X_WIDTH), BF16),
        grid=(batch, nt),
        in_specs=[pl.BlockSpec((tq, X_WIDTH), lambda b, i: (b * nt + i, 0)),
                  pl.BlockSpec((1, N_MEM, X_WIDTH), lambda b, i: (b, 0, 0)),
                  pl.BlockSpec((1, N_MEM, X_WIDTH), lambda b, i: (b, 0, 0))],
        out_specs=pl.BlockSpec((tq, X_WIDTH), lambda b, i: (b * nt + i, 0)),
        compiler_params=pltpu.CompilerParams(dimension_semantics=("parallel", "parallel"),
                                             vmem_limit_bytes=VMEM_LIMIT),
        name="attn_seq",
    )(q, k, v)


def _attn_batch_kernel(q_ref, k_ref, v_ref, o_ref, *, bb):
    units = [(bi, h) for bi in range(bb) for h in range(X_HEADS)]
    sl = lambda h: slice(h * X_HD, (h + 1) * X_HD)
    s = [_dot(q_ref[bi][:, sl(h)], k_ref[bi, :, sl(h)].astype(BF16), _NT) * (X_HD ** -0.5) for bi, h in units]
    e = [jnp.exp(x - jnp.max(x, axis=-1, keepdims=True)) for x in s]
    p = [x / jnp.sum(x, axis=-1, keepdims=True) for x in e]
    o = [_dot(x.astype(BF16), v_ref[bi, :, sl(h)].astype(BF16)) for x, (bi, h) in zip(p, units)]
    for bi in range(bb):
        o_ref[bi] = jnp.concatenate(o[bi * X_HEADS:(bi + 1) * X_HEADS], axis=-1).astype(BF16)


def _attn_batch(q, k, v, layer, bb):
    b, l, _ = q.shape
    blk = lambda i: (i, 0, 0)
    kv = pl.BlockSpec((None, bb, N_MEM, X_WIDTH), lambda i: (layer, i, 0, 0))
    return pl.pallas_call(
        functools.partial(_attn_batch_kernel, bb=bb),
        out_shape=jax.ShapeDtypeStruct((b, l, X_WIDTH), BF16),
        grid=(b // bb,),
        in_specs=[pl.BlockSpec((bb, l, X_WIDTH), blk), kv, kv],
        out_specs=pl.BlockSpec((bb, l, X_WIDTH), blk),
        compiler_params=pltpu.CompilerParams(dimension_semantics=("parallel",),
                                             vmem_limit_bytes=VMEM_LIMIT),
        name="attn_batch",
    )(q, k, v)


def _s5_kernel(u_ref, h0r_ref, h0i_ref, lre_ref, lim_ref, ldt_ref, bre_ref, bim_ref, cre_ref, cim_ref,
               d_ref, wg_ref, bg_ref, y_ref, hr_out, hi_out,
               xr_s, xi_s, st_r, st_i, coef_s, *, tb, nb, lane_chunk):
    step = pl.program_id(0)

    @pl.when(step == 0)
    def _():
        lr = lre_ref[...]
        li = lim_ref[...]
        dt = jnp.exp(ldt_ref[...])
        mag = jnp.exp(lr * dt)
        ar = mag * jnp.cos(li * dt)
        ai = mag * jnp.sin(li * dt)
        inv = 1.0 / (lr * lr + li * li)
        coef_s[0:1, :] = ar
        coef_s[1:2, :] = ai
        coef_s[2:3, :] = ((ar - 1.0) * lr + ai * li) * inv
        coef_s[3:4, :] = (ai * lr - (ar - 1.0) * li) * inv
        st_r[...] = h0r_ref[...]
        st_i[...] = h0i_ref[...]

    u = u_ref[...]
    ub = u.astype(BF16)
    bur = _dot(ub, bre_ref[...])
    bui = _dot(ub, bim_ref[...])
    cr = coef_s[2:3, :]
    ci = coef_s[3:4, :]
    xr_s[...] = cr * bur - ci * bui
    xi_s[...] = cr * bui + ci * bur

    for lc in range(S5_STATE // lane_chunk):
        sl = slice(lc * lane_chunk, (lc + 1) * lane_chunk)
        ar = jnp.broadcast_to(coef_s[0:1, sl], (nb, lane_chunk))
        ai = jnp.broadcast_to(coef_s[1:2, sl], (nb, lane_chunk))

        def body(t, carry, sl=sl, ar=ar, ai=ai):
            hr, hi = carry
            r0 = pl.multiple_of(t * nb, 8)
            nhr = ar * hr - ai * hi + xr_s[pl.ds(r0, nb), sl]
            nhi = ar * hi + ai * hr + xi_s[pl.ds(r0, nb), sl]
            xr_s[pl.ds(r0, nb), sl] = nhr
            xi_s[pl.ds(r0, nb), sl] = nhi
            return nhr, nhi

        hr, hi = lax.fori_loop(0, tb, body, (st_r[:, sl], st_i[:, sl]), unroll=min(tb, 8))
        st_r[:, sl] = hr
        st_i[:, sl] = hi

    y = (_dot(xr_s[...].astype(BF16), cre_ref[...]) - _dot(xi_s[...].astype(BF16), cim_ref[...])
         + d_ref[...] * u)
    zg = 0.5 * y * (1.0 + jnp.tanh(math.sqrt(2.0 / math.pi) * (y + 0.044715 * (y * y * y))))
    gate = _sigmoid(_dot(zg.astype(BF16), wg_ref[...]) + bg_ref[...])
    y_ref[...] = (zg * gate).astype(BF16)

    @pl.when(step == pl.num_programs(0) - 1)
    def _():
        hr_out[...] = st_r[...]
        hi_out[...] = st_i[...]


def _s5(u_tm, h0r, h0i, prm, tb, lane_chunk):
    nb = h0r.shape[0]
    l = u_tm.shape[0] // nb
    fix2 = lambda i: (0, 0)
    vec = pl.BlockSpec((1, S5_STATE), fix2)
    st = pl.BlockSpec((nb, S5_STATE), fix2)
    rows = tb * nb
    return pl.pallas_call(
        functools.partial(_s5_kernel, tb=tb, nb=nb, lane_chunk=lane_chunk),
        out_shape=(jax.ShapeDtypeStruct((l * nb, S5_WIDTH), BF16),
                   jax.ShapeDtypeStruct((nb, S5_STATE), F32), jax.ShapeDtypeStruct((nb, S5_STATE), F32)),
        grid=(l // tb,),
        in_specs=[pl.BlockSpec((rows, S5_WIDTH), lambda i: (i, 0)), st, st, vec, vec, vec,
                  pl.BlockSpec((S5_WIDTH, S5_STATE), fix2), pl.BlockSpec((S5_WIDTH, S5_STATE), fix2),
                  pl.BlockSpec((S5_STATE, S5_WIDTH), fix2), pl.BlockSpec((S5_STATE, S5_WIDTH), fix2),
                  pl.BlockSpec((1, S5_WIDTH), fix2), pl.BlockSpec((S5_WIDTH, S5_WIDTH), fix2),
                  pl.BlockSpec((1, S5_WIDTH), fix2)],
        out_specs=(pl.BlockSpec((rows, S5_WIDTH), lambda i: (i, 0)), st, st),
        scratch_shapes=[pltpu.VMEM((rows, S5_STATE), F32), pltpu.VMEM((rows, S5_STATE), F32),
                        pltpu.VMEM((nb, S5_STATE), F32), pltpu.VMEM((nb, S5_STATE), F32),
                        pltpu.VMEM((8, S5_STATE), F32)],
        compiler_params=pltpu.CompilerParams(dimension_semantics=("arbitrary",),
                                             vmem_limit_bytes=VMEM_LIMIT),
        name="s5",
    )(u_tm, h0r, h0i, prm["lre"], prm["lim"], prm["ldt"], prm["bre"], prm["bim"], prm["cre"],
      prm["cim"], prm["d"], prm["wg"], prm["bg"])


def _conv_silu(u, w8, fix):
    out = fix(pltpu.roll(u, 3, axis=0), 3) * w8[0:1, :]
    for s in (2, 1):
        out = out + fix(pltpu.roll(u, s, axis=0), s) * w8[3 - s:4 - s, :]
    out = out + u * w8[3:4, :]
    return out * _sigmoid(out)


def _l2norm_pairs(x_chunks, bones, scale):
    n = len(x_chunks)
    x = jnp.concatenate(x_chunks, axis=0) if n > 1 else x_chunks[0]
    out = [[None] * N_PAIRS for _ in range(n)]
    for p_ in range(N_PAIRS):
        xp = x[:, p_ * LANES:(p_ + 1) * LANES]
        xn = xp * (lax.rsqrt(_seg_mean(xp * xp, bones) * HEAD_DIM + EPS) * scale)
        for c in range(n):
            out[c][p_] = xn[c * CHUNK:(c + 1) * CHUNK]
    return out


def _head_norm_store(vals, gates, normw, bones, act, store):
    n = len(vals)
    for p_ in range(N_PAIRS):
        v = jnp.concatenate([vals[c][p_] for c in range(n)], axis=0) if n > 1 else vals[0][p_]
        g = jnp.concatenate([gates[c][p_] for c in range(n)], axis=0) if n > 1 else gates[0][p_]
        y = v * lax.rsqrt(_seg_mean(v * v, bones) + EPS) * normw * act(g)
        for c in range(n):
            store(c, p_, y[c * CHUNK:(c + 1) * CHUNK].astype(BF16))


def _dn_local(chunks, prm, sub, n_valid, prec):
    n = len(chunks)
    bones = prm["bones"]
    incl, strict = _pair_masks(sub)
    ltri, upair = _ltri(sub), _upair(sub)
    units = [(c, p_) for c in range(n) for p_ in range(N_PAIRS)]

    gates = []
    for ch in chunks:
        sg = _sigmoid(ch["zs"])
        gcomp = -jnp.exp(prm["alog_row"]) * _softplus(ch["zs"])
        g_rows = -jnp.exp(prm["rowalog"]) * _softplus(ch["rows"] + prm["rowadd"])
        if n_valid != sub:
            vc = _valid_col(sub, n_valid)
            sg = jnp.where(vc, sg, 0.0)
            gcomp = jnp.where(vc, gcomp, 0.0)
            g_rows = jnp.where(_valid_row(sub, n_valid), g_rows, 0.0)
        gates.append((sg, _mm_sel_l(ltri, gcomp), _mm_sel_r(g_rows, upair)))

    qn = _l2norm_pairs([ch["q"] for ch in chunks], bones, HEAD_DIM ** -0.5)
    kn = _l2norm_pairs([ch["k"] for ch in chunks], bones, 1.0)

    beta = [_expand(gates[c][0], 2 * p_, 2 * p_ + 1) for c, p_ in units]
    g_col = [_expand(gates[c][1], 6 + 2 * p_, 7 + 2 * p_) for c, p_ in units]
    dec = [jnp.exp(jnp.where(incl, gc - gates[c][2][p_:p_ + 1, :], NEG)) for gc, (c, p_) in zip(g_col, units)]
    kb = [kn[c][p_] * b for b, (c, p_) in zip(beta, units)]
    sc = [_mm(jnp.concatenate([kb_, qn[c][p_]], axis=0), _x2(kn[c][p_]), _NT)
          for kb_, (c, p_) in zip(kb, units)]
    a_in = [s[CHUNK:] * d for s, d in zip(sc, dec)]
    mbd = [_x2(jnp.where(strict, s[:CHUNK] * d, 0.0)) for s, d in zip(sc, dec)]
    t_bd = _tri_inverse_units(mbd, sub, n_valid, prec)
    eg = [jnp.exp(gc) for gc in g_col]
    rhs = [jnp.concatenate([_x2(chunks[c]["v"][:, p_ * LANES:(p_ + 1) * LANES] * b), _x2(kb_ * e)], axis=1)
           for b, kb_, e, (c, p_) in zip(beta, kb, eg, units)]
    sol = [_mm(t, r, prec=prec) for t, r in zip(t_bd, rhs)]
    out = [[None] * N_PAIRS for _ in range(n)]
    for i, (c, p_) in enumerate(units):
        g_last = _last_row_bcast(g_col[i], sub)
        out[c][p_] = dict(u=_fold(sol[i][:, :LANES]), w=_fold(sol[i][:, LANES:]), a_in=a_in[i],
                          qg=qn[c][p_] * eg[i], kd=kn[c][p_] * jnp.exp(g_last - g_col[i]),
                          eg_last=jnp.exp(g_last))
    return out


def _silu(x):
    return x * _sigmoid(x)


def _dn_seq_kernel(zq_ref, zk_ref, zv_ref, zg_ref, zs_ref, rows_ref, cp_ref, cw_ref, s0_ref,
                   add_ref, alog_ref, rowadd_ref, rowalog_ref, nw_ref,
                   o_ref, s_out_ref, s_scr, prev_scr, *, bb, prec):
    tstep = pl.program_id(1)

    @pl.when(tstep == 0)
    def _():
        s_scr[...] = s0_ref[...]
        prev_scr[...] = cp_ref[...]

    r8 = _iota((8, CONV_CH), 0)
    cw = cw_ref[...]
    chunks = []
    for b in range(bb):
        prev = prev_scr[b]

        def fix(rolled, s, prev=prev):
            first = jnp.where(r8 < s, pltpu.roll(prev, s, axis=0), rolled[0:8])
            return jnp.concatenate([first, rolled[8:]], axis=0)

        u_b = jnp.concatenate([zq_ref[b], zk_ref[b], zv_ref[b]], axis=1)
        qkv = _conv_silu(u_b, cw, fix)
        prev_scr[b] = u_b[CHUNK - 8:CHUNK]
        chunks.append(dict(q=qkv[:, 0:HEADS_W], k=qkv[:, HEADS_W:2 * HEADS_W], v=qkv[:, 2 * HEADS_W:],
                           zs=zs_ref[b] + add_ref[...], rows=rows_ref[b, 0]))

    prm = dict(bones=_block_ones(), alog_row=alog_ref[...], rowadd=rowadd_ref[...],
               rowalog=rowalog_ref[...])
    loc = _dn_local(chunks, prm, CHUNK, CHUNK, prec)
    bdm = _bd_mask()
    units = [(b, p_) for b in range(bb) for p_ in range(N_PAIRS)]
    s_prev = [s_scr[b, p_] for b, p_ in units]
    ws = [_mm(jnp.concatenate([loc[b][p_]["w"], loc[b][p_]["qg"]], axis=0), s)
          for s, (b, p_) in zip(s_prev, units)]
    v_new = [loc[b][p_]["u"] - w_[:CHUNK] for w_, (b, p_) in zip(ws, units)]
    o = [w_[CHUNK:] + _mm(loc[b][p_]["a_in"], _x2(v)) for w_, v, (b, p_) in zip(ws, v_new, units)]
    upd = [_mm(loc[b][p_]["kd"].T, v) for v, (b, p_) in zip(v_new, units)]
    for i, (b, p_) in enumerate(units):
        s_scr[b, p_] = loc[b][p_]["eg_last"][0:1, :] * s_prev[i] + jnp.where(bdm, upd[i], 0.0)

    def store(b, p_, val):
        o_ref[b, :, p_ * LANES:(p_ + 1) * LANES] = val

    _head_norm_store([[o[b * N_PAIRS + p_] for p_ in range(N_PAIRS)] for b in range(bb)],
                     [[zg_ref[b, :, p_ * LANES:(p_ + 1) * LANES] for p_ in range(N_PAIRS)] for b in range(bb)],
                     nw_ref[...], prm["bones"], _silu, store)

    @pl.when(tstep == pl.num_programs(1) - 1)
    def _():
        s_out_ref[...] = s_scr[...]


def _dn_batch_kernel(zq_ref, zk_ref, zv_ref, zg_ref, zs_ref, rows_ref, cp_ref, cw_ref, s0_ref,
                     add_ref, alog_ref, rowadd_ref, rowalog_ref, nw_ref,
                     o_ref, s_out_ref, *, nc, n_valid, prec):
    rb = nc * CHUNK
    pall = cp_ref[...]
    rr = _imod(_iota((rb, CONV_CH), 0), SUB)

    def fix(rolled, s):
        return jnp.where(rr < s, pltpu.roll(pall, (rb - SUB + s) % rb, axis=0), rolled)

    u_all = jnp.concatenate([zq_ref[...], zk_ref[...], zv_ref[...]], axis=1)
    qkv = _conv_silu(u_all, cw_ref[...], fix)
    zs_all = zs_ref[...] + add_ref[...]
    chunks = []
    for c in range(nc):
        rs = slice(c * CHUNK, (c + 1) * CHUNK)
        chunks.append(dict(q=qkv[rs, 0:HEADS_W], k=qkv[rs, HEADS_W:2 * HEADS_W], v=qkv[rs, 2 * HEADS_W:],
                           zs=zs_all[rs], rows=rows_ref[c]))

    prm = dict(bones=_block_ones(), alog_row=alog_ref[...], rowadd=rowadd_ref[...],
               rowalog=rowalog_ref[...])
    loc = _dn_local(chunks, prm, SUB, n_valid, prec)
    bdm = _bd_mask()
    nsub = CHUNK // SUB
    lane_sub = _idiv(_iota((LANES, CHUNK), 1), SUB)
    units = [(c, p_) for c in range(nc) for p_ in range(N_PAIRS)]
    subs = [(c, p_, sb) for c, p_ in units for sb in range(nsub)]
    s_prev = {k_: _bd_from_stacked(s0_ref[k_[0] * nsub + k_[2], k_[1]], bdm) for k_ in subs}
    rsl = lambda sb: slice(sb * SUB, (sb + 1) * SUB)
    ws = {k_: _mm(jnp.concatenate([loc[k_[0]][k_[1]]["w"][rsl(k_[2])], loc[k_[0]][k_[1]]["qg"][rsl(k_[2])]],
                                  axis=0), s_prev[k_]) for k_ in subs}
    v_new = {u_: jnp.concatenate([loc[u_[0]][u_[1]]["u"][rsl(sb)] - ws[u_ + (sb,)][:SUB] for sb in range(nsub)],
                                 axis=0) for u_ in units}
    o = {u_: jnp.concatenate([ws[u_ + (sb,)][SUB:] for sb in range(nsub)], axis=0)
         + _mm(loc[u_[0]][u_[1]]["a_in"], _x2(v_new[u_])) for u_ in units}
    kd_t = {u_: loc[u_[0]][u_[1]]["kd"].T for u_ in units}
    for k_ in subs:
        c, p_, sb = k_
        upd = _mm(jnp.where(lane_sub == sb, kd_t[(c, p_)], 0.0), v_new[(c, p_)])
        eg = loc[c][p_]["eg_last"][sb * SUB:sb * SUB + 1, :]
        s_out_ref[c * nsub + sb, p_] = _stacked_from_bd(eg * s_prev[k_] + jnp.where(bdm, upd, 0.0))

    def store(c, p_, val):
        o_ref[c * CHUNK:(c + 1) * CHUNK, p_ * LANES:(p_ + 1) * LANES] = val

    _head_norm_store([[o[(c, p_)] for p_ in range(N_PAIRS)] for c in range(nc)],
                     [[zg_ref[c * CHUNK:(c + 1) * CHUNK, p_ * LANES:(p_ + 1) * LANES] for p_ in range(N_PAIRS)]
                      for c in range(nc)],
                     nw_ref[...], prm["bones"], _silu, store)


def _deltanet_seq(z3, rows4, convp, s0, prm, bb, prec):
    batch, seq_len, _ = z3.shape
    fix2 = lambda g, i: (0, 0)
    zspec = lambda width, col: pl.BlockSpec((bb, CHUNK, width), lambda g, i: (g, i, col))
    s_spec = pl.BlockSpec((bb, N_PAIRS, LANES, LANES), lambda g, i: (g, 0, 0, 0))
    in_specs = [zspec(HEADS_W, 0), zspec(HEADS_W, 1), zspec(HEADS_W, 2), zspec(HEADS_W, 3), zspec(LANES, 26),
                pl.BlockSpec((bb, 1, 16, LANES), lambda g, i: (g, i, 0, 0)),
                pl.BlockSpec((bb, 8, CONV_CH), lambda g, i: (g, 0, 0)),
                pl.BlockSpec((8, CONV_CH), fix2), s_spec,
                pl.BlockSpec((1, LANES), fix2), pl.BlockSpec((1, LANES), fix2),
                pl.BlockSpec((16, LANES), fix2), pl.BlockSpec((16, LANES), fix2),
                pl.BlockSpec((1, LANES), fix2)]
    return pl.pallas_call(
        functools.partial(_dn_seq_kernel, bb=bb, prec=prec),
        out_shape=(jax.ShapeDtypeStruct((batch, seq_len, HEADS_W), BF16), jax.ShapeDtypeStruct(s0.shape, F32)),
        grid=(batch // bb, seq_len // CHUNK),
        in_specs=in_specs,
        out_specs=(pl.BlockSpec((bb, CHUNK, HEADS_W), lambda g, i: (g, i, 0)), s_spec),
        scratch_shapes=[pltpu.VMEM((bb, N_PAIRS, LANES, LANES), F32), pltpu.VMEM((bb, 8, CONV_CH), F32)],
        compiler_params=pltpu.CompilerParams(dimension_semantics=("parallel", "arbitrary"),
                                             vmem_limit_bytes=VMEM_LIMIT),
        name="deltanet_seq",
    )(z3, z3, z3, z3, z3, rows4, convp, prm["conv_w8"], s0, prm["add_row"], prm["alog_row"],
      prm["rowadd"], prm["rowalog"], prm["dn_normw"])


def _deltanet_batch(z, rows, convp, s0, prm, nc, n_valid, prec):
    t = z.shape[0]
    rb = nc * CHUNK
    nsb = rb // SUB
    fix2 = lambda i: (0, 0)
    zspec = lambda width, col: pl.BlockSpec((rb, width), lambda i: (i, col))
    s_spec = pl.BlockSpec((nsb, N_PAIRS, LANES, HEAD_DIM), lambda i: (i, 0, 0, 0))
    in_specs = [zspec(HEADS_W, 0), zspec(HEADS_W, 1), zspec(HEADS_W, 2), zspec(HEADS_W, 3), zspec(LANES, 26),
                pl.BlockSpec((nc, 16, LANES), lambda i: (i, 0, 0)),
                pl.BlockSpec((rb, CONV_CH), lambda i: (i, 0)),
                pl.BlockSpec((8, CONV_CH), fix2), s_spec,
                pl.BlockSpec((1, LANES), fix2), pl.BlockSpec((1, LANES), fix2),
                pl.BlockSpec((16, LANES), fix2), pl.BlockSpec((16, LANES), fix2),
                pl.BlockSpec((1, LANES), fix2)]
    return pl.pallas_call(
        functools.partial(_dn_batch_kernel, nc=nc, n_valid=n_valid, prec=prec),
        out_shape=(jax.ShapeDtypeStruct((t, HEADS_W), BF16), jax.ShapeDtypeStruct(s0.shape, F32)),
        grid=(t // rb,),
        in_specs=in_specs,
        out_specs=(pl.BlockSpec((rb, HEADS_W), lambda i: (i, 0)), s_spec),
        compiler_params=pltpu.CompilerParams(dimension_semantics=("parallel",),
                                             vmem_limit_bytes=VMEM_LIMIT),
        name="deltanet_batch",
    )(z, z, z, z, z, rows, convp, prm["conv_w8"], s0, prm["add_row"], prm["alog_row"],
      prm["rowadd"], prm["rowalog"], prm["dn_normw"])


def _ml_units(chunks, prm, sub, n_valid):
    n = len(chunks)
    incl, _ = _pair_masks(sub)
    lo = _lane_lo((CHUNK, LANES))
    ltri, upair = _ltri(sub), _upair(sub)
    ones2 = prm["bones"]
    bdm2 = jnp.concatenate([_bd_mask(), _bd_mask()], axis=1)
    nsub = CHUNK // sub
    units = [(c, p_) for c in range(n) for p_ in range(N_PAIRS)]
    pair = lambda x, p_: x[:, p_ * LANES:(p_ + 1) * LANES]

    comp = []
    for ch in chunks:
        rows = ch["rows"] + prm["rowadd"]
        ig_c, ls_c = ch["zs"], -_softplus(-ch["zs"])
        ig_r, ls_r = rows, -_softplus(-rows)
        if n_valid != sub:
            vc, vr = _valid_col(sub, n_valid), _valid_row(sub, n_valid)
            ig_c, ls_c = jnp.where(vc, ig_c, NEG), jnp.where(vc, ls_c, 0.0)
            ig_r, ls_r = jnp.where(vr, ig_r, NEG), jnp.where(vr, ls_r, 0.0)
        comp.append(dict(ig_c=ig_c, b_c=_mm_sel_l(ltri, ls_c), ig_r=ig_r, b_r=_mm_sel_r(ls_r, upair)))

    qp = [pair(chunks[c]["q"], p_) for c, p_ in units]
    kp = [pair(chunks[c]["k"], p_) * (HEAD_DIM ** -0.5) for c, p_ in units]
    vp = [pair(chunks[c]["v"], p_) for c, p_ in units]
    b_col = [_expand(comp[c]["b_c"], 18 + 2 * p_, 19 + 2 * p_) for c, p_ in units]
    ig_col = [_expand(comp[c]["ig_c"], 12 + 2 * p_, 13 + 2 * p_) for c, p_ in units]
    dl = [jnp.where(incl, bc - comp[c]["b_r"][6 + p_:7 + p_, :] + comp[c]["ig_r"][3 + p_:4 + p_, :], NEG)
          for bc, (c, p_) in zip(b_col, units)]
    mx = [jnp.where(lo, jnp.max(jnp.where(lo, d, NEG), axis=1, keepdims=True),
                    jnp.max(jnp.where(lo, NEG, d), axis=1, keepdims=True)) for d in dl]
    inter = [bc + chunks[c]["ms"][p_] for bc, (c, p_) in zip(b_col, units)]
    mt = [jnp.maximum(a, b) for a, b in zip(inter, mx)]
    s = [_mm(q_, _x2(k_), _NT) * jnp.exp(d - m) for q_, k_, d, m in zip(qp, kp, dl, mt)]
    sv = [_mm(s_, jnp.concatenate([_x2(v_), ones2], axis=1)) for s_, v_ in zip(s, vp)]
    cn_prev = [[chunks[c]["cn"](p_, sb) for sb in range(nsub)] for c, p_ in units]
    qcn = [jnp.concatenate([_mm(q_[sb * sub:(sb + 1) * sub], cn[sb]) for sb in range(nsub)], axis=0)
           if nsub > 1 else _mm(q_, cn[0]) for q_, cn in zip(qp, cn_prev)]
    w_inter = [jnp.exp(a - m) for a, m in zip(inter, mt)]
    h = [(wi * qc[:, :LANES] + sv_[:, :LANES])
         / jnp.maximum(jnp.abs(wi * qc[:, LANES:] + sv_[:, LANES:]), jnp.exp(-m))
         for wi, qc, sv_, m in zip(w_inter, qcn, sv, mt)]
    m_end = [_last_row_bcast(m, sub) for m in mt]
    b_last = [_last_row_bcast(bc, sub) for bc in b_col]
    a_end = [jnp.exp(bl + chunks[c]["ms"][p_] - me) for bl, me, (c, p_) in zip(b_last, m_end, units)]
    kw_t = [(k_ * jnp.exp(bl - bc + ig - me)).T for k_, bl, bc, ig, me in zip(kp, b_last, b_col, ig_col, m_end)]
    v1 = [jnp.concatenate([v_, jnp.ones((CHUNK, LANES), F32)], axis=1) for v_ in vp]
    lane_sub = _idiv(_iota((LANES, CHUNK), 1), sub) if nsub > 1 else None
    out = [[None] * N_PAIRS for _ in range(n)]
    for i, (c, p_) in enumerate(units):
        new_states, new_m = [], []
        for sb in range(nsub):
            lhs = kw_t[i] if nsub == 1 else jnp.where(lane_sub == sb, kw_t[i], 0.0)
            upd = _mm(lhs, v1[i])
            a_row = a_end[i][sb * sub:sb * sub + 1, :]
            a2 = jnp.concatenate([a_row, a_row], axis=1)
            new_states.append(a2 * cn_prev[i][sb] + jnp.where(bdm2, upd, 0.0))
            new_m.append(m_end[i][sb * sub:sb * sub + 1, :])
        out[c][p_] = (h[i], new_states, new_m)
    return out


def _ml_seq_kernel(zq_ref, zk_ref, zv_ref, zo_ref, zs_ref, rows_ref, cn0_ref, m0_ref, add_ref, rowadd_ref,
                   nw_ref, o_ref, cn_out_ref, m_out_ref, cn_scr, m_scr, *, bb):
    tstep = pl.program_id(1)

    @pl.when(tstep == 0)
    def _():
        cn_scr[...] = cn0_ref[...]
        m_scr[...] = m0_ref[...]

    prm = dict(bones=_block_ones(), rowadd=rowadd_ref[...])
    chunks = []
    for b in range(bb):
        chunks.append(dict(q=zq_ref[b], k=zk_ref[b], v=zv_ref[b], zs=zs_ref[b] + add_ref[...],
                           rows=rows_ref[b, 0],
                           ms=[jnp.broadcast_to(m_scr[b, p_:p_ + 1, :], (CHUNK, LANES)) for p_ in range(N_PAIRS)],
                           cn=lambda p_, sb, b=b: cn_scr[b, p_]))
    res = _ml_units(chunks, prm, CHUNK, CHUNK)
    for b in range(bb):
        for p_ in range(N_PAIRS):
            cn_scr[b, p_] = res[b][p_][1][0]
            m_scr[b, p_:p_ + 1, :] = res[b][p_][2][0]

    def store(b, p_, val):
        o_ref[b, :, p_ * LANES:(p_ + 1) * LANES] = val

    _head_norm_store([[res[b][p_][0] for p_ in range(N_PAIRS)] for b in range(bb)],
                     [[zo_ref[b, :, p_ * LANES:(p_ + 1) * LANES] for p_ in range(N_PAIRS)] for b in range(bb)],
                     nw_ref[...], prm["bones"], _sigmoid, store)

    @pl.when(tstep == pl.num_programs(1) - 1)
    def _():
        cn_out_ref[...] = cn_scr[...]
        m_out_ref[...] = m_scr[...]


def _ml_batch_kernel(zq_ref, zk_ref, zv_ref, zo_ref, zs_ref, rows_ref, c0_ref, n0_ref, m0_ref, add_ref,
                     rowadd_ref, nw_ref, o_ref, c_out_ref, n_out_ref, m_out_ref, *, nc, n_valid):
    prm = dict(bones=_block_ones(), rowadd=rowadd_ref[...])
    bdm = _bd_mask()
    zs_all = zs_ref[...] + add_ref[...]
    nsub = CHUNK // SUB

    def cn_of(c, p_, sb):
        i = c * nsub + sb
        return jnp.concatenate([_bd_from_stacked(c0_ref[i, p_], bdm), _bd_from_stacked(n0_ref[i, p_], bdm)],
                               axis=1)

    chunks = []
    for c in range(nc):
        rs = slice(c * CHUNK, (c + 1) * CHUNK)
        ms = [jnp.concatenate([jnp.broadcast_to(m0_ref[c * nsub + sb, p_:p_ + 1, :], (SUB, LANES))
                               for sb in range(nsub)], axis=0) for p_ in range(N_PAIRS)]
        chunks.append(dict(q=zq_ref[rs, :], k=zk_ref[rs, :], v=zv_ref[rs, :], zs=zs_all[rs], rows=rows_ref[c],
                           ms=ms, cn=functools.partial(cn_of, c)))
    res = _ml_units(chunks, prm, SUB, n_valid)
    for c in range(nc):
        for p_ in range(N_PAIRS):
            _, new_states, new_m = res[c][p_]
            for sb in range(nsub):
                i = c * nsub + sb
                c_out_ref[i, p_] = _stacked_from_bd(new_states[sb][:, :LANES])
                n_out_ref[i, p_] = _stacked_from_bd(new_states[sb][:, LANES:])
                m_out_ref[i, p_:p_ + 1, :] = new_m[sb]

    def store(c, p_, val):
        o_ref[c * CHUNK:(c + 1) * CHUNK, p_ * LANES:(p_ + 1) * LANES] = val

    _head_norm_store([[res[c][p_][0] for p_ in range(N_PAIRS)] for c in range(nc)],
                     [[zo_ref[c * CHUNK:(c + 1) * CHUNK, p_ * LANES:(p_ + 1) * LANES] for p_ in range(N_PAIRS)]
                      for c in range(nc)],
                     nw_ref[...], prm["bones"], _sigmoid, store)


def _mlstm_seq(z3, rows4, cn0, m0, prm, bb):
    batch, seq_len, _ = z3.shape
    fix2 = lambda g, i: (0, 0)
    zspec = lambda width, col: pl.BlockSpec((bb, CHUNK, width), lambda g, i: (g, i, col))
    cn_spec = pl.BlockSpec((bb, N_PAIRS, LANES, 2 * LANES), lambda g, i: (g, 0, 0, 0))
    m_spec = pl.BlockSpec((bb, 8, LANES), lambda g, i: (g, 0, 0))
    in_specs = [zspec(HEADS_W, 4), zspec(HEADS_W, 5), zspec(HEADS_W, 6), zspec(HEADS_W, 7), zspec(LANES, 26),
                pl.BlockSpec((bb, 1, 16, LANES), lambda g, i: (g, i, 0, 0)), cn_spec, m_spec,
                pl.BlockSpec((1, LANES), fix2), pl.BlockSpec((16, LANES), fix2), pl.BlockSpec((1, LANES), fix2)]
    return pl.pallas_call(
        functools.partial(_ml_seq_kernel, bb=bb),
        out_shape=(jax.ShapeDtypeStruct((batch, seq_len, HEADS_W), BF16), jax.ShapeDtypeStruct(cn0.shape, F32),
                   jax.ShapeDtypeStruct(m0.shape, F32)),
        grid=(batch // bb, seq_len // CHUNK),
        in_specs=in_specs,
        out_specs=(pl.BlockSpec((bb, CHUNK, HEADS_W), lambda g, i: (g, i, 0)), cn_spec, m_spec),
        scratch_shapes=[pltpu.VMEM((bb, N_PAIRS, LANES, 2 * LANES), F32), pltpu.VMEM((bb, 8, LANES), F32)],
        compiler_params=pltpu.CompilerParams(dimension_semantics=("parallel", "arbitrary"),
                                             vmem_limit_bytes=VMEM_LIMIT),
        name="mlstm_seq",
    )(z3, z3, z3, z3, z3, rows4, cn0, m0, prm["add_row"], prm["rowadd"], prm["ml_normw"])


def _mlstm_batch(z, rows, c0, n0, m0, prm, nc, n_valid):
    t = z.shape[0]
    rb = nc * CHUNK
    nsb = rb // SUB
    fix2 = lambda i: (0, 0)
    zspec = lambda width, col: pl.BlockSpec((rb, width), lambda i: (i, col))
    st_spec = pl.BlockSpec((nsb, N_PAIRS, LANES, HEAD_DIM), lambda i: (i, 0, 0, 0))
    m_spec = pl.BlockSpec((nsb, 8, LANES), lambda i: (i, 0, 0))
    in_specs = [zspec(HEADS_W, 4), zspec(HEADS_W, 5), zspec(HEADS_W, 6), zspec(HEADS_W, 7), zspec(LANES, 26),
                pl.BlockSpec((nc, 16, LANES), lambda i: (i, 0, 0)), st_spec, st_spec, m_spec,
                pl.BlockSpec((1, LANES), fix2), pl.BlockSpec((16, LANES), fix2), pl.BlockSpec((1, LANES), fix2)]
    return pl.pallas_call(
        functools.partial(_ml_batch_kernel, nc=nc, n_valid=n_valid),
        out_shape=(jax.ShapeDtypeStruct((t, HEADS_W), BF16), jax.ShapeDtypeStruct(c0.shape, F32),
                   jax.ShapeDtypeStruct(n0.shape, F32), jax.ShapeDtypeStruct(m0.shape, F32)),
        grid=(t // rb,),
        in_specs=in_specs,
        out_specs=(pl.BlockSpec((rb, HEADS_W), lambda i: (i, 0)), st_spec, st_spec, m_spec),
        compiler_params=pltpu.CompilerParams(dimension_semantics=("parallel",),
                                             vmem_limit_bytes=VMEM_LIMIT),
        name="mlstm_batch",
    )(z, z, z, z, z, rows, c0, n0, m0, prm["add_row"], prm["rowadd"], prm["ml_normw"])


def _pack_w_in(w_in):
    offs = np.concatenate([[0], np.cumsum(IN_SIZES)])
    seg = lambda i: w_in[:, int(offs[i]):int(offs[i + 1])]
    small = jnp.concatenate([seg(3), seg(4), seg(9), seg(10)], axis=1)
    small = jnp.pad(small, ((0, 0), (0, LANES - 4 * N_HEADS)))
    cols = [seg(0), seg(1), seg(2), seg(5), seg(6), seg(7), seg(8), seg(11), seg(12), small]
    return jnp.concatenate(cols, axis=1).astype(BF16)


def _pair_rows(v6):
    return jnp.repeat(v6.reshape(N_PAIRS, 2), HEAD_DIM, axis=1).reshape(N_PAIRS, LANES)


def _layer_params(l, a):
    f = lambda name: a[name][l].astype(F32)
    zeros6 = jnp.zeros((N_HEADS,), F32)
    add_row = jnp.concatenate([zeros6, f("dn_dt_bias"), f("ml_i_bias"), f("ml_f_bias")])
    add_row = jnp.pad(add_row, (0, LANES - 4 * N_HEADS))[None, :]
    alog_row = jnp.pad(jnp.concatenate([zeros6, f("dn_a_log")]), (0, LANES - 2 * N_HEADS))[None, :]
    rowadd = jnp.concatenate([_pair_rows(f("dn_dt_bias")), _pair_rows(f("ml_i_bias")),
                              _pair_rows(f("ml_f_bias")), jnp.zeros((7, LANES), F32)], axis=0)
    rowalog = jnp.concatenate([_pair_rows(f("dn_a_log")), jnp.zeros((13, LANES), F32)], axis=0)
    eye_g = jnp.eye(S5_GROUPS, dtype=F32)
    bd_in = lambda b: jnp.einsum("gpc,gh->gchp", b, eye_g).reshape(S5_WIDTH, S5_STATE).astype(BF16)
    bd_out = lambda c: jnp.einsum("gcp,gh->gphc", c, eye_g).reshape(S5_STATE, S5_WIDTH).astype(BF16)
    w_out = f("w_out").astype(BF16)
    return dict(
        norm_mix=f("norm_mix")[None, :], w_in=_pack_w_in(f("w_in")),
        add_row=add_row, alog_row=alog_row, rowadd=rowadd, rowalog=rowalog,
        conv_w8=jnp.pad(f("dn_conv_w"), ((0, 8 - CONV_W), (0, 0))),
        dn_normw=jnp.tile(f("dn_norm"), 2)[None, :], ml_normw=jnp.tile(f("ml_norm"), 2)[None, :],
        s5=dict(lre=f("s5_lam_re").reshape(1, S5_STATE), lim=f("s5_lam_im").reshape(1, S5_STATE),
                ldt=jnp.repeat(f("s5_log_dt"), S5_P)[None, :],
                bre=bd_in(f("s5_b_re")), bim=bd_in(f("s5_b_im")),
                cre=bd_out(f("s5_c_re")), cim=bd_out(f("s5_c_im")),
                d=f("s5_d").reshape(1, S5_WIDTH), wg=f("s5_w_glu").astype(BF16),
                bg=f("s5_b_glu")[None, :]),
        w_out1=w_out[:HEADS_W], w_out2=w_out[HEADS_W:2 * HEADS_W], w_out3=w_out[2 * HEADS_W:],
        norm_x=f("norm_x")[None, :], w_xq=f("w_xq").astype(BF16), w_xo=f("w_xo").astype(BF16),
        norm_ff=f("norm_ff")[None, :], w_ff1=f("w_ff1").astype(BF16), w_ff2=f("w_ff2").astype(BF16),
    )


def _row_form(z, n_chunks):
    zs = z[:, Z_COLS - LANES:Z_COLS - LANES + 4 * N_HEADS].reshape(n_chunks, CHUNK, 4, N_PAIRS, 2)
    r = jnp.transpose(zs[:, :, 1:], (0, 2, 3, 4, 1)).reshape(n_chunks, 9, LANES)
    return jnp.pad(r, ((0, 0), (0, 7), (0, 0)))


def _un_bd_pairs(s):
    n = s.shape[0]
    a = s[:, :, :HEAD_DIM, :HEAD_DIM]
    b = s[:, :, HEAD_DIM:, HEAD_DIM:]
    return jnp.stack([a, b], axis=2).reshape(n, N_HEADS, HEAD_DIM, HEAD_DIM)


def _pack_m(m):
    s = m.shape[0]
    r = jnp.repeat(m.reshape(s, N_PAIRS, 2), HEAD_DIM, axis=2).reshape(s, N_PAIRS, LANES)
    return jnp.pad(r, ((0, 0), (0, 8 - N_PAIRS), (0, 0)))


def _unpack_m(mp):
    return mp[:, :N_PAIRS, ::HEAD_DIM].reshape(mp.shape[0], N_HEADS)


ROW_TILE = 512
SEQ_BATCH = 4
S5_STEPS = 64
SAMPLE_CHUNKS = 2
ATTN_BATCH = 8


def _tile(n, pref):
    tile = min(n, pref)
    assert n % tile == 0, (n, pref)
    return tile


def _layer_prompt(x, mem_k, mem_v, prm, batch, seq_len, final_g):
    t = batch * seq_len
    tm = _tile(t, ROW_TILE)
    bb = _tile(batch, SEQ_BATCH)
    z = _norm_matmul(x, prm["norm_mix"], prm["w_in"], tm)
    z3 = z.reshape(batch, seq_len, Z_COLS)
    rows4 = _row_form(z, t // CHUNK).reshape(batch, seq_len // CHUNK, 16, LANES)
    new_conv = z3[:, seq_len - 3:, :CONV_CH]

    dn, s_new = _deltanet_seq(z3, rows4, jnp.zeros((batch, 8, CONV_CH), F32),
                              jnp.zeros((batch, N_PAIRS, LANES, LANES), F32), prm, bb, 3)
    ml, cn_new, m_new = _mlstm_seq(z3, rows4, jnp.zeros((batch, N_PAIRS, LANES, 2 * LANES), F32),
                                   jnp.zeros((batch, 8, LANES), F32), prm, bb)
    u_tm = jnp.transpose(z3[:, :, 3072:3072 + S5_WIDTH], (1, 0, 2))
    zero_h = jnp.zeros((batch, S5_STATE), F32)
    y_tm, hr, hi = _s5(u_tm.reshape(t, S5_WIDTH), zero_h, zero_h, prm["s5"], _tile(seq_len, S5_STEPS),
                       S5_STATE)
    s5o = jnp.transpose(y_tm.reshape(seq_len, batch, S5_WIDTH), (1, 0, 2)).reshape(t, S5_WIDTH)

    x1, q = _out_proj(dn.reshape(t, HEADS_W), ml.reshape(t, HEADS_W), s5o, x, prm["w_out1"], prm["w_out2"],
                      prm["w_out3"], prm["norm_x"], prm["w_xq"], tm)
    att = _attn_seq(q, mem_k, mem_v, batch, _tile(seq_len, ROW_TILE))
    g_fin = final_g if final_g is not None else prm["norm_ff"]
    x3 = _ffn(x1, att, prm["w_xo"], prm["norm_ff"], prm["w_ff1"], prm["w_ff2"], g_fin, tm,
              final_g is not None)
    st = (new_conv, _un_bd_pairs(s_new), _un_bd_pairs(cn_new[..., :LANES]),
          _un_bd_pairs(cn_new[..., LANES:])[..., 0], _unpack_m(m_new),
          hr.reshape(batch, S5_GROUPS, S5_P), hi.reshape(batch, S5_GROUPS, S5_P))
    return x3, st


def _layer_sample(x, cache_k, cache_v, layer, conv_buf, dn_s, ml_c, ml_n, ml_m, ssm_re, ssm_im, prm, batch,
                  seq_len, final_g):
    t = batch * seq_len
    z = _norm_matmul(x, prm["norm_mix"], prm["w_in"], t)
    z3 = z.reshape(batch, seq_len, Z_COLS)
    new_conv = jnp.concatenate([conv_buf, z3[:, :, :CONV_CH]], axis=1)[:, seq_len:]
    zp = jnp.pad(z3, ((0, 0), (0, SUB - seq_len), (0, 0))).reshape(batch * SUB, Z_COLS)
    n_chunks = batch * SUB // CHUNK
    nc = _tile(n_chunks, SAMPLE_CHUNKS)
    rows = _row_form(zp, n_chunks)
    convp = jnp.pad(conv_buf, ((0, 0), (SUB - (CONV_W - 1), 0), (0, 0))).reshape(batch * SUB, CONV_CH)
    stacked = lambda s: s.reshape(batch, N_PAIRS, LANES, HEAD_DIM)
    unstack = lambda s: s.reshape(batch, N_HEADS, HEAD_DIM, HEAD_DIM)

    dn, s_new = _deltanet_batch(zp, rows, convp, stacked(dn_s), prm, nc, seq_len, 3)
    n_b = jnp.broadcast_to(ml_n[..., None], ml_c.shape)
    ml, c_new, n_new, m_new = _mlstm_batch(zp, rows, stacked(ml_c), stacked(n_b), _pack_m(ml_m), prm, nc,
                                           seq_len)
    unpad = lambda a: a.reshape(batch, SUB, -1)[:, :seq_len].reshape(t, -1)
    dn, ml = unpad(dn), unpad(ml)

    u_tm = jnp.transpose(z3[:, :, 3072:3072 + S5_WIDTH], (1, 0, 2))
    y_tm, hr, hi = _s5(u_tm.reshape(t, S5_WIDTH), ssm_re.reshape(batch, S5_STATE),
                       ssm_im.reshape(batch, S5_STATE), prm["s5"], seq_len, LANES)
    s5o = jnp.transpose(y_tm.reshape(seq_len, batch, S5_WIDTH), (1, 0, 2)).reshape(t, S5_WIDTH)

    x1, q = _out_proj(dn, ml, s5o, x, prm["w_out1"], prm["w_out2"], prm["w_out3"], prm["norm_x"],
                      prm["w_xq"], t)
    q3 = jnp.pad(q.reshape(batch, seq_len, X_WIDTH), ((0, 0), (0, SUB - seq_len), (0, 0)))
    att = _attn_batch(q3, cache_k, cache_v, layer, _tile(batch, ATTN_BATCH))[:, :seq_len].reshape(t, X_WIDTH)
    g_fin = final_g if final_g is not None else prm["norm_ff"]
    x3 = _ffn(x1, att, prm["w_xo"], prm["norm_ff"], prm["w_ff1"], prm["w_ff2"], g_fin, t,
              final_g is not None)
    st = (new_conv, unstack(s_new), unstack(c_new), unstack(n_new)[..., 0], _unpack_m(m_new),
          hr.reshape(batch, S5_GROUPS, S5_P), hi.reshape(batch, S5_GROUPS, S5_P))
    return x3, st


def kernel(x_prompt, x_sample, mem_prompt, state_dn_conv, state_dn_s, state_ml_c, state_ml_n, state_ml_m, state_ssm_re, state_ssm_im, cache_mem_k, cache_mem_v, norm_mix, w_in, dn_conv_w, dn_a_log, dn_dt_bias, dn_norm, ml_i_bias, ml_f_bias, ml_norm, s5_lam_re, s5_lam_im, s5_log_dt, s5_b_re, s5_b_im, s5_c_re, s5_c_im, s5_d, s5_w_glu, s5_b_glu, w_out, norm_x, norm_mem, w_xq, w_xk, w_xv, w_xo, norm_ff, w_ff1, w_ff2, norm_final):
    a = dict(norm_mix=norm_mix, w_in=w_in, dn_conv_w=dn_conv_w, dn_a_log=dn_a_log, dn_dt_bias=dn_dt_bias,
             dn_norm=dn_norm, ml_i_bias=ml_i_bias, ml_f_bias=ml_f_bias, ml_norm=ml_norm,
             s5_lam_re=s5_lam_re, s5_lam_im=s5_lam_im, s5_log_dt=s5_log_dt, s5_b_re=s5_b_re,
             s5_b_im=s5_b_im, s5_c_re=s5_c_re, s5_c_im=s5_c_im, s5_d=s5_d, s5_w_glu=s5_w_glu,
             s5_b_glu=s5_b_glu, w_out=w_out, norm_x=norm_x, w_xq=w_xq, w_xo=w_xo, norm_ff=norm_ff,
             w_ff1=w_ff1, w_ff2=w_ff2)
    prms = [_layer_params(l, a) for l in range(DEPTH)]
    g_final = norm_final.astype(F32)[None, :]

    bp, lp, _ = x_prompt.shape
    bs, ls, _ = x_sample.shape

    xp = x_prompt.reshape(bp * lp, D_MODEL)
    mem = mem_prompt.reshape(bp * N_MEM, D_MODEL)
    p_st = []
    for l in range(DEPTH):
        tmem = _tile(bp * N_MEM, ROW_TILE)
        mk = _norm_matmul(mem, norm_mem[l][None, :], w_xk[l].astype(BF16), tmem).reshape(bp, N_MEM, X_WIDTH)
        mv = _norm_matmul(mem, norm_mem[l][None, :], w_xv[l].astype(BF16), tmem).reshape(bp, N_MEM, X_WIDTH)
        xp, st = _layer_prompt(xp, mk, mv, prms[l], bp, lp, g_final if l == DEPTH - 1 else None)
        p_st.append(st + (mk.reshape(bp, N_MEM, X_HEADS, X_HD), mv.reshape(bp, N_MEM, X_HEADS, X_HD)))
    y_prompt = xp.reshape(bp, lp, D_MODEL)

    xs = x_sample.reshape(bs * ls, D_MODEL)
    cache_k = cache_mem_k.reshape(DEPTH, bs, N_MEM, X_WIDTH)
    cache_v = cache_mem_v.reshape(DEPTH, bs, N_MEM, X_WIDTH)
    s_st = []
    for l in range(DEPTH):
        xs, st = _layer_sample(xs, cache_k, cache_v, l, state_dn_conv[l], state_dn_s[l], state_ml_c[l],
                               state_ml_n[l], state_ml_m[l], state_ssm_re[l], state_ssm_im[l], prms[l], bs, ls,
                               g_final if l == DEPTH - 1 else None)
        s_st.append(st)
    y_sample = xs.reshape(bs, ls, D_MODEL)

    stack = lambda sts, i: jnp.stack([s[i] for s in sts])
    return (y_prompt, y_sample) + tuple(stack(p_st, i) for i in range(9)) + tuple(stack(s_st, i) for i in range(7))
```

```python
import functools
import math

import numpy as np
import jax
import jax.numpy as jnp
from jax import lax
from jax.experimental import pallas as pl
from jax.experimental.pallas import tpu as pltpu

F32 = jnp.float32
BF16 = jnp.bfloat16

D_MODEL = 1024
DEPTH = 2
N_HEADS = 6
HEAD_DIM = 64
N_PAIRS = N_HEADS // 2
HEADS_W = N_HEADS * HEAD_DIM
CONV_W = 4
CONV_CH = 3 * HEADS_W
S5_WIDTH = 256
S5_GROUP = 16
S5_GROUPS = 16
S5_P = 64
S5_STATE = S5_GROUPS * S5_P
N_MEM = 256
X_HEADS = 4
X_HD = 128
X_WIDTH = X_HEADS * X_HD
D_FF = 4 * D_MODEL
EPS = 1e-6
IN_SIZES = (384, 384, 384, 6, 6, 384, 384, 384, 384, 6, 6, 384, 256)

LANES = 128
CHUNK = 64
SUB = 8
NEG = -1e30
Z_COLS = 3456
VMEM_LIMIT = 52 * 1024 * 1024

_NN = (((1,), (0,)), ((), ()))
_NT = (((1,), (1,)), ((), ()))


def _dot(a, b, dims=_NN):
    return lax.dot_general(a, b, dims, preferred_element_type=F32)


def _split(a, n):
    out = []
    r = a
    for i in range(n):
        h = r.astype(BF16)
        out.append(h)
        if i + 1 < n:
            r = r - h.astype(F32)
    return out


def _mm(a, b, dims=_NN, prec=1):
    if prec == 1:
        return _dot(a.astype(BF16), b.astype(BF16), dims)
    a1, a2 = _split(a, 2)
    b1, b2 = _split(b, 2)
    return (_dot(a1, b2, dims) + _dot(a2, b1, dims)) + _dot(a1, b1, dims)


def _mm_sel_r(a, sel, n=3):
    parts = _split(a, n)
    acc = _dot(parts[-1], sel)
    for p_ in parts[-2::-1]:
        acc = acc + _dot(p_, sel)
    return acc


def _mm_sel_l(sel, a, n=3):
    parts = _split(a, n)
    acc = _dot(sel, parts[-1])
    for p_ in parts[-2::-1]:
        acc = acc + _dot(sel, p_)
    return acc


def _sigmoid(x):
    return 1.0 / (1.0 + jnp.exp(-x))


def _softplus(x):
    return jnp.maximum(x, 0.0) + jnp.log(1.0 + jnp.exp(-jnp.abs(x)))


def _rmsnorm_rows(x, g):
    return x * lax.rsqrt(jnp.mean(x * x, axis=-1, keepdims=True) + EPS) * g


def _iota(shape, dim):
    return lax.broadcasted_iota(jnp.int32, shape, dim)


def _idiv(x, n):
    return x >> (n.bit_length() - 1)


def _imod(x, n):
    return x & (n - 1)


def _lane_lo(shape):
    return _iota(shape, 1) < HEAD_DIM


def _x2(x):
    lo = _lane_lo(x.shape)
    return jnp.concatenate([jnp.where(lo, x, 0.0), jnp.where(lo, 0.0, x)], axis=0)


def _fold(y):
    r = y.shape[0] // 2
    return y[:r] + y[r:]


def _expand(a, c0, c1):
    r = a.shape[0]
    return jnp.where(_lane_lo((r, LANES)), a[:, c0:c0 + 1], a[:, c1:c1 + 1])


def _block_ones():
    i = _iota((LANES, LANES), 0)
    j = _iota((LANES, LANES), 1)
    return jnp.where((i < HEAD_DIM) == (j < HEAD_DIM), 1.0, 0.0).astype(BF16)


def _seg_mean(x, bones):
    return _mm_sel_r(x, bones, n=2) * (1.0 / HEAD_DIM)


def _ltri(sub):
    i = _iota((CHUNK, CHUNK), 0)
    j = _iota((CHUNK, CHUNK), 1)
    ok = (j <= i) & (_idiv(i, sub) == _idiv(j, sub)) if sub != CHUNK else (j <= i)
    return jnp.where(ok, 1.0, 0.0).astype(BF16)


def _upair(sub):
    a = _iota((LANES, LANES), 0)
    b = _iota((LANES, LANES), 1)
    ok = ((a < HEAD_DIM) == (b < HEAD_DIM)) & (a <= b)
    if sub != CHUNK:
        ok = ok & (_idiv(_imod(a, HEAD_DIM), sub) == _idiv(_imod(b, HEAD_DIM), sub))
    return jnp.where(ok, 1.0, 0.0).astype(BF16)


def _pair_masks(sub):
    i = _iota((CHUNK, LANES), 0)
    j = _imod(_iota((CHUNK, LANES), 1), HEAD_DIM)
    same = (_idiv(i, sub) == _idiv(j, sub)) if sub != CHUNK else (i >= 0)
    return same & (j <= i), same & (j < i)


def _valid_col(sub, n_valid):
    i = _iota((CHUNK, 1), 0)
    return _imod(i, sub) < n_valid


def _valid_row(sub, n_valid):
    j = _iota((1, LANES), 1)
    return _imod(j, sub) < n_valid


def _bd_mask():
    i = _iota((LANES, LANES), 0)
    j = _iota((LANES, LANES), 1)
    return (i < HEAD_DIM) == (j < HEAD_DIM)


def _parts(a, prec):
    return tuple(_split(a, 1 if prec == 1 else 2))


def _mmp(ap, bp, dims=_NN):
    acc = None
    if len(bp) > 1:
        acc = _dot(ap[0], bp[1], dims)
    if len(ap) > 1:
        cross = _dot(ap[1], bp[0], dims)
        acc = cross if acc is None else acc + cross
    lead = _dot(ap[0], bp[0], dims)
    return lead if acc is None else acc + lead


def _blk_eq(n, b):
    return _idiv(_iota((n, n), 0), b) == _idiv(_iota((n, n), 1), b)


TRI_BASE = 8


def _tri_inverse_units(mbds, sub, n_valid, prec, newton):
    n = LANES
    eye = jnp.where(_iota((n, n), 0) == _iota((n, n), 1), 1.0, 0.0)
    if sub == CHUNK:
        base, n_sq = TRI_BASE, TRI_BASE.bit_length() - 2
        blk = _blk_eq(n, base)
        m0 = [jnp.where(blk, m, 0.0) for m in mbds]
    else:
        base, n_sq = sub, max(0, math.ceil(math.log2(n_valid)) - 1)
        m0 = mbds
    ts = [eye - m for m in m0]
    if n_sq > 0:
        ps = [_mm(m, m, prec=prec) for m in m0]
        for it in range(n_sq):
            if it == n_sq - 1:
                ts = [t + _mm(t, p_, prec=prec) for t, p_ in zip(ts, ps)]
            else:
                both = [_mm(jnp.concatenate([t, p_], axis=0), p_, prec=prec) for t, p_ in zip(ts, ps)]
                ts = [t + b_[:n] for t, b_ in zip(ts, both)]
                ps = [b_[n:] for b_ in both]
    b = base
    while b < sub:
        outer = _blk_eq(n, 2 * b) & jnp.logical_not(_blk_eq(n, b))
        tp = [_parts(t, prec) for t in ts]
        xs = [_mmp(_parts(jnp.where(outer, m, 0.0), prec), t_) for m, t_ in zip(mbds, tp)]
        ts = [t - _mmp(t_, _parts(x, prec)) for t, t_, x in zip(ts, tp, xs)]
        b *= 2
    if newton:
        res = [eye - (t + _mm(m, t, prec=3)) for m, t in zip(mbds, ts)]
        ts = [t + _mm(t, r) for t, r in zip(ts, res)]
    return ts


def _last_row_bcast(a, sub):
    if sub == CHUNK:
        return jnp.broadcast_to(a[CHUNK - 1:CHUNK, :], a.shape)
    parts = [jnp.broadcast_to(a[s * sub + sub - 1:s * sub + sub, :], (sub, a.shape[1]))
             for s in range(CHUNK // sub)]
    return jnp.concatenate(parts, axis=0)


def _bd_from_stacked(s, bdm):
    return jnp.where(bdm, jnp.concatenate([s, s], axis=1), 0.0)


def _stacked_from_bd(s):
    return s[:, :HEAD_DIM] + s[:, HEAD_DIM:]


SUB_ROWS = 256


def _sub_tiles(tm):
    r = min(tm, SUB_ROWS)
    return [slice(s * r, (s + 1) * r) for s in range(tm // r)]


def _rows3(nt):
    return lambda i: (i // nt, i % nt, 0)


def _dense_call(kernel_fn, name, out_shape, in_specs, out_specs, steps, args):
    return pl.pallas_call(
        kernel_fn, out_shape=out_shape, grid=(steps,), in_specs=in_specs, out_specs=out_specs,
        compiler_params=pltpu.CompilerParams(dimension_semantics=("parallel",), vmem_limit_bytes=VMEM_LIMIT),
        name=name)(*args)


def _in_proj_kernel(x_ref, g_ref, w_ref, o_ref):
    for rs in _sub_tiles(x_ref.shape[1]):
        xn = _rmsnorm_rows(x_ref[0, rs, :], g_ref[...]).astype(BF16)
        o_ref[0, rs, :] = _dot(xn, w_ref[...])


def _norm_matmul(x, g, w, tm):
    b, l, k = x.shape
    n = w.shape[1]
    nt = l // tm
    fix = lambda i: (0, 0)
    return _dense_call(
        _in_proj_kernel, "norm_matmul", jax.ShapeDtypeStruct((b, l, n), F32),
        [pl.BlockSpec((1, tm, k), _rows3(nt)), pl.BlockSpec((1, k), fix), pl.BlockSpec((k, n), fix)],
        pl.BlockSpec((1, tm, n), _rows3(nt)), b * nt, (x, g, w))


def _out_proj_kernel(dn_ref, ml_ref, s5_ref, x_ref, w1_ref, w2_ref, w3_ref, g_ref, wq_ref,
                     x1_ref, q_ref):
    for rs in _sub_tiles(x_ref.shape[1]):
        acc = x_ref[0, rs, :] + _dot(dn_ref[0, rs, :], w1_ref[...])
        acc = acc + _dot(ml_ref[0, rs, :], w2_ref[...])
        acc = acc + _dot(s5_ref[rs, :], w3_ref[...])
        x1_ref[0, rs, :] = acc
        xn = _rmsnorm_rows(acc, g_ref[...]).astype(BF16)
        q_ref[0, rs, :] = _dot(xn, wq_ref[...]).astype(BF16)


def _out_proj(dn, ml, s5, x, w1, w2, w3, g, wq, tm):
    b, l, _ = x.shape
    nt = l // tm
    row = _rows3(nt)
    fix = lambda i: (0, 0)
    return _dense_call(
        _out_proj_kernel, "out_proj",
        (jax.ShapeDtypeStruct((b, l, D_MODEL), F32), jax.ShapeDtypeStruct((b, l, X_WIDTH), BF16)),
        [pl.BlockSpec((1, tm, HEADS_W), row), pl.BlockSpec((1, tm, HEADS_W), row),
         pl.BlockSpec((tm, S5_WIDTH), lambda i: (i % nt, i // nt)), pl.BlockSpec((1, tm, D_MODEL), row),
         pl.BlockSpec((HEADS_W, D_MODEL), fix), pl.BlockSpec((HEADS_W, D_MODEL), fix),
         pl.BlockSpec((S5_WIDTH, D_MODEL), fix), pl.BlockSpec((1, D_MODEL), fix),
         pl.BlockSpec((D_MODEL, X_WIDTH), fix)],
        (pl.BlockSpec((1, tm, D_MODEL), row), pl.BlockSpec((1, tm, X_WIDTH), row)), b * nt,
        (dn, ml, s5, x, w1, w2, w3, g, wq))


def _ffn_kernel(x_ref, o_ref, wo_ref, g_ref, w1_ref, w2_ref, gf_ref, y_ref, *, final_norm, ff_chunk):
    for rs in _sub_tiles(x_ref.shape[1]):
        x2 = x_ref[0, rs, :] + _dot(o_ref[0, rs, :], wo_ref[...])
        xn = _rmsnorm_rows(x2, g_ref[...]).astype(BF16)
        acc = x2
        for c in range(D_FF // ff_chunk):
            hf = _dot(xn, w1_ref[:, c * ff_chunk:(c + 1) * ff_chunk])
            a = jnp.square(jnp.maximum(hf, 0.0)).astype(BF16)
            acc = acc + _dot(a, w2_ref[c * ff_chunk:(c + 1) * ff_chunk, :])
        if final_norm:
            acc = _rmsnorm_rows(acc, gf_ref[...])
        y_ref[0, rs, :] = acc


def _ffn(x, o, wo, g, w1, w2, gf, tm, final_norm):
    b, l, _ = x.shape
    nt = l // tm
    row = _rows3(nt)
    fix = lambda i: (0, 0)
    return _dense_call(
        functools.partial(_ffn_kernel, final_norm=final_norm, ff_chunk=1024), "ffn",
        jax.ShapeDtypeStruct((b, l, D_MODEL), F32),
        [pl.BlockSpec((1, tm, D_MODEL), row), pl.BlockSpec((1, tm, X_WIDTH), row),
         pl.BlockSpec((X_WIDTH, D_MODEL), fix), pl.BlockSpec((1, D_MODEL), fix),
         pl.BlockSpec((D_MODEL, D_FF), fix), pl.BlockSpec((D_FF, D_MODEL), fix),
         pl.BlockSpec((1, D_MODEL), fix)],
        pl.BlockSpec((1, tm, D_MODEL), row), b * nt, (x, o, wo, g, w1, w2, gf))


def _attn_heads(q, k_ref, v_ref, bi):
    outs = []
    for h in range(X_HEADS):
        sl = slice(h * X_HD, (h + 1) * X_HD)
        kh = k_ref[bi, :, sl].astype(BF16)
        vh = v_ref[bi, :, sl].astype(BF16)
        s = _dot(q[:, sl], kh, _NT) * (X_HD ** -0.5)
        e = jnp.exp(s - jnp.max(s, axis=-1, keepdims=True))
        p = e / jnp.sum(e, axis=-1, keepdims=True)
        outs.append(_dot(p.astype(BF16), vh))
    return jnp.concatenate(outs, axis=-1)


def _attn_seq_kernel(q_ref, k_ref, v_ref, o_ref):
    for rs in _sub_tiles(q_ref.shape[1]):
        o_ref[0, rs, :] = _attn_heads(q_ref[0, rs, :], k_ref, v_ref, 0).astype(BF16)


def _attn_seq(q, k, v, tq):
    batch, seq_len, _ = q.shape
    return pl.pallas_call(
        _attn_seq_kernel,
        out_shape=jax.ShapeDtypeStruct((batch, seq_len, X_WIDTH), BF16),
        grid=(batch, seq_len // tq),
        in_specs=[pl.BlockSpec((1, tq, X_WIDTH), lambda b, i: (b, i, 0)),
                  pl.BlockSpec((1, N_MEM, X_WIDTH), lambda b, i: (b, 0, 0)),
                  pl.BlockSpec((1, N_MEM, X_WIDTH), lambda b, i: (b, 0, 0))],
        out_specs=pl.BlockSpec((1, tq, X_WIDTH), lambda b, i: (b, i, 0)),
        compiler_params=pltpu.CompilerParams(dimension_semantics=("parallel", "parallel"),
                                             vmem_limit_bytes=VMEM_LIMIT),
        name="attn_seq",
    )(q, k, v)


def _attn_batch_kernel(q_ref, k_ref, v_ref, o_ref, *, bb):
    units = [(bi, h) for bi in range(bb) for h in range(X_HEADS)]
    sl = lambda h: slice(h * X_HD, (h + 1) * X_HD)
    s = [_dot(q_ref[bi][:, sl(h)], k_ref[bi, :, sl(h)].astype(BF16), _NT) * (X_HD ** -0.5) for bi, h in units]
    e = [jnp.exp(x - jnp.max(x, axis=-1, keepdims=True)) for x in s]
    p = [x / jnp.sum(x, axis=-1, keepdims=True) for x in e]
    o = [_dot(x.astype(BF16), v_ref[bi, :, sl(h)].astype(BF16)) for x, (bi, h) in zip(p, units)]
    for bi in range(bb):
        o_ref[bi] = jnp.concatenate(o[bi * X_HEADS:(bi + 1) * X_HEADS], axis=-1).astype(BF16)


def _attn_batch(q, k, v, layer, bb):
    b, l, _ = q.shape
    blk = lambda i: (i, 0, 0)
    kv = pl.BlockSpec((None, bb, N_MEM, X_WIDTH), lambda i: (layer, i, 0, 0))
    return pl.pallas_call(
        functools.partial(_attn_batch_kernel, bb=bb),
        out_shape=jax.ShapeDtypeStruct((b, l, X_WIDTH), BF16),
        grid=(b // bb,),
        in_specs=[pl.BlockSpec((bb, l, X_WIDTH), blk), kv, kv],
        out_specs=pl.BlockSpec((bb, l, X_WIDTH), blk),
        compiler_params=pltpu.CompilerParams(dimension_semantics=("parallel",),
                                             vmem_limit_bytes=VMEM_LIMIT),
        name="attn_batch",
    )(q, k, v)


def _s5_kernel(u_ref, h0r_ref, h0i_ref, lre_ref, lim_ref, ldt_ref, bre_ref, bim_ref, cre_ref, cim_ref,
               d_ref, wg_ref, bg_ref, y_ref, hr_out, hi_out,
               xr_s, xi_s, st_r, st_i, coef_s, *io_s, tb, nb, lane_chunk, seq_major_io):
    step = pl.program_id(0)
    if seq_major_io:
        u_s, y_s = io_s
        for b in range(nb):
            for j in range(S5_WIDTH // LANES):
                u_s[j, pl.ds(b, tb, stride=nb), :] = u_ref[b, :, j * LANES:(j + 1) * LANES]
        u = jnp.concatenate([u_s[j] for j in range(S5_WIDTH // LANES)], axis=1)
    else:
        u = u_ref[...]

    @pl.when(step == 0)
    def _():
        lr = lre_ref[...]
        li = lim_ref[...]
        dt = jnp.exp(ldt_ref[...])
        mag = jnp.exp(lr * dt)
        ar = mag * jnp.cos(li * dt)
        ai = mag * jnp.sin(li * dt)
        inv = 1.0 / (lr * lr + li * li)
        coef_s[0:1, :] = ar
        coef_s[1:2, :] = ai
        coef_s[2:3, :] = ((ar - 1.0) * lr + ai * li) * inv
        coef_s[3:4, :] = (ai * lr - (ar - 1.0) * li) * inv
        st_r[...] = h0r_ref[...]
        st_i[...] = h0i_ref[...]

    ub = u.astype(BF16)
    bur = _dot(ub, bre_ref[...])
    bui = _dot(ub, bim_ref[...])
    cr = coef_s[2:3, :]
    ci = coef_s[3:4, :]
    xr_s[...] = cr * bur - ci * bui
    xi_s[...] = cr * bui + ci * bur

    for lc in range(S5_STATE // lane_chunk):
        sl = slice(lc * lane_chunk, (lc + 1) * lane_chunk)
        ar = jnp.broadcast_to(coef_s[0:1, sl], (nb, lane_chunk))
        ai = jnp.broadcast_to(coef_s[1:2, sl], (nb, lane_chunk))

        def body(t, carry, sl=sl, ar=ar, ai=ai):
            hr, hi = carry
            r0 = pl.multiple_of(t * nb, 8)
            nhr = ar * hr - ai * hi + xr_s[pl.ds(r0, nb), sl]
            nhi = ar * hi + ai * hr + xi_s[pl.ds(r0, nb), sl]
            xr_s[pl.ds(r0, nb), sl] = nhr
            xi_s[pl.ds(r0, nb), sl] = nhi
            return nhr, nhi

        hr, hi = lax.fori_loop(0, tb, body, (st_r[:, sl], st_i[:, sl]), unroll=min(tb, 8))
        st_r[:, sl] = hr
        st_i[:, sl] = hi

    y = (_dot(xr_s[...].astype(BF16), cre_ref[...]) - _dot(xi_s[...].astype(BF16), cim_ref[...])
         + d_ref[...] * u)
    zg = 0.5 * y * (1.0 + jnp.tanh(math.sqrt(2.0 / math.pi) * (y + 0.044715 * (y * y * y))))
    gate = _sigmoid(_dot(zg.astype(BF16), wg_ref[...]) + bg_ref[...])
    if seq_major_io:
        yv = zg * gate
        for j in range(S5_WIDTH // LANES):
            y_s[j] = yv[:, j * LANES:(j + 1) * LANES]
        for b in range(nb):
            for j in range(S5_WIDTH // LANES):
                c0 = b * S5_WIDTH + j * LANES
                y_ref[:, c0:c0 + LANES] = y_s[j, pl.ds(b, tb, stride=nb), :].astype(BF16)
    else:
        y_ref[...] = (zg * gate).astype(BF16)

    @pl.when(step == pl.num_programs(0) - 1)
    def _():
        hr_out[...] = st_r[...]
        hi_out[...] = st_i[...]


def _s5(u, h0r, h0i, prm, tb, lane_chunk, seq_major_io):
    nb = h0r.shape[0]
    fix2 = lambda i: (0, 0)
    vec = pl.BlockSpec((1, S5_STATE), fix2)
    st = pl.BlockSpec((nb, S5_STATE), fix2)
    rows = tb * nb
    scratch = [pltpu.VMEM((rows, S5_STATE), F32), pltpu.VMEM((rows, S5_STATE), F32),
               pltpu.VMEM((nb, S5_STATE), F32), pltpu.VMEM((nb, S5_STATE), F32),
               pltpu.VMEM((8, S5_STATE), F32)]
    if seq_major_io:
        l = u.shape[1]
        u_spec = pl.BlockSpec((nb, tb, S5_WIDTH), lambda i: (0, i, 3072 // S5_WIDTH))
        y_shape = jax.ShapeDtypeStruct((l, nb * S5_WIDTH), BF16)
        y_spec = pl.BlockSpec((tb, nb * S5_WIDTH), lambda i: (i, 0))
        io_shape = (S5_WIDTH // LANES, rows, LANES)
        scratch += [pltpu.VMEM(io_shape, F32), pltpu.VMEM(io_shape, F32)]
    else:
        l = u.shape[0] // nb
        u_spec = pl.BlockSpec((rows, S5_WIDTH), lambda i: (i, 0))
        y_shape = jax.ShapeDtypeStruct((l * nb, S5_WIDTH), BF16)
        y_spec = pl.BlockSpec((rows, S5_WIDTH), lambda i: (i, 0))
    return pl.pallas_call(
        functools.partial(_s5_kernel, tb=tb, nb=nb, lane_chunk=lane_chunk, seq_major_io=seq_major_io),
        out_shape=(y_shape, jax.ShapeDtypeStruct((nb, S5_STATE), F32), jax.ShapeDtypeStruct((nb, S5_STATE), F32)),
        grid=(l // tb,),
        in_specs=[u_spec, st, st, vec, vec, vec,
                  pl.BlockSpec((S5_WIDTH, S5_STATE), fix2), pl.BlockSpec((S5_WIDTH, S5_STATE), fix2),
                  pl.BlockSpec((S5_STATE, S5_WIDTH), fix2), pl.BlockSpec((S5_STATE, S5_WIDTH), fix2),
                  pl.BlockSpec((1, S5_WIDTH), fix2), pl.BlockSpec((S5_WIDTH, S5_WIDTH), fix2),
                  pl.BlockSpec((1, S5_WIDTH), fix2)],
        out_specs=(y_spec, st, st),
        scratch_shapes=scratch,
        compiler_params=pltpu.CompilerParams(dimension_semantics=("arbitrary",),
                                             vmem_limit_bytes=VMEM_LIMIT),
        name="s5",
    )(u, h0r, h0i, prm["lre"], prm["lim"], prm["ldt"], prm["bre"], prm["bim"], prm["cre"],
      prm["cim"], prm["d"], prm["wg"], prm["bg"])


def _conv_silu(u, w8, fix):
    out = fix(pltpu.roll(u, 3, axis=0), 3) * w8[0:1, :]
    for s in (2, 1):
        out = out + fix(pltpu.roll(u, s, axis=0), s) * w8[3 - s:4 - s, :]
    out = out + u * w8[3:4, :]
    return out * _sigmoid(out)


def _l2norm_pairs(x_chunks, bones, scale):
    n = len(x_chunks)
    x = jnp.concatenate(x_chunks, axis=0) if n > 1 else x_chunks[0]
    out = [[None] * N_PAIRS for _ in range(n)]
    for p_ in range(N_PAIRS):
        xp = x[:, p_ * LANES:(p_ + 1) * LANES]
        xn = xp * (lax.rsqrt(_seg_mean(xp * xp, bones) * HEAD_DIM + EPS) * scale)
        for c in range(n):
            out[c][p_] = xn[c * CHUNK:(c + 1) * CHUNK]
    return out


def _head_norm_store(vals, gates, normw, bones, act, store):
    n = len(vals)
    for p_ in range(N_PAIRS):
        v = jnp.concatenate([vals[c][p_] for c in range(n)], axis=0) if n > 1 else vals[0][p_]
        g = jnp.concatenate([gates[c][p_] for c in range(n)], axis=0) if n > 1 else gates[0][p_]
        y = v * lax.rsqrt(_seg_mean(v * v, bones) + EPS) * normw * act(g)
        for c in range(n):
            store(c, p_, y[c * CHUNK:(c + 1) * CHUNK].astype(BF16))


def _dn_local(chunks, prm, sub, n_valid, prec):
    n = len(chunks)
    bones = prm["bones"]
    incl, strict = _pair_masks(sub)
    ltri, upair = _ltri(sub), _upair(sub)
    units = [(c, p_) for c in range(n) for p_ in range(N_PAIRS)]

    gates = []
    for ch in chunks:
        sg = _sigmoid(ch["zs"])
        gcomp = -jnp.exp(prm["alog_row"]) * _softplus(ch["zs"])
        g_rows = -jnp.exp(prm["rowalog"]) * _softplus(ch["rows"] + prm["rowadd"])
        if n_valid != sub:
            vc = _valid_col(sub, n_valid)
            sg = jnp.where(vc, sg, 0.0)
            gcomp = jnp.where(vc, gcomp, 0.0)
            g_rows = jnp.where(_valid_row(sub, n_valid), g_rows, 0.0)
        gates.append((sg, _mm_sel_l(ltri, gcomp), _mm_sel_r(g_rows, upair)))

    qn = _l2norm_pairs([ch["q"] for ch in chunks], bones, HEAD_DIM ** -0.5)
    kn = _l2norm_pairs([ch["k"] for ch in chunks], bones, 1.0)

    beta = [_expand(gates[c][0], 2 * p_, 2 * p_ + 1) for c, p_ in units]
    g_col = [_expand(gates[c][1], 6 + 2 * p_, 7 + 2 * p_) for c, p_ in units]
    dec = [jnp.exp(jnp.where(incl, gc - gates[c][2][p_:p_ + 1, :], NEG)) for gc, (c, p_) in zip(g_col, units)]
    kb = [kn[c][p_] * b for b, (c, p_) in zip(beta, units)]
    sc = [_mm(jnp.concatenate([kb_, qn[c][p_]], axis=0), _x2(kn[c][p_]), _NT)
          for kb_, (c, p_) in zip(kb, units)]
    a_in = [s[CHUNK:] * d for s, d in zip(sc, dec)]
    mbd = [_x2(jnp.where(strict, s[:CHUNK] * d, 0.0)) for s, d in zip(sc, dec)]
    tri_prec, newton, sol_prec = prec
    t_bd = _tri_inverse_units(mbd, sub, n_valid, tri_prec, newton)
    eg = [jnp.exp(gc) for gc in g_col]
    rhs = [jnp.concatenate([_x2(chunks[c]["v"][:, p_ * LANES:(p_ + 1) * LANES] * b), _x2(kb_ * e)], axis=1)
           for b, kb_, e, (c, p_) in zip(beta, kb, eg, units)]
    sol = [_mm(t, r, prec=sol_prec) for t, r in zip(t_bd, rhs)]
    out = [[None] * N_PAIRS for _ in range(n)]
    for i, (c, p_) in enumerate(units):
        g_last = _last_row_bcast(g_col[i], sub)
        out[c][p_] = dict(u=_fold(sol[i][:, :LANES]), w=_fold(sol[i][:, LANES:]), a_in=a_in[i],
                          qg=qn[c][p_] * eg[i], kd=kn[c][p_] * jnp.exp(g_last - g_col[i]),
                          eg_last=jnp.exp(g_last))
    return out


def _silu(x):
    return x * _sigmoid(x)


def _dn_seq_kernel(zq_ref, zk_ref, zv_ref, zg_ref, zs_ref, rows_ref, cp_ref, cw_ref, s0_ref,
                   add_ref, alog_ref, rowadd_ref, rowalog_ref, nw_ref,
                   o_ref, s_out_ref, s_scr, prev_scr, *, bb, prec):
    tstep = pl.program_id(1)

    @pl.when(tstep == 0)
    def _():
        s_scr[...] = s0_ref[...]
        prev_scr[...] = cp_ref[...]

    r8 = _iota((8, CONV_CH), 0)
    cw = cw_ref[...]
    chunks = []
    for b in range(bb):
        prev = prev_scr[b]

        def fix(rolled, s, prev=prev):
            first = jnp.where(r8 < s, pltpu.roll(prev, s, axis=0), rolled[0:8])
            return jnp.concatenate([first, rolled[8:]], axis=0)

        u_b = jnp.concatenate([zq_ref[b], zk_ref[b], zv_ref[b]], axis=1)
        qkv = _conv_silu(u_b, cw, fix)
        prev_scr[b] = u_b[CHUNK - 8:CHUNK]
        chunks.append(dict(q=qkv[:, 0:HEADS_W], k=qkv[:, HEADS_W:2 * HEADS_W], v=qkv[:, 2 * HEADS_W:],
                           zs=zs_ref[b] + add_ref[...], rows=rows_ref[b, 0]))

    prm = dict(bones=_block_ones(), alog_row=alog_ref[...], rowadd=rowadd_ref[...],
               rowalog=rowalog_ref[...])
    loc = _dn_local(chunks, prm, CHUNK, CHUNK, prec)
    bdm = _bd_mask()
    units = [(b, p_) for b in range(bb) for p_ in range(N_PAIRS)]
    s_prev = [s_scr[b, p_] for b, p_ in units]
    ws = [_mm(jnp.concatenate([loc[b][p_]["w"], loc[b][p_]["qg"]], axis=0), s)
          for s, (b, p_) in zip(s_prev, units)]
    v_new = [loc[b][p_]["u"] - w_[:CHUNK] for w_, (b, p_) in zip(ws, units)]
    o = [w_[CHUNK:] + _mm(loc[b][p_]["a_in"], _x2(v)) for w_, v, (b, p_) in zip(ws, v_new, units)]
    upd = [_mm(loc[b][p_]["kd"].T, v) for v, (b, p_) in zip(v_new, units)]
    for i, (b, p_) in enumerate(units):
        s_scr[b, p_] = loc[b][p_]["eg_last"][0:1, :] * s_prev[i] + jnp.where(bdm, upd[i], 0.0)

    def store(b, p_, val):
        o_ref[b, :, p_ * LANES:(p_ + 1) * LANES] = val

    _head_norm_store([[o[b * N_PAIRS + p_] for p_ in range(N_PAIRS)] for b in range(bb)],
                     [[zg_ref[b, :, p_ * LANES:(p_ + 1) * LANES] for p_ in range(N_PAIRS)] for b in range(bb)],
                     nw_ref[...], prm["bones"], _silu, store)

    @pl.when(tstep == pl.num_programs(1) - 1)
    def _():
        s_out_ref[...] = s_scr[...]


def _dn_batch_kernel(zq_ref, zk_ref, zv_ref, zg_ref, zs_ref, rows_ref, cp_ref, cw_ref, s0_ref,
                     add_ref, alog_ref, rowadd_ref, rowalog_ref, nw_ref,
                     o_ref, s_out_ref, *, nc, n_valid, prec):
    rb = nc * CHUNK
    pall = cp_ref[...]
    rr = _imod(_iota((rb, CONV_CH), 0), SUB)

    def fix(rolled, s):
        return jnp.where(rr < s, pltpu.roll(pall, (rb - SUB + s) % rb, axis=0), rolled)

    u_all = jnp.concatenate([zq_ref[...], zk_ref[...], zv_ref[...]], axis=1)
    qkv = _conv_silu(u_all, cw_ref[...], fix)
    zs_all = zs_ref[...] + add_ref[...]
    chunks = []
    for c in range(nc):
        rs = slice(c * CHUNK, (c + 1) * CHUNK)
        chunks.append(dict(q=qkv[rs, 0:HEADS_W], k=qkv[rs, HEADS_W:2 * HEADS_W], v=qkv[rs, 2 * HEADS_W:],
                           zs=zs_all[rs], rows=rows_ref[c]))

    prm = dict(bones=_block_ones(), alog_row=alog_ref[...], rowadd=rowadd_ref[...],
               rowalog=rowalog_ref[...])
    loc = _dn_local(chunks, prm, SUB, n_valid, prec)
    bdm = _bd_mask()
    nsub = CHUNK // SUB
    lane_sub = _idiv(_iota((LANES, CHUNK), 1), SUB)
    units = [(c, p_) for c in range(nc) for p_ in range(N_PAIRS)]
    subs = [(c, p_, sb) for c, p_ in units for sb in range(nsub)]
    s_prev = {k_: _bd_from_stacked(s0_ref[k_[0] * nsub + k_[2], k_[1]], bdm) for k_ in subs}
    rsl = lambda sb: slice(sb * SUB, (sb + 1) * SUB)
    ws = {k_: _mm(jnp.concatenate([loc[k_[0]][k_[1]]["w"][rsl(k_[2])], loc[k_[0]][k_[1]]["qg"][rsl(k_[2])]],
                                  axis=0), s_prev[k_]) for k_ in subs}
    v_new = {u_: jnp.concatenate([loc[u_[0]][u_[1]]["u"][rsl(sb)] - ws[u_ + (sb,)][:SUB] for sb in range(nsub)],
                                 axis=0) for u_ in units}
    o = {u_: jnp.concatenate([ws[u_ + (sb,)][SUB:] for sb in range(nsub)], axis=0)
         + _mm(loc[u_[0]][u_[1]]["a_in"], _x2(v_new[u_])) for u_ in units}
    kd_t = {u_: loc[u_[0]][u_[1]]["kd"].T for u_ in units}
    for k_ in subs:
        c, p_, sb = k_
        upd = _mm(jnp.where(lane_sub == sb, kd_t[(c, p_)], 0.0), v_new[(c, p_)])
        eg = loc[c][p_]["eg_last"][sb * SUB:sb * SUB + 1, :]
        s_out_ref[c * nsub + sb, p_] = _stacked_from_bd(eg * s_prev[k_] + jnp.where(bdm, upd, 0.0))

    def store(c, p_, val):
        o_ref[c * CHUNK:(c + 1) * CHUNK, p_ * LANES:(p_ + 1) * LANES] = val

    _head_norm_store([[o[(c, p_)] for p_ in range(N_PAIRS)] for c in range(nc)],
                     [[zg_ref[c * CHUNK:(c + 1) * CHUNK, p_ * LANES:(p_ + 1) * LANES] for p_ in range(N_PAIRS)]
                      for c in range(nc)],
                     nw_ref[...], prm["bones"], _silu, store)


def _deltanet_seq(z3, rows4, convp, s0, prm, bb, prec):
    batch, seq_len, _ = z3.shape
    fix2 = lambda g, i: (0, 0)
    zspec = lambda width, col: pl.BlockSpec((bb, CHUNK, width), lambda g, i: (g, i, col))
    s_spec = pl.BlockSpec((bb, N_PAIRS, LANES, LANES), lambda g, i: (g, 0, 0, 0))
    in_specs = [zspec(HEADS_W, 0), zspec(HEADS_W, 1), zspec(HEADS_W, 2), zspec(HEADS_W, 3), zspec(LANES, 26),
                pl.BlockSpec((bb, 1, 16, LANES), lambda g, i: (g, i, 0, 0)),
                pl.BlockSpec((bb, 8, CONV_CH), lambda g, i: (g, 0, 0)),
                pl.BlockSpec((8, CONV_CH), fix2), s_spec,
                pl.BlockSpec((1, LANES), fix2), pl.BlockSpec((1, LANES), fix2),
                pl.BlockSpec((16, LANES), fix2), pl.BlockSpec((16, LANES), fix2),
                pl.BlockSpec((1, LANES), fix2)]
    return pl.pallas_call(
        functools.partial(_dn_seq_kernel, bb=bb, prec=prec),
        out_shape=(jax.ShapeDtypeStruct((batch, seq_len, HEADS_W), BF16), jax.ShapeDtypeStruct(s0.shape, F32)),
        grid=(batch // bb, seq_len // CHUNK),
        in_specs=in_specs,
        out_specs=(pl.BlockSpec((bb, CHUNK, HEADS_W), lambda g, i: (g, i, 0)), s_spec),
        scratch_shapes=[pltpu.VMEM((bb, N_PAIRS, LANES, LANES), F32), pltpu.VMEM((bb, 8, CONV_CH), F32)],
        compiler_params=pltpu.CompilerParams(dimension_semantics=("parallel", "arbitrary"),
                                             vmem_limit_bytes=VMEM_LIMIT),
        name="deltanet_seq",
    )(z3, z3, z3, z3, z3, rows4, convp, prm["conv_w8"], s0, prm["add_row"], prm["alog_row"],
      prm["rowadd"], prm["rowalog"], prm["dn_normw"])


def _deltanet_batch(z, rows, convp, s0, prm, nc, n_valid, prec):
    t = z.shape[0]
    rb = nc * CHUNK
    nsb = rb // SUB
    fix2 = lambda i: (0, 0)
    zspec = lambda width, col: pl.BlockSpec((rb, width), lambda i: (i, col))
    s_spec = pl.BlockSpec((nsb, N_PAIRS, LANES, HEAD_DIM), lambda i: (i, 0, 0, 0))
    in_specs = [zspec(HEADS_W, 0), zspec(HEADS_W, 1), zspec(HEADS_W, 2), zspec(HEADS_W, 3), zspec(LANES, 26),
                pl.BlockSpec((nc, 16, LANES), lambda i: (i, 0, 0)),
                pl.BlockSpec((rb, CONV_CH), lambda i: (i, 0)),
                pl.BlockSpec((8, CONV_CH), fix2), s_spec,
                pl.BlockSpec((1, LANES), fix2), pl.BlockSpec((1, LANES), fix2),
                pl.BlockSpec((16, LANES), fix2), pl.BlockSpec((16, LANES), fix2),
                pl.BlockSpec((1, LANES), fix2)]
    return pl.pallas_call(
        functools.partial(_dn_batch_kernel, nc=nc, n_valid=n_valid, prec=prec),
        out_shape=(jax.ShapeDtypeStruct((t, HEADS_W), BF16), jax.ShapeDtypeStruct(s0.shape, F32)),
        grid=(t // rb,),
        in_specs=in_specs,
        out_specs=(pl.BlockSpec((rb, HEADS_W), lambda i: (i, 0)), s_spec),
        compiler_params=pltpu.CompilerParams(dimension_semantics=("parallel",),
                                             vmem_limit_bytes=VMEM_LIMIT),
        name="deltanet_batch",
    )(z, z, z, z, z, rows, convp, prm["conv_w8"], s0, prm["add_row"], prm["alog_row"],
      prm["rowadd"], prm["rowalog"], prm["dn_normw"])


def _ml_units(chunks, prm, sub, n_valid):
    n = len(chunks)
    incl, _ = _pair_masks(sub)
    lo = _lane_lo((CHUNK, LANES))
    ltri, upair = _ltri(sub), _upair(sub)
    ones2 = prm["bones"]
    bdm2 = jnp.concatenate([_bd_mask(), _bd_mask()], axis=1)
    nsub = CHUNK // sub
    units = [(c, p_) for c in range(n) for p_ in range(N_PAIRS)]
    pair = lambda x, p_: x[:, p_ * LANES:(p_ + 1) * LANES]

    comp = []
    for ch in chunks:
        rows = ch["rows"] + prm["rowadd"]
        ig_c, ls_c = ch["zs"], -_softplus(-ch["zs"])
        ig_r, ls_r = rows, -_softplus(-rows)
        if n_valid != sub:
            vc, vr = _valid_col(sub, n_valid), _valid_row(sub, n_valid)
            ig_c, ls_c = jnp.where(vc, ig_c, NEG), jnp.where(vc, ls_c, 0.0)
            ig_r, ls_r = jnp.where(vr, ig_r, NEG), jnp.where(vr, ls_r, 0.0)
        comp.append(dict(ig_c=ig_c, b_c=_mm_sel_l(ltri, ls_c), ig_r=ig_r, b_r=_mm_sel_r(ls_r, upair)))

    qp = [pair(chunks[c]["q"], p_) for c, p_ in units]
    kp = [pair(chunks[c]["k"], p_) * (HEAD_DIM ** -0.5) for c, p_ in units]
    vp = [pair(chunks[c]["v"], p_) for c, p_ in units]
    b_col = [_expand(comp[c]["b_c"], 18 + 2 * p_, 19 + 2 * p_) for c, p_ in units]
    ig_col = [_expand(comp[c]["ig_c"], 12 + 2 * p_, 13 + 2 * p_) for c, p_ in units]
    dl = [jnp.where(incl, bc - comp[c]["b_r"][6 + p_:7 + p_, :] + comp[c]["ig_r"][3 + p_:4 + p_, :], NEG)
          for bc, (c, p_) in zip(b_col, units)]
    mx = [jnp.where(lo, jnp.max(jnp.where(lo, d, NEG), axis=1, keepdims=True),
                    jnp.max(jnp.where(lo, NEG, d), axis=1, keepdims=True)) for d in dl]
    inter = [bc + chunks[c]["ms"][p_] for bc, (c, p_) in zip(b_col, units)]
    mt = [jnp.maximum(a, b) for a, b in zip(inter, mx)]
    s = [_mm(q_, _x2(k_), _NT) * jnp.exp(d - m) for q_, k_, d, m in zip(qp, kp, dl, mt)]
    sv = [_mm(s_, jnp.concatenate([_x2(v_), ones2], axis=1)) for s_, v_ in zip(s, vp)]
    cn_prev = [[chunks[c]["cn"](p_, sb) for sb in range(nsub)] for c, p_ in units]
    qcn = [jnp.concatenate([_mm(q_[sb * sub:(sb + 1) * sub], cn[sb]) for sb in range(nsub)], axis=0)
           if nsub > 1 else _mm(q_, cn[0]) for q_, cn in zip(qp, cn_prev)]
    w_inter = [jnp.exp(a - m) for a, m in zip(inter, mt)]
    h = [(wi * qc[:, :LANES] + sv_[:, :LANES])
         / jnp.maximum(jnp.abs(wi * qc[:, LANES:] + sv_[:, LANES:]), jnp.exp(-m))
         for wi, qc, sv_, m in zip(w_inter, qcn, sv, mt)]
    m_end = [_last_row_bcast(m, sub) for m in mt]
    b_last = [_last_row_bcast(bc, sub) for bc in b_col]
    a_end = [jnp.exp(bl + chunks[c]["ms"][p_] - me) for bl, me, (c, p_) in zip(b_last, m_end, units)]
    kw_t = [(k_ * jnp.exp(bl - bc + ig - me)).T for k_, bl, bc, ig, me in zip(kp, b_last, b_col, ig_col, m_end)]
    v1 = [jnp.concatenate([v_, jnp.ones((CHUNK, LANES), F32)], axis=1) for v_ in vp]
    lane_sub = _idiv(_iota((LANES, CHUNK), 1), sub) if nsub > 1 else None
    out = [[None] * N_PAIRS for _ in range(n)]
    for i, (c, p_) in enumerate(units):
        new_states, new_m = [], []
        for sb in range(nsub):
            lhs = kw_t[i] if nsub == 1 else jnp.where(lane_sub == sb, kw_t[i], 0.0)
            upd = _mm(lhs, v1[i])
            a_row = a_end[i][sb * sub:sb * sub + 1, :]
            a2 = jnp.concatenate([a_row, a_row], axis=1)
            new_states.append(a2 * cn_prev[i][sb] + jnp.where(bdm2, upd, 0.0))
            new_m.append(m_end[i][sb * sub:sb * sub + 1, :])
        out[c][p_] = (h[i], new_states, new_m)
    return out


def _ml_seq_kernel(zq_ref, zk_ref, zv_ref, zo_ref, zs_ref, rows_ref, cn0_ref, m0_ref, add_ref, rowadd_ref,
                   nw_ref, o_ref, cn_out_ref, m_out_ref, cn_scr, m_scr, *, bb):
    tstep = pl.program_id(1)

    @pl.when(tstep == 0)
    def _():
        cn_scr[...] = cn0_ref[...]
        m_scr[...] = m0_ref[...]

    prm = dict(bones=_block_ones(), rowadd=rowadd_ref[...])
    chunks = []
    for b in range(bb):
        chunks.append(dict(q=zq_ref[b], k=zk_ref[b], v=zv_ref[b], zs=zs_ref[b] + add_ref[...],
                           rows=rows_ref[b, 0],
                           ms=[jnp.broadcast_to(m_scr[b, p_:p_ + 1, :], (CHUNK, LANES)) for p_ in range(N_PAIRS)],
                           cn=lambda p_, sb, b=b: cn_scr[b, p_]))
    res = _ml_units(chunks, prm, CHUNK, CHUNK)
    for b in range(bb):
        for p_ in range(N_PAIRS):
            cn_scr[b, p_] = res[b][p_][1][0]
            m_scr[b, p_:p_ + 1, :] = res[b][p_][2][0]

    def store(b, p_, val):
        o_ref[b, :, p_ * LANES:(p_ + 1) * LANES] = val

    _head_norm_store([[res[b][p_][0] for p_ in range(N_PAIRS)] for b in range(bb)],
                     [[zo_ref[b, :, p_ * LANES:(p_ + 1) * LANES] for p_ in range(N_PAIRS)] for b in range(bb)],
                     nw_ref[...], prm["bones"], _sigmoid, store)

    @pl.when(tstep == pl.num_programs(1) - 1)
    def _():
        cn_out_ref[...] = cn_scr[...]
        m_out_ref[...] = m_scr[...]


def _ml_batch_kernel(zq_ref, zk_ref, zv_ref, zo_ref, zs_ref, rows_ref, c0_ref, n0_ref, m0_ref, add_ref,
                     rowadd_ref, nw_ref, o_ref, c_out_ref, n_out_ref, m_out_ref, *, nc, n_valid):
    prm = dict(bones=_block_ones(), rowadd=rowadd_ref[...])
    bdm = _bd_mask()
    zs_all = zs_ref[...] + add_ref[...]
    nsub = CHUNK // SUB

    def cn_of(c, p_, sb):
        i = c * nsub + sb
        return jnp.concatenate([_bd_from_stacked(c0_ref[i, p_], bdm), _bd_from_stacked(n0_ref[i, p_], bdm)],
                               axis=1)

    chunks = []
    for c in range(nc):
        rs = slice(c * CHUNK, (c + 1) * CHUNK)
        ms = [jnp.concatenate([jnp.broadcast_to(m0_ref[c * nsub + sb, p_:p_ + 1, :], (SUB, LANES))
                               for sb in range(nsub)], axis=0) for p_ in range(N_PAIRS)]
        chunks.append(dict(q=zq_ref[rs, :], k=zk_ref[rs, :], v=zv_ref[rs, :], zs=zs_all[rs], rows=rows_ref[c],
                           ms=ms, cn=functools.partial(cn_of, c)))
    res = _ml_units(chunks, prm, SUB, n_valid)
    for c in range(nc):
        for p_ in range(N_PAIRS):
            _, new_states, new_m = res[c][p_]
            for sb in range(nsub):
                i = c * nsub + sb
                c_out_ref[i, p_] = _stacked_from_bd(new_states[sb][:, :LANES])
                n_out_ref[i, p_] = _stacked_from_bd(new_states[sb][:, LANES:])
                m_out_ref[i, p_:p_ + 1, :] = new_m[sb]

    def store(c, p_, val):
        o_ref[c * CHUNK:(c + 1) * CHUNK, p_ * LANES:(p_ + 1) * LANES] = val

    _head_norm_store([[res[c][p_][0] for p_ in range(N_PAIRS)] for c in range(nc)],
                     [[zo_ref[c * CHUNK:(c + 1) * CHUNK, p_ * LANES:(p_ + 1) * LANES] for p_ in range(N_PAIRS)]
                      for c in range(nc)],
                     nw_ref[...], prm["bones"], _sigmoid, store)


def _mlstm_seq(z3, rows4, cn0, m0, prm, bb):
    batch, seq_len, _ = z3.shape
    fix2 = lambda g, i: (0, 0)
    zspec = lambda width, col: pl.BlockSpec((bb, CHUNK, width), lambda g, i: (g, i, col))
    cn_spec = pl.BlockSpec((bb, N_PAIRS, LANES, 2 * LANES), lambda g, i: (g, 0, 0, 0))
    m_spec = pl.BlockSpec((bb, 8, LANES), lambda g, i: (g, 0, 0))
    in_specs = [zspec(HEADS_W, 4), zspec(HEADS_W, 5), zspec(HEADS_W, 6), zspec(HEADS_W, 7), zspec(LANES, 26),
                pl.BlockSpec((bb, 1, 16, LANES), lambda g, i: (g, i, 0, 0)), cn_spec, m_spec,
                pl.BlockSpec((1, LANES), fix2), pl.BlockSpec((16, LANES), fix2), pl.BlockSpec((1, LANES), fix2)]
    return pl.pallas_call(
        functools.partial(_ml_seq_kernel, bb=bb),
        out_shape=(jax.ShapeDtypeStruct((batch, seq_len, HEADS_W), BF16), jax.ShapeDtypeStruct(cn0.shape, F32),
                   jax.ShapeDtypeStruct(m0.shape, F32)),
        grid=(batch // bb, seq_len // CHUNK),
        in_specs=in_specs,
        out_specs=(pl.BlockSpec((bb, CHUNK, HEADS_W), lambda g, i: (g, i, 0)), cn_spec, m_spec),
        scratch_shapes=[pltpu.VMEM((bb, N_PAIRS, LANES, 2 * LANES), F32), pltpu.VMEM((bb, 8, LANES), F32)],
        compiler_params=pltpu.CompilerParams(dimension_semantics=("parallel", "arbitrary"),
                                             vmem_limit_bytes=VMEM_LIMIT),
        name="mlstm_seq",
    )(z3, z3, z3, z3, z3, rows4, cn0, m0, prm["add_row"], prm["rowadd"], prm["ml_normw"])


def _mlstm_batch(z, rows, c0, n0, m0, prm, nc, n_valid):
    t = z.shape[0]
    rb = nc * CHUNK
    nsb = rb // SUB
    fix2 = lambda i: (0, 0)
    zspec = lambda width, col: pl.BlockSpec((rb, width), lambda i: (i, col))
    st_spec = pl.BlockSpec((nsb, N_PAIRS, LANES, HEAD_DIM), lambda i: (i, 0, 0, 0))
    m_spec = pl.BlockSpec((nsb, 8, LANES), lambda i: (i, 0, 0))
    in_specs = [zspec(HEADS_W, 4), zspec(HEADS_W, 5), zspec(HEADS_W, 6), zspec(HEADS_W, 7), zspec(LANES, 26),
                pl.BlockSpec((nc, 16, LANES), lambda i: (i, 0, 0)), st_spec, st_spec, m_spec,
                pl.BlockSpec((1, LANES), fix2), pl.BlockSpec((16, LANES), fix2), pl.BlockSpec((1, LANES), fix2)]
    return pl.pallas_call(
        functools.partial(_ml_batch_kernel, nc=nc, n_valid=n_valid),
        out_shape=(jax.ShapeDtypeStruct((t, HEADS_W), BF16), jax.ShapeDtypeStruct(c0.shape, F32),
                   jax.ShapeDtypeStruct(n0.shape, F32), jax.ShapeDtypeStruct(m0.shape, F32)),
        grid=(t // rb,),
        in_specs=in_specs,
        out_specs=(pl.BlockSpec((rb, HEADS_W), lambda i: (i, 0)), st_spec, st_spec, m_spec),
        compiler_params=pltpu.CompilerParams(dimension_semantics=("parallel",),
                                             vmem_limit_bytes=VMEM_LIMIT),
        name="mlstm_batch",
    )(z, z, z, z, z, rows, c0, n0, m0, prm["add_row"], prm["rowadd"], prm["ml_normw"])


def _pack_w_in(w_in):
    offs = np.concatenate([[0], np.cumsum(IN_SIZES)])
    seg = lambda i: w_in[:, int(offs[i]):int(offs[i + 1])]
    small = jnp.concatenate([seg(3), seg(4), seg(9), seg(10)], axis=1)
    small = jnp.pad(small, ((0, 0), (0, LANES - 4 * N_HEADS)))
    cols = [seg(0), seg(1), seg(2), seg(5), seg(6), seg(7), seg(8), seg(11), seg(12), small]
    return jnp.concatenate(cols, axis=1).astype(BF16)


def _pair_rows(v6):
    return jnp.repeat(v6.reshape(N_PAIRS, 2), HEAD_DIM, axis=1).reshape(N_PAIRS, LANES)


def _layer_params(l, a):
    f = lambda name: a[name][l].astype(F32)
    zeros6 = jnp.zeros((N_HEADS,), F32)
    add_row = jnp.concatenate([zeros6, f("dn_dt_bias"), f("ml_i_bias"), f("ml_f_bias")])
    add_row = jnp.pad(add_row, (0, LANES - 4 * N_HEADS))[None, :]
    alog_row = jnp.pad(jnp.concatenate([zeros6, f("dn_a_log")]), (0, LANES - 2 * N_HEADS))[None, :]
    rowadd = jnp.concatenate([_pair_rows(f("dn_dt_bias")), _pair_rows(f("ml_i_bias")),
                              _pair_rows(f("ml_f_bias")), jnp.zeros((7, LANES), F32)], axis=0)
    rowalog = jnp.concatenate([_pair_rows(f("dn_a_log")), jnp.zeros((13, LANES), F32)], axis=0)
    eye_g = jnp.eye(S5_GROUPS, dtype=F32)
    bd_in = lambda b: jnp.einsum("gpc,gh->gchp", b, eye_g).reshape(S5_WIDTH, S5_STATE).astype(BF16)
    bd_out = lambda c: jnp.einsum("gcp,gh->gphc", c, eye_g).reshape(S5_STATE, S5_WIDTH).astype(BF16)
    w_out = f("w_out").astype(BF16)
    return dict(
        norm_mix=f("norm_mix")[None, :], w_in=_pack_w_in(f("w_in")),
        add_row=add_row, alog_row=alog_row, rowadd=rowadd, rowalog=rowalog,
        conv_w8=jnp.pad(f("dn_conv_w"), ((0, 8 - CONV_W), (0, 0))),
        dn_normw=jnp.tile(f("dn_norm"), 2)[None, :], ml_normw=jnp.tile(f("ml_norm"), 2)[None, :],
        s5=dict(lre=f("s5_lam_re").reshape(1, S5_STATE), lim=f("s5_lam_im").reshape(1, S5_STATE),
                ldt=jnp.repeat(f("s5_log_dt"), S5_P)[None, :],
                bre=bd_in(f("s5_b_re")), bim=bd_in(f("s5_b_im")),
                cre=bd_out(f("s5_c_re")), cim=bd_out(f("s5_c_im")),
                d=f("s5_d").reshape(1, S5_WIDTH), wg=f("s5_w_glu").astype(BF16),
                bg=f("s5_b_glu")[None, :]),
        w_out1=w_out[:HEADS_W], w_out2=w_out[HEADS_W:2 * HEADS_W], w_out3=w_out[2 * HEADS_W:],
        norm_x=f("norm_x")[None, :], w_xq=f("w_xq").astype(BF16), w_xo=f("w_xo").astype(BF16),
        norm_ff=f("norm_ff")[None, :], w_ff1=f("w_ff1").astype(BF16), w_ff2=f("w_ff2").astype(BF16),
    )


def _row_form(z, n_chunks):
    zs = z[..., Z_COLS - LANES:Z_COLS - LANES + 4 * N_HEADS].reshape(n_chunks, CHUNK, 4, N_PAIRS, 2)
    r = jnp.transpose(zs[:, :, 1:], (0, 2, 3, 4, 1)).reshape(n_chunks, 9, LANES)
    return jnp.pad(r, ((0, 0), (0, 7), (0, 0)))


def _un_bd_pairs(s):
    n = s.shape[0]
    a = s[:, :, :HEAD_DIM, :HEAD_DIM]
    b = s[:, :, HEAD_DIM:, HEAD_DIM:]
    return jnp.stack([a, b], axis=2).reshape(n, N_HEADS, HEAD_DIM, HEAD_DIM)


def _pack_m(m):
    s = m.shape[0]
    r = jnp.repeat(m.reshape(s, N_PAIRS, 2), HEAD_DIM, axis=2).reshape(s, N_PAIRS, LANES)
    return jnp.pad(r, ((0, 0), (0, 8 - N_PAIRS), (0, 0)))


def _unpack_m(mp):
    return mp[:, :N_PAIRS, ::HEAD_DIM].reshape(mp.shape[0], N_HEADS)


ROW_TILE = 512
SEQ_BATCH = 8
S5_STEPS = 64
SAMPLE_CHUNKS = 2
ATTN_BATCH = 8
SEQ_SOLVE = (1, True, 1)
SAMPLE_SOLVE = (3, False, 1)


def _tile(n, pref):
    tile = min(n, pref)
    assert n % tile == 0, (n, pref)
    return tile


def _layer_prompt(x, mem_k, mem_v, prm, final_g):
    batch, seq_len, _ = x.shape
    tm = _tile(seq_len, ROW_TILE)
    bb = _tile(batch, SEQ_BATCH)
    z3 = _norm_matmul(x, prm["norm_mix"], prm["w_in"], tm)
    rows4 = _row_form(z3, batch * seq_len // CHUNK).reshape(batch, seq_len // CHUNK, 16, LANES)
    new_conv = z3[:, seq_len - 3:, :CONV_CH]

    dn, s_new = _deltanet_seq(z3, rows4, jnp.zeros((batch, 8, CONV_CH), F32),
                              jnp.zeros((batch, N_PAIRS, LANES, LANES), F32), prm, bb, SEQ_SOLVE)
    ml, cn_new, m_new = _mlstm_seq(z3, rows4, jnp.zeros((batch, N_PAIRS, LANES, 2 * LANES), F32),
                                   jnp.zeros((batch, 8, LANES), F32), prm, bb)
    zero_h = jnp.zeros((batch, S5_STATE), F32)
    s5o, hr, hi = _s5(z3, zero_h, zero_h, prm["s5"], _tile(seq_len, S5_STEPS), S5_STATE, True)

    x1, q = _out_proj(dn, ml, s5o, x, prm["w_out1"], prm["w_out2"], prm["w_out3"], prm["norm_x"],
                      prm["w_xq"], tm)
    att = _attn_seq(q, mem_k, mem_v, tm)
    g_fin = final_g if final_g is not None else prm["norm_ff"]
    x3 = _ffn(x1, att, prm["w_xo"], prm["norm_ff"], prm["w_ff1"], prm["w_ff2"], g_fin, tm,
              final_g is not None)
    st = (new_conv, _un_bd_pairs(s_new), _un_bd_pairs(cn_new[..., :LANES]),
          _un_bd_pairs(cn_new[..., LANES:])[..., 0], _unpack_m(m_new),
          hr.reshape(batch, S5_GROUPS, S5_P), hi.reshape(batch, S5_GROUPS, S5_P))
    return x3, st


def _layer_sample(x, cache_k, cache_v, layer, conv_buf, dn_s, ml_c, ml_n, ml_m, ssm_re, ssm_im, prm, batch,
                  seq_len, final_g):
    t = batch * seq_len
    z = _norm_matmul(x, prm["norm_mix"], prm["w_in"], t)
    z3 = z.reshape(batch, seq_len, Z_COLS)
    new_conv = jnp.concatenate([conv_buf, z3[:, :, :CONV_CH]], axis=1)[:, seq_len:]
    zp = jnp.pad(z3, ((0, 0), (0, SUB - seq_len), (0, 0))).reshape(batch * SUB, Z_COLS)
    n_chunks = batch * SUB // CHUNK
    nc = _tile(n_chunks, SAMPLE_CHUNKS)
    rows = _row_form(zp, n_chunks)
    convp = jnp.pad(conv_buf, ((0, 0), (SUB - (CONV_W - 1), 0), (0, 0))).reshape(batch * SUB, CONV_CH)
    stacked = lambda s: s.reshape(batch, N_PAIRS, LANES, HEAD_DIM)
    unstack = lambda s: s.reshape(batch, N_HEADS, HEAD_DIM, HEAD_DIM)

    dn, s_new = _deltanet_batch(zp, rows, convp, stacked(dn_s), prm, nc, seq_len, SAMPLE_SOLVE)
    n_b = jnp.broadcast_to(ml_n[..., None], ml_c.shape)
    ml, c_new, n_new, m_new = _mlstm_batch(zp, rows, stacked(ml_c), stacked(n_b), _pack_m(ml_m), prm, nc,
                                           seq_len)
    unpad = lambda a: a.reshape(batch, SUB, -1)[:, :seq_len].reshape(1, t, -1)
    dn, ml = unpad(dn), unpad(ml)

    u_tm = jnp.transpose(z3[:, :, 3072:3072 + S5_WIDTH], (1, 0, 2))
    y_tm, hr, hi = _s5(u_tm.reshape(t, S5_WIDTH), ssm_re.reshape(batch, S5_STATE),
                       ssm_im.reshape(batch, S5_STATE), prm["s5"], seq_len, LANES, False)
    s5o = jnp.transpose(y_tm.reshape(seq_len, batch, S5_WIDTH), (1, 0, 2)).reshape(t, S5_WIDTH)

    x1, q = _out_proj(dn, ml, s5o, x, prm["w_out1"], prm["w_out2"], prm["w_out3"], prm["norm_x"],
                      prm["w_xq"], t)
    q3 = jnp.pad(q.reshape(batch, seq_len, X_WIDTH), ((0, 0), (0, SUB - seq_len), (0, 0)))
    att = _attn_batch(q3, cache_k, cache_v, layer, _tile(batch, ATTN_BATCH))[:, :seq_len].reshape(1, t, X_WIDTH)
    g_fin = final_g if final_g is not None else prm["norm_ff"]
    x3 = _ffn(x1, att, prm["w_xo"], prm["norm_ff"], prm["w_ff1"], prm["w_ff2"], g_fin, t,
              final_g is not None)
    st = (new_conv, unstack(s_new), unstack(c_new), unstack(n_new)[..., 0], _unpack_m(m_new),
          hr.reshape(batch, S5_GROUPS, S5_P), hi.reshape(batch, S5_GROUPS, S5_P))
    return x3, st


def kernel(x_prompt, x_sample, mem_prompt, state_dn_conv, state_dn_s, state_ml_c, state_ml_n, state_ml_m, state_ssm_re, state_ssm_im, cache_mem_k, cache_mem_v, norm_mix, w_in, dn_conv_w, dn_a_log, dn_dt_bias, dn_norm, ml_i_bias, ml_f_bias, ml_norm, s5_lam_re, s5_lam_im, s5_log_dt, s5_b_re, s5_b_im, s5_c_re, s5_c_im, s5_d, s5_w_glu, s5_b_glu, w_out, norm_x, norm_mem, w_xq, w_xk, w_xv, w_xo, norm_ff, w_ff1, w_ff2, norm_final):
    a = dict(norm_mix=norm_mix, w_in=w_in, dn_conv_w=dn_conv_w, dn_a_log=dn_a_log, dn_dt_bias=dn_dt_bias,
             dn_norm=dn_norm, ml_i_bias=ml_i_bias, ml_f_bias=ml_f_bias, ml_norm=ml_norm,
             s5_lam_re=s5_lam_re, s5_lam_im=s5_lam_im, s5_log_dt=s5_log_dt, s5_b_re=s5_b_re,
             s5_b_im=s5_b_im, s5_c_re=s5_c_re, s5_c_im=s5_c_im, s5_d=s5_d, s5_w_glu=s5_w_glu,
             s5_b_glu=s5_b_glu, w_out=w_out, norm_x=norm_x, w_xq=w_xq, w_xo=w_xo, norm_ff=norm_ff,
             w_ff1=w_ff1, w_ff2=w_ff2)
    prms = [_layer_params(l, a) for l in range(DEPTH)]
    g_final = norm_final.astype(F32)[None, :]

    bp, lp, _ = x_prompt.shape
    bs, ls, _ = x_sample.shape

    xp = x_prompt
    p_st = []
    for l in range(DEPTH):
        mk = _norm_matmul(mem_prompt, norm_mem[l][None, :], w_xk[l].astype(BF16), N_MEM)
        mv = _norm_matmul(mem_prompt, norm_mem[l][None, :], w_xv[l].astype(BF16), N_MEM)
        xp, st = _layer_prompt(xp, mk, mv, prms[l], g_final if l == DEPTH - 1 else None)
        p_st.append(st + (mk.reshape(bp, N_MEM, X_HEADS, X_HD), mv.reshape(bp, N_MEM, X_HEADS, X_HD)))
    y_prompt = xp

    xs = x_sample.reshape(1, bs * ls, D_MODEL)
    cache_k = cache_mem_k.reshape(DEPTH, bs, N_MEM, X_WIDTH)
    cache_v = cache_mem_v.reshape(DEPTH, bs, N_MEM, X_WIDTH)
    s_st = []
    for l in range(DEPTH):
        xs, st = _layer_sample(xs, cache_k, cache_v, l, state_dn_conv[l], state_dn_s[l], state_ml_c[l],
                               state_ml_n[l], state_ml_m[l], state_ssm_re[l], state_ssm_im[l], prms[l], bs, ls,
                               g_final if l == DEPTH - 1 else None)
        s_st.append(st)
    y_sample = xs.reshape(bs, ls, D_MODEL)

    stack = lambda sts, i: jnp.stack([s[i] for s in sts])
    return (y_prompt, y_sample) + tuple(stack(p_st, i) for i in range(9)) + tuple(stack(s_st, i) for i in range(7))
```

```python
import functools
import math

import numpy as np
import jax
import jax.numpy as jnp
from jax import lax
from jax.experimental import pallas as pl
from jax.experimental.pallas import tpu as pltpu

F32 = jnp.float32
BF16 = jnp.bfloat16

D_MODEL = 1024
DEPTH = 2
N_HEADS = 6
HEAD_DIM = 64
N_PAIRS = N_HEADS // 2
HEADS_W = N_HEADS * HEAD_DIM
CONV_W = 4
CONV_CH = 3 * HEADS_W
S5_WIDTH = 256
S5_GROUP = 16
S5_GROUPS = 16
S5_P = 64
S5_STATE = S5_GROUPS * S5_P
N_MEM = 256
X_HEADS = 4
X_HD = 128
X_WIDTH = X_HEADS * X_HD
D_FF = 4 * D_MODEL
EPS = 1e-6
IN_SIZES = (384, 384, 384, 6, 6, 384, 384, 384, 384, 6, 6, 384, 256)

LANES = 128
CHUNK = 64
SUB = 8
NEG = -1e30
Z_COLS = 3456
VMEM_LIMIT = 52 * 1024 * 1024

_NN = (((1,), (0,)), ((), ()))
_NT = (((1,), (1,)), ((), ()))


def _dot(a, b, dims=_NN):
    return lax.dot_general(a, b, dims, preferred_element_type=F32)


def _split(a, n):
    out = []
    r = a
    for i in range(n):
        h = r.astype(BF16)
        out.append(h)
        if i + 1 < n:
            r = r - h.astype(F32)
    return out


def _mm(a, b, dims=_NN, prec=1):
    if prec == 1:
        return _dot(a.astype(BF16), b.astype(BF16), dims)
    a1, a2 = _split(a, 2)
    b1, b2 = _split(b, 2)
    return (_dot(a1, b2, dims) + _dot(a2, b1, dims)) + _dot(a1, b1, dims)


def _mm_sel_r(a, sel, n=3):
    parts = _split(a, n)
    acc = _dot(parts[-1], sel)
    for p_ in parts[-2::-1]:
        acc = acc + _dot(p_, sel)
    return acc


def _mm_sel_l(sel, a, n=3):
    parts = _split(a, n)
    acc = _dot(sel, parts[-1])
    for p_ in parts[-2::-1]:
        acc = acc + _dot(sel, p_)
    return acc


def _sigmoid(x):
    return 1.0 / (1.0 + jnp.exp(-x))


def _softplus(x):
    return jnp.maximum(x, 0.0) + jnp.log(1.0 + jnp.exp(-jnp.abs(x)))


def _rmsnorm_rows(x, g):
    return x * lax.rsqrt(jnp.mean(x * x, axis=-1, keepdims=True) + EPS) * g


def _iota(shape, dim):
    return lax.broadcasted_iota(jnp.int32, shape, dim)


def _idiv(x, n):
    return x >> (n.bit_length() - 1)


def _imod(x, n):
    return x & (n - 1)


def _lane_lo(shape):
    return _iota(shape, 1) < HEAD_DIM


def _x2(x):
    lo = _lane_lo(x.shape)
    return jnp.concatenate([jnp.where(lo, x, 0.0), jnp.where(lo, 0.0, x)], axis=0)


def _fold(y):
    r = y.shape[0] // 2
    return y[:r] + y[r:]


def _expand(a, c0, c1):
    r = a.shape[0]
    return jnp.where(_lane_lo((r, LANES)), a[:, c0:c0 + 1], a[:, c1:c1 + 1])


def _block_ones():
    i = _iota((LANES, LANES), 0)
    j = _iota((LANES, LANES), 1)
    return jnp.where((i < HEAD_DIM) == (j < HEAD_DIM), 1.0, 0.0).astype(BF16)


def _seg_mean(x, bones):
    return _mm_sel_r(x, bones, n=2) * (1.0 / HEAD_DIM)


def _ltri(sub):
    i = _iota((CHUNK, CHUNK), 0)
    j = _iota((CHUNK, CHUNK), 1)
    ok = (j <= i) & (_idiv(i, sub) == _idiv(j, sub)) if sub != CHUNK else (j <= i)
    return jnp.where(ok, 1.0, 0.0).astype(BF16)


def _upair(sub):
    a = _iota((LANES, LANES), 0)
    b = _iota((LANES, LANES), 1)
    ok = ((a < HEAD_DIM) == (b < HEAD_DIM)) & (a <= b)
    if sub != CHUNK:
        ok = ok & (_idiv(_imod(a, HEAD_DIM), sub) == _idiv(_imod(b, HEAD_DIM), sub))
    return jnp.where(ok, 1.0, 0.0).astype(BF16)


def _pair_masks(sub):
    i = _iota((CHUNK, LANES), 0)
    j = _imod(_iota((CHUNK, LANES), 1), HEAD_DIM)
    same = (_idiv(i, sub) == _idiv(j, sub)) if sub != CHUNK else (i >= 0)
    return same & (j <= i), same & (j < i)


def _valid_col(sub, n_valid):
    i = _iota((CHUNK, 1), 0)
    return _imod(i, sub) < n_valid


def _valid_row(sub, n_valid):
    j = _iota((1, LANES), 1)
    return _imod(j, sub) < n_valid


def _bd_mask():
    i = _iota((LANES, LANES), 0)
    j = _iota((LANES, LANES), 1)
    return (i < HEAD_DIM) == (j < HEAD_DIM)


def _parts(a, prec):
    return tuple(_split(a, 1 if prec == 1 else 2))


def _mmp(ap, bp, dims=_NN):
    acc = None
    if len(bp) > 1:
        acc = _dot(ap[0], bp[1], dims)
    if len(ap) > 1:
        cross = _dot(ap[1], bp[0], dims)
        acc = cross if acc is None else acc + cross
    lead = _dot(ap[0], bp[0], dims)
    return lead if acc is None else acc + lead


def _blk_eq(n, b):
    return _idiv(_iota((n, n), 0), b) == _idiv(_iota((n, n), 1), b)


TRI_BASE = 8


def _tri_inverse_units(mbds, sub, n_valid, prec, newton):
    n = LANES
    eye = jnp.where(_iota((n, n), 0) == _iota((n, n), 1), 1.0, 0.0)
    if sub == CHUNK:
        base, n_sq = TRI_BASE, TRI_BASE.bit_length() - 2
        blk = _blk_eq(n, base)
        m0 = [jnp.where(blk, m, 0.0) for m in mbds]
    else:
        base, n_sq = sub, max(0, math.ceil(math.log2(n_valid)) - 1)
        m0 = mbds
    ts = [eye - m for m in m0]
    if n_sq > 0:
        ps = [_mm(m, m, prec=prec) for m in m0]
        for it in range(n_sq):
            if it == n_sq - 1:
                ts = [t + _mm(t, p_, prec=prec) for t, p_ in zip(ts, ps)]
            else:
                both = [_mm(jnp.concatenate([t, p_], axis=0), p_, prec=prec) for t, p_ in zip(ts, ps)]
                ts = [t + b_[:n] for t, b_ in zip(ts, both)]
                ps = [b_[n:] for b_ in both]
    b = base
    while b < sub:
        outer = _blk_eq(n, 2 * b) & jnp.logical_not(_blk_eq(n, b))
        tp = [_parts(t, prec) for t in ts]
        xs = [_mmp(_parts(jnp.where(outer, m, 0.0), prec), t_) for m, t_ in zip(mbds, tp)]
        ts = [t - _mmp(t_, _parts(x, prec)) for t, t_, x in zip(ts, tp, xs)]
        b *= 2
    if newton:
        res = [eye - (t + _mm(m, t, prec=3)) for m, t in zip(mbds, ts)]
        ts = [t + _mm(t, r) for t, r in zip(ts, res)]
    return ts


def _last_row_bcast(a, sub):
    if sub == CHUNK:
        return jnp.broadcast_to(a[CHUNK - 1:CHUNK, :], a.shape)
    parts = [jnp.broadcast_to(a[s * sub + sub - 1:s * sub + sub, :], (sub, a.shape[1]))
             for s in range(CHUNK // sub)]
    return jnp.concatenate(parts, axis=0)


def _bd_from_stacked(s, bdm):
    return jnp.where(bdm, jnp.concatenate([s, s], axis=1), 0.0)


def _stacked_from_bd(s):
    return s[:, :HEAD_DIM] + s[:, HEAD_DIM:]


def _read_pair(ref, i, p_):
    return jnp.concatenate([ref[i, 2 * p_], ref[i, 2 * p_ + 1]], axis=0)


def _write_pair(ref, i, p_, val):
    ref[i, 2 * p_] = val[:HEAD_DIM]
    ref[i, 2 * p_ + 1] = val[HEAD_DIM:]


SUB_ROWS = 512


def _wspec(rows, cols, layer, row_block=0):
    return pl.BlockSpec((None, rows, cols), lambda i: (layer, row_block, 0))


def _sub_tiles(tm):
    r = min(tm, SUB_ROWS)
    return [slice(s * r, (s + 1) * r) for s in range(tm // r)]


def _rows3(nt):
    return lambda i: (i // nt, i % nt, 0)


def _dense_call(kernel_fn, name, out_shape, in_specs, out_specs, steps, args):
    return pl.pallas_call(
        kernel_fn, out_shape=out_shape, grid=(steps,), in_specs=in_specs, out_specs=out_specs,
        compiler_params=pltpu.CompilerParams(dimension_semantics=("parallel",), vmem_limit_bytes=VMEM_LIMIT),
        name=name)(*args)


def _in_proj_kernel(x_ref, g_ref, w_ref, o_ref):
    for rs in _sub_tiles(x_ref.shape[1]):
        xn = _rmsnorm_rows(x_ref[0, rs, :], g_ref[...]).astype(BF16)
        o_ref[0, rs, :] = _dot(xn, w_ref[...])


def _norm_matmul(x, g, w, layer, tm):
    b, l, k = x.shape
    n = w.shape[2]
    nt = l // tm
    fix = lambda i: (0, 0)
    return _dense_call(
        _in_proj_kernel, "norm_matmul", jax.ShapeDtypeStruct((b, l, n), F32),
        [pl.BlockSpec((1, tm, k), _rows3(nt)), pl.BlockSpec((1, k), fix), _wspec(k, n, layer)],
        pl.BlockSpec((1, tm, n), _rows3(nt)), b * nt, (x, g, w))


def _mem_proj_kernel(x_ref, g_ref, w_ref, o_ref, t_ref):
    res = _dot(_rmsnorm_rows(x_ref[0], g_ref[...]).astype(BF16), w_ref[...])
    o_ref[0] = res
    for h in range(X_HEADS):
        t_ref[0, pl.ds(h, N_MEM, stride=X_HEADS), :] = res[:, h * X_HD:(h + 1) * X_HD]


def _mem_proj(mem, g, w, layer):
    b = mem.shape[0]
    fix = lambda i: (0, 0)
    row = lambda i: (i, 0, 0)
    return _dense_call(
        _mem_proj_kernel, "mem_proj",
        (jax.ShapeDtypeStruct((b, N_MEM, X_WIDTH), F32), jax.ShapeDtypeStruct((b, N_MEM * X_HEADS, X_HD), F32)),
        [pl.BlockSpec((1, N_MEM, D_MODEL), row), pl.BlockSpec((1, D_MODEL), fix),
         _wspec(D_MODEL, X_WIDTH, layer)],
        (pl.BlockSpec((1, N_MEM, X_WIDTH), row), pl.BlockSpec((1, N_MEM * X_HEADS, X_HD), row)), b,
        (mem, g, w))


def _out_proj_kernel(dn_ref, ml_ref, s5_ref, x_ref, w1_ref, w2_ref, w3_ref, g_ref, wq_ref,
                     x1_ref, q_ref):
    for rs in _sub_tiles(x_ref.shape[1]):
        acc = x_ref[0, rs, :] + _dot(dn_ref[0, rs, :], w1_ref[...])
        acc = acc + _dot(ml_ref[0, rs, :], w2_ref[...])
        acc = acc + _dot(s5_ref[rs, :], w3_ref[...])
        x1_ref[0, rs, :] = acc
        xn = _rmsnorm_rows(acc, g_ref[...]).astype(BF16)
        q_ref[0, rs, :] = _dot(xn, wq_ref[...]).astype(BF16)


def _out_proj(dn, ml, s5, x, w_out, g, wq, layer, tm):
    b, l, _ = x.shape
    nt = l // tm
    row = _rows3(nt)
    fix = lambda i: (0, 0)
    return _dense_call(
        _out_proj_kernel, "out_proj",
        (jax.ShapeDtypeStruct((b, l, D_MODEL), F32), jax.ShapeDtypeStruct((b, l, X_WIDTH), BF16)),
        [pl.BlockSpec((1, tm, HEADS_W), row), pl.BlockSpec((1, tm, HEADS_W), row),
         pl.BlockSpec((tm, S5_WIDTH), lambda i: (i % nt, i // nt)), pl.BlockSpec((1, tm, D_MODEL), row),
         _wspec(HEADS_W, D_MODEL, layer, 0), _wspec(HEADS_W, D_MODEL, layer, 1),
         _wspec(S5_WIDTH, D_MODEL, layer, 2 * HEADS_W // S5_WIDTH), pl.BlockSpec((1, D_MODEL), fix),
         _wspec(D_MODEL, X_WIDTH, layer)],
        (pl.BlockSpec((1, tm, D_MODEL), row), pl.BlockSpec((1, tm, X_WIDTH), row)), b * nt,
        (dn, ml, s5, x, w_out, w_out, w_out, g, wq))


def _ffn_kernel(x_ref, o_ref, wo_ref, g_ref, w1_ref, w2_ref, gf_ref, y_ref, *, final_norm, ff_chunk):
    for rs in _sub_tiles(x_ref.shape[1]):
        x2 = x_ref[0, rs, :] + _dot(o_ref[0, rs, :], wo_ref[...])
        xn = _rmsnorm_rows(x2, g_ref[...]).astype(BF16)
        acc = x2
        for c in range(D_FF // ff_chunk):
            hf = _dot(xn, w1_ref[:, c * ff_chunk:(c + 1) * ff_chunk])
            a = jnp.square(jnp.maximum(hf, 0.0)).astype(BF16)
            acc = acc + _dot(a, w2_ref[c * ff_chunk:(c + 1) * ff_chunk, :])
        if final_norm:
            acc = _rmsnorm_rows(acc, gf_ref[...])
        y_ref[0, rs, :] = acc


def _ffn(x, o, wo, g, w1, w2, gf, layer, tm, final_norm):
    b, l, _ = x.shape
    nt = l // tm
    row = _rows3(nt)
    fix = lambda i: (0, 0)
    return _dense_call(
        functools.partial(_ffn_kernel, final_norm=final_norm, ff_chunk=1024), "ffn",
        jax.ShapeDtypeStruct((b, l, D_MODEL), F32),
        [pl.BlockSpec((1, tm, D_MODEL), row), pl.BlockSpec((1, tm, X_WIDTH), row),
         _wspec(X_WIDTH, D_MODEL, layer), pl.BlockSpec((1, D_MODEL), fix),
         _wspec(D_MODEL, D_FF, layer), _wspec(D_FF, D_MODEL, layer),
         pl.BlockSpec((1, D_MODEL), fix)],
        pl.BlockSpec((1, tm, D_MODEL), row), b * nt, (x, o, wo, g, w1, w2, gf))


def _attn_heads(q, k_ref, v_ref, bi):
    outs = []
    for h in range(X_HEADS):
        sl = slice(h * X_HD, (h + 1) * X_HD)
        kh = k_ref[bi, :, sl].astype(BF16)
        vh = v_ref[bi, :, sl].astype(BF16)
        s = _dot(q[:, sl], kh, _NT) * (X_HD ** -0.5)
        e = jnp.exp(s - jnp.max(s, axis=-1, keepdims=True))
        p = e / jnp.sum(e, axis=-1, keepdims=True)
        outs.append(_dot(p.astype(BF16), vh))
    return jnp.concatenate(outs, axis=-1)


def _attn_seq_kernel(q_ref, k_ref, v_ref, o_ref):
    for rs in _sub_tiles(q_ref.shape[1]):
        o_ref[0, rs, :] = _attn_heads(q_ref[0, rs, :], k_ref, v_ref, 0).astype(BF16)


def _attn_seq(q, k, v, tq):
    batch, seq_len, _ = q.shape
    return pl.pallas_call(
        _attn_seq_kernel,
        out_shape=jax.ShapeDtypeStruct((batch, seq_len, X_WIDTH), BF16),
        grid=(batch, seq_len // tq),
        in_specs=[pl.BlockSpec((1, tq, X_WIDTH), lambda b, i: (b, i, 0)),
                  pl.BlockSpec((1, N_MEM, X_WIDTH), lambda b, i: (b, 0, 0)),
                  pl.BlockSpec((1, N_MEM, X_WIDTH), lambda b, i: (b, 0, 0))],
        out_specs=pl.BlockSpec((1, tq, X_WIDTH), lambda b, i: (b, i, 0)),
        compiler_params=pltpu.CompilerParams(dimension_semantics=("parallel", "parallel"),
                                             vmem_limit_bytes=VMEM_LIMIT),
        name="attn_seq",
    )(q, k, v)


def _attn_batch_kernel(q_ref, k_ref, v_ref, o_ref, *, bb):
    rows = X_HEADS * SUB
    col_head = _imod(_iota((rows, N_MEM * X_HEADS), 1), X_HEADS)
    row_head = _idiv(_iota((rows, N_MEM * X_HEADS), 0), SUB)
    own = col_head == row_head
    q4 = [jnp.concatenate([q_ref[bi, :, h * X_HD:(h + 1) * X_HD] for h in range(X_HEADS)], axis=0).astype(BF16)
          for bi in range(bb)]
    s = [jnp.where(own, _dot(q_, k_ref[bi].astype(BF16), _NT) * (X_HD ** -0.5), NEG)
         for bi, q_ in enumerate(q4)]
    e = [jnp.exp(x - jnp.max(x, axis=-1, keepdims=True)) for x in s]
    p = [x / jnp.sum(x, axis=-1, keepdims=True) for x in e]
    o = [_dot(x.astype(BF16), v_ref[bi].astype(BF16)) for bi, x in enumerate(p)]
    for bi in range(bb):
        for h in range(X_HEADS):
            o_ref[bi, :, h * X_HD:(h + 1) * X_HD] = o[bi][h * SUB:(h + 1) * SUB].astype(BF16)


def _attn_batch(q, k, v, layer, bb):
    b, l, _ = q.shape
    blk = lambda i: (i, 0, 0)
    kv = pl.BlockSpec((None, bb, N_MEM * X_HEADS, X_HD), lambda i: (layer, i, 0, 0))
    return pl.pallas_call(
        functools.partial(_attn_batch_kernel, bb=bb),
        out_shape=jax.ShapeDtypeStruct((b, l, X_WIDTH), BF16),
        grid=(b // bb,),
        in_specs=[pl.BlockSpec((bb, l, X_WIDTH), blk), kv, kv],
        out_specs=pl.BlockSpec((bb, l, X_WIDTH), blk),
        compiler_params=pltpu.CompilerParams(dimension_semantics=("parallel",),
                                             vmem_limit_bytes=VMEM_LIMIT),
        name="attn_batch",
    )(q, k, v)


def _s5_kernel(u_ref, h0r_ref, h0i_ref, lre_ref, lim_ref, ldt_ref, bre_ref, bim_ref, cre_ref, cim_ref,
               d_ref, wg_ref, bg_ref, y_ref, hr_out, hi_out,
               xr_s, xi_s, st_r, st_i, coef_s, *io_s, tb, nb, lane_chunk, seq_major_io):
    step = pl.program_id(0)
    if seq_major_io:
        u_s, y_s = io_s
        for b in range(nb):
            for j in range(S5_WIDTH // LANES):
                u_s[j, pl.ds(b, tb, stride=nb), :] = u_ref[b, :, j * LANES:(j + 1) * LANES]
        u = jnp.concatenate([u_s[j] for j in range(S5_WIDTH // LANES)], axis=1)
    else:
        u = u_ref[...]

    @pl.when(step == 0)
    def _():
        lr = lre_ref[...]
        li = lim_ref[...]
        dt = jnp.exp(ldt_ref[...])
        mag = jnp.exp(lr * dt)
        ar = mag * jnp.cos(li * dt)
        ai = mag * jnp.sin(li * dt)
        inv = 1.0 / (lr * lr + li * li)
        coef_s[0:1, :] = ar
        coef_s[1:2, :] = ai
        coef_s[2:3, :] = ((ar - 1.0) * lr + ai * li) * inv
        coef_s[3:4, :] = (ai * lr - (ar - 1.0) * li) * inv
        st_r[...] = h0r_ref[...]
        st_i[...] = h0i_ref[...]

    ub = u.astype(BF16)
    bur = _dot(ub, bre_ref[...])
    bui = _dot(ub, bim_ref[...])
    cr = coef_s[2:3, :]
    ci = coef_s[3:4, :]
    xr_s[...] = cr * bur - ci * bui
    xi_s[...] = cr * bui + ci * bur

    for lc in range(S5_STATE // lane_chunk):
        sl = slice(lc * lane_chunk, (lc + 1) * lane_chunk)
        ar = jnp.broadcast_to(coef_s[0:1, sl], (nb, lane_chunk))
        ai = jnp.broadcast_to(coef_s[1:2, sl], (nb, lane_chunk))

        def body(t, carry, sl=sl, ar=ar, ai=ai):
            hr, hi = carry
            r0 = pl.multiple_of(t * nb, 8)
            nhr = ar * hr - ai * hi + xr_s[pl.ds(r0, nb), sl]
            nhi = ar * hi + ai * hr + xi_s[pl.ds(r0, nb), sl]
            xr_s[pl.ds(r0, nb), sl] = nhr
            xi_s[pl.ds(r0, nb), sl] = nhi
            return nhr, nhi

        hr, hi = lax.fori_loop(0, tb, body, (st_r[:, sl], st_i[:, sl]), unroll=min(tb, 8))
        st_r[:, sl] = hr
        st_i[:, sl] = hi

    y = (_dot(xr_s[...].astype(BF16), cre_ref[...]) - _dot(xi_s[...].astype(BF16), cim_ref[...])
         + d_ref[...] * u)
    zg = 0.5 * y * (1.0 + jnp.tanh(math.sqrt(2.0 / math.pi) * (y + 0.044715 * (y * y * y))))
    gate = _sigmoid(_dot(zg.astype(BF16), wg_ref[...]) + bg_ref[...])
    if seq_major_io:
        yv = zg * gate
        for j in range(S5_WIDTH // LANES):
            y_s[j] = yv[:, j * LANES:(j + 1) * LANES]
        for b in range(nb):
            for j in range(S5_WIDTH // LANES):
                c0 = b * S5_WIDTH + j * LANES
                y_ref[:, c0:c0 + LANES] = y_s[j, pl.ds(b, tb, stride=nb), :].astype(BF16)
    else:
        y_ref[...] = (zg * gate).astype(BF16)

    @pl.when(step == pl.num_programs(0) - 1)
    def _():
        hr_out[...] = st_r[...]
        hi_out[...] = st_i[...]


def _s5(u, h0r, h0i, prm, tb, lane_chunk, seq_major_io):
    nb = h0r.shape[0]
    fix2 = lambda i: (0, 0)
    vec = pl.BlockSpec((1, S5_STATE), fix2)
    st = pl.BlockSpec((nb, S5_STATE), fix2)
    rows = tb * nb
    scratch = [pltpu.VMEM((rows, S5_STATE), F32), pltpu.VMEM((rows, S5_STATE), F32),
               pltpu.VMEM((nb, S5_STATE), F32), pltpu.VMEM((nb, S5_STATE), F32),
               pltpu.VMEM((8, S5_STATE), F32)]
    if seq_major_io:
        l = u.shape[1]
        u_spec = pl.BlockSpec((nb, tb, S5_WIDTH), lambda i: (0, i, 3072 // S5_WIDTH))
        y_shape = jax.ShapeDtypeStruct((l, nb * S5_WIDTH), BF16)
        y_spec = pl.BlockSpec((tb, nb * S5_WIDTH), lambda i: (i, 0))
        io_shape = (S5_WIDTH // LANES, rows, LANES)
        scratch += [pltpu.VMEM(io_shape, F32), pltpu.VMEM(io_shape, F32)]
    else:
        l = u.shape[0] // nb
        u_spec = pl.BlockSpec((rows, S5_WIDTH), lambda i: (i, 0))
        y_shape = jax.ShapeDtypeStruct((l * nb, S5_WIDTH), BF16)
        y_spec = pl.BlockSpec((rows, S5_WIDTH), lambda i: (i, 0))
    return pl.pallas_call(
        functools.partial(_s5_kernel, tb=tb, nb=nb, lane_chunk=lane_chunk, seq_major_io=seq_major_io),
        out_shape=(y_shape, jax.ShapeDtypeStruct((nb, S5_STATE), F32), jax.ShapeDtypeStruct((nb, S5_STATE), F32)),
        grid=(l // tb,),
        in_specs=[u_spec, st, st, vec, vec, vec,
                  pl.BlockSpec((S5_WIDTH, S5_STATE), fix2), pl.BlockSpec((S5_WIDTH, S5_STATE), fix2),
                  pl.BlockSpec((S5_STATE, S5_WIDTH), fix2), pl.BlockSpec((S5_STATE, S5_WIDTH), fix2),
                  pl.BlockSpec((1, S5_WIDTH), fix2), pl.BlockSpec((S5_WIDTH, S5_WIDTH), fix2),
                  pl.BlockSpec((1, S5_WIDTH), fix2)],
        out_specs=(y_spec, st, st),
        scratch_shapes=scratch,
        compiler_params=pltpu.CompilerParams(dimension_semantics=("arbitrary",),
                                             vmem_limit_bytes=VMEM_LIMIT),
        name="s5",
    )(u, h0r, h0i, prm["lre"], prm["lim"], prm["ldt"], prm["bre"], prm["bim"], prm["cre"],
      prm["cim"], prm["d"], prm["wg"], prm["bg"])


def _conv_silu(u, w8, fix):
    out = fix(pltpu.roll(u, 3, axis=0), 3) * w8[0:1, :]
    for s in (2, 1):
        out = out + fix(pltpu.roll(u, s, axis=0), s) * w8[3 - s:4 - s, :]
    out = out + u * w8[3:4, :]
    return out * _sigmoid(out)


def _l2norm_pairs(x_chunks, bones, scale):
    n = len(x_chunks)
    x = jnp.concatenate(x_chunks, axis=0) if n > 1 else x_chunks[0]
    out = [[None] * N_PAIRS for _ in range(n)]
    for p_ in range(N_PAIRS):
        xp = x[:, p_ * LANES:(p_ + 1) * LANES]
        xn = xp * (lax.rsqrt(_seg_mean(xp * xp, bones) * HEAD_DIM + EPS) * scale)
        for c in range(n):
            out[c][p_] = xn[c * CHUNK:(c + 1) * CHUNK]
    return out


def _head_norm_store(vals, gates, normw, bones, act, store):
    n = len(vals)
    for p_ in range(N_PAIRS):
        v = jnp.concatenate([vals[c][p_] for c in range(n)], axis=0) if n > 1 else vals[0][p_]
        g = jnp.concatenate([gates[c][p_] for c in range(n)], axis=0) if n > 1 else gates[0][p_]
        y = v * lax.rsqrt(_seg_mean(v * v, bones) + EPS) * normw * act(g)
        for c in range(n):
            store(c, p_, y[c * CHUNK:(c + 1) * CHUNK].astype(BF16))


def _dn_local(chunks, prm, sub, n_valid, prec):
    n = len(chunks)
    bones = prm["bones"]
    incl, strict = _pair_masks(sub)
    ltri, upair = _ltri(sub), _upair(sub)
    units = [(c, p_) for c in range(n) for p_ in range(N_PAIRS)]

    gates = []
    for ch in chunks:
        sg = _sigmoid(ch["zs"])
        gcomp = -jnp.exp(prm["alog_row"]) * _softplus(ch["zs"])
        g_rows = -jnp.exp(prm["rowalog"]) * _softplus(ch["rows"] + prm["rowadd"])
        if n_valid != sub:
            vc = _valid_col(sub, n_valid)
            sg = jnp.where(vc, sg, 0.0)
            gcomp = jnp.where(vc, gcomp, 0.0)
            g_rows = jnp.where(_valid_row(sub, n_valid), g_rows, 0.0)
        gates.append((sg, _mm_sel_l(ltri, gcomp), _mm_sel_r(g_rows, upair)))

    qn = _l2norm_pairs([ch["q"] for ch in chunks], bones, HEAD_DIM ** -0.5)
    kn = _l2norm_pairs([ch["k"] for ch in chunks], bones, 1.0)

    beta = [_expand(gates[c][0], 2 * p_, 2 * p_ + 1) for c, p_ in units]
    g_col = [_expand(gates[c][1], 6 + 2 * p_, 7 + 2 * p_) for c, p_ in units]
    dec = [jnp.exp(jnp.where(incl, gc - gates[c][2][p_:p_ + 1, :], NEG)) for gc, (c, p_) in zip(g_col, units)]
    kb = [kn[c][p_] * b for b, (c, p_) in zip(beta, units)]
    sc = [_mm(jnp.concatenate([kb_, qn[c][p_]], axis=0), _x2(kn[c][p_]), _NT)
          for kb_, (c, p_) in zip(kb, units)]
    a_in = [s[CHUNK:] * d for s, d in zip(sc, dec)]
    mbd = [_x2(jnp.where(strict, s[:CHUNK] * d, 0.0)) for s, d in zip(sc, dec)]
    tri_prec, newton, sol_prec = prec
    t_bd = _tri_inverse_units(mbd, sub, n_valid, tri_prec, newton)
    eg = [jnp.exp(gc) for gc in g_col]
    rhs = [jnp.concatenate([_x2(chunks[c]["v"][:, p_ * LANES:(p_ + 1) * LANES] * b), _x2(kb_ * e)], axis=1)
           for b, kb_, e, (c, p_) in zip(beta, kb, eg, units)]
    sol = [_mm(t, r, prec=sol_prec) for t, r in zip(t_bd, rhs)]
    out = [[None] * N_PAIRS for _ in range(n)]
    for i, (c, p_) in enumerate(units):
        g_last = _last_row_bcast(g_col[i], sub)
        out[c][p_] = dict(u=_fold(sol[i][:, :LANES]), w=_fold(sol[i][:, LANES:]), a_in=a_in[i],
                          qg=qn[c][p_] * eg[i], kd=kn[c][p_] * jnp.exp(g_last - g_col[i]),
                          eg_last=jnp.exp(g_last))
    return out


def _silu(x):
    return x * _sigmoid(x)


def _dn_seq_kernel(zq_ref, zk_ref, zv_ref, zg_ref, zs_ref, rows_ref, cp_ref, cw_ref, s0_ref,
                   add_ref, alog_ref, rowadd_ref, rowalog_ref, nw_ref,
                   o_ref, s_out_ref, s_scr, prev_scr, *, bb, prec):
    tstep = pl.program_id(1)

    @pl.when(tstep == 0)
    def _():
        s_scr[...] = s0_ref[...]
        prev_scr[...] = cp_ref[...]

    r8 = _iota((8, CONV_CH), 0)
    cw = cw_ref[...]
    chunks = []
    for b in range(bb):
        prev = prev_scr[b]

        def fix(rolled, s, prev=prev):
            first = jnp.where(r8 < s, pltpu.roll(prev, s, axis=0), rolled[0:8])
            return jnp.concatenate([first, rolled[8:]], axis=0)

        u_b = jnp.concatenate([zq_ref[b], zk_ref[b], zv_ref[b]], axis=1)
        qkv = _conv_silu(u_b, cw, fix)
        prev_scr[b] = u_b[CHUNK - 8:CHUNK]
        chunks.append(dict(q=qkv[:, 0:HEADS_W], k=qkv[:, HEADS_W:2 * HEADS_W], v=qkv[:, 2 * HEADS_W:],
                           zs=zs_ref[b] + add_ref[...], rows=rows_ref[b, 0]))

    prm = dict(bones=_block_ones(), alog_row=alog_ref[...], rowadd=rowadd_ref[...],
               rowalog=rowalog_ref[...])
    loc = _dn_local(chunks, prm, CHUNK, CHUNK, prec)
    bdm = _bd_mask()
    units = [(b, p_) for b in range(bb) for p_ in range(N_PAIRS)]
    s_prev = [s_scr[b, p_] for b, p_ in units]
    ws = [_mm(jnp.concatenate([loc[b][p_]["w"], loc[b][p_]["qg"]], axis=0), s)
          for s, (b, p_) in zip(s_prev, units)]
    v_new = [loc[b][p_]["u"] - w_[:CHUNK] for w_, (b, p_) in zip(ws, units)]
    o = [w_[CHUNK:] + _mm(loc[b][p_]["a_in"], _x2(v)) for w_, v, (b, p_) in zip(ws, v_new, units)]
    upd = [_mm(loc[b][p_]["kd"].T, v) for v, (b, p_) in zip(v_new, units)]
    for i, (b, p_) in enumerate(units):
        s_scr[b, p_] = loc[b][p_]["eg_last"][0:1, :] * s_prev[i] + jnp.where(bdm, upd[i], 0.0)

    def store(b, p_, val):
        o_ref[b, :, p_ * LANES:(p_ + 1) * LANES] = val

    _head_norm_store([[o[b * N_PAIRS + p_] for p_ in range(N_PAIRS)] for b in range(bb)],
                     [[zg_ref[b, :, p_ * LANES:(p_ + 1) * LANES] for p_ in range(N_PAIRS)] for b in range(bb)],
                     nw_ref[...], prm["bones"], _silu, store)

    @pl.when(tstep == pl.num_programs(1) - 1)
    def _():
        s_out_ref[...] = s_scr[...]


def _dn_batch_kernel(zq_ref, zk_ref, zv_ref, zg_ref, zs_ref, rows_ref, cp_ref, cw_ref, s0_ref,
                     add_ref, alog_ref, rowadd_ref, rowalog_ref, nw_ref,
                     o_ref, s_out_ref, *, nc, n_valid, prec):
    rb = nc * CHUNK
    pall = cp_ref[...]
    rr = _imod(_iota((rb, CONV_CH), 0), SUB)

    def fix(rolled, s):
        return jnp.where(rr < s, pltpu.roll(pall, (rb - SUB + s) % rb, axis=0), rolled)

    u_all = jnp.concatenate([zq_ref[...], zk_ref[...], zv_ref[...]], axis=1)
    qkv = _conv_silu(u_all, cw_ref[...], fix)
    zs_all = zs_ref[...] + add_ref[...]
    chunks = []
    for c in range(nc):
        rs = slice(c * CHUNK, (c + 1) * CHUNK)
        chunks.append(dict(q=qkv[rs, 0:HEADS_W], k=qkv[rs, HEADS_W:2 * HEADS_W], v=qkv[rs, 2 * HEADS_W:],
                           zs=zs_all[rs], rows=rows_ref[c]))

    prm = dict(bones=_block_ones(), alog_row=alog_ref[...], rowadd=rowadd_ref[...],
               rowalog=rowalog_ref[...])
    loc = _dn_local(chunks, prm, SUB, n_valid, prec)
    bdm = _bd_mask()
    nsub = CHUNK // SUB
    lane_sub = _idiv(_iota((LANES, CHUNK), 1), SUB)
    units = [(c, p_) for c in range(nc) for p_ in range(N_PAIRS)]
    subs = [(c, p_, sb) for c, p_ in units for sb in range(nsub)]
    s_prev = {k_: _bd_from_stacked(_read_pair(s0_ref, k_[0] * nsub + k_[2], k_[1]), bdm) for k_ in subs}
    rsl = lambda sb: slice(sb * SUB, (sb + 1) * SUB)
    ws = {k_: _mm(jnp.concatenate([loc[k_[0]][k_[1]]["w"][rsl(k_[2])], loc[k_[0]][k_[1]]["qg"][rsl(k_[2])]],
                                  axis=0), s_prev[k_]) for k_ in subs}
    v_new = {u_: jnp.concatenate([loc[u_[0]][u_[1]]["u"][rsl(sb)] - ws[u_ + (sb,)][:SUB] for sb in range(nsub)],
                                 axis=0) for u_ in units}
    o = {u_: jnp.concatenate([ws[u_ + (sb,)][SUB:] for sb in range(nsub)], axis=0)
         + _mm(loc[u_[0]][u_[1]]["a_in"], _x2(v_new[u_])) for u_ in units}
    kd_t = {u_: loc[u_[0]][u_[1]]["kd"].T for u_ in units}
    for k_ in subs:
        c, p_, sb = k_
        upd = _mm(jnp.where(lane_sub == sb, kd_t[(c, p_)], 0.0), v_new[(c, p_)])
        eg = loc[c][p_]["eg_last"][sb * SUB:sb * SUB + 1, :]
        _write_pair(s_out_ref, c * nsub + sb, p_,
                    _stacked_from_bd(eg * s_prev[k_] + jnp.where(bdm, upd, 0.0)))

    def store(c, p_, val):
        o_ref[c * CHUNK:(c + 1) * CHUNK, p_ * LANES:(p_ + 1) * LANES] = val

    _head_norm_store([[o[(c, p_)] for p_ in range(N_PAIRS)] for c in range(nc)],
                     [[zg_ref[c * CHUNK:(c + 1) * CHUNK, p_ * LANES:(p_ + 1) * LANES] for p_ in range(N_PAIRS)]
                      for c in range(nc)],
                     nw_ref[...], prm["bones"], _silu, store)


def _deltanet_seq(z3, rows4, convp, s0, prm, bb, prec):
    batch, seq_len, _ = z3.shape
    fix2 = lambda g, i: (0, 0)
    zspec = lambda width, col: pl.BlockSpec((bb, CHUNK, width), lambda g, i: (g, i, col))
    s_spec = pl.BlockSpec((bb, N_PAIRS, LANES, LANES), lambda g, i: (g, 0, 0, 0))
    in_specs = [zspec(HEADS_W, 0), zspec(HEADS_W, 1), zspec(HEADS_W, 2), zspec(HEADS_W, 3), zspec(LANES, 26),
                pl.BlockSpec((bb, 1, 16, LANES), lambda g, i: (g, i, 0, 0)),
                pl.BlockSpec((bb, 8, CONV_CH), lambda g, i: (g, 0, 0)),
                pl.BlockSpec((8, CONV_CH), fix2), s_spec,
                pl.BlockSpec((1, LANES), fix2), pl.BlockSpec((1, LANES), fix2),
                pl.BlockSpec((16, LANES), fix2), pl.BlockSpec((16, LANES), fix2),
                pl.BlockSpec((1, LANES), fix2)]
    return pl.pallas_call(
        functools.partial(_dn_seq_kernel, bb=bb, prec=prec),
        out_shape=(jax.ShapeDtypeStruct((batch, seq_len, HEADS_W), BF16), jax.ShapeDtypeStruct(s0.shape, F32)),
        grid=(batch // bb, seq_len // CHUNK),
        in_specs=in_specs,
        out_specs=(pl.BlockSpec((bb, CHUNK, HEADS_W), lambda g, i: (g, i, 0)), s_spec),
        scratch_shapes=[pltpu.VMEM((bb, N_PAIRS, LANES, LANES), F32), pltpu.VMEM((bb, 8, CONV_CH), F32)],
        compiler_params=pltpu.CompilerParams(dimension_semantics=("parallel", "arbitrary"),
                                             vmem_limit_bytes=VMEM_LIMIT),
        name="deltanet_seq",
    )(z3, z3, z3, z3, z3, rows4, convp, prm["conv_w8"], s0, prm["add_row"], prm["alog_row"],
      prm["rowadd"], prm["rowalog"], prm["dn_normw"])


def _state_specs(nsb, layer):
    shape = (nsb, N_HEADS, HEAD_DIM, HEAD_DIM)
    return (pl.BlockSpec((None,) + shape, lambda i: (layer, i, 0, 0, 0)),
            pl.BlockSpec(shape, lambda i: (i, 0, 0, 0)))


def _deltanet_batch(z, rows, convp, s0, layer, prm, nc, n_valid, prec):
    t = z.shape[0]
    rb = nc * CHUNK
    nsb = rb // SUB
    fix2 = lambda i: (0, 0)
    zspec = lambda width, col: pl.BlockSpec((rb, width), lambda i: (i, col))
    s_in, s_spec = _state_specs(nsb, layer)
    in_specs = [zspec(HEADS_W, 0), zspec(HEADS_W, 1), zspec(HEADS_W, 2), zspec(HEADS_W, 3), zspec(LANES, 26),
                pl.BlockSpec((nc, 16, LANES), lambda i: (i, 0, 0)),
                pl.BlockSpec((rb, CONV_CH), lambda i: (i, 0)),
                pl.BlockSpec((8, CONV_CH), fix2), s_in,
                pl.BlockSpec((1, LANES), fix2), pl.BlockSpec((1, LANES), fix2),
                pl.BlockSpec((16, LANES), fix2), pl.BlockSpec((16, LANES), fix2),
                pl.BlockSpec((1, LANES), fix2)]
    return pl.pallas_call(
        functools.partial(_dn_batch_kernel, nc=nc, n_valid=n_valid, prec=prec),
        out_shape=(jax.ShapeDtypeStruct((t, HEADS_W), BF16), jax.ShapeDtypeStruct(s0.shape[1:], F32)),
        grid=(t // rb,),
        in_specs=in_specs,
        out_specs=(pl.BlockSpec((rb, HEADS_W), lambda i: (i, 0)), s_spec),
        compiler_params=pltpu.CompilerParams(dimension_semantics=("parallel",),
                                             vmem_limit_bytes=VMEM_LIMIT),
        name="deltanet_batch",
    )(z, z, z, z, z, rows, convp, prm["conv_w8"], s0, prm["add_row"], prm["alog_row"],
      prm["rowadd"], prm["rowalog"], prm["dn_normw"])


def _ml_units(chunks, prm, sub, n_valid):
    n = len(chunks)
    incl, _ = _pair_masks(sub)
    lo = _lane_lo((CHUNK, LANES))
    ltri, upair = _ltri(sub), _upair(sub)
    ones2 = prm["bones"]
    bdm2 = jnp.concatenate([_bd_mask(), _bd_mask()], axis=1)
    nsub = CHUNK // sub
    units = [(c, p_) for c in range(n) for p_ in range(N_PAIRS)]
    pair = lambda x, p_: x[:, p_ * LANES:(p_ + 1) * LANES]

    comp = []
    for ch in chunks:
        rows = ch["rows"] + prm["rowadd"]
        ig_c, ls_c = ch["zs"], -_softplus(-ch["zs"])
        ig_r, ls_r = rows, -_softplus(-rows)
        if n_valid != sub:
            vc, vr = _valid_col(sub, n_valid), _valid_row(sub, n_valid)
            ig_c, ls_c = jnp.where(vc, ig_c, NEG), jnp.where(vc, ls_c, 0.0)
            ig_r, ls_r = jnp.where(vr, ig_r, NEG), jnp.where(vr, ls_r, 0.0)
        comp.append(dict(ig_c=ig_c, b_c=_mm_sel_l(ltri, ls_c), ig_r=ig_r, b_r=_mm_sel_r(ls_r, upair)))

    qp = [pair(chunks[c]["q"], p_) for c, p_ in units]
    kp = [pair(chunks[c]["k"], p_) * (HEAD_DIM ** -0.5) for c, p_ in units]
    vp = [pair(chunks[c]["v"], p_) for c, p_ in units]
    b_col = [_expand(comp[c]["b_c"], 18 + 2 * p_, 19 + 2 * p_) for c, p_ in units]
    ig_col = [_expand(comp[c]["ig_c"], 12 + 2 * p_, 13 + 2 * p_) for c, p_ in units]
    dl = [jnp.where(incl, bc - comp[c]["b_r"][6 + p_:7 + p_, :] + comp[c]["ig_r"][3 + p_:4 + p_, :], NEG)
          for bc, (c, p_) in zip(b_col, units)]
    mx = [jnp.where(lo, jnp.max(jnp.where(lo, d, NEG), axis=1, keepdims=True),
                    jnp.max(jnp.where(lo, NEG, d), axis=1, keepdims=True)) for d in dl]
    inter = [bc + chunks[c]["ms"][p_] for bc, (c, p_) in zip(b_col, units)]
    mt = [jnp.maximum(a, b) for a, b in zip(inter, mx)]
    s = [_mm(q_, _x2(k_), _NT) * jnp.exp(d - m) for q_, k_, d, m in zip(qp, kp, dl, mt)]
    sv = [_mm(s_, jnp.concatenate([_x2(v_), ones2], axis=1)) for s_, v_ in zip(s, vp)]
    cn_prev = [[chunks[c]["cn"](p_, sb) for sb in range(nsub)] for c, p_ in units]
    qcn = [jnp.concatenate([_mm(q_[sb * sub:(sb + 1) * sub], cn[sb]) for sb in range(nsub)], axis=0)
           if nsub > 1 else _mm(q_, cn[0]) for q_, cn in zip(qp, cn_prev)]
    w_inter = [jnp.exp(a - m) for a, m in zip(inter, mt)]
    h = [(wi * qc[:, :LANES] + sv_[:, :LANES])
         / jnp.maximum(jnp.abs(wi * qc[:, LANES:] + sv_[:, LANES:]), jnp.exp(-m))
         for wi, qc, sv_, m in zip(w_inter, qcn, sv, mt)]
    m_end = [_last_row_bcast(m, sub) for m in mt]
    b_last = [_last_row_bcast(bc, sub) for bc in b_col]
    a_end = [jnp.exp(bl + chunks[c]["ms"][p_] - me) for bl, me, (c, p_) in zip(b_last, m_end, units)]
    kw_t = [(k_ * jnp.exp(bl - bc + ig - me)).T for k_, bl, bc, ig, me in zip(kp, b_last, b_col, ig_col, m_end)]
    v1 = [jnp.concatenate([v_, jnp.ones((CHUNK, LANES), F32)], axis=1) for v_ in vp]
    lane_sub = _idiv(_iota((LANES, CHUNK), 1), sub) if nsub > 1 else None
    out = [[None] * N_PAIRS for _ in range(n)]
    for i, (c, p_) in enumerate(units):
        new_states, new_m = [], []
        for sb in range(nsub):
            lhs = kw_t[i] if nsub == 1 else jnp.where(lane_sub == sb, kw_t[i], 0.0)
            upd = _mm(lhs, v1[i])
            a_row = a_end[i][sb * sub:sb * sub + 1, :]
            a2 = jnp.concatenate([a_row, a_row], axis=1)
            new_states.append(a2 * cn_prev[i][sb] + jnp.where(bdm2, upd, 0.0))
            new_m.append(m_end[i][sb * sub:sb * sub + 1, :])
        out[c][p_] = (h[i], new_states, new_m)
    return out


def _ml_seq_kernel(zq_ref, zk_ref, zv_ref, zo_ref, zs_ref, rows_ref, cn0_ref, m0_ref, add_ref, rowadd_ref,
                   nw_ref, o_ref, cn_out_ref, m_out_ref, cn_scr, m_scr, *, bb):
    tstep = pl.program_id(1)

    @pl.when(tstep == 0)
    def _():
        cn_scr[...] = cn0_ref[...]
        m_scr[...] = m0_ref[...]

    prm = dict(bones=_block_ones(), rowadd=rowadd_ref[...])
    chunks = []
    for b in range(bb):
        chunks.append(dict(q=zq_ref[b], k=zk_ref[b], v=zv_ref[b], zs=zs_ref[b] + add_ref[...],
                           rows=rows_ref[b, 0],
                           ms=[jnp.broadcast_to(m_scr[b, p_:p_ + 1, :], (CHUNK, LANES)) for p_ in range(N_PAIRS)],
                           cn=lambda p_, sb, b=b: cn_scr[b, p_]))
    res = _ml_units(chunks, prm, CHUNK, CHUNK)
    for b in range(bb):
        for p_ in range(N_PAIRS):
            cn_scr[b, p_] = res[b][p_][1][0]
            m_scr[b, p_:p_ + 1, :] = res[b][p_][2][0]

    def store(b, p_, val):
        o_ref[b, :, p_ * LANES:(p_ + 1) * LANES] = val

    _head_norm_store([[res[b][p_][0] for p_ in range(N_PAIRS)] for b in range(bb)],
                     [[zo_ref[b, :, p_ * LANES:(p_ + 1) * LANES] for p_ in range(N_PAIRS)] for b in range(bb)],
                     nw_ref[...], prm["bones"], _sigmoid, store)

    @pl.when(tstep == pl.num_programs(1) - 1)
    def _():
        cn_out_ref[...] = cn_scr[...]
        m_out_ref[...] = m_scr[...]


def _ml_batch_kernel(zq_ref, zk_ref, zv_ref, zo_ref, zs_ref, rows_ref, c0_ref, n0_ref, m0_ref, add_ref,
                     rowadd_ref, nw_ref, o_ref, c_out_ref, n_out_ref, m_out_ref, *, nc, n_valid):
    prm = dict(bones=_block_ones(), rowadd=rowadd_ref[...])
    bdm = _bd_mask()
    zs_all = zs_ref[...] + add_ref[...]
    nsub = CHUNK // SUB

    def cn_of(c, p_, sb):
        i = c * nsub + sb
        return jnp.concatenate([_bd_from_stacked(_read_pair(c0_ref, i, p_), bdm),
                                _bd_from_stacked(_read_pair(n0_ref, i, p_), bdm)], axis=1)

    chunks = []
    for c in range(nc):
        rs = slice(c * CHUNK, (c + 1) * CHUNK)
        ms = [jnp.concatenate([jnp.broadcast_to(m0_ref[c * nsub + sb, p_:p_ + 1, :], (SUB, LANES))
                               for sb in range(nsub)], axis=0) for p_ in range(N_PAIRS)]
        chunks.append(dict(q=zq_ref[rs, :], k=zk_ref[rs, :], v=zv_ref[rs, :], zs=zs_all[rs], rows=rows_ref[c],
                           ms=ms, cn=functools.partial(cn_of, c)))
    res = _ml_units(chunks, prm, SUB, n_valid)
    for c in range(nc):
        for sb in range(nsub):
            i = c * nsub + sb
            for p_ in range(N_PAIRS):
                new_state = res[c][p_][1][sb]
                _write_pair(c_out_ref, i, p_, _stacked_from_bd(new_state[:, :LANES]))
                _write_pair(n_out_ref, i, p_, _stacked_from_bd(new_state[:, LANES:]))
            m_tile = jnp.zeros((8, LANES), F32)
            for p_ in range(N_PAIRS):
                m_tile = jnp.where(_iota((8, LANES), 0) == p_,
                                   jnp.broadcast_to(res[c][p_][2][sb], (8, LANES)), m_tile)
            m_out_ref[i] = m_tile

    def store(c, p_, val):
        o_ref[c * CHUNK:(c + 1) * CHUNK, p_ * LANES:(p_ + 1) * LANES] = val

    _head_norm_store([[res[c][p_][0] for p_ in range(N_PAIRS)] for c in range(nc)],
                     [[zo_ref[c * CHUNK:(c + 1) * CHUNK, p_ * LANES:(p_ + 1) * LANES] for p_ in range(N_PAIRS)]
                      for c in range(nc)],
                     nw_ref[...], prm["bones"], _sigmoid, store)


def _mlstm_seq(z3, rows4, cn0, m0, prm, bb):
    batch, seq_len, _ = z3.shape
    fix2 = lambda g, i: (0, 0)
    zspec = lambda width, col: pl.BlockSpec((bb, CHUNK, width), lambda g, i: (g, i, col))
    cn_spec = pl.BlockSpec((bb, N_PAIRS, LANES, 2 * LANES), lambda g, i: (g, 0, 0, 0))
    m_spec = pl.BlockSpec((bb, 8, LANES), lambda g, i: (g, 0, 0))
    in_specs = [zspec(HEADS_W, 4), zspec(HEADS_W, 5), zspec(HEADS_W, 6), zspec(HEADS_W, 7), zspec(LANES, 26),
                pl.BlockSpec((bb, 1, 16, LANES), lambda g, i: (g, i, 0, 0)), cn_spec, m_spec,
                pl.BlockSpec((1, LANES), fix2), pl.BlockSpec((16, LANES), fix2), pl.BlockSpec((1, LANES), fix2)]
    return pl.pallas_call(
        functools.partial(_ml_seq_kernel, bb=bb),
        out_shape=(jax.ShapeDtypeStruct((batch, seq_len, HEADS_W), BF16), jax.ShapeDtypeStruct(cn0.shape, F32),
                   jax.ShapeDtypeStruct(m0.shape, F32)),
        grid=(batch // bb, seq_len // CHUNK),
        in_specs=in_specs,
        out_specs=(pl.BlockSpec((bb, CHUNK, HEADS_W), lambda g, i: (g, i, 0)), cn_spec, m_spec),
        scratch_shapes=[pltpu.VMEM((bb, N_PAIRS, LANES, 2 * LANES), F32), pltpu.VMEM((bb, 8, LANES), F32)],
        compiler_params=pltpu.CompilerParams(dimension_semantics=("parallel", "arbitrary"),
                                             vmem_limit_bytes=VMEM_LIMIT),
        name="mlstm_seq",
    )(z3, z3, z3, z3, z3, rows4, cn0, m0, prm["add_row"], prm["rowadd"], prm["ml_normw"])


def _mlstm_batch(z, rows, c0, layer, n0, m0, prm, nc, n_valid):
    t = z.shape[0]
    rb = nc * CHUNK
    nsb = rb // SUB
    fix2 = lambda i: (0, 0)
    zspec = lambda width, col: pl.BlockSpec((rb, width), lambda i: (i, col))
    c_in, st_spec = _state_specs(nsb, layer)
    m_spec = pl.BlockSpec((nsb, 8, LANES), lambda i: (i, 0, 0))
    in_specs = [zspec(HEADS_W, 4), zspec(HEADS_W, 5), zspec(HEADS_W, 6), zspec(HEADS_W, 7), zspec(LANES, 26),
                pl.BlockSpec((nc, 16, LANES), lambda i: (i, 0, 0)), c_in, st_spec, m_spec,
                pl.BlockSpec((1, LANES), fix2), pl.BlockSpec((16, LANES), fix2), pl.BlockSpec((1, LANES), fix2)]
    return pl.pallas_call(
        functools.partial(_ml_batch_kernel, nc=nc, n_valid=n_valid),
        out_shape=(jax.ShapeDtypeStruct((t, HEADS_W), BF16), jax.ShapeDtypeStruct(n0.shape, F32),
                   jax.ShapeDtypeStruct(n0.shape, F32), jax.ShapeDtypeStruct(m0.shape, F32)),
        grid=(t // rb,),
        in_specs=in_specs,
        out_specs=(pl.BlockSpec((rb, HEADS_W), lambda i: (i, 0)), st_spec, st_spec, m_spec),
        compiler_params=pltpu.CompilerParams(dimension_semantics=("parallel",),
                                             vmem_limit_bytes=VMEM_LIMIT),
        name="mlstm_batch",
    )(z, z, z, z, z, rows, c0, n0, m0, prm["add_row"], prm["rowadd"], prm["ml_normw"])


def _pack_w_in(w_in):
    offs = np.concatenate([[0], np.cumsum(IN_SIZES)])
    seg = lambda i: w_in[..., int(offs[i]):int(offs[i + 1])].astype(BF16)
    small = jnp.concatenate([seg(3), seg(4), seg(9), seg(10)], axis=-1)
    small = jnp.pad(small, ((0, 0),) * (w_in.ndim - 1) + ((0, LANES - 4 * N_HEADS),))
    cols = [seg(0), seg(1), seg(2), seg(5), seg(6), seg(7), seg(8), seg(11), seg(12), small]
    return jnp.concatenate(cols, axis=-1)


def _pair_rows(v6):
    return jnp.repeat(v6.reshape(N_PAIRS, 2), HEAD_DIM, axis=1).reshape(N_PAIRS, LANES)


def _layer_params(l, a):
    f = lambda name: a[name][l].astype(F32)
    zeros6 = jnp.zeros((N_HEADS,), F32)
    add_row = jnp.concatenate([zeros6, f("dn_dt_bias"), f("ml_i_bias"), f("ml_f_bias")])
    add_row = jnp.pad(add_row, (0, LANES - 4 * N_HEADS))[None, :]
    alog_row = jnp.pad(jnp.concatenate([zeros6, f("dn_a_log")]), (0, LANES - 2 * N_HEADS))[None, :]
    rowadd = jnp.concatenate([_pair_rows(f("dn_dt_bias")), _pair_rows(f("ml_i_bias")),
                              _pair_rows(f("ml_f_bias")), jnp.zeros((7, LANES), F32)], axis=0)
    rowalog = jnp.concatenate([_pair_rows(f("dn_a_log")), jnp.zeros((13, LANES), F32)], axis=0)
    eye_g = jnp.eye(S5_GROUPS, dtype=F32)
    bd_in = lambda b: jnp.einsum("gpc,gh->gchp", b, eye_g).reshape(S5_WIDTH, S5_STATE).astype(BF16)
    bd_out = lambda c: jnp.einsum("gcp,gh->gphc", c, eye_g).reshape(S5_STATE, S5_WIDTH).astype(BF16)
    return dict(
        layer=l, norm_mix=f("norm_mix")[None, :],
        add_row=add_row, alog_row=alog_row, rowadd=rowadd, rowalog=rowalog,
        conv_w8=jnp.pad(f("dn_conv_w"), ((0, 8 - CONV_W), (0, 0))),
        dn_normw=jnp.tile(f("dn_norm"), 2)[None, :], ml_normw=jnp.tile(f("ml_norm"), 2)[None, :],
        s5=dict(lre=f("s5_lam_re").reshape(1, S5_STATE), lim=f("s5_lam_im").reshape(1, S5_STATE),
                ldt=jnp.repeat(f("s5_log_dt"), S5_P)[None, :],
                bre=bd_in(f("s5_b_re")), bim=bd_in(f("s5_b_im")),
                cre=bd_out(f("s5_c_re")), cim=bd_out(f("s5_c_im")),
                d=f("s5_d").reshape(1, S5_WIDTH), wg=f("s5_w_glu").astype(BF16),
                bg=f("s5_b_glu")[None, :]),
        norm_x=f("norm_x")[None, :], norm_ff=f("norm_ff")[None, :],
    )


def _shared_weights(a):
    cast = lambda name: a[name].astype(BF16)
    return dict(w_in=_pack_w_in(a["w_in"]), w_out=cast("w_out"), w_xq=cast("w_xq"), w_xo=cast("w_xo"),
                w_ff1=cast("w_ff1"), w_ff2=cast("w_ff2"), w_xk=cast("w_xk"), w_xv=cast("w_xv"))


def _row_form(z, n_chunks):
    zs = z[..., Z_COLS - LANES:Z_COLS - LANES + 4 * N_HEADS].reshape(n_chunks, CHUNK, 4, N_PAIRS, 2)
    r = jnp.transpose(zs[:, :, 1:], (0, 2, 3, 4, 1)).reshape(n_chunks, 9, LANES)
    return jnp.pad(r, ((0, 0), (0, 7), (0, 0)))


def _un_bd_pairs(s):
    n = s.shape[0]
    a = s[:, :, :HEAD_DIM, :HEAD_DIM]
    b = s[:, :, HEAD_DIM:, HEAD_DIM:]
    return jnp.stack([a, b], axis=2).reshape(n, N_HEADS, HEAD_DIM, HEAD_DIM)


def _pack_m(m):
    s = m.shape[0]
    r = jnp.repeat(m.reshape(s, N_PAIRS, 2), HEAD_DIM, axis=2).reshape(s, N_PAIRS, LANES)
    return jnp.pad(r, ((0, 0), (0, 8 - N_PAIRS), (0, 0)))


def _unpack_m(mp):
    return mp[:, :N_PAIRS, ::HEAD_DIM].reshape(mp.shape[0], N_HEADS)


ROW_TILE = 512
SEQ_BATCH = 8
S5_STEPS = 64
SAMPLE_CHUNKS = 2
ATTN_BATCH = 8
SEQ_SOLVE = (1, True, 1)
SAMPLE_SOLVE = (3, False, 1)


def _tile(n, pref):
    tile = min(n, pref)
    assert n % tile == 0, (n, pref)
    return tile


def _layer_prompt(x, mem_k, mem_v, prm, wts, final_g):
    batch, seq_len, _ = x.shape
    tm = _tile(seq_len, ROW_TILE)
    bb = _tile(batch, SEQ_BATCH)
    layer = prm["layer"]
    z3 = _norm_matmul(x, prm["norm_mix"], wts["w_in"], layer, tm)
    rows4 = _row_form(z3, batch * seq_len // CHUNK).reshape(batch, seq_len // CHUNK, 16, LANES)
    new_conv = z3[:, seq_len - 3:, :CONV_CH]

    dn, s_new = _deltanet_seq(z3, rows4, jnp.zeros((batch, 8, CONV_CH), F32),
                              jnp.zeros((batch, N_PAIRS, LANES, LANES), F32), prm, bb, SEQ_SOLVE)
    ml, cn_new, m_new = _mlstm_seq(z3, rows4, jnp.zeros((batch, N_PAIRS, LANES, 2 * LANES), F32),
                                   jnp.zeros((batch, 8, LANES), F32), prm, bb)
    zero_h = jnp.zeros((batch, S5_STATE), F32)
    s5o, hr, hi = _s5(z3, zero_h, zero_h, prm["s5"], _tile(seq_len, S5_STEPS), S5_STATE, True)

    x1, q = _out_proj(dn, ml, s5o, x, wts["w_out"], prm["norm_x"], wts["w_xq"], layer, tm)
    att = _attn_seq(q, mem_k, mem_v, tm)
    g_fin = final_g if final_g is not None else prm["norm_ff"]
    x3 = _ffn(x1, att, wts["w_xo"], prm["norm_ff"], wts["w_ff1"], wts["w_ff2"], g_fin, layer, tm,
              final_g is not None)
    st = (new_conv, _un_bd_pairs(s_new), _un_bd_pairs(cn_new[..., :LANES]),
          _un_bd_pairs(cn_new[..., LANES:])[..., 0], _unpack_m(m_new),
          hr.reshape(batch, S5_GROUPS, S5_P), hi.reshape(batch, S5_GROUPS, S5_P))
    return x3, st


def _layer_sample(x, cache_k, cache_v, conv_buf, dn_s, ml_c, ml_n, ml_m, ssm_re, ssm_im, prm, wts, batch,
                  seq_len, final_g):
    t = batch * seq_len
    layer = prm["layer"]
    z = _norm_matmul(x, prm["norm_mix"], wts["w_in"], layer, t)
    z3 = z.reshape(batch, seq_len, Z_COLS)
    new_conv = jnp.concatenate([conv_buf, z3[:, :, :CONV_CH]], axis=1)[:, seq_len:]
    zp = jnp.pad(z3, ((0, 0), (0, SUB - seq_len), (0, 0))).reshape(batch * SUB, Z_COLS)
    n_chunks = batch * SUB // CHUNK
    nc = _tile(n_chunks, SAMPLE_CHUNKS)
    rows = _row_form(zp, n_chunks)
    convp = jnp.pad(conv_buf, ((0, 0), (SUB - (CONV_W - 1), 0), (0, 0))).reshape(batch * SUB, CONV_CH)
    dn, s_new = _deltanet_batch(zp, rows, convp, dn_s, layer, prm, nc, seq_len, SAMPLE_SOLVE)
    n_b = jnp.broadcast_to(ml_n[..., None], ml_c.shape[1:])
    ml, c_new, n_new, m_new = _mlstm_batch(zp, rows, ml_c, layer, n_b, _pack_m(ml_m), prm, nc, seq_len)
    unpad = lambda a: a.reshape(batch, SUB, -1)[:, :seq_len].reshape(1, t, -1)
    dn, ml = unpad(dn), unpad(ml)

    u_tm = jnp.transpose(z3[:, :, 3072:3072 + S5_WIDTH], (1, 0, 2))
    y_tm, hr, hi = _s5(u_tm.reshape(t, S5_WIDTH), ssm_re.reshape(batch, S5_STATE),
                       ssm_im.reshape(batch, S5_STATE), prm["s5"], seq_len, LANES, False)
    s5o = jnp.transpose(y_tm.reshape(seq_len, batch, S5_WIDTH), (1, 0, 2)).reshape(t, S5_WIDTH)

    x1, q = _out_proj(dn, ml, s5o, x, wts["w_out"], prm["norm_x"], wts["w_xq"], layer, t)
    q3 = jnp.pad(q.reshape(batch, seq_len, X_WIDTH).astype(F32), ((0, 0), (0, SUB - seq_len), (0, 0)))
    att = _attn_batch(q3, cache_k, cache_v, layer, _tile(batch, ATTN_BATCH))[:, :seq_len].reshape(1, t, X_WIDTH)
    g_fin = final_g if final_g is not None else prm["norm_ff"]
    x3 = _ffn(x1, att, wts["w_xo"], prm["norm_ff"], wts["w_ff1"], wts["w_ff2"], g_fin, layer, t,
              final_g is not None)
    st = (new_conv, s_new, c_new, n_new[..., 0], _unpack_m(m_new),
          hr.reshape(batch, S5_GROUPS, S5_P), hi.reshape(batch, S5_GROUPS, S5_P))
    return x3, st


def kernel(x_prompt, x_sample, mem_prompt, state_dn_conv, state_dn_s, state_ml_c, state_ml_n, state_ml_m, state_ssm_re, state_ssm_im, cache_mem_k, cache_mem_v, norm_mix, w_in, dn_conv_w, dn_a_log, dn_dt_bias, dn_norm, ml_i_bias, ml_f_bias, ml_norm, s5_lam_re, s5_lam_im, s5_log_dt, s5_b_re, s5_b_im, s5_c_re, s5_c_im, s5_d, s5_w_glu, s5_b_glu, w_out, norm_x, norm_mem, w_xq, w_xk, w_xv, w_xo, norm_ff, w_ff1, w_ff2, norm_final):
    a = dict(norm_mix=norm_mix, w_in=w_in, dn_conv_w=dn_conv_w, dn_a_log=dn_a_log, dn_dt_bias=dn_dt_bias,
             dn_norm=dn_norm, ml_i_bias=ml_i_bias, ml_f_bias=ml_f_bias, ml_norm=ml_norm,
             s5_lam_re=s5_lam_re, s5_lam_im=s5_lam_im, s5_log_dt=s5_log_dt, s5_b_re=s5_b_re,
             s5_b_im=s5_b_im, s5_c_re=s5_c_re, s5_c_im=s5_c_im, s5_d=s5_d, s5_w_glu=s5_w_glu,
             s5_b_glu=s5_b_glu, w_out=w_out, norm_x=norm_x, w_xq=w_xq, w_xo=w_xo, norm_ff=norm_ff,
             w_ff1=w_ff1, w_ff2=w_ff2)
    prms = [_layer_params(l, a) for l in range(DEPTH)]
    wts = _shared_weights(dict(a, w_xk=w_xk, w_xv=w_xv))
    g_final = norm_final.astype(F32)[None, :]

    bp, lp, _ = x_prompt.shape
    bs, ls, _ = x_sample.shape

    xp = x_prompt
    p_st = []
    for l in range(DEPTH):
        mk, mk_t = _mem_proj(mem_prompt, norm_mem[l][None, :], wts["w_xk"], l)
        mv, mv_t = _mem_proj(mem_prompt, norm_mem[l][None, :], wts["w_xv"], l)
        xp, st = _layer_prompt(xp, mk, mv, prms[l], wts, g_final if l == DEPTH - 1 else None)
        p_st.append(st + (mk_t.reshape(bp, N_MEM, X_HEADS, X_HD), mv_t.reshape(bp, N_MEM, X_HEADS, X_HD)))
    y_prompt = xp

    xs = x_sample.reshape(1, bs * ls, D_MODEL)
    cache_k = cache_mem_k.reshape(DEPTH, bs, N_MEM * X_HEADS, X_HD)
    cache_v = cache_mem_v.reshape(DEPTH, bs, N_MEM * X_HEADS, X_HD)
    s_st = []
    for l in range(DEPTH):
        xs, st = _layer_sample(xs, cache_k, cache_v, state_dn_conv[l], state_dn_s, state_ml_c,
                               state_ml_n[l], state_ml_m[l], state_ssm_re[l], state_ssm_im[l], prms[l], wts, bs,
                               ls, g_final if l == DEPTH - 1 else None)
        s_st.append(st)
    y_sample = xs.reshape(bs, ls, D_MODEL)

    stack = lambda sts, i: jnp.stack([s[i] for s in sts])
    return (y_prompt, y_sample) + tuple(stack(p_st, i) for i in range(9)) + tuple(stack(s_st, i) for i in range(7))
```

```python
import functools
import math

import numpy as np
import jax
import jax.numpy as jnp
from jax import lax
from jax.experimental import pallas as pl
from jax.experimental.pallas import tpu as pltpu

F32 = jnp.float32
BF16 = jnp.bfloat16

D_MODEL = 1024
DEPTH = 2
N_HEADS = 6
HEAD_DIM = 64
N_PAIRS = N_HEADS // 2
HEADS_W = N_HEADS * HEAD_DIM
CONV_W = 4
CONV_CH = 3 * HEADS_W
S5_WIDTH = 256
S5_GROUP = 16
S5_GROUPS = 16
S5_P = 64
S5_STATE = S5_GROUPS * S5_P
N_MEM = 256
X_HEADS = 4
X_HD = 128
X_WIDTH = X_HEADS * X_HD
D_FF = 4 * D_MODEL
EPS = 1e-6
IN_SIZES = (384, 384, 384, 6, 6, 384, 384, 384, 384, 6, 6, 384, 256)

LANES = 128
CHUNK = 64
SUB = 8
NEG = -1e30
Z_COLS = 3456
VMEM_LIMIT = 52 * 1024 * 1024

_NN = (((1,), (0,)), ((), ()))
_NT = (((1,), (1,)), ((), ()))


def _dot(a, b, dims=_NN):
    return lax.dot_general(a, b, dims, preferred_element_type=F32)


def _split(a, n):
    out = []
    r = a
    for i in range(n):
        h = r.astype(BF16)
        out.append(h)
        if i + 1 < n:
            r = r - h.astype(F32)
    return out


def _mm(a, b, dims=_NN, prec=1):
    if prec == 1:
        return _dot(a.astype(BF16), b.astype(BF16), dims)
    a1, a2 = _split(a, 2)
    b1, b2 = _split(b, 2)
    return (_dot(a1, b2, dims) + _dot(a2, b1, dims)) + _dot(a1, b1, dims)


def _mm_sel_r(a, sel, n=3):
    parts = _split(a, n)
    acc = _dot(parts[-1], sel)
    for p_ in parts[-2::-1]:
        acc = acc + _dot(p_, sel)
    return acc


def _mm_sel_l(sel, a, n=3):
    parts = _split(a, n)
    acc = _dot(sel, parts[-1])
    for p_ in parts[-2::-1]:
        acc = acc + _dot(sel, p_)
    return acc


def _sigmoid(x):
    return 1.0 / (1.0 + jnp.exp(-x))


def _softplus(x):
    return jnp.maximum(x, 0.0) + jnp.log(1.0 + jnp.exp(-jnp.abs(x)))


def _rmsnorm_rows(x, g):
    return x * lax.rsqrt(jnp.mean(x * x, axis=-1, keepdims=True) + EPS) * g


def _iota(shape, dim):
    return lax.broadcasted_iota(jnp.int32, shape, dim)


def _idiv(x, n):
    return x >> (n.bit_length() - 1)


def _imod(x, n):
    return x & (n - 1)


def _lane_lo(shape):
    return _iota(shape, 1) < HEAD_DIM


def _x2(x):
    lo = _lane_lo(x.shape)
    return jnp.concatenate([jnp.where(lo, x, 0.0), jnp.where(lo, 0.0, x)], axis=0)


def _fold(y):
    r = y.shape[0] // 2
    return y[:r] + y[r:]


def _expand(a, c0, c1):
    r = a.shape[0]
    return jnp.where(_lane_lo((r, LANES)), a[:, c0:c0 + 1], a[:, c1:c1 + 1])


def _block_ones():
    i = _iota((LANES, LANES), 0)
    j = _iota((LANES, LANES), 1)
    return jnp.where((i < HEAD_DIM) == (j < HEAD_DIM), 1.0, 0.0).astype(BF16)


def _seg_mean(x, bones):
    return _mm_sel_r(x, bones, n=2) * (1.0 / HEAD_DIM)


def _ltri(sub):
    i = _iota((CHUNK, CHUNK), 0)
    j = _iota((CHUNK, CHUNK), 1)
    ok = (j <= i) & (_idiv(i, sub) == _idiv(j, sub)) if sub != CHUNK else (j <= i)
    return jnp.where(ok, 1.0, 0.0).astype(BF16)


def _upair(sub):
    a = _iota((LANES, LANES), 0)
    b = _iota((LANES, LANES), 1)
    ok = ((a < HEAD_DIM) == (b < HEAD_DIM)) & (a <= b)
    if sub != CHUNK:
        ok = ok & (_idiv(_imod(a, HEAD_DIM), sub) == _idiv(_imod(b, HEAD_DIM), sub))
    return jnp.where(ok, 1.0, 0.0).astype(BF16)


def _pair_masks(sub):
    i = _iota((CHUNK, LANES), 0)
    j = _imod(_iota((CHUNK, LANES), 1), HEAD_DIM)
    same = (_idiv(i, sub) == _idiv(j, sub)) if sub != CHUNK else (i >= 0)
    return same & (j <= i), same & (j < i)


def _valid_col(sub, n_valid):
    i = _iota((CHUNK, 1), 0)
    return _imod(i, sub) < n_valid


def _valid_row(sub, n_valid):
    j = _iota((1, LANES), 1)
    return _imod(j, sub) < n_valid


def _bd_mask():
    i = _iota((LANES, LANES), 0)
    j = _iota((LANES, LANES), 1)
    return (i < HEAD_DIM) == (j < HEAD_DIM)


def _parts(a, prec):
    return tuple(_split(a, 1 if prec == 1 else 2))


def _mmp(ap, bp, dims=_NN):
    acc = None
    if len(bp) > 1:
        acc = _dot(ap[0], bp[1], dims)
    if len(ap) > 1:
        cross = _dot(ap[1], bp[0], dims)
        acc = cross if acc is None else acc + cross
    lead = _dot(ap[0], bp[0], dims)
    return lead if acc is None else acc + lead


def _blk_eq(n, b):
    return _idiv(_iota((n, n), 0), b) == _idiv(_iota((n, n), 1), b)


TRI_BASE = 8


def _tri_inverse_units(mbds, sub, n_valid, prec, newton):
    n = LANES
    eye = jnp.where(_iota((n, n), 0) == _iota((n, n), 1), 1.0, 0.0)
    if sub == CHUNK:
        base, n_sq = TRI_BASE, TRI_BASE.bit_length() - 2
        blk = _blk_eq(n, base)
        m0 = [jnp.where(blk, m, 0.0) for m in mbds]
    else:
        base, n_sq = sub, max(0, math.ceil(math.log2(n_valid)) - 1)
        m0 = mbds
    ts = [eye - m for m in m0]
    if n_sq > 0:
        ps = [_mm(m, m, prec=prec) for m in m0]
        for it in range(n_sq):
            if it == n_sq - 1:
                ts = [t + _mm(t, p_, prec=prec) for t, p_ in zip(ts, ps)]
            else:
                both = [_mm(jnp.concatenate([t, p_], axis=0), p_, prec=prec) for t, p_ in zip(ts, ps)]
                ts = [t + b_[:n] for t, b_ in zip(ts, both)]
                ps = [b_[n:] for b_ in both]
    b = base
    while b < sub:
        outer = _blk_eq(n, 2 * b) & jnp.logical_not(_blk_eq(n, b))
        tp = [_parts(t, prec) for t in ts]
        xs = [_mmp(_parts(jnp.where(outer, m, 0.0), prec), t_) for m, t_ in zip(mbds, tp)]
        ts = [t - _mmp(t_, _parts(x, prec)) for t, t_, x in zip(ts, tp, xs)]
        b *= 2
    if newton:
        res = [eye - (t + _mm(m, t, prec=3)) for m, t in zip(mbds, ts)]
        ts = [t + _mm(t, r) for t, r in zip(ts, res)]
    return ts


def _last_row_bcast(a, sub):
    if sub == CHUNK:
        return jnp.broadcast_to(a[CHUNK - 1:CHUNK, :], a.shape)
    parts = [jnp.broadcast_to(a[s * sub + sub - 1:s * sub + sub, :], (sub, a.shape[1]))
             for s in range(CHUNK // sub)]
    return jnp.concatenate(parts, axis=0)


def _bd_from_stacked(s, bdm):
    return jnp.where(bdm, jnp.concatenate([s, s], axis=1), 0.0)


def _stacked_from_bd(s):
    return s[:, :HEAD_DIM] + s[:, HEAD_DIM:]


def _read_pair(ref, i, p_):
    return jnp.concatenate([ref[i, 2 * p_], ref[i, 2 * p_ + 1]], axis=0)


def _write_pair(ref, i, p_, val):
    ref[i, 2 * p_] = val[:HEAD_DIM]
    ref[i, 2 * p_ + 1] = val[HEAD_DIM:]


SUB_ROWS = 1024


def _wspec(rows, cols, layer, row_block=0):
    return pl.BlockSpec((None, rows, cols), lambda i: (layer, row_block, 0), pipeline_mode=pl.Buffered(1))


def _sub_tiles(tm):
    r = min(tm, SUB_ROWS)
    return [slice(s * r, (s + 1) * r) for s in range(tm // r)]


def _rows3(nt):
    return lambda i: (i // nt, i % nt, 0)


def _dense_call(kernel_fn, name, out_shape, in_specs, out_specs, steps, args):
    return pl.pallas_call(
        kernel_fn, out_shape=out_shape, grid=(steps,), in_specs=in_specs, out_specs=out_specs,
        compiler_params=pltpu.CompilerParams(dimension_semantics=("parallel",), vmem_limit_bytes=VMEM_LIMIT),
        name=name)(*args)


def _in_proj_kernel(x_ref, g_ref, w_ref, o_ref):
    for rs in _sub_tiles(x_ref.shape[1]):
        xn = _rmsnorm_rows(x_ref[0, rs, :], g_ref[...]).astype(BF16)
        o_ref[0, rs, :] = _dot(xn, w_ref[...])


def _norm_matmul(x, g, w, layer, tm):
    b, l, k = x.shape
    n = w.shape[2]
    nt = l // tm
    fix = lambda i: (0, 0)
    return _dense_call(
        _in_proj_kernel, "norm_matmul", jax.ShapeDtypeStruct((b, l, n), F32),
        [pl.BlockSpec((1, tm, k), _rows3(nt)), pl.BlockSpec((1, k), fix), _wspec(k, n, layer)],
        pl.BlockSpec((1, tm, n), _rows3(nt)), b * nt, (x, g, w))


def _mem_proj_kernel(x_ref, g_ref, w_ref, o_ref, t_ref):
    res = _dot(_rmsnorm_rows(x_ref[0], g_ref[...]).astype(BF16), w_ref[...])
    o_ref[0] = res
    for h in range(X_HEADS):
        t_ref[0, pl.ds(h, N_MEM, stride=X_HEADS), :] = res[:, h * X_HD:(h + 1) * X_HD]


def _mem_proj(mem, g, w, layer):
    b = mem.shape[0]
    fix = lambda i: (0, 0)
    row = lambda i: (i, 0, 0)
    return _dense_call(
        _mem_proj_kernel, "mem_proj",
        (jax.ShapeDtypeStruct((b, N_MEM, X_WIDTH), F32), jax.ShapeDtypeStruct((b, N_MEM * X_HEADS, X_HD), F32)),
        [pl.BlockSpec((1, N_MEM, D_MODEL), row), pl.BlockSpec((1, D_MODEL), fix),
         _wspec(D_MODEL, X_WIDTH, layer)],
        (pl.BlockSpec((1, N_MEM, X_WIDTH), row), pl.BlockSpec((1, N_MEM * X_HEADS, X_HD), row)), b,
        (mem, g, w))


def _out_proj_kernel(dn_ref, ml_ref, s5_ref, x_ref, w1_ref, w2_ref, w3_ref, g_ref, wq_ref,
                     x1_ref, q_ref):
    for rs in _sub_tiles(x_ref.shape[1]):
        acc = x_ref[0, rs, :] + _dot(dn_ref[0, rs, :], w1_ref[...])
        acc = acc + _dot(ml_ref[0, rs, :], w2_ref[...])
        acc = acc + _dot(s5_ref[rs, :], w3_ref[...])
        x1_ref[0, rs, :] = acc
        xn = _rmsnorm_rows(acc, g_ref[...]).astype(BF16)
        q_ref[0, rs, :] = _dot(xn, wq_ref[...]).astype(BF16)


def _out_proj(dn, ml, s5, x, w_out, g, wq, layer, tm):
    b, l, _ = x.shape
    nt = l // tm
    row = _rows3(nt)
    fix = lambda i: (0, 0)
    return _dense_call(
        _out_proj_kernel, "out_proj",
        (jax.ShapeDtypeStruct((b, l, D_MODEL), F32), jax.ShapeDtypeStruct((b, l, X_WIDTH), BF16)),
        [pl.BlockSpec((1, tm, HEADS_W), row), pl.BlockSpec((1, tm, HEADS_W), row),
         pl.BlockSpec((tm, S5_WIDTH), lambda i: (i % nt, i // nt)), pl.BlockSpec((1, tm, D_MODEL), row),
         _wspec(HEADS_W, D_MODEL, layer, 0), _wspec(HEADS_W, D_MODEL, layer, 1),
         _wspec(S5_WIDTH, D_MODEL, layer, 2 * HEADS_W // S5_WIDTH), pl.BlockSpec((1, D_MODEL), fix),
         _wspec(D_MODEL, X_WIDTH, layer)],
        (pl.BlockSpec((1, tm, D_MODEL), row), pl.BlockSpec((1, tm, X_WIDTH), row)), b * nt,
        (dn, ml, s5, x, w_out, w_out, w_out, g, wq))


def _ffn_kernel(x_ref, o_ref, wo_ref, g_ref, w1_ref, w2_ref, gf_ref, y_ref, *, final_norm, ff_chunk):
    for rs in _sub_tiles(x_ref.shape[1]):
        x2 = x_ref[0, rs, :] + _dot(o_ref[0, rs, :], wo_ref[...])
        xn = _rmsnorm_rows(x2, g_ref[...]).astype(BF16)
        acc = x2
        for c in range(D_FF // ff_chunk):
            hf = _dot(xn, w1_ref[:, c * ff_chunk:(c + 1) * ff_chunk])
            a = jnp.square(jnp.maximum(hf, 0.0)).astype(BF16)
            acc = acc + _dot(a, w2_ref[c * ff_chunk:(c + 1) * ff_chunk, :])
        if final_norm:
            acc = _rmsnorm_rows(acc, gf_ref[...])
        y_ref[0, rs, :] = acc


def _ffn(x, o, wo, g, w1, w2, gf, layer, tm, final_norm):
    b, l, _ = x.shape
    nt = l // tm
    row = _rows3(nt)
    fix = lambda i: (0, 0)
    return _dense_call(
        functools.partial(_ffn_kernel, final_norm=final_norm, ff_chunk=1024), "ffn",
        jax.ShapeDtypeStruct((b, l, D_MODEL), F32),
        [pl.BlockSpec((1, tm, D_MODEL), row), pl.BlockSpec((1, tm, X_WIDTH), row),
         _wspec(X_WIDTH, D_MODEL, layer), pl.BlockSpec((1, D_MODEL), fix),
         _wspec(D_MODEL, D_FF, layer), _wspec(D_FF, D_MODEL, layer),
         pl.BlockSpec((1, D_MODEL), fix)],
        pl.BlockSpec((1, tm, D_MODEL), row), b * nt, (x, o, wo, g, w1, w2, gf))


def _attn_heads(q, k_ref, v_ref, bi):
    outs = []
    for h in range(X_HEADS):
        sl = slice(h * X_HD, (h + 1) * X_HD)
        kh = k_ref[bi, :, sl].astype(BF16)
        vh = v_ref[bi, :, sl].astype(BF16)
        s = _dot(q[:, sl], kh, _NT) * (X_HD ** -0.5)
        e = jnp.exp(s - jnp.max(s, axis=-1, keepdims=True))
        p = e / jnp.sum(e, axis=-1, keepdims=True)
        outs.append(_dot(p.astype(BF16), vh))
    return jnp.concatenate(outs, axis=-1)


def _attn_seq_kernel(q_ref, k_ref, v_ref, o_ref):
    for rs in _sub_tiles(q_ref.shape[1]):
        o_ref[0, rs, :] = _attn_heads(q_ref[0, rs, :], k_ref, v_ref, 0).astype(BF16)


def _attn_seq(q, k, v, tq):
    batch, seq_len, _ = q.shape
    return pl.pallas_call(
        _attn_seq_kernel,
        out_shape=jax.ShapeDtypeStruct((batch, seq_len, X_WIDTH), BF16),
        grid=(batch, seq_len // tq),
        in_specs=[pl.BlockSpec((1, tq, X_WIDTH), lambda b, i: (b, i, 0)),
                  pl.BlockSpec((1, N_MEM, X_WIDTH), lambda b, i: (b, 0, 0)),
                  pl.BlockSpec((1, N_MEM, X_WIDTH), lambda b, i: (b, 0, 0))],
        out_specs=pl.BlockSpec((1, tq, X_WIDTH), lambda b, i: (b, i, 0)),
        compiler_params=pltpu.CompilerParams(dimension_semantics=("parallel", "parallel"),
                                             vmem_limit_bytes=VMEM_LIMIT),
        name="attn_seq",
    )(q, k, v)


def _attn_batch_kernel(q_ref, k_ref, v_ref, o_ref, *, bb):
    rows = X_HEADS * SUB
    col_head = _imod(_iota((rows, N_MEM * X_HEADS), 1), X_HEADS)
    row_head = _idiv(_iota((rows, N_MEM * X_HEADS), 0), SUB)
    own = col_head == row_head
    q4 = [jnp.concatenate([q_ref[bi, :, h * X_HD:(h + 1) * X_HD] for h in range(X_HEADS)], axis=0).astype(BF16)
          for bi in range(bb)]
    s = [jnp.where(own, _dot(q_, k_ref[bi].astype(BF16), _NT) * (X_HD ** -0.5), NEG)
         for bi, q_ in enumerate(q4)]
    e = [jnp.exp(x - jnp.max(x, axis=-1, keepdims=True)) for x in s]
    p = [x / jnp.sum(x, axis=-1, keepdims=True) for x in e]
    o = [_dot(x.astype(BF16), v_ref[bi].astype(BF16)) for bi, x in enumerate(p)]
    for bi in range(bb):
        for h in range(X_HEADS):
            o_ref[bi, :, h * X_HD:(h + 1) * X_HD] = o[bi][h * SUB:(h + 1) * SUB].astype(BF16)


def _attn_batch(q, k, v, layer, bb):
    b, l, _ = q.shape
    blk = lambda i: (i, 0, 0)
    kv = pl.BlockSpec((None, bb, N_MEM * X_HEADS, X_HD), lambda i: (layer, i, 0, 0))
    return pl.pallas_call(
        functools.partial(_attn_batch_kernel, bb=bb),
        out_shape=jax.ShapeDtypeStruct((b, l, X_WIDTH), BF16),
        grid=(b // bb,),
        in_specs=[pl.BlockSpec((bb, l, X_WIDTH), blk), kv, kv],
        out_specs=pl.BlockSpec((bb, l, X_WIDTH), blk),
        compiler_params=pltpu.CompilerParams(dimension_semantics=("parallel",),
                                             vmem_limit_bytes=VMEM_LIMIT),
        name="attn_batch",
    )(q, k, v)


def _s5_kernel(u_ref, h0r_ref, h0i_ref, lre_ref, lim_ref, ldt_ref, bre_ref, bim_ref, cre_ref, cim_ref,
               d_ref, wg_ref, bg_ref, y_ref, hr_out, hi_out,
               xr_s, xi_s, st_r, st_i, coef_s, *io_s, tb, nb, lane_chunk, seq_major_io):
    step = pl.program_id(0)
    if seq_major_io:
        u_s, y_s = io_s
        for b in range(nb):
            for j in range(S5_WIDTH // LANES):
                u_s[j, pl.ds(b, tb, stride=nb), :] = u_ref[b, :, j * LANES:(j + 1) * LANES]
        u = jnp.concatenate([u_s[j] for j in range(S5_WIDTH // LANES)], axis=1)
    else:
        u = u_ref[...]

    @pl.when(step == 0)
    def _():
        lr = lre_ref[...]
        li = lim_ref[...]
        dt = jnp.exp(ldt_ref[...])
        mag = jnp.exp(lr * dt)
        ar = mag * jnp.cos(li * dt)
        ai = mag * jnp.sin(li * dt)
        inv = 1.0 / (lr * lr + li * li)
        coef_s[0:1, :] = ar
        coef_s[1:2, :] = ai
        coef_s[2:3, :] = ((ar - 1.0) * lr + ai * li) * inv
        coef_s[3:4, :] = (ai * lr - (ar - 1.0) * li) * inv
        st_r[...] = h0r_ref[...]
        st_i[...] = h0i_ref[...]

    ub = u.astype(BF16)
    bur = _dot(ub, bre_ref[...])
    bui = _dot(ub, bim_ref[...])
    cr = coef_s[2:3, :]
    ci = coef_s[3:4, :]
    xr_s[...] = cr * bur - ci * bui
    xi_s[...] = cr * bui + ci * bur

    for lc in range(S5_STATE // lane_chunk):
        sl = slice(lc * lane_chunk, (lc + 1) * lane_chunk)
        ar = jnp.broadcast_to(coef_s[0:1, sl], (nb, lane_chunk))
        ai = jnp.broadcast_to(coef_s[1:2, sl], (nb, lane_chunk))

        def body(t, carry, sl=sl, ar=ar, ai=ai):
            hr, hi = carry
            r0 = pl.multiple_of(t * nb, 8)
            nhr = ar * hr - ai * hi + xr_s[pl.ds(r0, nb), sl]
            nhi = ar * hi + ai * hr + xi_s[pl.ds(r0, nb), sl]
            xr_s[pl.ds(r0, nb), sl] = nhr
            xi_s[pl.ds(r0, nb), sl] = nhi
            return nhr, nhi

        hr, hi = lax.fori_loop(0, tb, body, (st_r[:, sl], st_i[:, sl]), unroll=min(tb, 8))
        st_r[:, sl] = hr
        st_i[:, sl] = hi

    y = (_dot(xr_s[...].astype(BF16), cre_ref[...]) - _dot(xi_s[...].astype(BF16), cim_ref[...])
         + d_ref[...] * u)
    zg = 0.5 * y * (1.0 + jnp.tanh(math.sqrt(2.0 / math.pi) * (y + 0.044715 * (y * y * y))))
    gate = _sigmoid(_dot(zg.astype(BF16), wg_ref[...]) + bg_ref[...])
    if seq_major_io:
        yv = zg * gate
        for j in range(S5_WIDTH // LANES):
            y_s[j] = yv[:, j * LANES:(j + 1) * LANES]
        for b in range(nb):
            for j in range(S5_WIDTH // LANES):
                c0 = b * S5_WIDTH + j * LANES
                y_ref[:, c0:c0 + LANES] = y_s[j, pl.ds(b, tb, stride=nb), :].astype(BF16)
    else:
        y_ref[...] = (zg * gate).astype(BF16)

    @pl.when(step == pl.num_programs(0) - 1)
    def _():
        hr_out[...] = st_r[...]
        hi_out[...] = st_i[...]


def _s5(u, h0r, h0i, prm, tb, lane_chunk, seq_major_io):
    nb = h0r.shape[0]
    fix2 = lambda i: (0, 0)
    vec = pl.BlockSpec((1, S5_STATE), fix2)
    st = pl.BlockSpec((nb, S5_STATE), fix2)
    rows = tb * nb
    scratch = [pltpu.VMEM((rows, S5_STATE), F32), pltpu.VMEM((rows, S5_STATE), F32),
               pltpu.VMEM((nb, S5_STATE), F32), pltpu.VMEM((nb, S5_STATE), F32),
               pltpu.VMEM((8, S5_STATE), F32)]
    if seq_major_io:
        l = u.shape[1]
        u_spec = pl.BlockSpec((nb, tb, S5_WIDTH), lambda i: (0, i, 3072 // S5_WIDTH))
        y_shape = jax.ShapeDtypeStruct((l, nb * S5_WIDTH), BF16)
        y_spec = pl.BlockSpec((tb, nb * S5_WIDTH), lambda i: (i, 0))
        io_shape = (S5_WIDTH // LANES, rows, LANES)
        scratch += [pltpu.VMEM(io_shape, F32), pltpu.VMEM(io_shape, F32)]
    else:
        l = u.shape[0] // nb
        u_spec = pl.BlockSpec((rows, S5_WIDTH), lambda i: (i, 0))
        y_shape = jax.ShapeDtypeStruct((l * nb, S5_WIDTH), BF16)
        y_spec = pl.BlockSpec((rows, S5_WIDTH), lambda i: (i, 0))
    return pl.pallas_call(
        functools.partial(_s5_kernel, tb=tb, nb=nb, lane_chunk=lane_chunk, seq_major_io=seq_major_io),
        out_shape=(y_shape, jax.ShapeDtypeStruct((nb, S5_STATE), F32), jax.ShapeDtypeStruct((nb, S5_STATE), F32)),
        grid=(l // tb,),
        in_specs=[u_spec, st, st, vec, vec, vec,
                  pl.BlockSpec((S5_WIDTH, S5_STATE), fix2), pl.BlockSpec((S5_WIDTH, S5_STATE), fix2),
                  pl.BlockSpec((S5_STATE, S5_WIDTH), fix2), pl.BlockSpec((S5_STATE, S5_WIDTH), fix2),
                  pl.BlockSpec((1, S5_WIDTH), fix2), pl.BlockSpec((S5_WIDTH, S5_WIDTH), fix2),
                  pl.BlockSpec((1, S5_WIDTH), fix2)],
        out_specs=(y_spec, st, st),
        scratch_shapes=scratch,
        compiler_params=pltpu.CompilerParams(dimension_semantics=("arbitrary",),
                                             vmem_limit_bytes=VMEM_LIMIT),
        name="s5",
    )(u, h0r, h0i, prm["lre"], prm["lim"], prm["ldt"], prm["bre"], prm["bim"], prm["cre"],
      prm["cim"], prm["d"], prm["wg"], prm["bg"])


def _conv_silu(u, w8, fix):
    out = fix(pltpu.roll(u, 3, axis=0), 3) * w8[0:1, :]
    for s in (2, 1):
        out = out + fix(pltpu.roll(u, s, axis=0), s) * w8[3 - s:4 - s, :]
    out = out + u * w8[3:4, :]
    return out * _sigmoid(out)


def _l2norm_pairs(x_chunks, bones, scale):
    n = len(x_chunks)
    x = jnp.concatenate(x_chunks, axis=0) if n > 1 else x_chunks[0]
    out = [[None] * N_PAIRS for _ in range(n)]
    for p_ in range(N_PAIRS):
        xp = x[:, p_ * LANES:(p_ + 1) * LANES]
        xn = xp * (lax.rsqrt(_seg_mean(xp * xp, bones) * HEAD_DIM + EPS) * scale)
        for c in range(n):
            out[c][p_] = xn[c * CHUNK:(c + 1) * CHUNK]
    return out


def _head_norm_store(vals, gates, normw, bones, act, store):
    n = len(vals)
    for p_ in range(N_PAIRS):
        v = jnp.concatenate([vals[c][p_] for c in range(n)], axis=0) if n > 1 else vals[0][p_]
        g = jnp.concatenate([gates[c][p_] for c in range(n)], axis=0) if n > 1 else gates[0][p_]
        y = v * lax.rsqrt(_seg_mean(v * v, bones) + EPS) * normw * act(g)
        for c in range(n):
            store(c, p_, y[c * CHUNK:(c + 1) * CHUNK].astype(BF16))


def _dn_local(chunks, prm, sub, n_valid, prec):
    n = len(chunks)
    bones = prm["bones"]
    incl, strict = _pair_masks(sub)
    ltri, upair = _ltri(sub), _upair(sub)
    units = [(c, p_) for c in range(n) for p_ in range(N_PAIRS)]

    gates = []
    for ch in chunks:
        sg = _sigmoid(ch["zs"])
        gcomp = -jnp.exp(prm["alog_row"]) * _softplus(ch["zs"])
        g_rows = -jnp.exp(prm["rowalog"]) * _softplus(ch["rows"] + prm["rowadd"])
        if n_valid != sub:
            vc = _valid_col(sub, n_valid)
            sg = jnp.where(vc, sg, 0.0)
            gcomp = jnp.where(vc, gcomp, 0.0)
            g_rows = jnp.where(_valid_row(sub, n_valid), g_rows, 0.0)
        gates.append((sg, _mm_sel_l(ltri, gcomp), _mm_sel_r(g_rows, upair)))

    qn = _l2norm_pairs([ch["q"] for ch in chunks], bones, HEAD_DIM ** -0.5)
    kn = _l2norm_pairs([ch["k"] for ch in chunks], bones, 1.0)

    beta = [_expand(gates[c][0], 2 * p_, 2 * p_ + 1) for c, p_ in units]
    g_col = [_expand(gates[c][1], 6 + 2 * p_, 7 + 2 * p_) for c, p_ in units]
    dec = [jnp.exp(jnp.where(incl, gc - gates[c][2][p_:p_ + 1, :], NEG)) for gc, (c, p_) in zip(g_col, units)]
    kb = [kn[c][p_] * b for b, (c, p_) in zip(beta, units)]
    sc = [_mm(jnp.concatenate([kb_, qn[c][p_]], axis=0), _x2(kn[c][p_]), _NT)
          for kb_, (c, p_) in zip(kb, units)]
    a_in = [s[CHUNK:] * d for s, d in zip(sc, dec)]
    mbd = [_x2(jnp.where(strict, s[:CHUNK] * d, 0.0)) for s, d in zip(sc, dec)]
    tri_prec, newton, sol_prec = prec
    t_bd = _tri_inverse_units(mbd, sub, n_valid, tri_prec, newton)
    eg = [jnp.exp(gc) for gc in g_col]
    rhs = [jnp.concatenate([_x2(chunks[c]["v"][:, p_ * LANES:(p_ + 1) * LANES] * b), _x2(kb_ * e)], axis=1)
           for b, kb_, e, (c, p_) in zip(beta, kb, eg, units)]
    sol = [_mm(t, r, prec=sol_prec) for t, r in zip(t_bd, rhs)]
    out = [[None] * N_PAIRS for _ in range(n)]
    for i, (c, p_) in enumerate(units):
        g_last = _last_row_bcast(g_col[i], sub)
        out[c][p_] = dict(u=_fold(sol[i][:, :LANES]), w=_fold(sol[i][:, LANES:]), a_in=a_in[i],
                          qg=qn[c][p_] * eg[i], kd=kn[c][p_] * jnp.exp(g_last - g_col[i]),
                          eg_last=jnp.exp(g_last))
    return out


def _silu(x):
    return x * _sigmoid(x)


def _dn_seq_kernel(zq_ref, zk_ref, zv_ref, zg_ref, zs_ref, rows_ref, cp_ref, cw_ref, s0_ref,
                   add_ref, alog_ref, rowadd_ref, rowalog_ref, nw_ref,
                   o_ref, s_out_ref, s_scr, prev_scr, *, bb, prec):
    tstep = pl.program_id(1)

    @pl.when(tstep == 0)
    def _():
        s_scr[...] = s0_ref[...]
        prev_scr[...] = cp_ref[...]

    r8 = _iota((8, CONV_CH), 0)
    cw = cw_ref[...]
    chunks = []
    for b in range(bb):
        prev = prev_scr[b]

        def fix(rolled, s, prev=prev):
            first = jnp.where(r8 < s, pltpu.roll(prev, s, axis=0), rolled[0:8])
            return jnp.concatenate([first, rolled[8:]], axis=0)

        u_b = jnp.concatenate([zq_ref[b], zk_ref[b], zv_ref[b]], axis=1)
        qkv = _conv_silu(u_b, cw, fix)
        prev_scr[b] = u_b[CHUNK - 8:CHUNK]
        chunks.append(dict(q=qkv[:, 0:HEADS_W], k=qkv[:, HEADS_W:2 * HEADS_W], v=qkv[:, 2 * HEADS_W:],
                           zs=zs_ref[b] + add_ref[...], rows=rows_ref[b, 0]))

    prm = dict(bones=_block_ones(), alog_row=alog_ref[...], rowadd=rowadd_ref[...],
               rowalog=rowalog_ref[...])
    loc = _dn_local(chunks, prm, CHUNK, CHUNK, prec)
    bdm = _bd_mask()
    units = [(b, p_) for b in range(bb) for p_ in range(N_PAIRS)]
    s_prev = [s_scr[b, p_] for b, p_ in units]
    ws = [_mm(jnp.concatenate([loc[b][p_]["w"], loc[b][p_]["qg"]], axis=0), s)
          for s, (b, p_) in zip(s_prev, units)]
    v_new = [loc[b][p_]["u"] - w_[:CHUNK] for w_, (b, p_) in zip(ws, units)]
    o = [w_[CHUNK:] + _mm(loc[b][p_]["a_in"], _x2(v)) for w_, v, (b, p_) in zip(ws, v_new, units)]
    upd = [_mm(loc[b][p_]["kd"].T, v) for v, (b, p_) in zip(v_new, units)]
    for i, (b, p_) in enumerate(units):
        s_scr[b, p_] = loc[b][p_]["eg_last"][0:1, :] * s_prev[i] + jnp.where(bdm, upd[i], 0.0)

    def store(b, p_, val):
        o_ref[b, :, p_ * LANES:(p_ + 1) * LANES] = val

    _head_norm_store([[o[b * N_PAIRS + p_] for p_ in range(N_PAIRS)] for b in range(bb)],
                     [[zg_ref[b, :, p_ * LANES:(p_ + 1) * LANES] for p_ in range(N_PAIRS)] for b in range(bb)],
                     nw_ref[...], prm["bones"], _silu, store)

    @pl.when(tstep == pl.num_programs(1) - 1)
    def _():
        s_out_ref[...] = s_scr[...]


def _dn_batch_kernel(zq_ref, zk_ref, zv_ref, zg_ref, zs_ref, rows_ref, cp_ref, cw_ref, s0_ref,
                     add_ref, alog_ref, rowadd_ref, rowalog_ref, nw_ref,
                     o_ref, s_out_ref, *, nc, n_valid, prec):
    rb = nc * CHUNK
    pall = cp_ref[...]
    rr = _imod(_iota((rb, CONV_CH), 0), SUB)

    def fix(rolled, s):
        return jnp.where(rr < s, pltpu.roll(pall, (rb - SUB + s) % rb, axis=0), rolled)

    u_all = jnp.concatenate([zq_ref[...], zk_ref[...], zv_ref[...]], axis=1)
    qkv = _conv_silu(u_all, cw_ref[...], fix)
    zs_all = zs_ref[...] + add_ref[...]
    chunks = []
    for c in range(nc):
        rs = slice(c * CHUNK, (c + 1) * CHUNK)
        chunks.append(dict(q=qkv[rs, 0:HEADS_W], k=qkv[rs, HEADS_W:2 * HEADS_W], v=qkv[rs, 2 * HEADS_W:],
                           zs=zs_all[rs], rows=rows_ref[c]))

    prm = dict(bones=_block_ones(), alog_row=alog_ref[...], rowadd=rowadd_ref[...],
               rowalog=rowalog_ref[...])
    loc = _dn_local(chunks, prm, SUB, n_valid, prec)
    bdm = _bd_mask()
    nsub = CHUNK // SUB
    lane_sub = _idiv(_iota((LANES, CHUNK), 1), SUB)
    units = [(c, p_) for c in range(nc) for p_ in range(N_PAIRS)]
    subs = [(c, p_, sb) for c, p_ in units for sb in range(nsub)]
    s_prev = {k_: _bd_from_stacked(_read_pair(s0_ref, k_[0] * nsub + k_[2], k_[1]), bdm) for k_ in subs}
    rsl = lambda sb: slice(sb * SUB, (sb + 1) * SUB)
    ws = {k_: _mm(jnp.concatenate([loc[k_[0]][k_[1]]["w"][rsl(k_[2])], loc[k_[0]][k_[1]]["qg"][rsl(k_[2])]],
                                  axis=0), s_prev[k_]) for k_ in subs}
    v_new = {u_: jnp.concatenate([loc[u_[0]][u_[1]]["u"][rsl(sb)] - ws[u_ + (sb,)][:SUB] for sb in range(nsub)],
                                 axis=0) for u_ in units}
    o = {u_: jnp.concatenate([ws[u_ + (sb,)][SUB:] for sb in range(nsub)], axis=0)
         + _mm(loc[u_[0]][u_[1]]["a_in"], _x2(v_new[u_])) for u_ in units}
    kd_t = {u_: loc[u_[0]][u_[1]]["kd"].T for u_ in units}
    for k_ in subs:
        c, p_, sb = k_
        upd = _mm(jnp.where(lane_sub == sb, kd_t[(c, p_)], 0.0), v_new[(c, p_)])
        eg = loc[c][p_]["eg_last"][sb * SUB:sb * SUB + 1, :]
        _write_pair(s_out_ref, c * nsub + sb, p_,
                    _stacked_from_bd(eg * s_prev[k_] + jnp.where(bdm, upd, 0.0)))

    def store(c, p_, val):
        o_ref[c * CHUNK:(c + 1) * CHUNK, p_ * LANES:(p_ + 1) * LANES] = val

    _head_norm_store([[o[(c, p_)] for p_ in range(N_PAIRS)] for c in range(nc)],
                     [[zg_ref[c * CHUNK:(c + 1) * CHUNK, p_ * LANES:(p_ + 1) * LANES] for p_ in range(N_PAIRS)]
                      for c in range(nc)],
                     nw_ref[...], prm["bones"], _silu, store)


def _deltanet_seq(z3, rows4, convp, s0, prm, bb, prec):
    batch, seq_len, _ = z3.shape
    fix2 = lambda g, i: (0, 0)
    zspec = lambda width, col: pl.BlockSpec((bb, CHUNK, width), lambda g, i: (g, i, col))
    s_spec = pl.BlockSpec((bb, N_PAIRS, LANES, LANES), lambda g, i: (g, 0, 0, 0))
    in_specs = [zspec(HEADS_W, 0), zspec(HEADS_W, 1), zspec(HEADS_W, 2), zspec(HEADS_W, 3), zspec(LANES, 26),
                pl.BlockSpec((bb, 1, 16, LANES), lambda g, i: (g, i, 0, 0)),
                pl.BlockSpec((bb, 8, CONV_CH), lambda g, i: (g, 0, 0)),
                pl.BlockSpec((8, CONV_CH), fix2), s_spec,
                pl.BlockSpec((1, LANES), fix2), pl.BlockSpec((1, LANES), fix2),
                pl.BlockSpec((16, LANES), fix2), pl.BlockSpec((16, LANES), fix2),
                pl.BlockSpec((1, LANES), fix2)]
    return pl.pallas_call(
        functools.partial(_dn_seq_kernel, bb=bb, prec=prec),
        out_shape=(jax.ShapeDtypeStruct((batch, seq_len, HEADS_W), BF16), jax.ShapeDtypeStruct(s0.shape, F32)),
        grid=(batch // bb, seq_len // CHUNK),
        in_specs=in_specs,
        out_specs=(pl.BlockSpec((bb, CHUNK, HEADS_W), lambda g, i: (g, i, 0)), s_spec),
        scratch_shapes=[pltpu.VMEM((bb, N_PAIRS, LANES, LANES), F32), pltpu.VMEM((bb, 8, CONV_CH), F32)],
        compiler_params=pltpu.CompilerParams(dimension_semantics=("parallel", "arbitrary"),
                                             vmem_limit_bytes=VMEM_LIMIT),
        name="deltanet_seq",
    )(z3, z3, z3, z3, z3, rows4, convp, prm["conv_w8"], s0, prm["add_row"], prm["alog_row"],
      prm["rowadd"], prm["rowalog"], prm["dn_normw"])


def _state_specs(nsb, layer):
    shape = (nsb, N_HEADS, HEAD_DIM, HEAD_DIM)
    return (pl.BlockSpec((None,) + shape, lambda i: (layer, i, 0, 0, 0)),
            pl.BlockSpec(shape, lambda i: (i, 0, 0, 0)))


def _deltanet_batch(z, rows, convp, s0, layer, prm, nc, n_valid, prec):
    t = z.shape[0]
    rb = nc * CHUNK
    nsb = rb // SUB
    fix2 = lambda i: (0, 0)
    zspec = lambda width, col: pl.BlockSpec((rb, width), lambda i: (i, col))
    s_in, s_spec = _state_specs(nsb, layer)
    in_specs = [zspec(HEADS_W, 0), zspec(HEADS_W, 1), zspec(HEADS_W, 2), zspec(HEADS_W, 3), zspec(LANES, 26),
                pl.BlockSpec((nc, 16, LANES), lambda i: (i, 0, 0)),
                pl.BlockSpec((rb, CONV_CH), lambda i: (i, 0)),
                pl.BlockSpec((8, CONV_CH), fix2), s_in,
                pl.BlockSpec((1, LANES), fix2), pl.BlockSpec((1, LANES), fix2),
                pl.BlockSpec((16, LANES), fix2), pl.BlockSpec((16, LANES), fix2),
                pl.BlockSpec((1, LANES), fix2)]
    return pl.pallas_call(
        functools.partial(_dn_batch_kernel, nc=nc, n_valid=n_valid, prec=prec),
        out_shape=(jax.ShapeDtypeStruct((t, HEADS_W), BF16), jax.ShapeDtypeStruct(s0.shape[1:], F32)),
        grid=(t // rb,),
        in_specs=in_specs,
        out_specs=(pl.BlockSpec((rb, HEADS_W), lambda i: (i, 0)), s_spec),
        compiler_params=pltpu.CompilerParams(dimension_semantics=("parallel",),
                                             vmem_limit_bytes=VMEM_LIMIT),
        name="deltanet_batch",
    )(z, z, z, z, z, rows, convp, prm["conv_w8"], s0, prm["add_row"], prm["alog_row"],
      prm["rowadd"], prm["rowalog"], prm["dn_normw"])


def _ml_units(chunks, prm, sub, n_valid):
    n = len(chunks)
    incl, _ = _pair_masks(sub)
    lo = _lane_lo((CHUNK, LANES))
    ltri, upair = _ltri(sub), _upair(sub)
    ones2 = prm["bones"]
    bdm2 = jnp.concatenate([_bd_mask(), _bd_mask()], axis=1)
    nsub = CHUNK // sub
    units = [(c, p_) for c in range(n) for p_ in range(N_PAIRS)]
    pair = lambda x, p_: x[:, p_ * LANES:(p_ + 1) * LANES]

    comp = []
    for ch in chunks:
        rows = ch["rows"] + prm["rowadd"]
        ig_c, ls_c = ch["zs"], -_softplus(-ch["zs"])
        ig_r, ls_r = rows, -_softplus(-rows)
        if n_valid != sub:
            vc, vr = _valid_col(sub, n_valid), _valid_row(sub, n_valid)
            ig_c, ls_c = jnp.where(vc, ig_c, NEG), jnp.where(vc, ls_c, 0.0)
            ig_r, ls_r = jnp.where(vr, ig_r, NEG), jnp.where(vr, ls_r, 0.0)
        comp.append(dict(ig_c=ig_c, b_c=_mm_sel_l(ltri, ls_c), ig_r=ig_r, b_r=_mm_sel_r(ls_r, upair)))

    qp = [pair(chunks[c]["q"], p_) for c, p_ in units]
    kp = [pair(chunks[c]["k"], p_) * (HEAD_DIM ** -0.5) for c, p_ in units]
    vp = [pair(chunks[c]["v"], p_) for c, p_ in units]
    b_col = [_expand(comp[c]["b_c"], 18 + 2 * p_, 19 + 2 * p_) for c, p_ in units]
    ig_col = [_expand(comp[c]["ig_c"], 12 + 2 * p_, 13 + 2 * p_) for c, p_ in units]
    dl = [jnp.where(incl, bc - comp[c]["b_r"][6 + p_:7 + p_, :] + comp[c]["ig_r"][3 + p_:4 + p_, :], NEG)
          for bc, (c, p_) in zip(b_col, units)]
    mx = [jnp.where(lo, jnp.max(jnp.where(lo, d, NEG), axis=1, keepdims=True),
                    jnp.max(jnp.where(lo, NEG, d), axis=1, keepdims=True)) for d in dl]
    inter = [bc + chunks[c]["ms"][p_] for bc, (c, p_) in zip(b_col, units)]
    mt = [jnp.maximum(a, b) for a, b in zip(inter, mx)]
    s = [_mm(q_, _x2(k_), _NT) * jnp.exp(d - m) for q_, k_, d, m in zip(qp, kp, dl, mt)]
    sv = [_mm(s_, jnp.concatenate([_x2(v_), ones2], axis=1)) for s_, v_ in zip(s, vp)]
    cn_prev = [[chunks[c]["cn"](p_, sb) for sb in range(nsub)] for c, p_ in units]
    qcn = [jnp.concatenate([_mm(q_[sb * sub:(sb + 1) * sub], cn[sb]) for sb in range(nsub)], axis=0)
           if nsub > 1 else _mm(q_, cn[0]) for q_, cn in zip(qp, cn_prev)]
    w_inter = [jnp.exp(a - m) for a, m in zip(inter, mt)]
    h = [(wi * qc[:, :LANES] + sv_[:, :LANES])
         / jnp.maximum(jnp.abs(wi * qc[:, LANES:] + sv_[:, LANES:]), jnp.exp(-m))
         for wi, qc, sv_, m in zip(w_inter, qcn, sv, mt)]
    m_end = [_last_row_bcast(m, sub) for m in mt]
    b_last = [_last_row_bcast(bc, sub) for bc in b_col]
    a_end = [jnp.exp(bl + chunks[c]["ms"][p_] - me) for bl, me, (c, p_) in zip(b_last, m_end, units)]
    kw_t = [(k_ * jnp.exp(bl - bc + ig - me)).T for k_, bl, bc, ig, me in zip(kp, b_last, b_col, ig_col, m_end)]
    v1 = [jnp.concatenate([v_, jnp.ones((CHUNK, LANES), F32)], axis=1) for v_ in vp]
    lane_sub = _idiv(_iota((LANES, CHUNK), 1), sub) if nsub > 1 else None
    out = [[None] * N_PAIRS for _ in range(n)]
    for i, (c, p_) in enumerate(units):
        new_states, new_m = [], []
        for sb in range(nsub):
            lhs = kw_t[i] if nsub == 1 else jnp.where(lane_sub == sb, kw_t[i], 0.0)
            upd = _mm(lhs, v1[i])
            a_row = a_end[i][sb * sub:sb * sub + 1, :]
            a2 = jnp.concatenate([a_row, a_row], axis=1)
            new_states.append(a2 * cn_prev[i][sb] + jnp.where(bdm2, upd, 0.0))
            new_m.append(m_end[i][sb * sub:sb * sub + 1, :])
        out[c][p_] = (h[i], new_states, new_m)
    return out


def _ml_seq_kernel(zq_ref, zk_ref, zv_ref, zo_ref, zs_ref, rows_ref, cn0_ref, m0_ref, add_ref, rowadd_ref,
                   nw_ref, o_ref, cn_out_ref, m_out_ref, cn_scr, m_scr, *, bb):
    tstep = pl.program_id(1)

    @pl.when(tstep == 0)
    def _():
        cn_scr[...] = cn0_ref[...]
        m_scr[...] = m0_ref[...]

    prm = dict(bones=_block_ones(), rowadd=rowadd_ref[...])
    chunks = []
    for b in range(bb):
        chunks.append(dict(q=zq_ref[b], k=zk_ref[b], v=zv_ref[b], zs=zs_ref[b] + add_ref[...],
                           rows=rows_ref[b, 0],
                           ms=[jnp.broadcast_to(m_scr[b, p_:p_ + 1, :], (CHUNK, LANES)) for p_ in range(N_PAIRS)],
                           cn=lambda p_, sb, b=b: cn_scr[b, p_]))
    res = _ml_units(chunks, prm, CHUNK, CHUNK)
    for b in range(bb):
        for p_ in range(N_PAIRS):
            cn_scr[b, p_] = res[b][p_][1][0]
            m_scr[b, p_:p_ + 1, :] = res[b][p_][2][0]

    def store(b, p_, val):
        o_ref[b, :, p_ * LANES:(p_ + 1) * LANES] = val

    _head_norm_store([[res[b][p_][0] for p_ in range(N_PAIRS)] for b in range(bb)],
                     [[zo_ref[b, :, p_ * LANES:(p_ + 1) * LANES] for p_ in range(N_PAIRS)] for b in range(bb)],
                     nw_ref[...], prm["bones"], _sigmoid, store)

    @pl.when(tstep == pl.num_programs(1) - 1)
    def _():
        cn_out_ref[...] = cn_scr[...]
        m_out_ref[...] = m_scr[...]


def _ml_batch_kernel(zq_ref, zk_ref, zv_ref, zo_ref, zs_ref, rows_ref, c0_ref, n0_ref, m0_ref, add_ref,
                     rowadd_ref, nw_ref, o_ref, c_out_ref, n_out_ref, m_out_ref, *, nc, n_valid):
    prm = dict(bones=_block_ones(), rowadd=rowadd_ref[...])
    bdm = _bd_mask()
    zs_all = zs_ref[...] + add_ref[...]
    nsub = CHUNK // SUB

    def cn_of(c, p_, sb):
        i = c * nsub + sb
        return jnp.concatenate([_bd_from_stacked(_read_pair(c0_ref, i, p_), bdm),
                                _bd_from_stacked(_read_pair(n0_ref, i, p_), bdm)], axis=1)

    chunks = []
    for c in range(nc):
        rs = slice(c * CHUNK, (c + 1) * CHUNK)
        ms = [jnp.concatenate([jnp.broadcast_to(m0_ref[c * nsub + sb, p_:p_ + 1, :], (SUB, LANES))
                               for sb in range(nsub)], axis=0) for p_ in range(N_PAIRS)]
        chunks.append(dict(q=zq_ref[rs, :], k=zk_ref[rs, :], v=zv_ref[rs, :], zs=zs_all[rs], rows=rows_ref[c],
                           ms=ms, cn=functools.partial(cn_of, c)))
    res = _ml_units(chunks, prm, SUB, n_valid)
    for c in range(nc):
        for sb in range(nsub):
            i = c * nsub + sb
            for p_ in range(N_PAIRS):
                new_state = res[c][p_][1][sb]
                _write_pair(c_out_ref, i, p_, _stacked_from_bd(new_state[:, :LANES]))
                _write_pair(n_out_ref, i, p_, _stacked_from_bd(new_state[:, LANES:]))
            m_tile = jnp.zeros((8, LANES), F32)
            for p_ in range(N_PAIRS):
                m_tile = jnp.where(_iota((8, LANES), 0) == p_,
                                   jnp.broadcast_to(res[c][p_][2][sb], (8, LANES)), m_tile)
            m_out_ref[i] = m_tile

    def store(c, p_, val):
        o_ref[c * CHUNK:(c + 1) * CHUNK, p_ * LANES:(p_ + 1) * LANES] = val

    _head_norm_store([[res[c][p_][0] for p_ in range(N_PAIRS)] for c in range(nc)],
                     [[zo_ref[c * CHUNK:(c + 1) * CHUNK, p_ * LANES:(p_ + 1) * LANES] for p_ in range(N_PAIRS)]
                      for c in range(nc)],
                     nw_ref[...], prm["bones"], _sigmoid, store)


def _mlstm_seq(z3, rows4, cn0, m0, prm, bb):
    batch, seq_len, _ = z3.shape
    fix2 = lambda g, i: (0, 0)
    zspec = lambda width, col: pl.BlockSpec((bb, CHUNK, width), lambda g, i: (g, i, col))
    cn_spec = pl.BlockSpec((bb, N_PAIRS, LANES, 2 * LANES), lambda g, i: (g, 0, 0, 0))
    m_spec = pl.BlockSpec((bb, 8, LANES), lambda g, i: (g, 0, 0))
    in_specs = [zspec(HEADS_W, 4), zspec(HEADS_W, 5), zspec(HEADS_W, 6), zspec(HEADS_W, 7), zspec(LANES, 26),
                pl.BlockSpec((bb, 1, 16, LANES), lambda g, i: (g, i, 0, 0)), cn_spec, m_spec,
                pl.BlockSpec((1, LANES), fix2), pl.BlockSpec((16, LANES), fix2), pl.BlockSpec((1, LANES), fix2)]
    return pl.pallas_call(
        functools.partial(_ml_seq_kernel, bb=bb),
        out_shape=(jax.ShapeDtypeStruct((batch, seq_len, HEADS_W), BF16), jax.ShapeDtypeStruct(cn0.shape, F32),
                   jax.ShapeDtypeStruct(m0.shape, F32)),
        grid=(batch // bb, seq_len // CHUNK),
        in_specs=in_specs,
        out_specs=(pl.BlockSpec((bb, CHUNK, HEADS_W), lambda g, i: (g, i, 0)), cn_spec, m_spec),
        scratch_shapes=[pltpu.VMEM((bb, N_PAIRS, LANES, 2 * LANES), F32), pltpu.VMEM((bb, 8, LANES), F32)],
        compiler_params=pltpu.CompilerParams(dimension_semantics=("parallel", "arbitrary"),
                                             vmem_limit_bytes=VMEM_LIMIT),
        name="mlstm_seq",
    )(z3, z3, z3, z3, z3, rows4, cn0, m0, prm["add_row"], prm["rowadd"], prm["ml_normw"])


def _mlstm_batch(z, rows, c0, layer, n0, m0, prm, nc, n_valid):
    t = z.shape[0]
    rb = nc * CHUNK
    nsb = rb // SUB
    fix2 = lambda i: (0, 0)
    zspec = lambda width, col: pl.BlockSpec((rb, width), lambda i: (i, col))
    c_in, st_spec = _state_specs(nsb, layer)
    m_spec = pl.BlockSpec((nsb, 8, LANES), lambda i: (i, 0, 0))
    in_specs = [zspec(HEADS_W, 4), zspec(HEADS_W, 5), zspec(HEADS_W, 6), zspec(HEADS_W, 7), zspec(LANES, 26),
                pl.BlockSpec((nc, 16, LANES), lambda i: (i, 0, 0)), c_in, st_spec, m_spec,
                pl.BlockSpec((1, LANES), fix2), pl.BlockSpec((16, LANES), fix2), pl.BlockSpec((1, LANES), fix2)]
    return pl.pallas_call(
        functools.partial(_ml_batch_kernel, nc=nc, n_valid=n_valid),
        out_shape=(jax.ShapeDtypeStruct((t, HEADS_W), BF16), jax.ShapeDtypeStruct(n0.shape, F32),
                   jax.ShapeDtypeStruct(n0.shape, F32), jax.ShapeDtypeStruct(m0.shape, F32)),
        grid=(t // rb,),
        in_specs=in_specs,
        out_specs=(pl.BlockSpec((rb, HEADS_W), lambda i: (i, 0)), st_spec, st_spec, m_spec),
        compiler_params=pltpu.CompilerParams(dimension_semantics=("parallel",),
                                             vmem_limit_bytes=VMEM_LIMIT),
        name="mlstm_batch",
    )(z, z, z, z, z, rows, c0, n0, m0, prm["add_row"], prm["rowadd"], prm["ml_normw"])


def _lane_mixers_kernel(zq, zk, zv, zg, lq, lk, lv, lo, zs, cpq, cpk, cpv, cwq, cwk, cwv, bias, alog, dnw, mlw,
                        s0, c0, n0, m0, dn_o, ml_o, s_o, c_o, n_o, m_o, k_s, q_s, *, n_tok):
    h = pl.program_id(0)
    nb = s_o.shape[-1]
    zrow = lambda t, r: zs[t, pl.ds(r, 1), :]
    prow = lambda ref, r: ref[pl.ds(r, 1), :]
    bcast = lambda ref, dk: jnp.broadcast_to(ref[pl.ds(dk, 1), :], (HEAD_DIM, nb))
    zeros = jnp.zeros((HEAD_DIM, nb), F32)

    def conv(z_ref, cp_ref, cw_ref):
        e = [cp_ref[j] for j in range(CONV_W - 1)] + [z_ref[t] for t in range(n_tok)]
        outs = []
        for t in range(n_tok):
            acc = e[t] * cw_ref[0]
            for j in range(1, CONV_W):
                acc = acc + e[t + j] * cw_ref[j]
            outs.append(acc * _sigmoid(acc))
        return outs

    def l2n(x, scale):
        return x * (lax.rsqrt(jnp.sum(x * x, axis=0, keepdims=True) + EPS) * scale)

    def head_norm(x, w):
        return x * lax.rsqrt(jnp.mean(x * x, axis=0, keepdims=True) + EPS) * w

    qc, kc, vc = conv(zq, cpq, cwq), conv(zk, cpk, cwk), conv(zv, cpv, cwv)
    s_o[...] = s0[...]
    for t in range(n_tok):
        k_s[...] = l2n(kc[t], 1.0)
        q_s[...] = l2n(qc[t], HEAD_DIM ** -0.5)
        beta = _sigmoid(zrow(t, h))
        a = jnp.exp(-jnp.exp(prow(alog, N_HEADS + h))
                    * _softplus(zrow(t, N_HEADS + h) + prow(bias, N_HEADS + h)))
        ks = lax.fori_loop(0, HEAD_DIM, lambda dk, acc: acc + bcast(k_s, dk) * s_o[dk], zeros, unroll=8)
        delta = beta * (vc[t] - a * ks)

        def dn_update(dk, acc, a=a, delta=delta):
            s_new = a * s_o[dk] + bcast(k_s, dk) * delta
            s_o[dk] = s_new
            return acc + bcast(q_s, dk) * s_new

        o = lax.fori_loop(0, HEAD_DIM, dn_update, zeros, unroll=8)
        dn_o[t] = (head_norm(o, dnw[...]) * _silu(zg[t])).astype(BF16)

    c_o[...] = c0[...]
    n = n0[...]
    m = m0[...]
    for t in range(n_tok):
        q, k, v = lq[t], lk[t] * (HEAD_DIM ** -0.5), lv[t]
        ig = zrow(t, 2 * N_HEADS + h) + prow(bias, 2 * N_HEADS + h)
        lsf = -_softplus(-(zrow(t, 3 * N_HEADS + h) + prow(bias, 3 * N_HEADS + h)))
        m_new = jnp.maximum(lsf + m, ig)
        fp = jnp.exp(lsf + m - m_new)
        ip = jnp.exp(ig - m_new)
        k_s[...] = k * ip
        q_s[...] = q

        def ml_update(dk, acc, fp=fp, v=v):
            c_new = fp * c_o[dk] + bcast(k_s, dk) * v
            c_o[dk] = c_new
            return acc + bcast(q_s, dk) * c_new

        num = lax.fori_loop(0, HEAD_DIM, ml_update, zeros, unroll=8)
        n = fp * n + ip * k
        den = jnp.sum(q * n, axis=0, keepdims=True)
        hh = num / jnp.maximum(jnp.abs(den), jnp.exp(-m_new))
        m = m_new
        ml_o[t] = (head_norm(hh, mlw[...]) * _sigmoid(lo[t])).astype(BF16)
    n_o[...] = n
    m_o[...] = m


def _lane_mixers(z_t, conv_t, s0, c0, n0, m0, layer, prm):
    n_tok, _, nb = z_t.shape
    hd = HEAD_DIM
    zblk = lambda first: pl.BlockSpec((n_tok, hd, nb), lambda h: (0, first + h, 0))
    cblk = lambda taps, seg: pl.BlockSpec((taps, hd, nb), lambda h: (0, seg * N_HEADS + h, 0))
    wblk = lambda seg: pl.BlockSpec((CONV_W, hd, nb), lambda h: (0, seg * N_HEADS + h, 0))
    whole = lambda shape: pl.BlockSpec(shape, lambda h: (0,) * len(shape))
    mat_in = pl.BlockSpec((None, None, hd, hd, nb), lambda h: (layer, h, 0, 0, 0))
    mat_out = pl.BlockSpec((None, hd, hd, nb), lambda h: (h, 0, 0, 0))
    small_rows = 32
    in_specs = [zblk(0), zblk(6), zblk(12), zblk(18), zblk(24), zblk(30), zblk(36), zblk(42),
                pl.BlockSpec((n_tok, small_rows, nb), lambda h: (0, (Z_COLS - LANES) // small_rows, 0)),
                cblk(CONV_W - 1, 0), cblk(CONV_W - 1, 1), cblk(CONV_W - 1, 2), wblk(0), wblk(1), wblk(2),
                whole((small_rows, nb)), whole((small_rows, nb)), whole((hd, nb)), whole((hd, nb)),
                mat_in, mat_in,
                pl.BlockSpec((None, None, hd, nb), lambda h: (layer, h, 0, 0)),
                pl.BlockSpec((None, None, 1, nb), lambda h: (layer, h, 0, 0))]
    out_blk = pl.BlockSpec((n_tok, hd, nb), lambda h: (0, h, 0))
    return pl.pallas_call(
        functools.partial(_lane_mixers_kernel, n_tok=n_tok),
        out_shape=(jax.ShapeDtypeStruct((n_tok, HEADS_W, nb), BF16), jax.ShapeDtypeStruct((n_tok, HEADS_W, nb), BF16),
                   jax.ShapeDtypeStruct((N_HEADS, hd, hd, nb), F32), jax.ShapeDtypeStruct((N_HEADS, hd, hd, nb), F32),
                   jax.ShapeDtypeStruct((N_HEADS, hd, nb), F32), jax.ShapeDtypeStruct((N_HEADS, 1, nb), F32)),
        grid=(N_HEADS,),
        in_specs=in_specs,
        out_specs=(out_blk, out_blk, mat_out, mat_out,
                   pl.BlockSpec((None, hd, nb), lambda h: (h, 0, 0)), pl.BlockSpec((None, 1, nb), lambda h: (h, 0, 0))),
        scratch_shapes=[pltpu.VMEM((hd, nb), F32), pltpu.VMEM((hd, nb), F32)],
        compiler_params=pltpu.CompilerParams(dimension_semantics=("parallel",), vmem_limit_bytes=VMEM_LIMIT),
        name="lane_mixers",
    )(z_t, z_t, z_t, z_t, z_t, z_t, z_t, z_t, z_t, conv_t, conv_t, conv_t, prm["conv_wb"], prm["conv_wb"],
      prm["conv_wb"], prm["bias_b"], prm["alog_b"], prm["dn_normb"], prm["ml_normb"], s0, c0, n0, m0)


def _pack_w_in(w_in):
    offs = np.concatenate([[0], np.cumsum(IN_SIZES)])
    seg = lambda i: w_in[..., int(offs[i]):int(offs[i + 1])].astype(BF16)
    small = jnp.concatenate([seg(3), seg(4), seg(9), seg(10)], axis=-1)
    small = jnp.pad(small, ((0, 0),) * (w_in.ndim - 1) + ((0, LANES - 4 * N_HEADS),))
    cols = [seg(0), seg(1), seg(2), seg(5), seg(6), seg(7), seg(8), seg(11), seg(12), small]
    return jnp.concatenate(cols, axis=-1)


def _pair_rows(v6):
    return jnp.repeat(v6.reshape(N_PAIRS, 2), HEAD_DIM, axis=1).reshape(N_PAIRS, LANES)


def _layer_params(l, a):
    f = lambda name: a[name][l].astype(F32)
    zeros6 = jnp.zeros((N_HEADS,), F32)
    add_row = jnp.concatenate([zeros6, f("dn_dt_bias"), f("ml_i_bias"), f("ml_f_bias")])
    add_row = jnp.pad(add_row, (0, LANES - 4 * N_HEADS))[None, :]
    alog_row = jnp.pad(jnp.concatenate([zeros6, f("dn_a_log")]), (0, LANES - 2 * N_HEADS))[None, :]
    rowadd = jnp.concatenate([_pair_rows(f("dn_dt_bias")), _pair_rows(f("ml_i_bias")),
                              _pair_rows(f("ml_f_bias")), jnp.zeros((7, LANES), F32)], axis=0)
    rowalog = jnp.concatenate([_pair_rows(f("dn_a_log")), jnp.zeros((13, LANES), F32)], axis=0)
    eye_g = jnp.eye(S5_GROUPS, dtype=F32)
    bd_in = lambda b: jnp.einsum("gpc,gh->gchp", b, eye_g).reshape(S5_WIDTH, S5_STATE).astype(BF16)
    bd_out = lambda c: jnp.einsum("gcp,gh->gphc", c, eye_g).reshape(S5_STATE, S5_WIDTH).astype(BF16)
    return dict(
        layer=l, norm_mix=f("norm_mix")[None, :],
        add_row=add_row, alog_row=alog_row, rowadd=rowadd, rowalog=rowalog,
        conv_w8=jnp.pad(f("dn_conv_w"), ((0, 8 - CONV_W), (0, 0))),
        dn_normw=jnp.tile(f("dn_norm"), 2)[None, :], ml_normw=jnp.tile(f("ml_norm"), 2)[None, :],
        s5=dict(lre=f("s5_lam_re").reshape(1, S5_STATE), lim=f("s5_lam_im").reshape(1, S5_STATE),
                ldt=jnp.repeat(f("s5_log_dt"), S5_P)[None, :],
                bre=bd_in(f("s5_b_re")), bim=bd_in(f("s5_b_im")),
                cre=bd_out(f("s5_c_re")), cim=bd_out(f("s5_c_im")),
                d=f("s5_d").reshape(1, S5_WIDTH), wg=f("s5_w_glu").astype(BF16),
                bg=f("s5_b_glu")[None, :]),
        norm_x=f("norm_x")[None, :], norm_ff=f("norm_ff")[None, :],
    )


def _shared_weights(a):
    cast = lambda name: a[name].astype(BF16)
    return dict(w_in=_pack_w_in(a["w_in"]), w_out=cast("w_out"), w_xq=cast("w_xq"), w_xo=cast("w_xo"),
                w_ff1=cast("w_ff1"), w_ff2=cast("w_ff2"), w_xk=cast("w_xk"), w_xv=cast("w_xv"))


def _row_form(z, n_chunks):
    zs = z[..., Z_COLS - LANES:Z_COLS - LANES + 4 * N_HEADS].reshape(n_chunks, CHUNK, 4, N_PAIRS, 2)
    r = jnp.transpose(zs[:, :, 1:], (0, 2, 3, 4, 1)).reshape(n_chunks, 9, LANES)
    return jnp.pad(r, ((0, 0), (0, 7), (0, 0)))


def _un_bd_pairs(s):
    n = s.shape[0]
    a = s[:, :, :HEAD_DIM, :HEAD_DIM]
    b = s[:, :, HEAD_DIM:, HEAD_DIM:]
    return jnp.stack([a, b], axis=2).reshape(n, N_HEADS, HEAD_DIM, HEAD_DIM)


def _pack_m(m):
    s = m.shape[0]
    r = jnp.repeat(m.reshape(s, N_PAIRS, 2), HEAD_DIM, axis=2).reshape(s, N_PAIRS, LANES)
    return jnp.pad(r, ((0, 0), (0, 8 - N_PAIRS), (0, 0)))


def _unpack_m(mp):
    return mp[:, :N_PAIRS, ::HEAD_DIM].reshape(mp.shape[0], N_HEADS)


ROW_TILE = 1024
SEQ_BATCH = 8
S5_STEPS = 64
SAMPLE_CHUNKS = 2
ATTN_BATCH = 8
SEQ_SOLVE = (1, True, 1)
SAMPLE_SOLVE = (3, False, 1)


def _tile(n, pref):
    tile = min(n, pref)
    assert n % tile == 0, (n, pref)
    return tile


def _layer_prompt(x, mem_k, mem_v, prm, wts, final_g):
    batch, seq_len, _ = x.shape
    tm = _tile(seq_len, ROW_TILE)
    bb = _tile(batch, SEQ_BATCH)
    layer = prm["layer"]
    z3 = _norm_matmul(x, prm["norm_mix"], wts["w_in"], layer, tm)
    rows4 = _row_form(z3, batch * seq_len // CHUNK).reshape(batch, seq_len // CHUNK, 16, LANES)
    new_conv = z3[:, seq_len - 3:, :CONV_CH]

    dn, s_new = _deltanet_seq(z3, rows4, jnp.zeros((batch, 8, CONV_CH), F32),
                              jnp.zeros((batch, N_PAIRS, LANES, LANES), F32), prm, bb, SEQ_SOLVE)
    ml, cn_new, m_new = _mlstm_seq(z3, rows4, jnp.zeros((batch, N_PAIRS, LANES, 2 * LANES), F32),
                                   jnp.zeros((batch, 8, LANES), F32), prm, bb)
    zero_h = jnp.zeros((batch, S5_STATE), F32)
    s5o, hr, hi = _s5(z3, zero_h, zero_h, prm["s5"], _tile(seq_len, S5_STEPS), S5_STATE, True)

    x1, q = _out_proj(dn, ml, s5o, x, wts["w_out"], prm["norm_x"], wts["w_xq"], layer, tm)
    att = _attn_seq(q, mem_k, mem_v, tm)
    g_fin = final_g if final_g is not None else prm["norm_ff"]
    x3 = _ffn(x1, att, wts["w_xo"], prm["norm_ff"], wts["w_ff1"], wts["w_ff2"], g_fin, layer, tm,
              final_g is not None)
    st = (new_conv, _un_bd_pairs(s_new), _un_bd_pairs(cn_new[..., :LANES]),
          _un_bd_pairs(cn_new[..., LANES:])[..., 0], _unpack_m(m_new),
          hr.reshape(batch, S5_GROUPS, S5_P), hi.reshape(batch, S5_GROUPS, S5_P))
    return x3, st


def _layer_sample(x, cache_k, cache_v, conv_buf, dn_s, ml_c, ml_n, ml_m, ssm_re, ssm_im, prm, wts, batch,
                  seq_len, final_g):
    t = batch * seq_len
    layer = prm["layer"]
    z = _norm_matmul(x, prm["norm_mix"], wts["w_in"], layer, t)
    z3 = z.reshape(batch, seq_len, Z_COLS)
    new_conv = jnp.concatenate([conv_buf, z3[:, :, :CONV_CH]], axis=1)[:, seq_len:]
    lanes = lambda v: jnp.broadcast_to(v[..., None], v.shape + (batch,))
    mix_prm = dict(conv_wb=lanes(prm["conv_w8"][:CONV_W]), bias_b=lanes(prm["add_row"][0, :32]),
                   alog_b=lanes(prm["alog_row"][0, :32]), dn_normb=lanes(prm["dn_normw"][0, :HEAD_DIM]),
                   ml_normb=lanes(prm["ml_normw"][0, :HEAD_DIM]))
    dn_t, ml_t, s_new, c_new, n_new, m_new = _lane_mixers(
        jnp.transpose(z3, (1, 2, 0)), jnp.transpose(conv_buf, (1, 2, 0)), dn_s, ml_c, ml_n, ml_m, layer, mix_prm)
    to_rows = lambda a: jnp.transpose(a, (2, 0, 1)).reshape(1, t, HEADS_W)
    dn, ml = to_rows(dn_t), to_rows(ml_t)

    u_tm = jnp.transpose(z3[:, :, 3072:3072 + S5_WIDTH], (1, 0, 2))
    y_tm, hr, hi = _s5(u_tm.reshape(t, S5_WIDTH), ssm_re.reshape(batch, S5_STATE),
                       ssm_im.reshape(batch, S5_STATE), prm["s5"], seq_len, LANES, False)
    s5o = jnp.transpose(y_tm.reshape(seq_len, batch, S5_WIDTH), (1, 0, 2)).reshape(t, S5_WIDTH)

    x1, q = _out_proj(dn, ml, s5o, x, wts["w_out"], prm["norm_x"], wts["w_xq"], layer, t)
    q3 = jnp.pad(q.reshape(batch, seq_len, X_WIDTH).astype(F32), ((0, 0), (0, SUB - seq_len), (0, 0)))
    att = _attn_batch(q3, cache_k, cache_v, layer, _tile(batch, ATTN_BATCH))[:, :seq_len].reshape(1, t, X_WIDTH)
    g_fin = final_g if final_g is not None else prm["norm_ff"]
    x3 = _ffn(x1, att, wts["w_xo"], prm["norm_ff"], wts["w_ff1"], wts["w_ff2"], g_fin, layer, t,
              final_g is not None)
    st = (new_conv, jnp.transpose(s_new, (3, 0, 1, 2)), jnp.transpose(c_new, (3, 0, 1, 2)),
          jnp.transpose(n_new, (2, 0, 1)), jnp.transpose(m_new[:, 0, :], (1, 0)),
          hr.reshape(batch, S5_GROUPS, S5_P), hi.reshape(batch, S5_GROUPS, S5_P))
    return x3, st


def kernel(x_prompt, x_sample, mem_prompt, state_dn_conv, state_dn_s, state_ml_c, state_ml_n, state_ml_m, state_ssm_re, state_ssm_im, cache_mem_k, cache_mem_v, norm_mix, w_in, dn_conv_w, dn_a_log, dn_dt_bias, dn_norm, ml_i_bias, ml_f_bias, ml_norm, s5_lam_re, s5_lam_im, s5_log_dt, s5_b_re, s5_b_im, s5_c_re, s5_c_im, s5_d, s5_w_glu, s5_b_glu, w_out, norm_x, norm_mem, w_xq, w_xk, w_xv, w_xo, norm_ff, w_ff1, w_ff2, norm_final):
    a = dict(norm_mix=norm_mix, w_in=w_in, dn_conv_w=dn_conv_w, dn_a_log=dn_a_log, dn_dt_bias=dn_dt_bias,
             dn_norm=dn_norm, ml_i_bias=ml_i_bias, ml_f_bias=ml_f_bias, ml_norm=ml_norm,
             s5_lam_re=s5_lam_re, s5_lam_im=s5_lam_im, s5_log_dt=s5_log_dt, s5_b_re=s5_b_re,
             s5_b_im=s5_b_im, s5_c_re=s5_c_re, s5_c_im=s5_c_im, s5_d=s5_d, s5_w_glu=s5_w_glu,
             s5_b_glu=s5_b_glu, w_out=w_out, norm_x=norm_x, w_xq=w_xq, w_xo=w_xo, norm_ff=norm_ff,
             w_ff1=w_ff1, w_ff2=w_ff2)
    prms = [_layer_params(l, a) for l in range(DEPTH)]
    wts = _shared_weights(dict(a, w_xk=w_xk, w_xv=w_xv))
    g_final = norm_final.astype(F32)[None, :]

    bp, lp, _ = x_prompt.shape
    bs, ls, _ = x_sample.shape

    xp = x_prompt
    p_st = []
    for l in range(DEPTH):
        mk, mk_t = _mem_proj(mem_prompt, norm_mem[l][None, :], wts["w_xk"], l)
        mv, mv_t = _mem_proj(mem_prompt, norm_mem[l][None, :], wts["w_xv"], l)
        xp, st = _layer_prompt(xp, mk, mv, prms[l], wts, g_final if l == DEPTH - 1 else None)
        p_st.append(st + (mk_t.reshape(bp, N_MEM, X_HEADS, X_HD), mv_t.reshape(bp, N_MEM, X_HEADS, X_HD)))
    y_prompt = xp

    xs = x_sample.reshape(1, bs * ls, D_MODEL)
    cache_k = cache_mem_k.reshape(DEPTH, bs, N_MEM * X_HEADS, X_HD)
    cache_v = cache_mem_v.reshape(DEPTH, bs, N_MEM * X_HEADS, X_HD)
    dn_s_t = jnp.transpose(state_dn_s, (0, 2, 3, 4, 1))
    ml_c_t = jnp.transpose(state_ml_c, (0, 2, 3, 4, 1))
    ml_n_t = jnp.transpose(state_ml_n, (0, 2, 3, 1))
    ml_m_t = jnp.transpose(state_ml_m, (0, 2, 1))[:, :, None, :]
    s_st = []
    for l in range(DEPTH):
        xs, st = _layer_sample(xs, cache_k, cache_v, state_dn_conv[l], dn_s_t, ml_c_t,
                               ml_n_t, ml_m_t, state_ssm_re[l], state_ssm_im[l], prms[l], wts, bs,
                               ls, g_final if l == DEPTH - 1 else None)
        s_st.append(st)
    y_sample = xs.reshape(bs, ls, D_MODEL)

    stack = lambda sts, i: jnp.stack([s[i] for s in sts])
    return (y_prompt, y_sample) + tuple(stack(p_st, i) for i in range(9)) + tuple(stack(s_st, i) for i in range(7))
```

```python
import functools
import math

import numpy as np
import jax
import jax.numpy as jnp
from jax import lax
from jax.experimental import pallas as pl
from jax.experimental.pallas import tpu as pltpu

F32 = jnp.float32
BF16 = jnp.bfloat16

D_MODEL = 1024
DEPTH = 2
N_HEADS = 6
HEAD_DIM = 64
N_PAIRS = N_HEADS // 2
HEADS_W = N_HEADS * HEAD_DIM
CONV_W = 4
CONV_CH = 3 * HEADS_W
S5_WIDTH = 256
S5_GROUP = 16
S5_GROUPS = 16
S5_P = 64
S5_STATE = S5_GROUPS * S5_P
N_MEM = 256
X_HEADS = 4
X_HD = 128
X_WIDTH = X_HEADS * X_HD
D_FF = 4 * D_MODEL
EPS = 1e-6
IN_SIZES = (384, 384, 384, 6, 6, 384, 384, 384, 384, 6, 6, 384, 256)

LANES = 128
CHUNK = 64
SUB = 8
NEG = -1e30
Z_COLS = 3456
VMEM_LIMIT = 52 * 1024 * 1024

_NN = (((1,), (0,)), ((), ()))
_NT = (((1,), (1,)), ((), ()))


def _dot(a, b, dims=_NN):
    return lax.dot_general(a, b, dims, preferred_element_type=F32)


def _split(a, n):
    out = []
    r = a
    for i in range(n):
        h = r.astype(BF16)
        out.append(h)
        if i + 1 < n:
            r = r - h.astype(F32)
    return out


def _mm(a, b, dims=_NN, prec=1):
    if prec == 1:
        return _dot(a.astype(BF16), b.astype(BF16), dims)
    a1, a2 = _split(a, 2)
    b1, b2 = _split(b, 2)
    return (_dot(a1, b2, dims) + _dot(a2, b1, dims)) + _dot(a1, b1, dims)


def _mm_sel_r(a, sel, n=3):
    parts = _split(a, n)
    acc = _dot(parts[-1], sel)
    for p_ in parts[-2::-1]:
        acc = acc + _dot(p_, sel)
    return acc


def _mm_sel_l(sel, a, n=3):
    parts = _split(a, n)
    acc = _dot(sel, parts[-1])
    for p_ in parts[-2::-1]:
        acc = acc + _dot(sel, p_)
    return acc


def _sigmoid(x):
    return 1.0 / (1.0 + jnp.exp(-x))


def _softplus(x):
    return jnp.maximum(x, 0.0) + jnp.log(1.0 + jnp.exp(-jnp.abs(x)))


def _rmsnorm_rows(x, g):
    return x * lax.rsqrt(jnp.mean(x * x, axis=-1, keepdims=True) + EPS) * g


def _iota(shape, dim):
    return lax.broadcasted_iota(jnp.int32, shape, dim)


def _idiv(x, n):
    return x >> (n.bit_length() - 1)


def _imod(x, n):
    return x & (n - 1)


def _lane_lo(shape):
    return _iota(shape, 1) < HEAD_DIM


def _x2(x):
    lo = _lane_lo(x.shape)
    return jnp.concatenate([jnp.where(lo, x, 0.0), jnp.where(lo, 0.0, x)], axis=0)


def _fold(y):
    r = y.shape[0] // 2
    return y[:r] + y[r:]


def _expand(a, c0, c1):
    r = a.shape[0]
    return jnp.where(_lane_lo((r, LANES)), a[:, c0:c0 + 1], a[:, c1:c1 + 1])


def _block_ones():
    i = _iota((LANES, LANES), 0)
    j = _iota((LANES, LANES), 1)
    return jnp.where((i < HEAD_DIM) == (j < HEAD_DIM), 1.0, 0.0).astype(BF16)


def _seg_mean(x, bones):
    return _mm_sel_r(x, bones, n=1) * (1.0 / HEAD_DIM)


def _ltri(sub):
    i = _iota((CHUNK, CHUNK), 0)
    j = _iota((CHUNK, CHUNK), 1)
    ok = (j <= i) & (_idiv(i, sub) == _idiv(j, sub)) if sub != CHUNK else (j <= i)
    return jnp.where(ok, 1.0, 0.0).astype(BF16)


def _upair(sub):
    a = _iota((LANES, LANES), 0)
    b = _iota((LANES, LANES), 1)
    ok = ((a < HEAD_DIM) == (b < HEAD_DIM)) & (a <= b)
    if sub != CHUNK:
        ok = ok & (_idiv(_imod(a, HEAD_DIM), sub) == _idiv(_imod(b, HEAD_DIM), sub))
    return jnp.where(ok, 1.0, 0.0).astype(BF16)


def _pair_masks(sub):
    i = _iota((CHUNK, LANES), 0)
    j = _imod(_iota((CHUNK, LANES), 1), HEAD_DIM)
    same = (_idiv(i, sub) == _idiv(j, sub)) if sub != CHUNK else (i >= 0)
    return same & (j <= i), same & (j < i)


def _valid_col(sub, n_valid):
    i = _iota((CHUNK, 1), 0)
    return _imod(i, sub) < n_valid


def _valid_row(sub, n_valid):
    j = _iota((1, LANES), 1)
    return _imod(j, sub) < n_valid


def _bd_mask():
    i = _iota((LANES, LANES), 0)
    j = _iota((LANES, LANES), 1)
    return (i < HEAD_DIM) == (j < HEAD_DIM)


def _parts(a, prec):
    return tuple(_split(a, 1 if prec == 1 else 2))


def _mmp(ap, bp, dims=_NN):
    acc = None
    if len(bp) > 1:
        acc = _dot(ap[0], bp[1], dims)
    if len(ap) > 1:
        cross = _dot(ap[1], bp[0], dims)
        acc = cross if acc is None else acc + cross
    lead = _dot(ap[0], bp[0], dims)
    return lead if acc is None else acc + lead


def _blk_eq(n, b):
    return _idiv(_iota((n, n), 0), b) == _idiv(_iota((n, n), 1), b)


TRI_BASE = 8


def _tri_inverse_units(mbds, sub, n_valid, prec, newton):
    n = LANES
    eye = jnp.where(_iota((n, n), 0) == _iota((n, n), 1), 1.0, 0.0)
    if sub == CHUNK:
        base, n_sq = TRI_BASE, TRI_BASE.bit_length() - 2
        blk = _blk_eq(n, base)
        m0 = [jnp.where(blk, m, 0.0) for m in mbds]
    else:
        base, n_sq = sub, max(0, math.ceil(math.log2(n_valid)) - 1)
        m0 = mbds
    ts = [eye - m for m in m0]
    if n_sq > 0:
        ps = [_mm(m, m, prec=prec) for m in m0]
        for it in range(n_sq):
            if it == n_sq - 1:
                ts = [t + _mm(t, p_, prec=prec) for t, p_ in zip(ts, ps)]
            else:
                both = [_mm(jnp.concatenate([t, p_], axis=0), p_, prec=prec) for t, p_ in zip(ts, ps)]
                ts = [t + b_[:n] for t, b_ in zip(ts, both)]
                ps = [b_[n:] for b_ in both]
    b = base
    while b < sub:
        outer = _blk_eq(n, 2 * b) & jnp.logical_not(_blk_eq(n, b))
        tp = [_parts(t, prec) for t in ts]
        xs = [_mmp(_parts(jnp.where(outer, m, 0.0), prec), t_) for m, t_ in zip(mbds, tp)]
        ts = [t - _mmp(t_, _parts(x, prec)) for t, t_, x in zip(ts, tp, xs)]
        b *= 2
    if newton:
        res = [eye - (t + _mm(m, t, prec=3)) for m, t in zip(mbds, ts)]
        ts = [t + _mm(t, r) for t, r in zip(ts, res)]
    return ts


def _last_row_bcast(a, sub):
    if sub == CHUNK:
        return jnp.broadcast_to(a[CHUNK - 1:CHUNK, :], a.shape)
    parts = [jnp.broadcast_to(a[s * sub + sub - 1:s * sub + sub, :], (sub, a.shape[1]))
             for s in range(CHUNK // sub)]
    return jnp.concatenate(parts, axis=0)


def _bd_from_stacked(s, bdm):
    return jnp.where(bdm, jnp.concatenate([s, s], axis=1), 0.0)


def _stacked_from_bd(s):
    return s[:, :HEAD_DIM] + s[:, HEAD_DIM:]


def _read_pair(ref, i, p_):
    return jnp.concatenate([ref[i, 2 * p_], ref[i, 2 * p_ + 1]], axis=0)


def _write_pair(ref, i, p_, val):
    ref[i, 2 * p_] = val[:HEAD_DIM]
    ref[i, 2 * p_ + 1] = val[HEAD_DIM:]


SUB_ROWS = 1024


def _wspec(rows, cols, layer, row_block=0):
    return pl.BlockSpec((None, rows, cols), lambda i: (layer, row_block, 0), pipeline_mode=pl.Buffered(1))


def _sub_tiles(tm):
    r = min(tm, SUB_ROWS)
    return [slice(s * r, (s + 1) * r) for s in range(tm // r)]


def _rows3(nt):
    return lambda i: (i // nt, i % nt, 0)


def _dense_call(kernel_fn, name, out_shape, in_specs, out_specs, steps, args):
    return pl.pallas_call(
        kernel_fn, out_shape=out_shape, grid=(steps,), in_specs=in_specs, out_specs=out_specs,
        compiler_params=pltpu.CompilerParams(dimension_semantics=("parallel",), vmem_limit_bytes=VMEM_LIMIT),
        name=name)(*args)


def _in_proj_kernel(x_ref, g_ref, w_ref, o_ref):
    for rs in _sub_tiles(x_ref.shape[1]):
        xn = _rmsnorm_rows(x_ref[0, rs, :], g_ref[...]).astype(BF16)
        o_ref[0, rs, :] = _dot(xn, w_ref[...])


def _norm_matmul(x, g, w, layer, tm):
    b, l, k = x.shape
    n = w.shape[2]
    nt = l // tm
    fix = lambda i: (0, 0)
    return _dense_call(
        _in_proj_kernel, "norm_matmul", jax.ShapeDtypeStruct((b, l, n), F32),
        [pl.BlockSpec((1, tm, k), _rows3(nt)), pl.BlockSpec((1, k), fix), _wspec(k, n, layer)],
        pl.BlockSpec((1, tm, n), _rows3(nt)), b * nt, (x, g, w))


def _mem_proj_kernel(x_ref, g_ref, wk_ref, wv_ref, k_ref, kt_ref, v_ref, vt_ref):
    x = x_ref[0]
    xhat = x * lax.rsqrt(jnp.mean(x * x, axis=-1, keepdims=True) + EPS)
    for l in range(DEPTH):
        xn = (xhat * g_ref[l:l + 1, :]).astype(BF16)
        for w_ref, o_ref, t_ref in ((wk_ref, k_ref, kt_ref), (wv_ref, v_ref, vt_ref)):
            res = _dot(xn, w_ref[l])
            o_ref[l, 0] = res
            for h in range(X_HEADS):
                t_ref[l, 0, pl.ds(h, N_MEM, stride=X_HEADS), :] = res[:, h * X_HD:(h + 1) * X_HD]


def _mem_proj(mem, g, wk, wv):
    b = mem.shape[0]
    whole = lambda shape: pl.BlockSpec(shape, lambda i: (0,) * len(shape))
    o_spec = pl.BlockSpec((DEPTH, 1, N_MEM, X_WIDTH), lambda i: (0, i, 0, 0))
    t_spec = pl.BlockSpec((DEPTH, 1, N_MEM * X_HEADS, X_HD), lambda i: (0, i, 0, 0))
    o_shape = jax.ShapeDtypeStruct((DEPTH, b, N_MEM, X_WIDTH), F32)
    t_shape = jax.ShapeDtypeStruct((DEPTH, b, N_MEM * X_HEADS, X_HD), F32)
    return _dense_call(
        _mem_proj_kernel, "mem_proj", (o_shape, t_shape, o_shape, t_shape),
        [pl.BlockSpec((1, N_MEM, D_MODEL), lambda i: (i, 0, 0)), whole((DEPTH, D_MODEL)),
         whole((DEPTH, D_MODEL, X_WIDTH)), whole((DEPTH, D_MODEL, X_WIDTH))],
        (o_spec, t_spec, o_spec, t_spec), b, (mem, g, wk, wv))


def _out_proj_kernel(dn_ref, ml_ref, s5_ref, x_ref, w1_ref, w2_ref, w3_ref, g_ref, wq_ref,
                     x1_ref, q_ref):
    for rs in _sub_tiles(x_ref.shape[1]):
        acc = x_ref[0, rs, :] + _dot(dn_ref[0, rs, :], w1_ref[...])
        acc = acc + _dot(ml_ref[0, rs, :], w2_ref[...])
        acc = acc + _dot(s5_ref[rs, :], w3_ref[...])
        x1_ref[0, rs, :] = acc
        xn = _rmsnorm_rows(acc, g_ref[...]).astype(BF16)
        q_ref[0, rs, :] = _dot(xn, wq_ref[...]).astype(BF16)


def _out_proj(dn, ml, s5, x, w_out, g, wq, layer, tm):
    b, l, _ = x.shape
    nt = l // tm
    row = _rows3(nt)
    fix = lambda i: (0, 0)
    return _dense_call(
        _out_proj_kernel, "out_proj",
        (jax.ShapeDtypeStruct((b, l, D_MODEL), F32), jax.ShapeDtypeStruct((b, l, X_WIDTH), BF16)),
        [pl.BlockSpec((1, tm, HEADS_W), row), pl.BlockSpec((1, tm, HEADS_W), row),
         pl.BlockSpec((tm, S5_WIDTH), lambda i: (i % nt, i // nt)), pl.BlockSpec((1, tm, D_MODEL), row),
         _wspec(HEADS_W, D_MODEL, layer, 0), _wspec(HEADS_W, D_MODEL, layer, 1),
         _wspec(S5_WIDTH, D_MODEL, layer, 2 * HEADS_W // S5_WIDTH), pl.BlockSpec((1, D_MODEL), fix),
         _wspec(D_MODEL, X_WIDTH, layer)],
        (pl.BlockSpec((1, tm, D_MODEL), row), pl.BlockSpec((1, tm, X_WIDTH), row)), b * nt,
        (dn, ml, s5, x, w_out, w_out, w_out, g, wq))


def _ffn_kernel(x_ref, o_ref, wo_ref, g_ref, w1_ref, w2_ref, gf_ref, y_ref, *, final_norm, ff_chunk):
    for rs in _sub_tiles(x_ref.shape[1]):
        x2 = x_ref[0, rs, :] + _dot(o_ref[0, rs, :], wo_ref[...])
        xn = _rmsnorm_rows(x2, g_ref[...]).astype(BF16)
        acc = x2
        for c in range(D_FF // ff_chunk):
            hf = _dot(xn, w1_ref[:, c * ff_chunk:(c + 1) * ff_chunk])
            a = jnp.square(jnp.maximum(hf, 0.0)).astype(BF16)
            acc = acc + _dot(a, w2_ref[c * ff_chunk:(c + 1) * ff_chunk, :])
        if final_norm:
            acc = _rmsnorm_rows(acc, gf_ref[...])
        y_ref[0, rs, :] = acc


def _ffn(x, o, wo, g, w1, w2, gf, layer, tm, final_norm):
    b, l, _ = x.shape
    nt = l // tm
    row = _rows3(nt)
    fix = lambda i: (0, 0)
    return _dense_call(
        functools.partial(_ffn_kernel, final_norm=final_norm, ff_chunk=1024), "ffn",
        jax.ShapeDtypeStruct((b, l, D_MODEL), F32),
        [pl.BlockSpec((1, tm, D_MODEL), row), pl.BlockSpec((1, tm, X_WIDTH), row),
         _wspec(X_WIDTH, D_MODEL, layer), pl.BlockSpec((1, D_MODEL), fix),
         _wspec(D_MODEL, D_FF, layer), _wspec(D_FF, D_MODEL, layer),
         pl.BlockSpec((1, D_MODEL), fix)],
        pl.BlockSpec((1, tm, D_MODEL), row), b * nt, (x, o, wo, g, w1, w2, gf))


def _attn_heads(q, k_ref, v_ref, bi):
    outs = []
    for h in range(X_HEADS):
        sl = slice(h * X_HD, (h + 1) * X_HD)
        kh = k_ref[bi, :, sl].astype(BF16)
        vh = v_ref[bi, :, sl].astype(BF16)
        s = _dot(q[:, sl], kh, _NT) * (X_HD ** -0.5)
        e = jnp.exp(s - jnp.max(s, axis=-1, keepdims=True))
        p = e / jnp.sum(e, axis=-1, keepdims=True)
        outs.append(_dot(p.astype(BF16), vh))
    return jnp.concatenate(outs, axis=-1)


def _attn_seq_kernel(q_ref, k_ref, v_ref, o_ref):
    for rs in _sub_tiles(q_ref.shape[1]):
        o_ref[0, rs, :] = _attn_heads(q_ref[0, rs, :], k_ref, v_ref, 0).astype(BF16)


def _attn_seq(q, k, v, tq):
    batch, seq_len, _ = q.shape
    return pl.pallas_call(
        _attn_seq_kernel,
        out_shape=jax.ShapeDtypeStruct((batch, seq_len, X_WIDTH), BF16),
        grid=(batch, seq_len // tq),
        in_specs=[pl.BlockSpec((1, tq, X_WIDTH), lambda b, i: (b, i, 0)),
                  pl.BlockSpec((1, N_MEM, X_WIDTH), lambda b, i: (b, 0, 0)),
                  pl.BlockSpec((1, N_MEM, X_WIDTH), lambda b, i: (b, 0, 0))],
        out_specs=pl.BlockSpec((1, tq, X_WIDTH), lambda b, i: (b, i, 0)),
        compiler_params=pltpu.CompilerParams(dimension_semantics=("parallel", "parallel"),
                                             vmem_limit_bytes=VMEM_LIMIT),
        name="attn_seq",
    )(q, k, v)


def _attn_batch_kernel(q_ref, k_ref, v_ref, o_ref, *, bb):
    rows = X_HEADS * SUB
    col_head = _imod(_iota((rows, N_MEM * X_HEADS), 1), X_HEADS)
    row_head = _idiv(_iota((rows, N_MEM * X_HEADS), 0), SUB)
    own = col_head == row_head
    q4 = [jnp.concatenate([q_ref[bi, :, h * X_HD:(h + 1) * X_HD] for h in range(X_HEADS)], axis=0).astype(BF16)
          for bi in range(bb)]
    s = [jnp.where(own, _dot(q_, k_ref[bi].astype(BF16), _NT) * (X_HD ** -0.5), NEG)
         for bi, q_ in enumerate(q4)]
    e = [jnp.exp(x - jnp.max(x, axis=-1, keepdims=True)) for x in s]
    p = [x / jnp.sum(x, axis=-1, keepdims=True) for x in e]
    o = [_dot(x.astype(BF16), v_ref[bi].astype(BF16)) for bi, x in enumerate(p)]
    for bi in range(bb):
        for h in range(X_HEADS):
            o_ref[bi, :, h * X_HD:(h + 1) * X_HD] = o[bi][h * SUB:(h + 1) * SUB].astype(BF16)


def _attn_batch(q, k, v, layer, bb):
    b, l, _ = q.shape
    blk = lambda i: (i, 0, 0)
    kv = pl.BlockSpec((None, bb, N_MEM * X_HEADS, X_HD), lambda i: (layer, i, 0, 0))
    return pl.pallas_call(
        functools.partial(_attn_batch_kernel, bb=bb),
        out_shape=jax.ShapeDtypeStruct((b, l, X_WIDTH), BF16),
        grid=(b // bb,),
        in_specs=[pl.BlockSpec((bb, l, X_WIDTH), blk), kv, kv],
        out_specs=pl.BlockSpec((bb, l, X_WIDTH), blk),
        compiler_params=pltpu.CompilerParams(dimension_semantics=("parallel",),
                                             vmem_limit_bytes=VMEM_LIMIT),
        name="attn_batch",
    )(q, k, v)


def _post_mix_kernel(dn_ref, ml_ref, s5_ref, x_ref, k_ref, v_ref, w1_ref, w2_ref, w3_ref, gx_ref, wq_ref,
                     wo_ref, gf_ref, f1_ref, f2_ref, gl_ref, y_ref, *, final_norm, ff_chunk):
    x1 = x_ref[0] + _dot(dn_ref[0], w1_ref[...])
    x1 = x1 + _dot(ml_ref[0], w2_ref[...])
    x1 = x1 + _dot(s5_ref[...], w3_ref[...])
    q = _dot(_rmsnorm_rows(x1, gx_ref[...]).astype(BF16), wq_ref[...]).astype(BF16)
    att = _attn_heads(q, k_ref, v_ref, 0).astype(BF16)
    x2 = x1 + _dot(att, wo_ref[...])
    xn = _rmsnorm_rows(x2, gf_ref[...]).astype(BF16)
    acc = x2
    for c in range(D_FF // ff_chunk):
        hf = _dot(xn, f1_ref[:, c * ff_chunk:(c + 1) * ff_chunk])
        a = jnp.square(jnp.maximum(hf, 0.0)).astype(BF16)
        acc = acc + _dot(a, f2_ref[c * ff_chunk:(c + 1) * ff_chunk, :])
    if final_norm:
        acc = _rmsnorm_rows(acc, gl_ref[...])
    y_ref[0] = acc


def _post_mix(dn, ml, s5, x, mem_k, mem_v, wts, prm, g_last, layer, tm, final_norm):
    b, l, _ = x.shape
    nt = l // tm
    row = _rows3(nt)
    fix = lambda i: (0, 0)
    mem = pl.BlockSpec((None, 1, N_MEM, X_WIDTH), lambda i: (layer, i // nt, 0, 0))
    vec = pl.BlockSpec((1, D_MODEL), fix)
    return _dense_call(
        functools.partial(_post_mix_kernel, final_norm=final_norm, ff_chunk=1024), "post_mix",
        jax.ShapeDtypeStruct((b, l, D_MODEL), F32),
        [pl.BlockSpec((1, tm, HEADS_W), row), pl.BlockSpec((1, tm, HEADS_W), row),
         pl.BlockSpec((tm, S5_WIDTH), lambda i: (i % nt, i // nt)), pl.BlockSpec((1, tm, D_MODEL), row),
         mem, mem,
         _wspec(HEADS_W, D_MODEL, layer, 0), _wspec(HEADS_W, D_MODEL, layer, 1),
         _wspec(S5_WIDTH, D_MODEL, layer, 2 * HEADS_W // S5_WIDTH), vec, _wspec(D_MODEL, X_WIDTH, layer),
         _wspec(X_WIDTH, D_MODEL, layer), vec, _wspec(D_MODEL, D_FF, layer), _wspec(D_FF, D_MODEL, layer), vec],
        pl.BlockSpec((1, tm, D_MODEL), row), b * nt,
        (dn, ml, s5, x, mem_k, mem_v, wts["w_out"], wts["w_out"], wts["w_out"], prm["norm_x"], wts["w_xq"],
         wts["w_xo"], prm["norm_ff"], wts["w_ff1"], wts["w_ff2"], g_last))


def _s5_kernel(u_ref, h0r_ref, h0i_ref, lre_ref, lim_ref, ldt_ref, bre_ref, bim_ref, cre_ref, cim_ref,
               d_ref, wg_ref, bg_ref, y_ref, hr_out, hi_out,
               xr_s, xi_s, st_r, st_i, coef_s, *io_s, tb, nb, lane_chunk, seq_major_io):
    step = pl.program_id(0)
    if seq_major_io:
        u_s, y_s = io_s
        for b in range(nb):
            for j in range(S5_WIDTH // LANES):
                u_s[j, pl.ds(b, tb, stride=nb), :] = u_ref[b, :, j * LANES:(j + 1) * LANES]
        u = jnp.concatenate([u_s[j] for j in range(S5_WIDTH // LANES)], axis=1)
    else:
        u = u_ref[...]

    @pl.when(step == 0)
    def _():
        lr = lre_ref[...]
        li = lim_ref[...]
        dt = jnp.exp(ldt_ref[...])
        mag = jnp.exp(lr * dt)
        ar = mag * jnp.cos(li * dt)
        ai = mag * jnp.sin(li * dt)
        inv = 1.0 / (lr * lr + li * li)
        coef_s[0:1, :] = ar
        coef_s[1:2, :] = ai
        coef_s[2:3, :] = ((ar - 1.0) * lr + ai * li) * inv
        coef_s[3:4, :] = (ai * lr - (ar - 1.0) * li) * inv
        st_r[...] = h0r_ref[...]
        st_i[...] = h0i_ref[...]

    ub = u.astype(BF16)
    bur = _dot(ub, bre_ref[...])
    bui = _dot(ub, bim_ref[...])
    cr = coef_s[2:3, :]
    ci = coef_s[3:4, :]
    xr_s[...] = cr * bur - ci * bui
    xi_s[...] = cr * bui + ci * bur

    for lc in range(S5_STATE // lane_chunk):
        sl = slice(lc * lane_chunk, (lc + 1) * lane_chunk)
        ar = jnp.broadcast_to(coef_s[0:1, sl], (nb, lane_chunk))
        ai = jnp.broadcast_to(coef_s[1:2, sl], (nb, lane_chunk))

        def body(t, carry, sl=sl, ar=ar, ai=ai):
            hr, hi = carry
            r0 = pl.multiple_of(t * nb, 8)
            nhr = ar * hr - ai * hi + xr_s[pl.ds(r0, nb), sl]
            nhi = ar * hi + ai * hr + xi_s[pl.ds(r0, nb), sl]
            xr_s[pl.ds(r0, nb), sl] = nhr
            xi_s[pl.ds(r0, nb), sl] = nhi
            return nhr, nhi

        hr, hi = lax.fori_loop(0, tb, body, (st_r[:, sl], st_i[:, sl]), unroll=min(tb, 8))
        st_r[:, sl] = hr
        st_i[:, sl] = hi

    y = (_dot(xr_s[...].astype(BF16), cre_ref[...]) - _dot(xi_s[...].astype(BF16), cim_ref[...])
         + d_ref[...] * u)
    zg = 0.5 * y * (1.0 + jnp.tanh(math.sqrt(2.0 / math.pi) * (y + 0.044715 * (y * y * y))))
    gate = _sigmoid(_dot(zg.astype(BF16), wg_ref[...]) + bg_ref[...])
    if seq_major_io:
        yv = zg * gate
        for j in range(S5_WIDTH // LANES):
            y_s[j] = yv[:, j * LANES:(j + 1) * LANES]
        for b in range(nb):
            for j in range(S5_WIDTH // LANES):
                c0 = b * S5_WIDTH + j * LANES
                y_ref[:, c0:c0 + LANES] = y_s[j, pl.ds(b, tb, stride=nb), :].astype(BF16)
    else:
        y_ref[...] = (zg * gate).astype(BF16)

    @pl.when(step == pl.num_programs(0) - 1)
    def _():
        hr_out[...] = st_r[...]
        hi_out[...] = st_i[...]


def _s5(u, h0r, h0i, prm, tb, lane_chunk, seq_major_io):
    nb = h0r.shape[0]
    fix2 = lambda i: (0, 0)
    vec = pl.BlockSpec((1, S5_STATE), fix2)
    st = pl.BlockSpec((nb, S5_STATE), fix2)
    rows = tb * nb
    scratch = [pltpu.VMEM((rows, S5_STATE), F32), pltpu.VMEM((rows, S5_STATE), F32),
               pltpu.VMEM((nb, S5_STATE), F32), pltpu.VMEM((nb, S5_STATE), F32),
               pltpu.VMEM((8, S5_STATE), F32)]
    if seq_major_io:
        l = u.shape[1]
        u_spec = pl.BlockSpec((nb, tb, S5_WIDTH), lambda i: (0, i, 3072 // S5_WIDTH))
        y_shape = jax.ShapeDtypeStruct((l, nb * S5_WIDTH), BF16)
        y_spec = pl.BlockSpec((tb, nb * S5_WIDTH), lambda i: (i, 0))
        io_shape = (S5_WIDTH // LANES, rows, LANES)
        scratch += [pltpu.VMEM(io_shape, F32), pltpu.VMEM(io_shape, F32)]
    else:
        l = u.shape[0] // nb
        u_spec = pl.BlockSpec((rows, S5_WIDTH), lambda i: (i, 0))
        y_shape = jax.ShapeDtypeStruct((l * nb, S5_WIDTH), BF16)
        y_spec = pl.BlockSpec((rows, S5_WIDTH), lambda i: (i, 0))
    return pl.pallas_call(
        functools.partial(_s5_kernel, tb=tb, nb=nb, lane_chunk=lane_chunk, seq_major_io=seq_major_io),
        out_shape=(y_shape, jax.ShapeDtypeStruct((nb, S5_STATE), F32), jax.ShapeDtypeStruct((nb, S5_STATE), F32)),
        grid=(l // tb,),
        in_specs=[u_spec, st, st, vec, vec, vec,
                  pl.BlockSpec((S5_WIDTH, S5_STATE), fix2), pl.BlockSpec((S5_WIDTH, S5_STATE), fix2),
                  pl.BlockSpec((S5_STATE, S5_WIDTH), fix2), pl.BlockSpec((S5_STATE, S5_WIDTH), fix2),
                  pl.BlockSpec((1, S5_WIDTH), fix2), pl.BlockSpec((S5_WIDTH, S5_WIDTH), fix2),
                  pl.BlockSpec((1, S5_WIDTH), fix2)],
        out_specs=(y_spec, st, st),
        scratch_shapes=scratch,
        compiler_params=pltpu.CompilerParams(dimension_semantics=("arbitrary",),
                                             vmem_limit_bytes=VMEM_LIMIT),
        name="s5",
    )(u, h0r, h0i, prm["lre"], prm["lim"], prm["ldt"], prm["bre"], prm["bim"], prm["cre"],
      prm["cim"], prm["d"], prm["wg"], prm["bg"])


def _conv_silu(u, w8, fix):
    out = fix(pltpu.roll(u, 3, axis=0), 3) * w8[0:1, :]
    for s in (2, 1):
        out = out + fix(pltpu.roll(u, s, axis=0), s) * w8[3 - s:4 - s, :]
    out = out + u * w8[3:4, :]
    return out * _sigmoid(out)


def _l2norm_pairs(x_chunks, bones, scale):
    n = len(x_chunks)
    x = jnp.concatenate(x_chunks, axis=0) if n > 1 else x_chunks[0]
    out = [[None] * N_PAIRS for _ in range(n)]
    for p_ in range(N_PAIRS):
        xp = x[:, p_ * LANES:(p_ + 1) * LANES]
        xn = xp * (lax.rsqrt(_seg_mean(xp * xp, bones) * HEAD_DIM + EPS) * scale)
        for c in range(n):
            out[c][p_] = xn[c * CHUNK:(c + 1) * CHUNK]
    return out


def _head_norm_store(vals, gates, normw, bones, act, store):
    n = len(vals)
    for p_ in range(N_PAIRS):
        v = jnp.concatenate([vals[c][p_] for c in range(n)], axis=0) if n > 1 else vals[0][p_]
        g = jnp.concatenate([gates[c][p_] for c in range(n)], axis=0) if n > 1 else gates[0][p_]
        y = v * lax.rsqrt(_seg_mean(v * v, bones) + EPS) * normw * act(g)
        for c in range(n):
            store(c, p_, y[c * CHUNK:(c + 1) * CHUNK].astype(BF16))


def _dn_local(chunks, prm, sub, n_valid, prec):
    n = len(chunks)
    bones = prm["bones"]
    incl, strict = _pair_masks(sub)
    ltri, upair = _ltri(sub), _upair(sub)
    units = [(c, p_) for c in range(n) for p_ in range(N_PAIRS)]

    gates = []
    for ch in chunks:
        sg = _sigmoid(ch["zs"])
        gcomp = -jnp.exp(prm["alog_row"]) * _softplus(ch["zs"])
        g_rows = -jnp.exp(prm["rowalog"]) * _softplus(ch["rows"] + prm["rowadd"])
        if n_valid != sub:
            vc = _valid_col(sub, n_valid)
            sg = jnp.where(vc, sg, 0.0)
            gcomp = jnp.where(vc, gcomp, 0.0)
            g_rows = jnp.where(_valid_row(sub, n_valid), g_rows, 0.0)
        gates.append((sg, _mm_sel_l(ltri, gcomp), _mm_sel_r(g_rows, upair)))

    qn = _l2norm_pairs([ch["q"] for ch in chunks], bones, HEAD_DIM ** -0.5)
    kn = _l2norm_pairs([ch["k"] for ch in chunks], bones, 1.0)

    beta = [_expand(gates[c][0], 2 * p_, 2 * p_ + 1) for c, p_ in units]
    g_col = [_expand(gates[c][1], 6 + 2 * p_, 7 + 2 * p_) for c, p_ in units]
    dec = [jnp.exp(jnp.where(incl, gc - gates[c][2][p_:p_ + 1, :], NEG)) for gc, (c, p_) in zip(g_col, units)]
    kb = [kn[c][p_] * b for b, (c, p_) in zip(beta, units)]
    sc = [_mm(jnp.concatenate([kb_, qn[c][p_]], axis=0), _x2(kn[c][p_]), _NT)
          for kb_, (c, p_) in zip(kb, units)]
    a_in = [s[CHUNK:] * d for s, d in zip(sc, dec)]
    mbd = [_x2(jnp.where(strict, s[:CHUNK] * d, 0.0)) for s, d in zip(sc, dec)]
    tri_prec, newton, sol_prec = prec
    t_bd = _tri_inverse_units(mbd, sub, n_valid, tri_prec, newton)
    eg = [jnp.exp(gc) for gc in g_col]
    rhs = [jnp.concatenate([_x2(chunks[c]["v"][:, p_ * LANES:(p_ + 1) * LANES] * b), _x2(kb_ * e)], axis=1)
           for b, kb_, e, (c, p_) in zip(beta, kb, eg, units)]
    sol = [_mm(t, r, prec=sol_prec) for t, r in zip(t_bd, rhs)]
    out = [[None] * N_PAIRS for _ in range(n)]
    for i, (c, p_) in enumerate(units):
        g_last = _last_row_bcast(g_col[i], sub)
        out[c][p_] = dict(u=_fold(sol[i][:, :LANES]), w=_fold(sol[i][:, LANES:]), a_in=a_in[i],
                          qg=qn[c][p_] * eg[i], kd=kn[c][p_] * jnp.exp(g_last - g_col[i]),
                          eg_last=jnp.exp(g_last))
    return out


def _silu(x):
    return x * _sigmoid(x)


def _dn_seq_kernel(zq_ref, zk_ref, zv_ref, zg_ref, zs_ref, rows_ref, cp_ref, cw_ref, s0_ref,
                   add_ref, alog_ref, rowadd_ref, rowalog_ref, nw_ref,
                   o_ref, s_out_ref, s_scr, prev_scr, *, bb, prec):
    tstep = pl.program_id(1)

    @pl.when(tstep == 0)
    def _():
        s_scr[...] = s0_ref[...]
        prev_scr[...] = cp_ref[...]

    r8 = _iota((8, CONV_CH), 0)
    cw = cw_ref[...]
    chunks = []
    for b in range(bb):
        prev = prev_scr[b]

        def fix(rolled, s, prev=prev):
            first = jnp.where(r8 < s, pltpu.roll(prev, s, axis=0), rolled[0:8])
            return jnp.concatenate([first, rolled[8:]], axis=0)

        u_b = jnp.concatenate([zq_ref[b], zk_ref[b], zv_ref[b]], axis=1)
        qkv = _conv_silu(u_b, cw, fix)
        prev_scr[b] = u_b[CHUNK - 8:CHUNK]
        chunks.append(dict(q=qkv[:, 0:HEADS_W], k=qkv[:, HEADS_W:2 * HEADS_W], v=qkv[:, 2 * HEADS_W:],
                           zs=zs_ref[b] + add_ref[...], rows=rows_ref[b, 0]))

    prm = dict(bones=_block_ones(), alog_row=alog_ref[...], rowadd=rowadd_ref[...],
               rowalog=rowalog_ref[...])
    loc = _dn_local(chunks, prm, CHUNK, CHUNK, prec)
    bdm = _bd_mask()
    units = [(b, p_) for b in range(bb) for p_ in range(N_PAIRS)]
    s_prev = [s_scr[b, p_] for b, p_ in units]
    ws = [_mm(jnp.concatenate([loc[b][p_]["w"], loc[b][p_]["qg"]], axis=0), s)
          for s, (b, p_) in zip(s_prev, units)]
    v_new = [loc[b][p_]["u"] - w_[:CHUNK] for w_, (b, p_) in zip(ws, units)]
    o = [w_[CHUNK:] + _mm(loc[b][p_]["a_in"], _x2(v)) for w_, v, (b, p_) in zip(ws, v_new, units)]
    upd = [_mm(loc[b][p_]["kd"].T, v) for v, (b, p_) in zip(v_new, units)]
    for i, (b, p_) in enumerate(units):
        s_scr[b, p_] = loc[b][p_]["eg_last"][0:1, :] * s_prev[i] + jnp.where(bdm, upd[i], 0.0)

    def store(b, p_, val):
        o_ref[b, :, p_ * LANES:(p_ + 1) * LANES] = val

    _head_norm_store([[o[b * N_PAIRS + p_] for p_ in range(N_PAIRS)] for b in range(bb)],
                     [[zg_ref[b, :, p_ * LANES:(p_ + 1) * LANES] for p_ in range(N_PAIRS)] for b in range(bb)],
                     nw_ref[...], prm["bones"], _silu, store)

    @pl.when(tstep == pl.num_programs(1) - 1)
    def _():
        s_out_ref[...] = s_scr[...]


def _dn_batch_kernel(zq_ref, zk_ref, zv_ref, zg_ref, zs_ref, rows_ref, cp_ref, cw_ref, s0_ref,
                     add_ref, alog_ref, rowadd_ref, rowalog_ref, nw_ref,
                     o_ref, s_out_ref, *, nc, n_valid, prec):
    rb = nc * CHUNK
    pall = cp_ref[...]
    rr = _imod(_iota((rb, CONV_CH), 0), SUB)

    def fix(rolled, s):
        return jnp.where(rr < s, pltpu.roll(pall, (rb - SUB + s) % rb, axis=0), rolled)

    u_all = jnp.concatenate([zq_ref[...], zk_ref[...], zv_ref[...]], axis=1)
    qkv = _conv_silu(u_all, cw_ref[...], fix)
    zs_all = zs_ref[...] + add_ref[...]
    chunks = []
    for c in range(nc):
        rs = slice(c * CHUNK, (c + 1) * CHUNK)
        chunks.append(dict(q=qkv[rs, 0:HEADS_W], k=qkv[rs, HEADS_W:2 * HEADS_W], v=qkv[rs, 2 * HEADS_W:],
                           zs=zs_all[rs], rows=rows_ref[c]))

    prm = dict(bones=_block_ones(), alog_row=alog_ref[...], rowadd=rowadd_ref[...],
               rowalog=rowalog_ref[...])
    loc = _dn_local(chunks, prm, SUB, n_valid, prec)
    bdm = _bd_mask()
    nsub = CHUNK // SUB
    lane_sub = _idiv(_iota((LANES, CHUNK), 1), SUB)
    units = [(c, p_) for c in range(nc) for p_ in range(N_PAIRS)]
    subs = [(c, p_, sb) for c, p_ in units for sb in range(nsub)]
    s_prev = {k_: _bd_from_stacked(_read_pair(s0_ref, k_[0] * nsub + k_[2], k_[1]), bdm) for k_ in subs}
    rsl = lambda sb: slice(sb * SUB, (sb + 1) * SUB)
    ws = {k_: _mm(jnp.concatenate([loc[k_[0]][k_[1]]["w"][rsl(k_[2])], loc[k_[0]][k_[1]]["qg"][rsl(k_[2])]],
                                  axis=0), s_prev[k_]) for k_ in subs}
    v_new = {u_: jnp.concatenate([loc[u_[0]][u_[1]]["u"][rsl(sb)] - ws[u_ + (sb,)][:SUB] for sb in range(nsub)],
                                 axis=0) for u_ in units}
    o = {u_: jnp.concatenate([ws[u_ + (sb,)][SUB:] for sb in range(nsub)], axis=0)
         + _mm(loc[u_[0]][u_[1]]["a_in"], _x2(v_new[u_])) for u_ in units}
    kd_t = {u_: loc[u_[0]][u_[1]]["kd"].T for u_ in units}
    for k_ in subs:
        c, p_, sb = k_
        upd = _mm(jnp.where(lane_sub == sb, kd_t[(c, p_)], 0.0), v_new[(c, p_)])
        eg = loc[c][p_]["eg_last"][sb * SUB:sb * SUB + 1, :]
        _write_pair(s_out_ref, c * nsub + sb, p_,
                    _stacked_from_bd(eg * s_prev[k_] + jnp.where(bdm, upd, 0.0)))

    def store(c, p_, val):
        o_ref[c * CHUNK:(c + 1) * CHUNK, p_ * LANES:(p_ + 1) * LANES] = val

    _head_norm_store([[o[(c, p_)] for p_ in range(N_PAIRS)] for c in range(nc)],
                     [[zg_ref[c * CHUNK:(c + 1) * CHUNK, p_ * LANES:(p_ + 1) * LANES] for p_ in range(N_PAIRS)]
                      for c in range(nc)],
                     nw_ref[...], prm["bones"], _silu, store)


def _deltanet_seq(z3, rows4, convp, s0, prm, bb, prec):
    batch, seq_len, _ = z3.shape
    fix2 = lambda g, i: (0, 0)
    zspec = lambda width, col: pl.BlockSpec((bb, CHUNK, width), lambda g, i: (g, i, col))
    s_spec = pl.BlockSpec((bb, N_PAIRS, LANES, LANES), lambda g, i: (g, 0, 0, 0))
    in_specs = [zspec(HEADS_W, 0), zspec(HEADS_W, 1), zspec(HEADS_W, 2), zspec(HEADS_W, 3), zspec(LANES, 26),
                pl.BlockSpec((bb, 1, 16, LANES), lambda g, i: (g, i, 0, 0)),
                pl.BlockSpec((bb, 8, CONV_CH), lambda g, i: (g, 0, 0)),
                pl.BlockSpec((8, CONV_CH), fix2), s_spec,
                pl.BlockSpec((1, LANES), fix2), pl.BlockSpec((1, LANES), fix2),
                pl.BlockSpec((16, LANES), fix2), pl.BlockSpec((16, LANES), fix2),
                pl.BlockSpec((1, LANES), fix2)]
    return pl.pallas_call(
        functools.partial(_dn_seq_kernel, bb=bb, prec=prec),
        out_shape=(jax.ShapeDtypeStruct((batch, seq_len, HEADS_W), BF16), jax.ShapeDtypeStruct(s0.shape, F32)),
        grid=(batch // bb, seq_len // CHUNK),
        in_specs=in_specs,
        out_specs=(pl.BlockSpec((bb, CHUNK, HEADS_W), lambda g, i: (g, i, 0)), s_spec),
        scratch_shapes=[pltpu.VMEM((bb, N_PAIRS, LANES, LANES), F32), pltpu.VMEM((bb, 8, CONV_CH), F32)],
        compiler_params=pltpu.CompilerParams(dimension_semantics=("parallel", "arbitrary"),
                                             vmem_limit_bytes=VMEM_LIMIT),
        name="deltanet_seq",
    )(z3, z3, z3, z3, z3, rows4, convp, prm["conv_w8"], s0, prm["add_row"], prm["alog_row"],
      prm["rowadd"], prm["rowalog"], prm["dn_normw"])


def _state_specs(nsb, layer):
    shape = (nsb, N_HEADS, HEAD_DIM, HEAD_DIM)
    return (pl.BlockSpec((None,) + shape, lambda i: (layer, i, 0, 0, 0)),
            pl.BlockSpec(shape, lambda i: (i, 0, 0, 0)))


def _deltanet_batch(z, rows, convp, s0, layer, prm, nc, n_valid, prec):
    t = z.shape[0]
    rb = nc * CHUNK
    nsb = rb // SUB
    fix2 = lambda i: (0, 0)
    zspec = lambda width, col: pl.BlockSpec((rb, width), lambda i: (i, col))
    s_in, s_spec = _state_specs(nsb, layer)
    in_specs = [zspec(HEADS_W, 0), zspec(HEADS_W, 1), zspec(HEADS_W, 2), zspec(HEADS_W, 3), zspec(LANES, 26),
                pl.BlockSpec((nc, 16, LANES), lambda i: (i, 0, 0)),
                pl.BlockSpec((rb, CONV_CH), lambda i: (i, 0)),
                pl.BlockSpec((8, CONV_CH), fix2), s_in,
                pl.BlockSpec((1, LANES), fix2), pl.BlockSpec((1, LANES), fix2),
                pl.BlockSpec((16, LANES), fix2), pl.BlockSpec((16, LANES), fix2),
                pl.BlockSpec((1, LANES), fix2)]
    return pl.pallas_call(
        functools.partial(_dn_batch_kernel, nc=nc, n_valid=n_valid, prec=prec),
        out_shape=(jax.ShapeDtypeStruct((t, HEADS_W), BF16), jax.ShapeDtypeStruct(s0.shape[1:], F32)),
        grid=(t // rb,),
        in_specs=in_specs,
        out_specs=(pl.BlockSpec((rb, HEADS_W), lambda i: (i, 0)), s_spec),
        compiler_params=pltpu.CompilerParams(dimension_semantics=("parallel",),
                                             vmem_limit_bytes=VMEM_LIMIT),
        name="deltanet_batch",
    )(z, z, z, z, z, rows, convp, prm["conv_w8"], s0, prm["add_row"], prm["alog_row"],
      prm["rowadd"], prm["rowalog"], prm["dn_normw"])


def _ml_units(chunks, prm, sub, n_valid):
    n = len(chunks)
    incl, _ = _pair_masks(sub)
    lo = _lane_lo((CHUNK, LANES))
    ltri, upair = _ltri(sub), _upair(sub)
    ones2 = prm["bones"]
    bdm2 = jnp.concatenate([_bd_mask(), _bd_mask()], axis=1)
    nsub = CHUNK // sub
    units = [(c, p_) for c in range(n) for p_ in range(N_PAIRS)]
    pair = lambda x, p_: x[:, p_ * LANES:(p_ + 1) * LANES]

    comp = []
    for ch in chunks:
        rows = ch["rows"] + prm["rowadd"]
        ig_c, ls_c = ch["zs"], -_softplus(-ch["zs"])
        ig_r, ls_r = rows, -_softplus(-rows)
        if n_valid != sub:
            vc, vr = _valid_col(sub, n_valid), _valid_row(sub, n_valid)
            ig_c, ls_c = jnp.where(vc, ig_c, NEG), jnp.where(vc, ls_c, 0.0)
            ig_r, ls_r = jnp.where(vr, ig_r, NEG), jnp.where(vr, ls_r, 0.0)
        comp.append(dict(ig_c=ig_c, b_c=_mm_sel_l(ltri, ls_c), ig_r=ig_r, b_r=_mm_sel_r(ls_r, upair)))

    qp = [pair(chunks[c]["q"], p_) for c, p_ in units]
    kp = [pair(chunks[c]["k"], p_) * (HEAD_DIM ** -0.5) for c, p_ in units]
    vp = [pair(chunks[c]["v"], p_) for c, p_ in units]
    b_col = [_expand(comp[c]["b_c"], 18 + 2 * p_, 19 + 2 * p_) for c, p_ in units]
    ig_col = [_expand(comp[c]["ig_c"], 12 + 2 * p_, 13 + 2 * p_) for c, p_ in units]
    dl = [jnp.where(incl, bc - comp[c]["b_r"][6 + p_:7 + p_, :] + comp[c]["ig_r"][3 + p_:4 + p_, :], NEG)
          for bc, (c, p_) in zip(b_col, units)]
    mx = [jnp.where(lo, jnp.max(jnp.where(lo, d, NEG), axis=1, keepdims=True),
                    jnp.max(jnp.where(lo, NEG, d), axis=1, keepdims=True)) for d in dl]
    inter = [bc + chunks[c]["ms"][p_] for bc, (c, p_) in zip(b_col, units)]
    mt = [jnp.maximum(a, b) for a, b in zip(inter, mx)]
    s = [_mm(q_, _x2(k_), _NT) * jnp.exp(d - m) for q_, k_, d, m in zip(qp, kp, dl, mt)]
    sv = [_mm(s_, jnp.concatenate([_x2(v_), ones2], axis=1)) for s_, v_ in zip(s, vp)]
    cn_prev = [[chunks[c]["cn"](p_, sb) for sb in range(nsub)] for c, p_ in units]
    qcn = [jnp.concatenate([_mm(q_[sb * sub:(sb + 1) * sub], cn[sb]) for sb in range(nsub)], axis=0)
           if nsub > 1 else _mm(q_, cn[0]) for q_, cn in zip(qp, cn_prev)]
    w_inter = [jnp.exp(a - m) for a, m in zip(inter, mt)]
    h = [(wi * qc[:, :LANES] + sv_[:, :LANES])
         / jnp.maximum(jnp.abs(wi * qc[:, LANES:] + sv_[:, LANES:]), jnp.exp(-m))
         for wi, qc, sv_, m in zip(w_inter, qcn, sv, mt)]
    m_end = [_last_row_bcast(m, sub) for m in mt]
    b_last = [_last_row_bcast(bc, sub) for bc in b_col]
    a_end = [jnp.exp(bl + chunks[c]["ms"][p_] - me) for bl, me, (c, p_) in zip(b_last, m_end, units)]
    kw_t = [(k_ * jnp.exp(bl - bc + ig - me)).T for k_, bl, bc, ig, me in zip(kp, b_last, b_col, ig_col, m_end)]
    v1 = [jnp.concatenate([v_, jnp.ones((CHUNK, LANES), F32)], axis=1) for v_ in vp]
    lane_sub = _idiv(_iota((LANES, CHUNK), 1), sub) if nsub > 1 else None
    out = [[None] * N_PAIRS for _ in range(n)]
    for i, (c, p_) in enumerate(units):
        new_states, new_m = [], []
        for sb in range(nsub):
            lhs = kw_t[i] if nsub == 1 else jnp.where(lane_sub == sb, kw_t[i], 0.0)
            upd = _mm(lhs, v1[i])
            a_row = a_end[i][sb * sub:sb * sub + 1, :]
            a2 = jnp.concatenate([a_row, a_row], axis=1)
            new_states.append(a2 * cn_prev[i][sb] + jnp.where(bdm2, upd, 0.0))
            new_m.append(m_end[i][sb * sub:sb * sub + 1, :])
        out[c][p_] = (h[i], new_states, new_m)
    return out


def _ml_seq_kernel(zq_ref, zk_ref, zv_ref, zo_ref, zs_ref, rows_ref, cn0_ref, m0_ref, add_ref, rowadd_ref,
                   nw_ref, o_ref, cn_out_ref, m_out_ref, cn_scr, m_scr, *, bb):
    tstep = pl.program_id(1)

    @pl.when(tstep == 0)
    def _():
        cn_scr[...] = cn0_ref[...]
        m_scr[...] = m0_ref[...]

    prm = dict(bones=_block_ones(), rowadd=rowadd_ref[...])
    chunks = []
    for b in range(bb):
        chunks.append(dict(q=zq_ref[b], k=zk_ref[b], v=zv_ref[b], zs=zs_ref[b] + add_ref[...],
                           rows=rows_ref[b, 0],
                           ms=[jnp.broadcast_to(m_scr[b, p_:p_ + 1, :], (CHUNK, LANES)) for p_ in range(N_PAIRS)],
                           cn=lambda p_, sb, b=b: cn_scr[b, p_]))
    res = _ml_units(chunks, prm, CHUNK, CHUNK)
    for b in range(bb):
        for p_ in range(N_PAIRS):
            cn_scr[b, p_] = res[b][p_][1][0]
            m_scr[b, p_:p_ + 1, :] = res[b][p_][2][0]

    def store(b, p_, val):
        o_ref[b, :, p_ * LANES:(p_ + 1) * LANES] = val

    _head_norm_store([[res[b][p_][0] for p_ in range(N_PAIRS)] for b in range(bb)],
                     [[zo_ref[b, :, p_ * LANES:(p_ + 1) * LANES] for p_ in range(N_PAIRS)] for b in range(bb)],
                     nw_ref[...], prm["bones"], _sigmoid, store)

    @pl.when(tstep == pl.num_programs(1) - 1)
    def _():
        cn_out_ref[...] = cn_scr[...]
        m_out_ref[...] = m_scr[...]


def _ml_batch_kernel(zq_ref, zk_ref, zv_ref, zo_ref, zs_ref, rows_ref, c0_ref, n0_ref, m0_ref, add_ref,
                     rowadd_ref, nw_ref, o_ref, c_out_ref, n_out_ref, m_out_ref, *, nc, n_valid):
    prm = dict(bones=_block_ones(), rowadd=rowadd_ref[...])
    bdm = _bd_mask()
    zs_all = zs_ref[...] + add_ref[...]
    nsub = CHUNK // SUB

    def cn_of(c, p_, sb):
        i = c * nsub + sb
        return jnp.concatenate([_bd_from_stacked(_read_pair(c0_ref, i, p_), bdm),
                                _bd_from_stacked(_read_pair(n0_ref, i, p_), bdm)], axis=1)

    chunks = []
    for c in range(nc):
        rs = slice(c * CHUNK, (c + 1) * CHUNK)
        ms = [jnp.concatenate([jnp.broadcast_to(m0_ref[c * nsub + sb, p_:p_ + 1, :], (SUB, LANES))
                               for sb in range(nsub)], axis=0) for p_ in range(N_PAIRS)]
        chunks.append(dict(q=zq_ref[rs, :], k=zk_ref[rs, :], v=zv_ref[rs, :], zs=zs_all[rs], rows=rows_ref[c],
                           ms=ms, cn=functools.partial(cn_of, c)))
    res = _ml_units(chunks, prm, SUB, n_valid)
    for c in range(nc):
        for sb in range(nsub):
            i = c * nsub + sb
            for p_ in range(N_PAIRS):
                new_state = res[c][p_][1][sb]
                _write_pair(c_out_ref, i, p_, _stacked_from_bd(new_state[:, :LANES]))
                _write_pair(n_out_ref, i, p_, _stacked_from_bd(new_state[:, LANES:]))
            m_tile = jnp.zeros((8, LANES), F32)
            for p_ in range(N_PAIRS):
                m_tile = jnp.where(_iota((8, LANES), 0) == p_,
                                   jnp.broadcast_to(res[c][p_][2][sb], (8, LANES)), m_tile)
            m_out_ref[i] = m_tile

    def store(c, p_, val):
        o_ref[c * CHUNK:(c + 1) * CHUNK, p_ * LANES:(p_ + 1) * LANES] = val

    _head_norm_store([[res[c][p_][0] for p_ in range(N_PAIRS)] for c in range(nc)],
                     [[zo_ref[c * CHUNK:(c + 1) * CHUNK, p_ * LANES:(p_ + 1) * LANES] for p_ in range(N_PAIRS)]
                      for c in range(nc)],
                     nw_ref[...], prm["bones"], _sigmoid, store)


def _mlstm_seq(z3, rows4, cn0, m0, prm, bb):
    batch, seq_len, _ = z3.shape
    fix2 = lambda g, i: (0, 0)
    zspec = lambda width, col: pl.BlockSpec((bb, CHUNK, width), lambda g, i: (g, i, col))
    cn_spec = pl.BlockSpec((bb, N_PAIRS, LANES, 2 * LANES), lambda g, i: (g, 0, 0, 0))
    m_spec = pl.BlockSpec((bb, 8, LANES), lambda g, i: (g, 0, 0))
    in_specs = [zspec(HEADS_W, 4), zspec(HEADS_W, 5), zspec(HEADS_W, 6), zspec(HEADS_W, 7), zspec(LANES, 26),
                pl.BlockSpec((bb, 1, 16, LANES), lambda g, i: (g, i, 0, 0)), cn_spec, m_spec,
                pl.BlockSpec((1, LANES), fix2), pl.BlockSpec((16, LANES), fix2), pl.BlockSpec((1, LANES), fix2)]
    return pl.pallas_call(
        functools.partial(_ml_seq_kernel, bb=bb),
        out_shape=(jax.ShapeDtypeStruct((batch, seq_len, HEADS_W), BF16), jax.ShapeDtypeStruct(cn0.shape, F32),
                   jax.ShapeDtypeStruct(m0.shape, F32)),
        grid=(batch // bb, seq_len // CHUNK),
        in_specs=in_specs,
        out_specs=(pl.BlockSpec((bb, CHUNK, HEADS_W), lambda g, i: (g, i, 0)), cn_spec, m_spec),
        scratch_shapes=[pltpu.VMEM((bb, N_PAIRS, LANES, 2 * LANES), F32), pltpu.VMEM((bb, 8, LANES), F32)],
        compiler_params=pltpu.CompilerParams(dimension_semantics=("parallel", "arbitrary"),
                                             vmem_limit_bytes=VMEM_LIMIT),
        name="mlstm_seq",
    )(z3, z3, z3, z3, z3, rows4, cn0, m0, prm["add_row"], prm["rowadd"], prm["ml_normw"])


def _mlstm_batch(z, rows, c0, layer, n0, m0, prm, nc, n_valid):
    t = z.shape[0]
    rb = nc * CHUNK
    nsb = rb // SUB
    fix2 = lambda i: (0, 0)
    zspec = lambda width, col: pl.BlockSpec((rb, width), lambda i: (i, col))
    c_in, st_spec = _state_specs(nsb, layer)
    m_spec = pl.BlockSpec((nsb, 8, LANES), lambda i: (i, 0, 0))
    in_specs = [zspec(HEADS_W, 4), zspec(HEADS_W, 5), zspec(HEADS_W, 6), zspec(HEADS_W, 7), zspec(LANES, 26),
                pl.BlockSpec((nc, 16, LANES), lambda i: (i, 0, 0)), c_in, st_spec, m_spec,
                pl.BlockSpec((1, LANES), fix2), pl.BlockSpec((16, LANES), fix2), pl.BlockSpec((1, LANES), fix2)]
    return pl.pallas_call(
        functools.partial(_ml_batch_kernel, nc=nc, n_valid=n_valid),
        out_shape=(jax.ShapeDtypeStruct((t, HEADS_W), BF16), jax.ShapeDtypeStruct(n0.shape, F32),
                   jax.ShapeDtypeStruct(n0.shape, F32), jax.ShapeDtypeStruct(m0.shape, F32)),
        grid=(t // rb,),
        in_specs=in_specs,
        out_specs=(pl.BlockSpec((rb, HEADS_W), lambda i: (i, 0)), st_spec, st_spec, m_spec),
        compiler_params=pltpu.CompilerParams(dimension_semantics=("parallel",),
                                             vmem_limit_bytes=VMEM_LIMIT),
        name="mlstm_batch",
    )(z, z, z, z, z, rows, c0, n0, m0, prm["add_row"], prm["rowadd"], prm["ml_normw"])


def _lane_mixers_kernel(zq, zk, zv, zg, lq, lk, lv, lo, zs, cpq, cpk, cpv, cwq, cwk, cwv, bias, alog, dnw, mlw,
                        s0, c0, n0, m0, dn_o, ml_o, s_o, c_o, n_o, m_o, k_s, q_s, *, n_tok):
    h = pl.program_id(0)
    nb = s_o.shape[-1]
    zrow = lambda t, r: zs[t, pl.ds(r, 1), :]
    prow = lambda ref, r: ref[pl.ds(r, 1), :]
    bcast = lambda ref, dk: jnp.broadcast_to(ref[pl.ds(dk, 1), :], (HEAD_DIM, nb))
    zeros = jnp.zeros((HEAD_DIM, nb), F32)

    def conv(z_ref, cp_ref, cw_ref):
        e = [cp_ref[j] for j in range(CONV_W - 1)] + [z_ref[t] for t in range(n_tok)]
        outs = []
        for t in range(n_tok):
            acc = e[t] * cw_ref[0]
            for j in range(1, CONV_W):
                acc = acc + e[t + j] * cw_ref[j]
            outs.append(acc * _sigmoid(acc))
        return outs

    def l2n(x, scale):
        return x * (lax.rsqrt(jnp.sum(x * x, axis=0, keepdims=True) + EPS) * scale)

    def head_norm(x, w):
        return x * lax.rsqrt(jnp.mean(x * x, axis=0, keepdims=True) + EPS) * w

    qc, kc, vc = conv(zq, cpq, cwq), conv(zk, cpk, cwk), conv(zv, cpv, cwv)
    s_o[...] = s0[...]
    for t in range(n_tok):
        k_s[...] = l2n(kc[t], 1.0)
        q_s[...] = l2n(qc[t], HEAD_DIM ** -0.5)
        beta = _sigmoid(zrow(t, h))
        a = jnp.exp(-jnp.exp(prow(alog, N_HEADS + h))
                    * _softplus(zrow(t, N_HEADS + h) + prow(bias, N_HEADS + h)))
        ks = lax.fori_loop(0, HEAD_DIM, lambda dk, acc: acc + bcast(k_s, dk) * s_o[dk], zeros, unroll=8)
        delta = beta * (vc[t] - a * ks)

        def dn_update(dk, acc, a=a, delta=delta):
            s_new = a * s_o[dk] + bcast(k_s, dk) * delta
            s_o[dk] = s_new
            return acc + bcast(q_s, dk) * s_new

        o = lax.fori_loop(0, HEAD_DIM, dn_update, zeros, unroll=8)
        dn_o[t] = (head_norm(o, dnw[...]) * _silu(zg[t])).astype(BF16)

    c_o[...] = c0[...]
    n = n0[...]
    m = m0[...]
    for t in range(n_tok):
        q, k, v = lq[t], lk[t] * (HEAD_DIM ** -0.5), lv[t]
        ig = zrow(t, 2 * N_HEADS + h) + prow(bias, 2 * N_HEADS + h)
        lsf = -_softplus(-(zrow(t, 3 * N_HEADS + h) + prow(bias, 3 * N_HEADS + h)))
        m_new = jnp.maximum(lsf + m, ig)
        fp = jnp.exp(lsf + m - m_new)
        ip = jnp.exp(ig - m_new)
        k_s[...] = k * ip
        q_s[...] = q

        def ml_update(dk, acc, fp=fp, v=v):
            c_new = fp * c_o[dk] + bcast(k_s, dk) * v
            c_o[dk] = c_new
            return acc + bcast(q_s, dk) * c_new

        num = lax.fori_loop(0, HEAD_DIM, ml_update, zeros, unroll=8)
        n = fp * n + ip * k
        den = jnp.sum(q * n, axis=0, keepdims=True)
        hh = num / jnp.maximum(jnp.abs(den), jnp.exp(-m_new))
        m = m_new
        ml_o[t] = (head_norm(hh, mlw[...]) * _sigmoid(lo[t])).astype(BF16)
    n_o[...] = n
    m_o[...] = m


def _lane_mixers(z_t, conv_t, s0, c0, n0, m0, layer, prm):
    n_tok, _, nb = z_t.shape
    hd = HEAD_DIM
    zblk = lambda first: pl.BlockSpec((n_tok, hd, nb), lambda h: (0, first + h, 0))
    cblk = lambda taps, seg: pl.BlockSpec((taps, hd, nb), lambda h: (0, seg * N_HEADS + h, 0))
    wblk = lambda seg: pl.BlockSpec((CONV_W, hd, nb), lambda h: (0, seg * N_HEADS + h, 0))
    whole = lambda shape: pl.BlockSpec(shape, lambda h: (0,) * len(shape))
    mat_in = pl.BlockSpec((None, None, hd, hd, nb), lambda h: (layer, h, 0, 0, 0))
    mat_out = pl.BlockSpec((None, hd, hd, nb), lambda h: (h, 0, 0, 0))
    small_rows = 32
    in_specs = [zblk(0), zblk(6), zblk(12), zblk(18), zblk(24), zblk(30), zblk(36), zblk(42),
                pl.BlockSpec((n_tok, small_rows, nb), lambda h: (0, (Z_COLS - LANES) // small_rows, 0)),
                cblk(CONV_W - 1, 0), cblk(CONV_W - 1, 1), cblk(CONV_W - 1, 2), wblk(0), wblk(1), wblk(2),
                whole((small_rows, nb)), whole((small_rows, nb)), whole((hd, nb)), whole((hd, nb)),
                mat_in, mat_in,
                pl.BlockSpec((None, None, hd, nb), lambda h: (layer, h, 0, 0)),
                pl.BlockSpec((None, None, 1, nb), lambda h: (layer, h, 0, 0))]
    out_blk = pl.BlockSpec((n_tok, hd, nb), lambda h: (0, h, 0))
    return pl.pallas_call(
        functools.partial(_lane_mixers_kernel, n_tok=n_tok),
        out_shape=(jax.ShapeDtypeStruct((n_tok, HEADS_W, nb), BF16), jax.ShapeDtypeStruct((n_tok, HEADS_W, nb), BF16),
                   jax.ShapeDtypeStruct((N_HEADS, hd, hd, nb), F32), jax.ShapeDtypeStruct((N_HEADS, hd, hd, nb), F32),
                   jax.ShapeDtypeStruct((N_HEADS, hd, nb), F32), jax.ShapeDtypeStruct((N_HEADS, 1, nb), F32)),
        grid=(N_HEADS,),
        in_specs=in_specs,
        out_specs=(out_blk, out_blk, mat_out, mat_out,
                   pl.BlockSpec((None, hd, nb), lambda h: (h, 0, 0)), pl.BlockSpec((None, 1, nb), lambda h: (h, 0, 0))),
        scratch_shapes=[pltpu.VMEM((hd, nb), F32), pltpu.VMEM((hd, nb), F32)],
        compiler_params=pltpu.CompilerParams(dimension_semantics=("parallel",), vmem_limit_bytes=VMEM_LIMIT),
        name="lane_mixers",
    )(z_t, z_t, z_t, z_t, z_t, z_t, z_t, z_t, z_t, conv_t, conv_t, conv_t, prm["conv_wb"], prm["conv_wb"],
      prm["conv_wb"], prm["bias_b"], prm["alog_b"], prm["dn_normb"], prm["ml_normb"], s0, c0, n0, m0)


def _pack_w_in(w_in):
    offs = np.concatenate([[0], np.cumsum(IN_SIZES)])
    seg = lambda i: w_in[..., int(offs[i]):int(offs[i + 1])].astype(BF16)
    small = jnp.concatenate([seg(3), seg(4), seg(9), seg(10)], axis=-1)
    small = jnp.pad(small, ((0, 0),) * (w_in.ndim - 1) + ((0, LANES - 4 * N_HEADS),))
    cols = [seg(0), seg(1), seg(2), seg(5), seg(6), seg(7), seg(8), seg(11), seg(12), small]
    return jnp.concatenate(cols, axis=-1)


def _pair_rows(v6):
    return jnp.repeat(v6.reshape(N_PAIRS, 2), HEAD_DIM, axis=1).reshape(N_PAIRS, LANES)


def _layer_params(l, a):
    f = lambda name: a[name][l].astype(F32)
    zeros6 = jnp.zeros((N_HEADS,), F32)
    add_row = jnp.concatenate([zeros6, f("dn_dt_bias"), f("ml_i_bias"), f("ml_f_bias")])
    add_row = jnp.pad(add_row, (0, LANES - 4 * N_HEADS))[None, :]
    alog_row = jnp.pad(jnp.concatenate([zeros6, f("dn_a_log")]), (0, LANES - 2 * N_HEADS))[None, :]
    rowadd = jnp.concatenate([_pair_rows(f("dn_dt_bias")), _pair_rows(f("ml_i_bias")),
                              _pair_rows(f("ml_f_bias")), jnp.zeros((7, LANES), F32)], axis=0)
    rowalog = jnp.concatenate([_pair_rows(f("dn_a_log")), jnp.zeros((13, LANES), F32)], axis=0)
    eye_g = jnp.eye(S5_GROUPS, dtype=F32)
    bd_in = lambda b: jnp.einsum("gpc,gh->gchp", b, eye_g).reshape(S5_WIDTH, S5_STATE).astype(BF16)
    bd_out = lambda c: jnp.einsum("gcp,gh->gphc", c, eye_g).reshape(S5_STATE, S5_WIDTH).astype(BF16)
    return dict(
        layer=l, norm_mix=f("norm_mix")[None, :],
        add_row=add_row, alog_row=alog_row, rowadd=rowadd, rowalog=rowalog,
        conv_w8=jnp.pad(f("dn_conv_w"), ((0, 8 - CONV_W), (0, 0))),
        dn_normw=jnp.tile(f("dn_norm"), 2)[None, :], ml_normw=jnp.tile(f("ml_norm"), 2)[None, :],
        s5=dict(lre=f("s5_lam_re").reshape(1, S5_STATE), lim=f("s5_lam_im").reshape(1, S5_STATE),
                ldt=jnp.repeat(f("s5_log_dt"), S5_P)[None, :],
                bre=bd_in(f("s5_b_re")), bim=bd_in(f("s5_b_im")),
                cre=bd_out(f("s5_c_re")), cim=bd_out(f("s5_c_im")),
                d=f("s5_d").reshape(1, S5_WIDTH), wg=f("s5_w_glu").astype(BF16),
                bg=f("s5_b_glu")[None, :]),
        norm_x=f("norm_x")[None, :], norm_ff=f("norm_ff")[None, :],
    )


def _shared_weights(a):
    cast = lambda name: a[name].astype(BF16)
    return dict(w_in=_pack_w_in(a["w_in"]), w_out=cast("w_out"), w_xq=cast("w_xq"), w_xo=cast("w_xo"),
                w_ff1=cast("w_ff1"), w_ff2=cast("w_ff2"), w_xk=cast("w_xk"), w_xv=cast("w_xv"))


def _row_form(z, n_chunks):
    zs = z[..., Z_COLS - LANES:Z_COLS - LANES + 4 * N_HEADS].reshape(n_chunks, CHUNK, 4, N_PAIRS, 2)
    r = jnp.transpose(zs[:, :, 1:], (0, 2, 3, 4, 1)).reshape(n_chunks, 9, LANES)
    return jnp.pad(r, ((0, 0), (0, 7), (0, 0)))


def _un_bd_pairs(s):
    n = s.shape[0]
    a = s[:, :, :HEAD_DIM, :HEAD_DIM]
    b = s[:, :, HEAD_DIM:, HEAD_DIM:]
    return jnp.stack([a, b], axis=2).reshape(n, N_HEADS, HEAD_DIM, HEAD_DIM)


def _pack_m(m):
    s = m.shape[0]
    r = jnp.repeat(m.reshape(s, N_PAIRS, 2), HEAD_DIM, axis=2).reshape(s, N_PAIRS, LANES)
    return jnp.pad(r, ((0, 0), (0, 8 - N_PAIRS), (0, 0)))


def _unpack_m(mp):
    return mp[:, :N_PAIRS, ::HEAD_DIM].reshape(mp.shape[0], N_HEADS)


ROW_TILE = 1024
POST_TILE = 512
SEQ_BATCH = 8
S5_STEPS = 64
SAMPLE_CHUNKS = 2
ATTN_BATCH = 8
SEQ_SOLVE = (1, True, 1)
SAMPLE_SOLVE = (3, False, 1)


def _tile(n, pref):
    tile = min(n, pref)
    assert n % tile == 0, (n, pref)
    return tile


def _layer_prompt(x, mem_k, mem_v, prm, wts, final_g):
    batch, seq_len, _ = x.shape
    tm = _tile(seq_len, ROW_TILE)
    bb = _tile(batch, SEQ_BATCH)
    layer = prm["layer"]
    z3 = _norm_matmul(x, prm["norm_mix"], wts["w_in"], layer, tm)
    rows4 = _row_form(z3, batch * seq_len // CHUNK).reshape(batch, seq_len // CHUNK, 16, LANES)
    new_conv = z3[:, seq_len - 3:, :CONV_CH]

    dn, s_new = _deltanet_seq(z3, rows4, jnp.zeros((batch, 8, CONV_CH), F32),
                              jnp.zeros((batch, N_PAIRS, LANES, LANES), F32), prm, bb, SEQ_SOLVE)
    ml, cn_new, m_new = _mlstm_seq(z3, rows4, jnp.zeros((batch, N_PAIRS, LANES, 2 * LANES), F32),
                                   jnp.zeros((batch, 8, LANES), F32), prm, bb)
    zero_h = jnp.zeros((batch, S5_STATE), F32)
    s5o, hr, hi = _s5(z3, zero_h, zero_h, prm["s5"], _tile(seq_len, S5_STEPS), S5_STATE, True)

    g_fin = final_g if final_g is not None else prm["norm_ff"]
    x3 = _post_mix(dn, ml, s5o, x, mem_k, mem_v, wts, prm, g_fin, layer, _tile(seq_len, POST_TILE),
                   final_g is not None)
    st = (new_conv, _un_bd_pairs(s_new), _un_bd_pairs(cn_new[..., :LANES]),
          _un_bd_pairs(cn_new[..., LANES:])[..., 0], _unpack_m(m_new),
          hr.reshape(batch, S5_GROUPS, S5_P), hi.reshape(batch, S5_GROUPS, S5_P))
    return x3, st


def _layer_sample(x, cache_k, cache_v, conv_buf, dn_s, ml_c, ml_n, ml_m, ssm_re, ssm_im, prm, wts, batch,
                  seq_len, final_g):
    t = batch * seq_len
    layer = prm["layer"]
    z = _norm_matmul(x, prm["norm_mix"], wts["w_in"], layer, t)
    z3 = z.reshape(batch, seq_len, Z_COLS)
    new_conv = jnp.concatenate([conv_buf, z3[:, :, :CONV_CH]], axis=1)[:, seq_len:]
    lanes = lambda v: jnp.broadcast_to(v[..., None], v.shape + (batch,))
    mix_prm = dict(conv_wb=lanes(prm["conv_w8"][:CONV_W]), bias_b=lanes(prm["add_row"][0, :32]),
                   alog_b=lanes(prm["alog_row"][0, :32]), dn_normb=lanes(prm["dn_normw"][0, :HEAD_DIM]),
                   ml_normb=lanes(prm["ml_normw"][0, :HEAD_DIM]))
    dn_t, ml_t, s_new, c_new, n_new, m_new = _lane_mixers(
        jnp.transpose(z3, (1, 2, 0)), jnp.transpose(conv_buf, (1, 2, 0)), dn_s, ml_c, ml_n, ml_m, layer, mix_prm)
    to_rows = lambda a: jnp.transpose(a, (2, 0, 1)).reshape(1, t, HEADS_W)
    dn, ml = to_rows(dn_t), to_rows(ml_t)

    u_tm = jnp.transpose(z3[:, :, 3072:3072 + S5_WIDTH], (1, 0, 2))
    y_tm, hr, hi = _s5(u_tm.reshape(t, S5_WIDTH), ssm_re.reshape(batch, S5_STATE),
                       ssm_im.reshape(batch, S5_STATE), prm["s5"], seq_len, LANES, False)
    s5o = jnp.transpose(y_tm.reshape(seq_len, batch, S5_WIDTH), (1, 0, 2)).reshape(t, S5_WIDTH)

    x1, q = _out_proj(dn, ml, s5o, x, wts["w_out"], prm["norm_x"], wts["w_xq"], layer, t)
    q3 = jnp.pad(q.reshape(batch, seq_len, X_WIDTH).astype(F32), ((0, 0), (0, SUB - seq_len), (0, 0)))
    att = _attn_batch(q3, cache_k, cache_v, layer, _tile(batch, ATTN_BATCH))[:, :seq_len].reshape(1, t, X_WIDTH)
    g_fin = final_g if final_g is not None else prm["norm_ff"]
    x3 = _ffn(x1, att, wts["w_xo"], prm["norm_ff"], wts["w_ff1"], wts["w_ff2"], g_fin, layer, t,
              final_g is not None)
    st = (new_conv, jnp.transpose(s_new, (3, 0, 1, 2)), jnp.transpose(c_new, (3, 0, 1, 2)),
          jnp.transpose(n_new, (2, 0, 1)), jnp.transpose(m_new[:, 0, :], (1, 0)),
          hr.reshape(batch, S5_GROUPS, S5_P), hi.reshape(batch, S5_GROUPS, S5_P))
    return x3, st


def kernel(x_prompt, x_sample, mem_prompt, state_dn_conv, state_dn_s, state_ml_c, state_ml_n, state_ml_m, state_ssm_re, state_ssm_im, cache_mem_k, cache_mem_v, norm_mix, w_in, dn_conv_w, dn_a_log, dn_dt_bias, dn_norm, ml_i_bias, ml_f_bias, ml_norm, s5_lam_re, s5_lam_im, s5_log_dt, s5_b_re, s5_b_im, s5_c_re, s5_c_im, s5_d, s5_w_glu, s5_b_glu, w_out, norm_x, norm_mem, w_xq, w_xk, w_xv, w_xo, norm_ff, w_ff1, w_ff2, norm_final):
    a = dict(norm_mix=norm_mix, w_in=w_in, dn_conv_w=dn_conv_w, dn_a_log=dn_a_log, dn_dt_bias=dn_dt_bias,
             dn_norm=dn_norm, ml_i_bias=ml_i_bias, ml_f_bias=ml_f_bias, ml_norm=ml_norm,
             s5_lam_re=s5_lam_re, s5_lam_im=s5_lam_im, s5_log_dt=s5_log_dt, s5_b_re=s5_b_re,
             s5_b_im=s5_b_im, s5_c_re=s5_c_re, s5_c_im=s5_c_im, s5_d=s5_d, s5_w_glu=s5_w_glu,
             s5_b_glu=s5_b_glu, w_out=w_out, norm_x=norm_x, w_xq=w_xq, w_xo=w_xo, norm_ff=norm_ff,
             w_ff1=w_ff1, w_ff2=w_ff2)
    prms = [_layer_params(l, a) for l in range(DEPTH)]
    wts = _shared_weights(dict(a, w_xk=w_xk, w_xv=w_xv))
    g_final = norm_final.astype(F32)[None, :]

    bp, lp, _ = x_prompt.shape
    bs, ls, _ = x_sample.shape

    xp = x_prompt
    mk, mk_t, mv, mv_t = _mem_proj(mem_prompt, norm_mem.astype(F32), wts["w_xk"], wts["w_xv"])
    p_st = []
    for l in range(DEPTH):
        xp, st = _layer_prompt(xp, mk, mv, prms[l], wts, g_final if l == DEPTH - 1 else None)
        p_st.append(st)
    y_prompt = xp
    p_mem = (mk_t.reshape(DEPTH, bp, N_MEM, X_HEADS, X_HD), mv_t.reshape(DEPTH, bp, N_MEM, X_HEADS, X_HD))

    xs = x_sample.reshape(1, bs * ls, D_MODEL)
    cache_k = cache_mem_k.reshape(DEPTH, bs, N_MEM * X_HEADS, X_HD)
    cache_v = cache_mem_v.reshape(DEPTH, bs, N_MEM * X_HEADS, X_HD)
    dn_s_t = jnp.transpose(state_dn_s, (0, 2, 3, 4, 1))
    ml_c_t = jnp.transpose(state_ml_c, (0, 2, 3, 4, 1))
    ml_n_t = jnp.transpose(state_ml_n, (0, 2, 3, 1))
    ml_m_t = jnp.transpose(state_ml_m, (0, 2, 1))[:, :, None, :]
    s_st = []
    for l in range(DEPTH):
        xs, st = _layer_sample(xs, cache_k, cache_v, state_dn_conv[l], dn_s_t, ml_c_t,
                               ml_n_t, ml_m_t, state_ssm_re[l], state_ssm_im[l], prms[l], wts, bs,
                               ls, g_final if l == DEPTH - 1 else None)
        s_st.append(st)
    y_sample = xs.reshape(bs, ls, D_MODEL)

    stack = lambda sts, i: jnp.stack([s[i] for s in sts])
    return ((y_prompt, y_sample) + tuple(stack(p_st, i) for i in range(7)) + p_mem
            + tuple(stack(s_st, i) for i in range(7)))
```

```python
import functools
import math

import numpy as np
import jax
import jax.numpy as jnp
from jax import lax
from jax.experimental import pallas as pl
from jax.experimental.pallas import tpu as pltpu

F32 = jnp.float32
BF16 = jnp.bfloat16

D_MODEL = 1024
DEPTH = 2
N_HEADS = 6
HEAD_DIM = 64
N_PAIRS = N_HEADS // 2
HEADS_W = N_HEADS * HEAD_DIM
CONV_W = 4
CONV_CH = 3 * HEADS_W
S5_WIDTH = 256
S5_GROUPS = 16
S5_P = 64
S5_STATE = S5_GROUPS * S5_P
N_MEM = 256
X_HEADS = 4
X_HD = 128
X_WIDTH = X_HEADS * X_HD
D_FF = 4 * D_MODEL
EPS = 1e-6
IN_SIZES = (384, 384, 384, 6, 6, 384, 384, 384, 384, 6, 6, 384, 256)

LANES = 128
CHUNK = 64
Q_ROWS = 8
NEG = -1e30
Z_COLS = 3456
S5_COL_BLOCK = 3072 // S5_WIDTH
GATE_COL_BLOCK = (Z_COLS - LANES) // LANES
VMEM_LIMIT = 52 * 1024 * 1024

ROW_TILE = 1024
POST_TILE = 512
SEQ_BATCH = 8
S5_STEPS = 64
ATTN_BATCH = 8
FF_CHUNK = 1024
TRI_BASE = 8
SEQ_SOLVE = (1, True, 1)

_NN = (((1,), (0,)), ((), ()))
_NT = (((1,), (1,)), ((), ()))


def _dot(a, b, dims=_NN):
    return lax.dot_general(a, b, dims, preferred_element_type=F32)


def _split(a, n):
    out = []
    r = a
    for i in range(n):
        h = r.astype(BF16)
        out.append(h)
        if i + 1 < n:
            r = r - h.astype(F32)
    return out


def _mm(a, b, dims=_NN, prec=1):
    if prec == 1:
        return _dot(a.astype(BF16), b.astype(BF16), dims)
    return _mmp(_parts(a, prec), _parts(b, prec), dims)


def _parts(a, prec):
    return tuple(_split(a, 1 if prec == 1 else 2))


def _mmp(ap, bp, dims=_NN):
    acc = None
    if len(bp) > 1:
        acc = _dot(ap[0], bp[1], dims)
    if len(ap) > 1:
        cross = _dot(ap[1], bp[0], dims)
        acc = cross if acc is None else acc + cross
    lead = _dot(ap[0], bp[0], dims)
    return lead if acc is None else acc + lead


def _mm_sel_r(a, sel, n=3):
    parts = _split(a, n)
    acc = _dot(parts[-1], sel)
    for p_ in parts[-2::-1]:
        acc = acc + _dot(p_, sel)
    return acc


def _mm_sel_l(sel, a, n=3):
    parts = _split(a, n)
    acc = _dot(sel, parts[-1])
    for p_ in parts[-2::-1]:
        acc = acc + _dot(sel, p_)
    return acc


def _sigmoid(x):
    return 1.0 / (1.0 + jnp.exp(-x))


def _silu(x):
    return x * _sigmoid(x)


def _softplus(x):
    return jnp.maximum(x, 0.0) + jnp.log(1.0 + jnp.exp(-jnp.abs(x)))


def _rmsnorm_rows(x, g):
    return x * lax.rsqrt(jnp.mean(x * x, axis=-1, keepdims=True) + EPS) * g


def _iota(shape, dim):
    return lax.broadcasted_iota(jnp.int32, shape, dim)


def _idiv(x, n):
    return x >> (n.bit_length() - 1)


def _imod(x, n):
    return x & (n - 1)


def _lane_lo(shape):
    return _iota(shape, 1) < HEAD_DIM


def _x2(x):
    lo = _lane_lo(x.shape)
    return jnp.concatenate([jnp.where(lo, x, 0.0), jnp.where(lo, 0.0, x)], axis=0)


def _fold(y):
    r = y.shape[0] // 2
    return y[:r] + y[r:]


def _expand(a, c0, c1):
    r = a.shape[0]
    return jnp.where(_lane_lo((r, LANES)), a[:, c0:c0 + 1], a[:, c1:c1 + 1])


def _bd_mask():
    return (_iota((LANES, LANES), 0) < HEAD_DIM) == (_iota((LANES, LANES), 1) < HEAD_DIM)


def _block_ones():
    return jnp.where(_bd_mask(), 1.0, 0.0).astype(BF16)


def _seg_mean(x, bones):
    return _mm_sel_r(x, bones, n=1) * (1.0 / HEAD_DIM)


def _ltri():
    return jnp.where(_iota((CHUNK, CHUNK), 1) <= _iota((CHUNK, CHUNK), 0), 1.0, 0.0).astype(BF16)


def _upair():
    a = _iota((LANES, LANES), 0)
    b = _iota((LANES, LANES), 1)
    return jnp.where(_bd_mask() & (a <= b), 1.0, 0.0).astype(BF16)


def _pair_masks():
    i = _iota((CHUNK, LANES), 0)
    j = _imod(_iota((CHUNK, LANES), 1), HEAD_DIM)
    return j <= i, j < i


def _blk_eq(n, b):
    return _idiv(_iota((n, n), 0), b) == _idiv(_iota((n, n), 1), b)


def _last_row(a):
    return jnp.broadcast_to(a[CHUNK - 1:CHUNK, :], a.shape)


def _tri_inverse_units(mbds, prec, newton):
    n = LANES
    eye = jnp.where(_iota((n, n), 0) == _iota((n, n), 1), 1.0, 0.0)
    n_sq = TRI_BASE.bit_length() - 2
    blk = _blk_eq(n, TRI_BASE)
    m0 = [jnp.where(blk, m, 0.0) for m in mbds]
    ts = [eye - m for m in m0]
    ps = [_mm(m, m, prec=prec) for m in m0]
    for it in range(n_sq):
        if it == n_sq - 1:
            ts = [t + _mm(t, p_, prec=prec) for t, p_ in zip(ts, ps)]
        else:
            both = [_mm(jnp.concatenate([t, p_], axis=0), p_, prec=prec) for t, p_ in zip(ts, ps)]
            ts = [t + b_[:n] for t, b_ in zip(ts, both)]
            ps = [b_[n:] for b_ in both]
    b = TRI_BASE
    while b < CHUNK:
        outer = _blk_eq(n, 2 * b) & jnp.logical_not(_blk_eq(n, b))
        tp = [_parts(t, prec) for t in ts]
        xs = [_mmp(_parts(jnp.where(outer, m, 0.0), prec), t_) for m, t_ in zip(mbds, tp)]
        ts = [t - _mmp(t_, _parts(x, prec)) for t, t_, x in zip(ts, tp, xs)]
        b *= 2
    if newton:
        res = [eye - (t + _mm(m, t, prec=3)) for m, t in zip(mbds, ts)]
        ts = [t + _mm(t, r) for t, r in zip(ts, res)]
    return ts


def _wspec(rows, cols, layer, row_block=0):
    return pl.BlockSpec((None, rows, cols), lambda i: (layer, row_block, 0), pipeline_mode=pl.Buffered(1))


def _rows3(nt):
    return lambda i: (i // nt, i % nt, 0)


def _dense_call(kernel_fn, name, out_shape, in_specs, out_specs, steps, args):
    return pl.pallas_call(
        kernel_fn, out_shape=out_shape, grid=(steps,), in_specs=in_specs, out_specs=out_specs,
        compiler_params=pltpu.CompilerParams(dimension_semantics=("parallel",), vmem_limit_bytes=VMEM_LIMIT),
        name=name)(*args)


def _in_proj_kernel(x_ref, g_ref, w_ref, o_ref):
    o_ref[0] = _dot(_rmsnorm_rows(x_ref[0], g_ref[...]).astype(BF16), w_ref[...])


def _norm_matmul(x, g, w, layer, tm):
    b, l, k = x.shape
    n = w.shape[2]
    nt = l // tm
    fix = lambda i: (0, 0)
    return _dense_call(
        _in_proj_kernel, "norm_matmul", jax.ShapeDtypeStruct((b, l, n), F32),
        [pl.BlockSpec((1, tm, k), _rows3(nt)), pl.BlockSpec((1, k), fix), _wspec(k, n, layer)],
        pl.BlockSpec((1, tm, n), _rows3(nt)), b * nt, (x, g, w))


def _mem_proj_kernel(x_ref, g_ref, wk_ref, wv_ref, k_ref, kt_ref, v_ref, vt_ref):
    x = x_ref[0]
    xhat = x * lax.rsqrt(jnp.mean(x * x, axis=-1, keepdims=True) + EPS)
    for l in range(DEPTH):
        xn = (xhat * g_ref[l:l + 1, :]).astype(BF16)
        for w_ref, o_ref, t_ref in ((wk_ref, k_ref, kt_ref), (wv_ref, v_ref, vt_ref)):
            res = _dot(xn, w_ref[l])
            o_ref[l, 0] = res
            for h in range(X_HEADS):
                t_ref[l, 0, pl.ds(h, N_MEM, stride=X_HEADS), :] = res[:, h * X_HD:(h + 1) * X_HD]


def _mem_proj(mem, g, wk, wv):
    b = mem.shape[0]
    whole = lambda shape: pl.BlockSpec(shape, lambda i: (0,) * len(shape))
    o_spec = pl.BlockSpec((DEPTH, 1, N_MEM, X_WIDTH), lambda i: (0, i, 0, 0))
    t_spec = pl.BlockSpec((DEPTH, 1, N_MEM * X_HEADS, X_HD), lambda i: (0, i, 0, 0))
    o_shape = jax.ShapeDtypeStruct((DEPTH, b, N_MEM, X_WIDTH), F32)
    t_shape = jax.ShapeDtypeStruct((DEPTH, b, N_MEM * X_HEADS, X_HD), F32)
    return _dense_call(
        _mem_proj_kernel, "mem_proj", (o_shape, t_shape, o_shape, t_shape),
        [pl.BlockSpec((1, N_MEM, D_MODEL), lambda i: (i, 0, 0)), whole((DEPTH, D_MODEL)),
         whole((DEPTH, D_MODEL, X_WIDTH)), whole((DEPTH, D_MODEL, X_WIDTH))],
        (o_spec, t_spec, o_spec, t_spec), b, (mem, g, wk, wv))


def _mix_residual(x, dn, ml, s5, w1_ref, w2_ref, w3_ref):
    acc = x + _dot(dn, w1_ref[...])
    acc = acc + _dot(ml, w2_ref[...])
    return acc + _dot(s5, w3_ref[...])


def _mlp_residual(x2, g_ref, f1_ref, f2_ref):
    xn = _rmsnorm_rows(x2, g_ref[...]).astype(BF16)
    acc = x2
    for c in range(D_FF // FF_CHUNK):
        hf = _dot(xn, f1_ref[:, c * FF_CHUNK:(c + 1) * FF_CHUNK])
        a = jnp.square(jnp.maximum(hf, 0.0)).astype(BF16)
        acc = acc + _dot(a, f2_ref[c * FF_CHUNK:(c + 1) * FF_CHUNK, :])
    return acc


def _out_proj_kernel(dn_ref, ml_ref, s5_ref, x_ref, w1_ref, w2_ref, w3_ref, g_ref, wq_ref, x1_ref, q_ref):
    acc = _mix_residual(x_ref[0], dn_ref[0], ml_ref[0], s5_ref[...], w1_ref, w2_ref, w3_ref)
    x1_ref[0] = acc
    q_ref[0] = _dot(_rmsnorm_rows(acc, g_ref[...]).astype(BF16), wq_ref[...])


def _w_out_specs(layer):
    return [_wspec(HEADS_W, D_MODEL, layer, 0), _wspec(HEADS_W, D_MODEL, layer, 1),
            _wspec(S5_WIDTH, D_MODEL, layer, 2 * HEADS_W // S5_WIDTH)]


def _out_proj(dn, ml, s5, x, wts, prm, layer, tm):
    b, l, _ = x.shape
    nt = l // tm
    row = _rows3(nt)
    vec = pl.BlockSpec((1, D_MODEL), lambda i: (0, 0))
    return _dense_call(
        _out_proj_kernel, "out_proj",
        (jax.ShapeDtypeStruct((b, l, D_MODEL), F32), jax.ShapeDtypeStruct((b, l, X_WIDTH), F32)),
        [pl.BlockSpec((1, tm, HEADS_W), row), pl.BlockSpec((1, tm, HEADS_W), row),
         pl.BlockSpec((tm, S5_WIDTH), lambda i: (i % nt, i // nt)), pl.BlockSpec((1, tm, D_MODEL), row)]
        + _w_out_specs(layer) + [vec, _wspec(D_MODEL, X_WIDTH, layer)],
        (pl.BlockSpec((1, tm, D_MODEL), row), pl.BlockSpec((1, tm, X_WIDTH), row)), b * nt,
        (dn, ml, s5, x, wts["w_out"], wts["w_out"], wts["w_out"], prm["norm_x"], wts["w_xq"]))


def _ffn_kernel(x_ref, o_ref, wo_ref, g_ref, f1_ref, f2_ref, gl_ref, y_ref, *, final_norm):
    acc = _mlp_residual(x_ref[0] + _dot(o_ref[0], wo_ref[...]), g_ref, f1_ref, f2_ref)
    y_ref[0] = _rmsnorm_rows(acc, gl_ref[...]) if final_norm else acc


def _ffn(x, att, wts, prm, g_last, layer, tm, final_norm):
    b, l, _ = x.shape
    nt = l // tm
    row = _rows3(nt)
    vec = pl.BlockSpec((1, D_MODEL), lambda i: (0, 0))
    return _dense_call(
        functools.partial(_ffn_kernel, final_norm=final_norm), "ffn", jax.ShapeDtypeStruct((b, l, D_MODEL), F32),
        [pl.BlockSpec((1, tm, D_MODEL), row), pl.BlockSpec((1, tm, X_WIDTH), row), _wspec(X_WIDTH, D_MODEL, layer),
         vec, _wspec(D_MODEL, D_FF, layer), _wspec(D_FF, D_MODEL, layer), vec],
        pl.BlockSpec((1, tm, D_MODEL), row), b * nt,
        (x, att, wts["w_xo"], prm["norm_ff"], wts["w_ff1"], wts["w_ff2"], g_last))


def _attn_heads(q, k_ref, v_ref):
    outs = []
    for h in range(X_HEADS):
        sl = slice(h * X_HD, (h + 1) * X_HD)
        s = _dot(q[:, sl], k_ref[0, :, sl].astype(BF16), _NT) * (X_HD ** -0.5)
        e = jnp.exp(s - jnp.max(s, axis=-1, keepdims=True))
        p = e / jnp.sum(e, axis=-1, keepdims=True)
        outs.append(_dot(p.astype(BF16), v_ref[0, :, sl].astype(BF16)))
    return jnp.concatenate(outs, axis=-1)


def _attn_batch_kernel(q_ref, k_ref, v_ref, o_ref, *, bb):
    rows = X_HEADS * Q_ROWS
    col_head = _imod(_iota((rows, N_MEM * X_HEADS), 1), X_HEADS)
    row_head = _idiv(_iota((rows, N_MEM * X_HEADS), 0), Q_ROWS)
    own = col_head == row_head
    q4 = [jnp.concatenate([q_ref[bi, :, h * X_HD:(h + 1) * X_HD] for h in range(X_HEADS)], axis=0).astype(BF16)
          for bi in range(bb)]
    s = [jnp.where(own, _dot(q_, k_ref[bi].astype(BF16), _NT) * (X_HD ** -0.5), NEG)
         for bi, q_ in enumerate(q4)]
    e = [jnp.exp(x - jnp.max(x, axis=-1, keepdims=True)) for x in s]
    p = [x / jnp.sum(x, axis=-1, keepdims=True) for x in e]
    o = [_dot(x.astype(BF16), v_ref[bi].astype(BF16)) for bi, x in enumerate(p)]
    for bi in range(bb):
        for h in range(X_HEADS):
            o_ref[bi, :, h * X_HD:(h + 1) * X_HD] = o[bi][h * Q_ROWS:(h + 1) * Q_ROWS].astype(BF16)


def _attn_batch(q, k, v, layer, bb):
    b, l, _ = q.shape
    blk = lambda i: (i, 0, 0)
    kv = pl.BlockSpec((None, bb, N_MEM * X_HEADS, X_HD), lambda i: (layer, i, 0, 0))
    return _dense_call(
        functools.partial(_attn_batch_kernel, bb=bb), "attn_batch", jax.ShapeDtypeStruct((b, l, X_WIDTH), BF16),
        [pl.BlockSpec((bb, l, X_WIDTH), blk), kv, kv], pl.BlockSpec((bb, l, X_WIDTH), blk), b // bb, (q, k, v))


def _post_mix_kernel(dn_ref, ml_ref, s5_ref, x_ref, k_ref, v_ref, w1_ref, w2_ref, w3_ref, gx_ref, wq_ref,
                     wo_ref, gf_ref, f1_ref, f2_ref, gl_ref, y_ref, *, final_norm):
    x1 = _mix_residual(x_ref[0], dn_ref[0], ml_ref[0], s5_ref[...], w1_ref, w2_ref, w3_ref)
    q = _dot(_rmsnorm_rows(x1, gx_ref[...]).astype(BF16), wq_ref[...]).astype(BF16)
    att = _attn_heads(q, k_ref, v_ref).astype(BF16)
    acc = _mlp_residual(x1 + _dot(att, wo_ref[...]), gf_ref, f1_ref, f2_ref)
    y_ref[0] = _rmsnorm_rows(acc, gl_ref[...]) if final_norm else acc


def _post_mix(dn, ml, s5, x, mem_k, mem_v, wts, prm, g_last, layer, tm, final_norm):
    b, l, _ = x.shape
    nt = l // tm
    row = _rows3(nt)
    mem = pl.BlockSpec((None, 1, N_MEM, X_WIDTH), lambda i: (layer, i // nt, 0, 0))
    vec = pl.BlockSpec((1, D_MODEL), lambda i: (0, 0))
    return _dense_call(
        functools.partial(_post_mix_kernel, final_norm=final_norm), "post_mix",
        jax.ShapeDtypeStruct((b, l, D_MODEL), F32),
        [pl.BlockSpec((1, tm, HEADS_W), row), pl.BlockSpec((1, tm, HEADS_W), row),
         pl.BlockSpec((tm, S5_WIDTH), lambda i: (i % nt, i // nt)), pl.BlockSpec((1, tm, D_MODEL), row), mem, mem]
        + _w_out_specs(layer)
        + [vec, _wspec(D_MODEL, X_WIDTH, layer), _wspec(X_WIDTH, D_MODEL, layer), vec,
           _wspec(D_MODEL, D_FF, layer), _wspec(D_FF, D_MODEL, layer), vec],
        pl.BlockSpec((1, tm, D_MODEL), row), b * nt,
        (dn, ml, s5, x, mem_k, mem_v, wts["w_out"], wts["w_out"], wts["w_out"], prm["norm_x"], wts["w_xq"],
         wts["w_xo"], prm["norm_ff"], wts["w_ff1"], wts["w_ff2"], g_last))


def _s5_kernel(u_ref, h0r_ref, h0i_ref, lre_ref, lim_ref, ldt_ref, bre_ref, bim_ref, cre_ref, cim_ref,
               d_ref, wg_ref, bg_ref, y_ref, hr_out, hi_out,
               xr_s, xi_s, st_r, st_i, coef_s, bbr_s, bbi_s, *io_s, tb, nb, lane_chunk, seq_major_io):
    step = pl.program_id(0)
    if seq_major_io:
        u_s, y_s = io_s
        for b in range(nb):
            for j in range(S5_WIDTH // LANES):
                u_s[j, pl.ds(b, tb, stride=nb), :] = u_ref[b, :, j * LANES:(j + 1) * LANES]
        u = jnp.concatenate([u_s[j] for j in range(S5_WIDTH // LANES)], axis=1)
    else:
        u = u_ref[...]

    @pl.when(step == 0)
    def _():
        lr = lre_ref[...]
        li = lim_ref[...]
        dt = jnp.exp(ldt_ref[...])
        mag = jnp.exp(lr * dt)
        ar = mag * jnp.cos(li * dt)
        ai = mag * jnp.sin(li * dt)
        inv = 1.0 / (lr * lr + li * li)
        coef_s[0:1, :] = ar
        coef_s[1:2, :] = ai
        cr = ((ar - 1.0) * lr + ai * li) * inv
        ci = (ai * lr - (ar - 1.0) * li) * inv
        bbr_s[...] = (cr * bre_ref[...] - ci * bim_ref[...]).astype(BF16)
        bbi_s[...] = (cr * bim_ref[...] + ci * bre_ref[...]).astype(BF16)
        st_r[...] = h0r_ref[...]
        st_i[...] = h0i_ref[...]

    ub = u.astype(BF16)
    xr_s[...] = _dot(ub, bbr_s[...])
    xi_s[...] = _dot(ub, bbi_s[...])

    for lc in range(S5_STATE // lane_chunk):
        sl = slice(lc * lane_chunk, (lc + 1) * lane_chunk)
        ar = jnp.broadcast_to(coef_s[0:1, sl], (nb, lane_chunk))
        ai = jnp.broadcast_to(coef_s[1:2, sl], (nb, lane_chunk))

        def body(t, carry, sl=sl, ar=ar, ai=ai):
            hr, hi = carry
            r0 = pl.multiple_of(t * nb, 8)
            nhr = ar * hr - ai * hi + xr_s[pl.ds(r0, nb), sl]
            nhi = ar * hi + ai * hr + xi_s[pl.ds(r0, nb), sl]
            xr_s[pl.ds(r0, nb), sl] = nhr
            xi_s[pl.ds(r0, nb), sl] = nhi
            return nhr, nhi

        hr, hi = lax.fori_loop(0, tb, body, (st_r[:, sl], st_i[:, sl]), unroll=min(tb, 8))
        st_r[:, sl] = hr
        st_i[:, sl] = hi

    y = (_dot(xr_s[...].astype(BF16), cre_ref[...]) - _dot(xi_s[...].astype(BF16), cim_ref[...])
         + d_ref[...] * u)
    zg = 0.5 * y * (1.0 + jnp.tanh(math.sqrt(2.0 / math.pi) * (y + 0.044715 * (y * y * y))))
    gate = _sigmoid(_dot(zg.astype(BF16), wg_ref[...]) + bg_ref[...])
    if seq_major_io:
        yv = zg * gate
        for j in range(S5_WIDTH // LANES):
            y_s[j] = yv[:, j * LANES:(j + 1) * LANES]
        for b in range(nb):
            for j in range(S5_WIDTH // LANES):
                c0 = b * S5_WIDTH + j * LANES
                y_ref[:, c0:c0 + LANES] = y_s[j, pl.ds(b, tb, stride=nb), :].astype(BF16)
    else:
        y_ref[...] = (zg * gate).astype(BF16)

    @pl.when(step == pl.num_programs(0) - 1)
    def _():
        hr_out[...] = st_r[...]
        hi_out[...] = st_i[...]


def _s5(u, h0r, h0i, prm, tb, lane_chunk, seq_major_io):
    nb = h0r.shape[0]
    fix2 = lambda i: (0, 0)
    vec = pl.BlockSpec((1, S5_STATE), fix2)
    st = pl.BlockSpec((nb, S5_STATE), fix2)
    rows = tb * nb
    scratch = [pltpu.VMEM((rows, S5_STATE), F32), pltpu.VMEM((rows, S5_STATE), F32),
               pltpu.VMEM((nb, S5_STATE), F32), pltpu.VMEM((nb, S5_STATE), F32),
               pltpu.VMEM((8, S5_STATE), F32),
               pltpu.VMEM((S5_WIDTH, S5_STATE), BF16), pltpu.VMEM((S5_WIDTH, S5_STATE), BF16)]
    if seq_major_io:
        l = u.shape[1]
        u_spec = pl.BlockSpec((nb, tb, S5_WIDTH), lambda i: (0, i, S5_COL_BLOCK))
        y_shape = jax.ShapeDtypeStruct((l, nb * S5_WIDTH), BF16)
        y_spec = pl.BlockSpec((tb, nb * S5_WIDTH), lambda i: (i, 0))
        io_shape = (S5_WIDTH // LANES, rows, LANES)
        scratch += [pltpu.VMEM(io_shape, F32), pltpu.VMEM(io_shape, F32)]
    else:
        l = u.shape[0] // nb
        u_spec = pl.BlockSpec((rows, S5_WIDTH), lambda i: (i, 0))
        y_shape = jax.ShapeDtypeStruct((l * nb, S5_WIDTH), BF16)
        y_spec = pl.BlockSpec((rows, S5_WIDTH), lambda i: (i, 0))
    return pl.pallas_call(
        functools.partial(_s5_kernel, tb=tb, nb=nb, lane_chunk=lane_chunk, seq_major_io=seq_major_io),
        out_shape=(y_shape, jax.ShapeDtypeStruct((nb, S5_STATE), F32), jax.ShapeDtypeStruct((nb, S5_STATE), F32)),
        grid=(l // tb,),
        in_specs=[u_spec, st, st, vec, vec, vec,
                  pl.BlockSpec((S5_WIDTH, S5_STATE), fix2), pl.BlockSpec((S5_WIDTH, S5_STATE), fix2),
                  pl.BlockSpec((S5_STATE, S5_WIDTH), fix2), pl.BlockSpec((S5_STATE, S5_WIDTH), fix2),
                  pl.BlockSpec((1, S5_WIDTH), fix2), pl.BlockSpec((S5_WIDTH, S5_WIDTH), fix2),
                  pl.BlockSpec((1, S5_WIDTH), fix2)],
        out_specs=(y_spec, st, st),
        scratch_shapes=scratch,
        compiler_params=pltpu.CompilerParams(dimension_semantics=("arbitrary",),
                                             vmem_limit_bytes=VMEM_LIMIT),
        name="s5",
    )(u, h0r, h0i, prm["lre"], prm["lim"], prm["ldt"], prm["bre"], prm["bim"], prm["cre"],
      prm["cim"], prm["d"], prm["wg"], prm["bg"])


def _conv_silu(u, w8, fix):
    out = fix(pltpu.roll(u, 3, axis=0), 3) * w8[0:1, :]
    for s in (2, 1):
        out = out + fix(pltpu.roll(u, s, axis=0), s) * w8[3 - s:4 - s, :]
    out = out + u * w8[3:4, :]
    return _silu(out)


def _l2norm_pairs(x_chunks, bones, scale):
    n = len(x_chunks)
    x = jnp.concatenate(x_chunks, axis=0) if n > 1 else x_chunks[0]
    out = [[None] * N_PAIRS for _ in range(n)]
    for p_ in range(N_PAIRS):
        xp = x[:, p_ * LANES:(p_ + 1) * LANES]
        xn = xp * (lax.rsqrt(_seg_mean(xp * xp, bones) * HEAD_DIM + EPS) * scale)
        for c in range(n):
            out[c][p_] = xn[c * CHUNK:(c + 1) * CHUNK]
    return out


def _head_norm_store(vals, gates, normw, bones, act, store):
    n = len(vals)
    for p_ in range(N_PAIRS):
        v = jnp.concatenate([vals[c][p_] for c in range(n)], axis=0) if n > 1 else vals[0][p_]
        g = jnp.concatenate([gates[c][p_] for c in range(n)], axis=0) if n > 1 else gates[0][p_]
        y = v * lax.rsqrt(_seg_mean(v * v, bones) + EPS) * normw * act(g)
        for c in range(n):
            store(c, p_, y[c * CHUNK:(c + 1) * CHUNK].astype(BF16))


def _dn_local(chunks, prm, prec):
    n = len(chunks)
    bones = prm["bones"]
    incl, strict = _pair_masks()
    ltri, upair = _ltri(), _upair()
    units = [(c, p_) for c in range(n) for p_ in range(N_PAIRS)]

    g_cols = _mm_sel_l(ltri, jnp.concatenate([-jnp.exp(prm["alog_row"]) * _softplus(ch["zs"]) for ch in chunks],
                                             axis=1))
    g_rows = _mm_sel_r(jnp.concatenate([-jnp.exp(prm["rowalog"]) * _softplus(ch["rows"] + prm["rowadd"])
                                        for ch in chunks], axis=0), upair)
    gates = [(_sigmoid(ch["zs"]), g_cols[:, c * LANES:(c + 1) * LANES], g_rows[c * 16:(c + 1) * 16])
             for c, ch in enumerate(chunks)]

    qn = _l2norm_pairs([ch["q"] for ch in chunks], bones, HEAD_DIM ** -0.5)
    kn = _l2norm_pairs([ch["k"] for ch in chunks], bones, 1.0)

    beta = [_expand(gates[c][0], 2 * p_, 2 * p_ + 1) for c, p_ in units]
    g_col = [_expand(gates[c][1], 6 + 2 * p_, 7 + 2 * p_) for c, p_ in units]
    dec = [jnp.exp(jnp.where(incl, gc - gates[c][2][p_:p_ + 1, :], NEG)) for gc, (c, p_) in zip(g_col, units)]
    kb = [kn[c][p_] * b for b, (c, p_) in zip(beta, units)]
    sc = [_mm(jnp.concatenate([kb_, qn[c][p_]], axis=0), _x2(kn[c][p_]), _NT)
          for kb_, (c, p_) in zip(kb, units)]
    a_in = [s[CHUNK:] * d for s, d in zip(sc, dec)]
    mbd = [_x2(jnp.where(strict, s[:CHUNK] * d, 0.0)) for s, d in zip(sc, dec)]
    tri_prec, newton, sol_prec = prec
    t_bd = _tri_inverse_units(mbd, tri_prec, newton)
    eg = [jnp.exp(gc) for gc in g_col]
    rhs = [jnp.concatenate([_x2(chunks[c]["v"][:, p_ * LANES:(p_ + 1) * LANES] * b), _x2(kb_ * e)], axis=1)
           for b, kb_, e, (c, p_) in zip(beta, kb, eg, units)]
    sol = [_mm(t, r, prec=sol_prec) for t, r in zip(t_bd, rhs)]
    out = [[None] * N_PAIRS for _ in range(n)]
    for i, (c, p_) in enumerate(units):
        g_last = _last_row(g_col[i])
        out[c][p_] = dict(u=_fold(sol[i][:, :LANES]), w=_fold(sol[i][:, LANES:]), a_in=a_in[i],
                          qg=qn[c][p_] * eg[i], kd=kn[c][p_] * jnp.exp(g_last - g_col[i]),
                          eg_last=jnp.exp(g_last))
    return out


def _dn_seq_step(zq_ref, zk_ref, zv_ref, zg_ref, zs_ref, rows_ref, cw_ref,
                 add_ref, alog_ref, rowadd_ref, rowalog_ref, nw_ref, o_ref, s_scr, prev_scr, *, bb, prec):
    r8 = _iota((8, CONV_CH), 0)
    cw = cw_ref[...]
    chunks = []
    for b in range(bb):
        prev = prev_scr[b]

        def fix(rolled, s, prev=prev):
            first = jnp.where(r8 < s, pltpu.roll(prev, s, axis=0), rolled[0:8])
            return jnp.concatenate([first, rolled[8:]], axis=0)

        u_b = jnp.concatenate([zq_ref[b], zk_ref[b], zv_ref[b]], axis=1)
        qkv = _conv_silu(u_b, cw, fix)
        prev_scr[b] = u_b[CHUNK - 8:CHUNK]
        chunks.append(dict(q=qkv[:, 0:HEADS_W], k=qkv[:, HEADS_W:2 * HEADS_W], v=qkv[:, 2 * HEADS_W:],
                           zs=zs_ref[b] + add_ref[...], rows=rows_ref[b, 0]))

    prm = dict(bones=_block_ones(), alog_row=alog_ref[...], rowadd=rowadd_ref[...],
               rowalog=rowalog_ref[...])
    loc = _dn_local(chunks, prm, prec)
    bdm = _bd_mask()
    units = [(b, p_) for b in range(bb) for p_ in range(N_PAIRS)]
    s_prev = [s_scr[b, p_] for b, p_ in units]
    ws = [_mm(jnp.concatenate([loc[b][p_]["w"], loc[b][p_]["qg"]], axis=0), s)
          for s, (b, p_) in zip(s_prev, units)]
    v_new = [loc[b][p_]["u"] - w_[:CHUNK] for w_, (b, p_) in zip(ws, units)]
    o = [w_[CHUNK:] + _mm(loc[b][p_]["a_in"], _x2(v)) for w_, v, (b, p_) in zip(ws, v_new, units)]
    upd = [_mm(loc[b][p_]["kd"].T, v) for v, (b, p_) in zip(v_new, units)]
    for i, (b, p_) in enumerate(units):
        s_scr[b, p_] = loc[b][p_]["eg_last"][0:1, :] * s_prev[i] + jnp.where(bdm, upd[i], 0.0)

    def store(b, p_, val):
        o_ref[b, :, p_ * LANES:(p_ + 1) * LANES] = val

    _head_norm_store([[o[b * N_PAIRS + p_] for p_ in range(N_PAIRS)] for b in range(bb)],
                     [[zg_ref[b, :, p_ * LANES:(p_ + 1) * LANES] for p_ in range(N_PAIRS)] for b in range(bb)],
                     nw_ref[...], prm["bones"], _silu, store)


def _ml_units(chunks, prm):
    n = len(chunks)
    incl, _ = _pair_masks()
    lo = _lane_lo((CHUNK, LANES))
    ltri, upair = _ltri(), _upair()
    ones2 = prm["bones"]
    bdm2 = jnp.concatenate([_bd_mask(), _bd_mask()], axis=1)
    units = [(c, p_) for c in range(n) for p_ in range(N_PAIRS)]
    pair = lambda x, p_: x[:, p_ * LANES:(p_ + 1) * LANES]

    rows = [ch["rows"] + prm["rowadd"] for ch in chunks]
    b_cols = _mm_sel_l(ltri, jnp.concatenate([-_softplus(-ch["zs"]) for ch in chunks], axis=1))
    b_rows = _mm_sel_r(jnp.concatenate([-_softplus(-r) for r in rows], axis=0), upair)
    comp = [dict(ig_c=ch["zs"], b_c=b_cols[:, c * LANES:(c + 1) * LANES], ig_r=rows[c],
                 b_r=b_rows[c * 16:(c + 1) * 16]) for c, ch in enumerate(chunks)]

    qp = [pair(chunks[c]["q"], p_) for c, p_ in units]
    kp = [pair(chunks[c]["k"], p_) * (HEAD_DIM ** -0.5) for c, p_ in units]
    vp = [pair(chunks[c]["v"], p_) for c, p_ in units]
    b_col = [_expand(comp[c]["b_c"], 18 + 2 * p_, 19 + 2 * p_) for c, p_ in units]
    ig_col = [_expand(comp[c]["ig_c"], 12 + 2 * p_, 13 + 2 * p_) for c, p_ in units]
    dl = [jnp.where(incl, bc - comp[c]["b_r"][6 + p_:7 + p_, :] + comp[c]["ig_r"][3 + p_:4 + p_, :], NEG)
          for bc, (c, p_) in zip(b_col, units)]
    mx = [jnp.where(lo, jnp.max(jnp.where(lo, d, NEG), axis=1, keepdims=True),
                    jnp.max(jnp.where(lo, NEG, d), axis=1, keepdims=True)) for d in dl]
    inter = [bc + chunks[c]["ms"][p_] for bc, (c, p_) in zip(b_col, units)]
    mt = [jnp.maximum(a, b) for a, b in zip(inter, mx)]
    s = [_mm(q_, _x2(k_), _NT) * jnp.exp(d - m) for q_, k_, d, m in zip(qp, kp, dl, mt)]
    sv = [_mm(s_, jnp.concatenate([_x2(v_), ones2], axis=1)) for s_, v_ in zip(s, vp)]
    cn_prev = [chunks[c]["cn"][p_] for c, p_ in units]
    qcn = [_mm(q_, cn) for q_, cn in zip(qp, cn_prev)]
    w_inter = [jnp.exp(a - m) for a, m in zip(inter, mt)]
    h = [(wi * qc[:, :LANES] + sv_[:, :LANES])
         / jnp.maximum(jnp.abs(wi * qc[:, LANES:] + sv_[:, LANES:]), jnp.exp(-m))
         for wi, qc, sv_, m in zip(w_inter, qcn, sv, mt)]
    m_end = [_last_row(m) for m in mt]
    b_last = [_last_row(bc) for bc in b_col]
    a_end = [jnp.exp(bl + chunks[c]["ms"][p_] - me) for bl, me, (c, p_) in zip(b_last, m_end, units)]
    kw_t = [(k_ * jnp.exp(bl - bc + ig - me)).T for k_, bl, bc, ig, me in zip(kp, b_last, b_col, ig_col, m_end)]
    upd = [_mm(k_, jnp.concatenate([v_, jnp.ones((CHUNK, LANES), F32)], axis=1)) for k_, v_ in zip(kw_t, vp)]
    out = [[None] * N_PAIRS for _ in range(n)]
    for i, (c, p_) in enumerate(units):
        a_row = a_end[i][0:1, :]
        new_state = jnp.concatenate([a_row, a_row], axis=1) * cn_prev[i] + jnp.where(bdm2, upd[i], 0.0)
        out[c][p_] = (h[i], new_state, m_end[i][0:1, :])
    return out


def _ml_seq_step(zq_ref, zk_ref, zv_ref, zo_ref, zs_ref, rows_ref, add_ref, rowadd_ref, nw_ref, o_ref,
                 cn_scr, m_scr, *, bb):
    prm = dict(bones=_block_ones(), rowadd=rowadd_ref[...])
    chunks = []
    for b in range(bb):
        chunks.append(dict(q=zq_ref[b], k=zk_ref[b], v=zv_ref[b], zs=zs_ref[b] + add_ref[...],
                           rows=rows_ref[b, 0],
                           ms=[m_scr[b, p_:p_ + 1, :] for p_ in range(N_PAIRS)],
                           cn=[cn_scr[b, p_] for p_ in range(N_PAIRS)]))
    res = _ml_units(chunks, prm)
    for b in range(bb):
        for p_ in range(N_PAIRS):
            cn_scr[b, p_] = res[b][p_][1]
            m_scr[b, p_:p_ + 1, :] = res[b][p_][2]

    def store(b, p_, val):
        o_ref[b, :, p_ * LANES:(p_ + 1) * LANES] = val

    _head_norm_store([[res[b][p_][0] for p_ in range(N_PAIRS)] for b in range(bb)],
                     [[zo_ref[b, :, p_ * LANES:(p_ + 1) * LANES] for p_ in range(N_PAIRS)] for b in range(bb)],
                     nw_ref[...], prm["bones"], _sigmoid, store)


def _mixers_seq_kernel(dq_ref, dk_ref, dv_ref, dg_ref, lq_ref, lk_ref, lv_ref, lo_ref, zs_ref, rows_ref,
                       cp_ref, cw_ref, s0_ref, cn0_ref, m0_ref,
                       add_ref, alog_ref, rowadd_ref, rowalog_ref, dnw_ref, mlw_ref,
                       dn_ref, ml_ref, s_out_ref, cn_out_ref, m_out_ref,
                       s_scr, prev_scr, cn_scr, m_scr, *, bb, prec):
    tstep = pl.program_id(1)

    @pl.when(tstep == 0)
    def _():
        s_scr[...] = s0_ref[...]
        prev_scr[...] = cp_ref[...]
        cn_scr[...] = cn0_ref[...]
        m_scr[...] = m0_ref[...]

    _dn_seq_step(dq_ref, dk_ref, dv_ref, dg_ref, zs_ref, rows_ref, cw_ref, add_ref, alog_ref, rowadd_ref,
                 rowalog_ref, dnw_ref, dn_ref, s_scr, prev_scr, bb=bb, prec=prec)
    _ml_seq_step(lq_ref, lk_ref, lv_ref, lo_ref, zs_ref, rows_ref, add_ref, rowadd_ref, mlw_ref, ml_ref,
                 cn_scr, m_scr, bb=bb)

    @pl.when(tstep == pl.num_programs(1) - 1)
    def _():
        s_out_ref[...] = s_scr[...]
        cn_out_ref[...] = cn_scr[...]
        m_out_ref[...] = m_scr[...]


def _mixers_seq(z3, rows4, convp, s0, cn0, m0, prm, bb, prec):
    batch, seq_len, _ = z3.shape
    fix2 = lambda g, i: (0, 0)
    zspec = lambda width, col: pl.BlockSpec((bb, CHUNK, width), lambda g, i: (g, i, col))
    per_seq = lambda *shape: pl.BlockSpec((bb,) + shape, lambda g, i: (g,) + (0,) * len(shape))
    s_spec, cn_spec, m_spec = per_seq(N_PAIRS, LANES, LANES), per_seq(N_PAIRS, LANES, 2 * LANES), per_seq(8, LANES)
    small = lambda rows: pl.BlockSpec((rows, LANES), fix2)
    in_specs = ([zspec(HEADS_W, c) for c in range(8)] + [zspec(LANES, GATE_COL_BLOCK)]
                + [pl.BlockSpec((bb, 1, 16, LANES), lambda g, i: (g, i, 0, 0)), per_seq(8, CONV_CH),
                   pl.BlockSpec((8, CONV_CH), fix2), s_spec, cn_spec, m_spec,
                   small(1), small(1), small(16), small(16), small(1), small(1)])
    out_blk = pl.BlockSpec((bb, CHUNK, HEADS_W), lambda g, i: (g, i, 0))
    act = jax.ShapeDtypeStruct((batch, seq_len, HEADS_W), BF16)
    return pl.pallas_call(
        functools.partial(_mixers_seq_kernel, bb=bb, prec=prec),
        out_shape=(act, act, jax.ShapeDtypeStruct(s0.shape, F32), jax.ShapeDtypeStruct(cn0.shape, F32),
                   jax.ShapeDtypeStruct(m0.shape, F32)),
        grid=(batch // bb, seq_len // CHUNK),
        in_specs=in_specs,
        out_specs=(out_blk, out_blk, s_spec, cn_spec, m_spec),
        scratch_shapes=[pltpu.VMEM((bb, N_PAIRS, LANES, LANES), F32), pltpu.VMEM((bb, 8, CONV_CH), F32),
                        pltpu.VMEM((bb, N_PAIRS, LANES, 2 * LANES), F32), pltpu.VMEM((bb, 8, LANES), F32)],
        compiler_params=pltpu.CompilerParams(dimension_semantics=("parallel", "arbitrary"),
                                             vmem_limit_bytes=VMEM_LIMIT),
        name="mixers_seq",
    )(*([z3] * 9), rows4, convp, prm["conv_w8"], s0, cn0, m0, prm["add_row"], prm["alog_row"],
      prm["rowadd"], prm["rowalog"], prm["dn_normw"], prm["ml_normw"])


def _lane_mixers_kernel(zq, zk, zv, zg, lq, lk, lv, lo, zs, cpq, cpk, cpv, cwq, cwk, cwv, bias, alog, dnw, mlw,
                        s0, c0, n0, m0, dn_o, ml_o, s_o, c_o, n_o, m_o, k_s, q_s, *, n_tok):
    h = pl.program_id(0)
    nb = s_o.shape[-1]
    zrow = lambda t, r: zs[t, pl.ds(r, 1), :]
    prow = lambda ref, r: ref[pl.ds(r, 1), :]
    bcast = lambda ref, dk: jnp.broadcast_to(ref[pl.ds(dk, 1), :], (HEAD_DIM, nb))
    zeros = jnp.zeros((HEAD_DIM, nb), F32)

    def conv(z_ref, cp_ref, cw_ref):
        e = [cp_ref[j] for j in range(CONV_W - 1)] + [z_ref[t] for t in range(n_tok)]
        outs = []
        for t in range(n_tok):
            acc = e[t] * cw_ref[0]
            for j in range(1, CONV_W):
                acc = acc + e[t + j] * cw_ref[j]
            outs.append(_silu(acc))
        return outs

    def l2n(x, scale):
        return x * (lax.rsqrt(jnp.sum(x * x, axis=0, keepdims=True) + EPS) * scale)

    def head_norm(x, w):
        return x * lax.rsqrt(jnp.mean(x * x, axis=0, keepdims=True) + EPS) * w

    qc, kc, vc = conv(zq, cpq, cwq), conv(zk, cpk, cwk), conv(zv, cpv, cwv)
    s_o[...] = s0[...]
    for t in range(n_tok):
        k_s[...] = l2n(kc[t], 1.0)
        q_s[...] = l2n(qc[t], HEAD_DIM ** -0.5)
        beta = _sigmoid(zrow(t, h))
        a = jnp.exp(-jnp.exp(prow(alog, N_HEADS + h))
                    * _softplus(zrow(t, N_HEADS + h) + prow(bias, N_HEADS + h)))
        ks = lax.fori_loop(0, HEAD_DIM, lambda dk, acc: acc + bcast(k_s, dk) * s_o[dk], zeros, unroll=8)
        delta = beta * (vc[t] - a * ks)

        def dn_update(dk, acc, a=a, delta=delta):
            s_new = a * s_o[dk] + bcast(k_s, dk) * delta
            s_o[dk] = s_new
            return acc + bcast(q_s, dk) * s_new

        o = lax.fori_loop(0, HEAD_DIM, dn_update, zeros, unroll=8)
        dn_o[t] = (head_norm(o, dnw[...]) * _silu(zg[t])).astype(BF16)

    c_o[...] = c0[...]
    n = n0[...]
    m = m0[...]
    for t in range(n_tok):
        q, k, v = lq[t], lk[t] * (HEAD_DIM ** -0.5), lv[t]
        ig = zrow(t, 2 * N_HEADS + h) + prow(bias, 2 * N_HEADS + h)
        lsf = -_softplus(-(zrow(t, 3 * N_HEADS + h) + prow(bias, 3 * N_HEADS + h)))
        m_new = jnp.maximum(lsf + m, ig)
        fp = jnp.exp(lsf + m - m_new)
        ip = jnp.exp(ig - m_new)
        k_s[...] = k * ip
        q_s[...] = q

        def ml_update(dk, acc, fp=fp, v=v):
            c_new = fp * c_o[dk] + bcast(k_s, dk) * v
            c_o[dk] = c_new
            return acc + bcast(q_s, dk) * c_new

        num = lax.fori_loop(0, HEAD_DIM, ml_update, zeros, unroll=8)
        n = fp * n + ip * k
        den = jnp.sum(q * n, axis=0, keepdims=True)
        hh = num / jnp.maximum(jnp.abs(den), jnp.exp(-m_new))
        m = m_new
        ml_o[t] = (head_norm(hh, mlw[...]) * _sigmoid(lo[t])).astype(BF16)
    n_o[...] = n
    m_o[...] = m


def _lane_mixers(z_t, conv_t, s0, c0, n0, m0, layer, prm):
    n_tok, _, nb = z_t.shape
    hd = HEAD_DIM
    zblk = lambda first: pl.BlockSpec((n_tok, hd, nb), lambda h: (0, first + h, 0))
    cblk = lambda taps, seg: pl.BlockSpec((taps, hd, nb), lambda h: (0, seg * N_HEADS + h, 0))
    wblk = lambda seg: pl.BlockSpec((CONV_W, hd, nb), lambda h: (0, seg * N_HEADS + h, 0))
    whole = lambda shape: pl.BlockSpec(shape, lambda h: (0,) * len(shape))
    mat_in = pl.BlockSpec((None, None, hd, hd, nb), lambda h: (layer, h, 0, 0, 0))
    mat_out = pl.BlockSpec((None, hd, hd, nb), lambda h: (h, 0, 0, 0))
    small_rows = 32
    in_specs = [zblk(seg * N_HEADS) for seg in range(8)] + [
        pl.BlockSpec((n_tok, small_rows, nb), lambda h: (0, (Z_COLS - LANES) // small_rows, 0)),
        cblk(CONV_W - 1, 0), cblk(CONV_W - 1, 1), cblk(CONV_W - 1, 2), wblk(0), wblk(1), wblk(2),
        whole((small_rows, nb)), whole((small_rows, nb)), whole((hd, nb)), whole((hd, nb)),
        mat_in, mat_in,
        pl.BlockSpec((None, None, hd, nb), lambda h: (layer, h, 0, 0)),
        pl.BlockSpec((None, None, 1, nb), lambda h: (layer, h, 0, 0))]
    out_blk = pl.BlockSpec((n_tok, hd, nb), lambda h: (0, h, 0))
    act = jax.ShapeDtypeStruct((n_tok, HEADS_W, nb), BF16)
    mat = jax.ShapeDtypeStruct((N_HEADS, hd, hd, nb), F32)
    return pl.pallas_call(
        functools.partial(_lane_mixers_kernel, n_tok=n_tok),
        out_shape=(act, act, mat, mat, jax.ShapeDtypeStruct((N_HEADS, hd, nb), F32),
                   jax.ShapeDtypeStruct((N_HEADS, 1, nb), F32)),
        grid=(N_HEADS,),
        in_specs=in_specs,
        out_specs=(out_blk, out_blk, mat_out, mat_out,
                   pl.BlockSpec((None, hd, nb), lambda h: (h, 0, 0)), pl.BlockSpec((None, 1, nb), lambda h: (h, 0, 0))),
        scratch_shapes=[pltpu.VMEM((hd, nb), F32), pltpu.VMEM((hd, nb), F32)],
        compiler_params=pltpu.CompilerParams(dimension_semantics=("parallel",), vmem_limit_bytes=VMEM_LIMIT),
        name="lane_mixers",
    )(*([z_t] * 9), conv_t, conv_t, conv_t, prm["conv_wb"], prm["conv_wb"], prm["conv_wb"], prm["bias_b"],
      prm["alog_b"], prm["dn_normb"], prm["ml_normb"], s0, c0, n0, m0)


def _pack_w_in(w_in):
    offs = np.concatenate([[0], np.cumsum(IN_SIZES)])
    seg = lambda i: w_in[..., int(offs[i]):int(offs[i + 1])].astype(BF16)
    small = jnp.concatenate([seg(3), seg(4), seg(9), seg(10)], axis=-1)
    small = jnp.pad(small, ((0, 0),) * (w_in.ndim - 1) + ((0, LANES - 4 * N_HEADS),))
    cols = [seg(0), seg(1), seg(2), seg(5), seg(6), seg(7), seg(8), seg(11), seg(12), small]
    return jnp.concatenate(cols, axis=-1)


def _pair_rows(v6):
    return jnp.repeat(v6.reshape(N_PAIRS, 2), HEAD_DIM, axis=1).reshape(N_PAIRS, LANES)


def _layer_params(l, a):
    f = lambda name: a[name][l].astype(F32)
    zeros6 = jnp.zeros((N_HEADS,), F32)
    add_row = jnp.concatenate([zeros6, f("dn_dt_bias"), f("ml_i_bias"), f("ml_f_bias")])
    add_row = jnp.pad(add_row, (0, LANES - 4 * N_HEADS))[None, :]
    alog_row = jnp.pad(jnp.concatenate([zeros6, f("dn_a_log")]), (0, LANES - 2 * N_HEADS))[None, :]
    rowadd = jnp.concatenate([_pair_rows(f("dn_dt_bias")), _pair_rows(f("ml_i_bias")),
                              _pair_rows(f("ml_f_bias")), jnp.zeros((7, LANES), F32)], axis=0)
    rowalog = jnp.concatenate([_pair_rows(f("dn_a_log")), jnp.zeros((13, LANES), F32)], axis=0)
    eye_g = jnp.eye(S5_GROUPS, dtype=F32)
    bd_in = lambda b: jnp.einsum("gpc,gh->gchp", b, eye_g).reshape(S5_WIDTH, S5_STATE)
    bd_out = lambda c: jnp.einsum("gcp,gh->gphc", c, eye_g).reshape(S5_STATE, S5_WIDTH).astype(BF16)
    return dict(
        layer=l, norm_mix=f("norm_mix")[None, :],
        add_row=add_row, alog_row=alog_row, rowadd=rowadd, rowalog=rowalog,
        conv_w8=jnp.pad(f("dn_conv_w"), ((0, 8 - CONV_W), (0, 0))),
        dn_normw=jnp.tile(f("dn_norm"), 2)[None, :], ml_normw=jnp.tile(f("ml_norm"), 2)[None, :],
        s5=dict(lre=f("s5_lam_re").reshape(1, S5_STATE), lim=f("s5_lam_im").reshape(1, S5_STATE),
                ldt=jnp.repeat(f("s5_log_dt"), S5_P)[None, :],
                bre=bd_in(f("s5_b_re")), bim=bd_in(f("s5_b_im")),
                cre=bd_out(f("s5_c_re")), cim=bd_out(f("s5_c_im")),
                d=f("s5_d").reshape(1, S5_WIDTH), wg=f("s5_w_glu").astype(BF16),
                bg=f("s5_b_glu")[None, :]),
        norm_x=f("norm_x")[None, :], norm_ff=f("norm_ff")[None, :],
    )


def _shared_weights(a):
    cast = lambda name: a[name].astype(BF16)
    return dict(w_in=_pack_w_in(a["w_in"]), w_out=cast("w_out"), w_xq=cast("w_xq"), w_xo=cast("w_xo"),
                w_ff1=cast("w_ff1"), w_ff2=cast("w_ff2"), w_xk=cast("w_xk"), w_xv=cast("w_xv"))


def _row_form(z, n_chunks):
    zs = z[..., Z_COLS - LANES:Z_COLS - LANES + 4 * N_HEADS].reshape(n_chunks, CHUNK, 4, N_PAIRS, 2)
    r = jnp.transpose(zs[:, :, 1:], (0, 2, 3, 4, 1)).reshape(n_chunks, 9, LANES)
    return jnp.pad(r, ((0, 0), (0, 7), (0, 0)))


def _un_bd_pairs(s):
    n = s.shape[0]
    a = s[:, :, :HEAD_DIM, :HEAD_DIM]
    b = s[:, :, HEAD_DIM:, HEAD_DIM:]
    return jnp.stack([a, b], axis=2).reshape(n, N_HEADS, HEAD_DIM, HEAD_DIM)


def _unpack_m(mp):
    return mp[:, :N_PAIRS, ::HEAD_DIM].reshape(mp.shape[0], N_HEADS)


def _tile(n, pref):
    tile = min(n, pref)
    assert n % tile == 0, (n, pref)
    return tile


def _layer_prompt(x, mem_k, mem_v, prm, wts, final_g):
    batch, seq_len, _ = x.shape
    layer = prm["layer"]
    z3 = _norm_matmul(x, prm["norm_mix"], wts["w_in"], layer, _tile(seq_len, ROW_TILE))
    rows4 = _row_form(z3, batch * seq_len // CHUNK).reshape(batch, seq_len // CHUNK, 16, LANES)
    new_conv = z3[:, seq_len - 3:, :CONV_CH]

    dn, ml, s_new, cn_new, m_new = _mixers_seq(
        z3, rows4, jnp.zeros((batch, 8, CONV_CH), F32), jnp.zeros((batch, N_PAIRS, LANES, LANES), F32),
        jnp.zeros((batch, N_PAIRS, LANES, 2 * LANES), F32), jnp.zeros((batch, 8, LANES), F32), prm,
        _tile(batch, SEQ_BATCH), SEQ_SOLVE)
    zero_h = jnp.zeros((batch, S5_STATE), F32)
    s5o, hr, hi = _s5(z3, zero_h, zero_h, prm["s5"], _tile(seq_len, S5_STEPS), S5_STATE, True)

    g_fin = final_g if final_g is not None else prm["norm_ff"]
    x3 = _post_mix(dn, ml, s5o, x, mem_k, mem_v, wts, prm, g_fin, layer, _tile(seq_len, POST_TILE),
                   final_g is not None)
    st = (new_conv, _un_bd_pairs(s_new), _un_bd_pairs(cn_new[..., :LANES]),
          _un_bd_pairs(cn_new[..., LANES:])[..., 0], _unpack_m(m_new),
          hr.reshape(batch, S5_GROUPS, S5_P), hi.reshape(batch, S5_GROUPS, S5_P))
    return x3, st


def _layer_sample(x, cache_k, cache_v, conv_buf, dn_s, ml_c, ml_n, ml_m, ssm_re, ssm_im, prm, wts, batch,
                  seq_len, final_g):
    t = batch * seq_len
    layer = prm["layer"]
    z = _norm_matmul(x, prm["norm_mix"], wts["w_in"], layer, t)
    z3 = z.reshape(batch, seq_len, Z_COLS)
    new_conv = jnp.concatenate([conv_buf, z3[:, :, :CONV_CH]], axis=1)[:, seq_len:]
    lanes = lambda v: jnp.broadcast_to(v[..., None], v.shape + (batch,))
    mix_prm = dict(conv_wb=lanes(prm["conv_w8"][:CONV_W]), bias_b=lanes(prm["add_row"][0, :32]),
                   alog_b=lanes(prm["alog_row"][0, :32]), dn_normb=lanes(prm["dn_normw"][0, :HEAD_DIM]),
                   ml_normb=lanes(prm["ml_normw"][0, :HEAD_DIM]))
    dn_t, ml_t, s_new, c_new, n_new, m_new = _lane_mixers(
        jnp.transpose(z3, (1, 2, 0)), jnp.transpose(conv_buf, (1, 2, 0)), dn_s, ml_c, ml_n, ml_m, layer, mix_prm)
    to_rows = lambda a: jnp.transpose(a, (2, 0, 1)).reshape(1, t, HEADS_W)
    dn, ml = to_rows(dn_t), to_rows(ml_t)

    u_tm = jnp.transpose(z3[:, :, 3072:3072 + S5_WIDTH], (1, 0, 2))
    y_tm, hr, hi = _s5(u_tm.reshape(t, S5_WIDTH), ssm_re.reshape(batch, S5_STATE),
                       ssm_im.reshape(batch, S5_STATE), prm["s5"], seq_len, LANES, False)
    s5o = jnp.transpose(y_tm.reshape(seq_len, batch, S5_WIDTH), (1, 0, 2)).reshape(t, S5_WIDTH)

    x1, q = _out_proj(dn, ml, s5o, x, wts, prm, layer, t)
    q3 = jnp.pad(q.reshape(batch, seq_len, X_WIDTH), ((0, 0), (0, Q_ROWS - seq_len), (0, 0)))
    att = _attn_batch(q3, cache_k, cache_v, layer, _tile(batch, ATTN_BATCH))[:, :seq_len].reshape(1, t, X_WIDTH)
    g_fin = final_g if final_g is not None else prm["norm_ff"]
    x3 = _ffn(x1, att, wts, prm, g_fin, layer, t, final_g is not None)
    st = (new_conv, jnp.transpose(s_new, (3, 0, 1, 2)), jnp.transpose(c_new, (3, 0, 1, 2)),
          jnp.transpose(n_new, (2, 0, 1)), jnp.transpose(m_new[:, 0, :], (1, 0)),
          hr.reshape(batch, S5_GROUPS, S5_P), hi.reshape(batch, S5_GROUPS, S5_P))
    return x3, st


def kernel(x_prompt, x_sample, mem_prompt, state_dn_conv, state_dn_s, state_ml_c, state_ml_n, state_ml_m, state_ssm_re, state_ssm_im, cache_mem_k, cache_mem_v, norm_mix, w_in, dn_conv_w, dn_a_log, dn_dt_bias, dn_norm, ml_i_bias, ml_f_bias, ml_norm, s5_lam_re, s5_lam_im, s5_log_dt, s5_b_re, s5_b_im, s5_c_re, s5_c_im, s5_d, s5_w_glu, s5_b_glu, w_out, norm_x, norm_mem, w_xq, w_xk, w_xv, w_xo, norm_ff, w_ff1, w_ff2, norm_final):
    a = dict(norm_mix=norm_mix, w_in=w_in, dn_conv_w=dn_conv_w, dn_a_log=dn_a_log, dn_dt_bias=dn_dt_bias,
             dn_norm=dn_norm, ml_i_bias=ml_i_bias, ml_f_bias=ml_f_bias, ml_norm=ml_norm,
             s5_lam_re=s5_lam_re, s5_lam_im=s5_lam_im, s5_log_dt=s5_log_dt, s5_b_re=s5_b_re,
             s5_b_im=s5_b_im, s5_c_re=s5_c_re, s5_c_im=s5_c_im, s5_d=s5_d, s5_w_glu=s5_w_glu,
             s5_b_glu=s5_b_glu, w_out=w_out, norm_x=norm_x, w_xq=w_xq, w_xo=w_xo, norm_ff=norm_ff,
             w_ff1=w_ff1, w_ff2=w_ff2)
    prms = [_layer_params(l, a) for l in range(DEPTH)]
    wts = _shared_weights(dict(a, w_xk=w_xk, w_xv=w_xv))
    g_final = norm_final.astype(F32)[None, :]

    bp, lp, _ = x_prompt.shape
    bs, ls, _ = x_sample.shape

    xp = x_prompt
    mk, mk_t, mv, mv_t = _mem_proj(mem_prompt, norm_mem.astype(F32), wts["w_xk"], wts["w_xv"])
    p_st = []
    for l in range(DEPTH):
        xp, st = _layer_prompt(xp, mk, mv, prms[l], wts, g_final if l == DEPTH - 1 else None)
        p_st.append(st)
    y_prompt = xp
    p_mem = (mk_t.reshape(DEPTH, bp, N_MEM, X_HEADS, X_HD), mv_t.reshape(DEPTH, bp, N_MEM, X_HEADS, X_HD))

    xs = x_sample.reshape(1, bs * ls, D_MODEL)
    cache_k = cache_mem_k.reshape(DEPTH, bs, N_MEM * X_HEADS, X_HD)
    cache_v = cache_mem_v.reshape(DEPTH, bs, N_MEM * X_HEADS, X_HD)
    dn_s_t = jnp.transpose(state_dn_s, (0, 2, 3, 4, 1))
    ml_c_t = jnp.transpose(state_ml_c, (0, 2, 3, 4, 1))
    ml_n_t = jnp.transpose(state_ml_n, (0, 2, 3, 1))
    ml_m_t = jnp.transpose(state_ml_m, (0, 2, 1))[:, :, None, :]
    s_st = []
    for l in range(DEPTH):
        xs, st = _layer_sample(xs, cache_k, cache_v, state_dn_conv[l], dn_s_t, ml_c_t, ml_n_t, ml_m_t,
                               state_ssm_re[l], state_ssm_im[l], prms[l], wts, bs, ls,
                               g_final if l == DEPTH - 1 else None)
        s_st.append(st)
    y_sample = xs.reshape(bs, ls, D_MODEL)

    stack = lambda sts, i: jnp.stack([s[i] for s in sts])
    return ((y_prompt, y_sample) + tuple(stack(p_st, i) for i in range(7)) + p_mem
            + tuple(stack(s_st, i) for i in range(7)))
```

```python
import functools
import math

import numpy as np
import jax
import jax.numpy as jnp
from jax import lax
from jax.experimental import pallas as pl
from jax.experimental.pallas import tpu as pltpu

F32 = jnp.float32
BF16 = jnp.bfloat16

D_MODEL = 1024
DEPTH = 2
N_HEADS = 6
HEAD_DIM = 64
N_PAIRS = N_HEADS // 2
HEADS_W = N_HEADS * HEAD_DIM
CONV_W = 4
CONV_CH = 3 * HEADS_W
S5_WIDTH = 256
S5_GROUPS = 16
S5_P = 64
S5_STATE = S5_GROUPS * S5_P
N_MEM = 256
X_HEADS = 4
X_HD = 128
X_WIDTH = X_HEADS * X_HD
D_FF = 4 * D_MODEL
EPS = 1e-6
IN_SIZES = (384, 384, 384, 6, 6, 384, 384, 384, 384, 6, 6, 384, 256)

LANES = 128
CHUNK = 64
Q_ROWS = 8
NEG = -1e30
Z_COLS = 3456
S5_COL_BLOCK = 3072 // S5_WIDTH
GATE_COL_BLOCK = (Z_COLS - LANES) // LANES
VMEM_LIMIT = 52 * 1024 * 1024

ROW_TILE = 1024
CONV_ROW_TILE = 512
POST_TILE = 512
SEQ_BATCH = 8
S5_STEPS = 64
ATTN_BATCH = 8
FF_CHUNK = 1024
TRI_BASE = 8
SEQ_SOLVE = (1, True, 1)

_NN = (((1,), (0,)), ((), ()))
_NT = (((1,), (1,)), ((), ()))


def _dot(a, b, dims=_NN):
    return lax.dot_general(a, b, dims, preferred_element_type=F32)


def _split(a, n):
    out = []
    r = a
    for i in range(n):
        h = r.astype(BF16)
        out.append(h)
        if i + 1 < n:
            r = r - h.astype(F32)
    return out


def _mm(a, b, dims=_NN, prec=1):
    if prec == 1:
        return _dot(a.astype(BF16), b.astype(BF16), dims)
    return _mmp(_parts(a, prec), _parts(b, prec), dims)


def _parts(a, prec):
    return tuple(_split(a, 1 if prec == 1 else 2))


def _mmp(ap, bp, dims=_NN):
    acc = None
    if len(bp) > 1:
        acc = _dot(ap[0], bp[1], dims)
    if len(ap) > 1:
        cross = _dot(ap[1], bp[0], dims)
        acc = cross if acc is None else acc + cross
    lead = _dot(ap[0], bp[0], dims)
    return lead if acc is None else acc + lead


def _mm_sel_r(a, sel, n=3):
    parts = _split(a, n)
    acc = _dot(parts[-1], sel)
    for p_ in parts[-2::-1]:
        acc = acc + _dot(p_, sel)
    return acc


def _mm_sel_l(sel, a, n=3):
    parts = _split(a, n)
    acc = _dot(sel, parts[-1])
    for p_ in parts[-2::-1]:
        acc = acc + _dot(sel, p_)
    return acc


def _sigmoid(x):
    return 1.0 / (1.0 + jnp.exp(-x))


def _silu(x):
    return x * _sigmoid(x)


def _softplus(x):
    return jnp.maximum(x, 0.0) + jnp.log(1.0 + jnp.exp(-jnp.abs(x)))


def _rmsnorm_rows(x, g):
    return x * lax.rsqrt(jnp.mean(x * x, axis=-1, keepdims=True) + EPS) * g


def _iota(shape, dim):
    return lax.broadcasted_iota(jnp.int32, shape, dim)


def _idiv(x, n):
    return x >> (n.bit_length() - 1)


def _imod(x, n):
    return x & (n - 1)


def _lane_lo(shape):
    return _iota(shape, 1) < HEAD_DIM


def _x2(x):
    lo = _lane_lo(x.shape)
    return jnp.concatenate([jnp.where(lo, x, 0.0), jnp.where(lo, 0.0, x)], axis=0)


def _fold(y):
    r = y.shape[0] // 2
    return y[:r] + y[r:]


def _expand(a, c0, c1):
    r = a.shape[0]
    return jnp.where(_lane_lo((r, LANES)), a[:, c0:c0 + 1], a[:, c1:c1 + 1])


def _bd_mask():
    return (_iota((LANES, LANES), 0) < HEAD_DIM) == (_iota((LANES, LANES), 1) < HEAD_DIM)


def _block_ones():
    return jnp.where(_bd_mask(), 1.0, 0.0).astype(BF16)


def _seg_mean(x, bones):
    return _mm_sel_r(x, bones, n=1) * (1.0 / HEAD_DIM)


def _ltri():
    return jnp.where(_iota((CHUNK, CHUNK), 1) <= _iota((CHUNK, CHUNK), 0), 1.0, 0.0).astype(BF16)


def _upair():
    a = _iota((LANES, LANES), 0)
    b = _iota((LANES, LANES), 1)
    return jnp.where(_bd_mask() & (a <= b), 1.0, 0.0).astype(BF16)


def _pair_masks():
    i = _iota((CHUNK, LANES), 0)
    j = _imod(_iota((CHUNK, LANES), 1), HEAD_DIM)
    return j <= i, j < i


def _blk_eq(n, b):
    return _idiv(_iota((n, n), 0), b) == _idiv(_iota((n, n), 1), b)


def _last_row(a):
    return jnp.broadcast_to(a[CHUNK - 1:CHUNK, :], a.shape)


def _tri_inverse_units(mbds, prec, newton):
    n = LANES
    eye = jnp.where(_iota((n, n), 0) == _iota((n, n), 1), 1.0, 0.0)
    n_sq = TRI_BASE.bit_length() - 2
    blk = _blk_eq(n, TRI_BASE)
    m0 = [jnp.where(blk, m, 0.0) for m in mbds]
    ts = [eye - m for m in m0]
    ps = [_mm(m, m, prec=prec) for m in m0]
    for it in range(n_sq):
        if it == n_sq - 1:
            ts = [t + _mm(t, p_, prec=prec) for t, p_ in zip(ts, ps)]
        else:
            both = [_mm(jnp.concatenate([t, p_], axis=0), p_, prec=prec) for t, p_ in zip(ts, ps)]
            ts = [t + b_[:n] for t, b_ in zip(ts, both)]
            ps = [b_[n:] for b_ in both]
    b = TRI_BASE
    while b < CHUNK:
        outer = _blk_eq(n, 2 * b) & jnp.logical_not(_blk_eq(n, b))
        tp = [_parts(t, prec) for t in ts]
        xs = [_mmp(_parts(jnp.where(outer, m, 0.0), prec), t_) for m, t_ in zip(mbds, tp)]
        ts = [t - _mmp(t_, _parts(x, prec)) for t, t_, x in zip(ts, tp, xs)]
        b *= 2
    if newton:
        res = [eye - (t + _mm(m, t, prec=3)) for m, t in zip(mbds, ts)]
        ts = [t + _mm(t, r) for t, r in zip(ts, res)]
    return ts


def _wspec(rows, cols, layer, row_block=0):
    return pl.BlockSpec((None, rows, cols), lambda i: (layer, row_block, 0), pipeline_mode=pl.Buffered(1))


def _rows3(nt):
    return lambda i: (i // nt, i % nt, 0)


def _dense_call(kernel_fn, name, out_shape, in_specs, out_specs, steps, args):
    return pl.pallas_call(
        kernel_fn, out_shape=out_shape, grid=(steps,), in_specs=in_specs, out_specs=out_specs,
        compiler_params=pltpu.CompilerParams(dimension_semantics=("parallel",), vmem_limit_bytes=VMEM_LIMIT),
        name=name)(*args)


def _in_proj_kernel(x_ref, g_ref, w_ref, o_ref):
    o_ref[0] = _dot(_rmsnorm_rows(x_ref[0], g_ref[...]).astype(BF16), w_ref[...])


def _norm_matmul(x, g, w, layer, tm):
    b, l, k = x.shape
    n = w.shape[2]
    nt = l // tm
    fix = lambda i: (0, 0)
    return _dense_call(
        _in_proj_kernel, "norm_matmul", jax.ShapeDtypeStruct((b, l, n), F32),
        [pl.BlockSpec((1, tm, k), _rows3(nt)), pl.BlockSpec((1, k), fix), _wspec(k, n, layer)],
        pl.BlockSpec((1, tm, n), _rows3(nt)), b * nt, (x, g, w))


def _conv_silu(u, w8, fix):
    out = fix(pltpu.roll(u, 3, axis=0), 3) * w8[0:1, :]
    for s in (2, 1):
        out = out + fix(pltpu.roll(u, s, axis=0), s) * w8[3 - s:4 - s, :]
    out = out + u * w8[3:4, :]
    return _silu(out)


def _in_proj_conv_kernel(x_ref, g_ref, w_ref, cp_ref, cw_ref, o_ref, c_ref, u_scr, tail_scr, *, nt, n_tiles):
    i = pl.program_id(0)
    tm = u_scr.shape[0]

    @pl.when(i == 0)
    def _():
        u_scr[...] = jnp.zeros_like(u_scr)
        tail_scr[...] = jnp.zeros_like(tail_scr)

    prev = tail_scr[...]
    u_prev = u_scr[...]
    r8 = _iota((8, CONV_CH), 0)

    def fix(rolled, s):
        first = jnp.where(r8 < s, pltpu.roll(prev, s, axis=0), rolled[0:8])
        return jnp.concatenate([first, rolled[8:]], axis=0)

    c_ref[0] = _conv_silu(u_prev, cw_ref[...], fix)

    z = _dot(_rmsnorm_rows(x_ref[0], g_ref[...]).astype(BF16), w_ref[...])
    o_ref[0] = z
    first_of_seq = jnp.minimum(i, n_tiles - 1) % nt == 0
    tail_scr[...] = jnp.where(first_of_seq, cp_ref[0], u_prev[tm - 8:])
    u_scr[...] = z[:, :CONV_CH]


def _norm_matmul_conv(x, g, w, convp, cw8, layer, tm):
    b, l, k = x.shape
    n = w.shape[2]
    nt = l // tm
    n_tiles = b * nt
    fix = lambda i: (0, 0)
    cur = lambda i: jnp.minimum(i, n_tiles - 1)
    tile = lambda i: (cur(i) // nt, cur(i) % nt, 0)
    prev_tile = lambda i: (jnp.maximum(i - 1, 0) // nt, jnp.maximum(i - 1, 0) % nt, 0)
    return pl.pallas_call(
        functools.partial(_in_proj_conv_kernel, nt=nt, n_tiles=n_tiles),
        out_shape=(jax.ShapeDtypeStruct((b, l, n), F32), jax.ShapeDtypeStruct((b, l, CONV_CH), F32)),
        grid=(n_tiles + 1,),
        in_specs=[pl.BlockSpec((1, tm, k), tile), pl.BlockSpec((1, k), fix), _wspec(k, n, layer),
                  pl.BlockSpec((1, 8, CONV_CH), lambda i: (cur(i) // nt, 0, 0)), pl.BlockSpec((8, CONV_CH), fix)],
        out_specs=(pl.BlockSpec((1, tm, n), tile), pl.BlockSpec((1, tm, CONV_CH), prev_tile)),
        scratch_shapes=[pltpu.VMEM((tm, CONV_CH), F32), pltpu.VMEM((8, CONV_CH), F32)],
        compiler_params=pltpu.CompilerParams(dimension_semantics=("arbitrary",), vmem_limit_bytes=VMEM_LIMIT),
        name="norm_matmul_conv",
    )(x, g, w, convp, cw8)


def _mem_proj_kernel(x_ref, g_ref, wk_ref, wv_ref, k_ref, kt_ref, v_ref, vt_ref):
    x = x_ref[0]
    xhat = x * lax.rsqrt(jnp.mean(x * x, axis=-1, keepdims=True) + EPS)
    for l in range(DEPTH):
        xn = (xhat * g_ref[l:l + 1, :]).astype(BF16)
        for w_ref, o_ref, t_ref in ((wk_ref, k_ref, kt_ref), (wv_ref, v_ref, vt_ref)):
            res = _dot(xn, w_ref[l])
            o_ref[l, 0] = res
            for h in range(X_HEADS):
                t_ref[l, 0, pl.ds(h, N_MEM, stride=X_HEADS), :] = res[:, h * X_HD:(h + 1) * X_HD]


def _mem_proj(mem, g, wk, wv):
    b = mem.shape[0]
    whole = lambda shape: pl.BlockSpec(shape, lambda i: (0,) * len(shape))
    o_spec = pl.BlockSpec((DEPTH, 1, N_MEM, X_WIDTH), lambda i: (0, i, 0, 0))
    t_spec = pl.BlockSpec((DEPTH, 1, N_MEM * X_HEADS, X_HD), lambda i: (0, i, 0, 0))
    o_shape = jax.ShapeDtypeStruct((DEPTH, b, N_MEM, X_WIDTH), F32)
    t_shape = jax.ShapeDtypeStruct((DEPTH, b, N_MEM * X_HEADS, X_HD), F32)
    return _dense_call(
        _mem_proj_kernel, "mem_proj", (o_shape, t_shape, o_shape, t_shape),
        [pl.BlockSpec((1, N_MEM, D_MODEL), lambda i: (i, 0, 0)), whole((DEPTH, D_MODEL)),
         whole((DEPTH, D_MODEL, X_WIDTH)), whole((DEPTH, D_MODEL, X_WIDTH))],
        (o_spec, t_spec, o_spec, t_spec), b, (mem, g, wk, wv))


def _mix_residual(x, dn, ml, s5, w1_ref, w2_ref, w3_ref):
    acc = x + _dot(dn, w1_ref[...])
    acc = acc + _dot(ml, w2_ref[...])
    return acc + _dot(s5, w3_ref[...])


def _mlp_residual(x2, g_ref, f1_ref, f2_ref):
    xn = _rmsnorm_rows(x2, g_ref[...]).astype(BF16)
    acc = x2
    for c in range(D_FF // FF_CHUNK):
        hf = _dot(xn, f1_ref[:, c * FF_CHUNK:(c + 1) * FF_CHUNK])
        a = jnp.square(jnp.maximum(hf, 0.0)).astype(BF16)
        acc = acc + _dot(a, f2_ref[c * FF_CHUNK:(c + 1) * FF_CHUNK, :])
    return acc


def _out_proj_kernel(dn_ref, ml_ref, s5_ref, x_ref, w1_ref, w2_ref, w3_ref, g_ref, wq_ref, x1_ref, q_ref):
    acc = _mix_residual(x_ref[0], dn_ref[0], ml_ref[0], s5_ref[...], w1_ref, w2_ref, w3_ref)
    x1_ref[0] = acc
    q_ref[0] = _dot(_rmsnorm_rows(acc, g_ref[...]).astype(BF16), wq_ref[...])


def _w_out_specs(layer):
    return [_wspec(HEADS_W, D_MODEL, layer, 0), _wspec(HEADS_W, D_MODEL, layer, 1),
            _wspec(S5_WIDTH, D_MODEL, layer, 2 * HEADS_W // S5_WIDTH)]


def _out_proj(dn, ml, s5, x, wts, prm, layer, tm):
    b, l, _ = x.shape
    nt = l // tm
    row = _rows3(nt)
    vec = pl.BlockSpec((1, D_MODEL), lambda i: (0, 0))
    return _dense_call(
        _out_proj_kernel, "out_proj",
        (jax.ShapeDtypeStruct((b, l, D_MODEL), F32), jax.ShapeDtypeStruct((b, l, X_WIDTH), F32)),
        [pl.BlockSpec((1, tm, HEADS_W), row), pl.BlockSpec((1, tm, HEADS_W), row),
         pl.BlockSpec((tm, S5_WIDTH), lambda i: (i % nt, i // nt)), pl.BlockSpec((1, tm, D_MODEL), row)]
        + _w_out_specs(layer) + [vec, _wspec(D_MODEL, X_WIDTH, layer)],
        (pl.BlockSpec((1, tm, D_MODEL), row), pl.BlockSpec((1, tm, X_WIDTH), row)), b * nt,
        (dn, ml, s5, x, wts["w_out"], wts["w_out"], wts["w_out"], prm["norm_x"], wts["w_xq"]))


def _ffn_kernel(x_ref, o_ref, wo_ref, g_ref, f1_ref, f2_ref, gl_ref, y_ref, *, final_norm):
    acc = _mlp_residual(x_ref[0] + _dot(o_ref[0], wo_ref[...]), g_ref, f1_ref, f2_ref)
    y_ref[0] = _rmsnorm_rows(acc, gl_ref[...]) if final_norm else acc


def _ffn(x, att, wts, prm, g_last, layer, tm, final_norm):
    b, l, _ = x.shape
    nt = l // tm
    row = _rows3(nt)
    vec = pl.BlockSpec((1, D_MODEL), lambda i: (0, 0))
    return _dense_call(
        functools.partial(_ffn_kernel, final_norm=final_norm), "ffn", jax.ShapeDtypeStruct((b, l, D_MODEL), F32),
        [pl.BlockSpec((1, tm, D_MODEL), row), pl.BlockSpec((1, tm, X_WIDTH), row), _wspec(X_WIDTH, D_MODEL, layer),
         vec, _wspec(D_MODEL, D_FF, layer), _wspec(D_FF, D_MODEL, layer), vec],
        pl.BlockSpec((1, tm, D_MODEL), row), b * nt,
        (x, att, wts["w_xo"], prm["norm_ff"], wts["w_ff1"], wts["w_ff2"], g_last))


def _attn_heads(q, k_ref, v_ref):
    outs = []
    for h in range(X_HEADS):
        sl = slice(h * X_HD, (h + 1) * X_HD)
        s = _dot(q[:, sl], k_ref[0, :, sl].astype(BF16), _NT) * (X_HD ** -0.5)
        e = jnp.exp(s - jnp.max(s, axis=-1, keepdims=True))
        p = e / jnp.sum(e, axis=-1, keepdims=True)
        outs.append(_dot(p.astype(BF16), v_ref[0, :, sl].astype(BF16)))
    return jnp.concatenate(outs, axis=-1)


def _attn_batch_kernel(q_ref, k_ref, v_ref, o_ref, *, bb):
    rows = X_HEADS * Q_ROWS
    col_head = _imod(_iota((rows, N_MEM * X_HEADS), 1), X_HEADS)
    row_head = _idiv(_iota((rows, N_MEM * X_HEADS), 0), Q_ROWS)
    own = col_head == row_head
    q4 = [jnp.concatenate([q_ref[bi, :, h * X_HD:(h + 1) * X_HD] for h in range(X_HEADS)], axis=0).astype(BF16)
          for bi in range(bb)]
    s = [jnp.where(own, _dot(q_, k_ref[bi].astype(BF16), _NT) * (X_HD ** -0.5), NEG)
         for bi, q_ in enumerate(q4)]
    e = [jnp.exp(x - jnp.max(x, axis=-1, keepdims=True)) for x in s]
    p = [x / jnp.sum(x, axis=-1, keepdims=True) for x in e]
    o = [_dot(x.astype(BF16), v_ref[bi].astype(BF16)) for bi, x in enumerate(p)]
    for bi in range(bb):
        for h in range(X_HEADS):
            o_ref[bi, :, h * X_HD:(h + 1) * X_HD] = o[bi][h * Q_ROWS:(h + 1) * Q_ROWS].astype(BF16)


def _attn_batch(q, k, v, layer, bb):
    b, l, _ = q.shape
    blk = lambda i: (i, 0, 0)
    kv = pl.BlockSpec((None, bb, N_MEM * X_HEADS, X_HD), lambda i: (layer, i, 0, 0))
    return _dense_call(
        functools.partial(_attn_batch_kernel, bb=bb), "attn_batch", jax.ShapeDtypeStruct((b, l, X_WIDTH), BF16),
        [pl.BlockSpec((bb, l, X_WIDTH), blk), kv, kv], pl.BlockSpec((bb, l, X_WIDTH), blk), b // bb, (q, k, v))


def _post_mix_kernel(dn_ref, ml_ref, s5_ref, x_ref, k_ref, v_ref, w1_ref, w2_ref, w3_ref, gx_ref, wq_ref,
                     wo_ref, gf_ref, f1_ref, f2_ref, gl_ref, y_ref, *, final_norm):
    x1 = _mix_residual(x_ref[0], dn_ref[0], ml_ref[0], s5_ref[...], w1_ref, w2_ref, w3_ref)
    q = _dot(_rmsnorm_rows(x1, gx_ref[...]).astype(BF16), wq_ref[...]).astype(BF16)
    att = _attn_heads(q, k_ref, v_ref).astype(BF16)
    acc = _mlp_residual(x1 + _dot(att, wo_ref[...]), gf_ref, f1_ref, f2_ref)
    y_ref[0] = _rmsnorm_rows(acc, gl_ref[...]) if final_norm else acc


def _post_mix(dn, ml, s5, x, mem_k, mem_v, wts, prm, g_last, layer, tm, final_norm):
    b, l, _ = x.shape
    nt = l // tm
    row = _rows3(nt)
    mem = pl.BlockSpec((None, 1, N_MEM, X_WIDTH), lambda i: (layer, i // nt, 0, 0))
    vec = pl.BlockSpec((1, D_MODEL), lambda i: (0, 0))
    return _dense_call(
        functools.partial(_post_mix_kernel, final_norm=final_norm), "post_mix",
        jax.ShapeDtypeStruct((b, l, D_MODEL), F32),
        [pl.BlockSpec((1, tm, HEADS_W), row), pl.BlockSpec((1, tm, HEADS_W), row),
         pl.BlockSpec((tm, S5_WIDTH), lambda i: (i % nt, i // nt)), pl.BlockSpec((1, tm, D_MODEL), row), mem, mem]
        + _w_out_specs(layer)
        + [vec, _wspec(D_MODEL, X_WIDTH, layer), _wspec(X_WIDTH, D_MODEL, layer), vec,
           _wspec(D_MODEL, D_FF, layer), _wspec(D_FF, D_MODEL, layer), vec],
        pl.BlockSpec((1, tm, D_MODEL), row), b * nt,
        (dn, ml, s5, x, mem_k, mem_v, wts["w_out"], wts["w_out"], wts["w_out"], prm["norm_x"], wts["w_xq"],
         wts["w_xo"], prm["norm_ff"], wts["w_ff1"], wts["w_ff2"], g_last))


def _s5_kernel(u_ref, h0r_ref, h0i_ref, lre_ref, lim_ref, ldt_ref, bre_ref, bim_ref, cre_ref, cim_ref,
               d_ref, wg_ref, bg_ref, y_ref, hr_out, hi_out,
               xr_s, xi_s, st_r, st_i, coef_s, bbr_s, bbi_s, *io_s, tb, nb, lane_chunk, seq_major_io):
    step = pl.program_id(0)
    if seq_major_io:
        u_s, y_s = io_s
        for b in range(nb):
            for j in range(S5_WIDTH // LANES):
                u_s[j, pl.ds(b, tb, stride=nb), :] = u_ref[b, :, j * LANES:(j + 1) * LANES]
        u = jnp.concatenate([u_s[j] for j in range(S5_WIDTH // LANES)], axis=1)
    else:
        u = u_ref[...]

    @pl.when(step == 0)
    def _():
        lr = lre_ref[...]
        li = lim_ref[...]
        dt = jnp.exp(ldt_ref[...])
        mag = jnp.exp(lr * dt)
        ar = mag * jnp.cos(li * dt)
        ai = mag * jnp.sin(li * dt)
        inv = 1.0 / (lr * lr + li * li)
        coef_s[0:1, :] = ar
        coef_s[1:2, :] = ai
        cr = ((ar - 1.0) * lr + ai * li) * inv
        ci = (ai * lr - (ar - 1.0) * li) * inv
        bbr_s[...] = (cr * bre_ref[...] - ci * bim_ref[...]).astype(BF16)
        bbi_s[...] = (cr * bim_ref[...] + ci * bre_ref[...]).astype(BF16)
        st_r[...] = h0r_ref[...]
        st_i[...] = h0i_ref[...]

    ub = u.astype(BF16)
    xr_s[...] = _dot(ub, bbr_s[...])
    xi_s[...] = _dot(ub, bbi_s[...])

    for lc in range(S5_STATE // lane_chunk):
        sl = slice(lc * lane_chunk, (lc + 1) * lane_chunk)
        ar = jnp.broadcast_to(coef_s[0:1, sl], (nb, lane_chunk))
        ai = jnp.broadcast_to(coef_s[1:2, sl], (nb, lane_chunk))

        def body(t, carry, sl=sl, ar=ar, ai=ai):
            hr, hi = carry
            r0 = pl.multiple_of(t * nb, 8)
            nhr = ar * hr - ai * hi + xr_s[pl.ds(r0, nb), sl]
            nhi = ar * hi + ai * hr + xi_s[pl.ds(r0, nb), sl]
            xr_s[pl.ds(r0, nb), sl] = nhr
            xi_s[pl.ds(r0, nb), sl] = nhi
            return nhr, nhi

        hr, hi = lax.fori_loop(0, tb, body, (st_r[:, sl], st_i[:, sl]), unroll=min(tb, 8))
        st_r[:, sl] = hr
        st_i[:, sl] = hi

    y = (_dot(xr_s[...].astype(BF16), cre_ref[...]) - _dot(xi_s[...].astype(BF16), cim_ref[...])
         + d_ref[...] * u)
    zg = 0.5 * y * (1.0 + jnp.tanh(math.sqrt(2.0 / math.pi) * (y + 0.044715 * (y * y * y))))
    yv = zg * _sigmoid(_dot(zg.astype(BF16), wg_ref[...]) + bg_ref[...])
    if seq_major_io:
        for j in range(S5_WIDTH // LANES):
            y_s[j] = yv[:, j * LANES:(j + 1) * LANES]
        for b in range(nb):
            for j in range(S5_WIDTH // LANES):
                c0 = b * S5_WIDTH + j * LANES
                y_ref[:, c0:c0 + LANES] = y_s[j, pl.ds(b, tb, stride=nb), :].astype(BF16)
    else:
        y_ref[...] = yv.astype(BF16)

    @pl.when(step == pl.num_programs(0) - 1)
    def _():
        hr_out[...] = st_r[...]
        hi_out[...] = st_i[...]


def _s5(u, h0r, h0i, prm, tb, lane_chunk, seq_major_io):
    nb = h0r.shape[0]
    fix2 = lambda i: (0, 0)
    vec = pl.BlockSpec((1, S5_STATE), fix2)
    st = pl.BlockSpec((nb, S5_STATE), fix2)
    rows = tb * nb
    scratch = [pltpu.VMEM((rows, S5_STATE), F32), pltpu.VMEM((rows, S5_STATE), F32),
               pltpu.VMEM((nb, S5_STATE), F32), pltpu.VMEM((nb, S5_STATE), F32),
               pltpu.VMEM((8, S5_STATE), F32),
               pltpu.VMEM((S5_WIDTH, S5_STATE), BF16), pltpu.VMEM((S5_WIDTH, S5_STATE), BF16)]
    if seq_major_io:
        l = u.shape[1]
        u_spec = pl.BlockSpec((nb, tb, S5_WIDTH), lambda i: (0, i, S5_COL_BLOCK))
        y_shape = jax.ShapeDtypeStruct((l, nb * S5_WIDTH), BF16)
        y_spec = pl.BlockSpec((tb, nb * S5_WIDTH), lambda i: (i, 0))
        io_shape = (S5_WIDTH // LANES, rows, LANES)
        scratch += [pltpu.VMEM(io_shape, F32), pltpu.VMEM(io_shape, F32)]
    else:
        l = u.shape[0] // nb
        u_spec = pl.BlockSpec((rows, S5_WIDTH), lambda i: (i, 0))
        y_shape = jax.ShapeDtypeStruct((l * nb, S5_WIDTH), BF16)
        y_spec = pl.BlockSpec((rows, S5_WIDTH), lambda i: (i, 0))
    return pl.pallas_call(
        functools.partial(_s5_kernel, tb=tb, nb=nb, lane_chunk=lane_chunk, seq_major_io=seq_major_io),
        out_shape=(y_shape, jax.ShapeDtypeStruct((nb, S5_STATE), F32), jax.ShapeDtypeStruct((nb, S5_STATE), F32)),
        grid=(l // tb,),
        in_specs=[u_spec, st, st, vec, vec, vec,
                  pl.BlockSpec((S5_WIDTH, S5_STATE), fix2), pl.BlockSpec((S5_WIDTH, S5_STATE), fix2),
                  pl.BlockSpec((S5_STATE, S5_WIDTH), fix2), pl.BlockSpec((S5_STATE, S5_WIDTH), fix2),
                  pl.BlockSpec((1, S5_WIDTH), fix2), pl.BlockSpec((S5_WIDTH, S5_WIDTH), fix2),
                  pl.BlockSpec((1, S5_WIDTH), fix2)],
        out_specs=(y_spec, st, st),
        scratch_shapes=scratch,
        compiler_params=pltpu.CompilerParams(dimension_semantics=("arbitrary",),
                                             vmem_limit_bytes=VMEM_LIMIT),
        name="s5",
    )(u, h0r, h0i, prm["lre"], prm["lim"], prm["ldt"], prm["bre"], prm["bim"], prm["cre"],
      prm["cim"], prm["d"], prm["wg"], prm["bg"])


def _l2norm_pairs(x_chunks, bones, scale):
    n = len(x_chunks)
    x = jnp.concatenate(x_chunks, axis=0) if n > 1 else x_chunks[0]
    out = [[None] * N_PAIRS for _ in range(n)]
    for p_ in range(N_PAIRS):
        xp = x[:, p_ * LANES:(p_ + 1) * LANES]
        xn = xp * (lax.rsqrt(_seg_mean(xp * xp, bones) * HEAD_DIM + EPS) * scale)
        for c in range(n):
            out[c][p_] = xn[c * CHUNK:(c + 1) * CHUNK]
    return out


def _head_norm_store(vals, gates, normw, bones, act, store):
    n = len(vals)
    for p_ in range(N_PAIRS):
        v = jnp.concatenate([vals[c][p_] for c in range(n)], axis=0) if n > 1 else vals[0][p_]
        g = jnp.concatenate([gates[c][p_] for c in range(n)], axis=0) if n > 1 else gates[0][p_]
        y = v * lax.rsqrt(_seg_mean(v * v, bones) + EPS) * normw * act(g)
        for c in range(n):
            store(c, p_, y[c * CHUNK:(c + 1) * CHUNK].astype(BF16))


def _dn_local(chunks, prm, prec):
    n = len(chunks)
    bones = prm["bones"]
    incl, strict = _pair_masks()
    ltri, upair = _ltri(), _upair()
    units = [(c, p_) for c in range(n) for p_ in range(N_PAIRS)]

    g_cols = _mm_sel_l(ltri, jnp.concatenate([-jnp.exp(prm["alog_row"]) * _softplus(ch["zs"]) for ch in chunks],
                                             axis=1))
    g_rows = _mm_sel_r(jnp.concatenate([-jnp.exp(prm["rowalog"]) * _softplus(ch["rows"] + prm["rowadd"])
                                        for ch in chunks], axis=0), upair)
    gates = [(_sigmoid(ch["zs"]), g_cols[:, c * LANES:(c + 1) * LANES], g_rows[c * 16:(c + 1) * 16])
             for c, ch in enumerate(chunks)]

    qn = _l2norm_pairs([ch["q"] for ch in chunks], bones, HEAD_DIM ** -0.5)
    kn = _l2norm_pairs([ch["k"] for ch in chunks], bones, 1.0)

    beta = [_expand(gates[c][0], 2 * p_, 2 * p_ + 1) for c, p_ in units]
    g_col = [_expand(gates[c][1], 6 + 2 * p_, 7 + 2 * p_) for c, p_ in units]
    dec = [jnp.exp(jnp.where(incl, gc - gates[c][2][p_:p_ + 1, :], NEG)) for gc, (c, p_) in zip(g_col, units)]
    kb = [kn[c][p_] * b for b, (c, p_) in zip(beta, units)]
    sc = [_mm(jnp.concatenate([kb_, qn[c][p_]], axis=0), _x2(kn[c][p_]), _NT)
          for kb_, (c, p_) in zip(kb, units)]
    a_in = [s[CHUNK:] * d for s, d in zip(sc, dec)]
    mbd = [_x2(jnp.where(strict, s[:CHUNK] * d, 0.0)) for s, d in zip(sc, dec)]
    tri_prec, newton, sol_prec = prec
    t_bd = _tri_inverse_units(mbd, tri_prec, newton)
    eg = [jnp.exp(gc) for gc in g_col]
    rhs = [jnp.concatenate([_x2(chunks[c]["v"][:, p_ * LANES:(p_ + 1) * LANES] * b), _x2(kb_ * e)], axis=1)
           for b, kb_, e, (c, p_) in zip(beta, kb, eg, units)]
    sol = [_mm(t, r, prec=sol_prec) for t, r in zip(t_bd, rhs)]
    out = [[None] * N_PAIRS for _ in range(n)]
    for i, (c, p_) in enumerate(units):
        g_last = _last_row(g_col[i])
        out[c][p_] = dict(u=_fold(sol[i][:, :LANES]), w=_fold(sol[i][:, LANES:]), a_in=a_in[i],
                          qg=qn[c][p_] * eg[i], kd=kn[c][p_] * jnp.exp(g_last - g_col[i]),
                          eg_last=jnp.exp(g_last))
    return out


def _dn_seq_step(zq_ref, zk_ref, zv_ref, zg_ref, zs_ref, rows_ref,
                 add_ref, alog_ref, rowadd_ref, rowalog_ref, nw_ref, o_ref, s_scr, *, bb, prec):
    chunks = [dict(q=zq_ref[b], k=zk_ref[b], v=zv_ref[b], zs=zs_ref[b] + add_ref[...], rows=rows_ref[b, 0])
              for b in range(bb)]

    prm = dict(bones=_block_ones(), alog_row=alog_ref[...], rowadd=rowadd_ref[...],
               rowalog=rowalog_ref[...])
    loc = _dn_local(chunks, prm, prec)
    bdm = _bd_mask()
    units = [(b, p_) for b in range(bb) for p_ in range(N_PAIRS)]
    s_prev = [s_scr[b, p_] for b, p_ in units]
    ws = [_mm(jnp.concatenate([loc[b][p_]["w"], loc[b][p_]["qg"]], axis=0), s)
          for s, (b, p_) in zip(s_prev, units)]
    v_new = [loc[b][p_]["u"] - w_[:CHUNK] for w_, (b, p_) in zip(ws, units)]
    o = [w_[CHUNK:] + _mm(loc[b][p_]["a_in"], _x2(v)) for w_, v, (b, p_) in zip(ws, v_new, units)]
    upd = [_mm(loc[b][p_]["kd"].T, v) for v, (b, p_) in zip(v_new, units)]
    for i, (b, p_) in enumerate(units):
        s_scr[b, p_] = loc[b][p_]["eg_last"][0:1, :] * s_prev[i] + jnp.where(bdm, upd[i], 0.0)

    def store(b, p_, val):
        o_ref[b, :, p_ * LANES:(p_ + 1) * LANES] = val

    _head_norm_store([[o[b * N_PAIRS + p_] for p_ in range(N_PAIRS)] for b in range(bb)],
                     [[zg_ref[b, :, p_ * LANES:(p_ + 1) * LANES] for p_ in range(N_PAIRS)] for b in range(bb)],
                     nw_ref[...], prm["bones"], _silu, store)


def _ml_units(chunks, prm):
    n = len(chunks)
    incl, _ = _pair_masks()
    lo = _lane_lo((CHUNK, LANES))
    ltri, upair = _ltri(), _upair()
    ones2 = prm["bones"]
    bdm2 = jnp.concatenate([_bd_mask(), _bd_mask()], axis=1)
    units = [(c, p_) for c in range(n) for p_ in range(N_PAIRS)]
    pair = lambda x, p_: x[:, p_ * LANES:(p_ + 1) * LANES]

    rows = [ch["rows"] + prm["rowadd"] for ch in chunks]
    b_cols = _mm_sel_l(ltri, jnp.concatenate([-_softplus(-ch["zs"]) for ch in chunks], axis=1))
    b_rows = _mm_sel_r(jnp.concatenate([-_softplus(-r) for r in rows], axis=0), upair)
    comp = [dict(ig_c=ch["zs"], b_c=b_cols[:, c * LANES:(c + 1) * LANES], ig_r=rows[c],
                 b_r=b_rows[c * 16:(c + 1) * 16]) for c, ch in enumerate(chunks)]

    qp = [pair(chunks[c]["q"], p_) for c, p_ in units]
    kp = [pair(chunks[c]["k"], p_) * (HEAD_DIM ** -0.5) for c, p_ in units]
    vp = [pair(chunks[c]["v"], p_) for c, p_ in units]
    b_col = [_expand(comp[c]["b_c"], 18 + 2 * p_, 19 + 2 * p_) for c, p_ in units]
    ig_col = [_expand(comp[c]["ig_c"], 12 + 2 * p_, 13 + 2 * p_) for c, p_ in units]
    dl = [jnp.where(incl, bc - comp[c]["b_r"][6 + p_:7 + p_, :] + comp[c]["ig_r"][3 + p_:4 + p_, :], NEG)
          for bc, (c, p_) in zip(b_col, units)]
    mx = [jnp.where(lo, jnp.max(jnp.where(lo, d, NEG), axis=1, keepdims=True),
                    jnp.max(jnp.where(lo, NEG, d), axis=1, keepdims=True)) for d in dl]
    inter = [bc + chunks[c]["ms"][p_] for bc, (c, p_) in zip(b_col, units)]
    mt = [jnp.maximum(a, b) for a, b in zip(inter, mx)]
    s = [_mm(q_, _x2(k_), _NT) * jnp.exp(d - m) for q_, k_, d, m in zip(qp, kp, dl, mt)]
    sv = [_mm(s_, jnp.concatenate([_x2(v_), ones2], axis=1)) for s_, v_ in zip(s, vp)]
    cn_prev = [chunks[c]["cn"][p_] for c, p_ in units]
    qcn = [_mm(q_, cn) for q_, cn in zip(qp, cn_prev)]
    w_inter = [jnp.exp(a - m) for a, m in zip(inter, mt)]
    h = [(wi * qc[:, :LANES] + sv_[:, :LANES])
         / jnp.maximum(jnp.abs(wi * qc[:, LANES:] + sv_[:, LANES:]), jnp.exp(-m))
         for wi, qc, sv_, m in zip(w_inter, qcn, sv, mt)]
    m_end = [_last_row(m) for m in mt]
    b_last = [_last_row(bc) for bc in b_col]
    a_end = [jnp.exp(bl + chunks[c]["ms"][p_] - me) for bl, me, (c, p_) in zip(b_last, m_end, units)]
    kw_t = [(k_ * jnp.exp(bl - bc + ig - me)).T for k_, bl, bc, ig, me in zip(kp, b_last, b_col, ig_col, m_end)]
    upd = [_mm(k_, jnp.concatenate([v_, jnp.ones((CHUNK, LANES), F32)], axis=1)) for k_, v_ in zip(kw_t, vp)]
    out = [[None] * N_PAIRS for _ in range(n)]
    for i, (c, p_) in enumerate(units):
        a_row = a_end[i][0:1, :]
        new_state = jnp.concatenate([a_row, a_row], axis=1) * cn_prev[i] + jnp.where(bdm2, upd[i], 0.0)
        out[c][p_] = (h[i], new_state, m_end[i][0:1, :])
    return out


def _ml_seq_step(zq_ref, zk_ref, zv_ref, zo_ref, zs_ref, rows_ref, add_ref, rowadd_ref, nw_ref, o_ref,
                 cn_scr, m_scr, *, bb):
    prm = dict(bones=_block_ones(), rowadd=rowadd_ref[...])
    chunks = []
    for b in range(bb):
        chunks.append(dict(q=zq_ref[b], k=zk_ref[b], v=zv_ref[b], zs=zs_ref[b] + add_ref[...],
                           rows=rows_ref[b, 0],
                           ms=[m_scr[b, p_:p_ + 1, :] for p_ in range(N_PAIRS)],
                           cn=[cn_scr[b, p_] for p_ in range(N_PAIRS)]))
    res = _ml_units(chunks, prm)
    for b in range(bb):
        for p_ in range(N_PAIRS):
            cn_scr[b, p_] = res[b][p_][1]
            m_scr[b, p_:p_ + 1, :] = res[b][p_][2]

    def store(b, p_, val):
        o_ref[b, :, p_ * LANES:(p_ + 1) * LANES] = val

    _head_norm_store([[res[b][p_][0] for p_ in range(N_PAIRS)] for b in range(bb)],
                     [[zo_ref[b, :, p_ * LANES:(p_ + 1) * LANES] for p_ in range(N_PAIRS)] for b in range(bb)],
                     nw_ref[...], prm["bones"], _sigmoid, store)


def _mixers_seq_kernel(dq_ref, dk_ref, dv_ref, dg_ref, lq_ref, lk_ref, lv_ref, lo_ref, zs_ref, rows_ref,
                       s0_ref, cn0_ref, m0_ref,
                       add_ref, alog_ref, rowadd_ref, rowalog_ref, dnw_ref, mlw_ref,
                       dn_ref, ml_ref, s_out_ref, cn_out_ref, m_out_ref,
                       s_scr, cn_scr, m_scr, *, bb, prec):
    tstep = pl.program_id(1)

    @pl.when(tstep == 0)
    def _():
        s_scr[...] = s0_ref[...]
        cn_scr[...] = cn0_ref[...]
        m_scr[...] = m0_ref[...]

    _dn_seq_step(dq_ref, dk_ref, dv_ref, dg_ref, zs_ref, rows_ref, add_ref, alog_ref, rowadd_ref,
                 rowalog_ref, dnw_ref, dn_ref, s_scr, bb=bb, prec=prec)
    _ml_seq_step(lq_ref, lk_ref, lv_ref, lo_ref, zs_ref, rows_ref, add_ref, rowadd_ref, mlw_ref, ml_ref,
                 cn_scr, m_scr, bb=bb)

    @pl.when(tstep == pl.num_programs(1) - 1)
    def _():
        s_out_ref[...] = s_scr[...]
        cn_out_ref[...] = cn_scr[...]
        m_out_ref[...] = m_scr[...]


def _mixers_seq(z3, zc, rows4, s0, cn0, m0, prm, bb, prec):
    batch, seq_len, _ = z3.shape
    fix2 = lambda g, i: (0, 0)
    zspec = lambda width, col: pl.BlockSpec((bb, CHUNK, width), lambda g, i: (g, i, col))
    per_seq = lambda *shape: pl.BlockSpec((bb,) + shape, lambda g, i: (g,) + (0,) * len(shape))
    s_spec, cn_spec, m_spec = per_seq(N_PAIRS, LANES, LANES), per_seq(N_PAIRS, LANES, 2 * LANES), per_seq(8, LANES)
    small = lambda rows: pl.BlockSpec((rows, LANES), fix2)
    in_specs = ([zspec(HEADS_W, c) for c in range(8)] + [zspec(LANES, GATE_COL_BLOCK)]
                + [pl.BlockSpec((bb, 1, 16, LANES), lambda g, i: (g, i, 0, 0)), s_spec, cn_spec, m_spec,
                   small(1), small(1), small(16), small(16), small(1), small(1)])
    out_blk = pl.BlockSpec((bb, CHUNK, HEADS_W), lambda g, i: (g, i, 0))
    act = jax.ShapeDtypeStruct((batch, seq_len, HEADS_W), BF16)
    return pl.pallas_call(
        functools.partial(_mixers_seq_kernel, bb=bb, prec=prec),
        out_shape=(act, act, jax.ShapeDtypeStruct(s0.shape, F32), jax.ShapeDtypeStruct(cn0.shape, F32),
                   jax.ShapeDtypeStruct(m0.shape, F32)),
        grid=(batch // bb, seq_len // CHUNK),
        in_specs=in_specs,
        out_specs=(out_blk, out_blk, s_spec, cn_spec, m_spec),
        scratch_shapes=[pltpu.VMEM((bb, N_PAIRS, LANES, LANES), F32),
                        pltpu.VMEM((bb, N_PAIRS, LANES, 2 * LANES), F32), pltpu.VMEM((bb, 8, LANES), F32)],
        compiler_params=pltpu.CompilerParams(dimension_semantics=("parallel", "arbitrary"),
                                             vmem_limit_bytes=VMEM_LIMIT),
        name="mixers_seq",
    )(zc, zc, zc, *([z3] * 6), rows4, s0, cn0, m0, prm["add_row"], prm["alog_row"],
      prm["rowadd"], prm["rowalog"], prm["dn_normw"], prm["ml_normw"])


def _lane_mixers_kernel(zq, zk, zv, zg, lq, lk, lv, lo, zs, cpq, cpk, cpv, cwq, cwk, cwv, bias, alog, dnw, mlw,
                        s0, c0, n0, m0, dn_o, ml_o, s_o, c_o, n_o, m_o, k_s, q_s, *, n_tok):
    h = pl.program_id(0)
    nb = s_o.shape[-1]
    zrow = lambda t, r: zs[t, pl.ds(r, 1), :]
    prow = lambda ref, r: ref[pl.ds(r, 1), :]
    bcast = lambda ref, dk: jnp.broadcast_to(ref[pl.ds(dk, 1), :], (HEAD_DIM, nb))
    zeros = jnp.zeros((HEAD_DIM, nb), F32)

    def conv(z_ref, cp_ref, cw_ref):
        e = [cp_ref[j] for j in range(CONV_W - 1)] + [z_ref[t] for t in range(n_tok)]
        outs = []
        for t in range(n_tok):
            acc = e[t] * cw_ref[0]
            for j in range(1, CONV_W):
                acc = acc + e[t + j] * cw_ref[j]
            outs.append(_silu(acc))
        return outs

    def l2n(x, scale):
        return x * (lax.rsqrt(jnp.sum(x * x, axis=0, keepdims=True) + EPS) * scale)

    def head_norm(x, w):
        return x * lax.rsqrt(jnp.mean(x * x, axis=0, keepdims=True) + EPS) * w

    qc, kc, vc = conv(zq, cpq, cwq), conv(zk, cpk, cwk), conv(zv, cpv, cwv)
    s_o[...] = s0[...]
    for t in range(n_tok):
        k_s[...] = l2n(kc[t], 1.0)
        q_s[...] = l2n(qc[t], HEAD_DIM ** -0.5)
        beta = _sigmoid(zrow(t, h))
        a = jnp.exp(-jnp.exp(prow(alog, N_HEADS + h))
                    * _softplus(zrow(t, N_HEADS + h) + prow(bias, N_HEADS + h)))
        ks = lax.fori_loop(0, HEAD_DIM, lambda dk, acc: acc + bcast(k_s, dk) * s_o[dk], zeros, unroll=8)
        delta = beta * (vc[t] - a * ks)

        def dn_update(dk, acc, a=a, delta=delta):
            s_new = a * s_o[dk] + bcast(k_s, dk) * delta
            s_o[dk] = s_new
            return acc + bcast(q_s, dk) * s_new

        o = lax.fori_loop(0, HEAD_DIM, dn_update, zeros, unroll=8)
        dn_o[t] = (head_norm(o, dnw[...]) * _silu(zg[t])).astype(BF16)

    c_o[...] = c0[...]
    n = n0[...]
    m = m0[...]
    for t in range(n_tok):
        q, k, v = lq[t], lk[t] * (HEAD_DIM ** -0.5), lv[t]
        ig = zrow(t, 2 * N_HEADS + h) + prow(bias, 2 * N_HEADS + h)
        lsf = -_softplus(-(zrow(t, 3 * N_HEADS + h) + prow(bias, 3 * N_HEADS + h)))
        m_new = jnp.maximum(lsf + m, ig)
        fp = jnp.exp(lsf + m - m_new)
        ip = jnp.exp(ig - m_new)
        k_s[...] = k * ip
        q_s[...] = q

        def ml_update(dk, acc, fp=fp, v=v):
            c_new = fp * c_o[dk] + bcast(k_s, dk) * v
            c_o[dk] = c_new
            return acc + bcast(q_s, dk) * c_new

        num = lax.fori_loop(0, HEAD_DIM, ml_update, zeros, unroll=8)
        n = fp * n + ip * k
        den = jnp.sum(q * n, axis=0, keepdims=True)
        hh = num / jnp.maximum(jnp.abs(den), jnp.exp(-m_new))
        m = m_new
        ml_o[t] = (head_norm(hh, mlw[...]) * _sigmoid(lo[t])).astype(BF16)
    n_o[...] = n
    m_o[...] = m


def _lane_mixers(z_t, conv_t, s0, c0, n0, m0, layer, prm):
    n_tok, _, nb = z_t.shape
    hd = HEAD_DIM
    zblk = lambda first: pl.BlockSpec((n_tok, hd, nb), lambda h: (0, first + h, 0))
    cblk = lambda taps, seg: pl.BlockSpec((taps, hd, nb), lambda h: (0, seg * N_HEADS + h, 0))
    wblk = lambda seg: pl.BlockSpec((CONV_W, hd, nb), lambda h: (0, seg * N_HEADS + h, 0))
    whole = lambda shape: pl.BlockSpec(shape, lambda h: (0,) * len(shape))
    mat_in = pl.BlockSpec((None, None, hd, hd, nb), lambda h: (layer, h, 0, 0, 0))
    mat_out = pl.BlockSpec((None, hd, hd, nb), lambda h: (h, 0, 0, 0))
    small_rows = 32
    in_specs = [zblk(seg * N_HEADS) for seg in range(8)] + [
        pl.BlockSpec((n_tok, small_rows, nb), lambda h: (0, (Z_COLS - LANES) // small_rows, 0)),
        cblk(CONV_W - 1, 0), cblk(CONV_W - 1, 1), cblk(CONV_W - 1, 2), wblk(0), wblk(1), wblk(2),
        whole((small_rows, nb)), whole((small_rows, nb)), whole((hd, nb)), whole((hd, nb)),
        mat_in, mat_in,
        pl.BlockSpec((None, None, hd, nb), lambda h: (layer, h, 0, 0)),
        pl.BlockSpec((None, None, 1, nb), lambda h: (layer, h, 0, 0))]
    out_blk = pl.BlockSpec((n_tok, hd, nb), lambda h: (0, h, 0))
    act = jax.ShapeDtypeStruct((n_tok, HEADS_W, nb), BF16)
    mat = jax.ShapeDtypeStruct((N_HEADS, hd, hd, nb), F32)
    return pl.pallas_call(
        functools.partial(_lane_mixers_kernel, n_tok=n_tok),
        out_shape=(act, act, mat, mat, jax.ShapeDtypeStruct((N_HEADS, hd, nb), F32),
                   jax.ShapeDtypeStruct((N_HEADS, 1, nb), F32)),
        grid=(N_HEADS,),
        in_specs=in_specs,
        out_specs=(out_blk, out_blk, mat_out, mat_out,
                   pl.BlockSpec((None, hd, nb), lambda h: (h, 0, 0)), pl.BlockSpec((None, 1, nb), lambda h: (h, 0, 0))),
        scratch_shapes=[pltpu.VMEM((hd, nb), F32), pltpu.VMEM((hd, nb), F32)],
        compiler_params=pltpu.CompilerParams(dimension_semantics=("parallel",), vmem_limit_bytes=VMEM_LIMIT),
        name="lane_mixers",
    )(*([z_t] * 9), conv_t, conv_t, conv_t, prm["conv_wb"], prm["conv_wb"], prm["conv_wb"], prm["bias_b"],
      prm["alog_b"], prm["dn_normb"], prm["ml_normb"], s0, c0, n0, m0)


def _pack_w_in(w_in):
    offs = np.concatenate([[0], np.cumsum(IN_SIZES)])
    seg = lambda i: w_in[..., int(offs[i]):int(offs[i + 1])].astype(BF16)
    small = jnp.concatenate([seg(3), seg(4), seg(9), seg(10)], axis=-1)
    small = jnp.pad(small, ((0, 0),) * (w_in.ndim - 1) + ((0, LANES - 4 * N_HEADS),))
    cols = [seg(0), seg(1), seg(2), seg(5), seg(6), seg(7), seg(8), seg(11), seg(12), small]
    return jnp.concatenate(cols, axis=-1)


def _pair_rows(v6):
    return jnp.repeat(v6.reshape(N_PAIRS, 2), HEAD_DIM, axis=1).reshape(N_PAIRS, LANES)


def _layer_params(l, a):
    f = lambda name: a[name][l].astype(F32)
    zeros6 = jnp.zeros((N_HEADS,), F32)
    add_row = jnp.concatenate([zeros6, f("dn_dt_bias"), f("ml_i_bias"), f("ml_f_bias")])
    add_row = jnp.pad(add_row, (0, LANES - 4 * N_HEADS))[None, :]
    alog_row = jnp.pad(jnp.concatenate([zeros6, f("dn_a_log")]), (0, LANES - 2 * N_HEADS))[None, :]
    rowadd = jnp.concatenate([_pair_rows(f("dn_dt_bias")), _pair_rows(f("ml_i_bias")),
                              _pair_rows(f("ml_f_bias")), jnp.zeros((7, LANES), F32)], axis=0)
    rowalog = jnp.concatenate([_pair_rows(f("dn_a_log")), jnp.zeros((13, LANES), F32)], axis=0)
    eye_g = jnp.eye(S5_GROUPS, dtype=F32)
    bd_in = lambda b: jnp.einsum("gpc,gh->gchp", b, eye_g).reshape(S5_WIDTH, S5_STATE)
    bd_out = lambda c: jnp.einsum("gcp,gh->gphc", c, eye_g).reshape(S5_STATE, S5_WIDTH).astype(BF16)
    return dict(
        layer=l, norm_mix=f("norm_mix")[None, :],
        add_row=add_row, alog_row=alog_row, rowadd=rowadd, rowalog=rowalog,
        conv_w8=jnp.pad(f("dn_conv_w"), ((0, 8 - CONV_W), (0, 0))),
        dn_normw=jnp.tile(f("dn_norm"), 2)[None, :], ml_normw=jnp.tile(f("ml_norm"), 2)[None, :],
        s5=dict(lre=f("s5_lam_re").reshape(1, S5_STATE), lim=f("s5_lam_im").reshape(1, S5_STATE),
                ldt=jnp.repeat(f("s5_log_dt"), S5_P)[None, :],
                bre=bd_in(f("s5_b_re")), bim=bd_in(f("s5_b_im")),
                cre=bd_out(f("s5_c_re")), cim=bd_out(f("s5_c_im")),
                d=f("s5_d").reshape(1, S5_WIDTH), wg=f("s5_w_glu").astype(BF16),
                bg=f("s5_b_glu")[None, :]),
        norm_x=f("norm_x")[None, :], norm_ff=f("norm_ff")[None, :],
    )


def _shared_weights(a):
    cast = lambda name: a[name].astype(BF16)
    return dict(w_in=_pack_w_in(a["w_in"]), w_out=cast("w_out"), w_xq=cast("w_xq"), w_xo=cast("w_xo"),
                w_ff1=cast("w_ff1"), w_ff2=cast("w_ff2"), w_xk=cast("w_xk"), w_xv=cast("w_xv"))


def _row_form(z, n_chunks):
    zs = z[..., Z_COLS - LANES:Z_COLS - LANES + 4 * N_HEADS].reshape(n_chunks, CHUNK, 4, N_PAIRS, 2)
    r = jnp.transpose(zs[:, :, 1:], (0, 2, 3, 4, 1)).reshape(n_chunks, 9, LANES)
    return jnp.pad(r, ((0, 0), (0, 7), (0, 0)))


def _un_bd_pairs(s):
    n = s.shape[0]
    a = s[:, :, :HEAD_DIM, :HEAD_DIM]
    b = s[:, :, HEAD_DIM:, HEAD_DIM:]
    return jnp.stack([a, b], axis=2).reshape(n, N_HEADS, HEAD_DIM, HEAD_DIM)


def _unpack_m(mp):
    return mp[:, :N_PAIRS, ::HEAD_DIM].reshape(mp.shape[0], N_HEADS)


def _tile(n, pref):
    tile = min(n, pref)
    assert n % tile == 0, (n, pref)
    return tile


def _layer_prompt(x, mem_k, mem_v, prm, wts, final_g):
    batch, seq_len, _ = x.shape
    layer = prm["layer"]
    z3, zc = _norm_matmul_conv(x, prm["norm_mix"], wts["w_in"], jnp.zeros((batch, 8, CONV_CH), F32),
                               prm["conv_w8"], layer, _tile(seq_len, CONV_ROW_TILE))
    rows4 = _row_form(z3, batch * seq_len // CHUNK).reshape(batch, seq_len // CHUNK, 16, LANES)
    new_conv = z3[:, seq_len - (CONV_W - 1):, :CONV_CH]

    dn, ml, s_new, cn_new, m_new = _mixers_seq(
        z3, zc, rows4, jnp.zeros((batch, N_PAIRS, LANES, LANES), F32),
        jnp.zeros((batch, N_PAIRS, LANES, 2 * LANES), F32), jnp.zeros((batch, 8, LANES), F32), prm,
        _tile(batch, SEQ_BATCH), SEQ_SOLVE)
    zero_h = jnp.zeros((batch, S5_STATE), F32)
    s5o, hr, hi = _s5(z3, zero_h, zero_h, prm["s5"], _tile(seq_len, S5_STEPS), S5_STATE, True)

    g_fin = final_g if final_g is not None else prm["norm_ff"]
    x3 = _post_mix(dn, ml, s5o, x, mem_k, mem_v, wts, prm, g_fin, layer, _tile(seq_len, POST_TILE),
                   final_g is not None)
    st = (new_conv, _un_bd_pairs(s_new), _un_bd_pairs(cn_new[..., :LANES]),
          _un_bd_pairs(cn_new[..., LANES:])[..., 0], _unpack_m(m_new),
          hr.reshape(batch, S5_GROUPS, S5_P), hi.reshape(batch, S5_GROUPS, S5_P))
    return x3, st


def _layer_sample(x, cache_k, cache_v, conv_buf, dn_s, ml_c, ml_n, ml_m, ssm_re, ssm_im, prm, wts, batch,
                  seq_len, final_g):
    t = batch * seq_len
    layer = prm["layer"]
    z = _norm_matmul(x, prm["norm_mix"], wts["w_in"], layer, t)
    z3 = z.reshape(batch, seq_len, Z_COLS)
    new_conv = jnp.concatenate([conv_buf, z3[:, :, :CONV_CH]], axis=1)[:, seq_len:]
    lanes = lambda v: jnp.broadcast_to(v[..., None], v.shape + (batch,))
    mix_prm = dict(conv_wb=lanes(prm["conv_w8"][:CONV_W]), bias_b=lanes(prm["add_row"][0, :32]),
                   alog_b=lanes(prm["alog_row"][0, :32]), dn_normb=lanes(prm["dn_normw"][0, :HEAD_DIM]),
                   ml_normb=lanes(prm["ml_normw"][0, :HEAD_DIM]))
    dn_t, ml_t, s_new, c_new, n_new, m_new = _lane_mixers(
        jnp.transpose(z3, (1, 2, 0)), jnp.transpose(conv_buf, (1, 2, 0)), dn_s, ml_c, ml_n, ml_m, layer, mix_prm)
    to_rows = lambda a: jnp.transpose(a, (2, 0, 1)).reshape(1, t, HEADS_W)
    dn, ml = to_rows(dn_t), to_rows(ml_t)

    u_tm = jnp.transpose(z3[:, :, 3072:3072 + S5_WIDTH], (1, 0, 2))
    y_tm, hr, hi = _s5(u_tm.reshape(t, S5_WIDTH), ssm_re.reshape(batch, S5_STATE),
                       ssm_im.reshape(batch, S5_STATE), prm["s5"], seq_len, LANES, False)
    s5o = jnp.transpose(y_tm.reshape(seq_len, batch, S5_WIDTH), (1, 0, 2)).reshape(t, S5_WIDTH)

    x1, q = _out_proj(dn, ml, s5o, x, wts, prm, layer, t)
    q3 = jnp.pad(q.reshape(batch, seq_len, X_WIDTH), ((0, 0), (0, Q_ROWS - seq_len), (0, 0)))
    att = _attn_batch(q3, cache_k, cache_v, layer, _tile(batch, ATTN_BATCH))[:, :seq_len].reshape(1, t, X_WIDTH)
    g_fin = final_g if final_g is not None else prm["norm_ff"]
    x3 = _ffn(x1, att, wts, prm, g_fin, layer, t, final_g is not None)
    st = (new_conv, jnp.transpose(s_new, (3, 0, 1, 2)), jnp.transpose(c_new, (3, 0, 1, 2)),
          jnp.transpose(n_new, (2, 0, 1)), jnp.transpose(m_new[:, 0, :], (1, 0)),
          hr.reshape(batch, S5_GROUPS, S5_P), hi.reshape(batch, S5_GROUPS, S5_P))
    return x3, st


def kernel(x_prompt, x_sample, mem_prompt, state_dn_conv, state_dn_s, state_ml_c, state_ml_n, state_ml_m, state_ssm_re, state_ssm_im, cache_mem_k, cache_mem_v, norm_mix, w_in, dn_conv_w, dn_a_log, dn_dt_bias, dn_norm, ml_i_bias, ml_f_bias, ml_norm, s5_lam_re, s5_lam_im, s5_log_dt, s5_b_re, s5_b_im, s5_c_re, s5_c_im, s5_d, s5_w_glu, s5_b_glu, w_out, norm_x, norm_mem, w_xq, w_xk, w_xv, w_xo, norm_ff, w_ff1, w_ff2, norm_final):
    a = dict(norm_mix=norm_mix, w_in=w_in, dn_conv_w=dn_conv_w, dn_a_log=dn_a_log, dn_dt_bias=dn_dt_bias,
             dn_norm=dn_norm, ml_i_bias=ml_i_bias, ml_f_bias=ml_f_bias, ml_norm=ml_norm,
             s5_lam_re=s5_lam_re, s5_lam_im=s5_lam_im, s5_log_dt=s5_log_dt, s5_b_re=s5_b_re,
             s5_b_im=s5_b_im, s5_c_re=s5_c_re, s5_c_im=s5_c_im, s5_d=s5_d, s5_w_glu=s5_w_glu,
             s5_b_glu=s5_b_glu, w_out=w_out, norm_x=norm_x, w_xq=w_xq, w_xo=w_xo, norm_ff=norm_ff,
             w_ff1=w_ff1, w_ff2=w_ff2)
    prms = [_layer_params(l, a) for l in range(DEPTH)]
    wts = _shared_weights(dict(a, w_xk=w_xk, w_xv=w_xv))
    g_final = norm_final.astype(F32)[None, :]

    bp, lp, _ = x_prompt.shape
    bs, ls, _ = x_sample.shape

    xp = x_prompt
    mk, mk_t, mv, mv_t = _mem_proj(mem_prompt, norm_mem.astype(F32), wts["w_xk"], wts["w_xv"])
    p_st = []
    for l in range(DEPTH):
        xp, st = _layer_prompt(xp, mk, mv, prms[l], wts, g_final if l == DEPTH - 1 else None)
        p_st.append(st)
    y_prompt = xp
    p_mem = (mk_t.reshape(DEPTH, bp, N_MEM, X_HEADS, X_HD), mv_t.reshape(DEPTH, bp, N_MEM, X_HEADS, X_HD))

    xs = x_sample.reshape(1, bs * ls, D_MODEL)
    cache_k = cache_mem_k.reshape(DEPTH, bs, N_MEM * X_HEADS, X_HD)
    cache_v = cache_mem_v.reshape(DEPTH, bs, N_MEM * X_HEADS, X_HD)
    dn_s_t = jnp.transpose(state_dn_s, (0, 2, 3, 4, 1))
    ml_c_t = jnp.transpose(state_ml_c, (0, 2, 3, 4, 1))
    ml_n_t = jnp.transpose(state_ml_n, (0, 2, 3, 1))
    ml_m_t = jnp.transpose(state_ml_m, (0, 2, 1))[:, :, None, :]
    s_st = []
    for l in range(DEPTH):
        xs, st = _layer_sample(xs, cache_k, cache_v, state_dn_conv[l], dn_s_t, ml_c_t, ml_n_t, ml_m_t,
                               state_ssm_re[l], state_ssm_im[l], prms[l], wts, bs, ls,
                               g_final if l == DEPTH - 1 else None)
        s_st.append(st)
    y_sample = xs.reshape(bs, ls, D_MODEL)

    stack = lambda sts, i: jnp.stack([s[i] for s in sts])
    return ((y_prompt, y_sample) + tuple(stack(p_st, i) for i in range(7)) + p_mem
            + tuple(stack(s_st, i) for i in range(7)))
```

```python
import functools
import math

import numpy as np
import jax
import jax.numpy as jnp
from jax import lax
from jax.experimental import pallas as pl
from jax.experimental.pallas import tpu as pltpu

F32 = jnp.float32
BF16 = jnp.bfloat16

D_MODEL = 1024
DEPTH = 2
N_HEADS = 6
HEAD_DIM = 64
N_PAIRS = N_HEADS // 2
HEADS_W = N_HEADS * HEAD_DIM
CONV_W = 4
CONV_CH = 3 * HEADS_W
S5_WIDTH = 256
S5_GROUPS = 16
S5_P = 64
S5_STATE = S5_GROUPS * S5_P
N_MEM = 256
X_HEADS = 4
X_HD = 128
X_WIDTH = X_HEADS * X_HD
D_FF = 4 * D_MODEL
EPS = 1e-6
IN_SIZES = (384, 384, 384, 6, 6, 384, 384, 384, 384, 6, 6, 384, 256)

LANES = 128
CHUNK = 64
Q_ROWS = 8
NEG = -1e30
Z_COLS = 3456
S5_COL_BLOCK = 3072 // S5_WIDTH
GATE_COL_BLOCK = (Z_COLS - LANES) // LANES
VMEM_LIMIT = 52 * 1024 * 1024

ROW_TILE = 1024
CONV_ROW_TILE = 512
POST_TILE = 512
SEQ_BATCH = 8
S5_STEPS = 64
ATTN_BATCH = 8
FF_CHUNK = 1024
TRI_BASE = 8
SEQ_SOLVE = (1, True, 1)

_NN = (((1,), (0,)), ((), ()))
_NT = (((1,), (1,)), ((), ()))


def _dot(a, b, dims=_NN):
    return lax.dot_general(a, b, dims, preferred_element_type=F32)


def _split(a, n):
    out = []
    r = a
    for i in range(n):
        h = r.astype(BF16)
        out.append(h)
        if i + 1 < n:
            r = r - h.astype(F32)
    return out


def _mm(a, b, dims=_NN, prec=1):
    if prec == 1:
        return _dot(a.astype(BF16), b.astype(BF16), dims)
    return _mmp(_parts(a, prec), _parts(b, prec), dims)


def _parts(a, prec):
    return tuple(_split(a, 1 if prec == 1 else 2))


def _mmp(ap, bp, dims=_NN):
    acc = None
    if len(bp) > 1:
        acc = _dot(ap[0], bp[1], dims)
    if len(ap) > 1:
        cross = _dot(ap[1], bp[0], dims)
        acc = cross if acc is None else acc + cross
    lead = _dot(ap[0], bp[0], dims)
    return lead if acc is None else acc + lead


def _mm_sel_r(a, sel, n=3):
    parts = _split(a, n)
    acc = _dot(parts[-1], sel)
    for p_ in parts[-2::-1]:
        acc = acc + _dot(p_, sel)
    return acc


def _mm_sel_l(sel, a, n=3):
    parts = _split(a, n)
    acc = _dot(sel, parts[-1])
    for p_ in parts[-2::-1]:
        acc = acc + _dot(sel, p_)
    return acc


def _sigmoid(x):
    return 1.0 / (1.0 + jnp.exp(-x))


def _silu(x):
    return x * _sigmoid(x)


def _softplus(x):
    return jnp.maximum(x, 0.0) + jnp.log(1.0 + jnp.exp(-jnp.abs(x)))


def _rmsnorm_rows(x, g):
    return x * lax.rsqrt(jnp.mean(x * x, axis=-1, keepdims=True) + EPS) * g


def _iota(shape, dim):
    return lax.broadcasted_iota(jnp.int32, shape, dim)


def _idiv(x, n):
    return x >> (n.bit_length() - 1)


def _imod(x, n):
    return x & (n - 1)


def _lane_lo(shape):
    return _iota(shape, 1) < HEAD_DIM


def _x2(x):
    lo = _lane_lo(x.shape)
    return jnp.concatenate([jnp.where(lo, x, 0.0), jnp.where(lo, 0.0, x)], axis=0)


def _x2b(x):
    return _x2(x.astype(BF16))


def _fold(y):
    r = y.shape[0] // 2
    return y[:r] + y[r:]


def _expand(a, c0, c1):
    r = a.shape[0]
    return jnp.where(_lane_lo((r, LANES)), a[:, c0:c0 + 1], a[:, c1:c1 + 1])


def _bd_mask():
    return (_iota((LANES, LANES), 0) < HEAD_DIM) == (_iota((LANES, LANES), 1) < HEAD_DIM)


def _block_ones():
    return jnp.where(_bd_mask(), 1.0, 0.0).astype(BF16)


def _seg_mean(x, bones):
    return _mm_sel_r(x, bones, n=1) * (1.0 / HEAD_DIM)


def _ltri():
    return jnp.where(_iota((CHUNK, CHUNK), 1) <= _iota((CHUNK, CHUNK), 0), 1.0, 0.0).astype(BF16)


def _upair():
    a = _iota((LANES, LANES), 0)
    b = _iota((LANES, LANES), 1)
    return jnp.where(_bd_mask() & (a <= b), 1.0, 0.0).astype(BF16)


def _pair_masks():
    i = _iota((CHUNK, LANES), 0)
    j = _imod(_iota((CHUNK, LANES), 1), HEAD_DIM)
    return j <= i, j < i


def _blk_eq(n, b):
    return _idiv(_iota((n, n), 0), b) == _idiv(_iota((n, n), 1), b)


def _last_row(a):
    return jnp.broadcast_to(a[CHUNK - 1:CHUNK, :], a.shape)


def _tri_inverse_units(mbds, prec, newton):
    n = LANES
    eye = jnp.where(_iota((n, n), 0) == _iota((n, n), 1), 1.0, 0.0)
    n_sq = TRI_BASE.bit_length() - 2
    blk = _blk_eq(n, TRI_BASE)
    m0 = [jnp.where(blk, m, 0.0) for m in mbds]
    ts = [eye - m for m in m0]
    ps = [_mm(m, m, prec=prec) for m in m0]
    for it in range(n_sq):
        if it == n_sq - 1:
            ts = [t + _mm(t, p_, prec=prec) for t, p_ in zip(ts, ps)]
        else:
            both = [_mm(jnp.concatenate([t, p_], axis=0), p_, prec=prec) for t, p_ in zip(ts, ps)]
            ts = [t + b_[:n] for t, b_ in zip(ts, both)]
            ps = [b_[n:] for b_ in both]
    b = TRI_BASE
    while b < CHUNK:
        outer = _blk_eq(n, 2 * b) & jnp.logical_not(_blk_eq(n, b))
        tp = [_parts(t, prec) for t in ts]
        xs = [_mmp(_parts(jnp.where(outer, m, 0.0), prec), t_) for m, t_ in zip(mbds, tp)]
        ts = [t - _mmp(t_, _parts(x, prec)) for t, t_, x in zip(ts, tp, xs)]
        b *= 2
    if newton:
        res = [eye - (t + _mm(m, t, prec=3)) for m, t in zip(mbds, ts)]
        ts = [t + _mm(t, r) for t, r in zip(ts, res)]
    return ts


def _wspec(rows, cols, layer, row_block=0):
    return pl.BlockSpec((None, rows, cols), lambda i: (layer, row_block, 0), pipeline_mode=pl.Buffered(1))


def _rows3(nt):
    return lambda i: (i // nt, i % nt, 0)


def _dense_call(kernel_fn, name, out_shape, in_specs, out_specs, steps, args):
    return pl.pallas_call(
        kernel_fn, out_shape=out_shape, grid=(steps,), in_specs=in_specs, out_specs=out_specs,
        compiler_params=pltpu.CompilerParams(dimension_semantics=("parallel",), vmem_limit_bytes=VMEM_LIMIT),
        name=name)(*args)


def _in_proj_kernel(x_ref, g_ref, w_ref, o_ref):
    o_ref[0] = _dot(_rmsnorm_rows(x_ref[0], g_ref[...]).astype(BF16), w_ref[...])


def _norm_matmul(x, g, w, layer, tm):
    b, l, k = x.shape
    n = w.shape[2]
    nt = l // tm
    fix = lambda i: (0, 0)
    return _dense_call(
        _in_proj_kernel, "norm_matmul", jax.ShapeDtypeStruct((b, l, n), F32),
        [pl.BlockSpec((1, tm, k), _rows3(nt)), pl.BlockSpec((1, k), fix), _wspec(k, n, layer)],
        pl.BlockSpec((1, tm, n), _rows3(nt)), b * nt, (x, g, w))


def _conv_silu(u, w8, fix):
    out = fix(pltpu.roll(u, 3, axis=0), 3) * w8[0:1, :]
    for s in (2, 1):
        out = out + fix(pltpu.roll(u, s, axis=0), s) * w8[3 - s:4 - s, :]
    out = out + u * w8[3:4, :]
    return _silu(out)


def _in_proj_conv_kernel(x_ref, g_ref, w_ref, cp_ref, cw_ref, o_ref, c_ref, u_scr, tail_scr, *, nt, n_tiles):
    i = pl.program_id(0)
    tm = u_scr.shape[0]

    @pl.when(i == 0)
    def _():
        u_scr[...] = jnp.zeros_like(u_scr)
        tail_scr[...] = jnp.zeros_like(tail_scr)

    prev = tail_scr[...]
    u_prev = u_scr[...]
    r8 = _iota((8, CONV_CH), 0)

    def fix(rolled, s):
        first = jnp.where(r8 < s, pltpu.roll(prev, s, axis=0), rolled[0:8])
        return jnp.concatenate([first, rolled[8:]], axis=0)

    c_ref[0] = _conv_silu(u_prev, cw_ref[...], fix)

    z = _dot(_rmsnorm_rows(x_ref[0], g_ref[...]).astype(BF16), w_ref[...])
    o_ref[0] = z
    first_of_seq = jnp.minimum(i, n_tiles - 1) % nt == 0
    tail_scr[...] = jnp.where(first_of_seq, cp_ref[0], u_prev[tm - 8:])
    u_scr[...] = z[:, :CONV_CH]


def _norm_matmul_conv(x, g, w, convp, cw8, layer, tm):
    b, l, k = x.shape
    n = w.shape[2]
    nt = l // tm
    n_tiles = b * nt
    fix = lambda i: (0, 0)
    cur = lambda i: jnp.minimum(i, n_tiles - 1)
    tile = lambda i: (cur(i) // nt, cur(i) % nt, 0)
    prev_tile = lambda i: (jnp.maximum(i - 1, 0) // nt, jnp.maximum(i - 1, 0) % nt, 0)
    return pl.pallas_call(
        functools.partial(_in_proj_conv_kernel, nt=nt, n_tiles=n_tiles),
        out_shape=(jax.ShapeDtypeStruct((b, l, n), F32), jax.ShapeDtypeStruct((b, l, CONV_CH), F32)),
        grid=(n_tiles + 1,),
        in_specs=[pl.BlockSpec((1, tm, k), tile), pl.BlockSpec((1, k), fix), _wspec(k, n, layer),
                  pl.BlockSpec((1, 8, CONV_CH), lambda i: (cur(i) // nt, 0, 0)), pl.BlockSpec((8, CONV_CH), fix)],
        out_specs=(pl.BlockSpec((1, tm, n), tile), pl.BlockSpec((1, tm, CONV_CH), prev_tile)),
        scratch_shapes=[pltpu.VMEM((tm, CONV_CH), F32), pltpu.VMEM((8, CONV_CH), F32)],
        compiler_params=pltpu.CompilerParams(dimension_semantics=("arbitrary",), vmem_limit_bytes=VMEM_LIMIT),
        name="norm_matmul_conv",
    )(x, g, w, convp, cw8)


def _mem_proj_kernel(x_ref, g_ref, wk_ref, wv_ref, k_ref, kt_ref, v_ref, vt_ref):
    x = x_ref[0]
    xhat = x * lax.rsqrt(jnp.mean(x * x, axis=-1, keepdims=True) + EPS)
    for l in range(DEPTH):
        xn = (xhat * g_ref[l:l + 1, :]).astype(BF16)
        for w_ref, o_ref, t_ref in ((wk_ref, k_ref, kt_ref), (wv_ref, v_ref, vt_ref)):
            res = _dot(xn, w_ref[l])
            o_ref[l, 0] = res
            for h in range(X_HEADS):
                t_ref[l, 0, pl.ds(h, N_MEM, stride=X_HEADS), :] = res[:, h * X_HD:(h + 1) * X_HD]


def _mem_proj(mem, g, wk, wv):
    b = mem.shape[0]
    whole = lambda shape: pl.BlockSpec(shape, lambda i: (0,) * len(shape))
    o_spec = pl.BlockSpec((DEPTH, 1, N_MEM, X_WIDTH), lambda i: (0, i, 0, 0))
    t_spec = pl.BlockSpec((DEPTH, 1, N_MEM * X_HEADS, X_HD), lambda i: (0, i, 0, 0))
    o_shape = jax.ShapeDtypeStruct((DEPTH, b, N_MEM, X_WIDTH), F32)
    t_shape = jax.ShapeDtypeStruct((DEPTH, b, N_MEM * X_HEADS, X_HD), F32)
    return _dense_call(
        _mem_proj_kernel, "mem_proj", (o_shape, t_shape, o_shape, t_shape),
        [pl.BlockSpec((1, N_MEM, D_MODEL), lambda i: (i, 0, 0)), whole((DEPTH, D_MODEL)),
         whole((DEPTH, D_MODEL, X_WIDTH)), whole((DEPTH, D_MODEL, X_WIDTH))],
        (o_spec, t_spec, o_spec, t_spec), b, (mem, g, wk, wv))


def _mix_residual(x, dn, ml, s5, w1_ref, w2_ref, w3_ref):
    acc = x + _dot(dn, w1_ref[...])
    acc = acc + _dot(ml, w2_ref[...])
    return acc + _dot(s5, w3_ref[...])


def _mlp_residual(x2, g_ref, f1_ref, f2_ref):
    xn = _rmsnorm_rows(x2, g_ref[...]).astype(BF16)
    acc = x2
    for c in range(D_FF // FF_CHUNK):
        hf = _dot(xn, f1_ref[:, c * FF_CHUNK:(c + 1) * FF_CHUNK])
        a = jnp.square(jnp.maximum(hf, 0.0)).astype(BF16)
        acc = acc + _dot(a, f2_ref[c * FF_CHUNK:(c + 1) * FF_CHUNK, :])
    return acc


def _out_proj_kernel(dn_ref, ml_ref, s5_ref, x_ref, w1_ref, w2_ref, w3_ref, g_ref, wq_ref, x1_ref, q_ref):
    acc = _mix_residual(x_ref[0], dn_ref[0], ml_ref[0], s5_ref[...], w1_ref, w2_ref, w3_ref)
    x1_ref[0] = acc
    q_ref[0] = _dot(_rmsnorm_rows(acc, g_ref[...]).astype(BF16), wq_ref[...])


def _w_out_specs(layer):
    return [_wspec(HEADS_W, D_MODEL, layer, 0), _wspec(HEADS_W, D_MODEL, layer, 1),
            _wspec(S5_WIDTH, D_MODEL, layer, 2 * HEADS_W // S5_WIDTH)]


def _out_proj(dn, ml, s5, x, wts, prm, layer, tm):
    b, l, _ = x.shape
    nt = l // tm
    row = _rows3(nt)
    vec = pl.BlockSpec((1, D_MODEL), lambda i: (0, 0))
    return _dense_call(
        _out_proj_kernel, "out_proj",
        (jax.ShapeDtypeStruct((b, l, D_MODEL), F32), jax.ShapeDtypeStruct((b, l, X_WIDTH), F32)),
        [pl.BlockSpec((1, tm, HEADS_W), row), pl.BlockSpec((1, tm, HEADS_W), row),
         pl.BlockSpec((tm, S5_WIDTH), lambda i: (i % nt, i // nt)), pl.BlockSpec((1, tm, D_MODEL), row)]
        + _w_out_specs(layer) + [vec, _wspec(D_MODEL, X_WIDTH, layer)],
        (pl.BlockSpec((1, tm, D_MODEL), row), pl.BlockSpec((1, tm, X_WIDTH), row)), b * nt,
        (dn, ml, s5, x, wts["w_out"], wts["w_out"], wts["w_out"], prm["norm_x"], wts["w_xq"]))


def _ffn_kernel(x_ref, o_ref, wo_ref, g_ref, f1_ref, f2_ref, gl_ref, y_ref, *, final_norm):
    acc = _mlp_residual(x_ref[0] + _dot(o_ref[0], wo_ref[...]), g_ref, f1_ref, f2_ref)
    y_ref[0] = _rmsnorm_rows(acc, gl_ref[...]) if final_norm else acc


def _ffn(x, att, wts, prm, g_last, layer, tm, final_norm):
    b, l, _ = x.shape
    nt = l // tm
    row = _rows3(nt)
    vec = pl.BlockSpec((1, D_MODEL), lambda i: (0, 0))
    return _dense_call(
        functools.partial(_ffn_kernel, final_norm=final_norm), "ffn", jax.ShapeDtypeStruct((b, l, D_MODEL), F32),
        [pl.BlockSpec((1, tm, D_MODEL), row), pl.BlockSpec((1, tm, X_WIDTH), row), _wspec(X_WIDTH, D_MODEL, layer),
         vec, _wspec(D_MODEL, D_FF, layer), _wspec(D_FF, D_MODEL, layer), vec],
        pl.BlockSpec((1, tm, D_MODEL), row), b * nt,
        (x, att, wts["w_xo"], prm["norm_ff"], wts["w_ff1"], wts["w_ff2"], g_last))


def _attn_heads(q, k_ref, v_ref):
    outs = []
    for h in range(X_HEADS):
        sl = slice(h * X_HD, (h + 1) * X_HD)
        s = _dot(q[:, sl], k_ref[0, :, sl].astype(BF16), _NT) * (X_HD ** -0.5)
        e = jnp.exp(s - jnp.max(s, axis=-1, keepdims=True))
        p = e / jnp.sum(e, axis=-1, keepdims=True)
        outs.append(_dot(p.astype(BF16), v_ref[0, :, sl].astype(BF16)))
    return jnp.concatenate(outs, axis=-1)


def _attn_batch_kernel(q_ref, k_ref, v_ref, o_ref, *, bb):
    rows = X_HEADS * Q_ROWS
    col_head = _imod(_iota((rows, N_MEM * X_HEADS), 1), X_HEADS)
    row_head = _idiv(_iota((rows, N_MEM * X_HEADS), 0), Q_ROWS)
    own = col_head == row_head
    q4 = [jnp.concatenate([q_ref[bi, :, h * X_HD:(h + 1) * X_HD] for h in range(X_HEADS)], axis=0).astype(BF16)
          for bi in range(bb)]
    s = [jnp.where(own, _dot(q_, k_ref[bi].astype(BF16), _NT) * (X_HD ** -0.5), NEG)
         for bi, q_ in enumerate(q4)]
    e = [jnp.exp(x - jnp.max(x, axis=-1, keepdims=True)) for x in s]
    p = [x / jnp.sum(x, axis=-1, keepdims=True) for x in e]
    o = [_dot(x.astype(BF16), v_ref[bi].astype(BF16)) for bi, x in enumerate(p)]
    for bi in range(bb):
        for h in range(X_HEADS):
            o_ref[bi, :, h * X_HD:(h + 1) * X_HD] = o[bi][h * Q_ROWS:(h + 1) * Q_ROWS].astype(BF16)


def _attn_batch(q, k, v, layer, bb):
    b, l, _ = q.shape
    blk = lambda i: (i, 0, 0)
    kv = pl.BlockSpec((None, bb, N_MEM * X_HEADS, X_HD), lambda i: (layer, i, 0, 0))
    return _dense_call(
        functools.partial(_attn_batch_kernel, bb=bb), "attn_batch", jax.ShapeDtypeStruct((b, l, X_WIDTH), BF16),
        [pl.BlockSpec((bb, l, X_WIDTH), blk), kv, kv], pl.BlockSpec((bb, l, X_WIDTH), blk), b // bb, (q, k, v))


def _post_mix_kernel(dn_ref, ml_ref, s5_ref, x_ref, k_ref, v_ref, w1_ref, w2_ref, w3_ref, gx_ref, wq_ref,
                     wo_ref, gf_ref, f1_ref, f2_ref, gl_ref, y_ref, *, final_norm):
    x1 = _mix_residual(x_ref[0], dn_ref[0], ml_ref[0], s5_ref[...], w1_ref, w2_ref, w3_ref)
    q = _dot(_rmsnorm_rows(x1, gx_ref[...]).astype(BF16), wq_ref[...]).astype(BF16)
    att = _attn_heads(q, k_ref, v_ref).astype(BF16)
    acc = _mlp_residual(x1 + _dot(att, wo_ref[...]), gf_ref, f1_ref, f2_ref)
    y_ref[0] = _rmsnorm_rows(acc, gl_ref[...]) if final_norm else acc


def _post_mix(dn, ml, s5, x, mem_k, mem_v, wts, prm, g_last, layer, tm, final_norm):
    b, l, _ = x.shape
    nt = l // tm
    row = _rows3(nt)
    mem = pl.BlockSpec((None, 1, N_MEM, X_WIDTH), lambda i: (layer, i // nt, 0, 0))
    vec = pl.BlockSpec((1, D_MODEL), lambda i: (0, 0))
    return _dense_call(
        functools.partial(_post_mix_kernel, final_norm=final_norm), "post_mix",
        jax.ShapeDtypeStruct((b, l, D_MODEL), F32),
        [pl.BlockSpec((1, tm, HEADS_W), row), pl.BlockSpec((1, tm, HEADS_W), row),
         pl.BlockSpec((tm, S5_WIDTH), lambda i: (i % nt, i // nt)), pl.BlockSpec((1, tm, D_MODEL), row), mem, mem]
        + _w_out_specs(layer)
        + [vec, _wspec(D_MODEL, X_WIDTH, layer), _wspec(X_WIDTH, D_MODEL, layer), vec,
           _wspec(D_MODEL, D_FF, layer), _wspec(D_FF, D_MODEL, layer), vec],
        pl.BlockSpec((1, tm, D_MODEL), row), b * nt,
        (dn, ml, s5, x, mem_k, mem_v, wts["w_out"], wts["w_out"], wts["w_out"], prm["norm_x"], wts["w_xq"],
         wts["w_xo"], prm["norm_ff"], wts["w_ff1"], wts["w_ff2"], g_last))


def _s5_kernel(u_ref, h0r_ref, h0i_ref, lre_ref, lim_ref, ldt_ref, bre_ref, bim_ref, cre_ref, cim_ref,
               d_ref, wg_ref, bg_ref, y_ref, hr_out, hi_out,
               xr_s, xi_s, st_r, st_i, coef_s, bbr_s, bbi_s, *io_s, tb, nb, lane_chunk, seq_major_io):
    step = pl.program_id(0)
    if seq_major_io:
        u_s, y_s = io_s
        for b in range(nb):
            for j in range(S5_WIDTH // LANES):
                u_s[j, pl.ds(b, tb, stride=nb), :] = u_ref[b, :, j * LANES:(j + 1) * LANES]
        u = jnp.concatenate([u_s[j] for j in range(S5_WIDTH // LANES)], axis=1)
    else:
        u = u_ref[...]

    @pl.when(step == 0)
    def _():
        lr = lre_ref[...]
        li = lim_ref[...]
        dt = jnp.exp(ldt_ref[...])
        mag = jnp.exp(lr * dt)
        ar = mag * jnp.cos(li * dt)
        ai = mag * jnp.sin(li * dt)
        inv = 1.0 / (lr * lr + li * li)
        coef_s[0:1, :] = ar
        coef_s[1:2, :] = ai
        cr = ((ar - 1.0) * lr + ai * li) * inv
        ci = (ai * lr - (ar - 1.0) * li) * inv
        bbr_s[...] = (cr * bre_ref[...] - ci * bim_ref[...]).astype(BF16)
        bbi_s[...] = (cr * bim_ref[...] + ci * bre_ref[...]).astype(BF16)
        st_r[...] = h0r_ref[...]
        st_i[...] = h0i_ref[...]

    ub = u.astype(BF16)
    xr_s[...] = _dot(ub, bbr_s[...])
    xi_s[...] = _dot(ub, bbi_s[...])

    for lc in range(S5_STATE // lane_chunk):
        sl = slice(lc * lane_chunk, (lc + 1) * lane_chunk)
        ar = jnp.broadcast_to(coef_s[0:1, sl], (nb, lane_chunk))
        ai = jnp.broadcast_to(coef_s[1:2, sl], (nb, lane_chunk))

        def body(t, carry, sl=sl, ar=ar, ai=ai):
            hr, hi = carry
            r0 = pl.multiple_of(t * nb, 8)
            nhr = ar * hr - ai * hi + xr_s[pl.ds(r0, nb), sl]
            nhi = ar * hi + ai * hr + xi_s[pl.ds(r0, nb), sl]
            xr_s[pl.ds(r0, nb), sl] = nhr
            xi_s[pl.ds(r0, nb), sl] = nhi
            return nhr, nhi

        hr, hi = lax.fori_loop(0, tb, body, (st_r[:, sl], st_i[:, sl]), unroll=min(tb, 8))
        st_r[:, sl] = hr
        st_i[:, sl] = hi

    y = (_dot(xr_s[...].astype(BF16), cre_ref[...]) - _dot(xi_s[...].astype(BF16), cim_ref[...])
         + d_ref[...] * u)
    zg = 0.5 * y * (1.0 + jnp.tanh(math.sqrt(2.0 / math.pi) * (y + 0.044715 * (y * y * y))))
    yv = zg * _sigmoid(_dot(zg.astype(BF16), wg_ref[...]) + bg_ref[...])
    if seq_major_io:
        for j in range(S5_WIDTH // LANES):
            y_s[j] = yv[:, j * LANES:(j + 1) * LANES]
        for b in range(nb):
            for j in range(S5_WIDTH // LANES):
                c0 = b * S5_WIDTH + j * LANES
                y_ref[:, c0:c0 + LANES] = y_s[j, pl.ds(b, tb, stride=nb), :].astype(BF16)
    else:
        y_ref[...] = yv.astype(BF16)

    @pl.when(step == pl.num_programs(0) - 1)
    def _():
        hr_out[...] = st_r[...]
        hi_out[...] = st_i[...]


def _s5(u, h0r, h0i, prm, tb, lane_chunk, seq_major_io):
    nb = h0r.shape[0]
    fix2 = lambda i: (0, 0)
    vec = pl.BlockSpec((1, S5_STATE), fix2)
    st = pl.BlockSpec((nb, S5_STATE), fix2)
    rows = tb * nb
    scratch = [pltpu.VMEM((rows, S5_STATE), F32), pltpu.VMEM((rows, S5_STATE), F32),
               pltpu.VMEM((nb, S5_STATE), F32), pltpu.VMEM((nb, S5_STATE), F32),
               pltpu.VMEM((8, S5_STATE), F32),
               pltpu.VMEM((S5_WIDTH, S5_STATE), BF16), pltpu.VMEM((S5_WIDTH, S5_STATE), BF16)]
    if seq_major_io:
        l = u.shape[1]
        u_spec = pl.BlockSpec((nb, tb, S5_WIDTH), lambda i: (0, i, S5_COL_BLOCK))
        y_shape = jax.ShapeDtypeStruct((l, nb * S5_WIDTH), BF16)
        y_spec = pl.BlockSpec((tb, nb * S5_WIDTH), lambda i: (i, 0))
        io_shape = (S5_WIDTH // LANES, rows, LANES)
        scratch += [pltpu.VMEM(io_shape, F32), pltpu.VMEM(io_shape, F32)]
    else:
        l = u.shape[0] // nb
        u_spec = pl.BlockSpec((rows, S5_WIDTH), lambda i: (i, 0))
        y_shape = jax.ShapeDtypeStruct((l * nb, S5_WIDTH), BF16)
        y_spec = pl.BlockSpec((rows, S5_WIDTH), lambda i: (i, 0))
    return pl.pallas_call(
        functools.partial(_s5_kernel, tb=tb, nb=nb, lane_chunk=lane_chunk, seq_major_io=seq_major_io),
        out_shape=(y_shape, jax.ShapeDtypeStruct((nb, S5_STATE), F32), jax.ShapeDtypeStruct((nb, S5_STATE), F32)),
        grid=(l // tb,),
        in_specs=[u_spec, st, st, vec, vec, vec,
                  pl.BlockSpec((S5_WIDTH, S5_STATE), fix2), pl.BlockSpec((S5_WIDTH, S5_STATE), fix2),
                  pl.BlockSpec((S5_STATE, S5_WIDTH), fix2), pl.BlockSpec((S5_STATE, S5_WIDTH), fix2),
                  pl.BlockSpec((1, S5_WIDTH), fix2), pl.BlockSpec((S5_WIDTH, S5_WIDTH), fix2),
                  pl.BlockSpec((1, S5_WIDTH), fix2)],
        out_specs=(y_spec, st, st),
        scratch_shapes=scratch,
        compiler_params=pltpu.CompilerParams(dimension_semantics=("arbitrary",),
                                             vmem_limit_bytes=VMEM_LIMIT),
        name="s5",
    )(u, h0r, h0i, prm["lre"], prm["lim"], prm["ldt"], prm["bre"], prm["bim"], prm["cre"],
      prm["cim"], prm["d"], prm["wg"], prm["bg"])


def _l2norm_pairs(x_chunks, bones, scale):
    n = len(x_chunks)
    x = jnp.concatenate(x_chunks, axis=0) if n > 1 else x_chunks[0]
    out = [[None] * N_PAIRS for _ in range(n)]
    for p_ in range(N_PAIRS):
        xp = x[:, p_ * LANES:(p_ + 1) * LANES]
        xn = xp * (lax.rsqrt(_seg_mean(xp * xp, bones) * HEAD_DIM + EPS) * scale)
        for c in range(n):
            out[c][p_] = xn[c * CHUNK:(c + 1) * CHUNK]
    return out


def _head_norm_store(vals, gates, normw, bones, act, store):
    n = len(vals)
    for p_ in range(N_PAIRS):
        v = jnp.concatenate([vals[c][p_] for c in range(n)], axis=0) if n > 1 else vals[0][p_]
        g = jnp.concatenate([gates[c][p_] for c in range(n)], axis=0) if n > 1 else gates[0][p_]
        y = v * lax.rsqrt(_seg_mean(v * v, bones) + EPS) * normw * act(g)
        for c in range(n):
            store(c, p_, y[c * CHUNK:(c + 1) * CHUNK].astype(BF16))


def _dn_local(chunks, prm, prec):
    n = len(chunks)
    bones = prm["bones"]
    incl, strict = _pair_masks()
    ltri, upair = _ltri(), _upair()
    units = [(c, p_) for c in range(n) for p_ in range(N_PAIRS)]

    g_cols = _mm_sel_l(ltri, jnp.concatenate([-jnp.exp(prm["alog_row"]) * _softplus(ch["zs"]) for ch in chunks],
                                             axis=1))
    g_rows = _mm_sel_r(jnp.concatenate([-jnp.exp(prm["rowalog"]) * _softplus(ch["rows"] + prm["rowadd"])
                                        for ch in chunks], axis=0), upair)
    gates = [(_sigmoid(ch["zs"]), g_cols[:, c * LANES:(c + 1) * LANES], g_rows[c * 16:(c + 1) * 16])
             for c, ch in enumerate(chunks)]

    qn = _l2norm_pairs([ch["q"] for ch in chunks], bones, HEAD_DIM ** -0.5)
    kn = _l2norm_pairs([ch["k"] for ch in chunks], bones, 1.0)

    beta = [_expand(gates[c][0], 2 * p_, 2 * p_ + 1) for c, p_ in units]
    g_col = [_expand(gates[c][1], 6 + 2 * p_, 7 + 2 * p_) for c, p_ in units]
    dec = [jnp.exp(jnp.where(incl, gc - gates[c][2][p_:p_ + 1, :], NEG)) for gc, (c, p_) in zip(g_col, units)]
    kb = [kn[c][p_] * b for b, (c, p_) in zip(beta, units)]
    sc = [_mm(jnp.concatenate([kb_, qn[c][p_]], axis=0), _x2b(kn[c][p_]), _NT)
          for kb_, (c, p_) in zip(kb, units)]
    a_in = [s[CHUNK:] * d for s, d in zip(sc, dec)]
    mbd = [_x2(jnp.where(strict, s[:CHUNK] * d, 0.0)) for s, d in zip(sc, dec)]
    tri_prec, newton, sol_prec = prec
    t_bd = _tri_inverse_units(mbd, tri_prec, newton)
    eg = [jnp.exp(gc) for gc in g_col]
    x2r = _x2b if sol_prec == 1 else _x2
    rhs = [jnp.concatenate([x2r(chunks[c]["v"][:, p_ * LANES:(p_ + 1) * LANES] * b), x2r(kb_ * e)], axis=1)
           for b, kb_, e, (c, p_) in zip(beta, kb, eg, units)]
    sol = [_mm(t, r, prec=sol_prec) for t, r in zip(t_bd, rhs)]
    out = [[None] * N_PAIRS for _ in range(n)]
    for i, (c, p_) in enumerate(units):
        g_last = _last_row(g_col[i])
        out[c][p_] = dict(u=_fold(sol[i][:, :LANES]), w=_fold(sol[i][:, LANES:]), a_in=a_in[i],
                          qg=qn[c][p_] * eg[i], kd=kn[c][p_] * jnp.exp(g_last - g_col[i]),
                          eg_last=jnp.exp(g_last))
    return out


def _dn_seq_step(zq_ref, zk_ref, zv_ref, zg_ref, zs_ref, rows_ref,
                 add_ref, alog_ref, rowadd_ref, rowalog_ref, nw_ref, o_ref, s_scr, *, bb, prec):
    chunks = [dict(q=zq_ref[b], k=zk_ref[b], v=zv_ref[b], zs=zs_ref[b] + add_ref[...], rows=rows_ref[b, 0])
              for b in range(bb)]

    prm = dict(bones=_block_ones(), alog_row=alog_ref[...], rowadd=rowadd_ref[...],
               rowalog=rowalog_ref[...])
    loc = _dn_local(chunks, prm, prec)
    bdm = _bd_mask()
    units = [(b, p_) for b in range(bb) for p_ in range(N_PAIRS)]
    s_prev = [s_scr[b, p_] for b, p_ in units]
    ws = [_mm(jnp.concatenate([loc[b][p_]["w"], loc[b][p_]["qg"]], axis=0), s)
          for s, (b, p_) in zip(s_prev, units)]
    v_new = [loc[b][p_]["u"] - w_[:CHUNK] for w_, (b, p_) in zip(ws, units)]
    o = [w_[CHUNK:] + _mm(loc[b][p_]["a_in"], _x2b(v)) for w_, v, (b, p_) in zip(ws, v_new, units)]
    upd = [_mm(loc[b][p_]["kd"].T, v) for v, (b, p_) in zip(v_new, units)]
    for i, (b, p_) in enumerate(units):
        s_scr[b, p_] = loc[b][p_]["eg_last"][0:1, :] * s_prev[i] + jnp.where(bdm, upd[i], 0.0)

    def store(b, p_, val):
        o_ref[b, :, p_ * LANES:(p_ + 1) * LANES] = val

    _head_norm_store([[o[b * N_PAIRS + p_] for p_ in range(N_PAIRS)] for b in range(bb)],
                     [[zg_ref[b, :, p_ * LANES:(p_ + 1) * LANES] for p_ in range(N_PAIRS)] for b in range(bb)],
                     nw_ref[...], prm["bones"], _silu, store)


def _ml_units(chunks, prm):
    n = len(chunks)
    incl, _ = _pair_masks()
    lo = _lane_lo((CHUNK, LANES))
    ltri, upair = _ltri(), _upair()
    ones2 = prm["bones"]
    bdm2 = jnp.concatenate([_bd_mask(), _bd_mask()], axis=1)
    units = [(c, p_) for c in range(n) for p_ in range(N_PAIRS)]
    pair = lambda x, p_: x[:, p_ * LANES:(p_ + 1) * LANES]

    rows = [ch["rows"] + prm["rowadd"] for ch in chunks]
    b_cols = _mm_sel_l(ltri, jnp.concatenate([-_softplus(-ch["zs"]) for ch in chunks], axis=1))
    b_rows = _mm_sel_r(jnp.concatenate([-_softplus(-r) for r in rows], axis=0), upair)
    comp = [dict(ig_c=ch["zs"], b_c=b_cols[:, c * LANES:(c + 1) * LANES], ig_r=rows[c],
                 b_r=b_rows[c * 16:(c + 1) * 16]) for c, ch in enumerate(chunks)]

    qp = [pair(chunks[c]["q"], p_) for c, p_ in units]
    kp = [pair(chunks[c]["k"], p_) * (HEAD_DIM ** -0.5) for c, p_ in units]
    vp = [pair(chunks[c]["v"], p_) for c, p_ in units]
    b_col = [_expand(comp[c]["b_c"], 18 + 2 * p_, 19 + 2 * p_) for c, p_ in units]
    ig_col = [_expand(comp[c]["ig_c"], 12 + 2 * p_, 13 + 2 * p_) for c, p_ in units]
    dl = [jnp.where(incl, bc - comp[c]["b_r"][6 + p_:7 + p_, :] + comp[c]["ig_r"][3 + p_:4 + p_, :], NEG)
          for bc, (c, p_) in zip(b_col, units)]
    mx = [jnp.where(lo, jnp.max(jnp.where(lo, d, NEG), axis=1, keepdims=True),
                    jnp.max(jnp.where(lo, NEG, d), axis=1, keepdims=True)) for d in dl]
    inter = [bc + chunks[c]["ms"][p_] for bc, (c, p_) in zip(b_col, units)]
    mt = [jnp.maximum(a, b) for a, b in zip(inter, mx)]
    s = [_mm(q_, _x2b(k_), _NT) * jnp.exp(d - m) for q_, k_, d, m in zip(qp, kp, dl, mt)]
    sv = [_mm(s_, jnp.concatenate([_x2b(v_), ones2], axis=1)) for s_, v_ in zip(s, vp)]
    cn_prev = [chunks[c]["cn"][p_] for c, p_ in units]
    qcn = [_mm(q_, cn) for q_, cn in zip(qp, cn_prev)]
    w_inter = [jnp.exp(a - m) for a, m in zip(inter, mt)]
    h = [(wi * qc[:, :LANES] + sv_[:, :LANES])
         / jnp.maximum(jnp.abs(wi * qc[:, LANES:] + sv_[:, LANES:]), jnp.exp(-m))
         for wi, qc, sv_, m in zip(w_inter, qcn, sv, mt)]
    m_end = [_last_row(m) for m in mt]
    b_last = [_last_row(bc) for bc in b_col]
    a_end = [jnp.exp(bl + chunks[c]["ms"][p_] - me) for bl, me, (c, p_) in zip(b_last, m_end, units)]
    kw_t = [(k_ * jnp.exp(bl - bc + ig - me)).T for k_, bl, bc, ig, me in zip(kp, b_last, b_col, ig_col, m_end)]
    upd = [_mm(k_, jnp.concatenate([v_, jnp.ones((CHUNK, LANES), F32)], axis=1)) for k_, v_ in zip(kw_t, vp)]
    out = [[None] * N_PAIRS for _ in range(n)]
    for i, (c, p_) in enumerate(units):
        a_row = a_end[i][0:1, :]
        new_state = jnp.concatenate([a_row, a_row], axis=1) * cn_prev[i] + jnp.where(bdm2, upd[i], 0.0)
        out[c][p_] = (h[i], new_state, m_end[i][0:1, :])
    return out


def _ml_seq_step(zq_ref, zk_ref, zv_ref, zo_ref, zs_ref, rows_ref, add_ref, rowadd_ref, nw_ref, o_ref,
                 cn_scr, m_scr, *, bb):
    prm = dict(bones=_block_ones(), rowadd=rowadd_ref[...])
    chunks = []
    for b in range(bb):
        chunks.append(dict(q=zq_ref[b], k=zk_ref[b], v=zv_ref[b], zs=zs_ref[b] + add_ref[...],
                           rows=rows_ref[b, 0],
                           ms=[m_scr[b, p_:p_ + 1, :] for p_ in range(N_PAIRS)],
                           cn=[cn_scr[b, p_] for p_ in range(N_PAIRS)]))
    res = _ml_units(chunks, prm)
    for b in range(bb):
        for p_ in range(N_PAIRS):
            cn_scr[b, p_] = res[b][p_][1]
            m_scr[b, p_:p_ + 1, :] = res[b][p_][2]

    def store(b, p_, val):
        o_ref[b, :, p_ * LANES:(p_ + 1) * LANES] = val

    _head_norm_store([[res[b][p_][0] for p_ in range(N_PAIRS)] for b in range(bb)],
                     [[zo_ref[b, :, p_ * LANES:(p_ + 1) * LANES] for p_ in range(N_PAIRS)] for b in range(bb)],
                     nw_ref[...], prm["bones"], _sigmoid, store)


def _mixers_seq_kernel(dq_ref, dk_ref, dv_ref, dg_ref, lq_ref, lk_ref, lv_ref, lo_ref, zs_ref, rows_ref,
                       s0_ref, cn0_ref, m0_ref,
                       add_ref, alog_ref, rowadd_ref, rowalog_ref, dnw_ref, mlw_ref,
                       dn_ref, ml_ref, s_out_ref, cn_out_ref, m_out_ref,
                       s_scr, cn_scr, m_scr, *, bb, prec):
    tstep = pl.program_id(1)

    @pl.when(tstep == 0)
    def _():
        s_scr[...] = s0_ref[...]
        cn_scr[...] = cn0_ref[...]
        m_scr[...] = m0_ref[...]

    _dn_seq_step(dq_ref, dk_ref, dv_ref, dg_ref, zs_ref, rows_ref, add_ref, alog_ref, rowadd_ref,
                 rowalog_ref, dnw_ref, dn_ref, s_scr, bb=bb, prec=prec)
    _ml_seq_step(lq_ref, lk_ref, lv_ref, lo_ref, zs_ref, rows_ref, add_ref, rowadd_ref, mlw_ref, ml_ref,
                 cn_scr, m_scr, bb=bb)

    @pl.when(tstep == pl.num_programs(1) - 1)
    def _():
        s_out_ref[...] = s_scr[...]
        cn_out_ref[...] = cn_scr[...]
        m_out_ref[...] = m_scr[...]


def _mixers_seq(z3, zc, rows4, s0, cn0, m0, prm, bb, prec):
    batch, seq_len, _ = z3.shape
    fix2 = lambda g, i: (0, 0)
    zspec = lambda width, col: pl.BlockSpec((bb, CHUNK, width), lambda g, i: (g, i, col))
    per_seq = lambda *shape: pl.BlockSpec((bb,) + shape, lambda g, i: (g,) + (0,) * len(shape))
    s_spec, cn_spec, m_spec = per_seq(N_PAIRS, LANES, LANES), per_seq(N_PAIRS, LANES, 2 * LANES), per_seq(8, LANES)
    small = lambda rows: pl.BlockSpec((rows, LANES), fix2)
    in_specs = ([zspec(HEADS_W, c) for c in range(8)] + [zspec(LANES, GATE_COL_BLOCK)]
                + [pl.BlockSpec((bb, 1, 16, LANES), lambda g, i: (g, i, 0, 0)), s_spec, cn_spec, m_spec,
                   small(1), small(1), small(16), small(16), small(1), small(1)])
    out_blk = pl.BlockSpec((bb, CHUNK, HEADS_W), lambda g, i: (g, i, 0))
    act = jax.ShapeDtypeStruct((batch, seq_len, HEADS_W), BF16)
    return pl.pallas_call(
        functools.partial(_mixers_seq_kernel, bb=bb, prec=prec),
        out_shape=(act, act, jax.ShapeDtypeStruct(s0.shape, F32), jax.ShapeDtypeStruct(cn0.shape, F32),
                   jax.ShapeDtypeStruct(m0.shape, F32)),
        grid=(batch // bb, seq_len // CHUNK),
        in_specs=in_specs,
        out_specs=(out_blk, out_blk, s_spec, cn_spec, m_spec),
        scratch_shapes=[pltpu.VMEM((bb, N_PAIRS, LANES, LANES), F32),
                        pltpu.VMEM((bb, N_PAIRS, LANES, 2 * LANES), F32), pltpu.VMEM((bb, 8, LANES), F32)],
        compiler_params=pltpu.CompilerParams(dimension_semantics=("parallel", "arbitrary"),
                                             vmem_limit_bytes=VMEM_LIMIT),
        name="mixers_seq",
    )(zc, zc, zc, *([z3] * 6), rows4, s0, cn0, m0, prm["add_row"], prm["alog_row"],
      prm["rowadd"], prm["rowalog"], prm["dn_normw"], prm["ml_normw"])


def _lane_mixers_kernel(zq, zk, zv, zg, lq, lk, lv, lo, zs, cpq, cpk, cpv, cwq, cwk, cwv, bias, alog, dnw, mlw,
                        s0, c0, n0, m0, *rest, n_tok, n_prev):
    if n_prev:
        s_prev, c_prev, dn_o, ml_o, s_st, c_st, n_o, m_o, k_s, q_s = rest
        s_st[0:n_prev] = s_prev[...]
        c_st[0:n_prev] = c_prev[...]
    else:
        dn_o, ml_o, s_st, c_st, n_o, m_o, k_s, q_s = rest
    s_o, c_o = s_st.at[n_prev], c_st.at[n_prev]
    h = pl.program_id(0)
    nb = s_o.shape[-1]
    zrow = lambda t, r: zs[t, pl.ds(r, 1), :]
    prow = lambda ref, r: ref[pl.ds(r, 1), :]
    bcast = lambda ref, dk: jnp.broadcast_to(ref[pl.ds(dk, 1), :], (HEAD_DIM, nb))
    zeros = jnp.zeros((HEAD_DIM, nb), F32)

    def conv(z_ref, cp_ref, cw_ref):
        e = [cp_ref[j] for j in range(CONV_W - 1)] + [z_ref[t] for t in range(n_tok)]
        outs = []
        for t in range(n_tok):
            acc = e[t] * cw_ref[0]
            for j in range(1, CONV_W):
                acc = acc + e[t + j] * cw_ref[j]
            outs.append(_silu(acc))
        return outs

    def l2n(x, scale):
        return x * (lax.rsqrt(jnp.sum(x * x, axis=0, keepdims=True) + EPS) * scale)

    def head_norm(x, w):
        return x * lax.rsqrt(jnp.mean(x * x, axis=0, keepdims=True) + EPS) * w

    qc, kc, vc = conv(zq, cpq, cwq), conv(zk, cpk, cwk), conv(zv, cpv, cwv)
    s_o[...] = s0[...]
    for t in range(n_tok):
        k_s[...] = l2n(kc[t], 1.0)
        q_s[...] = l2n(qc[t], HEAD_DIM ** -0.5)
        beta = _sigmoid(zrow(t, h))
        a = jnp.exp(-jnp.exp(prow(alog, N_HEADS + h))
                    * _softplus(zrow(t, N_HEADS + h) + prow(bias, N_HEADS + h)))
        ks = lax.fori_loop(0, HEAD_DIM, lambda dk, acc: acc + bcast(k_s, dk) * s_o[dk], zeros, unroll=8)
        delta = beta * (vc[t] - a * ks)

        def dn_update(dk, acc, a=a, delta=delta):
            s_new = a * s_o[dk] + bcast(k_s, dk) * delta
            s_o[dk] = s_new
            return acc + bcast(q_s, dk) * s_new

        o = lax.fori_loop(0, HEAD_DIM, dn_update, zeros, unroll=8)
        dn_o[t] = (head_norm(o, dnw[...]) * _silu(zg[t])).astype(BF16)

    c_o[...] = c0[...]
    n = n0[...]
    m = m0[...]
    for t in range(n_tok):
        q, k, v = lq[t], lk[t] * (HEAD_DIM ** -0.5), lv[t]
        ig = zrow(t, 2 * N_HEADS + h) + prow(bias, 2 * N_HEADS + h)
        lsf = -_softplus(-(zrow(t, 3 * N_HEADS + h) + prow(bias, 3 * N_HEADS + h)))
        m_new = jnp.maximum(lsf + m, ig)
        fp = jnp.exp(lsf + m - m_new)
        ip = jnp.exp(ig - m_new)
        k_s[...] = k * ip
        q_s[...] = q

        def ml_update(dk, acc, fp=fp, v=v):
            c_new = fp * c_o[dk] + bcast(k_s, dk) * v
            c_o[dk] = c_new
            return acc + bcast(q_s, dk) * c_new

        num = lax.fori_loop(0, HEAD_DIM, ml_update, zeros, unroll=8)
        n = fp * n + ip * k
        den = jnp.sum(q * n, axis=0, keepdims=True)
        hh = num / jnp.maximum(jnp.abs(den), jnp.exp(-m_new))
        m = m_new
        ml_o[t] = (head_norm(hh, mlw[...]) * _sigmoid(lo[t])).astype(BF16)
    n_o[...] = n
    m_o[...] = m


def _lane_mixers(z_t, conv_t, s0, c0, n0, m0, layer, prm, prev):
    n_tok, _, nb = z_t.shape
    hd = HEAD_DIM
    n_prev = 0 if prev is None else prev[0].shape[0]
    assert n_prev == layer
    zblk = lambda first: pl.BlockSpec((n_tok, hd, nb), lambda h: (0, first + h, 0))
    cblk = lambda taps, seg: pl.BlockSpec((taps, hd, nb), lambda h: (0, seg * N_HEADS + h, 0))
    wblk = lambda seg: pl.BlockSpec((CONV_W, hd, nb), lambda h: (0, seg * N_HEADS + h, 0))
    whole = lambda shape: pl.BlockSpec(shape, lambda h: (0,) * len(shape))
    mat_in = pl.BlockSpec((None, None, hd, hd, nb), lambda h: (layer, h, 0, 0, 0))
    mat_out = pl.BlockSpec((n_prev + 1, None, hd, hd, nb), lambda h: (0, h, 0, 0, 0))
    prev_specs = [pl.BlockSpec((n_prev, None, hd, hd, nb), lambda h: (0, h, 0, 0, 0))] * 2 if n_prev else []
    small_rows = 32
    in_specs = [zblk(seg * N_HEADS) for seg in range(8)] + [
        pl.BlockSpec((n_tok, small_rows, nb), lambda h: (0, (Z_COLS - LANES) // small_rows, 0)),
        cblk(CONV_W - 1, 0), cblk(CONV_W - 1, 1), cblk(CONV_W - 1, 2), wblk(0), wblk(1), wblk(2),
        whole((small_rows, nb)), whole((small_rows, nb)), whole((hd, nb)), whole((hd, nb)),
        mat_in, mat_in,
        pl.BlockSpec((None, None, hd, nb), lambda h: (layer, h, 0, 0)),
        pl.BlockSpec((None, None, 1, nb), lambda h: (layer, h, 0, 0))] + prev_specs
    out_blk = pl.BlockSpec((n_tok, hd, nb), lambda h: (0, h, 0))
    act = jax.ShapeDtypeStruct((n_tok, HEADS_W, nb), BF16)
    mat = jax.ShapeDtypeStruct((n_prev + 1, N_HEADS, hd, hd, nb), F32)
    return pl.pallas_call(
        functools.partial(_lane_mixers_kernel, n_tok=n_tok, n_prev=n_prev),
        out_shape=(act, act, mat, mat, jax.ShapeDtypeStruct((N_HEADS, hd, nb), F32),
                   jax.ShapeDtypeStruct((N_HEADS, 1, nb), F32)),
        grid=(N_HEADS,),
        in_specs=in_specs,
        out_specs=(out_blk, out_blk, mat_out, mat_out,
                   pl.BlockSpec((None, hd, nb), lambda h: (h, 0, 0)), pl.BlockSpec((None, 1, nb), lambda h: (h, 0, 0))),
        scratch_shapes=[pltpu.VMEM((hd, nb), F32), pltpu.VMEM((hd, nb), F32)],
        compiler_params=pltpu.CompilerParams(dimension_semantics=("parallel",), vmem_limit_bytes=VMEM_LIMIT),
        name="lane_mixers",
    )(*([z_t] * 9), conv_t, conv_t, conv_t, prm["conv_wb"], prm["conv_wb"], prm["conv_wb"], prm["bias_b"],
      prm["alog_b"], prm["dn_normb"], prm["ml_normb"], s0, c0, n0, m0, *(prev or ()))


def _pack_w_in(w_in):
    offs = np.concatenate([[0], np.cumsum(IN_SIZES)])
    seg = lambda i: w_in[..., int(offs[i]):int(offs[i + 1])].astype(BF16)
    small = jnp.concatenate([seg(3), seg(4), seg(9), seg(10)], axis=-1)
    small = jnp.pad(small, ((0, 0),) * (w_in.ndim - 1) + ((0, LANES - 4 * N_HEADS),))
    cols = [seg(0), seg(1), seg(2), seg(5), seg(6), seg(7), seg(8), seg(11), seg(12), small]
    return jnp.concatenate(cols, axis=-1)


def _pair_rows(v6):
    return jnp.repeat(v6.reshape(N_PAIRS, 2), HEAD_DIM, axis=1).reshape(N_PAIRS, LANES)


def _layer_params(l, a):
    f = lambda name: a[name][l].astype(F32)
    zeros6 = jnp.zeros((N_HEADS,), F32)
    add_row = jnp.concatenate([zeros6, f("dn_dt_bias"), f("ml_i_bias"), f("ml_f_bias")])
    add_row = jnp.pad(add_row, (0, LANES - 4 * N_HEADS))[None, :]
    alog_row = jnp.pad(jnp.concatenate([zeros6, f("dn_a_log")]), (0, LANES - 2 * N_HEADS))[None, :]
    rowadd = jnp.concatenate([_pair_rows(f("dn_dt_bias")), _pair_rows(f("ml_i_bias")),
                              _pair_rows(f("ml_f_bias")), jnp.zeros((7, LANES), F32)], axis=0)
    rowalog = jnp.concatenate([_pair_rows(f("dn_a_log")), jnp.zeros((13, LANES), F32)], axis=0)
    eye_g = jnp.eye(S5_GROUPS, dtype=F32)
    bd_in = lambda b: jnp.einsum("gpc,gh->gchp", b, eye_g).reshape(S5_WIDTH, S5_STATE)
    bd_out = lambda c: jnp.einsum("gcp,gh->gphc", c, eye_g).reshape(S5_STATE, S5_WIDTH).astype(BF16)
    return dict(
        layer=l, norm_mix=f("norm_mix")[None, :],
        add_row=add_row, alog_row=alog_row, rowadd=rowadd, rowalog=rowalog,
        conv_w8=jnp.pad(f("dn_conv_w"), ((0, 8 - CONV_W), (0, 0))),
        dn_normw=jnp.tile(f("dn_norm"), 2)[None, :], ml_normw=jnp.tile(f("ml_norm"), 2)[None, :],
        s5=dict(lre=f("s5_lam_re").reshape(1, S5_STATE), lim=f("s5_lam_im").reshape(1, S5_STATE),
                ldt=jnp.repeat(f("s5_log_dt"), S5_P)[None, :],
                bre=bd_in(f("s5_b_re")), bim=bd_in(f("s5_b_im")),
                cre=bd_out(f("s5_c_re")), cim=bd_out(f("s5_c_im")),
                d=f("s5_d").reshape(1, S5_WIDTH), wg=f("s5_w_glu").astype(BF16),
                bg=f("s5_b_glu")[None, :]),
        norm_x=f("norm_x")[None, :], norm_ff=f("norm_ff")[None, :],
    )


def _shared_weights(a):
    cast = lambda name: a[name].astype(BF16)
    return dict(w_in=_pack_w_in(a["w_in"]), w_out=cast("w_out"), w_xq=cast("w_xq"), w_xo=cast("w_xo"),
                w_ff1=cast("w_ff1"), w_ff2=cast("w_ff2"), w_xk=cast("w_xk"), w_xv=cast("w_xv"))


def _row_form(z, n_chunks):
    zs = z[..., Z_COLS - LANES:Z_COLS - LANES + 4 * N_HEADS].reshape(n_chunks, CHUNK, 4, N_PAIRS, 2)
    r = jnp.transpose(zs[:, :, 1:], (0, 2, 3, 4, 1)).reshape(n_chunks, 9, LANES)
    return jnp.pad(r, ((0, 0), (0, 7), (0, 0)))


def _un_bd_pairs(s):
    n = s.shape[0]
    a = s[:, :, :HEAD_DIM, :HEAD_DIM]
    b = s[:, :, HEAD_DIM:, HEAD_DIM:]
    return jnp.stack([a, b], axis=2).reshape(n, N_HEADS, HEAD_DIM, HEAD_DIM)


def _unpack_m(mp):
    return mp[:, :N_PAIRS, ::HEAD_DIM].reshape(mp.shape[0], N_HEADS)


def _tile(n, pref):
    tile = min(n, pref)
    assert n % tile == 0, (n, pref)
    return tile


def _layer_prompt(x, mem_k, mem_v, prm, wts, final_g):
    batch, seq_len, _ = x.shape
    layer = prm["layer"]
    z3, zc = _norm_matmul_conv(x, prm["norm_mix"], wts["w_in"], jnp.zeros((batch, 8, CONV_CH), F32),
                               prm["conv_w8"], layer, _tile(seq_len, CONV_ROW_TILE))
    rows4 = _row_form(z3, batch * seq_len // CHUNK).reshape(batch, seq_len // CHUNK, 16, LANES)
    new_conv = z3[:, seq_len - (CONV_W - 1):, :CONV_CH]

    dn, ml, s_new, cn_new, m_new = _mixers_seq(
        z3, zc, rows4, jnp.zeros((batch, N_PAIRS, LANES, LANES), F32),
        jnp.zeros((batch, N_PAIRS, LANES, 2 * LANES), F32), jnp.zeros((batch, 8, LANES), F32), prm,
        _tile(batch, SEQ_BATCH), SEQ_SOLVE)
    zero_h = jnp.zeros((batch, S5_STATE), F32)
    s5o, hr, hi = _s5(z3, zero_h, zero_h, prm["s5"], _tile(seq_len, S5_STEPS), S5_STATE, True)

    g_fin = final_g if final_g is not None else prm["norm_ff"]
    x3 = _post_mix(dn, ml, s5o, x, mem_k, mem_v, wts, prm, g_fin, layer, _tile(seq_len, POST_TILE),
                   final_g is not None)
    st = (new_conv, _un_bd_pairs(s_new), _un_bd_pairs(cn_new[..., :LANES]),
          _un_bd_pairs(cn_new[..., LANES:])[..., 0], _unpack_m(m_new),
          hr.reshape(batch, S5_GROUPS, S5_P), hi.reshape(batch, S5_GROUPS, S5_P))
    return x3, st


def _layer_sample(x, cache_k, cache_v, conv_buf, dn_s, ml_c, ml_n, ml_m, ssm_re, ssm_im, prm, wts, batch,
                  seq_len, final_g, prev_stacks):
    t = batch * seq_len
    layer = prm["layer"]
    z = _norm_matmul(x, prm["norm_mix"], wts["w_in"], layer, t)
    z3 = z.reshape(batch, seq_len, Z_COLS)
    new_conv = jnp.concatenate([conv_buf, z3[:, :, :CONV_CH]], axis=1)[:, seq_len:]
    lanes = lambda v: jnp.broadcast_to(v[..., None], v.shape + (batch,))
    mix_prm = dict(conv_wb=lanes(prm["conv_w8"][:CONV_W]), bias_b=lanes(prm["add_row"][0, :32]),
                   alog_b=lanes(prm["alog_row"][0, :32]), dn_normb=lanes(prm["dn_normw"][0, :HEAD_DIM]),
                   ml_normb=lanes(prm["ml_normw"][0, :HEAD_DIM]))
    dn_t, ml_t, s_stack, c_stack, n_new, m_new = _lane_mixers(
        jnp.transpose(z3, (1, 2, 0)), jnp.transpose(conv_buf, (1, 2, 0)), dn_s, ml_c, ml_n, ml_m, layer, mix_prm,
        prev_stacks)
    to_rows = lambda a: jnp.transpose(a, (2, 0, 1)).reshape(1, t, HEADS_W)
    dn, ml = to_rows(dn_t), to_rows(ml_t)

    u_tm = jnp.transpose(z3[:, :, 3072:3072 + S5_WIDTH], (1, 0, 2))
    y_tm, hr, hi = _s5(u_tm.reshape(t, S5_WIDTH), ssm_re.reshape(batch, S5_STATE),
                       ssm_im.reshape(batch, S5_STATE), prm["s5"], seq_len, LANES, False)
    s5o = jnp.transpose(y_tm.reshape(seq_len, batch, S5_WIDTH), (1, 0, 2)).reshape(t, S5_WIDTH)

    x1, q = _out_proj(dn, ml, s5o, x, wts, prm, layer, t)
    q3 = jnp.pad(q.reshape(batch, seq_len, X_WIDTH), ((0, 0), (0, Q_ROWS - seq_len), (0, 0)))
    att = _attn_batch(q3, cache_k, cache_v, layer, _tile(batch, ATTN_BATCH))[:, :seq_len].reshape(1, t, X_WIDTH)
    g_fin = final_g if final_g is not None else prm["norm_ff"]
    x3 = _ffn(x1, att, wts, prm, g_fin, layer, t, final_g is not None)
    st = (new_conv, jnp.transpose(n_new, (2, 0, 1)), jnp.transpose(m_new[:, 0, :], (1, 0)),
          hr.reshape(batch, S5_GROUPS, S5_P), hi.reshape(batch, S5_GROUPS, S5_P))
    return x3, st, (s_stack, c_stack)


def kernel(x_prompt, x_sample, mem_prompt, state_dn_conv, state_dn_s, state_ml_c, state_ml_n, state_ml_m, state_ssm_re, state_ssm_im, cache_mem_k, cache_mem_v, norm_mix, w_in, dn_conv_w, dn_a_log, dn_dt_bias, dn_norm, ml_i_bias, ml_f_bias, ml_norm, s5_lam_re, s5_lam_im, s5_log_dt, s5_b_re, s5_b_im, s5_c_re, s5_c_im, s5_d, s5_w_glu, s5_b_glu, w_out, norm_x, norm_mem, w_xq, w_xk, w_xv, w_xo, norm_ff, w_ff1, w_ff2, norm_final):
    a = dict(norm_mix=norm_mix, w_in=w_in, dn_conv_w=dn_conv_w, dn_a_log=dn_a_log, dn_dt_bias=dn_dt_bias,
             dn_norm=dn_norm, ml_i_bias=ml_i_bias, ml_f_bias=ml_f_bias, ml_norm=ml_norm,
             s5_lam_re=s5_lam_re, s5_lam_im=s5_lam_im, s5_log_dt=s5_log_dt, s5_b_re=s5_b_re,
             s5_b_im=s5_b_im, s5_c_re=s5_c_re, s5_c_im=s5_c_im, s5_d=s5_d, s5_w_glu=s5_w_glu,
             s5_b_glu=s5_b_glu, w_out=w_out, norm_x=norm_x, w_xq=w_xq, w_xo=w_xo, norm_ff=norm_ff,
             w_ff1=w_ff1, w_ff2=w_ff2)
    prms = [_layer_params(l, a) for l in range(DEPTH)]
    wts = _shared_weights(dict(a, w_xk=w_xk, w_xv=w_xv))
    g_final = norm_final.astype(F32)[None, :]

    bp, lp, _ = x_prompt.shape
    bs, ls, _ = x_sample.shape

    xp = x_prompt
    mk, mk_t, mv, mv_t = _mem_proj(mem_prompt, norm_mem.astype(F32), wts["w_xk"], wts["w_xv"])
    p_st = []
    for l in range(DEPTH):
        xp, st = _layer_prompt(xp, mk, mv, prms[l], wts, g_final if l == DEPTH - 1 else None)
        p_st.append(st)
    y_prompt = xp
    p_mem = (mk_t.reshape(DEPTH, bp, N_MEM, X_HEADS, X_HD), mv_t.reshape(DEPTH, bp, N_MEM, X_HEADS, X_HD))

    xs = x_sample.reshape(1, bs * ls, D_MODEL)
    cache_k = cache_mem_k.reshape(DEPTH, bs, N_MEM * X_HEADS, X_HD)
    cache_v = cache_mem_v.reshape(DEPTH, bs, N_MEM * X_HEADS, X_HD)
    dn_s_t = jnp.transpose(state_dn_s, (0, 2, 3, 4, 1))
    ml_c_t = jnp.transpose(state_ml_c, (0, 2, 3, 4, 1))
    ml_n_t = jnp.transpose(state_ml_n, (0, 2, 3, 1))
    ml_m_t = jnp.transpose(state_ml_m, (0, 2, 1))[:, :, None, :]
    s_st = []
    stacks = None
    for l in range(DEPTH):
        xs, st, stacks = _layer_sample(xs, cache_k, cache_v, state_dn_conv[l], dn_s_t, ml_c_t, ml_n_t, ml_m_t,
                                       state_ssm_re[l], state_ssm_im[l], prms[l], wts, bs, ls,
                                       g_final if l == DEPTH - 1 else None, stacks)
        s_st.append(st)
    y_sample = xs.reshape(bs, ls, D_MODEL)
    s_mats = tuple(jnp.transpose(m, (0, 4, 1, 2, 3)) for m in stacks)

    stack = lambda sts, i: jnp.stack([s[i] for s in sts])
    return ((y_prompt, y_sample) + tuple(stack(p_st, i) for i in range(7)) + p_mem
            + (stack(s_st, 0),) + s_mats + tuple(stack(s_st, i) for i in range(1, 5)))
```

```python
import functools
import math

import numpy as np
import jax
import jax.numpy as jnp
from jax import lax
from jax.experimental import pallas as pl
from jax.experimental.pallas import tpu as pltpu

F32 = jnp.float32
BF16 = jnp.bfloat16

D_MODEL = 1024
DEPTH = 2
N_HEADS = 6
HEAD_DIM = 64
N_PAIRS = N_HEADS // 2
HEADS_W = N_HEADS * HEAD_DIM
CONV_W = 4
CONV_CH = 3 * HEADS_W
S5_WIDTH = 256
S5_GROUPS = 16
S5_P = 64
S5_STATE = S5_GROUPS * S5_P
N_MEM = 256
X_HEADS = 4
X_HD = 128
X_WIDTH = X_HEADS * X_HD
D_FF = 4 * D_MODEL
EPS = 1e-6
IN_SIZES = (384, 384, 384, 6, 6, 384, 384, 384, 384, 6, 6, 384, 256)

LANES = 128
CHUNK = 64
Q_ROWS = 8
NEG = -1e30
Z_COLS = 3456
S5_COL_BLOCK = 3072 // S5_WIDTH
GATE_COL_BLOCK = (Z_COLS - LANES) // LANES
VMEM_LIMIT = 52 * 1024 * 1024

ROW_TILE = 1024
CONV_ROW_TILE = 512
POST_TILE = 512
SEQ_BATCH = 8
S5_STEPS = 64
ATTN_BATCH = 8
FF_CHUNK = 1024
TRI_BASE = 8
SEQ_SOLVE = (1, True, 1)

_NN = (((1,), (0,)), ((), ()))
_NT = (((1,), (1,)), ((), ()))


def _dot(a, b, dims=_NN):
    return lax.dot_general(a, b, dims, preferred_element_type=F32)


def _split(a, n):
    out = []
    r = a
    for i in range(n):
        h = r.astype(BF16)
        out.append(h)
        if i + 1 < n:
            r = r - h.astype(F32)
    return out


def _mm(a, b, dims=_NN, prec=1):
    if prec == 1:
        return _dot(a.astype(BF16), b.astype(BF16), dims)
    return _mmp(_parts(a, prec), _parts(b, prec), dims)


def _parts(a, prec):
    return tuple(_split(a, 1 if prec == 1 else 2))


def _mmp(ap, bp, dims=_NN):
    acc = None
    if len(bp) > 1:
        acc = _dot(ap[0], bp[1], dims)
    if len(ap) > 1:
        cross = _dot(ap[1], bp[0], dims)
        acc = cross if acc is None else acc + cross
    lead = _dot(ap[0], bp[0], dims)
    return lead if acc is None else acc + lead


def _mm_sel_r(a, sel, n=3):
    parts = _split(a, n)
    acc = _dot(parts[-1], sel)
    for p_ in parts[-2::-1]:
        acc = acc + _dot(p_, sel)
    return acc


def _mm_sel_l(sel, a, n=3):
    parts = _split(a, n)
    acc = _dot(sel, parts[-1])
    for p_ in parts[-2::-1]:
        acc = acc + _dot(sel, p_)
    return acc


def _sigmoid(x):
    return 1.0 / (1.0 + jnp.exp(-x))


def _silu(x):
    return x * _sigmoid(x)


def _softplus(x):
    return jnp.maximum(x, 0.0) + jnp.log(1.0 + jnp.exp(-jnp.abs(x)))


def _rmsnorm_rows(x, g):
    return x * lax.rsqrt(jnp.mean(x * x, axis=-1, keepdims=True) + EPS) * g


def _iota(shape, dim):
    return lax.broadcasted_iota(jnp.int32, shape, dim)


def _idiv(x, n):
    return x >> (n.bit_length() - 1)


def _imod(x, n):
    return x & (n - 1)


def _lane_lo(shape):
    return _iota(shape, 1) < HEAD_DIM


def _x2(x):
    lo = _lane_lo(x.shape)
    return jnp.concatenate([jnp.where(lo, x, 0.0), jnp.where(lo, 0.0, x)], axis=0)


def _x2b(x):
    return _x2(x.astype(BF16))


def _fold(y):
    r = y.shape[0] // 2
    return y[:r] + y[r:]


def _expand(a, c0, c1):
    r = a.shape[0]
    return jnp.where(_lane_lo((r, LANES)), a[:, c0:c0 + 1], a[:, c1:c1 + 1])


def _bd_mask():
    return (_iota((LANES, LANES), 0) < HEAD_DIM) == (_iota((LANES, LANES), 1) < HEAD_DIM)


def _block_ones():
    return jnp.where(_bd_mask(), 1.0, 0.0).astype(BF16)


def _seg_mean(x, bones):
    return _mm_sel_r(x, bones, n=1) * (1.0 / HEAD_DIM)


def _ltri():
    return jnp.where(_iota((CHUNK, CHUNK), 1) <= _iota((CHUNK, CHUNK), 0), 1.0, 0.0).astype(BF16)


def _upair():
    a = _iota((LANES, LANES), 0)
    b = _iota((LANES, LANES), 1)
    return jnp.where(_bd_mask() & (a <= b), 1.0, 0.0).astype(BF16)


def _pair_masks():
    i = _iota((CHUNK, LANES), 0)
    j = _imod(_iota((CHUNK, LANES), 1), HEAD_DIM)
    return j <= i, j < i


def _blk_eq(n, b):
    return _idiv(_iota((n, n), 0), b) == _idiv(_iota((n, n), 1), b)


def _last_row(a):
    return jnp.broadcast_to(a[CHUNK - 1:CHUNK, :], a.shape)


def _tri_inverse_units(mbds, prec, newton):
    n = LANES
    eye = jnp.where(_iota((n, n), 0) == _iota((n, n), 1), 1.0, 0.0)
    n_sq = TRI_BASE.bit_length() - 2
    blk = _blk_eq(n, TRI_BASE)
    m0 = [jnp.where(blk, m, 0.0) for m in mbds]
    ts = [eye - m for m in m0]
    ps = [_mm(m, m, prec=prec) for m in m0]
    for it in range(n_sq):
        if it == n_sq - 1:
            ts = [t + _mm(t, p_, prec=prec) for t, p_ in zip(ts, ps)]
        else:
            both = [_mm(jnp.concatenate([t, p_], axis=0), p_, prec=prec) for t, p_ in zip(ts, ps)]
            ts = [t + b_[:n] for t, b_ in zip(ts, both)]
            ps = [b_[n:] for b_ in both]
    b = TRI_BASE
    while b < CHUNK:
        outer = _blk_eq(n, 2 * b) & jnp.logical_not(_blk_eq(n, b))
        tp = [_parts(t, prec) for t in ts]
        xs = [_mmp(_parts(jnp.where(outer, m, 0.0), prec), t_) for m, t_ in zip(mbds, tp)]
        ts = [t - _mmp(t_, _parts(x, prec)) for t, t_, x in zip(ts, tp, xs)]
        b *= 2
    if newton:
        res = [eye - (t + _mm(m, t, prec=3)) for m, t in zip(mbds, ts)]
        ts = [t + _mm(t, r) for t, r in zip(ts, res)]
    return ts


def _wspec(rows, cols, layer, row_block=0):
    return pl.BlockSpec((None, rows, cols), lambda i: (layer, row_block, 0), pipeline_mode=pl.Buffered(1))


def _rows3(nt):
    return lambda i: (i // nt, i % nt, 0)


def _dense_call(kernel_fn, name, out_shape, in_specs, out_specs, steps, args):
    return pl.pallas_call(
        kernel_fn, out_shape=out_shape, grid=(steps,), in_specs=in_specs, out_specs=out_specs,
        compiler_params=pltpu.CompilerParams(dimension_semantics=("parallel",), vmem_limit_bytes=VMEM_LIMIT),
        name=name)(*args)


def _in_proj_kernel(x_ref, g_ref, w_ref, o_ref):
    o_ref[0] = _dot(_rmsnorm_rows(x_ref[0], g_ref[...]).astype(BF16), w_ref[...])


def _norm_matmul(x, g, w, layer, tm):
    b, l, k = x.shape
    n = w.shape[2]
    nt = l // tm
    fix = lambda i: (0, 0)
    return _dense_call(
        _in_proj_kernel, "norm_matmul", jax.ShapeDtypeStruct((b, l, n), F32),
        [pl.BlockSpec((1, tm, k), _rows3(nt)), pl.BlockSpec((1, k), fix), _wspec(k, n, layer)],
        pl.BlockSpec((1, tm, n), _rows3(nt)), b * nt, (x, g, w))


def _conv_silu(u, w8, fix):
    out = fix(pltpu.roll(u, 3, axis=0), 3) * w8[0:1, :]
    for s in (2, 1):
        out = out + fix(pltpu.roll(u, s, axis=0), s) * w8[3 - s:4 - s, :]
    out = out + u * w8[3:4, :]
    return _silu(out)


def _in_proj_conv_kernel(x_ref, g_ref, w_ref, cp_ref, cw_ref, o_ref, c_ref, u_scr, tail_scr, *, nt, n_tiles):
    i = pl.program_id(0)
    tm = u_scr.shape[0]

    @pl.when(i == 0)
    def _():
        u_scr[...] = jnp.zeros_like(u_scr)
        tail_scr[...] = jnp.zeros_like(tail_scr)

    prev = tail_scr[...]
    u_prev = u_scr[...]
    r8 = _iota((8, CONV_CH), 0)

    def fix(rolled, s):
        first = jnp.where(r8 < s, pltpu.roll(prev, s, axis=0), rolled[0:8])
        return jnp.concatenate([first, rolled[8:]], axis=0)

    c_ref[0] = _conv_silu(u_prev, cw_ref[...], fix)

    z = _dot(_rmsnorm_rows(x_ref[0], g_ref[...]).astype(BF16), w_ref[...])
    o_ref[0] = z
    first_of_seq = jnp.minimum(i, n_tiles - 1) % nt == 0
    tail_scr[...] = jnp.where(first_of_seq, cp_ref[0], u_prev[tm - 8:])
    u_scr[...] = z[:, :CONV_CH]


def _norm_matmul_conv(x, g, w, convp, cw8, layer, tm):
    b, l, k = x.shape
    n = w.shape[2]
    nt = l // tm
    n_tiles = b * nt
    fix = lambda i: (0, 0)
    cur = lambda i: jnp.minimum(i, n_tiles - 1)
    tile = lambda i: (cur(i) // nt, cur(i) % nt, 0)
    prev_tile = lambda i: (jnp.maximum(i - 1, 0) // nt, jnp.maximum(i - 1, 0) % nt, 0)
    return pl.pallas_call(
        functools.partial(_in_proj_conv_kernel, nt=nt, n_tiles=n_tiles),
        out_shape=(jax.ShapeDtypeStruct((b, l, n), F32), jax.ShapeDtypeStruct((b, l, CONV_CH), F32)),
        grid=(n_tiles + 1,),
        in_specs=[pl.BlockSpec((1, tm, k), tile), pl.BlockSpec((1, k), fix), _wspec(k, n, layer),
                  pl.BlockSpec((1, 8, CONV_CH), lambda i: (cur(i) // nt, 0, 0)), pl.BlockSpec((8, CONV_CH), fix)],
        out_specs=(pl.BlockSpec((1, tm, n), tile), pl.BlockSpec((1, tm, CONV_CH), prev_tile)),
        scratch_shapes=[pltpu.VMEM((tm, CONV_CH), F32), pltpu.VMEM((8, CONV_CH), F32)],
        compiler_params=pltpu.CompilerParams(dimension_semantics=("arbitrary",), vmem_limit_bytes=VMEM_LIMIT),
        name="norm_matmul_conv",
    )(x, g, w, convp, cw8)


def _mem_proj_kernel(x_ref, g_ref, wk_ref, wv_ref, k_ref, kt_ref, v_ref, vt_ref):
    x = x_ref[0]
    xhat = x * lax.rsqrt(jnp.mean(x * x, axis=-1, keepdims=True) + EPS)
    for l in range(DEPTH):
        xn = (xhat * g_ref[l:l + 1, :]).astype(BF16)
        for w_ref, o_ref, t_ref in ((wk_ref, k_ref, kt_ref), (wv_ref, v_ref, vt_ref)):
            res = _dot(xn, w_ref[l])
            o_ref[l, 0] = res
            for h in range(X_HEADS):
                t_ref[l, 0, pl.ds(h, N_MEM, stride=X_HEADS), :] = res[:, h * X_HD:(h + 1) * X_HD]


def _mem_proj(mem, g, wk, wv):
    b = mem.shape[0]
    whole = lambda shape: pl.BlockSpec(shape, lambda i: (0,) * len(shape))
    o_spec = pl.BlockSpec((DEPTH, 1, N_MEM, X_WIDTH), lambda i: (0, i, 0, 0))
    t_spec = pl.BlockSpec((DEPTH, 1, N_MEM * X_HEADS, X_HD), lambda i: (0, i, 0, 0))
    o_shape = jax.ShapeDtypeStruct((DEPTH, b, N_MEM, X_WIDTH), F32)
    t_shape = jax.ShapeDtypeStruct((DEPTH, b, N_MEM * X_HEADS, X_HD), F32)
    return _dense_call(
        _mem_proj_kernel, "mem_proj", (o_shape, t_shape, o_shape, t_shape),
        [pl.BlockSpec((1, N_MEM, D_MODEL), lambda i: (i, 0, 0)), whole((DEPTH, D_MODEL)),
         whole((DEPTH, D_MODEL, X_WIDTH)), whole((DEPTH, D_MODEL, X_WIDTH))],
        (o_spec, t_spec, o_spec, t_spec), b, (mem, g, wk, wv))


def _mix_residual(x, dn, ml, s5, w1_ref, w2_ref, w3_ref):
    acc = x + _dot(dn, w1_ref[...])
    acc = acc + _dot(ml, w2_ref[...])
    return acc + _dot(s5, w3_ref[...])


def _mlp_residual(x2, g_ref, f1_ref, f2_ref):
    xn = _rmsnorm_rows(x2, g_ref[...]).astype(BF16)
    acc = x2
    for c in range(D_FF // FF_CHUNK):
        hf = _dot(xn, f1_ref[:, c * FF_CHUNK:(c + 1) * FF_CHUNK])
        a = jnp.square(jnp.maximum(hf, 0.0)).astype(BF16)
        acc = acc + _dot(a, f2_ref[c * FF_CHUNK:(c + 1) * FF_CHUNK, :])
    return acc


def _out_proj_kernel(dn_ref, ml_ref, s5_ref, x_ref, w1_ref, w2_ref, w3_ref, g_ref, wq_ref, x1_ref, q_ref):
    acc = _mix_residual(x_ref[0], dn_ref[0], ml_ref[0], s5_ref[...], w1_ref, w2_ref, w3_ref)
    x1_ref[0] = acc
    q_ref[0] = _dot(_rmsnorm_rows(acc, g_ref[...]).astype(BF16), wq_ref[...])


def _w_out_specs(layer):
    return [_wspec(HEADS_W, D_MODEL, layer, 0), _wspec(HEADS_W, D_MODEL, layer, 1),
            _wspec(S5_WIDTH, D_MODEL, layer, 2 * HEADS_W // S5_WIDTH)]


def _out_proj(dn, ml, s5, x, wts, prm, layer, tm):
    b, l, _ = x.shape
    nt = l // tm
    row = _rows3(nt)
    vec = pl.BlockSpec((1, D_MODEL), lambda i: (0, 0))
    return _dense_call(
        _out_proj_kernel, "out_proj",
        (jax.ShapeDtypeStruct((b, l, D_MODEL), F32), jax.ShapeDtypeStruct((b, l, X_WIDTH), F32)),
        [pl.BlockSpec((1, tm, HEADS_W), row), pl.BlockSpec((1, tm, HEADS_W), row),
         pl.BlockSpec((tm, S5_WIDTH), lambda i: (i % nt, i // nt)), pl.BlockSpec((1, tm, D_MODEL), row)]
        + _w_out_specs(layer) + [vec, _wspec(D_MODEL, X_WIDTH, layer)],
        (pl.BlockSpec((1, tm, D_MODEL), row), pl.BlockSpec((1, tm, X_WIDTH), row)), b * nt,
        (dn, ml, s5, x, wts["w_out"], wts["w_out"], wts["w_out"], prm["norm_x"], wts["w_xq"]))


def _ffn_kernel(x_ref, o_ref, wo_ref, g_ref, f1_ref, f2_ref, gl_ref, y_ref, *, final_norm):
    acc = _mlp_residual(x_ref[0] + _dot(o_ref[0], wo_ref[...]), g_ref, f1_ref, f2_ref)
    y_ref[0] = _rmsnorm_rows(acc, gl_ref[...]) if final_norm else acc


def _ffn(x, att, wts, prm, g_last, layer, tm, final_norm):
    b, l, _ = x.shape
    nt = l // tm
    row = _rows3(nt)
    vec = pl.BlockSpec((1, D_MODEL), lambda i: (0, 0))
    return _dense_call(
        functools.partial(_ffn_kernel, final_norm=final_norm), "ffn", jax.ShapeDtypeStruct((b, l, D_MODEL), F32),
        [pl.BlockSpec((1, tm, D_MODEL), row), pl.BlockSpec((1, tm, X_WIDTH), row), _wspec(X_WIDTH, D_MODEL, layer),
         vec, _wspec(D_MODEL, D_FF, layer), _wspec(D_FF, D_MODEL, layer), vec],
        pl.BlockSpec((1, tm, D_MODEL), row), b * nt,
        (x, att, wts["w_xo"], prm["norm_ff"], wts["w_ff1"], wts["w_ff2"], g_last))


def _attn_heads(q, k_ref, v_ref):
    outs = []
    for h in range(X_HEADS):
        sl = slice(h * X_HD, (h + 1) * X_HD)
        s = _dot(q[:, sl], k_ref[0, :, sl].astype(BF16), _NT) * (X_HD ** -0.5)
        e = jnp.exp(s - jnp.max(s, axis=-1, keepdims=True))
        p = e / jnp.sum(e, axis=-1, keepdims=True)
        outs.append(_dot(p.astype(BF16), v_ref[0, :, sl].astype(BF16)))
    return jnp.concatenate(outs, axis=-1)


def _attn_batch_kernel(q_ref, k_ref, v_ref, o_ref, *, bb):
    rows = X_HEADS * Q_ROWS
    col_head = _imod(_iota((rows, N_MEM * X_HEADS), 1), X_HEADS)
    row_head = _idiv(_iota((rows, N_MEM * X_HEADS), 0), Q_ROWS)
    own = col_head == row_head
    q4 = [jnp.concatenate([q_ref[bi, :, h * X_HD:(h + 1) * X_HD] for h in range(X_HEADS)], axis=0).astype(BF16)
          for bi in range(bb)]
    s = [jnp.where(own, _dot(q_, k_ref[bi].astype(BF16), _NT) * (X_HD ** -0.5), NEG)
         for bi, q_ in enumerate(q4)]
    e = [jnp.exp(x - jnp.max(x, axis=-1, keepdims=True)) for x in s]
    p = [x / jnp.sum(x, axis=-1, keepdims=True) for x in e]
    o = [_dot(x.astype(BF16), v_ref[bi].astype(BF16)) for bi, x in enumerate(p)]
    for bi in range(bb):
        for h in range(X_HEADS):
            o_ref[bi, :, h * X_HD:(h + 1) * X_HD] = o[bi][h * Q_ROWS:(h + 1) * Q_ROWS].astype(BF16)


def _attn_batch(q, k, v, layer, bb):
    b, l, _ = q.shape
    blk = lambda i: (i, 0, 0)
    kv = pl.BlockSpec((None, bb, N_MEM * X_HEADS, X_HD), lambda i: (layer, i, 0, 0))
    return _dense_call(
        functools.partial(_attn_batch_kernel, bb=bb), "attn_batch", jax.ShapeDtypeStruct((b, l, X_WIDTH), BF16),
        [pl.BlockSpec((bb, l, X_WIDTH), blk), kv, kv], pl.BlockSpec((bb, l, X_WIDTH), blk), b // bb, (q, k, v))


def _post_mix_kernel(dn_ref, ml_ref, s5_ref, x_ref, k_ref, v_ref, w1_ref, w2_ref, w3_ref, gx_ref, wq_ref,
                     wo_ref, gf_ref, f1_ref, f2_ref, gl_ref, y_ref, *, final_norm):
    x1 = _mix_residual(x_ref[0], dn_ref[0], ml_ref[0], s5_ref[...], w1_ref, w2_ref, w3_ref)
    q = _dot(_rmsnorm_rows(x1, gx_ref[...]).astype(BF16), wq_ref[...]).astype(BF16)
    att = _attn_heads(q, k_ref, v_ref).astype(BF16)
    acc = _mlp_residual(x1 + _dot(att, wo_ref[...]), gf_ref, f1_ref, f2_ref)
    y_ref[0] = _rmsnorm_rows(acc, gl_ref[...]) if final_norm else acc


def _post_mix(dn, ml, s5, x, mem_k, mem_v, wts, prm, g_last, layer, tm, final_norm):
    b, l, _ = x.shape
    nt = l // tm
    row = _rows3(nt)
    mem = pl.BlockSpec((None, 1, N_MEM, X_WIDTH), lambda i: (layer, i // nt, 0, 0))
    vec = pl.BlockSpec((1, D_MODEL), lambda i: (0, 0))
    return _dense_call(
        functools.partial(_post_mix_kernel, final_norm=final_norm), "post_mix",
        jax.ShapeDtypeStruct((b, l, D_MODEL), F32),
        [pl.BlockSpec((1, tm, HEADS_W), row), pl.BlockSpec((1, tm, HEADS_W), row),
         pl.BlockSpec((tm, S5_WIDTH), lambda i: (i % nt, i // nt)), pl.BlockSpec((1, tm, D_MODEL), row), mem, mem]
        + _w_out_specs(layer)
        + [vec, _wspec(D_MODEL, X_WIDTH, layer), _wspec(X_WIDTH, D_MODEL, layer), vec,
           _wspec(D_MODEL, D_FF, layer), _wspec(D_FF, D_MODEL, layer), vec],
        pl.BlockSpec((1, tm, D_MODEL), row), b * nt,
        (dn, ml, s5, x, mem_k, mem_v, wts["w_out"], wts["w_out"], wts["w_out"], prm["norm_x"], wts["w_xq"],
         wts["w_xo"], prm["norm_ff"], wts["w_ff1"], wts["w_ff2"], g_last))


def _s5_kernel(u_ref, h0r_ref, h0i_ref, lre_ref, lim_ref, ldt_ref, bre_ref, bim_ref, cre_ref, cim_ref,
               d_ref, wg_ref, bg_ref, y_ref, hr_out, hi_out,
               xr_s, xi_s, st_r, st_i, coef_s, bbr_s, bbi_s, *io_s, tb, nb, lane_chunk, seq_major_io):
    step = pl.program_id(0)
    if seq_major_io:
        u_s, y_s = io_s
        for b in range(nb):
            for j in range(S5_WIDTH // LANES):
                u_s[j, pl.ds(b, tb, stride=nb), :] = u_ref[b, :, j * LANES:(j + 1) * LANES]
        u = jnp.concatenate([u_s[j] for j in range(S5_WIDTH // LANES)], axis=1)
    else:
        u = u_ref[...]

    @pl.when(step == 0)
    def _():
        lr = lre_ref[...]
        li = lim_ref[...]
        dt = jnp.exp(ldt_ref[...])
        mag = jnp.exp(lr * dt)
        ar = mag * jnp.cos(li * dt)
        ai = mag * jnp.sin(li * dt)
        inv = 1.0 / (lr * lr + li * li)
        coef_s[0:1, :] = ar
        coef_s[1:2, :] = ai
        cr = ((ar - 1.0) * lr + ai * li) * inv
        ci = (ai * lr - (ar - 1.0) * li) * inv
        bbr_s[...] = (cr * bre_ref[...] - ci * bim_ref[...]).astype(BF16)
        bbi_s[...] = (cr * bim_ref[...] + ci * bre_ref[...]).astype(BF16)
        st_r[...] = h0r_ref[...]
        st_i[...] = h0i_ref[...]

    ub = u.astype(BF16)
    xr_s[...] = _dot(ub, bbr_s[...])
    xi_s[...] = _dot(ub, bbi_s[...])

    for lc in range(S5_STATE // lane_chunk):
        sl = slice(lc * lane_chunk, (lc + 1) * lane_chunk)
        ar = jnp.broadcast_to(coef_s[0:1, sl], (nb, lane_chunk))
        ai = jnp.broadcast_to(coef_s[1:2, sl], (nb, lane_chunk))

        def body(t, carry, sl=sl, ar=ar, ai=ai):
            hr, hi = carry
            r0 = pl.multiple_of(t * nb, 8)
            nhr = ar * hr - ai * hi + xr_s[pl.ds(r0, nb), sl]
            nhi = ar * hi + ai * hr + xi_s[pl.ds(r0, nb), sl]
            xr_s[pl.ds(r0, nb), sl] = nhr
            xi_s[pl.ds(r0, nb), sl] = nhi
            return nhr, nhi

        hr, hi = lax.fori_loop(0, tb, body, (st_r[:, sl], st_i[:, sl]), unroll=min(tb, 8))
        st_r[:, sl] = hr
        st_i[:, sl] = hi

    y = (_dot(xr_s[...].astype(BF16), cre_ref[...]) - _dot(xi_s[...].astype(BF16), cim_ref[...])
         + d_ref[...] * u)
    zg = 0.5 * y * (1.0 + jnp.tanh(math.sqrt(2.0 / math.pi) * (y + 0.044715 * (y * y * y))))
    yv = zg * _sigmoid(_dot(zg.astype(BF16), wg_ref[...]) + bg_ref[...])
    if seq_major_io:
        for j in range(S5_WIDTH // LANES):
            y_s[j] = yv[:, j * LANES:(j + 1) * LANES]
        for b in range(nb):
            for j in range(S5_WIDTH // LANES):
                c0 = b * S5_WIDTH + j * LANES
                y_ref[:, c0:c0 + LANES] = y_s[j, pl.ds(b, tb, stride=nb), :].astype(BF16)
    else:
        y_ref[...] = yv.astype(BF16)

    @pl.when(step == pl.num_programs(0) - 1)
    def _():
        hr_out[...] = st_r[...]
        hi_out[...] = st_i[...]


def _s5(u, h0r, h0i, prm, tb, lane_chunk, seq_major_io):
    nb = h0r.shape[0]
    fix2 = lambda i: (0, 0)
    vec = pl.BlockSpec((1, S5_STATE), fix2)
    st = pl.BlockSpec((nb, S5_STATE), fix2)
    rows = tb * nb
    scratch = [pltpu.VMEM((rows, S5_STATE), F32), pltpu.VMEM((rows, S5_STATE), F32),
               pltpu.VMEM((nb, S5_STATE), F32), pltpu.VMEM((nb, S5_STATE), F32),
               pltpu.VMEM((8, S5_STATE), F32),
               pltpu.VMEM((S5_WIDTH, S5_STATE), BF16), pltpu.VMEM((S5_WIDTH, S5_STATE), BF16)]
    if seq_major_io:
        l = u.shape[1]
        u_spec = pl.BlockSpec((nb, tb, S5_WIDTH), lambda i: (0, i, S5_COL_BLOCK))
        y_shape = jax.ShapeDtypeStruct((l, nb * S5_WIDTH), BF16)
        y_spec = pl.BlockSpec((tb, nb * S5_WIDTH), lambda i: (i, 0))
        io_shape = (S5_WIDTH // LANES, rows, LANES)
        scratch += [pltpu.VMEM(io_shape, F32), pltpu.VMEM(io_shape, F32)]
    else:
        l = u.shape[0] // nb
        u_spec = pl.BlockSpec((rows, S5_WIDTH), lambda i: (i, 0))
        y_shape = jax.ShapeDtypeStruct((l * nb, S5_WIDTH), BF16)
        y_spec = pl.BlockSpec((rows, S5_WIDTH), lambda i: (i, 0))
    return pl.pallas_call(
        functools.partial(_s5_kernel, tb=tb, nb=nb, lane_chunk=lane_chunk, seq_major_io=seq_major_io),
        out_shape=(y_shape, jax.ShapeDtypeStruct((nb, S5_STATE), F32), jax.ShapeDtypeStruct((nb, S5_STATE), F32)),
        grid=(l // tb,),
        in_specs=[u_spec, st, st, vec, vec, vec,
                  pl.BlockSpec((S5_WIDTH, S5_STATE), fix2), pl.BlockSpec((S5_WIDTH, S5_STATE), fix2),
                  pl.BlockSpec((S5_STATE, S5_WIDTH), fix2), pl.BlockSpec((S5_STATE, S5_WIDTH), fix2),
                  pl.BlockSpec((1, S5_WIDTH), fix2), pl.BlockSpec((S5_WIDTH, S5_WIDTH), fix2),
                  pl.BlockSpec((1, S5_WIDTH), fix2)],
        out_specs=(y_spec, st, st),
        scratch_shapes=scratch,
        compiler_params=pltpu.CompilerParams(dimension_semantics=("arbitrary",),
                                             vmem_limit_bytes=VMEM_LIMIT),
        name="s5",
    )(u, h0r, h0i, prm["lre"], prm["lim"], prm["ldt"], prm["bre"], prm["bim"], prm["cre"],
      prm["cim"], prm["d"], prm["wg"], prm["bg"])


def _l2norm_pairs(x_chunks, bones, scale):
    n = len(x_chunks)
    x = jnp.concatenate(x_chunks, axis=0) if n > 1 else x_chunks[0]
    out = [[None] * N_PAIRS for _ in range(n)]
    for p_ in range(N_PAIRS):
        xp = x[:, p_ * LANES:(p_ + 1) * LANES]
        xn = xp * (lax.rsqrt(_seg_mean(xp * xp, bones) * HEAD_DIM + EPS) * scale)
        for c in range(n):
            out[c][p_] = xn[c * CHUNK:(c + 1) * CHUNK]
    return out


def _head_norm_store(vals, gates, normw, bones, act, store):
    n = len(vals)
    for p_ in range(N_PAIRS):
        v = jnp.concatenate([vals[c][p_] for c in range(n)], axis=0) if n > 1 else vals[0][p_]
        g = jnp.concatenate([gates[c][p_] for c in range(n)], axis=0) if n > 1 else gates[0][p_]
        y = v * lax.rsqrt(_seg_mean(v * v, bones) + EPS) * normw * act(g)
        for c in range(n):
            store(c, p_, y[c * CHUNK:(c + 1) * CHUNK].astype(BF16))


def _dn_local(chunks, prm, prec):
    n = len(chunks)
    bones = prm["bones"]
    incl, strict = _pair_masks()
    ltri, upair = _ltri(), _upair()
    units = [(c, p_) for c in range(n) for p_ in range(N_PAIRS)]

    g_cols = _mm_sel_l(ltri, jnp.concatenate([-jnp.exp(prm["alog_row"]) * _softplus(ch["zs"]) for ch in chunks],
                                             axis=1))
    g_rows = _mm_sel_r(jnp.concatenate([-jnp.exp(prm["rowalog"]) * _softplus(ch["rows"] + prm["rowadd"])
                                        for ch in chunks], axis=0), upair)
    gates = [(_sigmoid(ch["zs"]), g_cols[:, c * LANES:(c + 1) * LANES], g_rows[c * 16:(c + 1) * 16])
             for c, ch in enumerate(chunks)]

    qn = _l2norm_pairs([ch["q"] for ch in chunks], bones, HEAD_DIM ** -0.5)
    kn = _l2norm_pairs([ch["k"] for ch in chunks], bones, 1.0)

    beta = [_expand(gates[c][0], 2 * p_, 2 * p_ + 1) for c, p_ in units]
    g_col = [_expand(gates[c][1], 6 + 2 * p_, 7 + 2 * p_) for c, p_ in units]
    dec = [jnp.exp(jnp.where(incl, gc - gates[c][2][p_:p_ + 1, :], NEG)) for gc, (c, p_) in zip(g_col, units)]
    kb = [kn[c][p_] * b for b, (c, p_) in zip(beta, units)]
    sc = [_mm(jnp.concatenate([kb_, qn[c][p_]], axis=0), _x2b(kn[c][p_]), _NT)
          for kb_, (c, p_) in zip(kb, units)]
    a_in = [s[CHUNK:] * d for s, d in zip(sc, dec)]
    mbd = [_x2(jnp.where(strict, s[:CHUNK] * d, 0.0)) for s, d in zip(sc, dec)]
    tri_prec, newton, sol_prec = prec
    t_bd = _tri_inverse_units(mbd, tri_prec, newton)
    eg = [jnp.exp(gc) for gc in g_col]
    x2r = _x2b if sol_prec == 1 else _x2
    rhs = [jnp.concatenate([x2r(chunks[c]["v"][:, p_ * LANES:(p_ + 1) * LANES] * b), x2r(kb_ * e)], axis=1)
           for b, kb_, e, (c, p_) in zip(beta, kb, eg, units)]
    sol = [_mm(t, r, prec=sol_prec) for t, r in zip(t_bd, rhs)]
    out = [[None] * N_PAIRS for _ in range(n)]
    for i, (c, p_) in enumerate(units):
        g_last = _last_row(g_col[i])
        out[c][p_] = dict(u=_fold(sol[i][:, :LANES]), w=_fold(sol[i][:, LANES:]), a_in=a_in[i],
                          qg=qn[c][p_] * eg[i], kd=kn[c][p_] * jnp.exp(g_last - g_col[i]),
                          eg_last=jnp.exp(g_last))
    return out


def _dn_seq_step(zq_ref, zk_ref, zv_ref, zg_ref, zs_ref, rows_ref,
                 add_ref, alog_ref, rowadd_ref, rowalog_ref, nw_ref, o_ref, s_scr, *, bb, prec):
    chunks = [dict(q=zq_ref[b], k=zk_ref[b], v=zv_ref[b], zs=zs_ref[b] + add_ref[...], rows=rows_ref[b, 0])
              for b in range(bb)]

    prm = dict(bones=_block_ones(), alog_row=alog_ref[...], rowadd=rowadd_ref[...],
               rowalog=rowalog_ref[...])
    loc = _dn_local(chunks, prm, prec)
    bdm = _bd_mask()
    units = [(b, p_) for b in range(bb) for p_ in range(N_PAIRS)]
    s_prev = [s_scr[b, p_] for b, p_ in units]
    ws = [_mm(jnp.concatenate([loc[b][p_]["w"], loc[b][p_]["qg"]], axis=0), s)
          for s, (b, p_) in zip(s_prev, units)]
    v_new = [loc[b][p_]["u"] - w_[:CHUNK] for w_, (b, p_) in zip(ws, units)]
    o = [w_[CHUNK:] + _mm(loc[b][p_]["a_in"], _x2b(v)) for w_, v, (b, p_) in zip(ws, v_new, units)]
    upd = [_mm(loc[b][p_]["kd"].T, v) for v, (b, p_) in zip(v_new, units)]
    for i, (b, p_) in enumerate(units):
        s_scr[b, p_] = loc[b][p_]["eg_last"][0:1, :] * s_prev[i] + jnp.where(bdm, upd[i], 0.0)

    def store(b, p_, val):
        o_ref[b, :, p_ * LANES:(p_ + 1) * LANES] = val

    _head_norm_store([[o[b * N_PAIRS + p_] for p_ in range(N_PAIRS)] for b in range(bb)],
                     [[zg_ref[b, :, p_ * LANES:(p_ + 1) * LANES] for p_ in range(N_PAIRS)] for b in range(bb)],
                     nw_ref[...], prm["bones"], _silu, store)


def _ml_units(chunks, prm):
    n = len(chunks)
    incl, _ = _pair_masks()
    lo = _lane_lo((CHUNK, LANES))
    ltri, upair = _ltri(), _upair()
    ones2 = prm["bones"]
    bdm2 = jnp.concatenate([_bd_mask(), _bd_mask()], axis=1)
    units = [(c, p_) for c in range(n) for p_ in range(N_PAIRS)]
    pair = lambda x, p_: x[:, p_ * LANES:(p_ + 1) * LANES]

    rows = [ch["rows"] + prm["rowadd"] for ch in chunks]
    b_cols = _mm_sel_l(ltri, jnp.concatenate([-_softplus(-ch["zs"]) for ch in chunks], axis=1))
    b_rows = _mm_sel_r(jnp.concatenate([-_softplus(-r) for r in rows], axis=0), upair)
    comp = [dict(ig_c=ch["zs"], b_c=b_cols[:, c * LANES:(c + 1) * LANES], ig_r=rows[c],
                 b_r=b_rows[c * 16:(c + 1) * 16]) for c, ch in enumerate(chunks)]

    qp = [pair(chunks[c]["q"], p_) for c, p_ in units]
    kp = [pair(chunks[c]["k"], p_) * (HEAD_DIM ** -0.5) for c, p_ in units]
    vp = [pair(chunks[c]["v"], p_) for c, p_ in units]
    b_col = [_expand(comp[c]["b_c"], 18 + 2 * p_, 19 + 2 * p_) for c, p_ in units]
    ig_col = [_expand(comp[c]["ig_c"], 12 + 2 * p_, 13 + 2 * p_) for c, p_ in units]
    dl = [jnp.where(incl, bc - comp[c]["b_r"][6 + p_:7 + p_, :] + comp[c]["ig_r"][3 + p_:4 + p_, :], NEG)
          for bc, (c, p_) in zip(b_col, units)]
    mx = [jnp.where(lo, jnp.max(jnp.where(lo, d, NEG), axis=1, keepdims=True),
                    jnp.max(jnp.where(lo, NEG, d), axis=1, keepdims=True)) for d in dl]
    inter = [bc + chunks[c]["ms"][p_] for bc, (c, p_) in zip(b_col, units)]
    mt = [jnp.maximum(a, b) for a, b in zip(inter, mx)]
    s = [_mm(q_, _x2b(k_), _NT) * jnp.exp(d - m) for q_, k_, d, m in zip(qp, kp, dl, mt)]
    sv = [_mm(s_, jnp.concatenate([_x2b(v_), ones2], axis=1)) for s_, v_ in zip(s, vp)]
    cn_prev = [chunks[c]["cn"][p_] for c, p_ in units]
    qcn = [_mm(q_, cn) for q_, cn in zip(qp, cn_prev)]
    w_inter = [jnp.exp(a - m) for a, m in zip(inter, mt)]
    h = [(wi * qc[:, :LANES] + sv_[:, :LANES])
         / jnp.maximum(jnp.abs(wi * qc[:, LANES:] + sv_[:, LANES:]), jnp.exp(-m))
         for wi, qc, sv_, m in zip(w_inter, qcn, sv, mt)]
    m_end = [_last_row(m) for m in mt]
    b_last = [_last_row(bc) for bc in b_col]
    a_end = [jnp.exp(bl + chunks[c]["ms"][p_] - me) for bl, me, (c, p_) in zip(b_last, m_end, units)]
    kw_t = [(k_ * jnp.exp(bl - bc + ig - me)).T for k_, bl, bc, ig, me in zip(kp, b_last, b_col, ig_col, m_end)]
    upd = [_mm(k_, jnp.concatenate([v_, jnp.ones((CHUNK, LANES), F32)], axis=1)) for k_, v_ in zip(kw_t, vp)]
    out = [[None] * N_PAIRS for _ in range(n)]
    for i, (c, p_) in enumerate(units):
        a_row = a_end[i][0:1, :]
        new_state = jnp.concatenate([a_row, a_row], axis=1) * cn_prev[i] + jnp.where(bdm2, upd[i], 0.0)
        out[c][p_] = (h[i], new_state, m_end[i][0:1, :])
    return out


def _ml_seq_step(zq_ref, zk_ref, zv_ref, zo_ref, zs_ref, rows_ref, add_ref, rowadd_ref, nw_ref, o_ref,
                 cn_scr, m_scr, *, bb):
    prm = dict(bones=_block_ones(), rowadd=rowadd_ref[...])
    chunks = []
    for b in range(bb):
        chunks.append(dict(q=zq_ref[b], k=zk_ref[b], v=zv_ref[b], zs=zs_ref[b] + add_ref[...],
                           rows=rows_ref[b, 0],
                           ms=[m_scr[b, p_:p_ + 1, :] for p_ in range(N_PAIRS)],
                           cn=[cn_scr[b, p_] for p_ in range(N_PAIRS)]))
    res = _ml_units(chunks, prm)
    for b in range(bb):
        for p_ in range(N_PAIRS):
            cn_scr[b, p_] = res[b][p_][1]
            m_scr[b, p_:p_ + 1, :] = res[b][p_][2]

    def store(b, p_, val):
        o_ref[b, :, p_ * LANES:(p_ + 1) * LANES] = val

    _head_norm_store([[res[b][p_][0] for p_ in range(N_PAIRS)] for b in range(bb)],
                     [[zo_ref[b, :, p_ * LANES:(p_ + 1) * LANES] for p_ in range(N_PAIRS)] for b in range(bb)],
                     nw_ref[...], prm["bones"], _sigmoid, store)


def _mixers_seq_kernel(dq_ref, dk_ref, dv_ref, dg_ref, lq_ref, lk_ref, lv_ref, lo_ref, zs_ref, rows_ref,
                       s0_ref, cn0_ref, m0_ref,
                       add_ref, alog_ref, rowadd_ref, rowalog_ref, dnw_ref, mlw_ref,
                       dn_ref, ml_ref, s_out_ref, cn_out_ref, m_out_ref,
                       s_scr, cn_scr, m_scr, *, bb, prec):
    tstep = pl.program_id(1)

    @pl.when(tstep == 0)
    def _():
        s_scr[...] = s0_ref[...]
        cn_scr[...] = cn0_ref[...]
        m_scr[...] = m0_ref[...]

    _dn_seq_step(dq_ref, dk_ref, dv_ref, dg_ref, zs_ref, rows_ref, add_ref, alog_ref, rowadd_ref,
                 rowalog_ref, dnw_ref, dn_ref, s_scr, bb=bb, prec=prec)
    _ml_seq_step(lq_ref, lk_ref, lv_ref, lo_ref, zs_ref, rows_ref, add_ref, rowadd_ref, mlw_ref, ml_ref,
                 cn_scr, m_scr, bb=bb)

    @pl.when(tstep == pl.num_programs(1) - 1)
    def _():
        s_out_ref[...] = s_scr[...]
        cn_out_ref[...] = cn_scr[...]
        m_out_ref[...] = m_scr[...]


def _mixers_seq(z3, zc, rows4, s0, cn0, m0, prm, bb, prec):
    batch, seq_len, _ = z3.shape
    fix2 = lambda g, i: (0, 0)
    zspec = lambda width, col: pl.BlockSpec((bb, CHUNK, width), lambda g, i: (g, i, col))
    per_seq = lambda *shape: pl.BlockSpec((bb,) + shape, lambda g, i: (g,) + (0,) * len(shape))
    s_spec, cn_spec, m_spec = per_seq(N_PAIRS, LANES, LANES), per_seq(N_PAIRS, LANES, 2 * LANES), per_seq(8, LANES)
    small = lambda rows: pl.BlockSpec((rows, LANES), fix2)
    in_specs = ([zspec(HEADS_W, c) for c in range(8)] + [zspec(LANES, GATE_COL_BLOCK)]
                + [pl.BlockSpec((bb, 1, 16, LANES), lambda g, i: (g, i, 0, 0)), s_spec, cn_spec, m_spec,
                   small(1), small(1), small(16), small(16), small(1), small(1)])
    out_blk = pl.BlockSpec((bb, CHUNK, HEADS_W), lambda g, i: (g, i, 0))
    act = jax.ShapeDtypeStruct((batch, seq_len, HEADS_W), BF16)
    return pl.pallas_call(
        functools.partial(_mixers_seq_kernel, bb=bb, prec=prec),
        out_shape=(act, act, jax.ShapeDtypeStruct(s0.shape, F32), jax.ShapeDtypeStruct(cn0.shape, F32),
                   jax.ShapeDtypeStruct(m0.shape, F32)),
        grid=(batch // bb, seq_len // CHUNK),
        in_specs=in_specs,
        out_specs=(out_blk, out_blk, s_spec, cn_spec, m_spec),
        scratch_shapes=[pltpu.VMEM((bb, N_PAIRS, LANES, LANES), F32),
                        pltpu.VMEM((bb, N_PAIRS, LANES, 2 * LANES), F32), pltpu.VMEM((bb, 8, LANES), F32)],
        compiler_params=pltpu.CompilerParams(dimension_semantics=("parallel", "arbitrary"),
                                             vmem_limit_bytes=VMEM_LIMIT),
        name="mixers_seq",
    )(zc, zc, zc, *([z3] * 6), rows4, s0, cn0, m0, prm["add_row"], prm["alog_row"],
      prm["rowadd"], prm["rowalog"], prm["dn_normw"], prm["ml_normw"])


def _lane_mixers_kernel(zq, zk, zv, zg, lq, lk, lv, lo, zs, cpq, cpk, cpv, cwq, cwk, cwv, bias, alog, dnw, mlw,
                        s0, c0, n0, m0, *rest, n_tok, n_prev):
    if n_prev:
        s_prev, c_prev, dn_o, ml_o, s_st, c_st, n_o, m_o, k_s, q_s = rest
        s_st[0:n_prev] = s_prev[...]
        c_st[0:n_prev] = c_prev[...]
    else:
        dn_o, ml_o, s_st, c_st, n_o, m_o, k_s, q_s = rest
    s_o, c_o = s_st.at[n_prev], c_st.at[n_prev]
    h = pl.program_id(0)
    nb = s_o.shape[-1]
    zrow = lambda t, r: zs[t, pl.ds(r, 1), :]
    prow = lambda ref, r: ref[pl.ds(r, 1), :]
    bcast = lambda ref, dk: jnp.broadcast_to(ref[pl.ds(dk, 1), :], (HEAD_DIM, nb))
    zeros = jnp.zeros((HEAD_DIM, nb), F32)

    def conv(z_ref, cp_ref, cw_ref):
        e = [cp_ref[j] for j in range(CONV_W - 1)] + [z_ref[t] for t in range(n_tok)]
        outs = []
        for t in range(n_tok):
            acc = e[t] * cw_ref[0]
            for j in range(1, CONV_W):
                acc = acc + e[t + j] * cw_ref[j]
            outs.append(_silu(acc))
        return outs

    def l2n(x, scale):
        return x * (lax.rsqrt(jnp.sum(x * x, axis=0, keepdims=True) + EPS) * scale)

    def head_norm(x, w):
        return x * lax.rsqrt(jnp.mean(x * x, axis=0, keepdims=True) + EPS) * w

    qc, kc, vc = conv(zq, cpq, cwq), conv(zk, cpk, cwk), conv(zv, cpv, cwv)
    s_o[...] = s0[...]
    for t in range(n_tok):
        k_s[...] = l2n(kc[t], 1.0)
        q_s[...] = l2n(qc[t], HEAD_DIM ** -0.5)
        beta = _sigmoid(zrow(t, h))
        a = jnp.exp(-jnp.exp(prow(alog, N_HEADS + h))
                    * _softplus(zrow(t, N_HEADS + h) + prow(bias, N_HEADS + h)))
        ks = lax.fori_loop(0, HEAD_DIM, lambda dk, acc: acc + bcast(k_s, dk) * s_o[dk], zeros, unroll=8)
        delta = beta * (vc[t] - a * ks)

        def dn_update(dk, acc, a=a, delta=delta):
            s_new = a * s_o[dk] + bcast(k_s, dk) * delta
            s_o[dk] = s_new
            return acc + bcast(q_s, dk) * s_new

        o = lax.fori_loop(0, HEAD_DIM, dn_update, zeros, unroll=8)
        dn_o[t] = (head_norm(o, dnw[...]) * _silu(zg[t])).astype(BF16)

    c_o[...] = c0[...]
    n = n0[...]
    m = m0[...]
    for t in range(n_tok):
        q, k, v = lq[t], lk[t] * (HEAD_DIM ** -0.5), lv[t]
        ig = zrow(t, 2 * N_HEADS + h) + prow(bias, 2 * N_HEADS + h)
        lsf = -_softplus(-(zrow(t, 3 * N_HEADS + h) + prow(bias, 3 * N_HEADS + h)))
        m_new = jnp.maximum(lsf + m, ig)
        fp = jnp.exp(lsf + m - m_new)
        ip = jnp.exp(ig - m_new)
        k_s[...] = k * ip
        q_s[...] = q

        def ml_update(dk, acc, fp=fp, v=v):
            c_new = fp * c_o[dk] + bcast(k_s, dk) * v
            c_o[dk] = c_new
            return acc + bcast(q_s, dk) * c_new

        num = lax.fori_loop(0, HEAD_DIM, ml_update, zeros, unroll=8)
        n = fp * n + ip * k
        den = jnp.sum(q * n, axis=0, keepdims=True)
        hh = num / jnp.maximum(jnp.abs(den), jnp.exp(-m_new))
        m = m_new
        ml_o[t] = (head_norm(hh, mlw[...]) * _sigmoid(lo[t])).astype(BF16)
    n_o[...] = n
    m_o[...] = m


def _lane_mixers(z_t, conv_t, s0, c0, n0, m0, layer, prm, prev):
    n_tok, _, nb = z_t.shape
    hd = HEAD_DIM
    n_prev = 0 if prev is None else prev[0].shape[0]
    assert n_prev == layer
    zblk = lambda first: pl.BlockSpec((n_tok, hd, nb), lambda h: (0, first + h, 0))
    cblk = lambda taps, seg: pl.BlockSpec((taps, hd, nb), lambda h: (0, seg * N_HEADS + h, 0))
    wblk = lambda seg: pl.BlockSpec((CONV_W, hd, nb), lambda h: (0, seg * N_HEADS + h, 0))
    whole = lambda shape: pl.BlockSpec(shape, lambda h: (0,) * len(shape))
    mat_in = pl.BlockSpec((None, None, hd, hd, nb), lambda h: (layer, h, 0, 0, 0))
    mat_out = pl.BlockSpec((n_prev + 1, None, hd, hd, nb), lambda h: (0, h, 0, 0, 0))
    prev_specs = [pl.BlockSpec((n_prev, None, hd, hd, nb), lambda h: (0, h, 0, 0, 0))] * 2 if n_prev else []
    small_rows = 32
    in_specs = [zblk(seg * N_HEADS) for seg in range(8)] + [
        pl.BlockSpec((n_tok, small_rows, nb), lambda h: (0, (Z_COLS - LANES) // small_rows, 0)),
        cblk(CONV_W - 1, 0), cblk(CONV_W - 1, 1), cblk(CONV_W - 1, 2), wblk(0), wblk(1), wblk(2),
        whole((small_rows, nb)), whole((small_rows, nb)), whole((hd, nb)), whole((hd, nb)),
        mat_in, mat_in,
        pl.BlockSpec((None, None, hd, nb), lambda h: (layer, h, 0, 0)),
        pl.BlockSpec((None, None, 1, nb), lambda h: (layer, h, 0, 0))] + prev_specs
    out_blk = pl.BlockSpec((n_tok, hd, nb), lambda h: (0, h, 0))
    act = jax.ShapeDtypeStruct((n_tok, HEADS_W, nb), BF16)
    mat = jax.ShapeDtypeStruct((n_prev + 1, N_HEADS, hd, hd, nb), F32)
    return pl.pallas_call(
        functools.partial(_lane_mixers_kernel, n_tok=n_tok, n_prev=n_prev),
        out_shape=(act, act, mat, mat, jax.ShapeDtypeStruct((N_HEADS, hd, nb), F32),
                   jax.ShapeDtypeStruct((N_HEADS, 1, nb), F32)),
        grid=(N_HEADS,),
        in_specs=in_specs,
        out_specs=(out_blk, out_blk, mat_out, mat_out,
                   pl.BlockSpec((None, hd, nb), lambda h: (h, 0, 0)), pl.BlockSpec((None, 1, nb), lambda h: (h, 0, 0))),
        scratch_shapes=[pltpu.VMEM((hd, nb), F32), pltpu.VMEM((hd, nb), F32)],
        compiler_params=pltpu.CompilerParams(dimension_semantics=("parallel",), vmem_limit_bytes=VMEM_LIMIT),
        name="lane_mixers",
    )(*([z_t] * 9), conv_t, conv_t, conv_t, prm["conv_wb"], prm["conv_wb"], prm["conv_wb"], prm["bias_b"],
      prm["alog_b"], prm["dn_normb"], prm["ml_normb"], s0, c0, n0, m0, *(prev or ()))


_IN_OFFS = [int(v) for v in np.concatenate([[0], np.cumsum(IN_SIZES)])]
PACK_ROWS = 256


def _pack_w_kernel(w_ref, o_ref):
    w = w_ref[0]
    o = _IN_OFFS
    lane = _iota((w.shape[0], LANES), 1)
    t_a = w[:, o[3]:o[3] + LANES]
    b0 = (o[9] // LANES) * LANES
    t_b = w[:, b0:b0 + LANES]
    assert o[3] % LANES == 0 and o[9] - b0 == 2 * N_HEADS
    small = jnp.where(lane < 2 * N_HEADS, t_a, jnp.where(lane < 4 * N_HEADS, t_b, 0.0))
    cols = [w[:, o[0]:o[3]], w[:, o[5]:o[9]], w[:, o[11]:o[13]], small]
    o_ref[0] = jnp.concatenate(cols, axis=-1).astype(BF16)


def _pack_w_in(w_in):
    depth, k, n = w_in.shape
    return pl.pallas_call(
        _pack_w_kernel, out_shape=jax.ShapeDtypeStruct((depth, k, Z_COLS), BF16),
        grid=(depth, k // PACK_ROWS),
        in_specs=[pl.BlockSpec((1, PACK_ROWS, n), lambda l, i: (l, i, 0))],
        out_specs=pl.BlockSpec((1, PACK_ROWS, Z_COLS), lambda l, i: (l, i, 0)),
        compiler_params=pltpu.CompilerParams(dimension_semantics=("parallel", "parallel"),
                                             vmem_limit_bytes=VMEM_LIMIT),
        name="pack_w_in",
    )(w_in)


def _pair_rows(v6):
    return jnp.repeat(v6.reshape(N_PAIRS, 2), HEAD_DIM, axis=1).reshape(N_PAIRS, LANES)


def _layer_params(l, a):
    f = lambda name: a[name][l].astype(F32)
    zeros6 = jnp.zeros((N_HEADS,), F32)
    add_row = jnp.concatenate([zeros6, f("dn_dt_bias"), f("ml_i_bias"), f("ml_f_bias")])
    add_row = jnp.pad(add_row, (0, LANES - 4 * N_HEADS))[None, :]
    alog_row = jnp.pad(jnp.concatenate([zeros6, f("dn_a_log")]), (0, LANES - 2 * N_HEADS))[None, :]
    rowadd = jnp.concatenate([_pair_rows(f("dn_dt_bias")), _pair_rows(f("ml_i_bias")),
                              _pair_rows(f("ml_f_bias")), jnp.zeros((7, LANES), F32)], axis=0)
    rowalog = jnp.concatenate([_pair_rows(f("dn_a_log")), jnp.zeros((13, LANES), F32)], axis=0)
    eye_g = jnp.eye(S5_GROUPS, dtype=F32)
    bd_in = lambda b: jnp.einsum("gpc,gh->gchp", b, eye_g).reshape(S5_WIDTH, S5_STATE)
    bd_out = lambda c: jnp.einsum("gcp,gh->gphc", c, eye_g).reshape(S5_STATE, S5_WIDTH).astype(BF16)
    return dict(
        layer=l, norm_mix=f("norm_mix")[None, :],
        add_row=add_row, alog_row=alog_row, rowadd=rowadd, rowalog=rowalog,
        conv_w8=jnp.pad(f("dn_conv_w"), ((0, 8 - CONV_W), (0, 0))),
        dn_normw=jnp.tile(f("dn_norm"), 2)[None, :], ml_normw=jnp.tile(f("ml_norm"), 2)[None, :],
        s5=dict(lre=f("s5_lam_re").reshape(1, S5_STATE), lim=f("s5_lam_im").reshape(1, S5_STATE),
                ldt=jnp.repeat(f("s5_log_dt"), S5_P)[None, :],
                bre=bd_in(f("s5_b_re")), bim=bd_in(f("s5_b_im")),
                cre=bd_out(f("s5_c_re")), cim=bd_out(f("s5_c_im")),
                d=f("s5_d").reshape(1, S5_WIDTH), wg=f("s5_w_glu").astype(BF16),
                bg=f("s5_b_glu")[None, :]),
        norm_x=f("norm_x")[None, :], norm_ff=f("norm_ff")[None, :],
    )


def _shared_weights(a):
    cast = lambda name: a[name].astype(BF16)
    return dict(w_in=_pack_w_in(a["w_in"]), w_out=cast("w_out"), w_xq=cast("w_xq"), w_xo=cast("w_xo"),
                w_ff1=cast("w_ff1"), w_ff2=cast("w_ff2"), w_xk=cast("w_xk"), w_xv=cast("w_xv"))


def _row_form(z, n_chunks):
    zs = z[..., Z_COLS - LANES:Z_COLS - LANES + 4 * N_HEADS].reshape(n_chunks, CHUNK, 4, N_PAIRS, 2)
    r = jnp.transpose(zs[:, :, 1:], (0, 2, 3, 4, 1)).reshape(n_chunks, 9, LANES)
    return jnp.pad(r, ((0, 0), (0, 7), (0, 0)))


def _un_bd_pairs(s):
    n = s.shape[0]
    a = s[:, :, :HEAD_DIM, :HEAD_DIM]
    b = s[:, :, HEAD_DIM:, HEAD_DIM:]
    return jnp.stack([a, b], axis=2).reshape(n, N_HEADS, HEAD_DIM, HEAD_DIM)


def _unpack_m(mp):
    return mp[:, :N_PAIRS, ::HEAD_DIM].reshape(mp.shape[0], N_HEADS)


def _tile(n, pref):
    tile = min(n, pref)
    assert n % tile == 0, (n, pref)
    return tile


def _layer_prompt(x, mem_k, mem_v, prm, wts, final_g):
    batch, seq_len, _ = x.shape
    layer = prm["layer"]
    z3, zc = _norm_matmul_conv(x, prm["norm_mix"], wts["w_in"], jnp.zeros((batch, 8, CONV_CH), F32),
                               prm["conv_w8"], layer, _tile(seq_len, CONV_ROW_TILE))
    rows4 = _row_form(z3, batch * seq_len // CHUNK).reshape(batch, seq_len // CHUNK, 16, LANES)
    new_conv = z3[:, seq_len - (CONV_W - 1):, :CONV_CH]

    dn, ml, s_new, cn_new, m_new = _mixers_seq(
        z3, zc, rows4, jnp.zeros((batch, N_PAIRS, LANES, LANES), F32),
        jnp.zeros((batch, N_PAIRS, LANES, 2 * LANES), F32), jnp.zeros((batch, 8, LANES), F32), prm,
        _tile(batch, SEQ_BATCH), SEQ_SOLVE)
    zero_h = jnp.zeros((batch, S5_STATE), F32)
    s5o, hr, hi = _s5(z3, zero_h, zero_h, prm["s5"], _tile(seq_len, S5_STEPS), S5_STATE, True)

    g_fin = final_g if final_g is not None else prm["norm_ff"]
    x3 = _post_mix(dn, ml, s5o, x, mem_k, mem_v, wts, prm, g_fin, layer, _tile(seq_len, POST_TILE),
                   final_g is not None)
    st = (new_conv, _un_bd_pairs(s_new), _un_bd_pairs(cn_new[..., :LANES]),
          _un_bd_pairs(cn_new[..., LANES:])[..., 0], _unpack_m(m_new),
          hr.reshape(batch, S5_GROUPS, S5_P), hi.reshape(batch, S5_GROUPS, S5_P))
    return x3, st


def _layer_sample(x, cache_k, cache_v, conv_buf, dn_s, ml_c, ml_n, ml_m, ssm_re, ssm_im, prm, wts, batch,
                  seq_len, final_g, prev_stacks):
    t = batch * seq_len
    layer = prm["layer"]
    z = _norm_matmul(x, prm["norm_mix"], wts["w_in"], layer, t)
    z3 = z.reshape(seq_len, batch, Z_COLS)
    new_conv = jnp.concatenate([conv_buf, jnp.transpose(z3[:, :, :CONV_CH], (1, 0, 2))], axis=1)[:, seq_len:]
    lanes = lambda v: jnp.broadcast_to(v[..., None], v.shape + (batch,))
    mix_prm = dict(conv_wb=lanes(prm["conv_w8"][:CONV_W]), bias_b=lanes(prm["add_row"][0, :32]),
                   alog_b=lanes(prm["alog_row"][0, :32]), dn_normb=lanes(prm["dn_normw"][0, :HEAD_DIM]),
                   ml_normb=lanes(prm["ml_normw"][0, :HEAD_DIM]))
    dn_t, ml_t, s_stack, c_stack, n_new, m_new = _lane_mixers(
        jnp.transpose(z3, (0, 2, 1)), jnp.transpose(conv_buf, (1, 2, 0)), dn_s, ml_c, ml_n, ml_m, layer, mix_prm,
        prev_stacks)
    to_rows = lambda a: jnp.transpose(a, (0, 2, 1)).reshape(1, t, HEADS_W)
    dn, ml = to_rows(dn_t), to_rows(ml_t)

    s5o, hr, hi = _s5(z[0, :, 3072:3072 + S5_WIDTH], ssm_re.reshape(batch, S5_STATE),
                      ssm_im.reshape(batch, S5_STATE), prm["s5"], seq_len, LANES, False)

    x1, q = _out_proj(dn, ml, s5o, x, wts, prm, layer, t)
    q3 = jnp.pad(jnp.transpose(q.reshape(seq_len, batch, X_WIDTH), (1, 0, 2)),
                 ((0, 0), (0, Q_ROWS - seq_len), (0, 0)))
    att = _attn_batch(q3, cache_k, cache_v, layer, _tile(batch, ATTN_BATCH))[:, :seq_len]
    att = jnp.transpose(att, (1, 0, 2)).reshape(1, t, X_WIDTH)
    g_fin = final_g if final_g is not None else prm["norm_ff"]
    x3 = _ffn(x1, att, wts, prm, g_fin, layer, t, final_g is not None)
    st = (new_conv, jnp.transpose(n_new, (2, 0, 1)), jnp.transpose(m_new[:, 0, :], (1, 0)),
          hr.reshape(batch, S5_GROUPS, S5_P), hi.reshape(batch, S5_GROUPS, S5_P))
    return x3, st, (s_stack, c_stack)


def kernel(x_prompt, x_sample, mem_prompt, state_dn_conv, state_dn_s, state_ml_c, state_ml_n, state_ml_m, state_ssm_re, state_ssm_im, cache_mem_k, cache_mem_v, norm_mix, w_in, dn_conv_w, dn_a_log, dn_dt_bias, dn_norm, ml_i_bias, ml_f_bias, ml_norm, s5_lam_re, s5_lam_im, s5_log_dt, s5_b_re, s5_b_im, s5_c_re, s5_c_im, s5_d, s5_w_glu, s5_b_glu, w_out, norm_x, norm_mem, w_xq, w_xk, w_xv, w_xo, norm_ff, w_ff1, w_ff2, norm_final):
    a = dict(norm_mix=norm_mix, w_in=w_in, dn_conv_w=dn_conv_w, dn_a_log=dn_a_log, dn_dt_bias=dn_dt_bias,
             dn_norm=dn_norm, ml_i_bias=ml_i_bias, ml_f_bias=ml_f_bias, ml_norm=ml_norm,
             s5_lam_re=s5_lam_re, s5_lam_im=s5_lam_im, s5_log_dt=s5_log_dt, s5_b_re=s5_b_re,
             s5_b_im=s5_b_im, s5_c_re=s5_c_re, s5_c_im=s5_c_im, s5_d=s5_d, s5_w_glu=s5_w_glu,
             s5_b_glu=s5_b_glu, w_out=w_out, norm_x=norm_x, w_xq=w_xq, w_xo=w_xo, norm_ff=norm_ff,
             w_ff1=w_ff1, w_ff2=w_ff2)
    prms = [_layer_params(l, a) for l in range(DEPTH)]
    wts = _shared_weights(dict(a, w_xk=w_xk, w_xv=w_xv))
    g_final = norm_final.astype(F32)[None, :]

    bp, lp, _ = x_prompt.shape
    bs, ls, _ = x_sample.shape

    xp = x_prompt
    mk, mk_t, mv, mv_t = _mem_proj(mem_prompt, norm_mem.astype(F32), wts["w_xk"], wts["w_xv"])
    p_st = []
    for l in range(DEPTH):
        xp, st = _layer_prompt(xp, mk, mv, prms[l], wts, g_final if l == DEPTH - 1 else None)
        p_st.append(st)
    y_prompt = xp
    p_mem = (mk_t.reshape(DEPTH, bp, N_MEM, X_HEADS, X_HD), mv_t.reshape(DEPTH, bp, N_MEM, X_HEADS, X_HD))

    xs = jnp.transpose(x_sample, (1, 0, 2)).reshape(1, ls * bs, D_MODEL)
    cache_k = cache_mem_k.reshape(DEPTH, bs, N_MEM * X_HEADS, X_HD)
    cache_v = cache_mem_v.reshape(DEPTH, bs, N_MEM * X_HEADS, X_HD)
    dn_s_t = jnp.transpose(state_dn_s, (0, 2, 3, 4, 1))
    ml_c_t = jnp.transpose(state_ml_c, (0, 2, 3, 4, 1))
    ml_n_t = jnp.transpose(state_ml_n, (0, 2, 3, 1))
    ml_m_t = jnp.transpose(state_ml_m, (0, 2, 1))[:, :, None, :]
    s_st = []
    stacks = None
    for l in range(DEPTH):
        xs, st, stacks = _layer_sample(xs, cache_k, cache_v, state_dn_conv[l], dn_s_t, ml_c_t, ml_n_t, ml_m_t,
                                       state_ssm_re[l], state_ssm_im[l], prms[l], wts, bs, ls,
                                       g_final if l == DEPTH - 1 else None, stacks)
        s_st.append(st)
    y_sample = jnp.transpose(xs.reshape(ls, bs, D_MODEL), (1, 0, 2))
    s_mats = tuple(jnp.transpose(m, (0, 4, 1, 2, 3)) for m in stacks)

    stack = lambda sts, i: jnp.stack([s[i] for s in sts])
    return ((y_prompt, y_sample) + tuple(stack(p_st, i) for i in range(7)) + p_mem
            + (stack(s_st, 0),) + s_mats + tuple(stack(s_st, i) for i in range(1, 5)))
```

```python
import functools
import math

import numpy as np
import jax
import jax.numpy as jnp
from jax import lax
from jax.experimental import pallas as pl
from jax.experimental.pallas import tpu as pltpu

F32 = jnp.float32
BF16 = jnp.bfloat16

D_MODEL = 1024
DEPTH = 2
N_HEADS = 6
HEAD_DIM = 64
N_PAIRS = N_HEADS // 2
HEADS_W = N_HEADS * HEAD_DIM
CONV_W = 4
CONV_CH = 3 * HEADS_W
S5_WIDTH = 256
S5_GROUPS = 16
S5_P = 64
S5_STATE = S5_GROUPS * S5_P
N_MEM = 256
X_HEADS = 4
X_HD = 128
X_WIDTH = X_HEADS * X_HD
D_FF = 4 * D_MODEL
EPS = 1e-6
IN_SIZES = (384, 384, 384, 6, 6, 384, 384, 384, 384, 6, 6, 384, 256)

LANES = 128
CHUNK = 64
Q_ROWS = 8
NEG = -1e30
Z_COLS = 3456
S5_COL_BLOCK = 3072 // S5_WIDTH
GATE_COL_BLOCK = (Z_COLS - LANES) // LANES
VMEM_LIMIT = 52 * 1024 * 1024

ROW_TILE = 1024
CONV_ROW_TILE = 512
POST_TILE = 512
SEQ_BATCH = 8
S5_STEPS = 64
ATTN_BATCH = 8
FF_CHUNK = 1024
TRI_BASE = 8
SEQ_SOLVE = (1, True, 1)

_NN = (((1,), (0,)), ((), ()))
_NT = (((1,), (1,)), ((), ()))


def _dot(a, b, dims=_NN):
    return lax.dot_general(a, b, dims, preferred_element_type=F32)


def _split(a, n):
    out = []
    r = a
    for i in range(n):
        h = r.astype(BF16)
        out.append(h)
        if i + 1 < n:
            r = r - h.astype(F32)
    return out


def _mm(a, b, dims=_NN, prec=1):
    if prec == 1:
        return _dot(a.astype(BF16), b.astype(BF16), dims)
    return _mmp(_parts(a, prec), _parts(b, prec), dims)


def _parts(a, prec):
    return tuple(_split(a, 1 if prec == 1 else 2))


def _mmp(ap, bp, dims=_NN):
    acc = None
    if len(bp) > 1:
        acc = _dot(ap[0], bp[1], dims)
    if len(ap) > 1:
        cross = _dot(ap[1], bp[0], dims)
        acc = cross if acc is None else acc + cross
    lead = _dot(ap[0], bp[0], dims)
    return lead if acc is None else acc + lead


def _mm_sel_r(a, sel, n=3):
    parts = _split(a, n)
    acc = _dot(parts[-1], sel)
    for p_ in parts[-2::-1]:
        acc = acc + _dot(p_, sel)
    return acc


def _mm_sel_l(sel, a, n=3):
    parts = _split(a, n)
    acc = _dot(sel, parts[-1])
    for p_ in parts[-2::-1]:
        acc = acc + _dot(sel, p_)
    return acc


def _sigmoid(x):
    return 1.0 / (1.0 + jnp.exp(-x))


def _silu(x):
    return x * _sigmoid(x)


def _softplus(x):
    return jnp.maximum(x, 0.0) + jnp.log(1.0 + jnp.exp(-jnp.abs(x)))


def _rmsnorm_rows(x, g):
    return x * lax.rsqrt(jnp.mean(x * x, axis=-1, keepdims=True) + EPS) * g


def _iota(shape, dim):
    return lax.broadcasted_iota(jnp.int32, shape, dim)


def _idiv(x, n):
    return x >> (n.bit_length() - 1)


def _imod(x, n):
    return x & (n - 1)


def _lane_lo(shape):
    return _iota(shape, 1) < HEAD_DIM


def _x2(x):
    lo = _lane_lo(x.shape)
    return jnp.concatenate([jnp.where(lo, x, 0.0), jnp.where(lo, 0.0, x)], axis=0)


def _x2b(x):
    return _x2(x.astype(BF16))


def _fold(y):
    r = y.shape[0] // 2
    return y[:r] + y[r:]


def _expand(a, c0, c1):
    r = a.shape[0]
    return jnp.where(_lane_lo((r, LANES)), a[:, c0:c0 + 1], a[:, c1:c1 + 1])


def _bd_mask():
    return (_iota((LANES, LANES), 0) < HEAD_DIM) == (_iota((LANES, LANES), 1) < HEAD_DIM)


def _block_ones():
    return jnp.where(_bd_mask(), 1.0, 0.0).astype(BF16)


def _seg_mean(x, bones):
    return _mm_sel_r(x, bones, n=1) * (1.0 / HEAD_DIM)


def _ltri():
    return jnp.where(_iota((CHUNK, CHUNK), 1) <= _iota((CHUNK, CHUNK), 0), 1.0, 0.0).astype(BF16)


def _upair():
    a = _iota((LANES, LANES), 0)
    b = _iota((LANES, LANES), 1)
    return jnp.where(_bd_mask() & (a <= b), 1.0, 0.0).astype(BF16)


def _pair_masks():
    i = _iota((CHUNK, LANES), 0)
    j = _imod(_iota((CHUNK, LANES), 1), HEAD_DIM)
    return j <= i, j < i


def _blk_eq(n, b):
    return _idiv(_iota((n, n), 0), b) == _idiv(_iota((n, n), 1), b)


def _last_row(a):
    return jnp.broadcast_to(a[CHUNK - 1:CHUNK, :], a.shape)


def _tri_inverse_units(mbds, prec, newton):
    n = LANES
    eye = jnp.where(_iota((n, n), 0) == _iota((n, n), 1), 1.0, 0.0)
    n_sq = TRI_BASE.bit_length() - 2
    blk = _blk_eq(n, TRI_BASE)
    m0 = [jnp.where(blk, m, 0.0) for m in mbds]
    ts = [eye - m for m in m0]
    ps = [_mm(m, m, prec=prec) for m in m0]
    for it in range(n_sq):
        if it == n_sq - 1:
            ts = [t + _mm(t, p_, prec=prec) for t, p_ in zip(ts, ps)]
        else:
            both = [_mm(jnp.concatenate([t, p_], axis=0), p_, prec=prec) for t, p_ in zip(ts, ps)]
            ts = [t + b_[:n] for t, b_ in zip(ts, both)]
            ps = [b_[n:] for b_ in both]
    b = TRI_BASE
    while b < CHUNK:
        outer = _blk_eq(n, 2 * b) & jnp.logical_not(_blk_eq(n, b))
        tp = [_parts(t, prec) for t in ts]
        xs = [_mmp(_parts(jnp.where(outer, m, 0.0), prec), t_) for m, t_ in zip(mbds, tp)]
        ts = [t - _mmp(t_, _parts(x, prec)) for t, t_, x in zip(ts, tp, xs)]
        b *= 2
    if newton:
        res = [eye - (t + _mm(m, t, prec=3)) for m, t in zip(mbds, ts)]
        ts = [t + _mm(t, r) for t, r in zip(ts, res)]
    return ts


def _wspec(rows, cols, layer, row_block=0):
    return pl.BlockSpec((None, rows, cols), lambda i: (layer, row_block, 0), pipeline_mode=pl.Buffered(1))


def _rows3(nt):
    return lambda i: (i // nt, i % nt, 0)


def _dense_call(kernel_fn, name, out_shape, in_specs, out_specs, steps, args):
    return pl.pallas_call(
        kernel_fn, out_shape=out_shape, grid=(steps,), in_specs=in_specs, out_specs=out_specs,
        compiler_params=pltpu.CompilerParams(dimension_semantics=("parallel",), vmem_limit_bytes=VMEM_LIMIT),
        name=name)(*args)


def _in_proj_kernel(x_ref, g_ref, w_ref, o_ref):
    o_ref[0] = _dot(_rmsnorm_rows(x_ref[0], g_ref[...]).astype(BF16), w_ref[...])


def _norm_matmul(x, g, w, layer, tm):
    b, l, k = x.shape
    n = w.shape[2]
    nt = l // tm
    fix = lambda i: (0, 0)
    return _dense_call(
        _in_proj_kernel, "norm_matmul", jax.ShapeDtypeStruct((b, l, n), F32),
        [pl.BlockSpec((1, tm, k), _rows3(nt)), pl.BlockSpec((1, k), fix), _wspec(k, n, layer)],
        pl.BlockSpec((1, tm, n), _rows3(nt)), b * nt, (x, g, w))


def _conv_silu(u, w8, fix):
    out = fix(pltpu.roll(u, 3, axis=0), 3) * w8[0:1, :]
    for s in (2, 1):
        out = out + fix(pltpu.roll(u, s, axis=0), s) * w8[3 - s:4 - s, :]
    out = out + u * w8[3:4, :]
    return _silu(out)


def _in_proj_conv_kernel(x_ref, g_ref, w_ref, cp_ref, cw_ref, o_ref, c_ref, u_scr, tail_scr, *, nt, n_tiles):
    i = pl.program_id(0)
    tm = u_scr.shape[0]

    @pl.when(i == 0)
    def _():
        u_scr[...] = jnp.zeros_like(u_scr)
        tail_scr[...] = jnp.zeros_like(tail_scr)

    prev = tail_scr[...]
    u_prev = u_scr[...]
    r8 = _iota((8, CONV_CH), 0)

    def fix(rolled, s):
        first = jnp.where(r8 < s, pltpu.roll(prev, s, axis=0), rolled[0:8])
        return jnp.concatenate([first, rolled[8:]], axis=0)

    c_ref[0] = _conv_silu(u_prev, cw_ref[...], fix)

    z = _dot(_rmsnorm_rows(x_ref[0], g_ref[...]).astype(BF16), w_ref[...])
    o_ref[0] = z
    first_of_seq = jnp.minimum(i, n_tiles - 1) % nt == 0
    tail_scr[...] = jnp.where(first_of_seq, cp_ref[0], u_prev[tm - 8:])
    u_scr[...] = z[:, :CONV_CH]


def _norm_matmul_conv(x, g, w, convp, cw8, layer, tm):
    b, l, k = x.shape
    n = w.shape[2]
    nt = l // tm
    n_tiles = b * nt
    fix = lambda i: (0, 0)
    cur = lambda i: jnp.minimum(i, n_tiles - 1)
    tile = lambda i: (cur(i) // nt, cur(i) % nt, 0)
    prev_tile = lambda i: (jnp.maximum(i - 1, 0) // nt, jnp.maximum(i - 1, 0) % nt, 0)
    return pl.pallas_call(
        functools.partial(_in_proj_conv_kernel, nt=nt, n_tiles=n_tiles),
        out_shape=(jax.ShapeDtypeStruct((b, l, n), F32), jax.ShapeDtypeStruct((b, l, CONV_CH), F32)),
        grid=(n_tiles + 1,),
        in_specs=[pl.BlockSpec((1, tm, k), tile), pl.BlockSpec((1, k), fix), _wspec(k, n, layer),
                  pl.BlockSpec((1, 8, CONV_CH), lambda i: (cur(i) // nt, 0, 0)), pl.BlockSpec((8, CONV_CH), fix)],
        out_specs=(pl.BlockSpec((1, tm, n), tile), pl.BlockSpec((1, tm, CONV_CH), prev_tile)),
        scratch_shapes=[pltpu.VMEM((tm, CONV_CH), F32), pltpu.VMEM((8, CONV_CH), F32)],
        compiler_params=pltpu.CompilerParams(dimension_semantics=("arbitrary",), vmem_limit_bytes=VMEM_LIMIT),
        name="norm_matmul_conv",
    )(x, g, w, convp, cw8)


def _mem_proj_kernel(x_ref, g_ref, wk_ref, wv_ref, k_ref, kt_ref, v_ref, vt_ref):
    x = x_ref[0]
    xhat = x * lax.rsqrt(jnp.mean(x * x, axis=-1, keepdims=True) + EPS)
    for l in range(DEPTH):
        xn = (xhat * g_ref[l:l + 1, :]).astype(BF16)
        for w_ref, o_ref, t_ref in ((wk_ref, k_ref, kt_ref), (wv_ref, v_ref, vt_ref)):
            res = _dot(xn, w_ref[l])
            o_ref[l, 0] = res
            for h in range(X_HEADS):
                t_ref[l, 0, pl.ds(h, N_MEM, stride=X_HEADS), :] = res[:, h * X_HD:(h + 1) * X_HD]


def _mem_proj(mem, g, wk, wv):
    b = mem.shape[0]
    whole = lambda shape: pl.BlockSpec(shape, lambda i: (0,) * len(shape))
    o_spec = pl.BlockSpec((DEPTH, 1, N_MEM, X_WIDTH), lambda i: (0, i, 0, 0))
    t_spec = pl.BlockSpec((DEPTH, 1, N_MEM * X_HEADS, X_HD), lambda i: (0, i, 0, 0))
    o_shape = jax.ShapeDtypeStruct((DEPTH, b, N_MEM, X_WIDTH), F32)
    t_shape = jax.ShapeDtypeStruct((DEPTH, b, N_MEM * X_HEADS, X_HD), F32)
    return _dense_call(
        _mem_proj_kernel, "mem_proj", (o_shape, t_shape, o_shape, t_shape),
        [pl.BlockSpec((1, N_MEM, D_MODEL), lambda i: (i, 0, 0)), whole((DEPTH, D_MODEL)),
         whole((DEPTH, D_MODEL, X_WIDTH)), whole((DEPTH, D_MODEL, X_WIDTH))],
        (o_spec, t_spec, o_spec, t_spec), b, (mem, g, wk, wv))


def _mix_residual(x, dn, ml, s5, w1_ref, w2_ref, w3_ref):
    acc = x + _dot(dn, w1_ref[...])
    acc = acc + _dot(ml, w2_ref[...])
    return acc + _dot(s5, w3_ref[...])


def _mlp_residual(x2, g_ref, f1_ref, f2_ref):
    xn = _rmsnorm_rows(x2, g_ref[...]).astype(BF16)
    acc = x2
    for c in range(D_FF // FF_CHUNK):
        hf = _dot(xn, f1_ref[:, c * FF_CHUNK:(c + 1) * FF_CHUNK])
        a = jnp.square(jnp.maximum(hf, 0.0)).astype(BF16)
        acc = acc + _dot(a, f2_ref[c * FF_CHUNK:(c + 1) * FF_CHUNK, :])
    return acc


def _out_proj_kernel(dn_ref, ml_ref, s5_ref, x_ref, w1_ref, w2_ref, w3_ref, g_ref, wq_ref, x1_ref, q_ref):
    acc = _mix_residual(x_ref[0], dn_ref[0], ml_ref[0], s5_ref[...], w1_ref, w2_ref, w3_ref)
    x1_ref[0] = acc
    q_ref[0] = _dot(_rmsnorm_rows(acc, g_ref[...]).astype(BF16), wq_ref[...])


def _w_out_specs(layer):
    return [_wspec(HEADS_W, D_MODEL, layer, 0), _wspec(HEADS_W, D_MODEL, layer, 1),
            _wspec(S5_WIDTH, D_MODEL, layer, 2 * HEADS_W // S5_WIDTH)]


def _out_proj(dn, ml, s5, x, wts, prm, layer, tm):
    b, l, _ = x.shape
    nt = l // tm
    row = _rows3(nt)
    vec = pl.BlockSpec((1, D_MODEL), lambda i: (0, 0))
    return _dense_call(
        _out_proj_kernel, "out_proj",
        (jax.ShapeDtypeStruct((b, l, D_MODEL), F32), jax.ShapeDtypeStruct((b, l, X_WIDTH), F32)),
        [pl.BlockSpec((1, tm, HEADS_W), row), pl.BlockSpec((1, tm, HEADS_W), row),
         pl.BlockSpec((tm, S5_WIDTH), lambda i: (i % nt, i // nt)), pl.BlockSpec((1, tm, D_MODEL), row)]
        + _w_out_specs(layer) + [vec, _wspec(D_MODEL, X_WIDTH, layer)],
        (pl.BlockSpec((1, tm, D_MODEL), row), pl.BlockSpec((1, tm, X_WIDTH), row)), b * nt,
        (dn, ml, s5, x, wts["w_out"], wts["w_out"], wts["w_out"], prm["norm_x"], wts["w_xq"]))


def _ffn_kernel(x_ref, o_ref, wo_ref, g_ref, f1_ref, f2_ref, gl_ref, y_ref, *, final_norm):
    acc = _mlp_residual(x_ref[0] + _dot(o_ref[0], wo_ref[...]), g_ref, f1_ref, f2_ref)
    y_ref[0] = _rmsnorm_rows(acc, gl_ref[...]) if final_norm else acc


def _ffn(x, att, wts, prm, g_last, layer, tm, final_norm):
    b, l, _ = x.shape
    nt = l // tm
    row = _rows3(nt)
    vec = pl.BlockSpec((1, D_MODEL), lambda i: (0, 0))
    return _dense_call(
        functools.partial(_ffn_kernel, final_norm=final_norm), "ffn", jax.ShapeDtypeStruct((b, l, D_MODEL), F32),
        [pl.BlockSpec((1, tm, D_MODEL), row), pl.BlockSpec((1, tm, X_WIDTH), row), _wspec(X_WIDTH, D_MODEL, layer),
         vec, _wspec(D_MODEL, D_FF, layer), _wspec(D_FF, D_MODEL, layer), vec],
        pl.BlockSpec((1, tm, D_MODEL), row), b * nt,
        (x, att, wts["w_xo"], prm["norm_ff"], wts["w_ff1"], wts["w_ff2"], g_last))


def _attn_heads(q, k_ref, v_ref):
    outs = []
    for h in range(X_HEADS):
        sl = slice(h * X_HD, (h + 1) * X_HD)
        s = _dot(q[:, sl], k_ref[0, :, sl].astype(BF16), _NT) * (X_HD ** -0.5)
        e = jnp.exp(s - jnp.max(s, axis=-1, keepdims=True))
        p = e / jnp.sum(e, axis=-1, keepdims=True)
        outs.append(_dot(p.astype(BF16), v_ref[0, :, sl].astype(BF16)))
    return jnp.concatenate(outs, axis=-1)


def _attn_batch_kernel(q_ref, k_ref, v_ref, o_ref, *, bb):
    rows = X_HEADS * Q_ROWS
    col_head = _imod(_iota((rows, N_MEM * X_HEADS), 1), X_HEADS)
    row_head = _idiv(_iota((rows, N_MEM * X_HEADS), 0), Q_ROWS)
    own = col_head == row_head
    q4 = [jnp.concatenate([q_ref[bi, :, h * X_HD:(h + 1) * X_HD] for h in range(X_HEADS)], axis=0).astype(BF16)
          for bi in range(bb)]
    s = [jnp.where(own, _dot(q_, k_ref[bi].astype(BF16), _NT) * (X_HD ** -0.5), NEG)
         for bi, q_ in enumerate(q4)]
    e = [jnp.exp(x - jnp.max(x, axis=-1, keepdims=True)) for x in s]
    p = [x / jnp.sum(x, axis=-1, keepdims=True) for x in e]
    o = [_dot(x.astype(BF16), v_ref[bi].astype(BF16)) for bi, x in enumerate(p)]
    for bi in range(bb):
        for h in range(X_HEADS):
            o_ref[bi, :, h * X_HD:(h + 1) * X_HD] = o[bi][h * Q_ROWS:(h + 1) * Q_ROWS].astype(BF16)


def _attn_batch(q, k, v, layer, bb):
    b, l, _ = q.shape
    blk = lambda i: (i, 0, 0)
    kv = pl.BlockSpec((None, bb, N_MEM * X_HEADS, X_HD), lambda i: (layer, i, 0, 0))
    return _dense_call(
        functools.partial(_attn_batch_kernel, bb=bb), "attn_batch", jax.ShapeDtypeStruct((b, l, X_WIDTH), BF16),
        [pl.BlockSpec((bb, l, X_WIDTH), blk), kv, kv], pl.BlockSpec((bb, l, X_WIDTH), blk), b // bb, (q, k, v))


def _post_mix_kernel(dn_ref, ml_ref, s5_ref, x_ref, k_ref, v_ref, w1_ref, w2_ref, w3_ref, gx_ref, wq_ref,
                     wo_ref, gf_ref, f1_ref, f2_ref, gl_ref, y_ref, *, final_norm):
    x1 = _mix_residual(x_ref[0], dn_ref[0], ml_ref[0], s5_ref[...], w1_ref, w2_ref, w3_ref)
    q = _dot(_rmsnorm_rows(x1, gx_ref[...]).astype(BF16), wq_ref[...]).astype(BF16)
    att = _attn_heads(q, k_ref, v_ref).astype(BF16)
    acc = _mlp_residual(x1 + _dot(att, wo_ref[...]), gf_ref, f1_ref, f2_ref)
    y_ref[0] = _rmsnorm_rows(acc, gl_ref[...]) if final_norm else acc


def _post_mix(dn, ml, s5, x, mem_k, mem_v, wts, prm, g_last, layer, tm, final_norm):
    b, l, _ = x.shape
    nt = l // tm
    row = _rows3(nt)
    mem = pl.BlockSpec((None, 1, N_MEM, X_WIDTH), lambda i: (layer, i // nt, 0, 0))
    vec = pl.BlockSpec((1, D_MODEL), lambda i: (0, 0))
    return _dense_call(
        functools.partial(_post_mix_kernel, final_norm=final_norm), "post_mix",
        jax.ShapeDtypeStruct((b, l, D_MODEL), F32),
        [pl.BlockSpec((1, tm, HEADS_W), row), pl.BlockSpec((1, tm, HEADS_W), row),
         pl.BlockSpec((tm, S5_WIDTH), lambda i: (i % nt, i // nt)), pl.BlockSpec((1, tm, D_MODEL), row), mem, mem]
        + _w_out_specs(layer)
        + [vec, _wspec(D_MODEL, X_WIDTH, layer), _wspec(X_WIDTH, D_MODEL, layer), vec,
           _wspec(D_MODEL, D_FF, layer), _wspec(D_FF, D_MODEL, layer), vec],
        pl.BlockSpec((1, tm, D_MODEL), row), b * nt,
        (dn, ml, s5, x, mem_k, mem_v, wts["w_out"], wts["w_out"], wts["w_out"], prm["norm_x"], wts["w_xq"],
         wts["w_xo"], prm["norm_ff"], wts["w_ff1"], wts["w_ff2"], g_last))


def _s5_kernel(u_ref, h0r_ref, h0i_ref, lre_ref, lim_ref, ldt_ref, bre_ref, bim_ref, cre_ref, cim_ref,
               d_ref, wg_ref, bg_ref, y_ref, hr_out, hi_out,
               xr_s, xi_s, st_r, st_i, coef_s, bbr_s, bbi_s, *io_s, tb, nb, lane_chunk, seq_major_io):
    step = pl.program_id(0)
    if seq_major_io:
        u_s, y_s = io_s
        for b in range(nb):
            for j in range(S5_WIDTH // LANES):
                u_s[j, pl.ds(b, tb, stride=nb), :] = u_ref[b, :, j * LANES:(j + 1) * LANES]
        u = jnp.concatenate([u_s[j] for j in range(S5_WIDTH // LANES)], axis=1)
    else:
        u = u_ref[...]

    @pl.when(step == 0)
    def _():
        lr = lre_ref[...]
        li = lim_ref[...]
        dt = jnp.exp(ldt_ref[...])
        mag = jnp.exp(lr * dt)
        ar = mag * jnp.cos(li * dt)
        ai = mag * jnp.sin(li * dt)
        inv = 1.0 / (lr * lr + li * li)
        coef_s[0:1, :] = ar
        coef_s[1:2, :] = ai
        cr = ((ar - 1.0) * lr + ai * li) * inv
        ci = (ai * lr - (ar - 1.0) * li) * inv
        bbr_s[...] = (cr * bre_ref[...] - ci * bim_ref[...]).astype(BF16)
        bbi_s[...] = (cr * bim_ref[...] + ci * bre_ref[...]).astype(BF16)
        st_r[...] = h0r_ref[...]
        st_i[...] = h0i_ref[...]

    ub = u.astype(BF16)
    xr_s[...] = _dot(ub, bbr_s[...])
    xi_s[...] = _dot(ub, bbi_s[...])

    for lc in range(S5_STATE // lane_chunk):
        sl = slice(lc * lane_chunk, (lc + 1) * lane_chunk)
        ar = jnp.broadcast_to(coef_s[0:1, sl], (nb, lane_chunk))
        ai = jnp.broadcast_to(coef_s[1:2, sl], (nb, lane_chunk))

        def body(t, carry, sl=sl, ar=ar, ai=ai):
            hr, hi = carry
            r0 = pl.multiple_of(t * nb, 8)
            nhr = ar * hr - ai * hi + xr_s[pl.ds(r0, nb), sl]
            nhi = ar * hi + ai * hr + xi_s[pl.ds(r0, nb), sl]
            xr_s[pl.ds(r0, nb), sl] = nhr
            xi_s[pl.ds(r0, nb), sl] = nhi
            return nhr, nhi

        hr, hi = lax.fori_loop(0, tb, body, (st_r[:, sl], st_i[:, sl]), unroll=min(tb, 8))
        st_r[:, sl] = hr
        st_i[:, sl] = hi

    y = (_dot(xr_s[...].astype(BF16), cre_ref[...]) - _dot(xi_s[...].astype(BF16), cim_ref[...])
         + d_ref[...] * u)
    zg = 0.5 * y * (1.0 + jnp.tanh(math.sqrt(2.0 / math.pi) * (y + 0.044715 * (y * y * y))))
    yv = zg * _sigmoid(_dot(zg.astype(BF16), wg_ref[...]) + bg_ref[...])
    if seq_major_io:
        for j in range(S5_WIDTH // LANES):
            y_s[j] = yv[:, j * LANES:(j + 1) * LANES]
        for b in range(nb):
            for j in range(S5_WIDTH // LANES):
                c0 = b * S5_WIDTH + j * LANES
                y_ref[:, c0:c0 + LANES] = y_s[j, pl.ds(b, tb, stride=nb), :].astype(BF16)
    else:
        y_ref[...] = yv.astype(BF16)

    @pl.when(step == pl.num_programs(0) - 1)
    def _():
        hr_out[...] = st_r[...]
        hi_out[...] = st_i[...]


def _s5(u, h0r, h0i, prm, tb, lane_chunk, seq_major_io):
    nb = h0r.shape[0]
    fix2 = lambda i: (0, 0)
    vec = pl.BlockSpec((1, S5_STATE), fix2)
    st = pl.BlockSpec((nb, S5_STATE), fix2)
    rows = tb * nb
    scratch = [pltpu.VMEM((rows, S5_STATE), F32), pltpu.VMEM((rows, S5_STATE), F32),
               pltpu.VMEM((nb, S5_STATE), F32), pltpu.VMEM((nb, S5_STATE), F32),
               pltpu.VMEM((8, S5_STATE), F32),
               pltpu.VMEM((S5_WIDTH, S5_STATE), BF16), pltpu.VMEM((S5_WIDTH, S5_STATE), BF16)]
    if seq_major_io:
        l = u.shape[1]
        u_spec = pl.BlockSpec((nb, tb, S5_WIDTH), lambda i: (0, i, S5_COL_BLOCK))
        y_shape = jax.ShapeDtypeStruct((l, nb * S5_WIDTH), BF16)
        y_spec = pl.BlockSpec((tb, nb * S5_WIDTH), lambda i: (i, 0))
        io_shape = (S5_WIDTH // LANES, rows, LANES)
        scratch += [pltpu.VMEM(io_shape, F32), pltpu.VMEM(io_shape, F32)]
    else:
        l = u.shape[0] // nb
        u_spec = pl.BlockSpec((rows, S5_WIDTH), lambda i: (i, 0))
        y_shape = jax.ShapeDtypeStruct((l * nb, S5_WIDTH), BF16)
        y_spec = pl.BlockSpec((rows, S5_WIDTH), lambda i: (i, 0))
    return pl.pallas_call(
        functools.partial(_s5_kernel, tb=tb, nb=nb, lane_chunk=lane_chunk, seq_major_io=seq_major_io),
        out_shape=(y_shape, jax.ShapeDtypeStruct((nb, S5_STATE), F32), jax.ShapeDtypeStruct((nb, S5_STATE), F32)),
        grid=(l // tb,),
        in_specs=[u_spec, st, st, vec, vec, vec,
                  pl.BlockSpec((S5_WIDTH, S5_STATE), fix2), pl.BlockSpec((S5_WIDTH, S5_STATE), fix2),
                  pl.BlockSpec((S5_STATE, S5_WIDTH), fix2), pl.BlockSpec((S5_STATE, S5_WIDTH), fix2),
                  pl.BlockSpec((1, S5_WIDTH), fix2), pl.BlockSpec((S5_WIDTH, S5_WIDTH), fix2),
                  pl.BlockSpec((1, S5_WIDTH), fix2)],
        out_specs=(y_spec, st, st),
        scratch_shapes=scratch,
        compiler_params=pltpu.CompilerParams(dimension_semantics=("arbitrary",),
                                             vmem_limit_bytes=VMEM_LIMIT),
        name="s5",
    )(u, h0r, h0i, prm["lre"], prm["lim"], prm["ldt"], prm["bre"], prm["bim"], prm["cre"],
      prm["cim"], prm["d"], prm["wg"], prm["bg"])


def _l2norm_pairs(x_chunks, bones, scale):
    n = len(x_chunks)
    x = jnp.concatenate(x_chunks, axis=0) if n > 1 else x_chunks[0]
    out = [[None] * N_PAIRS for _ in range(n)]
    for p_ in range(N_PAIRS):
        xp = x[:, p_ * LANES:(p_ + 1) * LANES]
        xn = xp * (lax.rsqrt(_seg_mean(xp * xp, bones) * HEAD_DIM + EPS) * scale)
        for c in range(n):
            out[c][p_] = xn[c * CHUNK:(c + 1) * CHUNK]
    return out


def _head_norm_store(vals, gates, normw, bones, act, store):
    n = len(vals)
    for p_ in range(N_PAIRS):
        v = jnp.concatenate([vals[c][p_] for c in range(n)], axis=0) if n > 1 else vals[0][p_]
        g = jnp.concatenate([gates[c][p_] for c in range(n)], axis=0) if n > 1 else gates[0][p_]
        y = v * lax.rsqrt(_seg_mean(v * v, bones) + EPS) * normw * act(g)
        for c in range(n):
            store(c, p_, y[c * CHUNK:(c + 1) * CHUNK].astype(BF16))


def _dn_local(chunks, prm, prec):
    n = len(chunks)
    bones = prm["bones"]
    incl, strict = _pair_masks()
    ltri, upair = _ltri(), _upair()
    units = [(c, p_) for c in range(n) for p_ in range(N_PAIRS)]

    g_cols = _mm_sel_l(ltri, jnp.concatenate([-jnp.exp(prm["alog_row"]) * _softplus(ch["zs"]) for ch in chunks],
                                             axis=1))
    g_rows = _mm_sel_r(jnp.concatenate([-jnp.exp(prm["rowalog"]) * _softplus(ch["rows"] + prm["rowadd"])
                                        for ch in chunks], axis=0), upair)
    gates = [(_sigmoid(ch["zs"]), g_cols[:, c * LANES:(c + 1) * LANES], g_rows[c * 16:(c + 1) * 16])
             for c, ch in enumerate(chunks)]

    qn = _l2norm_pairs([ch["q"] for ch in chunks], bones, HEAD_DIM ** -0.5)
    kn = _l2norm_pairs([ch["k"] for ch in chunks], bones, 1.0)

    beta = [_expand(gates[c][0], 2 * p_, 2 * p_ + 1) for c, p_ in units]
    g_col = [_expand(gates[c][1], 6 + 2 * p_, 7 + 2 * p_) for c, p_ in units]
    dec = [jnp.exp(jnp.where(incl, gc - gates[c][2][p_:p_ + 1, :], NEG)) for gc, (c, p_) in zip(g_col, units)]
    kb = [kn[c][p_] * b for b, (c, p_) in zip(beta, units)]
    sc = [_mm(jnp.concatenate([kb_, qn[c][p_]], axis=0), _x2b(kn[c][p_]), _NT)
          for kb_, (c, p_) in zip(kb, units)]
    a_in = [s[CHUNK:] * d for s, d in zip(sc, dec)]
    mbd = [_x2(jnp.where(strict, s[:CHUNK] * d, 0.0)) for s, d in zip(sc, dec)]
    tri_prec, newton, sol_prec = prec
    t_bd = _tri_inverse_units(mbd, tri_prec, newton)
    eg = [jnp.exp(gc) for gc in g_col]
    x2r = _x2b if sol_prec == 1 else _x2
    rhs = [jnp.concatenate([x2r(chunks[c]["v"][:, p_ * LANES:(p_ + 1) * LANES] * b), x2r(kb_ * e)], axis=1)
           for b, kb_, e, (c, p_) in zip(beta, kb, eg, units)]
    sol = [_mm(t, r, prec=sol_prec) for t, r in zip(t_bd, rhs)]
    out = [[None] * N_PAIRS for _ in range(n)]
    for i, (c, p_) in enumerate(units):
        g_last = _last_row(g_col[i])
        out[c][p_] = dict(u=_fold(sol[i][:, :LANES]), w=_fold(sol[i][:, LANES:]), a_in=a_in[i],
                          qg=qn[c][p_] * eg[i], kd=kn[c][p_] * jnp.exp(g_last - g_col[i]),
                          eg_last=jnp.exp(g_last))
    return out


def _dn_seq_step(zq_ref, zk_ref, zv_ref, zg_ref, zs_ref, rows_ref,
                 add_ref, alog_ref, rowadd_ref, rowalog_ref, nw_ref, o_ref, s_scr, *, bb, prec):
    chunks = [dict(q=zq_ref[b], k=zk_ref[b], v=zv_ref[b], zs=zs_ref[b] + add_ref[...], rows=rows_ref[b, 0])
              for b in range(bb)]

    prm = dict(bones=_block_ones(), alog_row=alog_ref[...], rowadd=rowadd_ref[...],
               rowalog=rowalog_ref[...])
    loc = _dn_local(chunks, prm, prec)
    bdm = _bd_mask()
    units = [(b, p_) for b in range(bb) for p_ in range(N_PAIRS)]
    s_prev = [s_scr[b, p_] for b, p_ in units]
    ws = [_mm(jnp.concatenate([loc[b][p_]["w"], loc[b][p_]["qg"]], axis=0), s)
          for s, (b, p_) in zip(s_prev, units)]
    v_new = [loc[b][p_]["u"] - w_[:CHUNK] for w_, (b, p_) in zip(ws, units)]
    o = [w_[CHUNK:] + _mm(loc[b][p_]["a_in"], _x2b(v)) for w_, v, (b, p_) in zip(ws, v_new, units)]
    upd = [_mm(loc[b][p_]["kd"].T, v) for v, (b, p_) in zip(v_new, units)]
    for i, (b, p_) in enumerate(units):
        s_scr[b, p_] = loc[b][p_]["eg_last"][0:1, :] * s_prev[i] + jnp.where(bdm, upd[i], 0.0)

    def store(b, p_, val):
        o_ref[b, :, p_ * LANES:(p_ + 1) * LANES] = val

    _head_norm_store([[o[b * N_PAIRS + p_] for p_ in range(N_PAIRS)] for b in range(bb)],
                     [[zg_ref[b, :, p_ * LANES:(p_ + 1) * LANES] for p_ in range(N_PAIRS)] for b in range(bb)],
                     nw_ref[...], prm["bones"], _silu, store)


def _ml_units(chunks, prm):
    n = len(chunks)
    incl, _ = _pair_masks()
    lo = _lane_lo((CHUNK, LANES))
    ltri, upair = _ltri(), _upair()
    ones2 = prm["bones"]
    bdm2 = jnp.concatenate([_bd_mask(), _bd_mask()], axis=1)
    units = [(c, p_) for c in range(n) for p_ in range(N_PAIRS)]
    pair = lambda x, p_: x[:, p_ * LANES:(p_ + 1) * LANES]

    rows = [ch["rows"] + prm["rowadd"] for ch in chunks]
    b_cols = _mm_sel_l(ltri, jnp.concatenate([-_softplus(-ch["zs"]) for ch in chunks], axis=1))
    b_rows = _mm_sel_r(jnp.concatenate([-_softplus(-r) for r in rows], axis=0), upair)
    comp = [dict(ig_c=ch["zs"], b_c=b_cols[:, c * LANES:(c + 1) * LANES], ig_r=rows[c],
                 b_r=b_rows[c * 16:(c + 1) * 16]) for c, ch in enumerate(chunks)]

    qp = [pair(chunks[c]["q"], p_) for c, p_ in units]
    kp = [pair(chunks[c]["k"], p_) * (HEAD_DIM ** -0.5) for c, p_ in units]
    vp = [pair(chunks[c]["v"], p_) for c, p_ in units]
    b_col = [_expand(comp[c]["b_c"], 18 + 2 * p_, 19 + 2 * p_) for c, p_ in units]
    ig_col = [_expand(comp[c]["ig_c"], 12 + 2 * p_, 13 + 2 * p_) for c, p_ in units]
    dl = [jnp.where(incl, bc - comp[c]["b_r"][6 + p_:7 + p_, :] + comp[c]["ig_r"][3 + p_:4 + p_, :], NEG)
          for bc, (c, p_) in zip(b_col, units)]
    mx = [jnp.where(lo, jnp.max(jnp.where(lo, d, NEG), axis=1, keepdims=True),
                    jnp.max(jnp.where(lo, NEG, d), axis=1, keepdims=True)) for d in dl]
    inter = [bc + chunks[c]["ms"][p_] for bc, (c, p_) in zip(b_col, units)]
    mt = [jnp.maximum(a, b) for a, b in zip(inter, mx)]
    s = [_mm(q_, _x2b(k_), _NT) * jnp.exp(d - m) for q_, k_, d, m in zip(qp, kp, dl, mt)]
    sv = [_mm(s_, jnp.concatenate([_x2b(v_), ones2], axis=1)) for s_, v_ in zip(s, vp)]
    cn_prev = [chunks[c]["cn"][p_] for c, p_ in units]
    qcn = [_mm(q_, cn) for q_, cn in zip(qp, cn_prev)]
    w_inter = [jnp.exp(a - m) for a, m in zip(inter, mt)]
    h = [(wi * qc[:, :LANES] + sv_[:, :LANES])
         / jnp.maximum(jnp.abs(wi * qc[:, LANES:] + sv_[:, LANES:]), jnp.exp(-m))
         for wi, qc, sv_, m in zip(w_inter, qcn, sv, mt)]
    m_end = [_last_row(m) for m in mt]
    b_last = [_last_row(bc) for bc in b_col]
    a_end = [jnp.exp(bl + chunks[c]["ms"][p_] - me) for bl, me, (c, p_) in zip(b_last, m_end, units)]
    kw_t = [(k_ * jnp.exp(bl - bc + ig - me)).T for k_, bl, bc, ig, me in zip(kp, b_last, b_col, ig_col, m_end)]
    upd = [_mm(k_, jnp.concatenate([v_, jnp.ones((CHUNK, LANES), F32)], axis=1)) for k_, v_ in zip(kw_t, vp)]
    out = [[None] * N_PAIRS for _ in range(n)]
    for i, (c, p_) in enumerate(units):
        a_row = a_end[i][0:1, :]
        new_state = jnp.concatenate([a_row, a_row], axis=1) * cn_prev[i] + jnp.where(bdm2, upd[i], 0.0)
        out[c][p_] = (h[i], new_state, m_end[i][0:1, :])
    return out


def _ml_seq_step(zq_ref, zk_ref, zv_ref, zo_ref, zs_ref, rows_ref, add_ref, rowadd_ref, nw_ref, o_ref,
                 cn_scr, m_scr, *, bb):
    prm = dict(bones=_block_ones(), rowadd=rowadd_ref[...])
    chunks = []
    for b in range(bb):
        chunks.append(dict(q=zq_ref[b], k=zk_ref[b], v=zv_ref[b], zs=zs_ref[b] + add_ref[...],
                           rows=rows_ref[b, 0],
                           ms=[m_scr[b, p_:p_ + 1, :] for p_ in range(N_PAIRS)],
                           cn=[cn_scr[b, p_] for p_ in range(N_PAIRS)]))
    res = _ml_units(chunks, prm)
    for b in range(bb):
        for p_ in range(N_PAIRS):
            cn_scr[b, p_] = res[b][p_][1]
            m_scr[b, p_:p_ + 1, :] = res[b][p_][2]

    def store(b, p_, val):
        o_ref[b, :, p_ * LANES:(p_ + 1) * LANES] = val

    _head_norm_store([[res[b][p_][0] for p_ in range(N_PAIRS)] for b in range(bb)],
                     [[zo_ref[b, :, p_ * LANES:(p_ + 1) * LANES] for p_ in range(N_PAIRS)] for b in range(bb)],
                     nw_ref[...], prm["bones"], _sigmoid, store)


def _mixers_seq_kernel(dq_ref, dk_ref, dv_ref, dg_ref, lq_ref, lk_ref, lv_ref, lo_ref, zs_ref, rows_ref,
                       s0_ref, cn0_ref, m0_ref,
                       add_ref, alog_ref, rowadd_ref, rowalog_ref, dnw_ref, mlw_ref,
                       dn_ref, ml_ref, s_out_ref, cn_out_ref, m_out_ref,
                       s_scr, cn_scr, m_scr, *, bb, prec):
    tstep = pl.program_id(1)

    @pl.when(tstep == 0)
    def _():
        s_scr[...] = s0_ref[...]
        cn_scr[...] = cn0_ref[...]
        m_scr[...] = m0_ref[...]

    _dn_seq_step(dq_ref, dk_ref, dv_ref, dg_ref, zs_ref, rows_ref, add_ref, alog_ref, rowadd_ref,
                 rowalog_ref, dnw_ref, dn_ref, s_scr, bb=bb, prec=prec)
    _ml_seq_step(lq_ref, lk_ref, lv_ref, lo_ref, zs_ref, rows_ref, add_ref, rowadd_ref, mlw_ref, ml_ref,
                 cn_scr, m_scr, bb=bb)

    @pl.when(tstep == pl.num_programs(1) - 1)
    def _():
        s_out_ref[...] = s_scr[...]
        cn_out_ref[...] = cn_scr[...]
        m_out_ref[...] = m_scr[...]


def _mixers_seq(z3, zc, rows4, s0, cn0, m0, prm, bb, prec):
    batch, seq_len, _ = z3.shape
    fix2 = lambda g, i: (0, 0)
    zspec = lambda width, col: pl.BlockSpec((bb, CHUNK, width), lambda g, i: (g, i, col))
    per_seq = lambda *shape: pl.BlockSpec((bb,) + shape, lambda g, i: (g,) + (0,) * len(shape))
    s_spec, cn_spec, m_spec = per_seq(N_PAIRS, LANES, LANES), per_seq(N_PAIRS, LANES, 2 * LANES), per_seq(8, LANES)
    small = lambda rows: pl.BlockSpec((rows, LANES), fix2)
    in_specs = ([zspec(HEADS_W, c) for c in range(8)] + [zspec(LANES, GATE_COL_BLOCK)]
                + [pl.BlockSpec((bb, 1, 16, LANES), lambda g, i: (g, i, 0, 0)), s_spec, cn_spec, m_spec,
                   small(1), small(1), small(16), small(16), small(1), small(1)])
    out_blk = pl.BlockSpec((bb, CHUNK, HEADS_W), lambda g, i: (g, i, 0))
    act = jax.ShapeDtypeStruct((batch, seq_len, HEADS_W), BF16)
    return pl.pallas_call(
        functools.partial(_mixers_seq_kernel, bb=bb, prec=prec),
        out_shape=(act, act, jax.ShapeDtypeStruct(s0.shape, F32), jax.ShapeDtypeStruct(cn0.shape, F32),
                   jax.ShapeDtypeStruct(m0.shape, F32)),
        grid=(batch // bb, seq_len // CHUNK),
        in_specs=in_specs,
        out_specs=(out_blk, out_blk, s_spec, cn_spec, m_spec),
        scratch_shapes=[pltpu.VMEM((bb, N_PAIRS, LANES, LANES), F32),
                        pltpu.VMEM((bb, N_PAIRS, LANES, 2 * LANES), F32), pltpu.VMEM((bb, 8, LANES), F32)],
        compiler_params=pltpu.CompilerParams(dimension_semantics=("parallel", "arbitrary"),
                                             vmem_limit_bytes=VMEM_LIMIT),
        name="mixers_seq",
    )(zc, zc, zc, *([z3] * 6), rows4, s0, cn0, m0, prm["add_row"], prm["alog_row"],
      prm["rowadd"], prm["rowalog"], prm["dn_normw"], prm["ml_normw"])


def _lane_mixers_kernel(zq, zk, zv, zg, lq, lk, lv, lo, zs, cpq, cpk, cpv, cwq, cwk, cwv, bias, alog, dnw, mlw,
                        s0, c0, n0, m0, *rest, n_tok, n_prev):
    if n_prev:
        s_prev, c_prev, dn_o, ml_o, s_st, c_st, n_o, m_o, k_s, q_s = rest
        s_st[0:n_prev] = s_prev[...]
        c_st[0:n_prev] = c_prev[...]
    else:
        dn_o, ml_o, s_st, c_st, n_o, m_o, k_s, q_s = rest
    s_o, c_o = s_st.at[n_prev], c_st.at[n_prev]
    h = pl.program_id(0)
    nb = s_o.shape[-1]
    zrow = lambda t, r: zs[t, pl.ds(r, 1), :]
    prow = lambda ref, r: ref[pl.ds(r, 1), :]
    bcast = lambda ref, dk: jnp.broadcast_to(ref[pl.ds(dk, 1), :], (HEAD_DIM, nb))
    zeros = jnp.zeros((HEAD_DIM, nb), F32)

    def conv(z_ref, cp_ref, cw_ref):
        e = [cp_ref[j] for j in range(CONV_W - 1)] + [z_ref[t] for t in range(n_tok)]
        outs = []
        for t in range(n_tok):
            acc = e[t] * cw_ref[0]
            for j in range(1, CONV_W):
                acc = acc + e[t + j] * cw_ref[j]
            outs.append(_silu(acc))
        return outs

    def l2n(x, scale):
        return x * (lax.rsqrt(jnp.sum(x * x, axis=0, keepdims=True) + EPS) * scale)

    def head_norm(x, w):
        return x * lax.rsqrt(jnp.mean(x * x, axis=0, keepdims=True) + EPS) * w

    qc, kc, vc = conv(zq, cpq, cwq), conv(zk, cpk, cwk), conv(zv, cpv, cwv)
    s_o[...] = s0[...]
    for t in range(n_tok):
        k_s[...] = l2n(kc[t], 1.0)
        q_s[...] = l2n(qc[t], HEAD_DIM ** -0.5)
        beta = _sigmoid(zrow(t, h))
        a = jnp.exp(-jnp.exp(prow(alog, N_HEADS + h))
                    * _softplus(zrow(t, N_HEADS + h) + prow(bias, N_HEADS + h)))
        ks = lax.fori_loop(0, HEAD_DIM, lambda dk, acc: acc + bcast(k_s, dk) * s_o[dk], zeros, unroll=8)
        delta = beta * (vc[t] - a * ks)

        def dn_update(dk, acc, a=a, delta=delta):
            s_new = a * s_o[dk] + bcast(k_s, dk) * delta
            s_o[dk] = s_new
            return acc + bcast(q_s, dk) * s_new

        o = lax.fori_loop(0, HEAD_DIM, dn_update, zeros, unroll=8)
        dn_o[t] = (head_norm(o, dnw[...]) * _silu(zg[t])).astype(BF16)

    c_o[...] = c0[...]
    n = n0[...]
    m = m0[...]
    for t in range(n_tok):
        q, k, v = lq[t], lk[t] * (HEAD_DIM ** -0.5), lv[t]
        ig = zrow(t, 2 * N_HEADS + h) + prow(bias, 2 * N_HEADS + h)
        lsf = -_softplus(-(zrow(t, 3 * N_HEADS + h) + prow(bias, 3 * N_HEADS + h)))
        m_new = jnp.maximum(lsf + m, ig)
        fp = jnp.exp(lsf + m - m_new)
        ip = jnp.exp(ig - m_new)
        k_s[...] = k * ip
        q_s[...] = q

        def ml_update(dk, acc, fp=fp, v=v):
            c_new = fp * c_o[dk] + bcast(k_s, dk) * v
            c_o[dk] = c_new
            return acc + bcast(q_s, dk) * c_new

        num = lax.fori_loop(0, HEAD_DIM, ml_update, zeros, unroll=8)
        n = fp * n + ip * k
        den = jnp.sum(q * n, axis=0, keepdims=True)
        hh = num / jnp.maximum(jnp.abs(den), jnp.exp(-m_new))
        m = m_new
        ml_o[t] = (head_norm(hh, mlw[...]) * _sigmoid(lo[t])).astype(BF16)
    n_o[...] = n
    m_o[...] = m


def _lane_mixers(z_t, conv_t, s0, c0, n0, m0, layer, prm, prev):
    n_tok, _, nb = z_t.shape
    hd = HEAD_DIM
    n_prev = 0 if prev is None else prev[0].shape[0]
    assert n_prev == layer
    zblk = lambda first: pl.BlockSpec((n_tok, hd, nb), lambda h: (0, first + h, 0))
    cblk = lambda taps, seg: pl.BlockSpec((taps, hd, nb), lambda h: (0, seg * N_HEADS + h, 0))
    wblk = lambda seg: pl.BlockSpec((CONV_W, hd, nb), lambda h: (0, seg * N_HEADS + h, 0))
    whole = lambda shape: pl.BlockSpec(shape, lambda h: (0,) * len(shape))
    mat_in = pl.BlockSpec((None, None, hd, hd, nb), lambda h: (layer, h, 0, 0, 0))
    mat_out = pl.BlockSpec((n_prev + 1, None, hd, hd, nb), lambda h: (0, h, 0, 0, 0))
    prev_specs = [pl.BlockSpec((n_prev, None, hd, hd, nb), lambda h: (0, h, 0, 0, 0))] * 2 if n_prev else []
    small_rows = 32
    in_specs = [zblk(seg * N_HEADS) for seg in range(8)] + [
        pl.BlockSpec((n_tok, small_rows, nb), lambda h: (0, (Z_COLS - LANES) // small_rows, 0)),
        cblk(CONV_W - 1, 0), cblk(CONV_W - 1, 1), cblk(CONV_W - 1, 2), wblk(0), wblk(1), wblk(2),
        whole((small_rows, nb)), whole((small_rows, nb)), whole((hd, nb)), whole((hd, nb)),
        mat_in, mat_in,
        pl.BlockSpec((None, None, hd, nb), lambda h: (layer, h, 0, 0)),
        pl.BlockSpec((None, None, 1, nb), lambda h: (layer, h, 0, 0))] + prev_specs
    out_blk = pl.BlockSpec((n_tok, hd, nb), lambda h: (0, h, 0))
    act = jax.ShapeDtypeStruct((n_tok, HEADS_W, nb), BF16)
    mat = jax.ShapeDtypeStruct((n_prev + 1, N_HEADS, hd, hd, nb), F32)
    return pl.pallas_call(
        functools.partial(_lane_mixers_kernel, n_tok=n_tok, n_prev=n_prev),
        out_shape=(act, act, mat, mat, jax.ShapeDtypeStruct((N_HEADS, hd, nb), F32),
                   jax.ShapeDtypeStruct((N_HEADS, 1, nb), F32)),
        grid=(N_HEADS,),
        in_specs=in_specs,
        out_specs=(out_blk, out_blk, mat_out, mat_out,
                   pl.BlockSpec((None, hd, nb), lambda h: (h, 0, 0)), pl.BlockSpec((None, 1, nb), lambda h: (h, 0, 0))),
        scratch_shapes=[pltpu.VMEM((hd, nb), F32), pltpu.VMEM((hd, nb), F32)],
        compiler_params=pltpu.CompilerParams(dimension_semantics=("parallel",), vmem_limit_bytes=VMEM_LIMIT),
        name="lane_mixers",
    )(*([z_t] * 9), conv_t, conv_t, conv_t, prm["conv_wb"], prm["conv_wb"], prm["conv_wb"], prm["bias_b"],
      prm["alog_b"], prm["dn_normb"], prm["ml_normb"], s0, c0, n0, m0, *(prev or ()))


_IN_OFFS = [int(v) for v in np.concatenate([[0], np.cumsum(IN_SIZES)])]
PACK_ROWS = 256


def _pack_w_kernel(wt_ref, o_ref):
    o = _IN_OFFS
    seg_t = lambda a, b: wt_ref[0, a:b, :].T
    b0 = (o[9] // LANES) * LANES
    assert o[3] % LANES == 0 and o[9] - b0 == 2 * N_HEADS
    t_a = seg_t(o[3], o[3] + LANES)
    t_b = seg_t(b0, b0 + LANES)
    lane = _iota(t_a.shape, 1)
    small = jnp.where(lane < 2 * N_HEADS, t_a, jnp.where(lane < 4 * N_HEADS, t_b, 0.0))
    cols = [seg_t(o[0], o[3]), seg_t(o[5], o[9]), seg_t(o[11], o[13]), small]
    o_ref[0] = jnp.concatenate(cols, axis=-1).astype(BF16)


def _pack_w_in(w_in):
    depth, k, n = w_in.shape
    w_t = jnp.transpose(w_in, (0, 2, 1))
    return pl.pallas_call(
        _pack_w_kernel, out_shape=jax.ShapeDtypeStruct((depth, k, Z_COLS), BF16),
        grid=(depth, k // PACK_ROWS),
        in_specs=[pl.BlockSpec((1, n, PACK_ROWS), lambda l, i: (l, 0, i))],
        out_specs=pl.BlockSpec((1, PACK_ROWS, Z_COLS), lambda l, i: (l, i, 0)),
        compiler_params=pltpu.CompilerParams(dimension_semantics=("parallel", "parallel"),
                                             vmem_limit_bytes=VMEM_LIMIT),
        name="pack_w_in",
    )(w_t)


def _pair_rows(v6):
    return jnp.repeat(v6.reshape(N_PAIRS, 2), HEAD_DIM, axis=1).reshape(N_PAIRS, LANES)


def _layer_params(l, a):
    f = lambda name: a[name][l].astype(F32)
    zeros6 = jnp.zeros((N_HEADS,), F32)
    add_row = jnp.concatenate([zeros6, f("dn_dt_bias"), f("ml_i_bias"), f("ml_f_bias")])
    add_row = jnp.pad(add_row, (0, LANES - 4 * N_HEADS))[None, :]
    alog_row = jnp.pad(jnp.concatenate([zeros6, f("dn_a_log")]), (0, LANES - 2 * N_HEADS))[None, :]
    rowadd = jnp.concatenate([_pair_rows(f("dn_dt_bias")), _pair_rows(f("ml_i_bias")),
                              _pair_rows(f("ml_f_bias")), jnp.zeros((7, LANES), F32)], axis=0)
    rowalog = jnp.concatenate([_pair_rows(f("dn_a_log")), jnp.zeros((13, LANES), F32)], axis=0)
    eye_g = jnp.eye(S5_GROUPS, dtype=F32)
    bd_in = lambda b: jnp.einsum("gpc,gh->gchp", b, eye_g).reshape(S5_WIDTH, S5_STATE)
    bd_out = lambda c: jnp.einsum("gcp,gh->gphc", c, eye_g).reshape(S5_STATE, S5_WIDTH).astype(BF16)
    return dict(
        layer=l, norm_mix=f("norm_mix")[None, :],
        add_row=add_row, alog_row=alog_row, rowadd=rowadd, rowalog=rowalog,
        conv_w8=jnp.pad(f("dn_conv_w"), ((0, 8 - CONV_W), (0, 0))),
        dn_normw=jnp.tile(f("dn_norm"), 2)[None, :], ml_normw=jnp.tile(f("ml_norm"), 2)[None, :],
        s5=dict(lre=f("s5_lam_re").reshape(1, S5_STATE), lim=f("s5_lam_im").reshape(1, S5_STATE),
                ldt=jnp.repeat(f("s5_log_dt"), S5_P)[None, :],
                bre=bd_in(f("s5_b_re")), bim=bd_in(f("s5_b_im")),
                cre=bd_out(f("s5_c_re")), cim=bd_out(f("s5_c_im")),
                d=f("s5_d").reshape(1, S5_WIDTH), wg=f("s5_w_glu").astype(BF16),
                bg=f("s5_b_glu")[None, :]),
        norm_x=f("norm_x")[None, :], norm_ff=f("norm_ff")[None, :],
    )


def _shared_weights(a):
    cast = lambda name: a[name].astype(BF16)
    return dict(w_in=_pack_w_in(a["w_in"]), w_out=cast("w_out"), w_xq=cast("w_xq"), w_xo=cast("w_xo"),
                w_ff1=cast("w_ff1"), w_ff2=cast("w_ff2"), w_xk=cast("w_xk"), w_xv=cast("w_xv"))


def _row_form(z, n_chunks):
    zs = z[..., Z_COLS - LANES:Z_COLS - LANES + 4 * N_HEADS].reshape(n_chunks, CHUNK, 4, N_PAIRS, 2)
    r = jnp.transpose(zs[:, :, 1:], (0, 2, 3, 4, 1)).reshape(n_chunks, 9, LANES)
    return jnp.pad(r, ((0, 0), (0, 7), (0, 0)))


def _un_bd_pairs(s):
    n = s.shape[0]
    a = s[:, :, :HEAD_DIM, :HEAD_DIM]
    b = s[:, :, HEAD_DIM:, HEAD_DIM:]
    return jnp.stack([a, b], axis=2).reshape(n, N_HEADS, HEAD_DIM, HEAD_DIM)


def _unpack_m(mp):
    return mp[:, :N_PAIRS, ::HEAD_DIM].reshape(mp.shape[0], N_HEADS)


def _tile(n, pref):
    tile = min(n, pref)
    assert n % tile == 0, (n, pref)
    return tile


def _layer_prompt(x, mem_k, mem_v, prm, wts, final_g):
    batch, seq_len, _ = x.shape
    layer = prm["layer"]
    z3, zc = _norm_matmul_conv(x, prm["norm_mix"], wts["w_in"], jnp.zeros((batch, 8, CONV_CH), F32),
                               prm["conv_w8"], layer, _tile(seq_len, CONV_ROW_TILE))
    rows4 = _row_form(z3, batch * seq_len // CHUNK).reshape(batch, seq_len // CHUNK, 16, LANES)
    new_conv = z3[:, seq_len - (CONV_W - 1):, :CONV_CH]

    dn, ml, s_new, cn_new, m_new = _mixers_seq(
        z3, zc, rows4, jnp.zeros((batch, N_PAIRS, LANES, LANES), F32),
        jnp.zeros((batch, N_PAIRS, LANES, 2 * LANES), F32), jnp.zeros((batch, 8, LANES), F32), prm,
        _tile(batch, SEQ_BATCH), SEQ_SOLVE)
    zero_h = jnp.zeros((batch, S5_STATE), F32)
    s5o, hr, hi = _s5(z3, zero_h, zero_h, prm["s5"], _tile(seq_len, S5_STEPS), S5_STATE, True)

    g_fin = final_g if final_g is not None else prm["norm_ff"]
    x3 = _post_mix(dn, ml, s5o, x, mem_k, mem_v, wts, prm, g_fin, layer, _tile(seq_len, POST_TILE),
                   final_g is not None)
    st = (new_conv, _un_bd_pairs(s_new), _un_bd_pairs(cn_new[..., :LANES]),
          _un_bd_pairs(cn_new[..., LANES:])[..., 0], _unpack_m(m_new),
          hr.reshape(batch, S5_GROUPS, S5_P), hi.reshape(batch, S5_GROUPS, S5_P))
    return x3, st


def _layer_sample(x, cache_k, cache_v, conv_buf, dn_s, ml_c, ml_n, ml_m, ssm_re, ssm_im, prm, wts, batch,
                  seq_len, final_g, prev_stacks):
    t = batch * seq_len
    layer = prm["layer"]
    z = _norm_matmul(x, prm["norm_mix"], wts["w_in"], layer, t)
    z3 = z.reshape(seq_len, batch, Z_COLS)
    new_conv = jnp.concatenate([conv_buf, jnp.transpose(z3[:, :, :CONV_CH], (1, 0, 2))], axis=1)[:, seq_len:]
    lanes = lambda v: jnp.broadcast_to(v[..., None], v.shape + (batch,))
    mix_prm = dict(conv_wb=lanes(prm["conv_w8"][:CONV_W]), bias_b=lanes(prm["add_row"][0, :32]),
                   alog_b=lanes(prm["alog_row"][0, :32]), dn_normb=lanes(prm["dn_normw"][0, :HEAD_DIM]),
                   ml_normb=lanes(prm["ml_normw"][0, :HEAD_DIM]))
    dn_t, ml_t, s_stack, c_stack, n_new, m_new = _lane_mixers(
        jnp.transpose(z3, (0, 2, 1)), jnp.transpose(conv_buf, (1, 2, 0)), dn_s, ml_c, ml_n, ml_m, layer, mix_prm,
        prev_stacks)
    to_rows = lambda a: jnp.transpose(a, (0, 2, 1)).reshape(1, t, HEADS_W)
    dn, ml = to_rows(dn_t), to_rows(ml_t)

    s5o, hr, hi = _s5(z[0, :, 3072:3072 + S5_WIDTH], ssm_re.reshape(batch, S5_STATE),
                      ssm_im.reshape(batch, S5_STATE), prm["s5"], seq_len, LANES, False)

    x1, q = _out_proj(dn, ml, s5o, x, wts, prm, layer, t)
    q3 = jnp.pad(jnp.transpose(q.reshape(seq_len, batch, X_WIDTH), (1, 0, 2)),
                 ((0, 0), (0, Q_ROWS - seq_len), (0, 0)))
    att = _attn_batch(q3, cache_k, cache_v, layer, _tile(batch, ATTN_BATCH))[:, :seq_len]
    att = jnp.transpose(att, (1, 0, 2)).reshape(1, t, X_WIDTH)
    g_fin = final_g if final_g is not None else prm["norm_ff"]
    x3 = _ffn(x1, att, wts, prm, g_fin, layer, t, final_g is not None)
    st = (new_conv, jnp.transpose(n_new, (2, 0, 1)), jnp.transpose(m_new[:, 0, :], (1, 0)),
          hr.reshape(batch, S5_GROUPS, S5_P), hi.reshape(batch, S5_GROUPS, S5_P))
    return x3, st, (s_stack, c_stack)


def kernel(x_prompt, x_sample, mem_prompt, state_dn_conv, state_dn_s, state_ml_c, state_ml_n, state_ml_m, state_ssm_re, state_ssm_im, cache_mem_k, cache_mem_v, norm_mix, w_in, dn_conv_w, dn_a_log, dn_dt_bias, dn_norm, ml_i_bias, ml_f_bias, ml_norm, s5_lam_re, s5_lam_im, s5_log_dt, s5_b_re, s5_b_im, s5_c_re, s5_c_im, s5_d, s5_w_glu, s5_b_glu, w_out, norm_x, norm_mem, w_xq, w_xk, w_xv, w_xo, norm_ff, w_ff1, w_ff2, norm_final):
    a = dict(norm_mix=norm_mix, w_in=w_in, dn_conv_w=dn_conv_w, dn_a_log=dn_a_log, dn_dt_bias=dn_dt_bias,
             dn_norm=dn_norm, ml_i_bias=ml_i_bias, ml_f_bias=ml_f_bias, ml_norm=ml_norm,
             s5_lam_re=s5_lam_re, s5_lam_im=s5_lam_im, s5_log_dt=s5_log_dt, s5_b_re=s5_b_re,
             s5_b_im=s5_b_im, s5_c_re=s5_c_re, s5_c_im=s5_c_im, s5_d=s5_d, s5_w_glu=s5_w_glu,
             s5_b_glu=s5_b_glu, w_out=w_out, norm_x=norm_x, w_xq=w_xq, w_xo=w_xo, norm_ff=norm_ff,
             w_ff1=w_ff1, w_ff2=w_ff2)
    prms = [_layer_params(l, a) for l in range(DEPTH)]
    wts = _shared_weights(dict(a, w_xk=w_xk, w_xv=w_xv))
    g_final = norm_final.astype(F32)[None, :]

    bp, lp, _ = x_prompt.shape
    bs, ls, _ = x_sample.shape

    xp = x_prompt
    mk, mk_t, mv, mv_t = _mem_proj(mem_prompt, norm_mem.astype(F32), wts["w_xk"], wts["w_xv"])
    p_st = []
    for l in range(DEPTH):
        xp, st = _layer_prompt(xp, mk, mv, prms[l], wts, g_final if l == DEPTH - 1 else None)
        p_st.append(st)
    y_prompt = xp
    p_mem = (mk_t.reshape(DEPTH, bp, N_MEM, X_HEADS, X_HD), mv_t.reshape(DEPTH, bp, N_MEM, X_HEADS, X_HD))

    xs = jnp.transpose(x_sample, (1, 0, 2)).reshape(1, ls * bs, D_MODEL)
    cache_k = cache_mem_k.reshape(DEPTH, bs, N_MEM * X_HEADS, X_HD)
    cache_v = cache_mem_v.reshape(DEPTH, bs, N_MEM * X_HEADS, X_HD)
    dn_s_t = jnp.transpose(state_dn_s, (0, 2, 3, 4, 1))
    ml_c_t = jnp.transpose(state_ml_c, (0, 2, 3, 4, 1))
    ml_n_t = jnp.transpose(state_ml_n, (0, 2, 3, 1))
    ml_m_t = jnp.transpose(state_ml_m, (0, 2, 1))[:, :, None, :]
    s_st = []
    stacks = None
    for l in range(DEPTH):
        xs, st, stacks = _layer_sample(xs, cache_k, cache_v, state_dn_conv[l], dn_s_t, ml_c_t, ml_n_t, ml_m_t,
                                       state_ssm_re[l], state_ssm_im[l], prms[l], wts, bs, ls,
                                       g_final if l == DEPTH - 1 else None, stacks)
        s_st.append(st)
    y_sample = jnp.transpose(xs.reshape(ls, bs, D_MODEL), (1, 0, 2))
    s_mats = tuple(jnp.transpose(m, (0, 4, 1, 2, 3)) for m in stacks)

    stack = lambda sts, i: jnp.stack([s[i] for s in sts])
    return ((y_prompt, y_sample) + tuple(stack(p_st, i) for i in range(7)) + p_mem
            + (stack(s_st, 0),) + s_mats + tuple(stack(s_st, i) for i in range(1, 5)))
```

```python
import functools
import math

import numpy as np
import jax
import jax.numpy as jnp
from jax import lax
from jax.experimental import pallas as pl
from jax.experimental.pallas import tpu as pltpu

F32 = jnp.float32
BF16 = jnp.bfloat16

D_MODEL = 1024
DEPTH = 2
N_HEADS = 6
HEAD_DIM = 64
N_PAIRS = N_HEADS // 2
HEADS_W = N_HEADS * HEAD_DIM
CONV_W = 4
CONV_CH = 3 * HEADS_W
S5_WIDTH = 256
S5_GROUPS = 16
S5_P = 64
S5_STATE = S5_GROUPS * S5_P
N_MEM = 256
X_HEADS = 4
X_HD = 128
X_WIDTH = X_HEADS * X_HD
D_FF = 4 * D_MODEL
EPS = 1e-6
IN_SIZES = (384, 384, 384, 6, 6, 384, 384, 384, 384, 6, 6, 384, 256)

LANES = 128
CHUNK = 64
Q_ROWS = 8
NEG = -1e30
Z_COLS = 3456
S5_COL_BLOCK = 3072 // S5_WIDTH
GATE_COL_BLOCK = (Z_COLS - LANES) // LANES
DN_GATE_COL = 3 * HEADS_W
ML_GATE_COL = 7 * HEADS_W
ACT_COLS = CONV_CH + 2 * HEADS_W
VMEM_LIMIT = 52 * 1024 * 1024

ROW_TILE = 1024
CONV_ROW_TILE = 512
POST_TILE = 512
SEQ_BATCH = 8
S5_STEPS = 64
ATTN_BATCH = 8
FF_CHUNK = 1024
TRI_BASE = 8
SEQ_SOLVE = (1, True, 1)

_NN = (((1,), (0,)), ((), ()))
_NT = (((1,), (1,)), ((), ()))


def _dot(a, b, dims=_NN):
    return lax.dot_general(a, b, dims, preferred_element_type=F32)


def _split(a, n):
    out = []
    r = a
    for i in range(n):
        h = r.astype(BF16)
        out.append(h)
        if i + 1 < n:
            r = r - h.astype(F32)
    return out


def _mm(a, b, dims=_NN, prec=1):
    if prec == 1:
        return _dot(a.astype(BF16), b.astype(BF16), dims)
    return _mmp(_parts(a, prec), _parts(b, prec), dims)


def _parts(a, prec):
    return tuple(_split(a, 1 if prec == 1 else 2))


def _mmp(ap, bp, dims=_NN):
    acc = None
    if len(bp) > 1:
        acc = _dot(ap[0], bp[1], dims)
    if len(ap) > 1:
        cross = _dot(ap[1], bp[0], dims)
        acc = cross if acc is None else acc + cross
    lead = _dot(ap[0], bp[0], dims)
    return lead if acc is None else acc + lead


def _mm_sel_r(a, sel, n=3):
    parts = _split(a, n)
    acc = _dot(parts[-1], sel)
    for p_ in parts[-2::-1]:
        acc = acc + _dot(p_, sel)
    return acc


def _mm_sel_l(sel, a, n=3):
    parts = _split(a, n)
    acc = _dot(sel, parts[-1])
    for p_ in parts[-2::-1]:
        acc = acc + _dot(sel, p_)
    return acc


def _sigmoid(x):
    return 1.0 / (1.0 + jnp.exp(-x))


def _silu(x):
    return x * _sigmoid(x)


def _softplus(x):
    return jnp.maximum(x, 0.0) + jnp.log(1.0 + jnp.exp(-jnp.abs(x)))


def _rmsnorm_rows(x, g):
    return x * lax.rsqrt(jnp.mean(x * x, axis=-1, keepdims=True) + EPS) * g


def _iota(shape, dim):
    return lax.broadcasted_iota(jnp.int32, shape, dim)


def _idiv(x, n):
    return x >> (n.bit_length() - 1)


def _imod(x, n):
    return x & (n - 1)


def _lane_lo(shape):
    return _iota(shape, 1) < HEAD_DIM


def _x2(x):
    lo = _lane_lo(x.shape)
    return jnp.concatenate([jnp.where(lo, x, 0.0), jnp.where(lo, 0.0, x)], axis=0)


def _x2b(x):
    return _x2(x.astype(BF16))


def _fold(y):
    r = y.shape[0] // 2
    return y[:r] + y[r:]


def _expand(a, c0, c1):
    r = a.shape[0]
    return jnp.where(_lane_lo((r, LANES)), a[:, c0:c0 + 1], a[:, c1:c1 + 1])


def _bd_mask():
    return (_iota((LANES, LANES), 0) < HEAD_DIM) == (_iota((LANES, LANES), 1) < HEAD_DIM)


def _block_ones():
    return jnp.where(_bd_mask(), 1.0, 0.0).astype(BF16)


def _seg_mean(x, bones):
    return _mm_sel_r(x, bones, n=1) * (1.0 / HEAD_DIM)


def _ltri():
    return jnp.where(_iota((CHUNK, CHUNK), 1) <= _iota((CHUNK, CHUNK), 0), 1.0, 0.0).astype(BF16)


def _upair():
    a = _iota((LANES, LANES), 0)
    b = _iota((LANES, LANES), 1)
    return jnp.where(_bd_mask() & (a <= b), 1.0, 0.0).astype(BF16)


def _pair_masks():
    i = _iota((CHUNK, LANES), 0)
    j = _imod(_iota((CHUNK, LANES), 1), HEAD_DIM)
    return j <= i, j < i


def _blk_eq(n, b):
    return _idiv(_iota((n, n), 0), b) == _idiv(_iota((n, n), 1), b)


def _last_row(a):
    return jnp.broadcast_to(a[CHUNK - 1:CHUNK, :], a.shape)


def _tri_inverse_units(mbds, prec, newton):
    n = LANES
    eye = jnp.where(_iota((n, n), 0) == _iota((n, n), 1), 1.0, 0.0)
    n_sq = TRI_BASE.bit_length() - 2
    blk = _blk_eq(n, TRI_BASE)
    m0 = [jnp.where(blk, m, 0.0) for m in mbds]
    ts = [eye - m for m in m0]
    ps = [_mm(m, m, prec=prec) for m in m0]
    for it in range(n_sq):
        if it == n_sq - 1:
            ts = [t + _mm(t, p_, prec=prec) for t, p_ in zip(ts, ps)]
        else:
            both = [_mm(jnp.concatenate([t, p_], axis=0), p_, prec=prec) for t, p_ in zip(ts, ps)]
            ts = [t + b_[:n] for t, b_ in zip(ts, both)]
            ps = [b_[n:] for b_ in both]
    b = TRI_BASE
    while b < CHUNK:
        outer = _blk_eq(n, 2 * b) & jnp.logical_not(_blk_eq(n, b))
        tp = [_parts(t, prec) for t in ts]
        xs = [_mmp(_parts(jnp.where(outer, m, 0.0), prec), t_) for m, t_ in zip(mbds, tp)]
        ts = [t - _mmp(t_, _parts(x, prec)) for t, t_, x in zip(ts, tp, xs)]
        b *= 2
    if newton:
        res = [eye - (t + _mm(m, t, prec=3)) for m, t in zip(mbds, ts)]
        ts = [t + _mm(t, r) for t, r in zip(ts, res)]
    return ts


def _wspec(rows, cols, layer, row_block=0):
    return pl.BlockSpec((None, rows, cols), lambda i: (layer, row_block, 0), pipeline_mode=pl.Buffered(1))


def _rows3(nt):
    return lambda i: (i // nt, i % nt, 0)


def _dense_call(kernel_fn, name, out_shape, in_specs, out_specs, steps, args):
    return pl.pallas_call(
        kernel_fn, out_shape=out_shape, grid=(steps,), in_specs=in_specs, out_specs=out_specs,
        compiler_params=pltpu.CompilerParams(dimension_semantics=("parallel",), vmem_limit_bytes=VMEM_LIMIT),
        name=name)(*args)


def _in_proj_kernel(x_ref, g_ref, w_ref, o_ref):
    o_ref[0] = _dot(_rmsnorm_rows(x_ref[0], g_ref[...]).astype(BF16), w_ref[...])


def _norm_matmul(x, g, w, layer, tm):
    b, l, k = x.shape
    n = w.shape[2]
    nt = l // tm
    fix = lambda i: (0, 0)
    return _dense_call(
        _in_proj_kernel, "norm_matmul", jax.ShapeDtypeStruct((b, l, n), F32),
        [pl.BlockSpec((1, tm, k), _rows3(nt)), pl.BlockSpec((1, k), fix), _wspec(k, n, layer)],
        pl.BlockSpec((1, tm, n), _rows3(nt)), b * nt, (x, g, w))


def _conv_silu(u, w8, fix):
    out = fix(pltpu.roll(u, 3, axis=0), 3) * w8[0:1, :]
    for s in (2, 1):
        out = out + fix(pltpu.roll(u, s, axis=0), s) * w8[3 - s:4 - s, :]
    out = out + u * w8[3:4, :]
    return _silu(out)


def _in_proj_conv_kernel(x_ref, g_ref, w_ref, cp_ref, cw_ref, o_ref, c_ref, u_scr, gate_scr, tail_scr,
                         *, nt, n_tiles):
    i = pl.program_id(0)
    tm = u_scr.shape[0]

    @pl.when(i == 0)
    def _():
        u_scr[...] = jnp.zeros_like(u_scr)
        gate_scr[...] = jnp.zeros_like(gate_scr)
        tail_scr[...] = jnp.zeros_like(tail_scr)

    prev = tail_scr[...]
    u_prev = u_scr[...]
    r8 = _iota((8, CONV_CH), 0)

    def fix(rolled, s):
        first = jnp.where(r8 < s, pltpu.roll(prev, s, axis=0), rolled[0:8])
        return jnp.concatenate([first, rolled[8:]], axis=0)

    c_ref[0, :, :CONV_CH] = _conv_silu(u_prev, cw_ref[...], fix)
    c_ref[0, :, CONV_CH:CONV_CH + HEADS_W] = _silu(gate_scr[:, :HEADS_W])
    c_ref[0, :, CONV_CH + HEADS_W:] = _sigmoid(gate_scr[:, HEADS_W:])

    z = _dot(_rmsnorm_rows(x_ref[0], g_ref[...]).astype(BF16), w_ref[...])
    o_ref[0] = z
    first_of_seq = jnp.minimum(i, n_tiles - 1) % nt == 0
    tail_scr[...] = jnp.where(first_of_seq, cp_ref[0], u_prev[tm - 8:])
    u_scr[...] = z[:, :CONV_CH]
    gate_scr[:, :HEADS_W] = z[:, DN_GATE_COL:DN_GATE_COL + HEADS_W]
    gate_scr[:, HEADS_W:] = z[:, ML_GATE_COL:ML_GATE_COL + HEADS_W]


def _norm_matmul_conv(x, g, w, convp, cw8, layer, tm):
    b, l, k = x.shape
    n = w.shape[2]
    nt = l // tm
    n_tiles = b * nt
    fix = lambda i: (0, 0)
    cur = lambda i: jnp.minimum(i, n_tiles - 1)
    tile = lambda i: (cur(i) // nt, cur(i) % nt, 0)
    prev_tile = lambda i: (jnp.maximum(i - 1, 0) // nt, jnp.maximum(i - 1, 0) % nt, 0)
    return pl.pallas_call(
        functools.partial(_in_proj_conv_kernel, nt=nt, n_tiles=n_tiles),
        out_shape=(jax.ShapeDtypeStruct((b, l, n), F32), jax.ShapeDtypeStruct((b, l, ACT_COLS), F32)),
        grid=(n_tiles + 1,),
        in_specs=[pl.BlockSpec((1, tm, k), tile), pl.BlockSpec((1, k), fix), _wspec(k, n, layer),
                  pl.BlockSpec((1, 8, CONV_CH), lambda i: (cur(i) // nt, 0, 0)), pl.BlockSpec((8, CONV_CH), fix)],
        out_specs=(pl.BlockSpec((1, tm, n), tile), pl.BlockSpec((1, tm, ACT_COLS), prev_tile)),
        scratch_shapes=[pltpu.VMEM((tm, CONV_CH), F32), pltpu.VMEM((tm, 2 * HEADS_W), F32),
                        pltpu.VMEM((8, CONV_CH), F32)],
        compiler_params=pltpu.CompilerParams(dimension_semantics=("arbitrary",), vmem_limit_bytes=VMEM_LIMIT),
        name="norm_matmul_conv",
    )(x, g, w, convp, cw8)


def _mem_proj_kernel(x_ref, g_ref, wk_ref, wv_ref, k_ref, kt_ref, v_ref, vt_ref):
    x = x_ref[0]
    xhat = x * lax.rsqrt(jnp.mean(x * x, axis=-1, keepdims=True) + EPS)
    for l in range(DEPTH):
        xn = (xhat * g_ref[l:l + 1, :]).astype(BF16)
        for w_ref, o_ref, t_ref in ((wk_ref, k_ref, kt_ref), (wv_ref, v_ref, vt_ref)):
            res = _dot(xn, w_ref[l])
            o_ref[l, 0] = res
            for h in range(X_HEADS):
                t_ref[l, 0, pl.ds(h, N_MEM, stride=X_HEADS), :] = res[:, h * X_HD:(h + 1) * X_HD]


def _mem_proj(mem, g, wk, wv):
    b = mem.shape[0]
    whole = lambda shape: pl.BlockSpec(shape, lambda i: (0,) * len(shape))
    o_spec = pl.BlockSpec((DEPTH, 1, N_MEM, X_WIDTH), lambda i: (0, i, 0, 0))
    t_spec = pl.BlockSpec((DEPTH, 1, N_MEM * X_HEADS, X_HD), lambda i: (0, i, 0, 0))
    o_shape = jax.ShapeDtypeStruct((DEPTH, b, N_MEM, X_WIDTH), F32)
    t_shape = jax.ShapeDtypeStruct((DEPTH, b, N_MEM * X_HEADS, X_HD), F32)
    return _dense_call(
        _mem_proj_kernel, "mem_proj", (o_shape, t_shape, o_shape, t_shape),
        [pl.BlockSpec((1, N_MEM, D_MODEL), lambda i: (i, 0, 0)), whole((DEPTH, D_MODEL)),
         whole((DEPTH, D_MODEL, X_WIDTH)), whole((DEPTH, D_MODEL, X_WIDTH))],
        (o_spec, t_spec, o_spec, t_spec), b, (mem, g, wk, wv))


def _mix_residual(x, dn, ml, s5, w1_ref, w2_ref, w3_ref):
    acc = x + _dot(dn, w1_ref[...])
    acc = acc + _dot(ml, w2_ref[...])
    return acc + _dot(s5, w3_ref[...])


def _mlp_residual(x2, g_ref, f1_ref, f2_ref):
    xn = _rmsnorm_rows(x2, g_ref[...]).astype(BF16)
    acc = x2
    for c in range(D_FF // FF_CHUNK):
        hf = _dot(xn, f1_ref[:, c * FF_CHUNK:(c + 1) * FF_CHUNK])
        a = jnp.square(jnp.maximum(hf, 0.0)).astype(BF16)
        acc = acc + _dot(a, f2_ref[c * FF_CHUNK:(c + 1) * FF_CHUNK, :])
    return acc


def _out_proj_kernel(dn_ref, ml_ref, s5_ref, x_ref, w1_ref, w2_ref, w3_ref, g_ref, wq_ref, x1_ref, q_ref):
    acc = _mix_residual(x_ref[0], dn_ref[0], ml_ref[0], s5_ref[...], w1_ref, w2_ref, w3_ref)
    x1_ref[0] = acc
    q_ref[0] = _dot(_rmsnorm_rows(acc, g_ref[...]).astype(BF16), wq_ref[...])


def _w_out_specs(layer):
    return [_wspec(HEADS_W, D_MODEL, layer, 0), _wspec(HEADS_W, D_MODEL, layer, 1),
            _wspec(S5_WIDTH, D_MODEL, layer, 2 * HEADS_W // S5_WIDTH)]


def _out_proj(dn, ml, s5, x, wts, prm, layer, tm):
    b, l, _ = x.shape
    nt = l // tm
    row = _rows3(nt)
    vec = pl.BlockSpec((1, D_MODEL), lambda i: (0, 0))
    return _dense_call(
        _out_proj_kernel, "out_proj",
        (jax.ShapeDtypeStruct((b, l, D_MODEL), F32), jax.ShapeDtypeStruct((b, l, X_WIDTH), F32)),
        [pl.BlockSpec((1, tm, HEADS_W), row), pl.BlockSpec((1, tm, HEADS_W), row),
         pl.BlockSpec((tm, S5_WIDTH), lambda i: (i % nt, i // nt)), pl.BlockSpec((1, tm, D_MODEL), row)]
        + _w_out_specs(layer) + [vec, _wspec(D_MODEL, X_WIDTH, layer)],
        (pl.BlockSpec((1, tm, D_MODEL), row), pl.BlockSpec((1, tm, X_WIDTH), row)), b * nt,
        (dn, ml, s5, x, wts["w_out"], wts["w_out"], wts["w_out"], prm["norm_x"], wts["w_xq"]))


def _ffn_kernel(x_ref, o_ref, wo_ref, g_ref, f1_ref, f2_ref, gl_ref, y_ref, *, final_norm):
    acc = _mlp_residual(x_ref[0] + _dot(o_ref[0], wo_ref[...]), g_ref, f1_ref, f2_ref)
    y_ref[0] = _rmsnorm_rows(acc, gl_ref[...]) if final_norm else acc


def _ffn(x, att, wts, prm, g_last, layer, tm, final_norm):
    b, l, _ = x.shape
    nt = l // tm
    row = _rows3(nt)
    vec = pl.BlockSpec((1, D_MODEL), lambda i: (0, 0))
    return _dense_call(
        functools.partial(_ffn_kernel, final_norm=final_norm), "ffn", jax.ShapeDtypeStruct((b, l, D_MODEL), F32),
        [pl.BlockSpec((1, tm, D_MODEL), row), pl.BlockSpec((1, tm, X_WIDTH), row), _wspec(X_WIDTH, D_MODEL, layer),
         vec, _wspec(D_MODEL, D_FF, layer), _wspec(D_FF, D_MODEL, layer), vec],
        pl.BlockSpec((1, tm, D_MODEL), row), b * nt,
        (x, att, wts["w_xo"], prm["norm_ff"], wts["w_ff1"], wts["w_ff2"], g_last))


def _attn_heads(q, k_ref, v_ref):
    outs = []
    for h in range(X_HEADS):
        sl = slice(h * X_HD, (h + 1) * X_HD)
        s = _dot(q[:, sl], k_ref[0, :, sl].astype(BF16), _NT) * (X_HD ** -0.5)
        e = jnp.exp(s - jnp.max(s, axis=-1, keepdims=True))
        p = e / jnp.sum(e, axis=-1, keepdims=True)
        outs.append(_dot(p.astype(BF16), v_ref[0, :, sl].astype(BF16)))
    return jnp.concatenate(outs, axis=-1)


def _attn_batch_kernel(q_ref, k_ref, v_ref, o_ref, *, bb):
    rows = X_HEADS * Q_ROWS
    col_head = _imod(_iota((rows, N_MEM * X_HEADS), 1), X_HEADS)
    row_head = _idiv(_iota((rows, N_MEM * X_HEADS), 0), Q_ROWS)
    own = col_head == row_head
    q4 = [jnp.concatenate([q_ref[bi, :, h * X_HD:(h + 1) * X_HD] for h in range(X_HEADS)], axis=0).astype(BF16)
          for bi in range(bb)]
    s = [jnp.where(own, _dot(q_, k_ref[bi].astype(BF16), _NT) * (X_HD ** -0.5), NEG)
         for bi, q_ in enumerate(q4)]
    e = [jnp.exp(x - jnp.max(x, axis=-1, keepdims=True)) for x in s]
    p = [x / jnp.sum(x, axis=-1, keepdims=True) for x in e]
    o = [_dot(x.astype(BF16), v_ref[bi].astype(BF16)) for bi, x in enumerate(p)]
    for bi in range(bb):
        for h in range(X_HEADS):
            o_ref[bi, :, h * X_HD:(h + 1) * X_HD] = o[bi][h * Q_ROWS:(h + 1) * Q_ROWS].astype(BF16)


def _attn_batch(q, k, v, layer, bb):
    b, l, _ = q.shape
    blk = lambda i: (i, 0, 0)
    kv = pl.BlockSpec((None, bb, N_MEM * X_HEADS, X_HD), lambda i: (layer, i, 0, 0))
    return _dense_call(
        functools.partial(_attn_batch_kernel, bb=bb), "attn_batch", jax.ShapeDtypeStruct((b, l, X_WIDTH), BF16),
        [pl.BlockSpec((bb, l, X_WIDTH), blk), kv, kv], pl.BlockSpec((bb, l, X_WIDTH), blk), b // bb, (q, k, v))


def _post_mix_kernel(dn_ref, ml_ref, s5_ref, x_ref, k_ref, v_ref, w1_ref, w2_ref, w3_ref, gx_ref, wq_ref,
                     wo_ref, gf_ref, f1_ref, f2_ref, gl_ref, y_ref, *, final_norm):
    x1 = _mix_residual(x_ref[0], dn_ref[0], ml_ref[0], s5_ref[...], w1_ref, w2_ref, w3_ref)
    q = _dot(_rmsnorm_rows(x1, gx_ref[...]).astype(BF16), wq_ref[...]).astype(BF16)
    att = _attn_heads(q, k_ref, v_ref).astype(BF16)
    acc = _mlp_residual(x1 + _dot(att, wo_ref[...]), gf_ref, f1_ref, f2_ref)
    y_ref[0] = _rmsnorm_rows(acc, gl_ref[...]) if final_norm else acc


def _post_mix(dn, ml, s5, x, mem_k, mem_v, wts, prm, g_last, layer, tm, final_norm):
    b, l, _ = x.shape
    nt = l // tm
    row = _rows3(nt)
    mem = pl.BlockSpec((None, 1, N_MEM, X_WIDTH), lambda i: (layer, i // nt, 0, 0))
    vec = pl.BlockSpec((1, D_MODEL), lambda i: (0, 0))
    return _dense_call(
        functools.partial(_post_mix_kernel, final_norm=final_norm), "post_mix",
        jax.ShapeDtypeStruct((b, l, D_MODEL), F32),
        [pl.BlockSpec((1, tm, HEADS_W), row), pl.BlockSpec((1, tm, HEADS_W), row),
         pl.BlockSpec((tm, S5_WIDTH), lambda i: (i % nt, i // nt)), pl.BlockSpec((1, tm, D_MODEL), row), mem, mem]
        + _w_out_specs(layer)
        + [vec, _wspec(D_MODEL, X_WIDTH, layer), _wspec(X_WIDTH, D_MODEL, layer), vec,
           _wspec(D_MODEL, D_FF, layer), _wspec(D_FF, D_MODEL, layer), vec],
        pl.BlockSpec((1, tm, D_MODEL), row), b * nt,
        (dn, ml, s5, x, mem_k, mem_v, wts["w_out"], wts["w_out"], wts["w_out"], prm["norm_x"], wts["w_xq"],
         wts["w_xo"], prm["norm_ff"], wts["w_ff1"], wts["w_ff2"], g_last))


def _s5_kernel(u_ref, h0r_ref, h0i_ref, lre_ref, lim_ref, ldt_ref, bre_ref, bim_ref, cre_ref, cim_ref,
               d_ref, wg_ref, bg_ref, y_ref, hr_out, hi_out,
               xr_s, xi_s, st_r, st_i, coef_s, bbr_s, bbi_s, *io_s, tb, nb, lane_chunk, seq_major_io):
    step = pl.program_id(0)
    if seq_major_io:
        u_s, y_s = io_s
        for b in range(nb):
            for j in range(S5_WIDTH // LANES):
                u_s[j, pl.ds(b, tb, stride=nb), :] = u_ref[b, :, j * LANES:(j + 1) * LANES]
        u = jnp.concatenate([u_s[j] for j in range(S5_WIDTH // LANES)], axis=1)
    else:
        u = u_ref[...]

    @pl.when(step == 0)
    def _():
        lr = lre_ref[...]
        li = lim_ref[...]
        dt = jnp.exp(ldt_ref[...])
        mag = jnp.exp(lr * dt)
        ar = mag * jnp.cos(li * dt)
        ai = mag * jnp.sin(li * dt)
        inv = 1.0 / (lr * lr + li * li)
        coef_s[0:1, :] = ar
        coef_s[1:2, :] = ai
        cr = ((ar - 1.0) * lr + ai * li) * inv
        ci = (ai * lr - (ar - 1.0) * li) * inv
        bbr_s[...] = (cr * bre_ref[...] - ci * bim_ref[...]).astype(BF16)
        bbi_s[...] = (cr * bim_ref[...] + ci * bre_ref[...]).astype(BF16)
        st_r[...] = h0r_ref[...]
        st_i[...] = h0i_ref[...]

    ub = u.astype(BF16)
    xr_s[...] = _dot(ub, bbr_s[...])
    xi_s[...] = _dot(ub, bbi_s[...])

    for lc in range(S5_STATE // lane_chunk):
        sl = slice(lc * lane_chunk, (lc + 1) * lane_chunk)
        ar = jnp.broadcast_to(coef_s[0:1, sl], (nb, lane_chunk))
        ai = jnp.broadcast_to(coef_s[1:2, sl], (nb, lane_chunk))

        def body(t, carry, sl=sl, ar=ar, ai=ai):
            hr, hi = carry
            r0 = pl.multiple_of(t * nb, 8)
            nhr = ar * hr - ai * hi + xr_s[pl.ds(r0, nb), sl]
            nhi = ar * hi + ai * hr + xi_s[pl.ds(r0, nb), sl]
            xr_s[pl.ds(r0, nb), sl] = nhr
            xi_s[pl.ds(r0, nb), sl] = nhi
            return nhr, nhi

        hr, hi = lax.fori_loop(0, tb, body, (st_r[:, sl], st_i[:, sl]), unroll=min(tb, 8))
        st_r[:, sl] = hr
        st_i[:, sl] = hi

    y = (_dot(xr_s[...].astype(BF16), cre_ref[...]) - _dot(xi_s[...].astype(BF16), cim_ref[...])
         + d_ref[...] * u)
    zg = 0.5 * y * (1.0 + jnp.tanh(math.sqrt(2.0 / math.pi) * (y + 0.044715 * (y * y * y))))
    yv = zg * _sigmoid(_dot(zg.astype(BF16), wg_ref[...]) + bg_ref[...])
    if seq_major_io:
        for j in range(S5_WIDTH // LANES):
            y_s[j] = yv[:, j * LANES:(j + 1) * LANES]
        for b in range(nb):
            for j in range(S5_WIDTH // LANES):
                c0 = b * S5_WIDTH + j * LANES
                y_ref[:, c0:c0 + LANES] = y_s[j, pl.ds(b, tb, stride=nb), :].astype(BF16)
    else:
        y_ref[...] = yv.astype(BF16)

    @pl.when(step == pl.num_programs(0) - 1)
    def _():
        hr_out[...] = st_r[...]
        hi_out[...] = st_i[...]


def _s5(u, h0r, h0i, prm, tb, lane_chunk, seq_major_io):
    nb = h0r.shape[0]
    fix2 = lambda i: (0, 0)
    vec = pl.BlockSpec((1, S5_STATE), fix2)
    st = pl.BlockSpec((nb, S5_STATE), fix2)
    rows = tb * nb
    scratch = [pltpu.VMEM((rows, S5_STATE), F32), pltpu.VMEM((rows, S5_STATE), F32),
               pltpu.VMEM((nb, S5_STATE), F32), pltpu.VMEM((nb, S5_STATE), F32),
               pltpu.VMEM((8, S5_STATE), F32),
               pltpu.VMEM((S5_WIDTH, S5_STATE), BF16), pltpu.VMEM((S5_WIDTH, S5_STATE), BF16)]
    if seq_major_io:
        l = u.shape[1]
        u_spec = pl.BlockSpec((nb, tb, S5_WIDTH), lambda i: (0, i, S5_COL_BLOCK))
        y_shape = jax.ShapeDtypeStruct((l, nb * S5_WIDTH), BF16)
        y_spec = pl.BlockSpec((tb, nb * S5_WIDTH), lambda i: (i, 0))
        io_shape = (S5_WIDTH // LANES, rows, LANES)
        scratch += [pltpu.VMEM(io_shape, F32), pltpu.VMEM(io_shape, F32)]
    else:
        l = u.shape[0] // nb
        u_spec = pl.BlockSpec((rows, S5_WIDTH), lambda i: (i, 0))
        y_shape = jax.ShapeDtypeStruct((l * nb, S5_WIDTH), BF16)
        y_spec = pl.BlockSpec((rows, S5_WIDTH), lambda i: (i, 0))
    return pl.pallas_call(
        functools.partial(_s5_kernel, tb=tb, nb=nb, lane_chunk=lane_chunk, seq_major_io=seq_major_io),
        out_shape=(y_shape, jax.ShapeDtypeStruct((nb, S5_STATE), F32), jax.ShapeDtypeStruct((nb, S5_STATE), F32)),
        grid=(l // tb,),
        in_specs=[u_spec, st, st, vec, vec, vec,
                  pl.BlockSpec((S5_WIDTH, S5_STATE), fix2), pl.BlockSpec((S5_WIDTH, S5_STATE), fix2),
                  pl.BlockSpec((S5_STATE, S5_WIDTH), fix2), pl.BlockSpec((S5_STATE, S5_WIDTH), fix2),
                  pl.BlockSpec((1, S5_WIDTH), fix2), pl.BlockSpec((S5_WIDTH, S5_WIDTH), fix2),
                  pl.BlockSpec((1, S5_WIDTH), fix2)],
        out_specs=(y_spec, st, st),
        scratch_shapes=scratch,
        compiler_params=pltpu.CompilerParams(dimension_semantics=("arbitrary",),
                                             vmem_limit_bytes=VMEM_LIMIT),
        name="s5",
    )(u, h0r, h0i, prm["lre"], prm["lim"], prm["ldt"], prm["bre"], prm["bim"], prm["cre"],
      prm["cim"], prm["d"], prm["wg"], prm["bg"])


def _l2norm_pairs(x_chunks, bones, scale):
    n = len(x_chunks)
    x = jnp.concatenate(x_chunks, axis=0) if n > 1 else x_chunks[0]
    out = [[None] * N_PAIRS for _ in range(n)]
    for p_ in range(N_PAIRS):
        xp = x[:, p_ * LANES:(p_ + 1) * LANES]
        xn = xp * (lax.rsqrt(_seg_mean(xp * xp, bones) * HEAD_DIM + EPS) * scale)
        for c in range(n):
            out[c][p_] = xn[c * CHUNK:(c + 1) * CHUNK]
    return out


def _head_norm_store(vals, gates, normw, bones, store):
    n = len(vals)
    for p_ in range(N_PAIRS):
        v = jnp.concatenate([vals[c][p_] for c in range(n)], axis=0) if n > 1 else vals[0][p_]
        g = jnp.concatenate([gates[c][p_] for c in range(n)], axis=0) if n > 1 else gates[0][p_]
        y = v * lax.rsqrt(_seg_mean(v * v, bones) + EPS) * normw * g
        for c in range(n):
            store(c, p_, y[c * CHUNK:(c + 1) * CHUNK].astype(BF16))


def _dn_local(chunks, prm, prec):
    n = len(chunks)
    bones = prm["bones"]
    incl, strict = _pair_masks()
    ltri, upair = _ltri(), _upair()
    units = [(c, p_) for c in range(n) for p_ in range(N_PAIRS)]

    g_cols = _mm_sel_l(ltri, jnp.concatenate([-jnp.exp(prm["alog_row"]) * _softplus(ch["zs"]) for ch in chunks],
                                             axis=1))
    g_rows = _mm_sel_r(jnp.concatenate([-jnp.exp(prm["rowalog"]) * _softplus(ch["rows"] + prm["rowadd"])
                                        for ch in chunks], axis=0), upair)
    gates = [(_sigmoid(ch["zs"]), g_cols[:, c * LANES:(c + 1) * LANES], g_rows[c * 16:(c + 1) * 16])
             for c, ch in enumerate(chunks)]

    qn = _l2norm_pairs([ch["q"] for ch in chunks], bones, HEAD_DIM ** -0.5)
    kn = _l2norm_pairs([ch["k"] for ch in chunks], bones, 1.0)

    beta = [_expand(gates[c][0], 2 * p_, 2 * p_ + 1) for c, p_ in units]
    g_col = [_expand(gates[c][1], 6 + 2 * p_, 7 + 2 * p_) for c, p_ in units]
    dec = [jnp.exp(jnp.where(incl, gc - gates[c][2][p_:p_ + 1, :], NEG)) for gc, (c, p_) in zip(g_col, units)]
    kb = [kn[c][p_] * b for b, (c, p_) in zip(beta, units)]
    sc = [_mm(jnp.concatenate([kb_, qn[c][p_]], axis=0), _x2b(kn[c][p_]), _NT)
          for kb_, (c, p_) in zip(kb, units)]
    a_in = [s[CHUNK:] * d for s, d in zip(sc, dec)]
    mbd = [_x2(jnp.where(strict, s[:CHUNK] * d, 0.0)) for s, d in zip(sc, dec)]
    tri_prec, newton, sol_prec = prec
    t_bd = _tri_inverse_units(mbd, tri_prec, newton)
    eg = [jnp.exp(gc) for gc in g_col]
    x2r = _x2b if sol_prec == 1 else _x2
    rhs = [jnp.concatenate([x2r(chunks[c]["v"][:, p_ * LANES:(p_ + 1) * LANES] * b), x2r(kb_ * e)], axis=1)
           for b, kb_, e, (c, p_) in zip(beta, kb, eg, units)]
    sol = [_mm(t, r, prec=sol_prec) for t, r in zip(t_bd, rhs)]
    out = [[None] * N_PAIRS for _ in range(n)]
    for i, (c, p_) in enumerate(units):
        g_last = _last_row(g_col[i])
        out[c][p_] = dict(u=_fold(sol[i][:, :LANES]), w=_fold(sol[i][:, LANES:]), a_in=a_in[i],
                          qg=qn[c][p_] * eg[i], kd=kn[c][p_] * jnp.exp(g_last - g_col[i]),
                          eg_last=jnp.exp(g_last))
    return out


def _dn_seq_step(zq_ref, zk_ref, zv_ref, zg_ref, zs_ref, rows_ref,
                 add_ref, alog_ref, rowadd_ref, rowalog_ref, nw_ref, o_ref, s_scr, *, bb, prec):
    chunks = [dict(q=zq_ref[b], k=zk_ref[b], v=zv_ref[b], zs=zs_ref[b] + add_ref[...], rows=rows_ref[b, 0])
              for b in range(bb)]

    prm = dict(bones=_block_ones(), alog_row=alog_ref[...], rowadd=rowadd_ref[...],
               rowalog=rowalog_ref[...])
    loc = _dn_local(chunks, prm, prec)
    bdm = _bd_mask()
    units = [(b, p_) for b in range(bb) for p_ in range(N_PAIRS)]
    s_prev = [s_scr[b, p_] for b, p_ in units]
    ws = [_mm(jnp.concatenate([loc[b][p_]["w"], loc[b][p_]["qg"]], axis=0), s)
          for s, (b, p_) in zip(s_prev, units)]
    v_new = [loc[b][p_]["u"] - w_[:CHUNK] for w_, (b, p_) in zip(ws, units)]
    o = [w_[CHUNK:] + _mm(loc[b][p_]["a_in"], _x2b(v)) for w_, v, (b, p_) in zip(ws, v_new, units)]
    upd = [_mm(loc[b][p_]["kd"].T, v) for v, (b, p_) in zip(v_new, units)]
    for i, (b, p_) in enumerate(units):
        s_scr[b, p_] = loc[b][p_]["eg_last"][0:1, :] * s_prev[i] + jnp.where(bdm, upd[i], 0.0)

    def store(b, p_, val):
        o_ref[b, :, p_ * LANES:(p_ + 1) * LANES] = val

    _head_norm_store([[o[b * N_PAIRS + p_] for p_ in range(N_PAIRS)] for b in range(bb)],
                     [[zg_ref[b, :, p_ * LANES:(p_ + 1) * LANES] for p_ in range(N_PAIRS)] for b in range(bb)],
                     nw_ref[...], prm["bones"], store)


def _ml_units(chunks, prm):
    n = len(chunks)
    incl, _ = _pair_masks()
    lo = _lane_lo((CHUNK, LANES))
    ltri, upair = _ltri(), _upair()
    ones2 = prm["bones"]
    bdm2 = jnp.concatenate([_bd_mask(), _bd_mask()], axis=1)
    units = [(c, p_) for c in range(n) for p_ in range(N_PAIRS)]
    pair = lambda x, p_: x[:, p_ * LANES:(p_ + 1) * LANES]

    rows = [ch["rows"] + prm["rowadd"] for ch in chunks]
    b_cols = _mm_sel_l(ltri, jnp.concatenate([-_softplus(-ch["zs"]) for ch in chunks], axis=1))
    b_rows = _mm_sel_r(jnp.concatenate([-_softplus(-r) for r in rows], axis=0), upair)
    comp = [dict(ig_c=ch["zs"], b_c=b_cols[:, c * LANES:(c + 1) * LANES], ig_r=rows[c],
                 b_r=b_rows[c * 16:(c + 1) * 16]) for c, ch in enumerate(chunks)]

    qp = [pair(chunks[c]["q"], p_) for c, p_ in units]
    kp = [pair(chunks[c]["k"], p_) * (HEAD_DIM ** -0.5) for c, p_ in units]
    vp = [pair(chunks[c]["v"], p_) for c, p_ in units]
    b_col = [_expand(comp[c]["b_c"], 18 + 2 * p_, 19 + 2 * p_) for c, p_ in units]
    ig_col = [_expand(comp[c]["ig_c"], 12 + 2 * p_, 13 + 2 * p_) for c, p_ in units]
    dl = [jnp.where(incl, bc - comp[c]["b_r"][6 + p_:7 + p_, :] + comp[c]["ig_r"][3 + p_:4 + p_, :], NEG)
          for bc, (c, p_) in zip(b_col, units)]
    mx = [jnp.where(lo, jnp.max(jnp.where(lo, d, NEG), axis=1, keepdims=True),
                    jnp.max(jnp.where(lo, NEG, d), axis=1, keepdims=True)) for d in dl]
    inter = [bc + chunks[c]["ms"][p_] for bc, (c, p_) in zip(b_col, units)]
    mt = [jnp.maximum(a, b) for a, b in zip(inter, mx)]
    s = [_mm(q_, _x2b(k_), _NT) * jnp.exp(d - m) for q_, k_, d, m in zip(qp, kp, dl, mt)]
    sv = [_mm(s_, jnp.concatenate([_x2b(v_), ones2], axis=1)) for s_, v_ in zip(s, vp)]
    cn_prev = [chunks[c]["cn"][p_] for c, p_ in units]
    qcn = [_mm(q_, cn) for q_, cn in zip(qp, cn_prev)]
    w_inter = [jnp.exp(a - m) for a, m in zip(inter, mt)]
    h = [(wi * qc[:, :LANES] + sv_[:, :LANES])
         / jnp.maximum(jnp.abs(wi * qc[:, LANES:] + sv_[:, LANES:]), jnp.exp(-m))
         for wi, qc, sv_, m in zip(w_inter, qcn, sv, mt)]
    m_end = [_last_row(m) for m in mt]
    b_last = [_last_row(bc) for bc in b_col]
    a_end = [jnp.exp(bl + chunks[c]["ms"][p_] - me) for bl, me, (c, p_) in zip(b_last, m_end, units)]
    kw_t = [(k_ * jnp.exp(bl - bc + ig - me)).T for k_, bl, bc, ig, me in zip(kp, b_last, b_col, ig_col, m_end)]
    upd = [_mm(k_, jnp.concatenate([v_, jnp.ones((CHUNK, LANES), F32)], axis=1)) for k_, v_ in zip(kw_t, vp)]
    out = [[None] * N_PAIRS for _ in range(n)]
    for i, (c, p_) in enumerate(units):
        a_row = a_end[i][0:1, :]
        new_state = jnp.concatenate([a_row, a_row], axis=1) * cn_prev[i] + jnp.where(bdm2, upd[i], 0.0)
        out[c][p_] = (h[i], new_state, m_end[i][0:1, :])
    return out


def _ml_seq_step(zq_ref, zk_ref, zv_ref, zo_ref, zs_ref, rows_ref, add_ref, rowadd_ref, nw_ref, o_ref,
                 cn_scr, m_scr, *, bb):
    prm = dict(bones=_block_ones(), rowadd=rowadd_ref[...])
    chunks = []
    for b in range(bb):
        chunks.append(dict(q=zq_ref[b], k=zk_ref[b], v=zv_ref[b], zs=zs_ref[b] + add_ref[...],
                           rows=rows_ref[b, 0],
                           ms=[m_scr[b, p_:p_ + 1, :] for p_ in range(N_PAIRS)],
                           cn=[cn_scr[b, p_] for p_ in range(N_PAIRS)]))
    res = _ml_units(chunks, prm)
    for b in range(bb):
        for p_ in range(N_PAIRS):
            cn_scr[b, p_] = res[b][p_][1]
            m_scr[b, p_:p_ + 1, :] = res[b][p_][2]

    def store(b, p_, val):
        o_ref[b, :, p_ * LANES:(p_ + 1) * LANES] = val

    _head_norm_store([[res[b][p_][0] for p_ in range(N_PAIRS)] for b in range(bb)],
                     [[zo_ref[b, :, p_ * LANES:(p_ + 1) * LANES] for p_ in range(N_PAIRS)] for b in range(bb)],
                     nw_ref[...], prm["bones"], store)


def _mixers_seq_kernel(dq_ref, dk_ref, dv_ref, dg_ref, lq_ref, lk_ref, lv_ref, lo_ref, zs_ref, rows_ref,
                       s0_ref, cn0_ref, m0_ref,
                       add_ref, alog_ref, rowadd_ref, rowalog_ref, dnw_ref, mlw_ref,
                       dn_ref, ml_ref, s_out_ref, cn_out_ref, m_out_ref,
                       s_scr, cn_scr, m_scr, *, bb, prec):
    tstep = pl.program_id(1)

    @pl.when(tstep == 0)
    def _():
        s_scr[...] = s0_ref[...]
        cn_scr[...] = cn0_ref[...]
        m_scr[...] = m0_ref[...]

    _dn_seq_step(dq_ref, dk_ref, dv_ref, dg_ref, zs_ref, rows_ref, add_ref, alog_ref, rowadd_ref,
                 rowalog_ref, dnw_ref, dn_ref, s_scr, bb=bb, prec=prec)
    _ml_seq_step(lq_ref, lk_ref, lv_ref, lo_ref, zs_ref, rows_ref, add_ref, rowadd_ref, mlw_ref, ml_ref,
                 cn_scr, m_scr, bb=bb)

    @pl.when(tstep == pl.num_programs(1) - 1)
    def _():
        s_out_ref[...] = s_scr[...]
        cn_out_ref[...] = cn_scr[...]
        m_out_ref[...] = m_scr[...]


def _mixers_seq(z3, zc, rows4, s0, cn0, m0, prm, bb, prec):
    batch, seq_len, _ = z3.shape
    fix2 = lambda g, i: (0, 0)
    zspec = lambda width, col: pl.BlockSpec((bb, CHUNK, width), lambda g, i: (g, i, col))
    per_seq = lambda *shape: pl.BlockSpec((bb,) + shape, lambda g, i: (g,) + (0,) * len(shape))
    s_spec, cn_spec, m_spec = per_seq(N_PAIRS, LANES, LANES), per_seq(N_PAIRS, LANES, 2 * LANES), per_seq(8, LANES)
    small = lambda rows: pl.BlockSpec((rows, LANES), fix2)
    in_specs = ([zspec(HEADS_W, c) for c in (0, 1, 2, 3, 4, 5, 6, 4)] + [zspec(LANES, GATE_COL_BLOCK)]
                + [pl.BlockSpec((bb, 1, 16, LANES), lambda g, i: (g, i, 0, 0)), s_spec, cn_spec, m_spec,
                   small(1), small(1), small(16), small(16), small(1), small(1)])
    out_blk = pl.BlockSpec((bb, CHUNK, HEADS_W), lambda g, i: (g, i, 0))
    act = jax.ShapeDtypeStruct((batch, seq_len, HEADS_W), BF16)
    return pl.pallas_call(
        functools.partial(_mixers_seq_kernel, bb=bb, prec=prec),
        out_shape=(act, act, jax.ShapeDtypeStruct(s0.shape, F32), jax.ShapeDtypeStruct(cn0.shape, F32),
                   jax.ShapeDtypeStruct(m0.shape, F32)),
        grid=(batch // bb, seq_len // CHUNK),
        in_specs=in_specs,
        out_specs=(out_blk, out_blk, s_spec, cn_spec, m_spec),
        scratch_shapes=[pltpu.VMEM((bb, N_PAIRS, LANES, LANES), F32),
                        pltpu.VMEM((bb, N_PAIRS, LANES, 2 * LANES), F32), pltpu.VMEM((bb, 8, LANES), F32)],
        compiler_params=pltpu.CompilerParams(dimension_semantics=("parallel", "arbitrary"),
                                             vmem_limit_bytes=VMEM_LIMIT),
        name="mixers_seq",
    )(zc, zc, zc, zc, z3, z3, z3, zc, z3, rows4, s0, cn0, m0, prm["add_row"], prm["alog_row"],
      prm["rowadd"], prm["rowalog"], prm["dn_normw"], prm["ml_normw"])


def _lane_mixers_kernel(zq, zk, zv, zg, lq, lk, lv, lo, zs, cpq, cpk, cpv, cwq, cwk, cwv, bias, alog, dnw, mlw,
                        s0, c0, n0, m0, *rest, n_tok, n_prev):
    if n_prev:
        s_prev, c_prev, dn_o, ml_o, s_st, c_st, n_o, m_o, k_s, q_s = rest
        s_st[0:n_prev] = s_prev[...]
        c_st[0:n_prev] = c_prev[...]
    else:
        dn_o, ml_o, s_st, c_st, n_o, m_o, k_s, q_s = rest
    s_o, c_o = s_st.at[n_prev], c_st.at[n_prev]
    h = pl.program_id(0)
    nb = s_o.shape[-1]
    zrow = lambda t, r: zs[t, pl.ds(r, 1), :]
    prow = lambda ref, r: ref[pl.ds(r, 1), :]
    bcast = lambda ref, dk: jnp.broadcast_to(ref[pl.ds(dk, 1), :], (HEAD_DIM, nb))
    zeros = jnp.zeros((HEAD_DIM, nb), F32)

    def conv(z_ref, cp_ref, cw_ref):
        e = [cp_ref[j] for j in range(CONV_W - 1)] + [z_ref[t] for t in range(n_tok)]
        outs = []
        for t in range(n_tok):
            acc = e[t] * cw_ref[0]
            for j in range(1, CONV_W):
                acc = acc + e[t + j] * cw_ref[j]
            outs.append(_silu(acc))
        return outs

    def l2n(x, scale):
        return x * (lax.rsqrt(jnp.sum(x * x, axis=0, keepdims=True) + EPS) * scale)

    def head_norm(x, w):
        return x * lax.rsqrt(jnp.mean(x * x, axis=0, keepdims=True) + EPS) * w

    qc, kc, vc = conv(zq, cpq, cwq), conv(zk, cpk, cwk), conv(zv, cpv, cwv)
    s_o[...] = s0[...]
    for t in range(n_tok):
        k_s[...] = l2n(kc[t], 1.0)
        q_s[...] = l2n(qc[t], HEAD_DIM ** -0.5)
        beta = _sigmoid(zrow(t, h))
        a = jnp.exp(-jnp.exp(prow(alog, N_HEADS + h))
                    * _softplus(zrow(t, N_HEADS + h) + prow(bias, N_HEADS + h)))
        ks = lax.fori_loop(0, HEAD_DIM, lambda dk, acc: acc + bcast(k_s, dk) * s_o[dk], zeros, unroll=8)
        delta = beta * (vc[t] - a * ks)

        def dn_update(dk, acc, a=a, delta=delta):
            s_new = a * s_o[dk] + bcast(k_s, dk) * delta
            s_o[dk] = s_new
            return acc + bcast(q_s, dk) * s_new

        o = lax.fori_loop(0, HEAD_DIM, dn_update, zeros, unroll=8)
        dn_o[t] = (head_norm(o, dnw[...]) * _silu(zg[t])).astype(BF16)

    c_o[...] = c0[...]
    n = n0[...]
    m = m0[...]
    for t in range(n_tok):
        q, k, v = lq[t], lk[t] * (HEAD_DIM ** -0.5), lv[t]
        ig = zrow(t, 2 * N_HEADS + h) + prow(bias, 2 * N_HEADS + h)
        lsf = -_softplus(-(zrow(t, 3 * N_HEADS + h) + prow(bias, 3 * N_HEADS + h)))
        m_new = jnp.maximum(lsf + m, ig)
        fp = jnp.exp(lsf + m - m_new)
        ip = jnp.exp(ig - m_new)
        k_s[...] = k * ip
        q_s[...] = q

        def ml_update(dk, acc, fp=fp, v=v):
            c_new = fp * c_o[dk] + bcast(k_s, dk) * v
            c_o[dk] = c_new
            return acc + bcast(q_s, dk) * c_new

        num = lax.fori_loop(0, HEAD_DIM, ml_update, zeros, unroll=8)
        n = fp * n + ip * k
        den = jnp.sum(q * n, axis=0, keepdims=True)
        hh = num / jnp.maximum(jnp.abs(den), jnp.exp(-m_new))
        m = m_new
        ml_o[t] = (head_norm(hh, mlw[...]) * _sigmoid(lo[t])).astype(BF16)
    n_o[...] = n
    m_o[...] = m


def _lane_mixers(z_t, conv_t, s0, c0, n0, m0, layer, prm, prev):
    n_tok, _, nb = z_t.shape
    hd = HEAD_DIM
    n_prev = 0 if prev is None else prev[0].shape[0]
    assert n_prev == layer
    zblk = lambda first: pl.BlockSpec((n_tok, hd, nb), lambda h: (0, first + h, 0))
    cblk = lambda taps, seg: pl.BlockSpec((taps, hd, nb), lambda h: (0, seg * N_HEADS + h, 0))
    wblk = lambda seg: pl.BlockSpec((CONV_W, hd, nb), lambda h: (0, seg * N_HEADS + h, 0))
    whole = lambda shape: pl.BlockSpec(shape, lambda h: (0,) * len(shape))
    mat_in = pl.BlockSpec((None, None, hd, hd, nb), lambda h: (layer, h, 0, 0, 0))
    mat_out = pl.BlockSpec((n_prev + 1, None, hd, hd, nb), lambda h: (0, h, 0, 0, 0))
    prev_specs = [pl.BlockSpec((n_prev, None, hd, hd, nb), lambda h: (0, h, 0, 0, 0))] * 2 if n_prev else []
    small_rows = 32
    in_specs = [zblk(seg * N_HEADS) for seg in range(8)] + [
        pl.BlockSpec((n_tok, small_rows, nb), lambda h: (0, (Z_COLS - LANES) // small_rows, 0)),
        cblk(CONV_W - 1, 0), cblk(CONV_W - 1, 1), cblk(CONV_W - 1, 2), wblk(0), wblk(1), wblk(2),
        whole((small_rows, nb)), whole((small_rows, nb)), whole((hd, nb)), whole((hd, nb)),
        mat_in, mat_in,
        pl.BlockSpec((None, None, hd, nb), lambda h: (layer, h, 0, 0)),
        pl.BlockSpec((None, None, 1, nb), lambda h: (layer, h, 0, 0))] + prev_specs
    out_blk = pl.BlockSpec((n_tok, hd, nb), lambda h: (0, h, 0))
    act = jax.ShapeDtypeStruct((n_tok, HEADS_W, nb), BF16)
    mat = jax.ShapeDtypeStruct((n_prev + 1, N_HEADS, hd, hd, nb), F32)
    return pl.pallas_call(
        functools.partial(_lane_mixers_kernel, n_tok=n_tok, n_prev=n_prev),
        out_shape=(act, act, mat, mat, jax.ShapeDtypeStruct((N_HEADS, hd, nb), F32),
                   jax.ShapeDtypeStruct((N_HEADS, 1, nb), F32)),
        grid=(N_HEADS,),
        in_specs=in_specs,
        out_specs=(out_blk, out_blk, mat_out, mat_out,
                   pl.BlockSpec((None, hd, nb), lambda h: (h, 0, 0)), pl.BlockSpec((None, 1, nb), lambda h: (h, 0, 0))),
        scratch_shapes=[pltpu.VMEM((hd, nb), F32), pltpu.VMEM((hd, nb), F32)],
        compiler_params=pltpu.CompilerParams(dimension_semantics=("parallel",), vmem_limit_bytes=VMEM_LIMIT),
        name="lane_mixers",
    )(*([z_t] * 9), conv_t, conv_t, conv_t, prm["conv_wb"], prm["conv_wb"], prm["conv_wb"], prm["bias_b"],
      prm["alog_b"], prm["dn_normb"], prm["ml_normb"], s0, c0, n0, m0, *(prev or ()))


_IN_OFFS = [int(v) for v in np.concatenate([[0], np.cumsum(IN_SIZES)])]
PACK_ROWS = 256


def _pack_w_kernel(wt_ref, o_ref):
    o = _IN_OFFS
    seg_t = lambda a, b: wt_ref[0, a:b, :].T
    b0 = (o[9] // LANES) * LANES
    assert o[3] % LANES == 0 and o[9] - b0 == 2 * N_HEADS
    t_a = seg_t(o[3], o[3] + LANES)
    t_b = seg_t(b0, b0 + LANES)
    lane = _iota(t_a.shape, 1)
    small = jnp.where(lane < 2 * N_HEADS, t_a, jnp.where(lane < 4 * N_HEADS, t_b, 0.0))
    cols = [seg_t(o[0], o[3]), seg_t(o[5], o[9]), seg_t(o[11], o[13]), small]
    o_ref[0] = jnp.concatenate(cols, axis=-1).astype(BF16)


def _pack_w_in(w_in):
    depth, k, n = w_in.shape
    w_t = jnp.transpose(w_in, (0, 2, 1))
    return pl.pallas_call(
        _pack_w_kernel, out_shape=jax.ShapeDtypeStruct((depth, k, Z_COLS), BF16),
        grid=(depth, k // PACK_ROWS),
        in_specs=[pl.BlockSpec((1, n, PACK_ROWS), lambda l, i: (l, 0, i))],
        out_specs=pl.BlockSpec((1, PACK_ROWS, Z_COLS), lambda l, i: (l, i, 0)),
        compiler_params=pltpu.CompilerParams(dimension_semantics=("parallel", "parallel"),
                                             vmem_limit_bytes=VMEM_LIMIT),
        name="pack_w_in",
    )(w_t)


def _pair_rows(v6):
    return jnp.repeat(v6.reshape(N_PAIRS, 2), HEAD_DIM, axis=1).reshape(N_PAIRS, LANES)


def _layer_params(l, a):
    f = lambda name: a[name][l].astype(F32)
    zeros6 = jnp.zeros((N_HEADS,), F32)
    add_row = jnp.concatenate([zeros6, f("dn_dt_bias"), f("ml_i_bias"), f("ml_f_bias")])
    add_row = jnp.pad(add_row, (0, LANES - 4 * N_HEADS))[None, :]
    alog_row = jnp.pad(jnp.concatenate([zeros6, f("dn_a_log")]), (0, LANES - 2 * N_HEADS))[None, :]
    rowadd = jnp.concatenate([_pair_rows(f("dn_dt_bias")), _pair_rows(f("ml_i_bias")),
                              _pair_rows(f("ml_f_bias")), jnp.zeros((7, LANES), F32)], axis=0)
    rowalog = jnp.concatenate([_pair_rows(f("dn_a_log")), jnp.zeros((13, LANES), F32)], axis=0)
    eye_g = jnp.eye(S5_GROUPS, dtype=F32)
    bd_in = lambda b: jnp.einsum("gpc,gh->gchp", b, eye_g).reshape(S5_WIDTH, S5_STATE)
    bd_out = lambda c: jnp.einsum("gcp,gh->gphc", c, eye_g).reshape(S5_STATE, S5_WIDTH).astype(BF16)
    return dict(
        layer=l, norm_mix=f("norm_mix")[None, :],
        add_row=add_row, alog_row=alog_row, rowadd=rowadd, rowalog=rowalog,
        conv_w8=jnp.pad(f("dn_conv_w"), ((0, 8 - CONV_W), (0, 0))),
        dn_normw=jnp.tile(f("dn_norm"), 2)[None, :], ml_normw=jnp.tile(f("ml_norm"), 2)[None, :],
        s5=dict(lre=f("s5_lam_re").reshape(1, S5_STATE), lim=f("s5_lam_im").reshape(1, S5_STATE),
                ldt=jnp.repeat(f("s5_log_dt"), S5_P)[None, :],
                bre=bd_in(f("s5_b_re")), bim=bd_in(f("s5_b_im")),
                cre=bd_out(f("s5_c_re")), cim=bd_out(f("s5_c_im")),
                d=f("s5_d").reshape(1, S5_WIDTH), wg=f("s5_w_glu").astype(BF16),
                bg=f("s5_b_glu")[None, :]),
        norm_x=f("norm_x")[None, :], norm_ff=f("norm_ff")[None, :],
    )


def _shared_weights(a):
    cast = lambda name: a[name].astype(BF16)
    return dict(w_in=_pack_w_in(a["w_in"]), w_out=cast("w_out"), w_xq=cast("w_xq"), w_xo=cast("w_xo"),
                w_ff1=cast("w_ff1"), w_ff2=cast("w_ff2"), w_xk=cast("w_xk"), w_xv=cast("w_xv"))


def _row_form(z, n_chunks):
    zs = z[..., Z_COLS - LANES:Z_COLS - LANES + 4 * N_HEADS].reshape(n_chunks, CHUNK, 4, N_PAIRS, 2)
    r = jnp.transpose(zs[:, :, 1:], (0, 2, 3, 4, 1)).reshape(n_chunks, 9, LANES)
    return jnp.pad(r, ((0, 0), (0, 7), (0, 0)))


def _un_bd_pairs(s):
    n = s.shape[0]
    a = s[:, :, :HEAD_DIM, :HEAD_DIM]
    b = s[:, :, HEAD_DIM:, HEAD_DIM:]
    return jnp.stack([a, b], axis=2).reshape(n, N_HEADS, HEAD_DIM, HEAD_DIM)


def _unpack_m(mp):
    return mp[:, :N_PAIRS, ::HEAD_DIM].reshape(mp.shape[0], N_HEADS)


def _tile(n, pref):
    tile = min(n, pref)
    assert n % tile == 0, (n, pref)
    return tile


def _layer_prompt(x, mem_k, mem_v, prm, wts, final_g):
    batch, seq_len, _ = x.shape
    layer = prm["layer"]
    z3, zc = _norm_matmul_conv(x, prm["norm_mix"], wts["w_in"], jnp.zeros((batch, 8, CONV_CH), F32),
                               prm["conv_w8"], layer, _tile(seq_len, CONV_ROW_TILE))
    rows4 = _row_form(z3, batch * seq_len // CHUNK).reshape(batch, seq_len // CHUNK, 16, LANES)
    new_conv = z3[:, seq_len - (CONV_W - 1):, :CONV_CH]

    dn, ml, s_new, cn_new, m_new = _mixers_seq(
        z3, zc, rows4, jnp.zeros((batch, N_PAIRS, LANES, LANES), F32),
        jnp.zeros((batch, N_PAIRS, LANES, 2 * LANES), F32), jnp.zeros((batch, 8, LANES), F32), prm,
        _tile(batch, SEQ_BATCH), SEQ_SOLVE)
    zero_h = jnp.zeros((batch, S5_STATE), F32)
    s5o, hr, hi = _s5(z3, zero_h, zero_h, prm["s5"], _tile(seq_len, S5_STEPS), S5_STATE, True)

    g_fin = final_g if final_g is not None else prm["norm_ff"]
    x3 = _post_mix(dn, ml, s5o, x, mem_k, mem_v, wts, prm, g_fin, layer, _tile(seq_len, POST_TILE),
                   final_g is not None)
    st = (new_conv, _un_bd_pairs(s_new), _un_bd_pairs(cn_new[..., :LANES]),
          _un_bd_pairs(cn_new[..., LANES:])[..., 0], _unpack_m(m_new),
          hr.reshape(batch, S5_GROUPS, S5_P), hi.reshape(batch, S5_GROUPS, S5_P))
    return x3, st


def _layer_sample(x, cache_k, cache_v, conv_buf, dn_s, ml_c, ml_n, ml_m, ssm_re, ssm_im, prm, wts, batch,
                  seq_len, final_g, prev_stacks):
    t = batch * seq_len
    layer = prm["layer"]
    z = _norm_matmul(x, prm["norm_mix"], wts["w_in"], layer, t)
    z3 = z.reshape(seq_len, batch, Z_COLS)
    new_conv = jnp.concatenate([conv_buf, jnp.transpose(z3[:, :, :CONV_CH], (1, 0, 2))], axis=1)[:, seq_len:]
    lanes = lambda v: jnp.broadcast_to(v[..., None], v.shape + (batch,))
    mix_prm = dict(conv_wb=lanes(prm["conv_w8"][:CONV_W]), bias_b=lanes(prm["add_row"][0, :32]),
                   alog_b=lanes(prm["alog_row"][0, :32]), dn_normb=lanes(prm["dn_normw"][0, :HEAD_DIM]),
                   ml_normb=lanes(prm["ml_normw"][0, :HEAD_DIM]))
    dn_t, ml_t, s_stack, c_stack, n_new, m_new = _lane_mixers(
        jnp.transpose(z3, (0, 2, 1)), jnp.transpose(conv_buf, (1, 2, 0)), dn_s, ml_c, ml_n, ml_m, layer, mix_prm,
        prev_stacks)
    to_rows = lambda a: jnp.transpose(a, (0, 2, 1)).reshape(1, t, HEADS_W)
    dn, ml = to_rows(dn_t), to_rows(ml_t)

    s5o, hr, hi = _s5(z[0, :, 3072:3072 + S5_WIDTH], ssm_re.reshape(batch, S5_STATE),
                      ssm_im.reshape(batch, S5_STATE), prm["s5"], seq_len, LANES, False)

    x1, q = _out_proj(dn, ml, s5o, x, wts, prm, layer, t)
    q3 = jnp.pad(jnp.transpose(q.reshape(seq_len, batch, X_WIDTH), (1, 0, 2)),
                 ((0, 0), (0, Q_ROWS - seq_len), (0, 0)))
    att = _attn_batch(q3, cache_k, cache_v, layer, _tile(batch, ATTN_BATCH))[:, :seq_len]
    att = jnp.transpose(att, (1, 0, 2)).reshape(1, t, X_WIDTH)
    g_fin = final_g if final_g is not None else prm["norm_ff"]
    x3 = _ffn(x1, att, wts, prm, g_fin, layer, t, final_g is not None)
    st = (new_conv, jnp.transpose(n_new, (2, 0, 1)), jnp.transpose(m_new[:, 0, :], (1, 0)),
          hr.reshape(batch, S5_GROUPS, S5_P), hi.reshape(batch, S5_GROUPS, S5_P))
    return x3, st, (s_stack, c_stack)


def kernel(x_prompt, x_sample, mem_prompt, state_dn_conv, state_dn_s, state_ml_c, state_ml_n, state_ml_m, state_ssm_re, state_ssm_im, cache_mem_k, cache_mem_v, norm_mix, w_in, dn_conv_w, dn_a_log, dn_dt_bias, dn_norm, ml_i_bias, ml_f_bias, ml_norm, s5_lam_re, s5_lam_im, s5_log_dt, s5_b_re, s5_b_im, s5_c_re, s5_c_im, s5_d, s5_w_glu, s5_b_glu, w_out, norm_x, norm_mem, w_xq, w_xk, w_xv, w_xo, norm_ff, w_ff1, w_ff2, norm_final):
    a = dict(norm_mix=norm_mix, w_in=w_in, dn_conv_w=dn_conv_w, dn_a_log=dn_a_log, dn_dt_bias=dn_dt_bias,
             dn_norm=dn_norm, ml_i_bias=ml_i_bias, ml_f_bias=ml_f_bias, ml_norm=ml_norm,
             s5_lam_re=s5_lam_re, s5_lam_im=s5_lam_im, s5_log_dt=s5_log_dt, s5_b_re=s5_b_re,
             s5_b_im=s5_b_im, s5_c_re=s5_c_re, s5_c_im=s5_c_im, s5_d=s5_d, s5_w_glu=s5_w_glu,
             s5_b_glu=s5_b_glu, w_out=w_out, norm_x=norm_x, w_xq=w_xq, w_xo=w_xo, norm_ff=norm_ff,
             w_ff1=w_ff1, w_ff2=w_ff2)
    prms = [_layer_params(l, a) for l in range(DEPTH)]
    wts = _shared_weights(dict(a, w_xk=w_xk, w_xv=w_xv))
    g_final = norm_final.astype(F32)[None, :]

    bp, lp, _ = x_prompt.shape
    bs, ls, _ = x_sample.shape

    xp = x_prompt
    mk, mk_t, mv, mv_t = _mem_proj(mem_prompt, norm_mem.astype(F32), wts["w_xk"], wts["w_xv"])
    p_st = []
    for l in range(DEPTH):
        xp, st = _layer_prompt(xp, mk, mv, prms[l], wts, g_final if l == DEPTH - 1 else None)
        p_st.append(st)
    y_prompt = xp
    p_mem = (mk_t.reshape(DEPTH, bp, N_MEM, X_HEADS, X_HD), mv_t.reshape(DEPTH, bp, N_MEM, X_HEADS, X_HD))

    xs = jnp.transpose(x_sample, (1, 0, 2)).reshape(1, ls * bs, D_MODEL)
    cache_k = cache_mem_k.reshape(DEPTH, bs, N_MEM * X_HEADS, X_HD)
    cache_v = cache_mem_v.reshape(DEPTH, bs, N_MEM * X_HEADS, X_HD)
    dn_s_t = jnp.transpose(state_dn_s, (0, 2, 3, 4, 1))
    ml_c_t = jnp.transpose(state_ml_c, (0, 2, 3, 4, 1))
    ml_n_t = jnp.transpose(state_ml_n, (0, 2, 3, 1))
    ml_m_t = jnp.transpose(state_ml_m, (0, 2, 1))[:, :, None, :]
    s_st = []
    stacks = None
    for l in range(DEPTH):
        xs, st, stacks = _layer_sample(xs, cache_k, cache_v, state_dn_conv[l], dn_s_t, ml_c_t, ml_n_t, ml_m_t,
                                       state_ssm_re[l], state_ssm_im[l], prms[l], wts, bs, ls,
                                       g_final if l == DEPTH - 1 else None, stacks)
        s_st.append(st)
    y_sample = jnp.transpose(xs.reshape(ls, bs, D_MODEL), (1, 0, 2))
    s_mats = tuple(jnp.transpose(m, (0, 4, 1, 2, 3)) for m in stacks)

    stack = lambda sts, i: jnp.stack([s[i] for s in sts])
    return ((y_prompt, y_sample) + tuple(stack(p_st, i) for i in range(7)) + p_mem
            + (stack(s_st, 0),) + s_mats + tuple(stack(s_st, i) for i in range(1, 5)))
```

```python
import functools
import math

import numpy as np
import jax
import jax.numpy as jnp
from jax import lax
from jax.experimental import pallas as pl
from jax.experimental.pallas import tpu as pltpu

F32 = jnp.float32
BF16 = jnp.bfloat16

D_MODEL = 1024
DEPTH = 2
N_HEADS = 6
HEAD_DIM = 64
N_PAIRS = N_HEADS // 2
HEADS_W = N_HEADS * HEAD_DIM
CONV_W = 4
CONV_CH = 3 * HEADS_W
S5_WIDTH = 256
S5_GROUPS = 16
S5_P = 64
S5_STATE = S5_GROUPS * S5_P
N_MEM = 256
X_HEADS = 4
X_HD = 128
X_WIDTH = X_HEADS * X_HD
D_FF = 4 * D_MODEL
EPS = 1e-6
IN_SIZES = (384, 384, 384, 6, 6, 384, 384, 384, 384, 6, 6, 384, 256)

LANES = 128
CHUNK = 64
Q_ROWS = 8
NEG = -1e30
Z_COLS = 3456
DN_GATE_COL = 3 * HEADS_W
ML_GATE_COL = 7 * HEADS_W
Z_DROP = DN_GATE_COL + HEADS_W
S5_COL_BLOCK = (3072 - Z_DROP) // S5_WIDTH
GATE_COL_BLOCK = (Z_COLS - LANES - Z_DROP) // LANES
ACT_COLS = CONV_CH + 2 * HEADS_W
VMEM_LIMIT = 52 * 1024 * 1024

ROW_TILE = 1024
CONV_ROW_TILE = 512
POST_TILE = 512
SEQ_BATCH = 8
S5_STEPS = 64
ATTN_BATCH = 8
FF_CHUNK = 1024
TRI_BASE = 8
SEQ_SOLVE = (1, True, 1)

_NN = (((1,), (0,)), ((), ()))
_NT = (((1,), (1,)), ((), ()))


def _dot(a, b, dims=_NN):
    return lax.dot_general(a, b, dims, preferred_element_type=F32)


def _split(a, n):
    out = []
    r = a
    for i in range(n):
        h = r.astype(BF16)
        out.append(h)
        if i + 1 < n:
            r = r - h.astype(F32)
    return out


def _mm(a, b, dims=_NN, prec=1):
    if prec == 1:
        return _dot(a.astype(BF16), b.astype(BF16), dims)
    return _mmp(_parts(a, prec), _parts(b, prec), dims)


def _parts(a, prec):
    return tuple(_split(a, 1 if prec == 1 else 2))


def _mmp(ap, bp, dims=_NN):
    acc = None
    if len(bp) > 1:
        acc = _dot(ap[0], bp[1], dims)
    if len(ap) > 1:
        cross = _dot(ap[1], bp[0], dims)
        acc = cross if acc is None else acc + cross
    lead = _dot(ap[0], bp[0], dims)
    return lead if acc is None else acc + lead


def _mm_sel_r(a, sel, n=3):
    parts = _split(a, n)
    acc = _dot(parts[-1], sel)
    for p_ in parts[-2::-1]:
        acc = acc + _dot(p_, sel)
    return acc


def _mm_sel_l(sel, a, n=3):
    parts = _split(a, n)
    acc = _dot(sel, parts[-1])
    for p_ in parts[-2::-1]:
        acc = acc + _dot(sel, p_)
    return acc


def _sigmoid(x):
    return 1.0 / (1.0 + jnp.exp(-x))


def _silu(x):
    return x * _sigmoid(x)


def _softplus(x):
    return jnp.maximum(x, 0.0) + jnp.log(1.0 + jnp.exp(-jnp.abs(x)))


def _rmsnorm_rows(x, g):
    return x * lax.rsqrt(jnp.mean(x * x, axis=-1, keepdims=True) + EPS) * g


def _iota(shape, dim):
    return lax.broadcasted_iota(jnp.int32, shape, dim)


def _idiv(x, n):
    return x >> (n.bit_length() - 1)


def _imod(x, n):
    return x & (n - 1)


def _lane_lo(shape):
    return _iota(shape, 1) < HEAD_DIM


def _x2(x):
    lo = _lane_lo(x.shape)
    return jnp.concatenate([jnp.where(lo, x, 0.0), jnp.where(lo, 0.0, x)], axis=0)


def _x2b(x):
    return _x2(x.astype(BF16))


def _fold(y):
    r = y.shape[0] // 2
    return y[:r] + y[r:]


def _expand(a, c0, c1):
    r = a.shape[0]
    return jnp.where(_lane_lo((r, LANES)), a[:, c0:c0 + 1], a[:, c1:c1 + 1])


def _bd_mask():
    return (_iota((LANES, LANES), 0) < HEAD_DIM) == (_iota((LANES, LANES), 1) < HEAD_DIM)


def _block_ones():
    return jnp.where(_bd_mask(), 1.0, 0.0).astype(BF16)


def _seg_mean(x, bones):
    return _mm_sel_r(x, bones, n=1) * (1.0 / HEAD_DIM)


def _ltri():
    return jnp.where(_iota((CHUNK, CHUNK), 1) <= _iota((CHUNK, CHUNK), 0), 1.0, 0.0).astype(BF16)


def _upair():
    a = _iota((LANES, LANES), 0)
    b = _iota((LANES, LANES), 1)
    return jnp.where(_bd_mask() & (a <= b), 1.0, 0.0).astype(BF16)


def _pair_masks():
    i = _iota((CHUNK, LANES), 0)
    j = _imod(_iota((CHUNK, LANES), 1), HEAD_DIM)
    return j <= i, j < i


def _blk_eq(n, b):
    return _idiv(_iota((n, n), 0), b) == _idiv(_iota((n, n), 1), b)


def _last_row(a):
    return jnp.broadcast_to(a[CHUNK - 1:CHUNK, :], a.shape)


def _tri_inverse_units(mbds, prec, newton):
    n = LANES
    eye = jnp.where(_iota((n, n), 0) == _iota((n, n), 1), 1.0, 0.0)
    n_sq = TRI_BASE.bit_length() - 2
    blk = _blk_eq(n, TRI_BASE)
    m0 = [jnp.where(blk, m, 0.0) for m in mbds]
    ts = [eye - m for m in m0]
    ps = [_mm(m, m, prec=prec) for m in m0]
    for it in range(n_sq):
        if it == n_sq - 1:
            ts = [t + _mm(t, p_, prec=prec) for t, p_ in zip(ts, ps)]
        else:
            both = [_mm(jnp.concatenate([t, p_], axis=0), p_, prec=prec) for t, p_ in zip(ts, ps)]
            ts = [t + b_[:n] for t, b_ in zip(ts, both)]
            ps = [b_[n:] for b_ in both]
    b = TRI_BASE
    while b < CHUNK:
        outer = _blk_eq(n, 2 * b) & jnp.logical_not(_blk_eq(n, b))
        tp = [_parts(t, prec) for t in ts]
        xs = [_mmp(_parts(jnp.where(outer, m, 0.0), prec), t_) for m, t_ in zip(mbds, tp)]
        ts = [t - _mmp(t_, _parts(x, prec)) for t, t_, x in zip(ts, tp, xs)]
        b *= 2
    if newton:
        res = [eye - (t + _mm(m, t, prec=3)) for m, t in zip(mbds, ts)]
        ts = [t + _mm(t, r) for t, r in zip(ts, res)]
    return ts


def _wspec(rows, cols, layer, row_block=0):
    return pl.BlockSpec((None, rows, cols), lambda i: (layer, row_block, 0), pipeline_mode=pl.Buffered(1))


def _rows3(nt):
    return lambda i: (i // nt, i % nt, 0)


def _dense_call(kernel_fn, name, out_shape, in_specs, out_specs, steps, args):
    return pl.pallas_call(
        kernel_fn, out_shape=out_shape, grid=(steps,), in_specs=in_specs, out_specs=out_specs,
        compiler_params=pltpu.CompilerParams(dimension_semantics=("parallel",), vmem_limit_bytes=VMEM_LIMIT),
        name=name)(*args)


def _in_proj_kernel(x_ref, g_ref, w_ref, o_ref):
    o_ref[0] = _dot(_rmsnorm_rows(x_ref[0], g_ref[...]).astype(BF16), w_ref[...])


def _norm_matmul(x, g, w, layer, tm):
    b, l, k = x.shape
    n = w.shape[2]
    nt = l // tm
    fix = lambda i: (0, 0)
    return _dense_call(
        _in_proj_kernel, "norm_matmul", jax.ShapeDtypeStruct((b, l, n), F32),
        [pl.BlockSpec((1, tm, k), _rows3(nt)), pl.BlockSpec((1, k), fix), _wspec(k, n, layer)],
        pl.BlockSpec((1, tm, n), _rows3(nt)), b * nt, (x, g, w))


def _conv_silu(u, w8, fix):
    out = fix(pltpu.roll(u, 3, axis=0), 3) * w8[0:1, :]
    for s in (2, 1):
        out = out + fix(pltpu.roll(u, s, axis=0), s) * w8[3 - s:4 - s, :]
    out = out + u * w8[3:4, :]
    return _silu(out)


def _in_proj_conv_kernel(x_ref, g_ref, w_ref, cp_ref, cw_ref, o_ref, c_ref, last_ref, u_scr, gate_scr, tail_scr,
                         *, nt, n_tiles):
    i = pl.program_id(0)
    tm = u_scr.shape[0]

    @pl.when(i == 0)
    def _():
        u_scr[...] = jnp.zeros_like(u_scr)
        gate_scr[...] = jnp.zeros_like(gate_scr)
        tail_scr[...] = jnp.zeros_like(tail_scr)

    prev = tail_scr[...]
    u_prev = u_scr[...]
    r8 = _iota((8, CONV_CH), 0)

    def fix(rolled, s):
        first = jnp.where(r8 < s, pltpu.roll(prev, s, axis=0), rolled[0:8])
        return jnp.concatenate([first, rolled[8:]], axis=0)

    c_ref[0, :, :CONV_CH] = _conv_silu(u_prev, cw_ref[...], fix)
    c_ref[0, :, CONV_CH:CONV_CH + HEADS_W] = _silu(gate_scr[:, :HEADS_W])
    c_ref[0, :, CONV_CH + HEADS_W:] = _sigmoid(gate_scr[:, HEADS_W:])

    z = _dot(_rmsnorm_rows(x_ref[0], g_ref[...]).astype(BF16), w_ref[...])
    o_ref[0] = z[:, Z_DROP:]
    last_ref[0] = z[tm - 8:, :CONV_CH]
    first_of_seq = jnp.minimum(i, n_tiles - 1) % nt == 0
    tail_scr[...] = jnp.where(first_of_seq, cp_ref[0], u_prev[tm - 8:])
    u_scr[...] = z[:, :CONV_CH]
    gate_scr[:, :HEADS_W] = z[:, DN_GATE_COL:DN_GATE_COL + HEADS_W]
    gate_scr[:, HEADS_W:] = z[:, ML_GATE_COL:ML_GATE_COL + HEADS_W]


def _norm_matmul_conv(x, g, w, convp, cw8, layer, tm):
    b, l, k = x.shape
    n = w.shape[2]
    nt = l // tm
    n_tiles = b * nt
    fix = lambda i: (0, 0)
    cur = lambda i: jnp.minimum(i, n_tiles - 1)
    tile = lambda i: (cur(i) // nt, cur(i) % nt, 0)
    prev_tile = lambda i: (jnp.maximum(i - 1, 0) // nt, jnp.maximum(i - 1, 0) % nt, 0)
    return pl.pallas_call(
        functools.partial(_in_proj_conv_kernel, nt=nt, n_tiles=n_tiles),
        out_shape=(jax.ShapeDtypeStruct((b, l, n - Z_DROP), F32), jax.ShapeDtypeStruct((b, l, ACT_COLS), F32),
                   jax.ShapeDtypeStruct((b, 8, CONV_CH), F32)),
        grid=(n_tiles + 1,),
        in_specs=[pl.BlockSpec((1, tm, k), tile), pl.BlockSpec((1, k), fix), _wspec(k, n, layer),
                  pl.BlockSpec((1, 8, CONV_CH), lambda i: (cur(i) // nt, 0, 0)), pl.BlockSpec((8, CONV_CH), fix)],
        out_specs=(pl.BlockSpec((1, tm, n - Z_DROP), tile), pl.BlockSpec((1, tm, ACT_COLS), prev_tile),
                   pl.BlockSpec((1, 8, CONV_CH), lambda i: (cur(i) // nt, 0, 0))),
        scratch_shapes=[pltpu.VMEM((tm, CONV_CH), F32), pltpu.VMEM((tm, 2 * HEADS_W), F32),
                        pltpu.VMEM((8, CONV_CH), F32)],
        compiler_params=pltpu.CompilerParams(dimension_semantics=("arbitrary",), vmem_limit_bytes=VMEM_LIMIT),
        name="norm_matmul_conv",
    )(x, g, w, convp, cw8)


def _mem_proj_kernel(x_ref, g_ref, wk_ref, wv_ref, k_ref, kt_ref, v_ref, vt_ref):
    x = x_ref[0]
    xhat = x * lax.rsqrt(jnp.mean(x * x, axis=-1, keepdims=True) + EPS)
    for l in range(DEPTH):
        xn = (xhat * g_ref[l:l + 1, :]).astype(BF16)
        for w_ref, o_ref, t_ref in ((wk_ref, k_ref, kt_ref), (wv_ref, v_ref, vt_ref)):
            res = _dot(xn, w_ref[l])
            o_ref[l, 0] = res
            for h in range(X_HEADS):
                t_ref[l, 0, pl.ds(h, N_MEM, stride=X_HEADS), :] = res[:, h * X_HD:(h + 1) * X_HD]


def _mem_proj(mem, g, wk, wv):
    b = mem.shape[0]
    whole = lambda shape: pl.BlockSpec(shape, lambda i: (0,) * len(shape))
    o_spec = pl.BlockSpec((DEPTH, 1, N_MEM, X_WIDTH), lambda i: (0, i, 0, 0))
    t_spec = pl.BlockSpec((DEPTH, 1, N_MEM * X_HEADS, X_HD), lambda i: (0, i, 0, 0))
    o_shape = jax.ShapeDtypeStruct((DEPTH, b, N_MEM, X_WIDTH), F32)
    t_shape = jax.ShapeDtypeStruct((DEPTH, b, N_MEM * X_HEADS, X_HD), F32)
    return _dense_call(
        _mem_proj_kernel, "mem_proj", (o_shape, t_shape, o_shape, t_shape),
        [pl.BlockSpec((1, N_MEM, D_MODEL), lambda i: (i, 0, 0)), whole((DEPTH, D_MODEL)),
         whole((DEPTH, D_MODEL, X_WIDTH)), whole((DEPTH, D_MODEL, X_WIDTH))],
        (o_spec, t_spec, o_spec, t_spec), b, (mem, g, wk, wv))


def _mix_residual(x, dn, ml, s5, w1_ref, w2_ref, w3_ref):
    acc = x + _dot(dn, w1_ref[...])
    acc = acc + _dot(ml, w2_ref[...])
    return acc + _dot(s5, w3_ref[...])


def _mlp_residual(x2, g_ref, f1_ref, f2_ref):
    xn = _rmsnorm_rows(x2, g_ref[...]).astype(BF16)
    acc = x2
    for c in range(D_FF // FF_CHUNK):
        hf = _dot(xn, f1_ref[:, c * FF_CHUNK:(c + 1) * FF_CHUNK])
        a = jnp.square(jnp.maximum(hf, 0.0)).astype(BF16)
        acc = acc + _dot(a, f2_ref[c * FF_CHUNK:(c + 1) * FF_CHUNK, :])
    return acc


def _out_proj_kernel(dn_ref, ml_ref, s5_ref, x_ref, w1_ref, w2_ref, w3_ref, g_ref, wq_ref, x1_ref, q_ref):
    acc = _mix_residual(x_ref[0], dn_ref[0], ml_ref[0], s5_ref[...], w1_ref, w2_ref, w3_ref)
    x1_ref[0] = acc
    q_ref[0] = _dot(_rmsnorm_rows(acc, g_ref[...]).astype(BF16), wq_ref[...])


def _w_out_specs(layer):
    return [_wspec(HEADS_W, D_MODEL, layer, 0), _wspec(HEADS_W, D_MODEL, layer, 1),
            _wspec(S5_WIDTH, D_MODEL, layer, 2 * HEADS_W // S5_WIDTH)]


def _out_proj(dn, ml, s5, x, wts, prm, layer, tm):
    b, l, _ = x.shape
    nt = l // tm
    row = _rows3(nt)
    vec = pl.BlockSpec((1, D_MODEL), lambda i: (0, 0))
    return _dense_call(
        _out_proj_kernel, "out_proj",
        (jax.ShapeDtypeStruct((b, l, D_MODEL), F32), jax.ShapeDtypeStruct((b, l, X_WIDTH), F32)),
        [pl.BlockSpec((1, tm, HEADS_W), row), pl.BlockSpec((1, tm, HEADS_W), row),
         pl.BlockSpec((tm, S5_WIDTH), lambda i: (i % nt, i // nt)), pl.BlockSpec((1, tm, D_MODEL), row)]
        + _w_out_specs(layer) + [vec, _wspec(D_MODEL, X_WIDTH, layer)],
        (pl.BlockSpec((1, tm, D_MODEL), row), pl.BlockSpec((1, tm, X_WIDTH), row)), b * nt,
        (dn, ml, s5, x, wts["w_out"], wts["w_out"], wts["w_out"], prm["norm_x"], wts["w_xq"]))


def _ffn_kernel(x_ref, o_ref, wo_ref, g_ref, f1_ref, f2_ref, gl_ref, y_ref, *, final_norm):
    acc = _mlp_residual(x_ref[0] + _dot(o_ref[0], wo_ref[...]), g_ref, f1_ref, f2_ref)
    y_ref[0] = _rmsnorm_rows(acc, gl_ref[...]) if final_norm else acc


def _ffn(x, att, wts, prm, g_last, layer, tm, final_norm):
    b, l, _ = x.shape
    nt = l // tm
    row = _rows3(nt)
    vec = pl.BlockSpec((1, D_MODEL), lambda i: (0, 0))
    return _dense_call(
        functools.partial(_ffn_kernel, final_norm=final_norm), "ffn", jax.ShapeDtypeStruct((b, l, D_MODEL), F32),
        [pl.BlockSpec((1, tm, D_MODEL), row), pl.BlockSpec((1, tm, X_WIDTH), row), _wspec(X_WIDTH, D_MODEL, layer),
         vec, _wspec(D_MODEL, D_FF, layer), _wspec(D_FF, D_MODEL, layer), vec],
        pl.BlockSpec((1, tm, D_MODEL), row), b * nt,
        (x, att, wts["w_xo"], prm["norm_ff"], wts["w_ff1"], wts["w_ff2"], g_last))


def _attn_heads(q, k_ref, v_ref):
    outs = []
    for h in range(X_HEADS):
        sl = slice(h * X_HD, (h + 1) * X_HD)
        s = _dot(q[:, sl], k_ref[0, :, sl].astype(BF16), _NT) * (X_HD ** -0.5)
        e = jnp.exp(s - jnp.max(s, axis=-1, keepdims=True))
        p = e / jnp.sum(e, axis=-1, keepdims=True)
        outs.append(_dot(p.astype(BF16), v_ref[0, :, sl].astype(BF16)))
    return jnp.concatenate(outs, axis=-1)


def _attn_batch_kernel(q_ref, k_ref, v_ref, o_ref, *, bb):
    rows = X_HEADS * Q_ROWS
    col_head = _imod(_iota((rows, N_MEM * X_HEADS), 1), X_HEADS)
    row_head = _idiv(_iota((rows, N_MEM * X_HEADS), 0), Q_ROWS)
    own = col_head == row_head
    q4 = [jnp.concatenate([q_ref[bi, :, h * X_HD:(h + 1) * X_HD] for h in range(X_HEADS)], axis=0).astype(BF16)
          for bi in range(bb)]
    s = [jnp.where(own, _dot(q_, k_ref[bi].astype(BF16), _NT) * (X_HD ** -0.5), NEG)
         for bi, q_ in enumerate(q4)]
    e = [jnp.exp(x - jnp.max(x, axis=-1, keepdims=True)) for x in s]
    p = [x / jnp.sum(x, axis=-1, keepdims=True) for x in e]
    o = [_dot(x.astype(BF16), v_ref[bi].astype(BF16)) for bi, x in enumerate(p)]
    for bi in range(bb):
        for h in range(X_HEADS):
            o_ref[bi, :, h * X_HD:(h + 1) * X_HD] = o[bi][h * Q_ROWS:(h + 1) * Q_ROWS].astype(BF16)


def _attn_batch(q, k, v, layer, bb):
    b, l, _ = q.shape
    blk = lambda i: (i, 0, 0)
    kv = pl.BlockSpec((None, bb, N_MEM * X_HEADS, X_HD), lambda i: (layer, i, 0, 0))
    return _dense_call(
        functools.partial(_attn_batch_kernel, bb=bb), "attn_batch", jax.ShapeDtypeStruct((b, l, X_WIDTH), BF16),
        [pl.BlockSpec((bb, l, X_WIDTH), blk), kv, kv], pl.BlockSpec((bb, l, X_WIDTH), blk), b // bb, (q, k, v))


def _post_mix_kernel(dn_ref, ml_ref, s5_ref, x_ref, k_ref, v_ref, w1_ref, w2_ref, w3_ref, gx_ref, wq_ref,
                     wo_ref, gf_ref, f1_ref, f2_ref, gl_ref, y_ref, *, final_norm):
    x1 = _mix_residual(x_ref[0], dn_ref[0], ml_ref[0], s5_ref[...], w1_ref, w2_ref, w3_ref)
    q = _dot(_rmsnorm_rows(x1, gx_ref[...]).astype(BF16), wq_ref[...]).astype(BF16)
    att = _attn_heads(q, k_ref, v_ref).astype(BF16)
    acc = _mlp_residual(x1 + _dot(att, wo_ref[...]), gf_ref, f1_ref, f2_ref)
    y_ref[0] = _rmsnorm_rows(acc, gl_ref[...]) if final_norm else acc


def _post_mix(dn, ml, s5, x, mem_k, mem_v, wts, prm, g_last, layer, tm, final_norm):
    b, l, _ = x.shape
    nt = l // tm
    row = _rows3(nt)
    mem = pl.BlockSpec((None, 1, N_MEM, X_WIDTH), lambda i: (layer, i // nt, 0, 0))
    vec = pl.BlockSpec((1, D_MODEL), lambda i: (0, 0))
    return _dense_call(
        functools.partial(_post_mix_kernel, final_norm=final_norm), "post_mix",
        jax.ShapeDtypeStruct((b, l, D_MODEL), F32),
        [pl.BlockSpec((1, tm, HEADS_W), row), pl.BlockSpec((1, tm, HEADS_W), row),
         pl.BlockSpec((tm, S5_WIDTH), lambda i: (i % nt, i // nt)), pl.BlockSpec((1, tm, D_MODEL), row), mem, mem]
        + _w_out_specs(layer)
        + [vec, _wspec(D_MODEL, X_WIDTH, layer), _wspec(X_WIDTH, D_MODEL, layer), vec,
           _wspec(D_MODEL, D_FF, layer), _wspec(D_FF, D_MODEL, layer), vec],
        pl.BlockSpec((1, tm, D_MODEL), row), b * nt,
        (dn, ml, s5, x, mem_k, mem_v, wts["w_out"], wts["w_out"], wts["w_out"], prm["norm_x"], wts["w_xq"],
         wts["w_xo"], prm["norm_ff"], wts["w_ff1"], wts["w_ff2"], g_last))


def _s5_kernel(u_ref, h0r_ref, h0i_ref, lre_ref, lim_ref, ldt_ref, bre_ref, bim_ref, cre_ref, cim_ref,
               d_ref, wg_ref, bg_ref, y_ref, hr_out, hi_out,
               xr_s, xi_s, st_r, st_i, coef_s, bbr_s, bbi_s, *io_s, tb, nb, lane_chunk, seq_major_io):
    step = pl.program_id(0)
    if seq_major_io:
        u_s, y_s = io_s
        for b in range(nb):
            for j in range(S5_WIDTH // LANES):
                u_s[j, pl.ds(b, tb, stride=nb), :] = u_ref[b, :, j * LANES:(j + 1) * LANES]
        u = jnp.concatenate([u_s[j] for j in range(S5_WIDTH // LANES)], axis=1)
    else:
        u = u_ref[...]

    @pl.when(step == 0)
    def _():
        lr = lre_ref[...]
        li = lim_ref[...]
        dt = jnp.exp(ldt_ref[...])
        mag = jnp.exp(lr * dt)
        ar = mag * jnp.cos(li * dt)
        ai = mag * jnp.sin(li * dt)
        inv = 1.0 / (lr * lr + li * li)
        coef_s[0:1, :] = ar
        coef_s[1:2, :] = ai
        cr = ((ar - 1.0) * lr + ai * li) * inv
        ci = (ai * lr - (ar - 1.0) * li) * inv
        bbr_s[...] = (cr * bre_ref[...] - ci * bim_ref[...]).astype(BF16)
        bbi_s[...] = (cr * bim_ref[...] + ci * bre_ref[...]).astype(BF16)
        st_r[...] = h0r_ref[...]
        st_i[...] = h0i_ref[...]

    ub = u.astype(BF16)
    xr_s[...] = _dot(ub, bbr_s[...])
    xi_s[...] = _dot(ub, bbi_s[...])

    for lc in range(S5_STATE // lane_chunk):
        sl = slice(lc * lane_chunk, (lc + 1) * lane_chunk)
        ar = jnp.broadcast_to(coef_s[0:1, sl], (nb, lane_chunk))
        ai = jnp.broadcast_to(coef_s[1:2, sl], (nb, lane_chunk))

        def body(t, carry, sl=sl, ar=ar, ai=ai):
            hr, hi = carry
            r0 = pl.multiple_of(t * nb, 8)
            nhr = ar * hr - ai * hi + xr_s[pl.ds(r0, nb), sl]
            nhi = ar * hi + ai * hr + xi_s[pl.ds(r0, nb), sl]
            xr_s[pl.ds(r0, nb), sl] = nhr
            xi_s[pl.ds(r0, nb), sl] = nhi
            return nhr, nhi

        hr, hi = lax.fori_loop(0, tb, body, (st_r[:, sl], st_i[:, sl]), unroll=min(tb, 8))
        st_r[:, sl] = hr
        st_i[:, sl] = hi

    y = (_dot(xr_s[...].astype(BF16), cre_ref[...]) - _dot(xi_s[...].astype(BF16), cim_ref[...])
         + d_ref[...] * u)
    zg = 0.5 * y * (1.0 + jnp.tanh(math.sqrt(2.0 / math.pi) * (y + 0.044715 * (y * y * y))))
    yv = zg * _sigmoid(_dot(zg.astype(BF16), wg_ref[...]) + bg_ref[...])
    if seq_major_io:
        for j in range(S5_WIDTH // LANES):
            y_s[j] = yv[:, j * LANES:(j + 1) * LANES]
        for b in range(nb):
            for j in range(S5_WIDTH // LANES):
                c0 = b * S5_WIDTH + j * LANES
                y_ref[:, c0:c0 + LANES] = y_s[j, pl.ds(b, tb, stride=nb), :].astype(BF16)
    else:
        y_ref[...] = yv.astype(BF16)

    @pl.when(step == pl.num_programs(0) - 1)
    def _():
        hr_out[...] = st_r[...]
        hi_out[...] = st_i[...]


def _s5(u, h0r, h0i, prm, tb, lane_chunk, seq_major_io):
    nb = h0r.shape[0]
    fix2 = lambda i: (0, 0)
    vec = pl.BlockSpec((1, S5_STATE), fix2)
    st = pl.BlockSpec((nb, S5_STATE), fix2)
    rows = tb * nb
    scratch = [pltpu.VMEM((rows, S5_STATE), F32), pltpu.VMEM((rows, S5_STATE), F32),
               pltpu.VMEM((nb, S5_STATE), F32), pltpu.VMEM((nb, S5_STATE), F32),
               pltpu.VMEM((8, S5_STATE), F32),
               pltpu.VMEM((S5_WIDTH, S5_STATE), BF16), pltpu.VMEM((S5_WIDTH, S5_STATE), BF16)]
    if seq_major_io:
        l = u.shape[1]
        u_spec = pl.BlockSpec((nb, tb, S5_WIDTH), lambda i: (0, i, S5_COL_BLOCK))
        y_shape = jax.ShapeDtypeStruct((l, nb * S5_WIDTH), BF16)
        y_spec = pl.BlockSpec((tb, nb * S5_WIDTH), lambda i: (i, 0))
        io_shape = (S5_WIDTH // LANES, rows, LANES)
        scratch += [pltpu.VMEM(io_shape, F32), pltpu.VMEM(io_shape, F32)]
    else:
        l = u.shape[0] // nb
        u_spec = pl.BlockSpec((rows, S5_WIDTH), lambda i: (i, 0))
        y_shape = jax.ShapeDtypeStruct((l * nb, S5_WIDTH), BF16)
        y_spec = pl.BlockSpec((rows, S5_WIDTH), lambda i: (i, 0))
    return pl.pallas_call(
        functools.partial(_s5_kernel, tb=tb, nb=nb, lane_chunk=lane_chunk, seq_major_io=seq_major_io),
        out_shape=(y_shape, jax.ShapeDtypeStruct((nb, S5_STATE), F32), jax.ShapeDtypeStruct((nb, S5_STATE), F32)),
        grid=(l // tb,),
        in_specs=[u_spec, st, st, vec, vec, vec,
                  pl.BlockSpec((S5_WIDTH, S5_STATE), fix2), pl.BlockSpec((S5_WIDTH, S5_STATE), fix2),
                  pl.BlockSpec((S5_STATE, S5_WIDTH), fix2), pl.BlockSpec((S5_STATE, S5_WIDTH), fix2),
                  pl.BlockSpec((1, S5_WIDTH), fix2), pl.BlockSpec((S5_WIDTH, S5_WIDTH), fix2),
                  pl.BlockSpec((1, S5_WIDTH), fix2)],
        out_specs=(y_spec, st, st),
        scratch_shapes=scratch,
        compiler_params=pltpu.CompilerParams(dimension_semantics=("arbitrary",),
                                             vmem_limit_bytes=VMEM_LIMIT),
        name="s5",
    )(u, h0r, h0i, prm["lre"], prm["lim"], prm["ldt"], prm["bre"], prm["bim"], prm["cre"],
      prm["cim"], prm["d"], prm["wg"], prm["bg"])


def _l2norm_pairs(x_chunks, bones, scale):
    n = len(x_chunks)
    x = jnp.concatenate(x_chunks, axis=0) if n > 1 else x_chunks[0]
    out = [[None] * N_PAIRS for _ in range(n)]
    for p_ in range(N_PAIRS):
        xp = x[:, p_ * LANES:(p_ + 1) * LANES]
        xn = xp * (lax.rsqrt(_seg_mean(xp * xp, bones) * HEAD_DIM + EPS) * scale)
        for c in range(n):
            out[c][p_] = xn[c * CHUNK:(c + 1) * CHUNK]
    return out


def _head_norm_store(vals, gates, normw, bones, store):
    n = len(vals)
    for p_ in range(N_PAIRS):
        v = jnp.concatenate([vals[c][p_] for c in range(n)], axis=0) if n > 1 else vals[0][p_]
        g = jnp.concatenate([gates[c][p_] for c in range(n)], axis=0) if n > 1 else gates[0][p_]
        y = v * lax.rsqrt(_seg_mean(v * v, bones) + EPS) * normw * g
        for c in range(n):
            store(c, p_, y[c * CHUNK:(c + 1) * CHUNK].astype(BF16))


def _dn_local(chunks, prm, prec):
    n = len(chunks)
    bones = prm["bones"]
    incl, strict = _pair_masks()
    ltri, upair = _ltri(), _upair()
    units = [(c, p_) for c in range(n) for p_ in range(N_PAIRS)]

    g_cols = _mm_sel_l(ltri, jnp.concatenate([-jnp.exp(prm["alog_row"]) * _softplus(ch["zs"]) for ch in chunks],
                                             axis=1))
    g_rows = _mm_sel_r(jnp.concatenate([-jnp.exp(prm["rowalog"]) * _softplus(ch["rows"] + prm["rowadd"])
                                        for ch in chunks], axis=0), upair)
    gates = [(_sigmoid(ch["zs"]), g_cols[:, c * LANES:(c + 1) * LANES], g_rows[c * 16:(c + 1) * 16])
             for c, ch in enumerate(chunks)]

    qn = _l2norm_pairs([ch["q"] for ch in chunks], bones, HEAD_DIM ** -0.5)
    kn = _l2norm_pairs([ch["k"] for ch in chunks], bones, 1.0)

    beta = [_expand(gates[c][0], 2 * p_, 2 * p_ + 1) for c, p_ in units]
    g_col = [_expand(gates[c][1], 6 + 2 * p_, 7 + 2 * p_) for c, p_ in units]
    dec = [jnp.exp(jnp.where(incl, gc - gates[c][2][p_:p_ + 1, :], NEG)) for gc, (c, p_) in zip(g_col, units)]
    kb = [kn[c][p_] * b for b, (c, p_) in zip(beta, units)]
    sc = [_mm(jnp.concatenate([kb_, qn[c][p_]], axis=0), _x2b(kn[c][p_]), _NT)
          for kb_, (c, p_) in zip(kb, units)]
    a_in = [s[CHUNK:] * d for s, d in zip(sc, dec)]
    mbd = [_x2(jnp.where(strict, s[:CHUNK] * d, 0.0)) for s, d in zip(sc, dec)]
    tri_prec, newton, sol_prec = prec
    t_bd = _tri_inverse_units(mbd, tri_prec, newton)
    eg = [jnp.exp(gc) for gc in g_col]
    x2r = _x2b if sol_prec == 1 else _x2
    rhs = [jnp.concatenate([x2r(chunks[c]["v"][:, p_ * LANES:(p_ + 1) * LANES] * b), x2r(kb_ * e)], axis=1)
           for b, kb_, e, (c, p_) in zip(beta, kb, eg, units)]
    sol = [_mm(t, r, prec=sol_prec) for t, r in zip(t_bd, rhs)]
    out = [[None] * N_PAIRS for _ in range(n)]
    for i, (c, p_) in enumerate(units):
        g_last = _last_row(g_col[i])
        out[c][p_] = dict(u=_fold(sol[i][:, :LANES]), w=_fold(sol[i][:, LANES:]), a_in=a_in[i],
                          qg=qn[c][p_] * eg[i], kd=kn[c][p_] * jnp.exp(g_last - g_col[i]),
                          eg_last=jnp.exp(g_last))
    return out


def _dn_seq_step(zq_ref, zk_ref, zv_ref, zg_ref, zs_ref, rows_ref,
                 add_ref, alog_ref, rowadd_ref, rowalog_ref, nw_ref, o_ref, s_scr, *, bb, prec):
    chunks = [dict(q=zq_ref[b], k=zk_ref[b], v=zv_ref[b], zs=zs_ref[b] + add_ref[...], rows=rows_ref[b, 0])
              for b in range(bb)]

    prm = dict(bones=_block_ones(), alog_row=alog_ref[...], rowadd=rowadd_ref[...],
               rowalog=rowalog_ref[...])
    loc = _dn_local(chunks, prm, prec)
    bdm = _bd_mask()
    units = [(b, p_) for b in range(bb) for p_ in range(N_PAIRS)]
    s_prev = [s_scr[b, p_] for b, p_ in units]
    ws = [_mm(jnp.concatenate([loc[b][p_]["w"], loc[b][p_]["qg"]], axis=0), s)
          for s, (b, p_) in zip(s_prev, units)]
    v_new = [loc[b][p_]["u"] - w_[:CHUNK] for w_, (b, p_) in zip(ws, units)]
    o = [w_[CHUNK:] + _mm(loc[b][p_]["a_in"], _x2b(v)) for w_, v, (b, p_) in zip(ws, v_new, units)]
    upd = [_mm(loc[b][p_]["kd"].T, v) for v, (b, p_) in zip(v_new, units)]
    for i, (b, p_) in enumerate(units):
        s_scr[b, p_] = loc[b][p_]["eg_last"][0:1, :] * s_prev[i] + jnp.where(bdm, upd[i], 0.0)

    def store(b, p_, val):
        o_ref[b, :, p_ * LANES:(p_ + 1) * LANES] = val

    _head_norm_store([[o[b * N_PAIRS + p_] for p_ in range(N_PAIRS)] for b in range(bb)],
                     [[zg_ref[b, :, p_ * LANES:(p_ + 1) * LANES] for p_ in range(N_PAIRS)] for b in range(bb)],
                     nw_ref[...], prm["bones"], store)


def _ml_units(chunks, prm):
    n = len(chunks)
    incl, _ = _pair_masks()
    lo = _lane_lo((CHUNK, LANES))
    ltri, upair = _ltri(), _upair()
    ones2 = prm["bones"]
    bdm2 = jnp.concatenate([_bd_mask(), _bd_mask()], axis=1)
    units = [(c, p_) for c in range(n) for p_ in range(N_PAIRS)]
    pair = lambda x, p_: x[:, p_ * LANES:(p_ + 1) * LANES]

    rows = [ch["rows"] + prm["rowadd"] for ch in chunks]
    b_cols = _mm_sel_l(ltri, jnp.concatenate([-_softplus(-ch["zs"]) for ch in chunks], axis=1))
    b_rows = _mm_sel_r(jnp.concatenate([-_softplus(-r) for r in rows], axis=0), upair)
    comp = [dict(ig_c=ch["zs"], b_c=b_cols[:, c * LANES:(c + 1) * LANES], ig_r=rows[c],
                 b_r=b_rows[c * 16:(c + 1) * 16]) for c, ch in enumerate(chunks)]

    qp = [pair(chunks[c]["q"], p_) for c, p_ in units]
    kp = [pair(chunks[c]["k"], p_) * (HEAD_DIM ** -0.5) for c, p_ in units]
    vp = [pair(chunks[c]["v"], p_) for c, p_ in units]
    b_col = [_expand(comp[c]["b_c"], 18 + 2 * p_, 19 + 2 * p_) for c, p_ in units]
    ig_col = [_expand(comp[c]["ig_c"], 12 + 2 * p_, 13 + 2 * p_) for c, p_ in units]
    dl = [jnp.where(incl, bc - comp[c]["b_r"][6 + p_:7 + p_, :] + comp[c]["ig_r"][3 + p_:4 + p_, :], NEG)
          for bc, (c, p_) in zip(b_col, units)]
    mx = [jnp.where(lo, jnp.max(jnp.where(lo, d, NEG), axis=1, keepdims=True),
                    jnp.max(jnp.where(lo, NEG, d), axis=1, keepdims=True)) for d in dl]
    inter = [bc + chunks[c]["ms"][p_] for bc, (c, p_) in zip(b_col, units)]
    mt = [jnp.maximum(a, b) for a, b in zip(inter, mx)]
    s = [_mm(q_, _x2b(k_), _NT) * jnp.exp(d - m) for q_, k_, d, m in zip(qp, kp, dl, mt)]
    sv = [_mm(s_, jnp.concatenate([_x2b(v_), ones2], axis=1)) for s_, v_ in zip(s, vp)]
    cn_prev = [chunks[c]["cn"][p_] for c, p_ in units]
    qcn = [_mm(q_, cn) for q_, cn in zip(qp, cn_prev)]
    w_inter = [jnp.exp(a - m) for a, m in zip(inter, mt)]
    h = [(wi * qc[:, :LANES] + sv_[:, :LANES])
         / jnp.maximum(jnp.abs(wi * qc[:, LANES:] + sv_[:, LANES:]), jnp.exp(-m))
         for wi, qc, sv_, m in zip(w_inter, qcn, sv, mt)]
    m_end = [_last_row(m) for m in mt]
    b_last = [_last_row(bc) for bc in b_col]
    a_end = [jnp.exp(bl + chunks[c]["ms"][p_] - me) for bl, me, (c, p_) in zip(b_last, m_end, units)]
    kw_t = [(k_ * jnp.exp(bl - bc + ig - me)).T for k_, bl, bc, ig, me in zip(kp, b_last, b_col, ig_col, m_end)]
    upd = [_mm(k_, jnp.concatenate([v_, jnp.ones((CHUNK, LANES), F32)], axis=1)) for k_, v_ in zip(kw_t, vp)]
    out = [[None] * N_PAIRS for _ in range(n)]
    for i, (c, p_) in enumerate(units):
        a_row = a_end[i][0:1, :]
        new_state = jnp.concatenate([a_row, a_row], axis=1) * cn_prev[i] + jnp.where(bdm2, upd[i], 0.0)
        out[c][p_] = (h[i], new_state, m_end[i][0:1, :])
    return out


def _ml_seq_step(zq_ref, zk_ref, zv_ref, zo_ref, zs_ref, rows_ref, add_ref, rowadd_ref, nw_ref, o_ref,
                 cn_scr, m_scr, *, bb):
    prm = dict(bones=_block_ones(), rowadd=rowadd_ref[...])
    chunks = []
    for b in range(bb):
        chunks.append(dict(q=zq_ref[b], k=zk_ref[b], v=zv_ref[b], zs=zs_ref[b] + add_ref[...],
                           rows=rows_ref[b, 0],
                           ms=[m_scr[b, p_:p_ + 1, :] for p_ in range(N_PAIRS)],
                           cn=[cn_scr[b, p_] for p_ in range(N_PAIRS)]))
    res = _ml_units(chunks, prm)
    for b in range(bb):
        for p_ in range(N_PAIRS):
            cn_scr[b, p_] = res[b][p_][1]
            m_scr[b, p_:p_ + 1, :] = res[b][p_][2]

    def store(b, p_, val):
        o_ref[b, :, p_ * LANES:(p_ + 1) * LANES] = val

    _head_norm_store([[res[b][p_][0] for p_ in range(N_PAIRS)] for b in range(bb)],
                     [[zo_ref[b, :, p_ * LANES:(p_ + 1) * LANES] for p_ in range(N_PAIRS)] for b in range(bb)],
                     nw_ref[...], prm["bones"], store)


def _mixers_seq_kernel(dq_ref, dk_ref, dv_ref, dg_ref, lq_ref, lk_ref, lv_ref, lo_ref, zs_ref, rows_ref,
                       s0_ref, cn0_ref, m0_ref,
                       add_ref, alog_ref, rowadd_ref, rowalog_ref, dnw_ref, mlw_ref,
                       dn_ref, ml_ref, s_out_ref, cn_out_ref, m_out_ref,
                       s_scr, cn_scr, m_scr, *, bb, prec):
    tstep = pl.program_id(1)

    @pl.when(tstep == 0)
    def _():
        s_scr[...] = s0_ref[...]
        cn_scr[...] = cn0_ref[...]
        m_scr[...] = m0_ref[...]

    _dn_seq_step(dq_ref, dk_ref, dv_ref, dg_ref, zs_ref, rows_ref, add_ref, alog_ref, rowadd_ref,
                 rowalog_ref, dnw_ref, dn_ref, s_scr, bb=bb, prec=prec)
    _ml_seq_step(lq_ref, lk_ref, lv_ref, lo_ref, zs_ref, rows_ref, add_ref, rowadd_ref, mlw_ref, ml_ref,
                 cn_scr, m_scr, bb=bb)

    @pl.when(tstep == pl.num_programs(1) - 1)
    def _():
        s_out_ref[...] = s_scr[...]
        cn_out_ref[...] = cn_scr[...]
        m_out_ref[...] = m_scr[...]


def _mixers_seq(z3, zc, rows4, s0, cn0, m0, prm, bb, prec):
    batch, seq_len, _ = z3.shape
    fix2 = lambda g, i: (0, 0)
    zspec = lambda width, col: pl.BlockSpec((bb, CHUNK, width), lambda g, i: (g, i, col))
    per_seq = lambda *shape: pl.BlockSpec((bb,) + shape, lambda g, i: (g,) + (0,) * len(shape))
    s_spec, cn_spec, m_spec = per_seq(N_PAIRS, LANES, LANES), per_seq(N_PAIRS, LANES, 2 * LANES), per_seq(8, LANES)
    small = lambda rows: pl.BlockSpec((rows, LANES), fix2)
    in_specs = ([zspec(HEADS_W, c) for c in (0, 1, 2, 3, 0, 1, 2, 4)] + [zspec(LANES, GATE_COL_BLOCK)]
                + [pl.BlockSpec((bb, 1, 16, LANES), lambda g, i: (g, i, 0, 0)), s_spec, cn_spec, m_spec,
                   small(1), small(1), small(16), small(16), small(1), small(1)])
    out_blk = pl.BlockSpec((bb, CHUNK, HEADS_W), lambda g, i: (g, i, 0))
    act = jax.ShapeDtypeStruct((batch, seq_len, HEADS_W), BF16)
    return pl.pallas_call(
        functools.partial(_mixers_seq_kernel, bb=bb, prec=prec),
        out_shape=(act, act, jax.ShapeDtypeStruct(s0.shape, F32), jax.ShapeDtypeStruct(cn0.shape, F32),
                   jax.ShapeDtypeStruct(m0.shape, F32)),
        grid=(batch // bb, seq_len // CHUNK),
        in_specs=in_specs,
        out_specs=(out_blk, out_blk, s_spec, cn_spec, m_spec),
        scratch_shapes=[pltpu.VMEM((bb, N_PAIRS, LANES, LANES), F32),
                        pltpu.VMEM((bb, N_PAIRS, LANES, 2 * LANES), F32), pltpu.VMEM((bb, 8, LANES), F32)],
        compiler_params=pltpu.CompilerParams(dimension_semantics=("parallel", "arbitrary"),
                                             vmem_limit_bytes=VMEM_LIMIT),
        name="mixers_seq",
    )(zc, zc, zc, zc, z3, z3, z3, zc, z3, rows4, s0, cn0, m0, prm["add_row"], prm["alog_row"],
      prm["rowadd"], prm["rowalog"], prm["dn_normw"], prm["ml_normw"])


def _lane_mixers_kernel(zq, zk, zv, zg, lq, lk, lv, lo, zs, cpq, cpk, cpv, cwq, cwk, cwv, bias, alog, dnw, mlw,
                        s0, c0, n0, m0, *rest, n_tok, n_prev):
    if n_prev:
        s_prev, c_prev, dn_o, ml_o, s_st, c_st, n_o, m_o, k_s, q_s = rest
        s_st[0:n_prev] = s_prev[...]
        c_st[0:n_prev] = c_prev[...]
    else:
        dn_o, ml_o, s_st, c_st, n_o, m_o, k_s, q_s = rest
    s_o, c_o = s_st.at[n_prev], c_st.at[n_prev]
    h = pl.program_id(0)
    nb = s_o.shape[-1]
    zrow = lambda t, r: zs[t, pl.ds(r, 1), :]
    prow = lambda ref, r: ref[pl.ds(r, 1), :]
    bcast = lambda ref, dk: jnp.broadcast_to(ref[pl.ds(dk, 1), :], (HEAD_DIM, nb))
    zeros = jnp.zeros((HEAD_DIM, nb), F32)

    def conv(z_ref, cp_ref, cw_ref):
        e = [cp_ref[j] for j in range(CONV_W - 1)] + [z_ref[t] for t in range(n_tok)]
        outs = []
        for t in range(n_tok):
            acc = e[t] * cw_ref[0]
            for j in range(1, CONV_W):
                acc = acc + e[t + j] * cw_ref[j]
            outs.append(_silu(acc))
        return outs

    def l2n(x, scale):
        return x * (lax.rsqrt(jnp.sum(x * x, axis=0, keepdims=True) + EPS) * scale)

    def head_norm(x, w):
        return x * lax.rsqrt(jnp.mean(x * x, axis=0, keepdims=True) + EPS) * w

    qc, kc, vc = conv(zq, cpq, cwq), conv(zk, cpk, cwk), conv(zv, cpv, cwv)
    s_o[...] = s0[...]
    for t in range(n_tok):
        k_s[...] = l2n(kc[t], 1.0)
        q_s[...] = l2n(qc[t], HEAD_DIM ** -0.5)
        beta = _sigmoid(zrow(t, h))
        a = jnp.exp(-jnp.exp(prow(alog, N_HEADS + h))
                    * _softplus(zrow(t, N_HEADS + h) + prow(bias, N_HEADS + h)))
        ks = lax.fori_loop(0, HEAD_DIM, lambda dk, acc: acc + bcast(k_s, dk) * s_o[dk], zeros, unroll=8)
        delta = beta * (vc[t] - a * ks)

        def dn_update(dk, acc, a=a, delta=delta):
            s_new = a * s_o[dk] + bcast(k_s, dk) * delta
            s_o[dk] = s_new
            return acc + bcast(q_s, dk) * s_new

        o = lax.fori_loop(0, HEAD_DIM, dn_update, zeros, unroll=8)
        dn_o[t] = (head_norm(o, dnw[...]) * _silu(zg[t])).astype(BF16)

    c_o[...] = c0[...]
    n = n0[...]
    m = m0[...]
    for t in range(n_tok):
        q, k, v = lq[t], lk[t] * (HEAD_DIM ** -0.5), lv[t]
        ig = zrow(t, 2 * N_HEADS + h) + prow(bias, 2 * N_HEADS + h)
        lsf = -_softplus(-(zrow(t, 3 * N_HEADS + h) + prow(bias, 3 * N_HEADS + h)))
        m_new = jnp.maximum(lsf + m, ig)
        fp = jnp.exp(lsf + m - m_new)
        ip = jnp.exp(ig - m_new)
        k_s[...] = k * ip
        q_s[...] = q

        def ml_update(dk, acc, fp=fp, v=v):
            c_new = fp * c_o[dk] + bcast(k_s, dk) * v
            c_o[dk] = c_new
            return acc + bcast(q_s, dk) * c_new

        num = lax.fori_loop(0, HEAD_DIM, ml_update, zeros, unroll=8)
        n = fp * n + ip * k
        den = jnp.sum(q * n, axis=0, keepdims=True)
        hh = num / jnp.maximum(jnp.abs(den), jnp.exp(-m_new))
        m = m_new
        ml_o[t] = (head_norm(hh, mlw[...]) * _sigmoid(lo[t])).astype(BF16)
    n_o[...] = n
    m_o[...] = m


def _lane_mixers(z_t, conv_t, s0, c0, n0, m0, layer, prm, prev):
    n_tok, _, nb = z_t.shape
    hd = HEAD_DIM
    n_prev = 0 if prev is None else prev[0].shape[0]
    assert n_prev == layer
    zblk = lambda first: pl.BlockSpec((n_tok, hd, nb), lambda h: (0, first + h, 0))
    cblk = lambda taps, seg: pl.BlockSpec((taps, hd, nb), lambda h: (0, seg * N_HEADS + h, 0))
    wblk = lambda seg: pl.BlockSpec((CONV_W, hd, nb), lambda h: (0, seg * N_HEADS + h, 0))
    whole = lambda shape: pl.BlockSpec(shape, lambda h: (0,) * len(shape))
    mat_in = pl.BlockSpec((None, None, hd, hd, nb), lambda h: (layer, h, 0, 0, 0))
    mat_out = pl.BlockSpec((n_prev + 1, None, hd, hd, nb), lambda h: (0, h, 0, 0, 0))
    prev_specs = [pl.BlockSpec((n_prev, None, hd, hd, nb), lambda h: (0, h, 0, 0, 0))] * 2 if n_prev else []
    small_rows = 32
    in_specs = [zblk(seg * N_HEADS) for seg in range(8)] + [
        pl.BlockSpec((n_tok, small_rows, nb), lambda h: (0, (Z_COLS - LANES) // small_rows, 0)),
        cblk(CONV_W - 1, 0), cblk(CONV_W - 1, 1), cblk(CONV_W - 1, 2), wblk(0), wblk(1), wblk(2),
        whole((small_rows, nb)), whole((small_rows, nb)), whole((hd, nb)), whole((hd, nb)),
        mat_in, mat_in,
        pl.BlockSpec((None, None, hd, nb), lambda h: (layer, h, 0, 0)),
        pl.BlockSpec((None, None, 1, nb), lambda h: (layer, h, 0, 0))] + prev_specs
    out_blk = pl.BlockSpec((n_tok, hd, nb), lambda h: (0, h, 0))
    act = jax.ShapeDtypeStruct((n_tok, HEADS_W, nb), BF16)
    mat = jax.ShapeDtypeStruct((n_prev + 1, N_HEADS, hd, hd, nb), F32)
    return pl.pallas_call(
        functools.partial(_lane_mixers_kernel, n_tok=n_tok, n_prev=n_prev),
        out_shape=(act, act, mat, mat, jax.ShapeDtypeStruct((N_HEADS, hd, nb), F32),
                   jax.ShapeDtypeStruct((N_HEADS, 1, nb), F32)),
        grid=(N_HEADS,),
        in_specs=in_specs,
        out_specs=(out_blk, out_blk, mat_out, mat_out,
                   pl.BlockSpec((None, hd, nb), lambda h: (h, 0, 0)), pl.BlockSpec((None, 1, nb), lambda h: (h, 0, 0))),
        scratch_shapes=[pltpu.VMEM((hd, nb), F32), pltpu.VMEM((hd, nb), F32)],
        compiler_params=pltpu.CompilerParams(dimension_semantics=("parallel",), vmem_limit_bytes=VMEM_LIMIT),
        name="lane_mixers",
    )(*([z_t] * 9), conv_t, conv_t, conv_t, prm["conv_wb"], prm["conv_wb"], prm["conv_wb"], prm["bias_b"],
      prm["alog_b"], prm["dn_normb"], prm["ml_normb"], s0, c0, n0, m0, *(prev or ()))


_IN_OFFS = [int(v) for v in np.concatenate([[0], np.cumsum(IN_SIZES)])]
PACK_ROWS = 256


def _pack_w_kernel(wt_ref, o_ref):
    o = _IN_OFFS
    seg_t = lambda a, b: wt_ref[0, a:b, :].T
    b0 = (o[9] // LANES) * LANES
    assert o[3] % LANES == 0 and o[9] - b0 == 2 * N_HEADS
    t_a = seg_t(o[3], o[3] + LANES)
    t_b = seg_t(b0, b0 + LANES)
    lane = _iota(t_a.shape, 1)
    small = jnp.where(lane < 2 * N_HEADS, t_a, jnp.where(lane < 4 * N_HEADS, t_b, 0.0))
    cols = [seg_t(o[0], o[3]), seg_t(o[5], o[9]), seg_t(o[11], o[13]), small]
    o_ref[0] = jnp.concatenate(cols, axis=-1).astype(BF16)


def _pack_w_in(w_in):
    depth, k, n = w_in.shape
    w_t = jnp.transpose(w_in, (0, 2, 1))
    return pl.pallas_call(
        _pack_w_kernel, out_shape=jax.ShapeDtypeStruct((depth, k, Z_COLS), BF16),
        grid=(depth, k // PACK_ROWS),
        in_specs=[pl.BlockSpec((1, n, PACK_ROWS), lambda l, i: (l, 0, i))],
        out_specs=pl.BlockSpec((1, PACK_ROWS, Z_COLS), lambda l, i: (l, i, 0)),
        compiler_params=pltpu.CompilerParams(dimension_semantics=("parallel", "parallel"),
                                             vmem_limit_bytes=VMEM_LIMIT),
        name="pack_w_in",
    )(w_t)


def _pair_rows(v6):
    return jnp.repeat(v6.reshape(N_PAIRS, 2), HEAD_DIM, axis=1).reshape(N_PAIRS, LANES)


def _layer_params(l, a):
    f = lambda name: a[name][l].astype(F32)
    zeros6 = jnp.zeros((N_HEADS,), F32)
    add_row = jnp.concatenate([zeros6, f("dn_dt_bias"), f("ml_i_bias"), f("ml_f_bias")])
    add_row = jnp.pad(add_row, (0, LANES - 4 * N_HEADS))[None, :]
    alog_row = jnp.pad(jnp.concatenate([zeros6, f("dn_a_log")]), (0, LANES - 2 * N_HEADS))[None, :]
    rowadd = jnp.concatenate([_pair_rows(f("dn_dt_bias")), _pair_rows(f("ml_i_bias")),
                              _pair_rows(f("ml_f_bias")), jnp.zeros((7, LANES), F32)], axis=0)
    rowalog = jnp.concatenate([_pair_rows(f("dn_a_log")), jnp.zeros((13, LANES), F32)], axis=0)
    eye_g = jnp.eye(S5_GROUPS, dtype=F32)
    bd_in = lambda b: jnp.einsum("gpc,gh->gchp", b, eye_g).reshape(S5_WIDTH, S5_STATE)
    bd_out = lambda c: jnp.einsum("gcp,gh->gphc", c, eye_g).reshape(S5_STATE, S5_WIDTH).astype(BF16)
    return dict(
        layer=l, norm_mix=f("norm_mix")[None, :],
        add_row=add_row, alog_row=alog_row, rowadd=rowadd, rowalog=rowalog,
        conv_w8=jnp.pad(f("dn_conv_w"), ((0, 8 - CONV_W), (0, 0))),
        dn_normw=jnp.tile(f("dn_norm"), 2)[None, :], ml_normw=jnp.tile(f("ml_norm"), 2)[None, :],
        s5=dict(lre=f("s5_lam_re").reshape(1, S5_STATE), lim=f("s5_lam_im").reshape(1, S5_STATE),
                ldt=jnp.repeat(f("s5_log_dt"), S5_P)[None, :],
                bre=bd_in(f("s5_b_re")), bim=bd_in(f("s5_b_im")),
                cre=bd_out(f("s5_c_re")), cim=bd_out(f("s5_c_im")),
                d=f("s5_d").reshape(1, S5_WIDTH), wg=f("s5_w_glu").astype(BF16),
                bg=f("s5_b_glu")[None, :]),
        norm_x=f("norm_x")[None, :], norm_ff=f("norm_ff")[None, :],
    )


def _shared_weights(a):
    cast = lambda name: a[name].astype(BF16)
    return dict(w_in=_pack_w_in(a["w_in"]), w_out=cast("w_out"), w_xq=cast("w_xq"), w_xo=cast("w_xo"),
                w_ff1=cast("w_ff1"), w_ff2=cast("w_ff2"), w_xk=cast("w_xk"), w_xv=cast("w_xv"))


def _row_form(z, n_chunks):
    c0 = z.shape[-1] - LANES
    zs = z[..., c0:c0 + 4 * N_HEADS].reshape(n_chunks, CHUNK, 4, N_PAIRS, 2)
    r = jnp.transpose(zs[:, :, 1:], (0, 2, 3, 4, 1)).reshape(n_chunks, 9, LANES)
    return jnp.pad(r, ((0, 0), (0, 7), (0, 0)))


def _un_bd_pairs(s):
    n = s.shape[0]
    a = s[:, :, :HEAD_DIM, :HEAD_DIM]
    b = s[:, :, HEAD_DIM:, HEAD_DIM:]
    return jnp.stack([a, b], axis=2).reshape(n, N_HEADS, HEAD_DIM, HEAD_DIM)


def _unpack_m(mp):
    return mp[:, :N_PAIRS, ::HEAD_DIM].reshape(mp.shape[0], N_HEADS)


def _tile(n, pref):
    tile = min(n, pref)
    assert n % tile == 0, (n, pref)
    return tile


def _layer_prompt(x, mem_k, mem_v, prm, wts, final_g):
    batch, seq_len, _ = x.shape
    layer = prm["layer"]
    z3, zc, last8 = _norm_matmul_conv(x, prm["norm_mix"], wts["w_in"], jnp.zeros((batch, 8, CONV_CH), F32),
                                      prm["conv_w8"], layer, _tile(seq_len, CONV_ROW_TILE))
    rows4 = _row_form(z3, batch * seq_len // CHUNK).reshape(batch, seq_len // CHUNK, 16, LANES)
    new_conv = last8[:, 8 - (CONV_W - 1):]

    dn, ml, s_new, cn_new, m_new = _mixers_seq(
        z3, zc, rows4, jnp.zeros((batch, N_PAIRS, LANES, LANES), F32),
        jnp.zeros((batch, N_PAIRS, LANES, 2 * LANES), F32), jnp.zeros((batch, 8, LANES), F32), prm,
        _tile(batch, SEQ_BATCH), SEQ_SOLVE)
    zero_h = jnp.zeros((batch, S5_STATE), F32)
    s5o, hr, hi = _s5(z3, zero_h, zero_h, prm["s5"], _tile(seq_len, S5_STEPS), S5_STATE, True)

    g_fin = final_g if final_g is not None else prm["norm_ff"]
    x3 = _post_mix(dn, ml, s5o, x, mem_k, mem_v, wts, prm, g_fin, layer, _tile(seq_len, POST_TILE),
                   final_g is not None)
    st = (new_conv, _un_bd_pairs(s_new), _un_bd_pairs(cn_new[..., :LANES]),
          _un_bd_pairs(cn_new[..., LANES:])[..., 0], _unpack_m(m_new),
          hr.reshape(batch, S5_GROUPS, S5_P), hi.reshape(batch, S5_GROUPS, S5_P))
    return x3, st


def _layer_sample(x, cache_k, cache_v, conv_buf, dn_s, ml_c, ml_n, ml_m, ssm_re, ssm_im, prm, wts, batch,
                  seq_len, final_g, prev_stacks):
    t = batch * seq_len
    layer = prm["layer"]
    z = _norm_matmul(x, prm["norm_mix"], wts["w_in"], layer, t)
    z3 = z.reshape(seq_len, batch, Z_COLS)
    new_conv = jnp.concatenate([conv_buf, jnp.transpose(z3[:, :, :CONV_CH], (1, 0, 2))], axis=1)[:, seq_len:]
    lanes = lambda v: jnp.broadcast_to(v[..., None], v.shape + (batch,))
    mix_prm = dict(conv_wb=lanes(prm["conv_w8"][:CONV_W]), bias_b=lanes(prm["add_row"][0, :32]),
                   alog_b=lanes(prm["alog_row"][0, :32]), dn_normb=lanes(prm["dn_normw"][0, :HEAD_DIM]),
                   ml_normb=lanes(prm["ml_normw"][0, :HEAD_DIM]))
    dn_t, ml_t, s_stack, c_stack, n_new, m_new = _lane_mixers(
        jnp.transpose(z3, (0, 2, 1)), jnp.transpose(conv_buf, (1, 2, 0)), dn_s, ml_c, ml_n, ml_m, layer, mix_prm,
        prev_stacks)
    to_rows = lambda a: jnp.transpose(a, (0, 2, 1)).reshape(1, t, HEADS_W)
    dn, ml = to_rows(dn_t), to_rows(ml_t)

    s5o, hr, hi = _s5(z[0, :, 3072:3072 + S5_WIDTH], ssm_re.reshape(batch, S5_STATE),
                      ssm_im.reshape(batch, S5_STATE), prm["s5"], seq_len, LANES, False)

    x1, q = _out_proj(dn, ml, s5o, x, wts, prm, layer, t)
    q3 = jnp.pad(jnp.transpose(q.reshape(seq_len, batch, X_WIDTH), (1, 0, 2)),
                 ((0, 0), (0, Q_ROWS - seq_len), (0, 0)))
    att = _attn_batch(q3, cache_k, cache_v, layer, _tile(batch, ATTN_BATCH))[:, :seq_len]
    att = jnp.transpose(att, (1, 0, 2)).reshape(1, t, X_WIDTH)
    g_fin = final_g if final_g is not None else prm["norm_ff"]
    x3 = _ffn(x1, att, wts, prm, g_fin, layer, t, final_g is not None)
    st = (new_conv, jnp.transpose(n_new, (2, 0, 1)), jnp.transpose(m_new[:, 0, :], (1, 0)),
          hr.reshape(batch, S5_GROUPS, S5_P), hi.reshape(batch, S5_GROUPS, S5_P))
    return x3, st, (s_stack, c_stack)


def kernel(x_prompt, x_sample, mem_prompt, state_dn_conv, state_dn_s, state_ml_c, state_ml_n, state_ml_m, state_ssm_re, state_ssm_im, cache_mem_k, cache_mem_v, norm_mix, w_in, dn_conv_w, dn_a_log, dn_dt_bias, dn_norm, ml_i_bias, ml_f_bias, ml_norm, s5_lam_re, s5_lam_im, s5_log_dt, s5_b_re, s5_b_im, s5_c_re, s5_c_im, s5_d, s5_w_glu, s5_b_glu, w_out, norm_x, norm_mem, w_xq, w_xk, w_xv, w_xo, norm_ff, w_ff1, w_ff2, norm_final):
    a = dict(norm_mix=norm_mix, w_in=w_in, dn_conv_w=dn_conv_w, dn_a_log=dn_a_log, dn_dt_bias=dn_dt_bias,
             dn_norm=dn_norm, ml_i_bias=ml_i_bias, ml_f_bias=ml_f_bias, ml_norm=ml_norm,
             s5_lam_re=s5_lam_re, s5_lam_im=s5_lam_im, s5_log_dt=s5_log_dt, s5_b_re=s5_b_re,
             s5_b_im=s5_b_im, s5_c_re=s5_c_re, s5_c_im=s5_c_im, s5_d=s5_d, s5_w_glu=s5_w_glu,
             s5_b_glu=s5_b_glu, w_out=w_out, norm_x=norm_x, w_xq=w_xq, w_xo=w_xo, norm_ff=norm_ff,
             w_ff1=w_ff1, w_ff2=w_ff2)
    prms = [_layer_params(l, a) for l in range(DEPTH)]
    wts = _shared_weights(dict(a, w_xk=w_xk, w_xv=w_xv))
    g_final = norm_final.astype(F32)[None, :]

    bp, lp, _ = x_prompt.shape
    bs, ls, _ = x_sample.shape

    xp = x_prompt
    mk, mk_t, mv, mv_t = _mem_proj(mem_prompt, norm_mem.astype(F32), wts["w_xk"], wts["w_xv"])
    p_st = []
    for l in range(DEPTH):
        xp, st = _layer_prompt(xp, mk, mv, prms[l], wts, g_final if l == DEPTH - 1 else None)
        p_st.append(st)
    y_prompt = xp
    p_mem = (mk_t.reshape(DEPTH, bp, N_MEM, X_HEADS, X_HD), mv_t.reshape(DEPTH, bp, N_MEM, X_HEADS, X_HD))

    xs = jnp.transpose(x_sample, (1, 0, 2)).reshape(1, ls * bs, D_MODEL)
    cache_k = cache_mem_k.reshape(DEPTH, bs, N_MEM * X_HEADS, X_HD)
    cache_v = cache_mem_v.reshape(DEPTH, bs, N_MEM * X_HEADS, X_HD)
    dn_s_t = jnp.transpose(state_dn_s, (0, 2, 3, 4, 1))
    ml_c_t = jnp.transpose(state_ml_c, (0, 2, 3, 4, 1))
    ml_n_t = jnp.transpose(state_ml_n, (0, 2, 3, 1))
    ml_m_t = jnp.transpose(state_ml_m, (0, 2, 1))[:, :, None, :]
    s_st = []
    stacks = None
    for l in range(DEPTH):
        xs, st, stacks = _layer_sample(xs, cache_k, cache_v, state_dn_conv[l], dn_s_t, ml_c_t, ml_n_t, ml_m_t,
                                       state_ssm_re[l], state_ssm_im[l], prms[l], wts, bs, ls,
                                       g_final if l == DEPTH - 1 else None, stacks)
        s_st.append(st)
    y_sample = jnp.transpose(xs.reshape(ls, bs, D_MODEL), (1, 0, 2))
    s_mats = tuple(jnp.transpose(m, (0, 4, 1, 2, 3)) for m in stacks)

    stack = lambda sts, i: jnp.stack([s[i] for s in sts])
    return ((y_prompt, y_sample) + tuple(stack(p_st, i) for i in range(7)) + p_mem
            + (stack(s_st, 0),) + s_mats + tuple(stack(s_st, i) for i in range(1, 5)))
```
